```python
import math
import jax, jax.numpy as jnp
from jax import lax
import numpy as np

D_MODEL = 1024
BATCH = 8
SEQ = 8192
DEPTH = 1

CHUNK = 64
Q_BLOCK = 128
EPS = 1e-6
CONV_CH = D_MODEL // 2
CONV_WIDTH = 31
MLA_HEADS = 8
MLA_NOPE = 64
MLA_ROPE = 32
MLA_V = 64
MLA_Q_RANK = 256
MLA_KV_RANK = 128
ROPE_THETA = 10000.0
MIX_WIDTH = CONV_CH + MLA_HEADS * MLA_V
IN_COLS = 2 * CONV_CH + MLA_Q_RANK + MLA_KV_RANK + MLA_ROPE
MEM_LEN = 256
MEM_HEADS = 4
MEM_HEAD_DIM = D_MODEL // MEM_HEADS
D_FF = 2816
FFN_CONV_WIDTH = 3
MAX_START = 4096

kernel_name = "hybrid_conformer_mla_stream_layer"


def rms_norm(x, g):
    xf = x.astype(jnp.float32)
    y = xf * lax.rsqrt(jnp.mean(xf * xf, axis=-1, keepdims=True) + EPS)
    return (y * g.astype(jnp.float32)).astype(x.dtype)


def layer_norm(x, g, b):
    xf = x.astype(jnp.float32)
    mu = jnp.mean(xf, axis=-1, keepdims=True)
    xc = xf - mu
    y = xc * lax.rsqrt(jnp.mean(xc * xc, axis=-1, keepdims=True) + EPS)
    return (y * g.astype(jnp.float32) + b.astype(jnp.float32)).astype(x.dtype)


def causal_depthwise_conv(x, w, b):
    k_width, ch = w.shape
    y = lax.conv_general_dilated(
        x, w[:, None, :].astype(x.dtype), window_strides=(1,), padding=[(k_width - 1, 0)],
        dimension_numbers=('NWC', 'WIO', 'NWC'), feature_group_count=ch)
    return y + b.astype(x.dtype)


def rope_tables(positions, dim):
    inv_freq = ROPE_THETA ** (-jnp.arange(0, dim, 2, dtype=jnp.float32) / dim)
    ang = positions.astype(jnp.float32)[..., None] * inv_freq
    return jnp.cos(ang), jnp.sin(ang)


def apply_rope(x, cos, sin):
    x1, x2 = jnp.split(x.astype(jnp.float32), 2, axis=-1)
    return jnp.concatenate([x1 * cos - x2 * sin, x1 * sin + x2 * cos], axis=-1).astype(x.dtype)


def chunk_causal_attention(q, k, v, scale):
    bsz, seq, heads, dk = q.shape
    n_blocks = seq // Q_BLOCK
    q_blocks = q.reshape(bsz, n_blocks, Q_BLOCK, heads, dk).transpose(1, 0, 2, 3, 4)
    key_chunk = jnp.arange(seq) // CHUNK

    def one_block(args):
        i, qi = args
        s = jnp.einsum('bqhd,bkhd->bhqk', qi, k, preferred_element_type=jnp.float32) * scale
        q_chunk = (i * Q_BLOCK + jnp.arange(Q_BLOCK)) // CHUNK
        mask = key_chunk[None, :] <= q_chunk[:, None]
        s = jnp.where(mask[None, None], s, -jnp.inf)
        p = jax.nn.softmax(s, axis=-1)
        return jnp.einsum('bhqk,bkhd->bqhd', p.astype(v.dtype), v)

    out = lax.map(one_block, (jnp.arange(n_blocks), q_blocks))
    return out.transpose(1, 0, 2, 3, 4).reshape(bsz, seq, heads, v.shape[-1])


def hybrid_mixer(h, cos, sin, w_in, b_conv_in, w_conv_dw, b_conv_dw, conv_ln_g, conv_ln_b,
                 q_lat_norm_g, w_uq, kv_lat_norm_g, w_ukv, q_norm_g, k_norm_g, w_out):
    bsz, seq, _ = h.shape
    z = h @ w_in
    s1 = 2 * CONV_CH
    s2 = s1 + MLA_Q_RANK
    s3 = s2 + MLA_KV_RANK
    conv_in, c_q, c_kv, k_rope = jnp.split(z, [s1, s2, s3], axis=-1)

    a, gate = jnp.split(conv_in + b_conv_in, 2, axis=-1)
    u = a * jax.nn.sigmoid(gate)
    u = causal_depthwise_conv(u, w_conv_dw, b_conv_dw)
    u = jax.nn.silu(layer_norm(u, conv_ln_g, conv_ln_b))

    q = (rms_norm(c_q, q_lat_norm_g) @ w_uq).reshape(bsz, seq, MLA_HEADS, MLA_NOPE + MLA_ROPE)
    kv = (rms_norm(c_kv, kv_lat_norm_g) @ w_ukv).reshape(bsz, seq, MLA_HEADS, MLA_NOPE + MLA_V)
    k_nope, v = jnp.split(kv, [MLA_NOPE], axis=-1)
    k_r = jnp.broadcast_to(k_rope[:, :, None, :], (bsz, seq, MLA_HEADS, MLA_ROPE))
    k = jnp.concatenate([k_nope, k_r], axis=-1)
    q = rms_norm(q, q_norm_g)
    k = rms_norm(k, k_norm_g)
    q = jnp.concatenate([q[..., :MLA_NOPE], apply_rope(q[..., MLA_NOPE:], cos, sin)], axis=-1)
    k = jnp.concatenate([k[..., :MLA_NOPE], apply_rope(k[..., MLA_NOPE:], cos, sin)], axis=-1)
    attn = chunk_causal_attention(q, k, v, 1.0 / math.sqrt(MLA_NOPE + MLA_ROPE))
    attn = attn.reshape(bsz, seq, MLA_HEADS * MLA_V)

    return jnp.concatenate([u, attn], axis=-1) @ w_out


def memory_cross_attention(hq, hm, w_mem_q, w_mem_kv, mem_q_norm_g, mem_k_norm_g, w_mem_o):
    bsz, seq, _ = hq.shape
    q = (hq @ w_mem_q).reshape(bsz, seq, MEM_HEADS, MEM_HEAD_DIM)
    k, v = jnp.split(hm @ w_mem_kv, 2, axis=-1)
    k = k.reshape(bsz, MEM_LEN, MEM_HEADS, MEM_HEAD_DIM)
    v = v.reshape(bsz, MEM_LEN, MEM_HEADS, MEM_HEAD_DIM)
    q = rms_norm(q, mem_q_norm_g)
    k = rms_norm(k, mem_k_norm_g)
    s = jnp.einsum('bqhd,bkhd->bhqk', q, k, preferred_element_type=jnp.float32) / math.sqrt(MEM_HEAD_DIM)
    p = jax.nn.softmax(s, axis=-1)
    o = jnp.einsum('bhqk,bkhd->bqhd', p.astype(v.dtype), v).reshape(bsz, seq, D_MODEL)
    return o @ w_mem_o


def conv_gated_ffn(h, w_up, w_ffn_dw, b_ffn_dw, w_down):
    up = causal_depthwise_conv(h @ w_up, w_ffn_dw, b_ffn_dw)
    g, val = jnp.split(up, 2, axis=-1)
    return (jax.nn.silu(g) * val) @ w_down


def _fwd_setup_inputs(seed: int = 0) -> dict:
    key = jax.random.key(seed)
    ks = iter(jax.random.split(key, 40))
    L = DEPTH

    def w(shape, fan_in):
        return jax.random.normal(next(ks), shape, jnp.float32) * fan_in ** -0.5

    def gain(shape):
        return 1.0 + 0.05 * jax.random.normal(next(ks), shape, jnp.float32)

    def bias(shape):
        return 0.02 * jax.random.normal(next(ks), shape, jnp.float32)

    x = jax.random.normal(next(ks), (BATCH, SEQ, D_MODEL), jnp.float32)
    mem = jax.random.normal(next(ks), (BATCH, MEM_LEN, D_MODEL), jnp.float32)
    start = jax.random.randint(next(ks), (BATCH, 1), 0, MAX_START, dtype=jnp.int32)
    positions = start + jnp.arange(SEQ, dtype=jnp.int32)[None, :]
    return {
        "x": x,
        "mem": mem,
        "positions": positions,
        "mix_norm_g": gain((L, D_MODEL)),
        "w_in": w((L, D_MODEL, IN_COLS), D_MODEL),
        "b_conv_in": bias((L, 2 * CONV_CH)),
        "w_conv_dw": w((L, CONV_WIDTH, CONV_CH), CONV_WIDTH),
        "b_conv_dw": bias((L, CONV_CH)),
        "conv_ln_g": gain((L, CONV_CH)),
        "conv_ln_b": bias((L, CONV_CH)),
        "q_lat_norm_g": gain((L, MLA_Q_RANK)),
        "w_uq": w((L, MLA_Q_RANK, MLA_HEADS * (MLA_NOPE + MLA_ROPE)), MLA_Q_RANK),
        "kv_lat_norm_g": gain((L, MLA_KV_RANK)),
        "w_ukv": w((L, MLA_KV_RANK, MLA_HEADS * (MLA_NOPE + MLA_V)), MLA_KV_RANK),
        "q_norm_g": gain((L, MLA_NOPE + MLA_ROPE)),
        "k_norm_g": gain((L, MLA_NOPE + MLA_ROPE)),
        "w_out": w((L, MIX_WIDTH, D_MODEL), MIX_WIDTH),
        "mem_norm_x_g": gain((L, D_MODEL)),
        "mem_norm_m_g": gain((L, D_MODEL)),
        "w_mem_q": w((L, D_MODEL, D_MODEL), D_MODEL),
        "w_mem_kv": w((L, D_MODEL, 2 * D_MODEL), D_MODEL),
        "mem_q_norm_g": gain((L, MEM_HEAD_DIM)),
        "mem_k_norm_g": gain((L, MEM_HEAD_DIM)),
        "w_mem_o": w((L, D_MODEL, D_MODEL), D_MODEL),
        "ffn_norm_g": gain((L, D_MODEL)),
        "w_up": w((L, D_MODEL, 2 * D_FF), D_MODEL),
        "w_ffn_dw": w((L, FFN_CONV_WIDTH, 2 * D_FF), FFN_CONV_WIDTH),
        "b_ffn_dw": bias((L, 2 * D_FF)),
        "w_down": w((L, D_FF, D_MODEL), D_FF),
    }


def _fwd_reference(x, mem, positions, mix_norm_g, w_in, b_conv_in, w_conv_dw, b_conv_dw, conv_ln_g,
              conv_ln_b, q_lat_norm_g, w_uq, kv_lat_norm_g, w_ukv, q_norm_g, k_norm_g, w_out,
              mem_norm_x_g, mem_norm_m_g, w_mem_q, w_mem_kv, mem_q_norm_g, mem_k_norm_g, w_mem_o,
              ffn_norm_g, w_up, w_ffn_dw, b_ffn_dw, w_down):
    cos, sin = rope_tables(positions, MLA_ROPE)
    cos, sin = cos[:, :, None, :], sin[:, :, None, :]
    for l in range(DEPTH):
        h = rms_norm(x, mix_norm_g[l])
        x = x + hybrid_mixer(h, cos, sin, w_in[l], b_conv_in[l], w_conv_dw[l], b_conv_dw[l],
                             conv_ln_g[l], conv_ln_b[l], q_lat_norm_g[l], w_uq[l],
                             kv_lat_norm_g[l], w_ukv[l], q_norm_g[l], k_norm_g[l], w_out[l])
        hq = rms_norm(x, mem_norm_x_g[l])
        hm = rms_norm(mem, mem_norm_m_g[l])
        x = x + memory_cross_attention(hq, hm, w_mem_q[l], w_mem_kv[l], mem_q_norm_g[l],
                                       mem_k_norm_g[l], w_mem_o[l])
        h = rms_norm(x, ffn_norm_g[l])
        x = x + conv_gated_ffn(h, w_up[l], w_ffn_dw[l], b_ffn_dw[l], w_down[l])
    return x


import jax as _jax
import jax.numpy as _jnp

TWIN_FORMAT = 'train_step'
FWD_PARAMS = ['x', 'mem', 'positions', 'mix_norm_g', 'w_in', 'b_conv_in', 'w_conv_dw', 'b_conv_dw', 'conv_ln_g', 'conv_ln_b', 'q_lat_norm_g', 'w_uq', 'kv_lat_norm_g', 'w_ukv', 'q_norm_g', 'k_norm_g', 'w_out', 'mem_norm_x_g', 'mem_norm_m_g', 'w_mem_q', 'w_mem_kv', 'mem_q_norm_g', 'mem_k_norm_g', 'w_mem_o', 'ffn_norm_g', 'w_up', 'w_ffn_dw', 'b_ffn_dw', 'w_down']
TWIN_WEIGHTS = ['mix_norm_g', 'w_in', 'b_conv_in', 'w_conv_dw', 'b_conv_dw', 'conv_ln_g', 'conv_ln_b', 'q_lat_norm_g', 'w_uq', 'kv_lat_norm_g', 'w_ukv', 'q_norm_g', 'k_norm_g', 'w_out', 'mem_norm_x_g', 'mem_norm_m_g', 'w_mem_q', 'w_mem_kv', 'mem_q_norm_g', 'mem_k_norm_g', 'w_mem_o', 'ffn_norm_g', 'w_up', 'w_ffn_dw', 'b_ffn_dw', 'w_down']
TWIN_DIFF_INPUT = 'x'
TWIN_INPUTS = ['x', 'mem', 'positions', 'mix_norm_g', 'w_in', 'b_conv_in', 'w_conv_dw', 'b_conv_dw', 'conv_ln_g', 'conv_ln_b', 'q_lat_norm_g', 'w_uq', 'kv_lat_norm_g', 'w_ukv', 'q_norm_g', 'k_norm_g', 'w_out', 'mem_norm_x_g', 'mem_norm_m_g', 'w_mem_q', 'w_mem_kv', 'mem_q_norm_g', 'mem_k_norm_g', 'w_mem_o', 'ffn_norm_g', 'w_up', 'w_ffn_dw', 'b_ffn_dw', 'w_down', 'loss_target', 'm_mix_norm_g', 'm_w_in', 'm_b_conv_in', 'm_w_conv_dw', 'm_b_conv_dw', 'm_conv_ln_g', 'm_conv_ln_b', 'm_q_lat_norm_g', 'm_w_uq', 'm_kv_lat_norm_g', 'm_w_ukv', 'm_q_norm_g', 'm_k_norm_g', 'm_w_out', 'm_mem_norm_x_g', 'm_mem_norm_m_g', 'm_w_mem_q', 'm_w_mem_kv', 'm_mem_q_norm_g', 'm_mem_k_norm_g', 'm_w_mem_o', 'm_ffn_norm_g', 'm_w_up', 'm_w_ffn_dw', 'm_b_ffn_dw', 'm_w_down', 'v_mix_norm_g', 'v_w_in', 'v_b_conv_in', 'v_w_conv_dw', 'v_b_conv_dw', 'v_conv_ln_g', 'v_conv_ln_b', 'v_q_lat_norm_g', 'v_w_uq', 'v_kv_lat_norm_g', 'v_w_ukv', 'v_q_norm_g', 'v_k_norm_g', 'v_w_out', 'v_mem_norm_x_g', 'v_mem_norm_m_g', 'v_w_mem_q', 'v_w_mem_kv', 'v_mem_q_norm_g', 'v_mem_k_norm_g', 'v_w_mem_o', 'v_ffn_norm_g', 'v_w_up', 'v_w_ffn_dw', 'v_b_ffn_dw', 'v_w_down']
TWIN_OUTPUTS = ['loss', 'grad_x', 'grad_mix_norm_g', 'grad_w_in', 'grad_b_conv_in', 'grad_w_conv_dw', 'grad_b_conv_dw', 'grad_conv_ln_g', 'grad_conv_ln_b', 'grad_q_lat_norm_g', 'grad_w_uq', 'grad_kv_lat_norm_g', 'grad_w_ukv', 'grad_q_norm_g', 'grad_k_norm_g', 'grad_w_out', 'grad_mem_norm_x_g', 'grad_mem_norm_m_g', 'grad_w_mem_q', 'grad_w_mem_kv', 'grad_mem_q_norm_g', 'grad_mem_k_norm_g', 'grad_w_mem_o', 'grad_ffn_norm_g', 'grad_w_up', 'grad_w_ffn_dw', 'grad_b_ffn_dw', 'grad_w_down', 'delta_mix_norm_g', 'delta_w_in', 'delta_b_conv_in', 'delta_w_conv_dw', 'delta_b_conv_dw', 'delta_conv_ln_g', 'delta_conv_ln_b', 'delta_q_lat_norm_g', 'delta_w_uq', 'delta_kv_lat_norm_g', 'delta_w_ukv', 'delta_q_norm_g', 'delta_k_norm_g', 'delta_w_out', 'delta_mem_norm_x_g', 'delta_mem_norm_m_g', 'delta_w_mem_q', 'delta_w_mem_kv', 'delta_mem_q_norm_g', 'delta_mem_k_norm_g', 'delta_w_mem_o', 'delta_ffn_norm_g', 'delta_w_up', 'delta_w_ffn_dw', 'delta_b_ffn_dw', 'delta_w_down', 'new_m_mix_norm_g', 'new_m_w_in', 'new_m_b_conv_in', 'new_m_w_conv_dw', 'new_m_b_conv_dw', 'new_m_conv_ln_g', 'new_m_conv_ln_b', 'new_m_q_lat_norm_g', 'new_m_w_uq', 'new_m_kv_lat_norm_g', 'new_m_w_ukv', 'new_m_q_norm_g', 'new_m_k_norm_g', 'new_m_w_out', 'new_m_mem_norm_x_g', 'new_m_mem_norm_m_g', 'new_m_w_mem_q', 'new_m_w_mem_kv', 'new_m_mem_q_norm_g', 'new_m_mem_k_norm_g', 'new_m_w_mem_o', 'new_m_ffn_norm_g', 'new_m_w_up', 'new_m_w_ffn_dw', 'new_m_b_ffn_dw', 'new_m_w_down', 'new_v_mix_norm_g', 'new_v_w_in', 'new_v_b_conv_in', 'new_v_w_conv_dw', 'new_v_b_conv_dw', 'new_v_conv_ln_g', 'new_v_conv_ln_b', 'new_v_q_lat_norm_g', 'new_v_w_uq', 'new_v_kv_lat_norm_g', 'new_v_w_ukv', 'new_v_q_norm_g', 'new_v_k_norm_g', 'new_v_w_out', 'new_v_mem_norm_x_g', 'new_v_mem_norm_m_g', 'new_v_w_mem_q', 'new_v_w_mem_kv', 'new_v_mem_q_norm_g', 'new_v_mem_k_norm_g', 'new_v_w_mem_o', 'new_v_ffn_norm_g', 'new_v_w_up', 'new_v_w_ffn_dw', 'new_v_b_ffn_dw', 'new_v_w_down']
TWIN_LEAF_KINDS = {'loss': 'loss', 'grad_x': 'grad_x', 'grad_mix_norm_g': 'grad_w', 'grad_w_in': 'grad_w', 'grad_b_conv_in': 'grad_w', 'grad_w_conv_dw': 'grad_w', 'grad_b_conv_dw': 'grad_w', 'grad_conv_ln_g': 'grad_w', 'grad_conv_ln_b': 'grad_w', 'grad_q_lat_norm_g': 'grad_w', 'grad_w_uq': 'grad_w', 'grad_kv_lat_norm_g': 'grad_w', 'grad_w_ukv': 'grad_w', 'grad_q_norm_g': 'grad_w', 'grad_k_norm_g': 'grad_w', 'grad_w_out': 'grad_w', 'grad_mem_norm_x_g': 'grad_w', 'grad_mem_norm_m_g': 'grad_w', 'grad_w_mem_q': 'grad_w', 'grad_w_mem_kv': 'grad_w', 'grad_mem_q_norm_g': 'grad_w', 'grad_mem_k_norm_g': 'grad_w', 'grad_w_mem_o': 'grad_w', 'grad_ffn_norm_g': 'grad_w', 'grad_w_up': 'grad_w', 'grad_w_ffn_dw': 'grad_w', 'grad_b_ffn_dw': 'grad_w', 'grad_w_down': 'grad_w', 'delta_mix_norm_g': 'delta_w', 'delta_w_in': 'delta_w', 'delta_b_conv_in': 'delta_w', 'delta_w_conv_dw': 'delta_w', 'delta_b_conv_dw': 'delta_w', 'delta_conv_ln_g': 'delta_w', 'delta_conv_ln_b': 'delta_w', 'delta_q_lat_norm_g': 'delta_w', 'delta_w_uq': 'delta_w', 'delta_kv_lat_norm_g': 'delta_w', 'delta_w_ukv': 'delta_w', 'delta_q_norm_g': 'delta_w', 'delta_k_norm_g': 'delta_w', 'delta_w_out': 'delta_w', 'delta_mem_norm_x_g': 'delta_w', 'delta_mem_norm_m_g': 'delta_w', 'delta_w_mem_q': 'delta_w', 'delta_w_mem_kv': 'delta_w', 'delta_mem_q_norm_g': 'delta_w', 'delta_mem_k_norm_g': 'delta_w', 'delta_w_mem_o': 'delta_w', 'delta_ffn_norm_g': 'delta_w', 'delta_w_up': 'delta_w', 'delta_w_ffn_dw': 'delta_w', 'delta_b_ffn_dw': 'delta_w', 'delta_w_down': 'delta_w', 'new_m_mix_norm_g': 'new_m', 'new_m_w_in': 'new_m', 'new_m_b_conv_in': 'new_m', 'new_m_w_conv_dw': 'new_m', 'new_m_b_conv_dw': 'new_m', 'new_m_conv_ln_g': 'new_m', 'new_m_conv_ln_b': 'new_m', 'new_m_q_lat_norm_g': 'new_m', 'new_m_w_uq': 'new_m', 'new_m_kv_lat_norm_g': 'new_m', 'new_m_w_ukv': 'new_m', 'new_m_q_norm_g': 'new_m', 'new_m_k_norm_g': 'new_m', 'new_m_w_out': 'new_m', 'new_m_mem_norm_x_g': 'new_m', 'new_m_mem_norm_m_g': 'new_m', 'new_m_w_mem_q': 'new_m', 'new_m_w_mem_kv': 'new_m', 'new_m_mem_q_norm_g': 'new_m', 'new_m_mem_k_norm_g': 'new_m', 'new_m_w_mem_o': 'new_m', 'new_m_ffn_norm_g': 'new_m', 'new_m_w_up': 'new_m', 'new_m_w_ffn_dw': 'new_m', 'new_m_b_ffn_dw': 'new_m', 'new_m_w_down': 'new_m', 'new_v_mix_norm_g': 'new_v', 'new_v_w_in': 'new_v', 'new_v_b_conv_in': 'new_v', 'new_v_w_conv_dw': 'new_v', 'new_v_b_conv_dw': 'new_v', 'new_v_conv_ln_g': 'new_v', 'new_v_conv_ln_b': 'new_v', 'new_v_q_lat_norm_g': 'new_v', 'new_v_w_uq': 'new_v', 'new_v_kv_lat_norm_g': 'new_v', 'new_v_w_ukv': 'new_v', 'new_v_q_norm_g': 'new_v', 'new_v_k_norm_g': 'new_v', 'new_v_w_out': 'new_v', 'new_v_mem_norm_x_g': 'new_v', 'new_v_mem_norm_m_g': 'new_v', 'new_v_w_mem_q': 'new_v', 'new_v_w_mem_kv': 'new_v', 'new_v_mem_q_norm_g': 'new_v', 'new_v_mem_k_norm_g': 'new_v', 'new_v_w_mem_o': 'new_v', 'new_v_ffn_norm_g': 'new_v', 'new_v_w_up': 'new_v', 'new_v_w_ffn_dw': 'new_v', 'new_v_b_ffn_dw': 'new_v', 'new_v_w_down': 'new_v'}


def _forward(args):
    return _fwd_reference(*[args[k] for k in FWD_PARAMS])


def _output_shape():
    def fwd():
        inp = _fwd_setup_inputs(0)
        return _fwd_reference(*[inp[k] for k in FWD_PARAMS])
    out = _jax.eval_shape(fwd)
    return out.shape, out.dtype

N_MICROBATCH = 1
ADAM_LR = 0.001
ADAM_B1 = 0.9
ADAM_B2 = 0.999
ADAM_EPS = 1e-08
ADAM_WD = 0.01
ADAM_STEP = 10
PER_EXAMPLE_BATCH_AXIS = {'x': 0, 'mem': 0, 'positions': 0, 'loss_target': 0}
SHARED_INPUTS = []
_WEIGHT_DTYPES = {'mix_norm_g': _jnp.float32, 'w_in': _jnp.float32, 'b_conv_in': _jnp.float32, 'w_conv_dw': _jnp.float32, 'b_conv_dw': _jnp.float32, 'conv_ln_g': _jnp.float32, 'conv_ln_b': _jnp.float32, 'q_lat_norm_g': _jnp.float32, 'w_uq': _jnp.float32, 'kv_lat_norm_g': _jnp.float32, 'w_ukv': _jnp.float32, 'q_norm_g': _jnp.float32, 'k_norm_g': _jnp.float32, 'w_out': _jnp.float32, 'mem_norm_x_g': _jnp.float32, 'mem_norm_m_g': _jnp.float32, 'w_mem_q': _jnp.float32, 'w_mem_kv': _jnp.float32, 'mem_q_norm_g': _jnp.float32, 'mem_k_norm_g': _jnp.float32, 'w_mem_o': _jnp.float32, 'ffn_norm_g': _jnp.float32, 'w_up': _jnp.float32, 'w_ffn_dw': _jnp.float32, 'b_ffn_dw': _jnp.float32, 'w_down': _jnp.float32}
MOMENT_SCALE = {'mix_norm_g': 3.926688e-01, 'w_in': 3.095627e-01, 'b_conv_in': 7.970704e+00, 'w_conv_dw': 1.185680e+00, 'b_conv_dw': 2.039199e+01, 'conv_ln_g': 2.884556e+01, 'conv_ln_b': 2.118571e+01, 'q_lat_norm_g': 1.514039e-01, 'w_uq': 8.974302e-02, 'kv_lat_norm_g': 1.848748e+00, 'w_ukv': 1.390169e-01, 'q_norm_g': 2.125654e+00, 'k_norm_g': 2.124591e+00, 'w_out': 3.167045e+00, 'mem_norm_x_g': 1.265632e-01, 'mem_norm_m_g': 8.498720e-01, 'w_mem_q': 1.146046e-01, 'w_mem_kv': 2.984166e-01, 'mem_q_norm_g': 2.687653e+00, 'mem_k_norm_g': 2.688170e+00, 'w_mem_o': 4.694075e-01, 'ffn_norm_g': 5.405769e+01, 'w_up': 1.184247e+00, 'w_ffn_dw': 7.828395e+00, 'b_ffn_dw': 6.947320e+00, 'w_down': 7.423182e-01}


def _to_microbatches(a, axis):
    t = _jnp.moveaxis(a, axis, 0)
    t = t.reshape((N_MICROBATCH, t.shape[0] // N_MICROBATCH) + t.shape[1:])
    return _jnp.moveaxis(t, 1, axis + 1)


def setup_inputs(seed: int = 0) -> dict:
    inp = _fwd_setup_inputs(seed)
    key = _jax.random.fold_in(_jax.random.key(seed), 7919)
    shape, _ = _output_shape()
    out = dict(inp)
    out["loss_target"] = _jax.random.normal(_jax.random.fold_in(key, 0), shape, _jnp.float32)
    for i, name in enumerate(TWIN_WEIGHTS):
        w = inp[name].astype(_jnp.float32)
        if MOMENT_SCALE is None:
            s = _jnp.sqrt(_jnp.mean(_jnp.square(w)) + 1e-30)
        else:
            s = MOMENT_SCALE[name]
        km, kv = _jax.random.split(_jax.random.fold_in(key, i + 1))
        out[name] = w
        out["m_" + name] = s * _jax.random.normal(km, w.shape, _jnp.float32)
        out["v_" + name] = (s * s) * _jax.random.uniform(kv, w.shape, _jnp.float32, 0.5, 1.5)
    if N_MICROBATCH > 1:
        for name, axis in PER_EXAMPLE_BATCH_AXIS.items():
            out[name] = _to_microbatches(out[name], axis)
    return {'x': out['x'], 'mem': out['mem'], 'positions': out['positions'], 'mix_norm_g': out['mix_norm_g'], 'w_in': out['w_in'], 'b_conv_in': out['b_conv_in'], 'w_conv_dw': out['w_conv_dw'], 'b_conv_dw': out['b_conv_dw'], 'conv_ln_g': out['conv_ln_g'], 'conv_ln_b': out['conv_ln_b'], 'q_lat_norm_g': out['q_lat_norm_g'], 'w_uq': out['w_uq'], 'kv_lat_norm_g': out['kv_lat_norm_g'], 'w_ukv': out['w_ukv'], 'q_norm_g': out['q_norm_g'], 'k_norm_g': out['k_norm_g'], 'w_out': out['w_out'], 'mem_norm_x_g': out['mem_norm_x_g'], 'mem_norm_m_g': out['mem_norm_m_g'], 'w_mem_q': out['w_mem_q'], 'w_mem_kv': out['w_mem_kv'], 'mem_q_norm_g': out['mem_q_norm_g'], 'mem_k_norm_g': out['mem_k_norm_g'], 'w_mem_o': out['w_mem_o'], 'ffn_norm_g': out['ffn_norm_g'], 'w_up': out['w_up'], 'w_ffn_dw': out['w_ffn_dw'], 'b_ffn_dw': out['b_ffn_dw'], 'w_down': out['w_down'], 'loss_target': out['loss_target'], 'm_mix_norm_g': out['m_mix_norm_g'], 'm_w_in': out['m_w_in'], 'm_b_conv_in': out['m_b_conv_in'], 'm_w_conv_dw': out['m_w_conv_dw'], 'm_b_conv_dw': out['m_b_conv_dw'], 'm_conv_ln_g': out['m_conv_ln_g'], 'm_conv_ln_b': out['m_conv_ln_b'], 'm_q_lat_norm_g': out['m_q_lat_norm_g'], 'm_w_uq': out['m_w_uq'], 'm_kv_lat_norm_g': out['m_kv_lat_norm_g'], 'm_w_ukv': out['m_w_ukv'], 'm_q_norm_g': out['m_q_norm_g'], 'm_k_norm_g': out['m_k_norm_g'], 'm_w_out': out['m_w_out'], 'm_mem_norm_x_g': out['m_mem_norm_x_g'], 'm_mem_norm_m_g': out['m_mem_norm_m_g'], 'm_w_mem_q': out['m_w_mem_q'], 'm_w_mem_kv': out['m_w_mem_kv'], 'm_mem_q_norm_g': out['m_mem_q_norm_g'], 'm_mem_k_norm_g': out['m_mem_k_norm_g'], 'm_w_mem_o': out['m_w_mem_o'], 'm_ffn_norm_g': out['m_ffn_norm_g'], 'm_w_up': out['m_w_up'], 'm_w_ffn_dw': out['m_w_ffn_dw'], 'm_b_ffn_dw': out['m_b_ffn_dw'], 'm_w_down': out['m_w_down'], 'v_mix_norm_g': out['v_mix_norm_g'], 'v_w_in': out['v_w_in'], 'v_b_conv_in': out['v_b_conv_in'], 'v_w_conv_dw': out['v_w_conv_dw'], 'v_b_conv_dw': out['v_b_conv_dw'], 'v_conv_ln_g': out['v_conv_ln_g'], 'v_conv_ln_b': out['v_conv_ln_b'], 'v_q_lat_norm_g': out['v_q_lat_norm_g'], 'v_w_uq': out['v_w_uq'], 'v_kv_lat_norm_g': out['v_kv_lat_norm_g'], 'v_w_ukv': out['v_w_ukv'], 'v_q_norm_g': out['v_q_norm_g'], 'v_k_norm_g': out['v_k_norm_g'], 'v_w_out': out['v_w_out'], 'v_mem_norm_x_g': out['v_mem_norm_x_g'], 'v_mem_norm_m_g': out['v_mem_norm_m_g'], 'v_w_mem_q': out['v_w_mem_q'], 'v_w_mem_kv': out['v_w_mem_kv'], 'v_mem_q_norm_g': out['v_mem_q_norm_g'], 'v_mem_k_norm_g': out['v_mem_k_norm_g'], 'v_w_mem_o': out['v_w_mem_o'], 'v_ffn_norm_g': out['v_ffn_norm_g'], 'v_w_up': out['v_w_up'], 'v_w_ffn_dw': out['v_w_ffn_dw'], 'v_b_ffn_dw': out['v_b_ffn_dw'], 'v_w_down': out['v_w_down']}


def _loss(weights, diff, rest, loss_target):
    with _jax.named_scope("forward"):
        args = {**rest, TWIN_DIFF_INPUT: diff, **{k: w.astype(_WEIGHT_DTYPES[k]) for k, w in weights.items()}}
        y = _forward(args)
    with _jax.named_scope("loss_head"):
        err = _jnp.square(y.astype(_jnp.float32) - loss_target)
        return 0.5 * _jnp.sum(_jnp.mean(err, axis=-1)) if err.ndim else 0.5 * err


def _adamw(w, g, m, v):
    m = ADAM_B1 * m + (1.0 - ADAM_B1) * g
    v = ADAM_B2 * v + (1.0 - ADAM_B2) * _jnp.square(g)
    m_hat = m / (1.0 - ADAM_B1 ** ADAM_STEP)
    v_hat = v / (1.0 - ADAM_B2 ** ADAM_STEP)
    delta = -ADAM_LR * (m_hat / (_jnp.sqrt(v_hat) + ADAM_EPS) + ADAM_WD * w)
    return delta, m, v


def reference(x, mem, positions, mix_norm_g, w_in, b_conv_in, w_conv_dw, b_conv_dw, conv_ln_g, conv_ln_b, q_lat_norm_g, w_uq, kv_lat_norm_g, w_ukv, q_norm_g, k_norm_g, w_out, mem_norm_x_g, mem_norm_m_g, w_mem_q, w_mem_kv, mem_q_norm_g, mem_k_norm_g, w_mem_o, ffn_norm_g, w_up, w_ffn_dw, b_ffn_dw, w_down, loss_target, m_mix_norm_g, m_w_in, m_b_conv_in, m_w_conv_dw, m_b_conv_dw, m_conv_ln_g, m_conv_ln_b, m_q_lat_norm_g, m_w_uq, m_kv_lat_norm_g, m_w_ukv, m_q_norm_g, m_k_norm_g, m_w_out, m_mem_norm_x_g, m_mem_norm_m_g, m_w_mem_q, m_w_mem_kv, m_mem_q_norm_g, m_mem_k_norm_g, m_w_mem_o, m_ffn_norm_g, m_w_up, m_w_ffn_dw, m_b_ffn_dw, m_w_down, v_mix_norm_g, v_w_in, v_b_conv_in, v_w_conv_dw, v_b_conv_dw, v_conv_ln_g, v_conv_ln_b, v_q_lat_norm_g, v_w_uq, v_kv_lat_norm_g, v_w_ukv, v_q_norm_g, v_k_norm_g, v_w_out, v_mem_norm_x_g, v_mem_norm_m_g, v_w_mem_q, v_w_mem_kv, v_mem_q_norm_g, v_mem_k_norm_g, v_w_mem_o, v_ffn_norm_g, v_w_up, v_w_ffn_dw, v_b_ffn_dw, v_w_down):
    given = dict(x=x, mem=mem, positions=positions, mix_norm_g=mix_norm_g, w_in=w_in, b_conv_in=b_conv_in, w_conv_dw=w_conv_dw, b_conv_dw=b_conv_dw, conv_ln_g=conv_ln_g, conv_ln_b=conv_ln_b, q_lat_norm_g=q_lat_norm_g, w_uq=w_uq, kv_lat_norm_g=kv_lat_norm_g, w_ukv=w_ukv, q_norm_g=q_norm_g, k_norm_g=k_norm_g, w_out=w_out, mem_norm_x_g=mem_norm_x_g, mem_norm_m_g=mem_norm_m_g, w_mem_q=w_mem_q, w_mem_kv=w_mem_kv, mem_q_norm_g=mem_q_norm_g, mem_k_norm_g=mem_k_norm_g, w_mem_o=w_mem_o, ffn_norm_g=ffn_norm_g, w_up=w_up, w_ffn_dw=w_ffn_dw, b_ffn_dw=b_ffn_dw, w_down=w_down, loss_target=loss_target, m_mix_norm_g=m_mix_norm_g, m_w_in=m_w_in, m_b_conv_in=m_b_conv_in, m_w_conv_dw=m_w_conv_dw, m_b_conv_dw=m_b_conv_dw, m_conv_ln_g=m_conv_ln_g, m_conv_ln_b=m_conv_ln_b, m_q_lat_norm_g=m_q_lat_norm_g, m_w_uq=m_w_uq, m_kv_lat_norm_g=m_kv_lat_norm_g, m_w_ukv=m_w_ukv, m_q_norm_g=m_q_norm_g, m_k_norm_g=m_k_norm_g, m_w_out=m_w_out, m_mem_norm_x_g=m_mem_norm_x_g, m_mem_norm_m_g=m_mem_norm_m_g, m_w_mem_q=m_w_mem_q, m_w_mem_kv=m_w_mem_kv, m_mem_q_norm_g=m_mem_q_norm_g, m_mem_k_norm_g=m_mem_k_norm_g, m_w_mem_o=m_w_mem_o, m_ffn_norm_g=m_ffn_norm_g, m_w_up=m_w_up, m_w_ffn_dw=m_w_ffn_dw, m_b_ffn_dw=m_b_ffn_dw, m_w_down=m_w_down, v_mix_norm_g=v_mix_norm_g, v_w_in=v_w_in, v_b_conv_in=v_b_conv_in, v_w_conv_dw=v_w_conv_dw, v_b_conv_dw=v_b_conv_dw, v_conv_ln_g=v_conv_ln_g, v_conv_ln_b=v_conv_ln_b, v_q_lat_norm_g=v_q_lat_norm_g, v_w_uq=v_w_uq, v_kv_lat_norm_g=v_kv_lat_norm_g, v_w_ukv=v_w_ukv, v_q_norm_g=v_q_norm_g, v_k_norm_g=v_k_norm_g, v_w_out=v_w_out, v_mem_norm_x_g=v_mem_norm_x_g, v_mem_norm_m_g=v_mem_norm_m_g, v_w_mem_q=v_w_mem_q, v_w_mem_kv=v_w_mem_kv, v_mem_q_norm_g=v_mem_q_norm_g, v_mem_k_norm_g=v_mem_k_norm_g, v_w_mem_o=v_w_mem_o, v_ffn_norm_g=v_ffn_norm_g, v_w_up=v_w_up, v_w_ffn_dw=v_w_ffn_dw, v_b_ffn_dw=v_b_ffn_dw, v_w_down=v_w_down)
    weights = {n: given[n] for n in TWIN_WEIGHTS}
    shared = {n: given[n] for n in SHARED_INPUTS}
    per_example = {n: given[n] for n in ['x', 'mem', 'positions']}
    grad_fn = _jax.value_and_grad(_loss, argnums=(0, 1))

    def one_microbatch(ex, loss_target):
        ex = dict(ex)
        diff = ex.pop(TWIN_DIFF_INPUT)
        return grad_fn(weights, diff, {**shared, **ex}, loss_target)

    if N_MICROBATCH == 1:
        loss, (grad_w, grad_x) = one_microbatch(per_example, given["loss_target"])
    else:
        def body(carry, xs):
            loss_sum, grad_sum = carry
            l_k, (gw_k, gx_k) = one_microbatch(xs[0], xs[1])
            with _jax.named_scope("update"):
                return (loss_sum + l_k, _jax.tree.map(_jnp.add, grad_sum, gw_k)), gx_k

        init = (_jnp.zeros((), _jnp.float32), _jax.tree.map(_jnp.zeros_like, weights))
        (loss, grad_w), grad_x = _jax.lax.scan(body, init, (per_example, given["loss_target"]))
    with _jax.named_scope("update"):
        delta_w, new_m, new_v = {}, {}, {}
        for n in TWIN_WEIGHTS:
            delta_w[n], new_m[n], new_v[n] = _adamw(weights[n], grad_w[n], given["m_" + n], given["v_" + n])
    return (loss, grad_x, *[grad_w[n] for n in TWIN_WEIGHTS], *[delta_w[n] for n in TWIN_WEIGHTS],
            *[new_m[n] for n in TWIN_WEIGHTS], *[new_v[n] for n in TWIN_WEIGHTS])
```

```python
import functools
import math

import jax
import jax.numpy as jnp
from jax import lax
from jax.experimental import pallas as pl
from jax.experimental.pallas import tpu as pltpu

F32 = jnp.float32
BF16 = jnp.bfloat16
EPS = 1e-6
LANES = 128
N_DEV = 8
D_MODEL = 1024
CONV_CH = 512
CONV_WIDTH = 31
MLA_HEADS = 8
MLA_NOPE = 64
MLA_ROPE = 32
MLA_V = 64
MLA_QK = MLA_NOPE + MLA_ROPE
MLA_Q_RANK = 256
MLA_KV_RANK = 128
ROPE_THETA = 10000.0
IN_COLS = 2 * CONV_CH + MLA_Q_RANK + MLA_KV_RANK + MLA_ROPE
IN_PAD = 2 * CONV_CH + MLA_Q_RANK + MLA_KV_RANK + LANES
MEM_HEADS = 4
MEM_HEAD_DIM = 256
D_FF = 2816
FFN_CONV_WIDTH = 3
CHUNK = 64
ATT_SCALE = 1.0 / math.sqrt(MLA_QK)
MEM_SCALE = 1.0 / math.sqrt(MEM_HEAD_DIM)
ADAM_LR, ADAM_B1, ADAM_B2, ADAM_EPS, ADAM_WD, ADAM_STEP = 0.001, 0.9, 0.999, 1e-08, 0.01, 10

TM = 512
TQ = 512
CONV_ROWS = 256
NEG = -1e30
VMEM_LIMIT = 56 * 1024 * 1024

MESH = pl.DeviceIdType.MESH
ANY = pl.BlockSpec(memory_space=pl.ANY)
NT_DIMS = (((1,), (1,)), ((), ()))

BIG = [
    ("w_in", (1024, 1440), 1), ("w_conv_dw", (31, 512), 1), ("w_uq", (256, 768), 1),
    ("w_ukv", (128, 1024), 1), ("w_out", (1024, 1024), 0), ("w_mem_q", (1024, 1024), 0),
    ("w_mem_kv", (1024, 2048), 1), ("w_mem_o", (1024, 1024), 0), ("w_up", (1024, 5632), 1),
    ("w_ffn_dw", (3, 5632), 1), ("w_down", (2816, 1024), 0),
]
SMALL = [
    ("mix_norm_g", 1024), ("b_conv_in", 1024), ("b_conv_dw", 512), ("conv_ln_g", 512),
    ("conv_ln_b", 512), ("q_lat_norm_g", 256), ("kv_lat_norm_g", 128), ("q_norm_g", 96),
    ("k_norm_g", 96), ("mem_norm_x_g", 1024), ("mem_norm_m_g", 1024), ("mem_q_norm_g", 256),
    ("mem_k_norm_g", 256), ("ffn_norm_g", 1024), ("b_ffn_dw", 5632),
]
WEIGHTS = [
    "mix_norm_g", "w_in", "b_conv_in", "w_conv_dw", "b_conv_dw", "conv_ln_g", "conv_ln_b",
    "q_lat_norm_g", "w_uq", "kv_lat_norm_g", "w_ukv", "q_norm_g", "k_norm_g", "w_out",
    "mem_norm_x_g", "mem_norm_m_g", "w_mem_q", "w_mem_kv", "mem_q_norm_g", "mem_k_norm_g",
    "w_mem_o", "ffn_norm_g", "w_up", "w_ffn_dw", "b_ffn_dw", "w_down",
]


def _shard_shape(shape, axis):
    s = list(shape)
    s[axis] //= N_DEV
    return tuple(s)


BIG_ROWS = sum(math.prod(_shard_shape(s, a)) for _, s, a in BIG) // LANES
SMALL_PAD = [(-(-n // LANES)) * LANES for _, n in SMALL]
SMALL_ROWS = -(-sum(SMALL_PAD) // (8 * LANES)) * 8


def _call(body, **kw):
    return pl.pallas_call(body, **kw)


def _params(*sem):
    return pltpu.CompilerParams(dimension_semantics=sem, vmem_limit_bytes=VMEM_LIMIT)


def _all_gather(xs, name):
    rows_per, cols = xs.shape

    def body(x_ref, out_ref, send_sems, recv_sems, local_sem):
        x, y, c = lax.axis_index("x"), lax.axis_index("y"), lax.axis_index("c")
        me, sibling = (x, y, c), (x, y, 1 - c)
        chips = [(1 - x, y), (x, 1 - y), (1 - x, 1 - y)]

        def rows(px, py, pc):
            return out_ref.at[4 * px + 2 * py + pc]

        def copy(k, block, to, src=None):
            return pltpu.make_async_remote_copy(
                src_ref=rows(*block) if src is None else src, dst_ref=rows(*block),
                send_sem=send_sems.at[k], recv_sem=recv_sems.at[k],
                device_id=to, device_id_type=MESH)

        mine = pltpu.make_async_copy(x_ref, rows(*me), local_sem)
        mine.start()
        first = [copy(0, me, sibling, src=x_ref)]
        first += [copy(1 + j, me, (*chip, c), src=x_ref) for j, chip in enumerate(chips)]
        for cp in first:
            cp.start()
        passed = [copy(4 + j, (*chip, c), sibling) for j, chip in enumerate(chips)]
        for j, chip in enumerate(chips):
            copy(1 + j, (*chip, c), me).wait_recv()
            passed[j].start()
        copy(0, sibling, me).wait_recv()
        for j, chip in enumerate(chips):
            copy(4 + j, (*chip, 1 - c), me).wait_recv()
        for cp in first + passed:
            cp.wait_send()
        mine.wait()

    return _call(
        body, name=name, out_shape=jax.ShapeDtypeStruct((N_DEV, rows_per, cols), xs.dtype),
        in_specs=[ANY], out_specs=ANY,
        scratch_shapes=[pltpu.SemaphoreType.DMA((7,)), pltpu.SemaphoreType.DMA((7,)),
                        pltpu.SemaphoreType.DMA(())],
    )(xs)


def _swap_sibling(send, name):
    def body(s_ref, r_ref, send_sem, recv_sem):
        x, y, c = lax.axis_index("x"), lax.axis_index("y"), lax.axis_index("c")
        cp = pltpu.make_async_remote_copy(
            src_ref=s_ref, dst_ref=r_ref, send_sem=send_sem, recv_sem=recv_sem,
            device_id=(x, y, 1 - c), device_id_type=MESH)
        cp.start()
        cp.wait()

    return _call(
        body, name=name, out_shape=jax.ShapeDtypeStruct(send.shape, send.dtype),
        in_specs=[ANY], out_specs=ANY,
        scratch_shapes=[pltpu.SemaphoreType.DMA(()), pltpu.SemaphoreType.DMA(())],
    )(send)


def _swap_chips(a, name):
    def body(a_ref, r_ref, send_sems, recv_sems):
        x, y, c = lax.axis_index("x"), lax.axis_index("y"), lax.axis_index("c")
        peers = [(x, 1 - y, c), (1 - x, y, c), (1 - x, 1 - y, c)]
        copies = [
            pltpu.make_async_remote_copy(
                src_ref=a_ref.at[k + 1], dst_ref=r_ref.at[k], send_sem=send_sems.at[k],
                recv_sem=recv_sems.at[k], device_id=peer, device_id_type=MESH)
            for k, peer in enumerate(peers)]
        for cp in copies:
            cp.start()
        for cp in copies:
            cp.wait()

    return _call(
        body, name=name, out_shape=jax.ShapeDtypeStruct((3,) + a.shape[1:], a.dtype),
        in_specs=[ANY], out_specs=ANY,
        scratch_shapes=[pltpu.SemaphoreType.DMA((3,)), pltpu.SemaphoreType.DMA((3,))],
    )(a)


def _tile(n, prefs):
    for t in prefs:
        if n % t == 0:
            return t
    return n


def _mm(a, b, *, name, trans_b=False, add=None, out_dtype=F32, tm=TM):
    m, k = a.shape
    n = b.shape[0] if trans_b else b.shape[1]
    tn = _tile(n, (1536, 1408, 1024, 768, 512, 256, 128))
    tk = _tile(k, (1408, 1024, 768, 512, 256, 128))
    nk = k // tk
    has_add = add is not None

    def body(*refs):
        a_ref, b_ref = refs[0], refs[1]
        add_ref = refs[2] if has_add else None
        o_ref = refs[2 + has_add]
        av = a_ref[...].astype(BF16)
        bv = b_ref[...].astype(BF16)
        if trans_b:
            part = lax.dot_general(av, bv, NT_DIMS, preferred_element_type=F32)
        else:
            part = jnp.dot(av, bv, preferred_element_type=F32)

        def finish(acc):
            if has_add:
                acc = acc + add_ref[...].astype(F32)
            o_ref[...] = acc.astype(o_ref.dtype)

        if nk == 1:
            finish(part)
        else:
            acc_ref = refs[3 + has_add]
            kk = pl.program_id(2)

            @pl.when(kk == 0)
            def _():
                acc_ref[...] = part

            @pl.when(kk > 0)
            def _():
                acc_ref[...] += part

            @pl.when(kk == nk - 1)
            def _():
                finish(acc_ref[...])

    in_specs = [pl.BlockSpec((tm, tk), lambda i, j, kk: (i, kk))]
    if trans_b:
        in_specs.append(pl.BlockSpec((tn, tk), lambda i, j, kk: (j, kk)))
    else:
        in_specs.append(pl.BlockSpec((tk, tn), lambda i, j, kk: (kk, j)))
    args = [a, b]
    if has_add:
        in_specs.append(pl.BlockSpec((tm, tn), lambda i, j, kk: (i, j)))
        args.append(add)
    return _call(
        body, name=name, grid=(m // tm, n // tn, nk), in_specs=in_specs,
        out_specs=pl.BlockSpec((tm, tn), lambda i, j, kk: (i, j)),
        out_shape=jax.ShapeDtypeStruct((m, n), out_dtype),
        scratch_shapes=[pltpu.VMEM((tm, tn), F32)] if nk > 1 else [],
        compiler_params=_params("parallel", "parallel", "arbitrary"),
    )(*args)


def _mm_tn(a, b, *, name, ts=TM):
    s, m = a.shape
    n = b.shape[1]
    tm = _tile(m, (1408, 1024, 768, 512, 256, 128))
    tn = _tile(n, (1536, 1408, 1024, 768, 512, 256, 128))

    def body(a_ref, b_ref, o_ref):
        kk = pl.program_id(2)
        part = jnp.dot(a_ref[...].astype(BF16).T, b_ref[...].astype(BF16),
                       preferred_element_type=F32)

        @pl.when(kk == 0)
        def _():
            o_ref[...] = part

        @pl.when(kk > 0)
        def _():
            o_ref[...] += part

    return _call(
        body, name=name, grid=(m // tm, n // tn, s // ts),
        in_specs=[pl.BlockSpec((ts, tm), lambda i, j, kk: (kk, i)),
                  pl.BlockSpec((ts, tn), lambda i, j, kk: (kk, j))],
        out_specs=pl.BlockSpec((tm, tn), lambda i, j, kk: (i, j)),
        out_shape=jax.ShapeDtypeStruct((m, n), F32),
        compiler_params=_params("parallel", "parallel", "arbitrary"),
    )(a, b)


def _rowwise(fn, rows, consts, row_outs, acc_outs, *, name, tm=TM, n_rows=None):
    rows = [r if isinstance(r, tuple) else (r, r.shape[1], 0, 0) for r in rows]
    s = n_rows or rows[0][0].shape[0]
    nr, nc, no, na = len(rows), len(consts), len(row_outs), len(acc_outs)

    def body(*refs):
        r_in, c_in = refs[:nr], refs[nr:nr + nc]
        o_refs, a_refs = refs[nr + nc:nr + nc + no], refs[nr + nc + no:]
        outs = fn(*[r[...] for r in r_in], *[c[...] for c in c_in])
        for r, v in zip(o_refs, outs[:no]):
            r[...] = v.astype(r.dtype)
        if na:
            i = pl.program_id(0)

            @pl.when(i == 0)
            def _():
                for r, v in zip(a_refs, outs[no:]):
                    r[...] = v.astype(F32)

            @pl.when(i > 0)
            def _():
                for r, v in zip(a_refs, outs[no:]):
                    r[...] += v.astype(F32)

    in_specs = [pl.BlockSpec((tm, w), lambda i, cb=cb, rb=rb: (i + rb, cb)) for _, w, cb, rb in rows]
    in_specs += [pl.BlockSpec(c.shape, lambda i: (0, 0)) for c in consts]
    out_specs = [pl.BlockSpec((tm, w), lambda i: (i, 0)) for w, _ in row_outs]
    out_specs += [pl.BlockSpec(sh, lambda i: (0, 0)) for sh in acc_outs]
    out_shape = [jax.ShapeDtypeStruct((s, w), dt) for w, dt in row_outs]
    out_shape += [jax.ShapeDtypeStruct(sh, F32) for sh in acc_outs]
    return _call(
        body, name=name, grid=(s // tm,), in_specs=in_specs, out_specs=out_specs,
        out_shape=out_shape, compiler_params=_params("arbitrary"),
    )(*[r[0] for r in rows], *consts)


def _rms(x, g, n=None):
    ms = jnp.sum(x * x, axis=-1, keepdims=True) / float(n or x.shape[-1])
    return x * lax.rsqrt(ms + EPS) * g


def _layer_norm(x, g, b):
    mu = jnp.sum(x, axis=-1, keepdims=True) / float(x.shape[-1])
    xc = x - mu
    var = jnp.sum(xc * xc, axis=-1, keepdims=True) / float(x.shape[-1])
    return xc * lax.rsqrt(var + EPS) * g + b


def _silu(x):
    return x * jax.nn.sigmoid(x)


@jax.custom_vjp
def _rope(y, cos, sin_a, sin_b):
    return y * cos + pltpu.roll(y, 112, 1) * sin_a + pltpu.roll(y, 16, 1) * sin_b


def _rope_fwd(y, cos, sin_a, sin_b):
    return _rope(y, cos, sin_a, sin_b), (cos, sin_a, sin_b)


def _rope_bwd(res, ct):
    cos, sin_a, sin_b = res
    dy = ct * cos + pltpu.roll(ct * sin_a, 16, 1) + pltpu.roll(ct * sin_b, 112, 1)
    return dy, jnp.zeros_like(cos), jnp.zeros_like(sin_a), jnp.zeros_like(sin_b)


_rope.defvjp(_rope_fwd, _rope_bwd)


def _qk_head(xh, g, cos, sin_a, sin_b):
    return _rope(_rms(xh, g, MLA_QK), cos, sin_a, sin_b)


def _heads(x, width):
    return [x[:, h * width:(h + 1) * width] for h in range(x.shape[1] // width)]


def _f_rms(x, g):
    return (_rms(x, g),)


def _f_rope_tab(pos, inv_freq):
    ang = pos.astype(F32) * inv_freq
    lane = lax.broadcasted_iota(jnp.int32, ang.shape, 1)
    sn = jnp.sin(ang)
    first = (lane >= MLA_NOPE) & (lane < MLA_NOPE + MLA_ROPE // 2)
    second = (lane >= MLA_NOPE + MLA_ROPE // 2) & (lane < MLA_QK)
    return jnp.cos(ang), jnp.where(first, -sn, 0.0), jnp.where(second, sn, 0.0)


def _mix_pre(za, zg, zcq, zckv, ba, bg, gq, gkv):
    u0 = (za + ba) * jax.nn.sigmoid(zg + bg)
    return u0, _rms(zcq, gq), _rms(zckv, gkv)


def _ln_silu(c1, bdw, lg, lb):
    return _silu(_layer_norm(c1 + bdw, lg, lb))


def _f_qk_prep(q0, kn, kr, cos, sa, sb, gq, gk):
    qs = [_qk_head(xh, gq, cos, sa, sb) for xh in _heads(q0, LANES)]
    ks = [_qk_head(xh + kr, gk, cos, sa, sb) for xh in _heads(kn, LANES)]
    return jnp.concatenate(qs, axis=1), jnp.concatenate(ks, axis=1)


def _act(cg, cv, bg, bv):
    return _silu(cg + bg) * (cv + bv)


def _f_mem_k(kk, g):
    return (jnp.concatenate([_rms(xh, g) for xh in _heads(kk, MEM_HEAD_DIM)], axis=1),)


def _mem_probs(qn, kmh):
    s = lax.dot_general(qn.astype(BF16), kmh, NT_DIMS, preferred_element_type=F32) * MEM_SCALE
    e = jnp.exp(s - jnp.max(s, axis=-1, keepdims=True))
    return e / jnp.sum(e, axis=-1, keepdims=True)


def _f_mem_attn(qm0, km, vm, g):
    outs = []
    for h, xh in enumerate(_heads(qm0, MEM_HEAD_DIM)):
        sl = slice(h * MEM_HEAD_DIM, (h + 1) * MEM_HEAD_DIM)
        p = _mem_probs(_rms(xh, g), km[:, sl])
        outs.append(jnp.dot(p.astype(BF16), vm[:, sl].astype(BF16), preferred_element_type=F32))
    return (jnp.concatenate(outs, axis=1),)


def _f_loss(y, t):
    e = y - t
    return e * (1.0 / D_MODEL), jnp.sum(e * e, axis=0, keepdims=True)


def _b_rms(x, dh, dres, g):
    _, vjp = jax.vjp(_rms, x, g)
    dx, dg = vjp(dh)
    return dx + dres, dg


def _b_rms_nores(x, dh, g):
    _, vjp = jax.vjp(_rms, x, g)
    dx, dg = vjp(dh)
    return dx, dg


def _b_mix_pre(za, zg, zcq, zckv, du0, dcqn, dckvn, dkr, ba, bg, gq, gkv):
    _, vjp = jax.vjp(_mix_pre, za, zg, zcq, zckv, ba, bg, gq, gkv)
    dza, dzg, dzcq, dzckv, dba, dbg, dgq, dgkv = vjp((du0, dcqn, dckvn))
    return jnp.concatenate([dza, dzg, dzcq, dzckv, dkr], axis=1), dba, dbg, dgq, dgkv


def _b_ln_silu(c1, du, bdw, lg, lb):
    _, vjp = jax.vjp(_ln_silu, c1, bdw, lg, lb)
    return vjp(du)


def _b_qk_prep(q0, kn, kr, cos, sa, sb, dq, dk, gq, gk):
    head = lambda xh, g: _qk_head(xh, g, cos, sa, sb)
    dq0, dkn = [], []
    dkr = jnp.zeros_like(kr)
    dgq = jnp.zeros_like(gq)
    dgk = jnp.zeros_like(gk)
    for xh, ct in zip(_heads(q0, LANES), _heads(dq, LANES)):
        _, vjp = jax.vjp(head, xh, gq)
        dx, dg = vjp(ct)
        dq0.append(dx)
        dgq = dgq + dg
    for xh, ct in zip(_heads(kn, LANES), _heads(dk, LANES)):
        _, vjp = jax.vjp(head, xh + kr, gk)
        dx, dg = vjp(ct)
        dkn.append(dx)
        dkr = dkr + dx
        dgk = dgk + dg
    return jnp.concatenate(dq0, axis=1), jnp.concatenate(dkn, axis=1), dkr, dgq, dgk


def _b_act(cg, cv, dact, bg, bv):
    _, vjp = jax.vjp(_act, cg, cv, bg, bv)
    return vjp(dact)


def _b_mem_k(kk, dkm, g):
    dkk = []
    dg = jnp.zeros_like(g)
    for xh, ct in zip(_heads(kk, MEM_HEAD_DIM), _heads(dkm, MEM_HEAD_DIM)):
        _, vjp = jax.vjp(_rms, xh, g)
        dx, dgh = vjp(ct)
        dkk.append(dx)
        dg = dg + dgh
    return jnp.concatenate(dkk, axis=1), dg


def _b_mem_attn(dom, qm0, km, vm, g):
    dq0, dkm, dvm = [], [], []
    dg = jnp.zeros_like(g)
    for h, (xh, doh) in enumerate(zip(_heads(qm0, MEM_HEAD_DIM), _heads(dom, MEM_HEAD_DIM))):
        sl = slice(h * MEM_HEAD_DIM, (h + 1) * MEM_HEAD_DIM)
        kmh, vmh = km[:, sl], vm[:, sl].astype(BF16)
        qn, vjp = jax.vjp(_rms, xh, g)
        p = _mem_probs(qn, kmh)
        dob = doh.astype(BF16)
        dp = lax.dot_general(dob, vmh, NT_DIMS, preferred_element_type=F32)
        ds = (p * (dp - jnp.sum(dp * p, axis=-1, keepdims=True)) * MEM_SCALE).astype(BF16)
        dqn = jnp.dot(ds, kmh, preferred_element_type=F32)
        dkm.append(jnp.dot(ds.T, qn.astype(BF16), preferred_element_type=F32))
        dvm.append(jnp.dot(p.astype(BF16).T, dob, preferred_element_type=F32))
        dx, dgh = vjp(dqn)
        dq0.append(dx)
        dg = dg + dgh
    return (jnp.concatenate(dq0, axis=1), jnp.concatenate(dkm, axis=1),
            jnp.concatenate(dvm, axis=1), dg)


def _adamw(w, g, m, v):
    m = ADAM_B1 * m + (1.0 - ADAM_B1) * g
    v = ADAM_B2 * v + (1.0 - ADAM_B2) * jnp.square(g)
    m_hat = m / (1.0 - ADAM_B1 ** ADAM_STEP)
    v_hat = v / (1.0 - ADAM_B2 ** ADAM_STEP)
    delta = -ADAM_LR * (m_hat / (jnp.sqrt(v_hat) + ADAM_EPS) + ADAM_WD * w)
    return delta, m, v


def _f_adam_big(a0, r1, r2, r3, w, m, v):
    g = ((a0 + r1) + r2) + r3
    return (g,) + _adamw(w, g, m, v)


def _f_adam_small(parts, w, m, v):
    g = parts[0]
    for k in range(1, N_DEV):
        g = g + parts[k]
    return (g,) + _adamw(w, g, m, v)


def _conv_fwd(x, w, name):
    s, ch = x.shape
    kw = w.shape[0]
    halo = -(-(kw - 1) // 8) * 8
    r = CONV_ROWS
    n = s // r

    def chunk(window, wv):
        acc = jnp.zeros((r, LANES), F32)
        for k in range(kw):
            shift = kw - 1 - k
            sh = window if shift == 0 else pltpu.roll(window, shift, 0)
            acc = acc + sh[halo:halo + r] * wv[k:k + 1]
        return acc

    def body(x_ref, w_ref, y_ref):
        wv = w_ref[...]
        first = jnp.concatenate([jnp.zeros((halo, LANES), F32), x_ref[0:r]], axis=0)
        y_ref[0:r] = chunk(first, wv)

        def step(i, carry):
            base = pl.multiple_of(i * r, 8)
            y_ref[pl.ds(base, r)] = chunk(x_ref[pl.ds(base - halo, r + halo)], wv)
            return carry

        lax.fori_loop(1, n, step, 0)

    return _call(
        body, name=name, grid=(ch // LANES,),
        in_specs=[pl.BlockSpec((s, LANES), lambda c: (0, c)), pl.BlockSpec((kw, LANES), lambda c: (0, c))],
        out_specs=pl.BlockSpec((s, LANES), lambda c: (0, c)),
        out_shape=jax.ShapeDtypeStruct((s, ch), F32), compiler_params=_params("parallel"),
    )(x, w)


def _conv_bwd(dy, x, w, name):
    s, ch = x.shape
    kw = w.shape[0]
    halo = -(-(kw - 1) // 8) * 8
    r = CONV_ROWS
    n = s // r

    def dx_chunk(window, wv):
        acc = jnp.zeros((r, LANES), F32)
        for k in range(kw):
            shift = kw - 1 - k
            sh = window if shift == 0 else pltpu.roll(window, r + halo - shift, 0)
            acc = acc + sh[0:r] * wv[k:k + 1]
        return acc

    def dw_chunk(xwin, dyc, acc_ref):
        for k in range(kw):
            shift = kw - 1 - k
            sh = xwin if shift == 0 else pltpu.roll(xwin, shift, 0)
            prod = sh[halo:halo + r] * dyc
            acc_ref[k] += jnp.sum(prod.reshape(r // 8, 8, LANES), axis=0)

    def body(dy_ref, x_ref, w_ref, dx_ref, dw_ref, acc_ref):
        wv = w_ref[...]
        acc_ref[...] = jnp.zeros_like(acc_ref)
        xfirst = jnp.concatenate([jnp.zeros((halo, LANES), F32), x_ref[0:r]], axis=0)
        dw_chunk(xfirst, dy_ref[0:r], acc_ref)
        last = jnp.concatenate([dy_ref[s - r:s], jnp.zeros((halo, LANES), F32)], axis=0)
        dx_ref[s - r:s] = dx_chunk(last, wv)

        def step(i, carry):
            base = pl.multiple_of(i * r, 8)
            dw_chunk(x_ref[pl.ds(base - halo, r + halo)], dy_ref[pl.ds(base, r)], acc_ref)
            prev = pl.multiple_of((i - 1) * r, 8)
            dx_ref[pl.ds(prev, r)] = dx_chunk(dy_ref[pl.ds(prev, r + halo)], wv)
            return carry

        lax.fori_loop(1, n, step, 0)
        dw_ref[...] = jnp.sum(acc_ref[...], axis=1)

    spec = pl.BlockSpec((s, LANES), lambda c: (0, c))
    wspec = pl.BlockSpec((kw, LANES), lambda c: (0, c))
    return _call(
        body, name=name, grid=(ch // LANES,), in_specs=[spec, spec, wspec], out_specs=[spec, wspec],
        out_shape=[jax.ShapeDtypeStruct((s, ch), F32), jax.ShapeDtypeStruct((kw, ch), F32)],
        scratch_shapes=[pltpu.VMEM((kw, 8, LANES), F32)], compiler_params=_params("parallel"),
    )(dy, x, w)


def _chunk_mask(rows_are_queries):
    a = lax.broadcasted_iota(jnp.int32, (TQ, TQ), 0) // CHUNK
    b = lax.broadcasted_iota(jnp.int32, (TQ, TQ), 1) // CHUNK
    return (b <= a) if rows_are_queries else (a <= b)


def _flash_fwd(q, k, v, name):
    s = q.shape[0]
    nq = s // TQ

    def body(q_ref, k_ref, v_ref, o_ref, lse_ref):
        i = pl.program_id(1)
        qv = q_ref[...]

        def step(j, carry, masked):
            m_prev, l_prev, acc = carry
            base = pl.multiple_of(j * TQ, TQ)
            kj = k_ref[pl.ds(base, TQ), :]
            vj = v_ref[pl.ds(base, TQ), :].astype(BF16)
            sc = lax.dot_general(qv, kj, NT_DIMS, preferred_element_type=F32) * ATT_SCALE
            if masked:
                sc = jnp.where(_chunk_mask(True), sc, NEG)
            m_new = jnp.maximum(m_prev, jnp.max(sc, axis=-1, keepdims=True))
            alpha = jnp.exp(m_prev - m_new)
            p = jnp.exp(sc - m_new)
            l_new = alpha * l_prev + jnp.sum(p, axis=-1, keepdims=True)
            acc = acc * alpha + jnp.dot(p.astype(BF16), vj, preferred_element_type=F32)
            return m_new, l_new, acc

        init = (jnp.full((TQ, 1), NEG, F32), jnp.zeros((TQ, 1), F32), jnp.zeros((TQ, LANES), F32))
        carry = lax.fori_loop(0, i, functools.partial(step, masked=False), init)
        m_fin, l_fin, acc = step(i, carry, True)
        o_ref[...] = acc / l_fin
        lse_ref[...] = jnp.broadcast_to(m_fin + jnp.log(l_fin), (TQ, LANES))

    tile = pl.BlockSpec((TQ, LANES), lambda h, i: (i, h))
    whole = pl.BlockSpec((s, LANES), lambda h, i: (0, h))
    return _call(
        body, name=name, grid=(MLA_HEADS, nq), in_specs=[tile, whole, whole], out_specs=[tile, tile],
        out_shape=[jax.ShapeDtypeStruct((s, MLA_HEADS * LANES), F32)] * 2,
        compiler_params=_params("parallel", "arbitrary"),
    )(q, k, v)


def _flash_bwd_dq(q, k, v, do, o, lse, name):
    s = q.shape[0]
    nq = s // TQ

    def body(q_ref, k_ref, v_ref, do_ref, o_ref, lse_ref, dq_ref):
        i = pl.program_id(1)
        qv = q_ref[...]
        dov = do_ref[...]
        delta = jnp.sum(dov * o_ref[...], axis=-1, keepdims=True)
        lse1 = lse_ref[...][:, 0:1]
        dob = dov.astype(BF16)

        def step(j, dq, masked):
            base = pl.multiple_of(j * TQ, TQ)
            kj = k_ref[pl.ds(base, TQ), :]
            vj = v_ref[pl.ds(base, TQ), :].astype(BF16)
            sc = lax.dot_general(qv, kj, NT_DIMS, preferred_element_type=F32) * ATT_SCALE
            if masked:
                sc = jnp.where(_chunk_mask(True), sc, NEG)
            p = jnp.exp(sc - lse1)
            dp = lax.dot_general(dob, vj, NT_DIMS, preferred_element_type=F32)
            ds = (p * (dp - delta) * ATT_SCALE).astype(BF16)
            return dq + jnp.dot(ds, kj, preferred_element_type=F32)

        dq = lax.fori_loop(0, i, functools.partial(step, masked=False), jnp.zeros((TQ, LANES), F32))
        dq_ref[...] = step(i, dq, True)

    tile = pl.BlockSpec((TQ, LANES), lambda h, i: (i, h))
    whole = pl.BlockSpec((s, LANES), lambda h, i: (0, h))
    return _call(
        body, name=name, grid=(MLA_HEADS, nq), in_specs=[tile, whole, whole, tile, tile, tile],
        out_specs=tile, out_shape=jax.ShapeDtypeStruct((s, MLA_HEADS * LANES), F32),
        compiler_params=_params("parallel", "arbitrary"),
    )(q, k, v, do, o, lse)


def _flash_bwd_dkv(q, k, v, do, o, lse, name):
    s = q.shape[0]
    nq = s // TQ

    def body(q_ref, k_ref, v_ref, do_ref, o_ref, lse_ref, dk_ref, dv_ref):
        j = pl.program_id(1)
        kj = k_ref[...]
        vj = v_ref[...].astype(BF16)

        def step(i, carry, masked):
            dk, dv = carry
            base = pl.multiple_of(i * TQ, TQ)
            qi = q_ref[pl.ds(base, TQ), :]
            doi = do_ref[pl.ds(base, TQ), :]
            oi = o_ref[pl.ds(base, TQ), :]
            lse_t = lse_ref[pl.ds(base, TQ), :].T[0:1, :]
            delta = jnp.sum(doi * oi, axis=-1, keepdims=True)
            delta_t = jnp.broadcast_to(delta, (TQ, LANES)).T[0:1, :]
            dob = doi.astype(BF16)
            sc_t = lax.dot_general(kj, qi, NT_DIMS, preferred_element_type=F32) * ATT_SCALE
            if masked:
                sc_t = jnp.where(_chunk_mask(False), sc_t, NEG)
            p_t = jnp.exp(sc_t - lse_t)
            dv = dv + jnp.dot(p_t.astype(BF16), dob, preferred_element_type=F32)
            dp_t = lax.dot_general(vj, dob, NT_DIMS, preferred_element_type=F32)
            ds_t = (p_t * (dp_t - delta_t) * ATT_SCALE).astype(BF16)
            dk = dk + jnp.dot(ds_t, qi, preferred_element_type=F32)
            return dk, dv

        zero = jnp.zeros((TQ, LANES), F32)
        carry = step(j, (zero, zero), True)
        dk, dv = lax.fori_loop(j + 1, nq, functools.partial(step, masked=False), carry)
        dk_ref[...] = dk
        dv_ref[...] = dv

    tile = pl.BlockSpec((TQ, LANES), lambda h, j: (j, h))
    whole = pl.BlockSpec((s, LANES), lambda h, j: (0, h))
    return _call(
        body, name=name, grid=(MLA_HEADS, nq), in_specs=[whole, tile, tile, whole, whole, whole],
        out_specs=[tile, tile], out_shape=[jax.ShapeDtypeStruct((s, MLA_HEADS * LANES), F32)] * 2,
        compiler_params=_params("parallel", "arbitrary"),
    )(q, k, v, do, o, lse)


def _pack_shards(shards):
    flat = jnp.concatenate([shards[n].reshape(-1) for n, _, _ in BIG])
    return flat.reshape(BIG_ROWS, LANES)


def _unpack_shards(packed):
    flat = packed.reshape(-1)
    out, off = {}, 0
    for n, shape, axis in BIG:
        sh = _shard_shape(shape, axis)
        size = math.prod(sh)
        out[n] = flat[off:off + size].reshape((1,) + sh)
        off += size
    return out


def _gathered_to_full(gathered):
    flat = gathered.reshape(N_DEV, -1)
    out, off = {}, 0
    for n, shape, axis in BIG:
        sh = _shard_shape(shape, axis)
        size = math.prod(sh)
        part = flat[:, off:off + size].reshape((N_DEV,) + sh)
        off += size
        if axis == 0:
            out[n] = part.reshape(shape)
        else:
            out[n] = part.transpose(1, 0, 2).reshape(shape)
    return out


def _full_to_shards(full):
    parts = []
    for n, shape, axis in BIG:
        g = full[n]
        if axis == 0:
            parts.append(g.reshape(N_DEV, -1))
        else:
            parts.append(g.reshape(shape[0], N_DEV, shape[1] // N_DEV).transpose(1, 0, 2).reshape(N_DEV, -1))
    return jnp.concatenate(parts, axis=1)


def _pack_small(vals):
    parts = []
    for (n, size), pad in zip(SMALL, SMALL_PAD):
        parts.append(jnp.pad(vals[n].reshape(-1), (0, pad - size)))
    flat = jnp.concatenate(parts)
    return jnp.pad(flat, (0, SMALL_ROWS * LANES - flat.shape[0])).reshape(SMALL_ROWS, LANES)


def _unpack_small(packed):
    flat = packed.reshape(-1)
    out, off = {}, 0
    for (n, size), pad in zip(SMALL, SMALL_PAD):
        out[n] = flat[off:off + size].reshape(1, size)
        off += pad
    return out


def _pad_heads(w, per_head, axis):
    shape = list(w.shape)
    shape[axis:axis + 1] = [MLA_HEADS, per_head]
    w = w.reshape(shape)
    pad = [(0, 0)] * len(shape)
    pad[axis + 1] = (0, LANES - per_head)
    w = jnp.pad(w, pad)
    shape[axis:axis + 2] = [MLA_HEADS * LANES]
    return w.reshape(shape)


def _unpad_heads(w, per_head, axis):
    shape = list(w.shape)
    shape[axis:axis + 1] = [MLA_HEADS, LANES]
    w = w.reshape(shape)
    w = lax.slice_in_dim(w, 0, per_head, axis=axis + 1)
    shape[axis:axis + 2] = [MLA_HEADS * per_head]
    return w.reshape(shape)


def _row(v, pad_to=None):
    v = v.reshape(1, -1)
    if pad_to is not None:
        v = jnp.pad(v, ((0, 0), (0, pad_to - v.shape[1])))
    return v


def kernel(x, mem, positions, mix_norm_g, w_in, b_conv_in, w_conv_dw, b_conv_dw, conv_ln_g, conv_ln_b, q_lat_norm_g, w_uq, kv_lat_norm_g, w_ukv, q_norm_g, k_norm_g, w_out, mem_norm_x_g, mem_norm_m_g, w_mem_q, w_mem_kv, mem_q_norm_g, mem_k_norm_g, w_mem_o, ffn_norm_g, w_up, w_ffn_dw, b_ffn_dw, w_down, loss_target, m_mix_norm_g, m_w_in, m_b_conv_in, m_w_conv_dw, m_b_conv_dw, m_conv_ln_g, m_conv_ln_b, m_q_lat_norm_g, m_w_uq, m_kv_lat_norm_g, m_w_ukv, m_q_norm_g, m_k_norm_g, m_w_out, m_mem_norm_x_g, m_mem_norm_m_g, m_w_mem_q, m_w_mem_kv, m_mem_q_norm_g, m_mem_k_norm_g, m_w_mem_o, m_ffn_norm_g, m_w_up, m_w_ffn_dw, m_b_ffn_dw, m_w_down, v_mix_norm_g, v_w_in, v_b_conv_in, v_w_conv_dw, v_b_conv_dw, v_conv_ln_g, v_conv_ln_b, v_q_lat_norm_g, v_w_uq, v_kv_lat_norm_g, v_w_ukv, v_q_norm_g, v_k_norm_g, v_w_out, v_mem_norm_x_g, v_mem_norm_m_g, v_w_mem_q, v_w_mem_kv, v_mem_q_norm_g, v_mem_k_norm_g, v_w_mem_o, v_ffn_norm_g, v_w_up, v_w_ffn_dw, v_b_ffn_dw, v_w_down):
    a = dict(locals())
    seq = x.shape[1]
    xs = x.reshape(seq, D_MODEL)
    mems = mem.reshape(-1, D_MODEL)
    target = loss_target.reshape(seq, D_MODEL)
    me = 4 * lax.axis_index("x") + 2 * lax.axis_index("y") + lax.axis_index("c")

    packed_w = _pack_shards({n: a[n] for n, _, _ in BIG})
    full = _gathered_to_full(_all_gather(packed_w.astype(BF16), "ag_weights"))
    wi = full["w_in"]
    s3 = 2 * CONV_CH + MLA_Q_RANK + MLA_KV_RANK
    w_in_p = jnp.concatenate([
        wi[:, :s3], jnp.zeros((D_MODEL, MLA_NOPE), BF16), wi[:, s3:],
        jnp.zeros((D_MODEL, LANES - MLA_QK), BF16)], axis=1)
    w_uq_p = _pad_heads(full["w_uq"], MLA_QK, 1)
    ukv = full["w_ukv"].reshape(MLA_KV_RANK, MLA_HEADS, MLA_NOPE + MLA_V)
    w_uk_p = _pad_heads(ukv[:, :, :MLA_NOPE].reshape(MLA_KV_RANK, -1), MLA_NOPE, 1)
    w_uv_p = _pad_heads(ukv[:, :, MLA_NOPE:].reshape(MLA_KV_RANK, -1), MLA_V, 1)
    w_out_u = full["w_out"][:CONV_CH]
    w_out_a = _pad_heads(full["w_out"][CONV_CH:], MLA_V, 0)
    w_mq, w_mkv, w_mo = full["w_mem_q"], full["w_mem_kv"], full["w_mem_o"]
    w_up_g, w_up_v = full["w_up"][:, :D_FF], full["w_up"][:, D_FF:]
    w_dn = full["w_down"]
    w_cdw = full["w_conv_dw"].astype(F32)
    w_fdw = full["w_ffn_dw"].astype(F32)
    w_fdw_g, w_fdw_v = w_fdw[:, :D_FF], w_fdw[:, D_FF:]

    g_mix, g_qlat, g_kvlat = _row(mix_norm_g), _row(q_lat_norm_g), _row(kv_lat_norm_g)
    b_in = _row(b_conv_in)
    b_in_a, b_in_g = b_in[:, :CONV_CH], b_in[:, CONV_CH:]
    b_cdw, ln_g, ln_b = _row(b_conv_dw), _row(conv_ln_g), _row(conv_ln_b)
    g_q, g_k = _row(q_norm_g, LANES), _row(k_norm_g, LANES)
    g_memx, g_memm = _row(mem_norm_x_g), _row(mem_norm_m_g)
    g_mq, g_mk, g_ffn = _row(mem_q_norm_g), _row(mem_k_norm_g), _row(ffn_norm_g)
    b_f = _row(b_ffn_dw)
    b_f_g, b_f_v = b_f[:, :D_FF], b_f[:, D_FF:]

    freq = ROPE_THETA ** (-jnp.arange(0, MLA_ROPE, 2, dtype=F32) / MLA_ROPE)
    inv_freq = jnp.concatenate([jnp.zeros((MLA_NOPE,), F32), freq, freq,
                                jnp.zeros((LANES - MLA_QK,), F32)]).reshape(1, LANES)
    cos, sin_a, sin_b = _rowwise(_f_rope_tab, [positions.reshape(seq, 1)], [inv_freq],
                                 [(LANES, F32)] * 3, [], name="rope_tables")

    (h1,) = _rowwise(_f_rms, [xs], [g_mix], [(D_MODEL, BF16)], [], name="rms_mix")
    z = _mm(h1, w_in_p, name="mm_in")
    z_rows = [(z, CONV_CH, 0, 0), (z, CONV_CH, 1, 0), (z, MLA_Q_RANK, 4, 0), (z, MLA_KV_RANK, 10, 0)]
    z_kr = (z, LANES, 11, 0)
    u0, cqn, ckvn = _rowwise(
        _mix_pre, z_rows, [b_in_a, b_in_g, g_qlat, g_kvlat],
        [(CONV_CH, F32), (MLA_Q_RANK, BF16), (MLA_KV_RANK, BF16)], [], name="mix_pre")
    c1 = _conv_fwd(u0, w_cdw, "conv31_fwd")
    (u,) = _rowwise(lambda c, b, g, bb: (_ln_silu(c, b, g, bb),), [c1], [b_cdw, ln_g, ln_b],
                    [(CONV_CH, BF16)], [], name="ln_silu")
    q0 = _mm(cqn, w_uq_p, name="mm_uq")
    kn0 = _mm(ckvn, w_uk_p, name="mm_uk")
    v0 = _mm(ckvn, w_uv_p, name="mm_uv")
    qk_rows = [q0, kn0, z_kr, cos, sin_a, sin_b]
    qh, kh = _rowwise(_f_qk_prep, qk_rows, [g_q, g_k],
                      [(MLA_HEADS * LANES, BF16)] * 2, [], name="qk_prep")
    attn, lse = _flash_fwd(qh, kh, v0, "flash_fwd")
    x1 = _mm(u, w_out_u, add=xs, name="mm_out_u")
    x1 = _mm(attn, w_out_a, add=x1, name="mm_out_a")

    (hq,) = _rowwise(_f_rms, [x1], [g_memx], [(D_MODEL, BF16)], [], name="rms_memx")
    (hm,) = _rowwise(_f_rms, [mems], [g_memm], [(D_MODEL, BF16)], [], name="rms_memm", tm=mems.shape[0])
    qm0 = _mm(hq, w_mq, name="mm_memq")
    kvm0 = _mm(hm, w_mkv, name="mm_memkv", tm=mems.shape[0])
    (km,) = _rowwise(_f_mem_k, [(kvm0, D_MODEL, 0, 0)], [g_mk], [(D_MODEL, BF16)], [],
                     name="mem_k", tm=mems.shape[0])
    vm = kvm0[:, D_MODEL:]
    (om,) = _rowwise(_f_mem_attn, [qm0], [km, vm, g_mq], [(D_MODEL, BF16)], [], name="mem_attn")
    x2 = _mm(om, w_mo, add=x1, name="mm_memo")

    (h3,) = _rowwise(_f_rms, [x2], [g_ffn], [(D_MODEL, BF16)], [], name="rms_ffn")
    up_g = _mm(h3, w_up_g, name="mm_up_g")
    up_v = _mm(h3, w_up_v, name="mm_up_v")
    cg = _conv_fwd(up_g, w_fdw_g, "conv3_g_fwd")
    cv = _conv_fwd(up_v, w_fdw_v, "conv3_v_fwd")
    (act,) = _rowwise(lambda g_, v_, bg, bv: (_act(g_, v_, bg, bv),), [cg, cv], [b_f_g, b_f_v],
                      [(D_FF, BF16)], [], name="ffn_act", tm=256)
    y = _mm(act, w_dn, add=x2, name="mm_down")
    dy, sq = _rowwise(_f_loss, [y, target], [], [(D_MODEL, F32)], [(1, D_MODEL)], name="loss")
    loss = lax.psum(0.5 * jnp.sum(sq) / D_MODEL, ("x", "y", "c"))

    gw, gs = {}, {}
    gw["w_down"] = _mm_tn(act, dy, name="tn_down")
    dact = _mm(dy, w_dn, trans_b=True, name="mm_down_t")
    dcg, dcv, db_g, db_v = _rowwise(_b_act, [cg, cv, dact], [b_f_g, b_f_v],
                                    [(D_FF, F32)] * 2, [(1, D_FF)] * 2, name="ffn_act_bwd", tm=256)
    gs["b_ffn_dw"] = jnp.concatenate([db_g, db_v], axis=1)
    dup_g, dwf_g = _conv_bwd(dcg, up_g, w_fdw_g, "conv3_g_bwd")
    dup_v, dwf_v = _conv_bwd(dcv, up_v, w_fdw_v, "conv3_v_bwd")
    gw["w_ffn_dw"] = jnp.concatenate([dwf_g, dwf_v], axis=1)
    gw["w_up"] = jnp.concatenate([_mm_tn(h3, dup_g, name="tn_up_g"), _mm_tn(h3, dup_v, name="tn_up_v")], axis=1)
    dh3 = _mm(dup_g, w_up_g, trans_b=True, name="mm_up_g_t")
    dh3 = _mm(dup_v, w_up_v, trans_b=True, add=dh3, name="mm_up_v_t")
    dx2, gs["ffn_norm_g"] = _rowwise(_b_rms, [x2, dh3, dy], [g_ffn], [(D_MODEL, F32)], [(1, D_MODEL)],
                                     name="rms_ffn_bwd")

    gw["w_mem_o"] = _mm_tn(om, dx2, name="tn_memo")
    dom = _mm(dx2, w_mo, trans_b=True, name="mm_memo_t")
    n_mem = mems.shape[0]
    dqm0, dkm, dvm, gs["mem_q_norm_g"] = _rowwise(
        _b_mem_attn, [dom, qm0], [km, vm, g_mq], [(D_MODEL, F32)],
        [(n_mem, D_MODEL), (n_mem, D_MODEL), (1, MEM_HEAD_DIM)], name="mem_attn_bwd")
    gw["w_mem_q"] = _mm_tn(hq, dqm0, name="tn_memq")
    dhq = _mm(dqm0, w_mq, trans_b=True, name="mm_memq_t")
    dx1, gs["mem_norm_x_g"] = _rowwise(_b_rms, [x1, dhq, dx2], [g_memx], [(D_MODEL, F32)],
                                       [(1, D_MODEL)], name="rms_memx_bwd")
    dkk, gs["mem_k_norm_g"] = _rowwise(_b_mem_k, [(kvm0, D_MODEL, 0, 0), dkm], [g_mk],
                                       [(D_MODEL, F32)], [(1, MEM_HEAD_DIM)], name="mem_k_bwd", tm=n_mem)
    dkvm0 = jnp.concatenate([dkk, dvm], axis=1)
    gw["w_mem_kv"] = _mm_tn(hm, dkvm0, name="tn_memkv", ts=n_mem)
    dhm = _mm(dkvm0, w_mkv, trans_b=True, name="mm_memkv_t", tm=n_mem)
    _, gs["mem_norm_m_g"] = _rowwise(_b_rms_nores, [mems, dhm], [g_memm], [(D_MODEL, F32)],
                                     [(1, D_MODEL)], name="rms_memm_bwd", tm=n_mem)

    gw_out_u = _mm_tn(u, dx1, name="tn_out_u")
    gw_out_a = _mm_tn(attn, dx1, name="tn_out_a")
    gw["w_out"] = jnp.concatenate([gw_out_u, _unpad_heads(gw_out_a, MLA_V, 0)], axis=0)
    du = _mm(dx1, w_out_u, trans_b=True, name="mm_out_u_t")
    dattn = _mm(dx1, w_out_a, trans_b=True, name="mm_out_a_t")
    dc1, gs["b_conv_dw"], gs["conv_ln_g"], gs["conv_ln_b"] = _rowwise(
        _b_ln_silu, [c1, du], [b_cdw, ln_g, ln_b], [(CONV_CH, F32)], [(1, CONV_CH)] * 3, name="ln_silu_bwd")
    du0, gw["w_conv_dw"] = _conv_bwd(dc1, u0, w_cdw, "conv31_bwd")
    dqh = _flash_bwd_dq(qh, kh, v0, dattn, attn, lse, "flash_bwd_dq")
    dkh, dv0 = _flash_bwd_dkv(qh, kh, v0, dattn, attn, lse, "flash_bwd_dkv")
    dq0, dkn0, dkr, dgq, dgk = _rowwise(
        _b_qk_prep, qk_rows + [dqh, dkh], [g_q, g_k],
        [(MLA_HEADS * LANES, F32)] * 2 + [(LANES, F32)], [(1, LANES)] * 2, name="qk_prep_bwd")
    gs["q_norm_g"], gs["k_norm_g"] = dgq[:, :MLA_QK], dgk[:, :MLA_QK]
    gw["w_uq"] = _unpad_heads(_mm_tn(cqn, dq0, name="tn_uq"), MLA_QK, 1)
    g_uk = _unpad_heads(_mm_tn(ckvn, dkn0, name="tn_uk"), MLA_NOPE, 1).reshape(MLA_KV_RANK, MLA_HEADS, MLA_NOPE)
    g_uv = _unpad_heads(_mm_tn(ckvn, dv0, name="tn_uv"), MLA_V, 1).reshape(MLA_KV_RANK, MLA_HEADS, MLA_V)
    gw["w_ukv"] = jnp.concatenate([g_uk, g_uv], axis=2).reshape(MLA_KV_RANK, -1)
    dcqn = _mm(dq0, w_uq_p, trans_b=True, name="mm_uq_t")
    dckvn = _mm(dkn0, w_uk_p, trans_b=True, name="mm_uk_t")
    dckvn = _mm(dv0, w_uv_p, trans_b=True, add=dckvn, name="mm_uv_t")
    dz, dba, dbg, gs["q_lat_norm_g"], gs["kv_lat_norm_g"] = _rowwise(
        _b_mix_pre, z_rows + [du0, dcqn, dckvn, dkr], [b_in_a, b_in_g, g_qlat, g_kvlat],
        [(IN_PAD, F32)], [(1, CONV_CH)] * 2 + [(1, MLA_Q_RANK), (1, MLA_KV_RANK)], name="mix_pre_bwd")
    gs["b_conv_in"] = jnp.concatenate([dba, dbg], axis=1)
    gw_in = _mm_tn(h1, dz, name="tn_in")
    gw["w_in"] = jnp.concatenate([gw_in[:, :s3], gw_in[:, s3 + MLA_NOPE:s3 + MLA_QK]], axis=1)
    dh1 = _mm(dz, w_in_p, trans_b=True, name="mm_in_t")
    dx, gs["mix_norm_g"] = _rowwise(_b_rms, [xs, dh1, dx1], [g_mix], [(D_MODEL, F32)], [(1, D_MODEL)],
                                    name="rms_mix_bwd")

    shards = _full_to_shards(gw).reshape(N_DEV, BIG_ROWS, LANES)
    rel = jnp.take(shards, me ^ jnp.arange(N_DEV, dtype=jnp.int32), axis=0)
    rel = rel.reshape(4, 2, BIG_ROWS, LANES)
    keep, give = rel[:, 0], rel[:, 1]
    got = _swap_sibling(give, "rs_sibling")
    (chip_sum,) = _rowwise(lambda p, q_: (p + q_,),
                           [keep.reshape(4 * BIG_ROWS, LANES), got.reshape(4 * BIG_ROWS, LANES)], [],
                           [(LANES, F32)], [], name="rs_add", tm=1024)
    chip_sum = chip_sum.reshape(4, BIG_ROWS, LANES)
    others = _swap_chips(chip_sum, "rs_chips").reshape(3 * BIG_ROWS, LANES)
    nb = BIG_ROWS // 1024
    big_rows = [(chip_sum.reshape(4 * BIG_ROWS, LANES), LANES, 0, 0)]
    big_rows += [(others, LANES, 0, k * nb) for k in range(3)]
    big_rows += [_pack_shards({n: a[p + n] for n, _, _ in BIG}) for p in ("", "m_", "v_")]
    g_big, d_big, m_big, v_big = [_unpack_shards(t) for t in _rowwise(
        _f_adam_big, big_rows, [], [(LANES, F32)] * 4, [], name="adamw_big", tm=1024, n_rows=BIG_ROWS)]

    parts = _all_gather(_pack_small(gs), "ag_small_grads")

    def small_body(p_ref, w_ref, m_ref, v_ref, g_ref, d_ref, mo_ref, vo_ref):
        outs = _f_adam_small([p_ref[k] for k in range(N_DEV)], w_ref[...], m_ref[...], v_ref[...])
        for r, val in zip((g_ref, d_ref, mo_ref, vo_ref), outs):
            r[...] = val

    small_in = [_pack_small({n: a[p + n] for n, _ in SMALL}) for p in ("", "m_", "v_")]
    g_sm, d_sm, m_sm, v_sm = [_unpack_small(t) for t in _call(
        small_body, name="adamw_small",
        out_shape=[jax.ShapeDtypeStruct((SMALL_ROWS, LANES), F32)] * 4)(parts, *small_in)]

    def pick(big, small):
        return [big[n] if n in big else small[n] for n in WEIGHTS]

    return (loss, dx.reshape(1, seq, D_MODEL), *pick(g_big, g_sm), *pick(d_big, d_sm),
            *pick(m_big, m_sm), *pick(v_big, v_sm))
```

```python
import functools
import math

import jax
import jax.numpy as jnp
from jax import lax
from jax.experimental import pallas as pl
from jax.experimental.pallas import tpu as pltpu

F32 = jnp.float32
BF16 = jnp.bfloat16
EPS = 1e-6
LANES = 128
N_DEV = 8
D_MODEL = 1024
CONV_CH = 512
CONV_WIDTH = 31
MLA_HEADS = 8
MLA_NOPE = 64
MLA_ROPE = 32
MLA_V = 64
MLA_QK = MLA_NOPE + MLA_ROPE
MLA_Q_RANK = 256
MLA_KV_RANK = 128
ROPE_THETA = 10000.0
IN_COLS = 2 * CONV_CH + MLA_Q_RANK + MLA_KV_RANK + MLA_ROPE
IN_PAD = 2 * CONV_CH + MLA_Q_RANK + MLA_KV_RANK + LANES
MEM_HEADS = 4
MEM_HEAD_DIM = 256
D_FF = 2816
FFN_CONV_WIDTH = 3
CHUNK = 64
ATT_SCALE = 1.0 / math.sqrt(MLA_QK)
MEM_SCALE = 1.0 / math.sqrt(MEM_HEAD_DIM)
ADAM_LR, ADAM_B1, ADAM_B2, ADAM_EPS, ADAM_WD, ADAM_STEP = 0.001, 0.9, 0.999, 1e-08, 0.01, 10

TM = 512
TQ = 512
HEADS_PER_STEP = 2
CONV_ROWS = 256
NEG = -1e30
VMEM_LIMIT = 56 * 1024 * 1024

MESH = pl.DeviceIdType.MESH
ANY = pl.BlockSpec(memory_space=pl.ANY)
NT_DIMS = (((1,), (1,)), ((), ()))

BIG = [
    ("w_in", (1024, 1440), 1), ("w_conv_dw", (31, 512), 1), ("w_uq", (256, 768), 1),
    ("w_ukv", (128, 1024), 1), ("w_out", (1024, 1024), 0), ("w_mem_q", (1024, 1024), 0),
    ("w_mem_kv", (1024, 2048), 1), ("w_mem_o", (1024, 1024), 0), ("w_up", (1024, 5632), 1),
    ("w_ffn_dw", (3, 5632), 1), ("w_down", (2816, 1024), 0),
]
SMALL = [
    ("mix_norm_g", 1024), ("b_conv_in", 1024), ("b_conv_dw", 512), ("conv_ln_g", 512),
    ("conv_ln_b", 512), ("q_lat_norm_g", 256), ("kv_lat_norm_g", 128), ("q_norm_g", 96),
    ("k_norm_g", 96), ("mem_norm_x_g", 1024), ("mem_norm_m_g", 1024), ("mem_q_norm_g", 256),
    ("mem_k_norm_g", 256), ("ffn_norm_g", 1024), ("b_ffn_dw", 5632),
]
WEIGHTS = [
    "mix_norm_g", "w_in", "b_conv_in", "w_conv_dw", "b_conv_dw", "conv_ln_g", "conv_ln_b",
    "q_lat_norm_g", "w_uq", "kv_lat_norm_g", "w_ukv", "q_norm_g", "k_norm_g", "w_out",
    "mem_norm_x_g", "mem_norm_m_g", "w_mem_q", "w_mem_kv", "mem_q_norm_g", "mem_k_norm_g",
    "w_mem_o", "ffn_norm_g", "w_up", "w_ffn_dw", "b_ffn_dw", "w_down",
]


def _shard_shape(shape, axis):
    s = list(shape)
    s[axis] //= N_DEV
    return tuple(s)


BIG_ROWS = sum(math.prod(_shard_shape(s, a)) for _, s, a in BIG) // LANES
SMALL_PAD = [(-(-n // LANES)) * LANES for _, n in SMALL]
SMALL_ROWS = -(-sum(SMALL_PAD) // (8 * LANES)) * 8


def _call(body, **kw):
    return pl.pallas_call(body, **kw)


def _params(*sem):
    return pltpu.CompilerParams(dimension_semantics=sem, vmem_limit_bytes=VMEM_LIMIT)


def _all_gather(xs, name):
    rows_per, cols = xs.shape

    def body(x_ref, out_ref, send_sems, recv_sems, local_sem):
        x, y, c = lax.axis_index("x"), lax.axis_index("y"), lax.axis_index("c")
        me, sibling = (x, y, c), (x, y, 1 - c)
        chips = [(1 - x, y), (x, 1 - y), (1 - x, 1 - y)]

        def rows(px, py, pc):
            return out_ref.at[4 * px + 2 * py + pc]

        def copy(k, block, to, src=None):
            return pltpu.make_async_remote_copy(
                src_ref=rows(*block) if src is None else src, dst_ref=rows(*block),
                send_sem=send_sems.at[k], recv_sem=recv_sems.at[k],
                device_id=to, device_id_type=MESH)

        mine = pltpu.make_async_copy(x_ref, rows(*me), local_sem)
        mine.start()
        first = [copy(0, me, sibling, src=x_ref)]
        first += [copy(1 + j, me, (*chip, c), src=x_ref) for j, chip in enumerate(chips)]
        for cp in first:
            cp.start()
        passed = [copy(4 + j, (*chip, c), sibling) for j, chip in enumerate(chips)]
        for j, chip in enumerate(chips):
            copy(1 + j, (*chip, c), me).wait_recv()
            passed[j].start()
        copy(0, sibling, me).wait_recv()
        for j, chip in enumerate(chips):
            copy(4 + j, (*chip, 1 - c), me).wait_recv()
        for cp in first + passed:
            cp.wait_send()
        mine.wait()

    return _call(
        body, name=name, out_shape=jax.ShapeDtypeStruct((N_DEV, rows_per, cols), xs.dtype),
        in_specs=[ANY], out_specs=ANY,
        scratch_shapes=[pltpu.SemaphoreType.DMA((7,)), pltpu.SemaphoreType.DMA((7,)),
                        pltpu.SemaphoreType.DMA(())],
    )(xs)


def _swap_sibling(shards, name):
    n_chip = shards.shape[0]

    def body(s_ref, keep_ref, got_ref, send_sems, recv_sems, local_sems):
        x, y, c = lax.axis_index("x"), lax.axis_index("y"), lax.axis_index("c")
        copies = []
        for chip in range(n_chip):
            copies.append(pltpu.make_async_copy(s_ref.at[chip, c], keep_ref.at[chip], local_sems.at[chip]))
            copies.append(pltpu.make_async_remote_copy(
                src_ref=s_ref.at[chip, 1 - c], dst_ref=got_ref.at[chip], send_sem=send_sems.at[chip],
                recv_sem=recv_sems.at[chip], device_id=(x, y, 1 - c), device_id_type=MESH))
        for cp in copies:
            cp.start()
        for cp in copies:
            cp.wait()

    out = jax.ShapeDtypeStruct((n_chip,) + shards.shape[2:], shards.dtype)
    return _call(
        body, name=name, out_shape=[out, out], in_specs=[ANY], out_specs=[ANY, ANY],
        scratch_shapes=[pltpu.SemaphoreType.DMA((n_chip,))] * 3,
    )(shards)


def _swap_chips(a, name):
    def body(a_ref, r_ref, send_sems, recv_sems, local_sem):
        x, y, c = lax.axis_index("x"), lax.axis_index("y"), lax.axis_index("c")
        peers = [(x, 1 - y), (1 - x, y), (1 - x, 1 - y)]
        copies = [pltpu.make_async_copy(a_ref.at[2 * x + y], r_ref.at[0], local_sem)]
        copies += [
            pltpu.make_async_remote_copy(
                src_ref=a_ref.at[2 * px + py], dst_ref=r_ref.at[k + 1], send_sem=send_sems.at[k],
                recv_sem=recv_sems.at[k], device_id=(px, py, c), device_id_type=MESH)
            for k, (px, py) in enumerate(peers)]
        for cp in copies:
            cp.start()
        for cp in copies:
            cp.wait()

    return _call(
        body, name=name, out_shape=jax.ShapeDtypeStruct(a.shape, a.dtype),
        in_specs=[ANY], out_specs=ANY,
        scratch_shapes=[pltpu.SemaphoreType.DMA((3,)), pltpu.SemaphoreType.DMA((3,)),
                        pltpu.SemaphoreType.DMA(())],
    )(a)


def _tile(n, prefs):
    for t in prefs:
        if n % t == 0:
            return t
    return n


def _mm(a, b, *, name, trans_b=False, add=None, out_dtype=F32, tm=TM):
    m, k = a.shape
    n = b.shape[0] if trans_b else b.shape[1]
    tn = _tile(n, (1536, 1408, 1024, 768, 512, 256, 128))
    tk = _tile(k, (1408, 1024, 768, 512, 256, 128))
    nk = k // tk
    has_add = add is not None

    def body(*refs):
        a_ref, b_ref = refs[0], refs[1]
        add_ref = refs[2] if has_add else None
        o_ref = refs[2 + has_add]
        av = a_ref[...].astype(BF16)
        bv = b_ref[...].astype(BF16)
        if trans_b:
            part = lax.dot_general(av, bv, NT_DIMS, preferred_element_type=F32)
        else:
            part = jnp.dot(av, bv, preferred_element_type=F32)

        def finish(acc):
            if has_add:
                acc = acc + add_ref[...].astype(F32)
            o_ref[...] = acc.astype(o_ref.dtype)

        if nk == 1:
            finish(part)
        else:
            acc_ref = refs[3 + has_add]
            kk = pl.program_id(2)

            @pl.when(kk == 0)
            def _():
                acc_ref[...] = part

            @pl.when(kk > 0)
            def _():
                acc_ref[...] += part

            @pl.when(kk == nk - 1)
            def _():
                finish(acc_ref[...])

    in_specs = [pl.BlockSpec((tm, tk), lambda i, j, kk: (i, kk))]
    if trans_b:
        in_specs.append(pl.BlockSpec((tn, tk), lambda i, j, kk: (j, kk)))
    else:
        in_specs.append(pl.BlockSpec((tk, tn), lambda i, j, kk: (kk, j)))
    args = [a, b]
    if has_add:
        in_specs.append(pl.BlockSpec((tm, tn), lambda i, j, kk: (i, j)))
        args.append(add)
    return _call(
        body, name=name, grid=(m // tm, n // tn, nk), in_specs=in_specs,
        out_specs=pl.BlockSpec((tm, tn), lambda i, j, kk: (i, j)),
        out_shape=jax.ShapeDtypeStruct((m, n), out_dtype),
        scratch_shapes=[pltpu.VMEM((tm, tn), F32)] if nk > 1 else [],
        compiler_params=_params("parallel", "parallel", "arbitrary"),
    )(*args)


def _mm_tn(a, b, *, name, ts=TM):
    s, m = a.shape
    n = b.shape[1]
    tm = _tile(m, (1408, 1024, 768, 512, 256, 128))
    tn = _tile(n, (1536, 1408, 1024, 768, 512, 256, 128))

    def body(a_ref, b_ref, o_ref):
        kk = pl.program_id(2)
        part = jnp.dot(a_ref[...].astype(BF16).T, b_ref[...].astype(BF16),
                       preferred_element_type=F32)

        @pl.when(kk == 0)
        def _():
            o_ref[...] = part

        @pl.when(kk > 0)
        def _():
            o_ref[...] += part

    return _call(
        body, name=name, grid=(m // tm, n // tn, s // ts),
        in_specs=[pl.BlockSpec((ts, tm), lambda i, j, kk: (kk, i)),
                  pl.BlockSpec((ts, tn), lambda i, j, kk: (kk, j))],
        out_specs=pl.BlockSpec((tm, tn), lambda i, j, kk: (i, j)),
        out_shape=jax.ShapeDtypeStruct((m, n), F32),
        compiler_params=_params("parallel", "parallel", "arbitrary"),
    )(a, b)


def _rowwise(fn, rows, consts, row_outs, acc_outs, *, name, tm=TM, n_rows=None):
    rows = [r if isinstance(r, tuple) else (r, r.shape[1], 0, 0) for r in rows]
    s = n_rows or rows[0][0].shape[0]
    nr, nc, no, na = len(rows), len(consts), len(row_outs), len(acc_outs)

    def body(*refs):
        r_in, c_in = refs[:nr], refs[nr:nr + nc]
        o_refs, a_refs = refs[nr + nc:nr + nc + no], refs[nr + nc + no:]
        outs = fn(*[r[...] for r in r_in], *[c[...] for c in c_in])
        for r, v in zip(o_refs, outs[:no]):
            r[...] = v.astype(r.dtype)
        if na:
            i = pl.program_id(0)

            @pl.when(i == 0)
            def _():
                for r, v in zip(a_refs, outs[no:]):
                    r[...] = v.astype(F32)

            @pl.when(i > 0)
            def _():
                for r, v in zip(a_refs, outs[no:]):
                    r[...] += v.astype(F32)

    in_specs = [pl.BlockSpec((tm, w), lambda i, cb=cb, rb=rb: (i + rb, cb)) for _, w, cb, rb in rows]
    in_specs += [pl.BlockSpec(c.shape, lambda i: (0, 0)) for c in consts]
    out_specs = [pl.BlockSpec((tm, w), lambda i: (i, 0)) for w, _ in row_outs]
    out_specs += [pl.BlockSpec(sh, lambda i: (0, 0)) for sh in acc_outs]
    out_shape = [jax.ShapeDtypeStruct((s, w), dt) for w, dt in row_outs]
    out_shape += [jax.ShapeDtypeStruct(sh, F32) for sh in acc_outs]
    return _call(
        body, name=name, grid=(s // tm,), in_specs=in_specs, out_specs=out_specs,
        out_shape=out_shape, compiler_params=_params("arbitrary"),
    )(*[r[0] for r in rows], *consts)


def _rms(x, g, n=None):
    ms = jnp.sum(x * x, axis=-1, keepdims=True) / float(n or x.shape[-1])
    return x * lax.rsqrt(ms + EPS) * g


def _layer_norm(x, g, b):
    mu = jnp.sum(x, axis=-1, keepdims=True) / float(x.shape[-1])
    xc = x - mu
    var = jnp.sum(xc * xc, axis=-1, keepdims=True) / float(x.shape[-1])
    return xc * lax.rsqrt(var + EPS) * g + b


def _silu(x):
    return x * jax.nn.sigmoid(x)


@jax.custom_vjp
def _rope(y, cos, sin_a, sin_b):
    return y * cos + pltpu.roll(y, 112, 1) * sin_a + pltpu.roll(y, 16, 1) * sin_b


def _rope_fwd(y, cos, sin_a, sin_b):
    return _rope(y, cos, sin_a, sin_b), (cos, sin_a, sin_b)


def _rope_bwd(res, ct):
    cos, sin_a, sin_b = res
    dy = ct * cos + pltpu.roll(ct * sin_a, 16, 1) + pltpu.roll(ct * sin_b, 112, 1)
    return dy, jnp.zeros_like(cos), jnp.zeros_like(sin_a), jnp.zeros_like(sin_b)


_rope.defvjp(_rope_fwd, _rope_bwd)


def _qk_head(xh, g, cos, sin_a, sin_b):
    return _rope(_rms(xh, g, MLA_QK), cos, sin_a, sin_b)


def _heads(x, width):
    return [x[:, h * width:(h + 1) * width] for h in range(x.shape[1] // width)]


def _f_rms(x, g):
    return (_rms(x, g),)


def _f_rope_tab(pos, inv_freq):
    ang = pos.astype(F32) * inv_freq
    lane = lax.broadcasted_iota(jnp.int32, ang.shape, 1)
    sn = jnp.sin(ang)
    first = (lane >= MLA_NOPE) & (lane < MLA_NOPE + MLA_ROPE // 2)
    second = (lane >= MLA_NOPE + MLA_ROPE // 2) & (lane < MLA_QK)
    return jnp.cos(ang), jnp.where(first, -sn, 0.0), jnp.where(second, sn, 0.0)


def _mix_pre(za, zg, zcq, zckv, ba, bg, gq, gkv):
    u0 = (za + ba) * jax.nn.sigmoid(zg + bg)
    return u0, _rms(zcq, gq), _rms(zckv, gkv)


def _ln_silu(c1, bdw, lg, lb):
    return _silu(_layer_norm(c1 + bdw, lg, lb))


def _f_qk_prep(q0, kn, kr, cos, sa, sb, gq, gk):
    qs = [_qk_head(xh, gq, cos, sa, sb) * ATT_SCALE for xh in _heads(q0, LANES)]
    ks = [_qk_head(xh + kr, gk, cos, sa, sb) for xh in _heads(kn, LANES)]
    return jnp.concatenate(qs, axis=1), jnp.concatenate(ks, axis=1)


def _act(cg, cv, bg, bv):
    return _silu(cg + bg) * (cv + bv)


def _f_mem_k(kk, g):
    return (jnp.concatenate([_rms(xh, g) for xh in _heads(kk, MEM_HEAD_DIM)], axis=1),)


def _mem_probs(qn, kmh):
    s = lax.dot_general(qn.astype(BF16), kmh, NT_DIMS, preferred_element_type=F32) * MEM_SCALE
    e = jnp.exp(s - jnp.max(s, axis=-1, keepdims=True))
    return e / jnp.sum(e, axis=-1, keepdims=True)


def _f_mem_attn(qm0, km, vm, g):
    outs = []
    for h, xh in enumerate(_heads(qm0, MEM_HEAD_DIM)):
        sl = slice(h * MEM_HEAD_DIM, (h + 1) * MEM_HEAD_DIM)
        p = _mem_probs(_rms(xh, g), km[:, sl])
        outs.append(jnp.dot(p.astype(BF16), vm[:, sl].astype(BF16), preferred_element_type=F32))
    return (jnp.concatenate(outs, axis=1),)


def _f_loss(y, t):
    e = y - t
    return e * (1.0 / D_MODEL), jnp.sum(e * e, axis=0, keepdims=True)


def _b_rms(x, dh, dres, g):
    _, vjp = jax.vjp(_rms, x, g)
    dx, dg = vjp(dh)
    return dx + dres, dg


def _b_rms_nores(x, dh, g):
    _, vjp = jax.vjp(_rms, x, g)
    dx, dg = vjp(dh)
    return dx, dg


def _b_mix_pre(za, zg, zcq, zckv, du0, dcqn, dckvn, dkr, ba, bg, gq, gkv):
    _, vjp = jax.vjp(_mix_pre, za, zg, zcq, zckv, ba, bg, gq, gkv)
    dza, dzg, dzcq, dzckv, dba, dbg, dgq, dgkv = vjp((du0, dcqn, dckvn))
    return jnp.concatenate([dza, dzg, dzcq, dzckv, dkr], axis=1), dba, dbg, dgq, dgkv


def _b_ln_silu(c1, du, bdw, lg, lb):
    _, vjp = jax.vjp(_ln_silu, c1, bdw, lg, lb)
    return vjp(du)


def _b_qk_prep(q0, kn, kr, cos, sa, sb, dq, dk, gq, gk):
    head = lambda xh, g: _qk_head(xh, g, cos, sa, sb)
    dq = dq * ATT_SCALE
    dq0, dkn = [], []
    dkr = jnp.zeros_like(kr)
    dgq = jnp.zeros_like(gq)
    dgk = jnp.zeros_like(gk)
    for xh, ct in zip(_heads(q0, LANES), _heads(dq, LANES)):
        _, vjp = jax.vjp(head, xh, gq)
        dx, dg = vjp(ct)
        dq0.append(dx)
        dgq = dgq + dg
    for xh, ct in zip(_heads(kn, LANES), _heads(dk, LANES)):
        _, vjp = jax.vjp(head, xh + kr, gk)
        dx, dg = vjp(ct)
        dkn.append(dx)
        dkr = dkr + dx
        dgk = dgk + dg
    return jnp.concatenate(dq0, axis=1), jnp.concatenate(dkn, axis=1), dkr, dgq, dgk


def _b_act(cg, cv, dact, bg, bv):
    _, vjp = jax.vjp(_act, cg, cv, bg, bv)
    return vjp(dact)


def _b_mem_k(kk, dkm, g):
    dkk = []
    dg = jnp.zeros_like(g)
    for xh, ct in zip(_heads(kk, MEM_HEAD_DIM), _heads(dkm, MEM_HEAD_DIM)):
        _, vjp = jax.vjp(_rms, xh, g)
        dx, dgh = vjp(ct)
        dkk.append(dx)
        dg = dg + dgh
    return jnp.concatenate(dkk, axis=1), dg


def _b_mem_attn(dom, qm0, km, vm, g):
    dq0, dkm, dvm = [], [], []
    dg = jnp.zeros_like(g)
    for h, (xh, doh) in enumerate(zip(_heads(qm0, MEM_HEAD_DIM), _heads(dom, MEM_HEAD_DIM))):
        sl = slice(h * MEM_HEAD_DIM, (h + 1) * MEM_HEAD_DIM)
        kmh, vmh = km[:, sl], vm[:, sl].astype(BF16)
        qn, vjp = jax.vjp(_rms, xh, g)
        p = _mem_probs(qn, kmh)
        dob = doh.astype(BF16)
        dp = lax.dot_general(dob, vmh, NT_DIMS, preferred_element_type=F32)
        ds = (p * (dp - jnp.sum(dp * p, axis=-1, keepdims=True)) * MEM_SCALE).astype(BF16)
        dqn = jnp.dot(ds, kmh, preferred_element_type=F32)
        dkm.append(jnp.dot(ds.T, qn.astype(BF16), preferred_element_type=F32))
        dvm.append(jnp.dot(p.astype(BF16).T, dob, preferred_element_type=F32))
        dx, dgh = vjp(dqn)
        dq0.append(dx)
        dg = dg + dgh
    return (jnp.concatenate(dq0, axis=1), jnp.concatenate(dkm, axis=1),
            jnp.concatenate(dvm, axis=1), dg)


def _adamw(w, g, m, v):
    m = ADAM_B1 * m + (1.0 - ADAM_B1) * g
    v = ADAM_B2 * v + (1.0 - ADAM_B2) * jnp.square(g)
    m_hat = m / (1.0 - ADAM_B1 ** ADAM_STEP)
    v_hat = v / (1.0 - ADAM_B2 ** ADAM_STEP)
    delta = -ADAM_LR * (m_hat / (jnp.sqrt(v_hat) + ADAM_EPS) + ADAM_WD * w)
    return delta, m, v


def _f_adam_big(a0, r1, r2, r3, w, m, v):
    g = ((a0 + r1) + r2) + r3
    return (g,) + _adamw(w, g, m, v)


def _f_adam_small(parts, w, m, v):
    g = parts[0]
    for k in range(1, N_DEV):
        g = g + parts[k]
    return (g,) + _adamw(w, g, m, v)


def _conv_fwd(x, w, name):
    s, ch = x.shape
    kw = w.shape[0]
    halo = -(-(kw - 1) // 8) * 8
    r = CONV_ROWS
    n = s // r

    def chunk(window, wv):
        acc = jnp.zeros((r, LANES), F32)
        for k in range(kw):
            shift = kw - 1 - k
            sh = window if shift == 0 else pltpu.roll(window, shift, 0)
            acc = acc + sh[halo:halo + r] * wv[k:k + 1]
        return acc

    def body(x_ref, w_ref, y_ref):
        wv = w_ref[...]
        first = jnp.concatenate([jnp.zeros((halo, LANES), F32), x_ref[0:r]], axis=0)
        y_ref[0:r] = chunk(first, wv)

        def step(i, carry):
            base = pl.multiple_of(i * r, 8)
            y_ref[pl.ds(base, r)] = chunk(x_ref[pl.ds(base - halo, r + halo)], wv)
            return carry

        lax.fori_loop(1, n, step, 0)

    return _call(
        body, name=name, grid=(ch // LANES,),
        in_specs=[pl.BlockSpec((s, LANES), lambda c: (0, c)), pl.BlockSpec((kw, LANES), lambda c: (0, c))],
        out_specs=pl.BlockSpec((s, LANES), lambda c: (0, c)),
        out_shape=jax.ShapeDtypeStruct((s, ch), F32), compiler_params=_params("parallel"),
    )(x, w)


def _conv_bwd(dy, x, w, name):
    s, ch = x.shape
    kw = w.shape[0]
    halo = -(-(kw - 1) // 8) * 8
    r = CONV_ROWS
    n = s // r

    def dx_chunk(window, wv):
        acc = jnp.zeros((r, LANES), F32)
        for k in range(kw):
            shift = kw - 1 - k
            sh = window if shift == 0 else pltpu.roll(window, r + halo - shift, 0)
            acc = acc + sh[0:r] * wv[k:k + 1]
        return acc

    def dw_chunk(xwin, dyc, acc_ref):
        for k in range(kw):
            shift = kw - 1 - k
            sh = xwin if shift == 0 else pltpu.roll(xwin, shift, 0)
            prod = sh[halo:halo + r] * dyc
            acc_ref[k] += jnp.sum(prod.reshape(r // 8, 8, LANES), axis=0)

    def body(dy_ref, x_ref, w_ref, dx_ref, dw_ref, acc_ref):
        wv = w_ref[...]
        acc_ref[...] = jnp.zeros_like(acc_ref)
        xfirst = jnp.concatenate([jnp.zeros((halo, LANES), F32), x_ref[0:r]], axis=0)
        dw_chunk(xfirst, dy_ref[0:r], acc_ref)
        last = jnp.concatenate([dy_ref[s - r:s], jnp.zeros((halo, LANES), F32)], axis=0)
        dx_ref[s - r:s] = dx_chunk(last, wv)

        def step(i, carry):
            base = pl.multiple_of(i * r, 8)
            dw_chunk(x_ref[pl.ds(base - halo, r + halo)], dy_ref[pl.ds(base, r)], acc_ref)
            prev = pl.multiple_of((i - 1) * r, 8)
            dx_ref[pl.ds(prev, r)] = dx_chunk(dy_ref[pl.ds(prev, r + halo)], wv)
            return carry

        lax.fori_loop(1, n, step, 0)
        dw_ref[...] = jnp.sum(acc_ref[...], axis=1)

    spec = pl.BlockSpec((s, LANES), lambda c: (0, c))
    wspec = pl.BlockSpec((kw, LANES), lambda c: (0, c))
    return _call(
        body, name=name, grid=(ch // LANES,), in_specs=[spec, spec, wspec], out_specs=[spec, wspec],
        out_shape=[jax.ShapeDtypeStruct((s, ch), F32), jax.ShapeDtypeStruct((kw, ch), F32)],
        scratch_shapes=[pltpu.VMEM((kw, 8, LANES), F32)], compiler_params=_params("parallel"),
    )(dy, x, w)


def _chunk_mask(rows_are_queries):
    a = lax.broadcasted_iota(jnp.int32, (TQ, TQ), 0) // CHUNK
    b = lax.broadcasted_iota(jnp.int32, (TQ, TQ), 1) // CHUNK
    return (b <= a) if rows_are_queries else (a <= b)


def _head_lanes(hh):
    return slice(hh * LANES, (hh + 1) * LANES)


def _to_row(col):
    return jnp.broadcast_to(col, (TQ, LANES)).T[0:1, :]


def _flash_specs(s):
    width = HEADS_PER_STEP * LANES
    tile = pl.BlockSpec((TQ, width), lambda h, i: (i, h))
    whole = pl.BlockSpec((s, width), lambda h, i: (0, h))
    row_tile = pl.BlockSpec((HEADS_PER_STEP, 1, 1, TQ), lambda h, i: (h, i, 0, 0))
    row_whole = pl.BlockSpec((HEADS_PER_STEP, s // TQ, 1, TQ), lambda h, i: (h, 0, 0, 0))
    return tile, whole, row_tile, row_whole


def _flash_fwd(q, k, v, name):
    s = q.shape[0]
    nq = s // TQ

    def body(q_ref, k_ref, v_ref, o_ref, lse_ref, lse_row_ref):
        i = pl.program_id(1)
        qs = [q_ref[:, _head_lanes(hh)] for hh in range(HEADS_PER_STEP)]

        def step(j, carry, masked):
            base = pl.multiple_of(j * TQ, TQ)
            out = []
            for hh in range(HEADS_PER_STEP):
                m_prev, l_prev, acc = carry[hh]
                kj = k_ref[pl.ds(base, TQ), _head_lanes(hh)]
                vj = v_ref[pl.ds(base, TQ), _head_lanes(hh)]
                sc = lax.dot_general(qs[hh], kj, NT_DIMS, preferred_element_type=F32)
                if masked:
                    sc = jnp.where(_chunk_mask(True), sc, NEG)
                m_new = jnp.maximum(m_prev, jnp.max(sc, axis=-1, keepdims=True))
                alpha = jnp.exp(m_prev - m_new)
                p = jnp.exp(sc - m_new)
                l_new = alpha * l_prev + jnp.sum(p, axis=-1, keepdims=True)
                acc = acc * alpha + jnp.dot(p.astype(BF16), vj, preferred_element_type=F32)
                out.append((m_new, l_new, acc))
            return tuple(out)

        init = tuple((jnp.full((TQ, 1), NEG, F32), jnp.zeros((TQ, 1), F32), jnp.zeros((TQ, LANES), F32))
                     for _ in range(HEADS_PER_STEP))
        carry = lax.fori_loop(0, i, functools.partial(step, masked=False), init)
        for hh, (m_fin, l_fin, acc) in enumerate(step(i, carry, True)):
            o_ref[:, _head_lanes(hh)] = acc / l_fin
            lse = m_fin + jnp.log(l_fin)
            lse_ref[:, _head_lanes(hh)] = jnp.broadcast_to(lse, (TQ, LANES))
            lse_row_ref[hh, 0] = _to_row(lse)

    tile, whole, row_tile, _ = _flash_specs(s)
    wide = jax.ShapeDtypeStruct((s, MLA_HEADS * LANES), F32)
    return _call(
        body, name=name, grid=(MLA_HEADS // HEADS_PER_STEP, nq), in_specs=[tile, whole, whole],
        out_specs=[tile, tile, row_tile],
        out_shape=[wide, wide, jax.ShapeDtypeStruct((MLA_HEADS, nq, 1, TQ), F32)],
        compiler_params=_params("parallel", "arbitrary"),
    )(q, k, v)


def _flash_bwd_dq(q, k, v, do, o, lse, name):
    s = q.shape[0]
    nq = s // TQ

    def body(q_ref, k_ref, v_ref, do_ref, o_ref, lse_ref, dq_ref, delta_row_ref):
        i = pl.program_id(1)
        qs, dobs, deltas, lses = [], [], [], []
        for hh in range(HEADS_PER_STEP):
            dov = do_ref[:, _head_lanes(hh)]
            qs.append(q_ref[:, _head_lanes(hh)])
            dobs.append(dov.astype(BF16))
            deltas.append(jnp.sum(dov * o_ref[:, _head_lanes(hh)], axis=-1, keepdims=True))
            lses.append(lse_ref[:, _head_lanes(hh)][:, 0:1])

        def step(j, carry, masked):
            base = pl.multiple_of(j * TQ, TQ)
            out = []
            for hh in range(HEADS_PER_STEP):
                kj = k_ref[pl.ds(base, TQ), _head_lanes(hh)]
                vj = v_ref[pl.ds(base, TQ), _head_lanes(hh)]
                sc = lax.dot_general(qs[hh], kj, NT_DIMS, preferred_element_type=F32)
                if masked:
                    sc = jnp.where(_chunk_mask(True), sc, NEG)
                p = jnp.exp(sc - lses[hh])
                dp = lax.dot_general(dobs[hh], vj, NT_DIMS, preferred_element_type=F32)
                ds = (p * (dp - deltas[hh])).astype(BF16)
                out.append(carry[hh] + jnp.dot(ds, kj, preferred_element_type=F32))
            return tuple(out)

        init = tuple(jnp.zeros((TQ, LANES), F32) for _ in range(HEADS_PER_STEP))
        carry = lax.fori_loop(0, i, functools.partial(step, masked=False), init)
        for hh, dq in enumerate(step(i, carry, True)):
            dq_ref[:, _head_lanes(hh)] = dq
            delta_row_ref[hh, 0] = _to_row(deltas[hh])

    tile, whole, row_tile, _ = _flash_specs(s)
    return _call(
        body, name=name, grid=(MLA_HEADS // HEADS_PER_STEP, nq),
        in_specs=[tile, whole, whole, tile, tile, tile], out_specs=[tile, row_tile],
        out_shape=[jax.ShapeDtypeStruct((s, MLA_HEADS * LANES), F32),
                   jax.ShapeDtypeStruct((MLA_HEADS, nq, 1, TQ), F32)],
        compiler_params=_params("parallel", "arbitrary"),
    )(q, k, v, do, o, lse)


def _flash_bwd_dkv(q, k, v, do, lse_row, delta_row, name):
    s = q.shape[0]
    nq = s // TQ

    def body(q_ref, k_ref, v_ref, do_ref, lse_row_ref, delta_row_ref, dk_ref, dv_ref):
        j = pl.program_id(1)
        kjs = [k_ref[:, _head_lanes(hh)] for hh in range(HEADS_PER_STEP)]
        vjs = [v_ref[:, _head_lanes(hh)] for hh in range(HEADS_PER_STEP)]

        def step(i, carry, masked):
            base = pl.multiple_of(i * TQ, TQ)
            out = []
            for hh in range(HEADS_PER_STEP):
                dk, dv = carry[hh]
                qi = q_ref[pl.ds(base, TQ), _head_lanes(hh)]
                dob = do_ref[pl.ds(base, TQ), _head_lanes(hh)].astype(BF16)
                sc_t = lax.dot_general(kjs[hh], qi, NT_DIMS, preferred_element_type=F32)
                if masked:
                    sc_t = jnp.where(_chunk_mask(False), sc_t, NEG)
                p_t = jnp.exp(sc_t - lse_row_ref[hh, i])
                dv = dv + jnp.dot(p_t.astype(BF16), dob, preferred_element_type=F32)
                dp_t = lax.dot_general(vjs[hh], dob, NT_DIMS, preferred_element_type=F32)
                ds_t = (p_t * (dp_t - delta_row_ref[hh, i])).astype(BF16)
                dk = dk + jnp.dot(ds_t, qi, preferred_element_type=F32)
                out.append((dk, dv))
            return tuple(out)

        zero = jnp.zeros((TQ, LANES), F32)
        carry = step(j, tuple((zero, zero) for _ in range(HEADS_PER_STEP)), True)
        carry = lax.fori_loop(j + 1, nq, functools.partial(step, masked=False), carry)
        for hh, (dk, dv) in enumerate(carry):
            dk_ref[:, _head_lanes(hh)] = dk
            dv_ref[:, _head_lanes(hh)] = dv

    tile, whole, _, row_whole = _flash_specs(s)
    return _call(
        body, name=name, grid=(MLA_HEADS // HEADS_PER_STEP, nq),
        in_specs=[whole, tile, tile, whole, row_whole, row_whole], out_specs=[tile, tile],
        out_shape=[jax.ShapeDtypeStruct((s, MLA_HEADS * LANES), F32)] * 2,
        compiler_params=_params("parallel", "arbitrary"),
    )(q, k, v, do, lse_row, delta_row)


def _pack_shards(shards):
    flat = jnp.concatenate([shards[n].reshape(-1) for n, _, _ in BIG])
    return flat.reshape(BIG_ROWS, LANES)


def _unpack_shards(packed):
    flat = packed.reshape(-1)
    out, off = {}, 0
    for n, shape, axis in BIG:
        sh = _shard_shape(shape, axis)
        size = math.prod(sh)
        out[n] = flat[off:off + size].reshape((1,) + sh)
        off += size
    return out


def _gathered_to_full(gathered):
    flat = gathered.reshape(N_DEV, -1)
    out, off = {}, 0
    for n, shape, axis in BIG:
        sh = _shard_shape(shape, axis)
        size = math.prod(sh)
        part = flat[:, off:off + size].reshape((N_DEV,) + sh)
        off += size
        if axis == 0:
            out[n] = part.reshape(shape)
        else:
            out[n] = part.transpose(1, 0, 2).reshape(shape)
    return out


def _full_to_shards(full):
    parts = []
    for n, shape, axis in BIG:
        g = full[n]
        if axis == 0:
            parts.append(g.reshape(N_DEV, -1))
        else:
            parts.append(g.reshape(shape[0], N_DEV, shape[1] // N_DEV).transpose(1, 0, 2).reshape(N_DEV, -1))
    return jnp.concatenate(parts, axis=1)


def _pack_small(vals):
    parts = []
    for (n, size), pad in zip(SMALL, SMALL_PAD):
        parts.append(jnp.pad(vals[n].reshape(-1), (0, pad - size)))
    flat = jnp.concatenate(parts)
    return jnp.pad(flat, (0, SMALL_ROWS * LANES - flat.shape[0])).reshape(SMALL_ROWS, LANES)


def _unpack_small(packed):
    flat = packed.reshape(-1)
    out, off = {}, 0
    for (n, size), pad in zip(SMALL, SMALL_PAD):
        out[n] = flat[off:off + size].reshape(1, size)
        off += pad
    return out


def _pad_heads(w, per_head, axis):
    shape = list(w.shape)
    shape[axis:axis + 1] = [MLA_HEADS, per_head]
    w = w.reshape(shape)
    pad = [(0, 0)] * len(shape)
    pad[axis + 1] = (0, LANES - per_head)
    w = jnp.pad(w, pad)
    shape[axis:axis + 2] = [MLA_HEADS * LANES]
    return w.reshape(shape)


def _unpad_heads(w, per_head, axis):
    shape = list(w.shape)
    shape[axis:axis + 1] = [MLA_HEADS, LANES]
    w = w.reshape(shape)
    w = lax.slice_in_dim(w, 0, per_head, axis=axis + 1)
    shape[axis:axis + 2] = [MLA_HEADS * per_head]
    return w.reshape(shape)


def _row(v, pad_to=None):
    v = v.reshape(1, -1)
    if pad_to is not None:
        v = jnp.pad(v, ((0, 0), (0, pad_to - v.shape[1])))
    return v


def kernel(x, mem, positions, mix_norm_g, w_in, b_conv_in, w_conv_dw, b_conv_dw, conv_ln_g, conv_ln_b, q_lat_norm_g, w_uq, kv_lat_norm_g, w_ukv, q_norm_g, k_norm_g, w_out, mem_norm_x_g, mem_norm_m_g, w_mem_q, w_mem_kv, mem_q_norm_g, mem_k_norm_g, w_mem_o, ffn_norm_g, w_up, w_ffn_dw, b_ffn_dw, w_down, loss_target, m_mix_norm_g, m_w_in, m_b_conv_in, m_w_conv_dw, m_b_conv_dw, m_conv_ln_g, m_conv_ln_b, m_q_lat_norm_g, m_w_uq, m_kv_lat_norm_g, m_w_ukv, m_q_norm_g, m_k_norm_g, m_w_out, m_mem_norm_x_g, m_mem_norm_m_g, m_w_mem_q, m_w_mem_kv, m_mem_q_norm_g, m_mem_k_norm_g, m_w_mem_o, m_ffn_norm_g, m_w_up, m_w_ffn_dw, m_b_ffn_dw, m_w_down, v_mix_norm_g, v_w_in, v_b_conv_in, v_w_conv_dw, v_b_conv_dw, v_conv_ln_g, v_conv_ln_b, v_q_lat_norm_g, v_w_uq, v_kv_lat_norm_g, v_w_ukv, v_q_norm_g, v_k_norm_g, v_w_out, v_mem_norm_x_g, v_mem_norm_m_g, v_w_mem_q, v_w_mem_kv, v_mem_q_norm_g, v_mem_k_norm_g, v_w_mem_o, v_ffn_norm_g, v_w_up, v_w_ffn_dw, v_b_ffn_dw, v_w_down):
    a = dict(locals())
    seq = x.shape[1]
    xs = x.reshape(seq, D_MODEL)
    mems = mem.reshape(-1, D_MODEL)
    target = loss_target.reshape(seq, D_MODEL)

    packed_w = _pack_shards({n: a[n] for n, _, _ in BIG})
    full = _gathered_to_full(_all_gather(packed_w.astype(BF16), "ag_weights"))
    wi = full["w_in"]
    s3 = 2 * CONV_CH + MLA_Q_RANK + MLA_KV_RANK
    w_in_p = jnp.concatenate([
        wi[:, :s3], jnp.zeros((D_MODEL, MLA_NOPE), BF16), wi[:, s3:],
        jnp.zeros((D_MODEL, LANES - MLA_QK), BF16)], axis=1)
    w_uq_p = _pad_heads(full["w_uq"], MLA_QK, 1)
    ukv = full["w_ukv"].reshape(MLA_KV_RANK, MLA_HEADS, MLA_NOPE + MLA_V)
    w_uk_p = _pad_heads(ukv[:, :, :MLA_NOPE].reshape(MLA_KV_RANK, -1), MLA_NOPE, 1)
    w_uv_p = _pad_heads(ukv[:, :, MLA_NOPE:].reshape(MLA_KV_RANK, -1), MLA_V, 1)
    w_out_u = full["w_out"][:CONV_CH]
    w_out_a = _pad_heads(full["w_out"][CONV_CH:], MLA_V, 0)
    w_mq, w_mkv, w_mo = full["w_mem_q"], full["w_mem_kv"], full["w_mem_o"]
    w_up_g, w_up_v = full["w_up"][:, :D_FF], full["w_up"][:, D_FF:]
    w_dn = full["w_down"]
    w_cdw = full["w_conv_dw"].astype(F32)
    w_fdw = full["w_ffn_dw"].astype(F32)
    w_fdw_g, w_fdw_v = w_fdw[:, :D_FF], w_fdw[:, D_FF:]

    g_mix, g_qlat, g_kvlat = _row(mix_norm_g), _row(q_lat_norm_g), _row(kv_lat_norm_g)
    b_in = _row(b_conv_in)
    b_in_a, b_in_g = b_in[:, :CONV_CH], b_in[:, CONV_CH:]
    b_cdw, ln_g, ln_b = _row(b_conv_dw), _row(conv_ln_g), _row(conv_ln_b)
    g_q, g_k = _row(q_norm_g, LANES), _row(k_norm_g, LANES)
    g_memx, g_memm = _row(mem_norm_x_g), _row(mem_norm_m_g)
    g_mq, g_mk, g_ffn = _row(mem_q_norm_g), _row(mem_k_norm_g), _row(ffn_norm_g)
    b_f = _row(b_ffn_dw)
    b_f_g, b_f_v = b_f[:, :D_FF], b_f[:, D_FF:]

    freq = ROPE_THETA ** (-jnp.arange(0, MLA_ROPE, 2, dtype=F32) / MLA_ROPE)
    inv_freq = jnp.concatenate([jnp.zeros((MLA_NOPE,), F32), freq, freq,
                                jnp.zeros((LANES - MLA_QK,), F32)]).reshape(1, LANES)
    cos, sin_a, sin_b = _rowwise(_f_rope_tab, [positions.reshape(seq, 1)], [inv_freq],
                                 [(LANES, F32)] * 3, [], name="rope_tables")

    (h1,) = _rowwise(_f_rms, [xs], [g_mix], [(D_MODEL, BF16)], [], name="rms_mix")
    z = _mm(h1, w_in_p, name="mm_in")
    z_rows = [(z, CONV_CH, 0, 0), (z, CONV_CH, 1, 0), (z, MLA_Q_RANK, 4, 0), (z, MLA_KV_RANK, 10, 0)]
    z_kr = (z, LANES, 11, 0)
    u0, cqn, ckvn = _rowwise(
        _mix_pre, z_rows, [b_in_a, b_in_g, g_qlat, g_kvlat],
        [(CONV_CH, F32), (MLA_Q_RANK, BF16), (MLA_KV_RANK, BF16)], [], name="mix_pre")
    c1 = _conv_fwd(u0, w_cdw, "conv31_fwd")
    (u,) = _rowwise(lambda c, b, g, bb: (_ln_silu(c, b, g, bb),), [c1], [b_cdw, ln_g, ln_b],
                    [(CONV_CH, BF16)], [], name="ln_silu")
    q0 = _mm(cqn, w_uq_p, name="mm_uq")
    kn0 = _mm(ckvn, w_uk_p, name="mm_uk")
    v0 = _mm(ckvn, w_uv_p, out_dtype=BF16, name="mm_uv")
    qk_rows = [q0, kn0, z_kr, cos, sin_a, sin_b]
    qh, kh = _rowwise(_f_qk_prep, qk_rows, [g_q, g_k],
                      [(MLA_HEADS * LANES, BF16)] * 2, [], name="qk_prep")
    attn, lse, lse_row = _flash_fwd(qh, kh, v0, "flash_fwd")
    x1 = _mm(u, w_out_u, add=xs, name="mm_out_u")
    x1 = _mm(attn, w_out_a, add=x1, name="mm_out_a")

    (hq,) = _rowwise(_f_rms, [x1], [g_memx], [(D_MODEL, BF16)], [], name="rms_memx")
    (hm,) = _rowwise(_f_rms, [mems], [g_memm], [(D_MODEL, BF16)], [], name="rms_memm", tm=mems.shape[0])
    qm0 = _mm(hq, w_mq, name="mm_memq")
    kvm0 = _mm(hm, w_mkv, name="mm_memkv", tm=mems.shape[0])
    (km,) = _rowwise(_f_mem_k, [(kvm0, D_MODEL, 0, 0)], [g_mk], [(D_MODEL, BF16)], [],
                     name="mem_k", tm=mems.shape[0])
    vm = kvm0[:, D_MODEL:]
    (om,) = _rowwise(_f_mem_attn, [qm0], [km, vm, g_mq], [(D_MODEL, BF16)], [], name="mem_attn")
    x2 = _mm(om, w_mo, add=x1, name="mm_memo")

    (h3,) = _rowwise(_f_rms, [x2], [g_ffn], [(D_MODEL, BF16)], [], name="rms_ffn")
    up_g = _mm(h3, w_up_g, name="mm_up_g")
    up_v = _mm(h3, w_up_v, name="mm_up_v")
    cg = _conv_fwd(up_g, w_fdw_g, "conv3_g_fwd")
    cv = _conv_fwd(up_v, w_fdw_v, "conv3_v_fwd")
    (act,) = _rowwise(lambda g_, v_, bg, bv: (_act(g_, v_, bg, bv),), [cg, cv], [b_f_g, b_f_v],
                      [(D_FF, BF16)], [], name="ffn_act", tm=256)
    y = _mm(act, w_dn, add=x2, name="mm_down")
    dy, sq = _rowwise(_f_loss, [y, target], [], [(D_MODEL, F32)], [(1, D_MODEL)], name="loss")
    loss = lax.psum(0.5 * jnp.sum(sq) / D_MODEL, ("x", "y", "c"))

    gw, gs = {}, {}
    gw["w_down"] = _mm_tn(act, dy, name="tn_down")
    dact = _mm(dy, w_dn, trans_b=True, name="mm_down_t")
    dcg, dcv, db_g, db_v = _rowwise(_b_act, [cg, cv, dact], [b_f_g, b_f_v],
                                    [(D_FF, F32)] * 2, [(1, D_FF)] * 2, name="ffn_act_bwd", tm=256)
    gs["b_ffn_dw"] = jnp.concatenate([db_g, db_v], axis=1)
    dup_g, dwf_g = _conv_bwd(dcg, up_g, w_fdw_g, "conv3_g_bwd")
    dup_v, dwf_v = _conv_bwd(dcv, up_v, w_fdw_v, "conv3_v_bwd")
    gw["w_ffn_dw"] = jnp.concatenate([dwf_g, dwf_v], axis=1)
    gw["w_up"] = jnp.concatenate([_mm_tn(h3, dup_g, name="tn_up_g"), _mm_tn(h3, dup_v, name="tn_up_v")], axis=1)
    dh3 = _mm(dup_g, w_up_g, trans_b=True, name="mm_up_g_t")
    dh3 = _mm(dup_v, w_up_v, trans_b=True, add=dh3, name="mm_up_v_t")
    dx2, gs["ffn_norm_g"] = _rowwise(_b_rms, [x2, dh3, dy], [g_ffn], [(D_MODEL, F32)], [(1, D_MODEL)],
                                     name="rms_ffn_bwd")

    gw["w_mem_o"] = _mm_tn(om, dx2, name="tn_memo")
    dom = _mm(dx2, w_mo, trans_b=True, name="mm_memo_t")
    n_mem = mems.shape[0]
    dqm0, dkm, dvm, gs["mem_q_norm_g"] = _rowwise(
        _b_mem_attn, [dom, qm0], [km, vm, g_mq], [(D_MODEL, F32)],
        [(n_mem, D_MODEL), (n_mem, D_MODEL), (1, MEM_HEAD_DIM)], name="mem_attn_bwd")
    gw["w_mem_q"] = _mm_tn(hq, dqm0, name="tn_memq")
    dhq = _mm(dqm0, w_mq, trans_b=True, name="mm_memq_t")
    dx1, gs["mem_norm_x_g"] = _rowwise(_b_rms, [x1, dhq, dx2], [g_memx], [(D_MODEL, F32)],
                                       [(1, D_MODEL)], name="rms_memx_bwd")
    dkk, gs["mem_k_norm_g"] = _rowwise(_b_mem_k, [(kvm0, D_MODEL, 0, 0), dkm], [g_mk],
                                       [(D_MODEL, F32)], [(1, MEM_HEAD_DIM)], name="mem_k_bwd", tm=n_mem)
    dkvm0 = jnp.concatenate([dkk, dvm], axis=1)
    gw["w_mem_kv"] = _mm_tn(hm, dkvm0, name="tn_memkv", ts=n_mem)
    dhm = _mm(dkvm0, w_mkv, trans_b=True, name="mm_memkv_t", tm=n_mem)
    _, gs["mem_norm_m_g"] = _rowwise(_b_rms_nores, [mems, dhm], [g_memm], [(D_MODEL, F32)],
                                     [(1, D_MODEL)], name="rms_memm_bwd", tm=n_mem)

    gw_out_u = _mm_tn(u, dx1, name="tn_out_u")
    gw_out_a = _mm_tn(attn, dx1, name="tn_out_a")
    gw["w_out"] = jnp.concatenate([gw_out_u, _unpad_heads(gw_out_a, MLA_V, 0)], axis=0)
    du = _mm(dx1, w_out_u, trans_b=True, name="mm_out_u_t")
    dattn = _mm(dx1, w_out_a, trans_b=True, name="mm_out_a_t")
    dc1, gs["b_conv_dw"], gs["conv_ln_g"], gs["conv_ln_b"] = _rowwise(
        _b_ln_silu, [c1, du], [b_cdw, ln_g, ln_b], [(CONV_CH, F32)], [(1, CONV_CH)] * 3, name="ln_silu_bwd")
    du0, gw["w_conv_dw"] = _conv_bwd(dc1, u0, w_cdw, "conv31_bwd")
    dqh, delta_row = _flash_bwd_dq(qh, kh, v0, dattn, attn, lse, "flash_bwd_dq")
    dkh, dv0 = _flash_bwd_dkv(qh, kh, v0, dattn, lse_row, delta_row, "flash_bwd_dkv")
    dq0, dkn0, dkr, dgq, dgk = _rowwise(
        _b_qk_prep, qk_rows + [dqh, dkh], [g_q, g_k],
        [(MLA_HEADS * LANES, F32)] * 2 + [(LANES, F32)], [(1, LANES)] * 2, name="qk_prep_bwd")
    gs["q_norm_g"], gs["k_norm_g"] = dgq[:, :MLA_QK], dgk[:, :MLA_QK]
    gw["w_uq"] = _unpad_heads(_mm_tn(cqn, dq0, name="tn_uq"), MLA_QK, 1)
    g_uk = _unpad_heads(_mm_tn(ckvn, dkn0, name="tn_uk"), MLA_NOPE, 1).reshape(MLA_KV_RANK, MLA_HEADS, MLA_NOPE)
    g_uv = _unpad_heads(_mm_tn(ckvn, dv0, name="tn_uv"), MLA_V, 1).reshape(MLA_KV_RANK, MLA_HEADS, MLA_V)
    gw["w_ukv"] = jnp.concatenate([g_uk, g_uv], axis=2).reshape(MLA_KV_RANK, -1)
    dcqn = _mm(dq0, w_uq_p, trans_b=True, name="mm_uq_t")
    dckvn = _mm(dkn0, w_uk_p, trans_b=True, name="mm_uk_t")
    dckvn = _mm(dv0, w_uv_p, trans_b=True, add=dckvn, name="mm_uv_t")
    dz, dba, dbg, gs["q_lat_norm_g"], gs["kv_lat_norm_g"] = _rowwise(
        _b_mix_pre, z_rows + [du0, dcqn, dckvn, dkr], [b_in_a, b_in_g, g_qlat, g_kvlat],
        [(IN_PAD, F32)], [(1, CONV_CH)] * 2 + [(1, MLA_Q_RANK), (1, MLA_KV_RANK)], name="mix_pre_bwd")
    gs["b_conv_in"] = jnp.concatenate([dba, dbg], axis=1)
    gw_in = _mm_tn(h1, dz, name="tn_in")
    gw["w_in"] = jnp.concatenate([gw_in[:, :s3], gw_in[:, s3 + MLA_NOPE:s3 + MLA_QK]], axis=1)
    dh1 = _mm(dz, w_in_p, trans_b=True, name="mm_in_t")
    dx, gs["mix_norm_g"] = _rowwise(_b_rms, [xs, dh1, dx1], [g_mix], [(D_MODEL, F32)], [(1, D_MODEL)],
                                    name="rms_mix_bwd")

    shards = _full_to_shards(gw).reshape(4, 2, BIG_ROWS, LANES)
    keep, got = _swap_sibling(shards, "rs_sibling")
    (chip_sum,) = _rowwise(lambda p, q_: (p + q_,),
                           [keep.reshape(4 * BIG_ROWS, LANES), got.reshape(4 * BIG_ROWS, LANES)], [],
                           [(LANES, F32)], [], name="rs_add", tm=1024)
    pieces = _swap_chips(chip_sum.reshape(4, BIG_ROWS, LANES), "rs_chips").reshape(4 * BIG_ROWS, LANES)
    nb = BIG_ROWS // 1024
    big_rows = [(pieces, LANES, 0, k * nb) for k in range(4)]
    big_rows += [_pack_shards({n: a[p + n] for n, _, _ in BIG}) for p in ("", "m_", "v_")]
    g_big, d_big, m_big, v_big = [_unpack_shards(t) for t in _rowwise(
        _f_adam_big, big_rows, [], [(LANES, F32)] * 4, [], name="adamw_big", tm=1024, n_rows=BIG_ROWS)]

    parts = _all_gather(_pack_small(gs), "ag_small_grads")

    def small_body(p_ref, w_ref, m_ref, v_ref, g_ref, d_ref, mo_ref, vo_ref):
        outs = _f_adam_small([p_ref[k] for k in range(N_DEV)], w_ref[...], m_ref[...], v_ref[...])
        for r, val in zip((g_ref, d_ref, mo_ref, vo_ref), outs):
            r[...] = val

    small_in = [_pack_small({n: a[p + n] for n, _ in SMALL}) for p in ("", "m_", "v_")]
    g_sm, d_sm, m_sm, v_sm = [_unpack_small(t) for t in _call(
        small_body, name="adamw_small",
        out_shape=[jax.ShapeDtypeStruct((SMALL_ROWS, LANES), F32)] * 4)(parts, *small_in)]

    def pick(big, small):
        return [big[n] if n in big else small[n] for n in WEIGHTS]

    return (loss, dx.reshape(1, seq, D_MODEL), *pick(g_big, g_sm), *pick(d_big, d_sm),
            *pick(m_big, m_sm), *pick(v_big, v_sm))
```

```python
import functools
import math

import jax
import jax.numpy as jnp
from jax import lax
from jax.experimental import pallas as pl
from jax.experimental.pallas import tpu as pltpu

F32 = jnp.float32
BF16 = jnp.bfloat16
EPS = 1e-6
LANES = 128
N_DEV = 8
D_MODEL = 1024
CONV_CH = 512
CONV_WIDTH = 31
MLA_HEADS = 8
MLA_NOPE = 64
MLA_ROPE = 32
MLA_V = 64
MLA_QK = MLA_NOPE + MLA_ROPE
MLA_Q_RANK = 256
MLA_KV_RANK = 128
ROPE_THETA = 10000.0
IN_COLS = 2 * CONV_CH + MLA_Q_RANK + MLA_KV_RANK + MLA_ROPE
IN_PAD = 2 * CONV_CH + MLA_Q_RANK + MLA_KV_RANK + LANES
MEM_HEADS = 4
MEM_HEAD_DIM = 256
D_FF = 2816
FFN_CONV_WIDTH = 3
CHUNK = 64
ATT_SCALE = 1.0 / math.sqrt(MLA_QK)
MEM_SCALE = 1.0 / math.sqrt(MEM_HEAD_DIM)
ADAM_LR, ADAM_B1, ADAM_B2, ADAM_EPS, ADAM_WD, ADAM_STEP = 0.001, 0.9, 0.999, 1e-08, 0.01, 10

TM = 512
TQ = 512
HEADS_PER_STEP = 2
CONV_ROWS = 256
NEG = -1e30
VMEM_LIMIT = 56 * 1024 * 1024

MESH = pl.DeviceIdType.MESH
ANY = pl.BlockSpec(memory_space=pl.ANY)
NT_DIMS = (((1,), (1,)), ((), ()))

BIG = [
    ("w_in", (1024, 180)), ("w_uq", (256, 96)), ("w_ukv", (128, 128)), ("w_out", (128, 1024)),
    ("w_mem_q", (128, 1024)), ("w_mem_kv", (1024, 256)), ("w_mem_o", (128, 1024)),
    ("w_up", (1024, 704)), ("w_down", (352, 1024)),
]
TINY = [("w_conv_dw", (31, 64), 16), ("w_ffn_dw", (3, 704), 24)]
ROW_STEPS = 4
FF_SHARD = D_FF // 4
FF_PAD = 768
D_FF_PAD = 4 * FF_PAD
SMALL = [
    ("mix_norm_g", 1024), ("b_conv_in", 1024), ("b_conv_dw", 512), ("conv_ln_g", 512),
    ("conv_ln_b", 512), ("q_lat_norm_g", 256), ("kv_lat_norm_g", 128), ("q_norm_g", 96),
    ("k_norm_g", 96), ("mem_norm_x_g", 1024), ("mem_norm_m_g", 1024), ("mem_q_norm_g", 256),
    ("mem_k_norm_g", 256), ("ffn_norm_g", 1024), ("b_ffn_dw", 5632),
]
WEIGHTS = [
    "mix_norm_g", "w_in", "b_conv_in", "w_conv_dw", "b_conv_dw", "conv_ln_g", "conv_ln_b",
    "q_lat_norm_g", "w_uq", "kv_lat_norm_g", "w_ukv", "q_norm_g", "k_norm_g", "w_out",
    "mem_norm_x_g", "mem_norm_m_g", "w_mem_q", "w_mem_kv", "mem_q_norm_g", "mem_k_norm_g",
    "w_mem_o", "ffn_norm_g", "w_up", "w_ffn_dw", "b_ffn_dw", "w_down",
]


SMALL_PAD = [(-(-n // LANES)) * LANES for _, n in SMALL]
SMALL_ROWS = -(-sum(SMALL_PAD) // (8 * LANES)) * 8
TINY_BASE = [SMALL_ROWS + N_DEV * sum(r for _, _, r in TINY[:i]) for i in range(len(TINY))]
PART_ROWS = SMALL_ROWS + N_DEV * sum(r for _, _, r in TINY)


def _call(body, **kw):
    return pl.pallas_call(body, **kw)


def _params(*sem):
    return pltpu.CompilerParams(dimension_semantics=sem, vmem_limit_bytes=VMEM_LIMIT)


def _all_gather(xs, name):
    n = len(xs)

    def body(*refs):
        x_refs, out_refs = refs[:n], refs[n:2 * n]
        send_sems, recv_sems, local_sems = refs[2 * n:]
        x, y, c = lax.axis_index("x"), lax.axis_index("y"), lax.axis_index("c")
        me, sibling = (x, y, c), (x, y, 1 - c)
        chips = [(1 - x, y), (x, 1 - y), (1 - x, 1 - y)]

        def slot(o, px, py, pc):
            return out_refs[o].at[4 * px + 2 * py + pc]

        def copy(o, k, block, to, src=None):
            return pltpu.make_async_remote_copy(
                src_ref=slot(o, *block) if src is None else src, dst_ref=slot(o, *block),
                send_sem=send_sems.at[o, k], recv_sem=recv_sems.at[o, k],
                device_id=to, device_id_type=MESH)

        mine = [pltpu.make_async_copy(x_refs[o], slot(o, *me), local_sems.at[o]) for o in range(n)]
        first = [copy(o, 0, me, sibling, src=x_refs[o]) for o in range(n)]
        first += [copy(o, 1 + j, me, (*chip, c), src=x_refs[o])
                  for j, chip in enumerate(chips) for o in range(n)]
        for cp in mine + first:
            cp.start()
        passed = []
        for j, chip in enumerate(chips):
            for o in range(n):
                copy(o, 1 + j, (*chip, c), me).wait_recv()
                passed.append(copy(o, 4 + j, (*chip, c), sibling))
                passed[-1].start()
        for o in range(n):
            copy(o, 0, sibling, me).wait_recv()
        for j, chip in enumerate(chips):
            for o in range(n):
                copy(o, 4 + j, (*chip, 1 - c), me).wait_recv()
        for cp in first + passed:
            cp.wait_send()
        for cp in mine:
            cp.wait()

    return _call(
        body, name=name,
        out_shape=[jax.ShapeDtypeStruct((N_DEV,) + v.shape, v.dtype) for v in xs],
        in_specs=[ANY] * n, out_specs=[ANY] * n,
        scratch_shapes=[pltpu.SemaphoreType.DMA((n, 7)), pltpu.SemaphoreType.DMA((n, 7)),
                        pltpu.SemaphoreType.DMA((n,))],
    )(*xs)


def _swap_sibling(grads, name):
    n = len(grads)

    def body(*refs):
        g_refs, got_refs = refs[:n], refs[n:2 * n]
        send_sems, recv_sems = refs[2 * n:]
        x, y, c = lax.axis_index("x"), lax.axis_index("y"), lax.axis_index("c")
        copies = [
            pltpu.make_async_remote_copy(
                src_ref=g_refs[o].at[2 * chip + 1 - c], dst_ref=got_refs[o].at[chip],
                send_sem=send_sems.at[o, chip], recv_sem=recv_sems.at[o, chip],
                device_id=(x, y, 1 - c), device_id_type=MESH)
            for o in range(n) for chip in range(4)]
        for cp in copies:
            cp.start()
        for cp in copies:
            cp.wait()

    return _call(
        body, name=name,
        out_shape=[jax.ShapeDtypeStruct((4,) + g.shape[1:], g.dtype) for g in grads],
        in_specs=[ANY] * n, out_specs=[ANY] * n,
        scratch_shapes=[pltpu.SemaphoreType.DMA((n, 4)), pltpu.SemaphoreType.DMA((n, 4))],
    )(*grads)


def _swap_chips(sums, name):
    n = len(sums)

    def body(*refs):
        a_refs, r_refs = refs[:n], refs[n:2 * n]
        send_sems, recv_sems = refs[2 * n:]
        x, y, c = lax.axis_index("x"), lax.axis_index("y"), lax.axis_index("c")
        peers = [(x, 1 - y), (1 - x, y), (1 - x, 1 - y)]
        copies = [
            pltpu.make_async_remote_copy(
                src_ref=a_refs[o].at[2 * px + py], dst_ref=r_refs[o].at[k],
                send_sem=send_sems.at[o, k], recv_sem=recv_sems.at[o, k],
                device_id=(px, py, c), device_id_type=MESH)
            for k, (px, py) in enumerate(peers) for o in range(n)]
        for cp in copies:
            cp.start()
        for cp in copies:
            cp.wait()

    return _call(
        body, name=name,
        out_shape=[jax.ShapeDtypeStruct((3,) + a.shape[1:], a.dtype) for a in sums],
        in_specs=[ANY] * n, out_specs=[ANY] * n,
        scratch_shapes=[pltpu.SemaphoreType.DMA((n, 3)), pltpu.SemaphoreType.DMA((n, 3))],
    )(*sums)


def _shard_block(shape):
    return (None, shape[-2] // ROW_STEPS, shape[-1])


def _rs_add(grads, gots, pos, name):
    n = len(grads)

    def body(pos_ref, *refs):
        for g_ref, t_ref, o_ref in zip(refs[:n], refs[n:2 * n], refs[2 * n:]):
            o_ref[...] = g_ref[...] + t_ref[...]

    in_specs = [pl.BlockSpec(_shard_block(g.shape), lambda a, t, pos: (2 * a + pos[1], t, 0)) for g in grads]
    in_specs += [pl.BlockSpec(_shard_block(g.shape), lambda a, t, pos: (a, t, 0)) for g in gots]
    return _call(
        body, name=name,
        grid_spec=pltpu.PrefetchScalarGridSpec(
            num_scalar_prefetch=1, grid=(4, ROW_STEPS), in_specs=in_specs,
            out_specs=[pl.BlockSpec(_shard_block(g.shape), lambda a, t, pos: (a, t, 0)) for g in gots]),
        out_shape=[jax.ShapeDtypeStruct(g.shape, g.dtype) for g in gots],
        compiler_params=_params("arbitrary", "arbitrary"),
    )(pos, *grads, *gots)


def _adamw_big(sums, recvs, ws, ms, vs, pos, name):
    n = len(sums)

    def body(pos_ref, *refs):
        ins, outs = refs[:7 * n], refs[7 * n:]
        for o in range(n):
            own, r1, r2, r3, w, m, v = [r[...] for r in ins[7 * o:7 * o + 7]]
            g = ((own + r1) + r2) + r3
            for ref, val in zip(outs[4 * o:4 * o + 4], (g,) + _adamw(w, g, m, v)):
                ref[...] = val

    in_specs, args, out_specs, out_shape = [], [], [], []
    for s_, r_, w_, m_, v_ in zip(sums, recvs, ws, ms, vs):
        blk = _shard_block(w_.shape)
        in_specs.append(pl.BlockSpec(blk, lambda t, pos: (pos[0], t, 0)))
        in_specs += [pl.BlockSpec(blk, lambda t, pos, k=k: (k, t, 0)) for k in range(3)]
        in_specs += [pl.BlockSpec(blk, lambda t, pos: (0, t, 0))] * 3
        args += [s_, r_, r_, r_, w_, m_, v_]
        out_specs += [pl.BlockSpec(blk, lambda t, pos: (0, t, 0))] * 4
        out_shape += [jax.ShapeDtypeStruct(w_.shape, F32)] * 4
    return _call(
        body, name=name,
        grid_spec=pltpu.PrefetchScalarGridSpec(
            num_scalar_prefetch=1, grid=(ROW_STEPS,), in_specs=in_specs, out_specs=out_specs),
        out_shape=out_shape, compiler_params=_params("arbitrary"),
    )(pos, *args)


def _tile(n, prefs):
    for t in prefs:
        if n % t == 0:
            return t
    return n


def _mm(a, b, *, name, trans_b=False, add=None, out_dtype=F32, tm=TM, b_shards=None):
    m, k = a.shape
    if b_shards is None:
        n = b.shape[0] if trans_b else b.shape[1]
        tn = _tile(n, (1536, 1408, 1024, 768, 512, 256, 128))
        tk = _tile(k, (1408, 1024, 768, 512, 256, 128))
    elif trans_b:
        n, tn, tk = b.shape[1], b.shape[1], b.shape[2]
    else:
        n, tn, tk = b_shards[1] * b.shape[2], b.shape[2], _tile(k, (1024, 512))
    nk = k // tk
    has_add = add is not None

    def body(*refs):
        a_ref, b_ref = refs[0], refs[1]
        add_ref = refs[2] if has_add else None
        o_ref = refs[2 + has_add]
        av = a_ref[...].astype(BF16)
        bv = b_ref[...].astype(BF16)
        if trans_b:
            part = lax.dot_general(av, bv, NT_DIMS, preferred_element_type=F32)
        else:
            part = jnp.dot(av, bv, preferred_element_type=F32)

        def finish(acc):
            if has_add:
                acc = acc + add_ref[...].astype(F32)
            o_ref[...] = acc.astype(o_ref.dtype)

        if nk == 1:
            finish(part)
        else:
            acc_ref = refs[3 + has_add]
            kk = pl.program_id(2)

            @pl.when(kk == 0)
            def _():
                acc_ref[...] = part

            @pl.when(kk > 0)
            def _():
                acc_ref[...] += part

            @pl.when(kk == nk - 1)
            def _():
                finish(acc_ref[...])

    in_specs = [pl.BlockSpec((tm, tk), lambda i, j, kk: (i, kk))]
    if b_shards is not None and trans_b:
        in_specs.append(pl.BlockSpec((None, tn, tk), lambda i, j, kk: (b_shards[0] + kk, j, 0)))
    elif b_shards is not None:
        in_specs.append(pl.BlockSpec((None, tk, tn), lambda i, j, kk: (b_shards[0] + j, kk, 0)))
    elif trans_b:
        in_specs.append(pl.BlockSpec((tn, tk), lambda i, j, kk: (j, kk)))
    else:
        in_specs.append(pl.BlockSpec((tk, tn), lambda i, j, kk: (kk, j)))
    args = [a, b]
    if has_add:
        in_specs.append(pl.BlockSpec((tm, tn), lambda i, j, kk: (i, j)))
        args.append(add)
    return _call(
        body, name=name, grid=(m // tm, n // tn, nk), in_specs=in_specs,
        out_specs=pl.BlockSpec((tm, tn), lambda i, j, kk: (i, j)),
        out_shape=jax.ShapeDtypeStruct((m, n), out_dtype),
        scratch_shapes=[pltpu.VMEM((tm, tn), F32)] if nk > 1 else [],
        compiler_params=_params("parallel", "parallel", "arbitrary"),
    )(*args)


def _mm_tn(a, b, *, name, ts=TM, shard_cols=None):
    s, m = a.shape
    n = b.shape[1]
    tm = _tile(m, (1408, 1024, 768, 512, 256, 128))
    tn = shard_cols or _tile(n, (1536, 1408, 1024, 768, 512, 256, 128))
    if shard_cols:
        out_spec = pl.BlockSpec((None, tm, tn), lambda i, j, kk: (j, i, 0))
        out_shape = jax.ShapeDtypeStruct((n // tn, m, tn), F32)
    else:
        out_spec = pl.BlockSpec((tm, tn), lambda i, j, kk: (i, j))
        out_shape = jax.ShapeDtypeStruct((m, n), F32)

    def body(a_ref, b_ref, o_ref):
        kk = pl.program_id(2)
        part = jnp.dot(a_ref[...].astype(BF16).T, b_ref[...].astype(BF16),
                       preferred_element_type=F32)

        @pl.when(kk == 0)
        def _():
            o_ref[...] = part

        @pl.when(kk > 0)
        def _():
            o_ref[...] += part

    return _call(
        body, name=name, grid=(m // tm, n // tn, s // ts),
        in_specs=[pl.BlockSpec((ts, tm), lambda i, j, kk: (kk, i)),
                  pl.BlockSpec((ts, tn), lambda i, j, kk: (kk, j))],
        out_specs=out_spec, out_shape=out_shape,
        compiler_params=_params("parallel", "parallel", "arbitrary"),
    )(a, b)


def _rowwise(fn, rows, consts, row_outs, acc_outs, *, name, tm=TM, n_rows=None):
    rows = [r if isinstance(r, tuple) else (r, r.shape[1], 0, 0) for r in rows]
    s = n_rows or rows[0][0].shape[0]
    nr, nc, no, na = len(rows), len(consts), len(row_outs), len(acc_outs)

    def body(*refs):
        r_in, c_in = refs[:nr], refs[nr:nr + nc]
        o_refs, a_refs = refs[nr + nc:nr + nc + no], refs[nr + nc + no:]
        outs = fn(*[r[...] for r in r_in], *[c[...] for c in c_in])
        for r, v in zip(o_refs, outs[:no]):
            r[...] = v.astype(r.dtype)
        if na:
            i = pl.program_id(0)

            @pl.when(i == 0)
            def _():
                for r, v in zip(a_refs, outs[no:]):
                    r[...] = v.astype(F32)

            @pl.when(i > 0)
            def _():
                for r, v in zip(a_refs, outs[no:]):
                    r[...] += v.astype(F32)

    in_specs = [pl.BlockSpec((tm, w), lambda i, cb=cb, rb=rb: (i + rb, cb)) for _, w, cb, rb in rows]
    in_specs += [pl.BlockSpec(c.shape, lambda i: (0, 0)) for c in consts]
    out_specs = [pl.BlockSpec((tm, w), lambda i: (i, 0)) for w, _ in row_outs]
    out_specs += [pl.BlockSpec(sh, lambda i: (0, 0)) for sh in acc_outs]
    out_shape = [jax.ShapeDtypeStruct((s, w), dt) for w, dt in row_outs]
    out_shape += [jax.ShapeDtypeStruct(sh, F32) for sh in acc_outs]
    return _call(
        body, name=name, grid=(s // tm,), in_specs=in_specs, out_specs=out_specs,
        out_shape=out_shape, compiler_params=_params("arbitrary"),
    )(*[r[0] for r in rows], *consts)


def _rms(x, g, n=None):
    ms = jnp.sum(x * x, axis=-1, keepdims=True) / float(n or x.shape[-1])
    return x * lax.rsqrt(ms + EPS) * g


def _layer_norm(x, g, b):
    mu = jnp.sum(x, axis=-1, keepdims=True) / float(x.shape[-1])
    xc = x - mu
    var = jnp.sum(xc * xc, axis=-1, keepdims=True) / float(x.shape[-1])
    return xc * lax.rsqrt(var + EPS) * g + b


def _silu(x):
    return x * jax.nn.sigmoid(x)


@jax.custom_vjp
def _rope(y, cos, sin_a, sin_b):
    return y * cos + pltpu.roll(y, 112, 1) * sin_a + pltpu.roll(y, 16, 1) * sin_b


def _rope_fwd(y, cos, sin_a, sin_b):
    return _rope(y, cos, sin_a, sin_b), (cos, sin_a, sin_b)


def _rope_bwd(res, ct):
    cos, sin_a, sin_b = res
    dy = ct * cos + pltpu.roll(ct * sin_a, 16, 1) + pltpu.roll(ct * sin_b, 112, 1)
    return dy, jnp.zeros_like(cos), jnp.zeros_like(sin_a), jnp.zeros_like(sin_b)


_rope.defvjp(_rope_fwd, _rope_bwd)


def _qk_head(xh, g, cos, sin_a, sin_b):
    return _rope(_rms(xh, g, MLA_QK), cos, sin_a, sin_b)


def _heads(x, width):
    return [x[:, h * width:(h + 1) * width] for h in range(x.shape[1] // width)]


def _f_rms(x, g):
    return (_rms(x, g),)


def _f_rope_tab(pos, inv_freq):
    ang = pos.astype(F32) * inv_freq
    lane = lax.broadcasted_iota(jnp.int32, ang.shape, 1)
    sn = jnp.sin(ang)
    first = (lane >= MLA_NOPE) & (lane < MLA_NOPE + MLA_ROPE // 2)
    second = (lane >= MLA_NOPE + MLA_ROPE // 2) & (lane < MLA_QK)
    return jnp.cos(ang), jnp.where(first, -sn, 0.0), jnp.where(second, sn, 0.0)


def _mix_pre(za, zg, zcq, zckv, ba, bg, gq, gkv):
    u0 = (za + ba) * jax.nn.sigmoid(zg + bg)
    return u0, _rms(zcq, gq), _rms(zckv, gkv)


def _ln_silu(c1, bdw, lg, lb):
    return _silu(_layer_norm(c1 + bdw, lg, lb))


def _f_qk_prep(q0, kn, kr, cos, sa, sb, gq, gk):
    qs = [_qk_head(xh, gq, cos, sa, sb) * ATT_SCALE for xh in _heads(q0, LANES)]
    ks = [_qk_head(xh + kr, gk, cos, sa, sb) for xh in _heads(kn, LANES)]
    return jnp.concatenate(qs, axis=1), jnp.concatenate(ks, axis=1)


def _act(cg, cv, bg, bv):
    return _silu(cg + bg) * (cv + bv)


def _f_mem_k(kk, g):
    return (jnp.concatenate([_rms(xh, g) for xh in _heads(kk, MEM_HEAD_DIM)], axis=1),)


def _mem_probs(qn, kmh):
    s = lax.dot_general(qn.astype(BF16), kmh, NT_DIMS, preferred_element_type=F32) * MEM_SCALE
    e = jnp.exp(s - jnp.max(s, axis=-1, keepdims=True))
    return e / jnp.sum(e, axis=-1, keepdims=True)


def _f_mem_attn(qm0, km, vm, g):
    outs = []
    for h, xh in enumerate(_heads(qm0, MEM_HEAD_DIM)):
        sl = slice(h * MEM_HEAD_DIM, (h + 1) * MEM_HEAD_DIM)
        p = _mem_probs(_rms(xh, g), km[:, sl])
        outs.append(jnp.dot(p.astype(BF16), vm[:, sl].astype(BF16), preferred_element_type=F32))
    return (jnp.concatenate(outs, axis=1),)


def _f_loss(y, t):
    e = y - t
    return e * (1.0 / D_MODEL), jnp.sum(e * e, axis=0, keepdims=True)


def _b_rms(x, dh, dres, g):
    _, vjp = jax.vjp(_rms, x, g)
    dx, dg = vjp(dh)
    return dx + dres, dg


def _b_rms_nores(x, dh, g):
    _, vjp = jax.vjp(_rms, x, g)
    dx, dg = vjp(dh)
    return dx, dg


def _b_mix_pre(za, zg, zcq, zckv, du0, dcqn, dckvn, dkr, ba, bg, gq, gkv):
    _, vjp = jax.vjp(_mix_pre, za, zg, zcq, zckv, ba, bg, gq, gkv)
    dza, dzg, dzcq, dzckv, dba, dbg, dgq, dgkv = vjp((du0, dcqn, dckvn))
    return jnp.concatenate([dza, dzg, dzcq, dzckv, dkr], axis=1), dba, dbg, dgq, dgkv


def _b_ln_silu(c1, du, bdw, lg, lb):
    _, vjp = jax.vjp(_ln_silu, c1, bdw, lg, lb)
    return vjp(du)


def _b_qk_prep(q0, kn, kr, cos, sa, sb, dq, dk, gq, gk):
    head = lambda xh, g: _qk_head(xh, g, cos, sa, sb)
    dq = dq * ATT_SCALE
    dq0, dkn = [], []
    dkr = jnp.zeros_like(kr)
    dgq = jnp.zeros_like(gq)
    dgk = jnp.zeros_like(gk)
    for xh, ct in zip(_heads(q0, LANES), _heads(dq, LANES)):
        _, vjp = jax.vjp(head, xh, gq)
        dx, dg = vjp(ct)
        dq0.append(dx)
        dgq = dgq + dg
    for xh, ct in zip(_heads(kn, LANES), _heads(dk, LANES)):
        _, vjp = jax.vjp(head, xh + kr, gk)
        dx, dg = vjp(ct)
        dkn.append(dx)
        dkr = dkr + dx
        dgk = dgk + dg
    return jnp.concatenate(dq0, axis=1), jnp.concatenate(dkn, axis=1), dkr, dgq, dgk


def _b_act(cg, cv, dact, bg, bv):
    _, vjp = jax.vjp(_act, cg, cv, bg, bv)
    return vjp(dact)


def _b_mem_k(kk, dkm, g):
    dkk = []
    dg = jnp.zeros_like(g)
    for xh, ct in zip(_heads(kk, MEM_HEAD_DIM), _heads(dkm, MEM_HEAD_DIM)):
        _, vjp = jax.vjp(_rms, xh, g)
        dx, dgh = vjp(ct)
        dkk.append(dx)
        dg = dg + dgh
    return jnp.concatenate(dkk, axis=1), dg


def _b_mem_attn(dom, qm0, km, vm, g):
    dq0, dkm, dvm = [], [], []
    dg = jnp.zeros_like(g)
    for h, (xh, doh) in enumerate(zip(_heads(qm0, MEM_HEAD_DIM), _heads(dom, MEM_HEAD_DIM))):
        sl = slice(h * MEM_HEAD_DIM, (h + 1) * MEM_HEAD_DIM)
        kmh, vmh = km[:, sl], vm[:, sl].astype(BF16)
        qn, vjp = jax.vjp(_rms, xh, g)
        p = _mem_probs(qn, kmh)
        dob = doh.astype(BF16)
        dp = lax.dot_general(dob, vmh, NT_DIMS, preferred_element_type=F32)
        ds = (p * (dp - jnp.sum(dp * p, axis=-1, keepdims=True)) * MEM_SCALE).astype(BF16)
        dqn = jnp.dot(ds, kmh, preferred_element_type=F32)
        dkm.append(jnp.dot(ds.T, qn.astype(BF16), preferred_element_type=F32))
        dvm.append(jnp.dot(p.astype(BF16).T, dob, preferred_element_type=F32))
        dx, dgh = vjp(dqn)
        dq0.append(dx)
        dg = dg + dgh
    return (jnp.concatenate(dq0, axis=1), jnp.concatenate(dkm, axis=1),
            jnp.concatenate(dvm, axis=1), dg)


def _adamw(w, g, m, v):
    m = ADAM_B1 * m + (1.0 - ADAM_B1) * g
    v = ADAM_B2 * v + (1.0 - ADAM_B2) * jnp.square(g)
    m_hat = m / (1.0 - ADAM_B1 ** ADAM_STEP)
    v_hat = v / (1.0 - ADAM_B2 ** ADAM_STEP)
    delta = -ADAM_LR * (m_hat / (jnp.sqrt(v_hat) + ADAM_EPS) + ADAM_WD * w)
    return delta, m, v


def _adamw_small(parts, small, tiny, name):
    def body(*refs):
        p_ref, ins, outs = refs[0], refs[1:4 + 3 * len(TINY)], refs[4 + 3 * len(TINY):]
        me = 4 * lax.axis_index("x") + 2 * lax.axis_index("y") + lax.axis_index("c")
        groups = [(0, SMALL_ROWS)]
        groups += [(pl.multiple_of(base + me * rows, 8), rows) for base, (_, _, rows) in zip(TINY_BASE, TINY)]
        for k, (start, rows) in enumerate(groups):
            g = p_ref[0, pl.ds(start, rows), :]
            for d in range(1, N_DEV):
                g = g + p_ref[d, pl.ds(start, rows), :]
            w, m, v = [r[...] for r in ins[3 * k:3 * k + 3]]
            for ref, val in zip(outs[4 * k:4 * k + 4], (g,) + _adamw(w, g, m, v)):
                ref[...] = val

    args = list(small) + [t for grp in tiny for t in grp]
    out_shape = []
    for grp in [small] + list(tiny):
        out_shape += [jax.ShapeDtypeStruct(grp[0].shape, F32)] * 4
    return _call(body, name=name, out_shape=out_shape)(parts, *args)


def _conv_fwd(x, w, name):
    s, ch = x.shape
    kw = w.shape[0]
    halo = -(-(kw - 1) // 8) * 8
    r = CONV_ROWS
    n = s // r

    def chunk(window, wv):
        acc = jnp.zeros((r, LANES), F32)
        for k in range(kw):
            shift = kw - 1 - k
            sh = window if shift == 0 else pltpu.roll(window, shift, 0)
            acc = acc + sh[halo:halo + r] * wv[k:k + 1]
        return acc

    def body(x_ref, w_ref, y_ref):
        wv = w_ref[...]
        first = jnp.concatenate([jnp.zeros((halo, LANES), F32), x_ref[0:r]], axis=0)
        y_ref[0:r] = chunk(first, wv)

        def step(i, carry):
            base = pl.multiple_of(i * r, 8)
            y_ref[pl.ds(base, r)] = chunk(x_ref[pl.ds(base - halo, r + halo)], wv)
            return carry

        lax.fori_loop(1, n, step, 0)

    return _call(
        body, name=name, grid=(ch // LANES,),
        in_specs=[pl.BlockSpec((s, LANES), lambda c: (0, c)), pl.BlockSpec((kw, LANES), lambda c: (0, c))],
        out_specs=pl.BlockSpec((s, LANES), lambda c: (0, c)),
        out_shape=jax.ShapeDtypeStruct((s, ch), F32), compiler_params=_params("parallel"),
    )(x, w)


def _conv_bwd(dy, x, w, name):
    s, ch = x.shape
    kw = w.shape[0]
    halo = -(-(kw - 1) // 8) * 8
    r = CONV_ROWS
    n = s // r

    def dx_chunk(window, wv):
        acc = jnp.zeros((r, LANES), F32)
        for k in range(kw):
            shift = kw - 1 - k
            sh = window if shift == 0 else pltpu.roll(window, r + halo - shift, 0)
            acc = acc + sh[0:r] * wv[k:k + 1]
        return acc

    def dw_chunk(xwin, dyc, acc_ref):
        for k in range(kw):
            shift = kw - 1 - k
            sh = xwin if shift == 0 else pltpu.roll(xwin, shift, 0)
            prod = sh[halo:halo + r] * dyc
            acc_ref[k] += jnp.sum(prod.reshape(r // 8, 8, LANES), axis=0)

    def body(dy_ref, x_ref, w_ref, dx_ref, dw_ref, acc_ref):
        wv = w_ref[...]
        acc_ref[...] = jnp.zeros_like(acc_ref)
        xfirst = jnp.concatenate([jnp.zeros((halo, LANES), F32), x_ref[0:r]], axis=0)
        dw_chunk(xfirst, dy_ref[0:r], acc_ref)
        last = jnp.concatenate([dy_ref[s - r:s], jnp.zeros((halo, LANES), F32)], axis=0)
        dx_ref[s - r:s] = dx_chunk(last, wv)

        def step(i, carry):
            base = pl.multiple_of(i * r, 8)
            dw_chunk(x_ref[pl.ds(base - halo, r + halo)], dy_ref[pl.ds(base, r)], acc_ref)
            prev = pl.multiple_of((i - 1) * r, 8)
            dx_ref[pl.ds(prev, r)] = dx_chunk(dy_ref[pl.ds(prev, r + halo)], wv)
            return carry

        lax.fori_loop(1, n, step, 0)
        dw_ref[...] = jnp.sum(acc_ref[...], axis=1)

    spec = pl.BlockSpec((s, LANES), lambda c: (0, c))
    wspec = pl.BlockSpec((kw, LANES), lambda c: (0, c))
    return _call(
        body, name=name, grid=(ch // LANES,), in_specs=[spec, spec, wspec], out_specs=[spec, wspec],
        out_shape=[jax.ShapeDtypeStruct((s, ch), F32), jax.ShapeDtypeStruct((kw, ch), F32)],
        scratch_shapes=[pltpu.VMEM((kw, 8, LANES), F32)], compiler_params=_params("parallel"),
    )(dy, x, w)


def _chunk_mask(rows_are_queries):
    a = lax.broadcasted_iota(jnp.int32, (TQ, TQ), 0) // CHUNK
    b = lax.broadcasted_iota(jnp.int32, (TQ, TQ), 1) // CHUNK
    return (b <= a) if rows_are_queries else (a <= b)


def _head_lanes(hh):
    return slice(hh * LANES, (hh + 1) * LANES)


def _to_row(col):
    return jnp.broadcast_to(col, (TQ, LANES)).T[0:1, :]


def _flash_specs(s):
    width = HEADS_PER_STEP * LANES
    tile = pl.BlockSpec((TQ, width), lambda h, i: (i, h))
    whole = pl.BlockSpec((s, width), lambda h, i: (0, h))
    row_tile = pl.BlockSpec((HEADS_PER_STEP, 1, 1, TQ), lambda h, i: (h, i, 0, 0))
    row_whole = pl.BlockSpec((HEADS_PER_STEP, s // TQ, 1, TQ), lambda h, i: (h, 0, 0, 0))
    return tile, whole, row_tile, row_whole


def _flash_fwd(q, k, v, name):
    s = q.shape[0]
    nq = s // TQ

    def body(q_ref, k_ref, v_ref, o_ref, lse_ref, lse_row_ref):
        i = pl.program_id(1)
        qs = [q_ref[:, _head_lanes(hh)] for hh in range(HEADS_PER_STEP)]

        def step(j, carry, masked):
            base = pl.multiple_of(j * TQ, TQ)
            out = []
            for hh in range(HEADS_PER_STEP):
                m_prev, l_prev, acc = carry[hh]
                kj = k_ref[pl.ds(base, TQ), _head_lanes(hh)]
                vj = v_ref[pl.ds(base, TQ), _head_lanes(hh)]
                sc = lax.dot_general(qs[hh], kj, NT_DIMS, preferred_element_type=F32)
                if masked:
                    sc = jnp.where(_chunk_mask(True), sc, NEG)
                m_new = jnp.maximum(m_prev, jnp.max(sc, axis=-1, keepdims=True))
                alpha = jnp.exp(m_prev - m_new)
                p = jnp.exp(sc - m_new)
                l_new = alpha * l_prev + jnp.sum(p, axis=-1, keepdims=True)
                acc = acc * alpha + jnp.dot(p.astype(BF16), vj, preferred_element_type=F32)
                out.append((m_new, l_new, acc))
            return tuple(out)

        init = tuple((jnp.full((TQ, 1), NEG, F32), jnp.zeros((TQ, 1), F32), jnp.zeros((TQ, LANES), F32))
                     for _ in range(HEADS_PER_STEP))
        carry = lax.fori_loop(0, i, functools.partial(step, masked=False), init)
        for hh, (m_fin, l_fin, acc) in enumerate(step(i, carry, True)):
            o_ref[:, _head_lanes(hh)] = acc / l_fin
            lse = m_fin + jnp.log(l_fin)
            lse_ref[:, _head_lanes(hh)] = jnp.broadcast_to(lse, (TQ, LANES))
            lse_row_ref[hh, 0] = _to_row(lse)

    tile, whole, row_tile, _ = _flash_specs(s)
    wide = jax.ShapeDtypeStruct((s, MLA_HEADS * LANES), F32)
    return _call(
        body, name=name, grid=(MLA_HEADS // HEADS_PER_STEP, nq), in_specs=[tile, whole, whole],
        out_specs=[tile, tile, row_tile],
        out_shape=[wide, wide, jax.ShapeDtypeStruct((MLA_HEADS, nq, 1, TQ), F32)],
        compiler_params=_params("parallel", "arbitrary"),
    )(q, k, v)


def _flash_bwd_dq(q, k, v, do, o, lse, name):
    s = q.shape[0]
    nq = s // TQ

    def body(q_ref, k_ref, v_ref, do_ref, o_ref, lse_ref, dq_ref, delta_row_ref):
        i = pl.program_id(1)
        qs, dobs, deltas, lses = [], [], [], []
        for hh in range(HEADS_PER_STEP):
            dov = do_ref[:, _head_lanes(hh)]
            qs.append(q_ref[:, _head_lanes(hh)])
            dobs.append(dov.astype(BF16))
            deltas.append(jnp.sum(dov * o_ref[:, _head_lanes(hh)], axis=-1, keepdims=True))
            lses.append(lse_ref[:, _head_lanes(hh)][:, 0:1])

        def step(j, carry, masked):
            base = pl.multiple_of(j * TQ, TQ)
            out = []
            for hh in range(HEADS_PER_STEP):
                kj = k_ref[pl.ds(base, TQ), _head_lanes(hh)]
                vj = v_ref[pl.ds(base, TQ), _head_lanes(hh)]
                sc = lax.dot_general(qs[hh], kj, NT_DIMS, preferred_element_type=F32)
                if masked:
                    sc = jnp.where(_chunk_mask(True), sc, NEG)
                p = jnp.exp(sc - lses[hh])
                dp = lax.dot_general(dobs[hh], vj, NT_DIMS, preferred_element_type=F32)
                ds = (p * (dp - deltas[hh])).astype(BF16)
                out.append(carry[hh] + jnp.dot(ds, kj, preferred_element_type=F32))
            return tuple(out)

        init = tuple(jnp.zeros((TQ, LANES), F32) for _ in range(HEADS_PER_STEP))
        carry = lax.fori_loop(0, i, functools.partial(step, masked=False), init)
        for hh, dq in enumerate(step(i, carry, True)):
            dq_ref[:, _head_lanes(hh)] = dq
            delta_row_ref[hh, 0] = _to_row(deltas[hh])

    tile, whole, row_tile, _ = _flash_specs(s)
    return _call(
        body, name=name, grid=(MLA_HEADS // HEADS_PER_STEP, nq),
        in_specs=[tile, whole, whole, tile, tile, tile], out_specs=[tile, row_tile],
        out_shape=[jax.ShapeDtypeStruct((s, MLA_HEADS * LANES), F32),
                   jax.ShapeDtypeStruct((MLA_HEADS, nq, 1, TQ), F32)],
        compiler_params=_params("parallel", "arbitrary"),
    )(q, k, v, do, o, lse)


def _flash_bwd_dkv(q, k, v, do, lse_row, delta_row, name):
    s = q.shape[0]
    nq = s // TQ

    def body(q_ref, k_ref, v_ref, do_ref, lse_row_ref, delta_row_ref, dk_ref, dv_ref):
        j = pl.program_id(1)
        kjs = [k_ref[:, _head_lanes(hh)] for hh in range(HEADS_PER_STEP)]
        vjs = [v_ref[:, _head_lanes(hh)] for hh in range(HEADS_PER_STEP)]

        def step(i, carry, masked):
            base = pl.multiple_of(i * TQ, TQ)
            out = []
            for hh in range(HEADS_PER_STEP):
                dk, dv = carry[hh]
                qi = q_ref[pl.ds(base, TQ), _head_lanes(hh)]
                dob = do_ref[pl.ds(base, TQ), _head_lanes(hh)].astype(BF16)
                sc_t = lax.dot_general(kjs[hh], qi, NT_DIMS, preferred_element_type=F32)
                if masked:
                    sc_t = jnp.where(_chunk_mask(False), sc_t, NEG)
                p_t = jnp.exp(sc_t - lse_row_ref[hh, i])
                dv = dv + jnp.dot(p_t.astype(BF16), dob, preferred_element_type=F32)
                dp_t = lax.dot_general(vjs[hh], dob, NT_DIMS, preferred_element_type=F32)
                ds_t = (p_t * (dp_t - delta_row_ref[hh, i])).astype(BF16)
                dk = dk + jnp.dot(ds_t, qi, preferred_element_type=F32)
                out.append((dk, dv))
            return tuple(out)

        zero = jnp.zeros((TQ, LANES), F32)
        carry = step(j, tuple((zero, zero) for _ in range(HEADS_PER_STEP)), True)
        carry = lax.fori_loop(j + 1, nq, functools.partial(step, masked=False), carry)
        for hh, (dk, dv) in enumerate(carry):
            dk_ref[:, _head_lanes(hh)] = dk
            dv_ref[:, _head_lanes(hh)] = dv

    tile, whole, _, row_whole = _flash_specs(s)
    return _call(
        body, name=name, grid=(MLA_HEADS // HEADS_PER_STEP, nq),
        in_specs=[whole, tile, tile, whole, row_whole, row_whole], out_specs=[tile, tile],
        out_shape=[jax.ShapeDtypeStruct((s, MLA_HEADS * LANES), F32)] * 2,
        compiler_params=_params("parallel", "arbitrary"),
    )(q, k, v, do, lse_row, delta_row)


def _side_by_side(g):
    return g.transpose(1, 0, 2).reshape(g.shape[1], N_DEV * g.shape[2])


def _col_shards(g):
    return g.reshape(g.shape[0], N_DEV, g.shape[1] // N_DEV).transpose(1, 0, 2)


def _pad_last(v, to):
    return jnp.pad(v, [(0, 0)] * (v.ndim - 1) + [(0, to - v.shape[-1])])


def _tiny_rows(v, rows):
    flat = v.reshape(v.shape[:-2] + (-1,))
    return _pad_last(flat, rows * LANES).reshape(v.shape[:-2] + (rows, LANES))


def _pack_small(vals):
    parts = []
    for (n, size), pad in zip(SMALL, SMALL_PAD):
        parts.append(jnp.pad(vals[n].reshape(-1), (0, pad - size)))
    flat = jnp.concatenate(parts)
    return jnp.pad(flat, (0, SMALL_ROWS * LANES - flat.shape[0])).reshape(SMALL_ROWS, LANES)


def _unpack_small(packed):
    flat = packed.reshape(-1)
    out, off = {}, 0
    for (n, size), pad in zip(SMALL, SMALL_PAD):
        out[n] = flat[off:off + size].reshape(1, size)
        off += pad
    return out


def _pad_heads(w, per_head, axis):
    shape = list(w.shape)
    shape[axis:axis + 1] = [MLA_HEADS, per_head]
    w = w.reshape(shape)
    pad = [(0, 0)] * len(shape)
    pad[axis + 1] = (0, LANES - per_head)
    w = jnp.pad(w, pad)
    shape[axis:axis + 2] = [MLA_HEADS * LANES]
    return w.reshape(shape)


def _unpad_heads(w, per_head, axis):
    shape = list(w.shape)
    shape[axis:axis + 1] = [MLA_HEADS, LANES]
    w = w.reshape(shape)
    w = lax.slice_in_dim(w, 0, per_head, axis=axis + 1)
    shape[axis:axis + 2] = [MLA_HEADS * per_head]
    return w.reshape(shape)


def _row(v, pad_to=None):
    v = v.reshape(1, -1)
    if pad_to is not None:
        v = jnp.pad(v, ((0, 0), (0, pad_to - v.shape[1])))
    return v


def kernel(x, mem, positions, mix_norm_g, w_in, b_conv_in, w_conv_dw, b_conv_dw, conv_ln_g, conv_ln_b, q_lat_norm_g, w_uq, kv_lat_norm_g, w_ukv, q_norm_g, k_norm_g, w_out, mem_norm_x_g, mem_norm_m_g, w_mem_q, w_mem_kv, mem_q_norm_g, mem_k_norm_g, w_mem_o, ffn_norm_g, w_up, w_ffn_dw, b_ffn_dw, w_down, loss_target, m_mix_norm_g, m_w_in, m_b_conv_in, m_w_conv_dw, m_b_conv_dw, m_conv_ln_g, m_conv_ln_b, m_q_lat_norm_g, m_w_uq, m_kv_lat_norm_g, m_w_ukv, m_q_norm_g, m_k_norm_g, m_w_out, m_mem_norm_x_g, m_mem_norm_m_g, m_w_mem_q, m_w_mem_kv, m_mem_q_norm_g, m_mem_k_norm_g, m_w_mem_o, m_ffn_norm_g, m_w_up, m_w_ffn_dw, m_b_ffn_dw, m_w_down, v_mix_norm_g, v_w_in, v_b_conv_in, v_w_conv_dw, v_b_conv_dw, v_conv_ln_g, v_conv_ln_b, v_q_lat_norm_g, v_w_uq, v_kv_lat_norm_g, v_w_ukv, v_q_norm_g, v_k_norm_g, v_w_out, v_mem_norm_x_g, v_mem_norm_m_g, v_w_mem_q, v_w_mem_kv, v_mem_q_norm_g, v_mem_k_norm_g, v_w_mem_o, v_ffn_norm_g, v_w_up, v_w_ffn_dw, v_b_ffn_dw, v_w_down):
    a = dict(locals())
    seq = x.shape[1]
    xs = x.reshape(seq, D_MODEL)
    mems = mem.reshape(-1, D_MODEL)
    target = loss_target.reshape(seq, D_MODEL)

    names = [n for n, _ in BIG] + [n for n, _, _ in TINY]
    mine = [a[n][0].astype(BF16) for n, _ in BIG] + [a[n][0] for n, _, _ in TINY]
    wg = dict(zip(names, _all_gather(mine, "ag_weights")))
    wi = _side_by_side(wg["w_in"])
    s3 = 2 * CONV_CH + MLA_Q_RANK + MLA_KV_RANK
    w_in_p = jnp.concatenate([
        wi[:, :s3], jnp.zeros((D_MODEL, MLA_NOPE), BF16), wi[:, s3:],
        jnp.zeros((D_MODEL, LANES - MLA_QK), BF16)], axis=1)
    w_uq_p = _side_by_side(_pad_last(wg["w_uq"], LANES))
    w_uk_p = _side_by_side(_pad_last(wg["w_ukv"][:, :, :MLA_NOPE], LANES))
    w_uv_p = _side_by_side(_pad_last(wg["w_ukv"][:, :, MLA_NOPE:], LANES))
    w_out = wg["w_out"].reshape(D_MODEL, D_MODEL)
    w_out_u = w_out[:CONV_CH]
    w_out_a = _pad_heads(w_out[CONV_CH:], MLA_V, 0)
    w_mq, w_mo = wg["w_mem_q"].reshape(D_MODEL, D_MODEL), wg["w_mem_o"].reshape(D_MODEL, D_MODEL)
    w_mkv = _side_by_side(wg["w_mem_kv"])
    w_up_p = _pad_last(wg["w_up"], FF_PAD)
    w_dn = jnp.pad(wg["w_down"].reshape(4, FF_SHARD, D_MODEL),
                   ((0, 0), (0, FF_PAD - FF_SHARD), (0, 0))).reshape(D_FF_PAD, D_MODEL)
    w_cdw = _side_by_side(wg["w_conv_dw"])
    w_fdw = _pad_last(wg["w_ffn_dw"], FF_PAD)
    w_fdw_g = w_fdw[:4].transpose(1, 0, 2).reshape(FFN_CONV_WIDTH, D_FF_PAD)
    w_fdw_v = w_fdw[4:].transpose(1, 0, 2).reshape(FFN_CONV_WIDTH, D_FF_PAD)

    g_mix, g_qlat, g_kvlat = _row(mix_norm_g), _row(q_lat_norm_g), _row(kv_lat_norm_g)
    b_in = _row(b_conv_in)
    b_in_a, b_in_g = b_in[:, :CONV_CH], b_in[:, CONV_CH:]
    b_cdw, ln_g, ln_b = _row(b_conv_dw), _row(conv_ln_g), _row(conv_ln_b)
    g_q, g_k = _row(q_norm_g, LANES), _row(k_norm_g, LANES)
    g_memx, g_memm = _row(mem_norm_x_g), _row(mem_norm_m_g)
    g_mq, g_mk, g_ffn = _row(mem_q_norm_g), _row(mem_k_norm_g), _row(ffn_norm_g)
    b_f = _pad_last(b_ffn_dw.reshape(N_DEV, FF_SHARD), FF_PAD)
    b_f_g, b_f_v = b_f[:4].reshape(1, D_FF_PAD), b_f[4:].reshape(1, D_FF_PAD)

    freq = ROPE_THETA ** (-jnp.arange(0, MLA_ROPE, 2, dtype=F32) / MLA_ROPE)
    inv_freq = jnp.concatenate([jnp.zeros((MLA_NOPE,), F32), freq, freq,
                                jnp.zeros((LANES - MLA_QK,), F32)]).reshape(1, LANES)
    cos, sin_a, sin_b = _rowwise(_f_rope_tab, [positions.reshape(seq, 1)], [inv_freq],
                                 [(LANES, F32)] * 3, [], name="rope_tables")

    (h1,) = _rowwise(_f_rms, [xs], [g_mix], [(D_MODEL, BF16)], [], name="rms_mix")
    z = _mm(h1, w_in_p, name="mm_in")
    z_rows = [(z, CONV_CH, 0, 0), (z, CONV_CH, 1, 0), (z, MLA_Q_RANK, 4, 0), (z, MLA_KV_RANK, 10, 0)]
    z_kr = (z, LANES, 11, 0)
    u0, cqn, ckvn = _rowwise(
        _mix_pre, z_rows, [b_in_a, b_in_g, g_qlat, g_kvlat],
        [(CONV_CH, F32), (MLA_Q_RANK, BF16), (MLA_KV_RANK, BF16)], [], name="mix_pre")
    c1 = _conv_fwd(u0, w_cdw, "conv31_fwd")
    (u,) = _rowwise(lambda c, b, g, bb: (_ln_silu(c, b, g, bb),), [c1], [b_cdw, ln_g, ln_b],
                    [(CONV_CH, BF16)], [], name="ln_silu")
    q0 = _mm(cqn, w_uq_p, name="mm_uq")
    kn0 = _mm(ckvn, w_uk_p, name="mm_uk")
    v0 = _mm(ckvn, w_uv_p, out_dtype=BF16, name="mm_uv")
    qk_rows = [q0, kn0, z_kr, cos, sin_a, sin_b]
    qh, kh = _rowwise(_f_qk_prep, qk_rows, [g_q, g_k],
                      [(MLA_HEADS * LANES, BF16)] * 2, [], name="qk_prep")
    attn, lse, lse_row = _flash_fwd(qh, kh, v0, "flash_fwd")
    x1 = _mm(u, w_out_u, add=xs, name="mm_out_u")
    x1 = _mm(attn, w_out_a, add=x1, name="mm_out_a")

    (hq,) = _rowwise(_f_rms, [x1], [g_memx], [(D_MODEL, BF16)], [], name="rms_memx")
    (hm,) = _rowwise(_f_rms, [mems], [g_memm], [(D_MODEL, BF16)], [], name="rms_memm", tm=mems.shape[0])
    qm0 = _mm(hq, w_mq, name="mm_memq")
    kvm0 = _mm(hm, w_mkv, name="mm_memkv", tm=mems.shape[0])
    (km,) = _rowwise(_f_mem_k, [(kvm0, D_MODEL, 0, 0)], [g_mk], [(D_MODEL, BF16)], [],
                     name="mem_k", tm=mems.shape[0])
    vm = kvm0[:, D_MODEL:]
    (om,) = _rowwise(_f_mem_attn, [qm0], [km, vm, g_mq], [(D_MODEL, BF16)], [], name="mem_attn")
    x2 = _mm(om, w_mo, add=x1, name="mm_memo")

    (h3,) = _rowwise(_f_rms, [x2], [g_ffn], [(D_MODEL, BF16)], [], name="rms_ffn")
    up_g = _mm(h3, w_up_p, b_shards=(0, 4), name="mm_up_g")
    up_v = _mm(h3, w_up_p, b_shards=(4, 4), name="mm_up_v")
    cg = _conv_fwd(up_g, w_fdw_g, "conv3_g_fwd")
    cv = _conv_fwd(up_v, w_fdw_v, "conv3_v_fwd")
    (act,) = _rowwise(lambda g_, v_, bg, bv: (_act(g_, v_, bg, bv),), [cg, cv], [b_f_g, b_f_v],
                      [(D_FF_PAD, BF16)], [], name="ffn_act", tm=256)
    y = _mm(act, w_dn, add=x2, name="mm_down")
    dy, sq = _rowwise(_f_loss, [y, target], [], [(D_MODEL, F32)], [(1, D_MODEL)], name="loss")
    loss = lax.psum(0.5 * jnp.sum(sq) / D_MODEL, ("x", "y", "c"))

    gw, gs, gt = {}, {}, {}
    gw_dn = _mm_tn(act, dy, name="tn_down").reshape(4, FF_PAD, D_MODEL)
    gw["w_down"] = gw_dn[:, :FF_SHARD].reshape(N_DEV, FF_SHARD // 2, D_MODEL)
    dact = _mm(dy, w_dn, trans_b=True, name="mm_down_t")
    dcg, dcv, db_g, db_v = _rowwise(_b_act, [cg, cv, dact], [b_f_g, b_f_v],
                                    [(D_FF_PAD, F32)] * 2, [(1, D_FF_PAD)] * 2, name="ffn_act_bwd", tm=256)
    db_f = jnp.concatenate([db_g.reshape(4, FF_PAD), db_v.reshape(4, FF_PAD)], axis=0)
    gs["b_ffn_dw"] = db_f[:, :FF_SHARD].reshape(1, 2 * D_FF)
    dup_g, dwf_g = _conv_bwd(dcg, up_g, w_fdw_g, "conv3_g_bwd")
    dup_v, dwf_v = _conv_bwd(dcv, up_v, w_fdw_v, "conv3_v_bwd")
    dwf = jnp.concatenate([dwf_g.reshape(FFN_CONV_WIDTH, 4, FF_PAD), dwf_v.reshape(FFN_CONV_WIDTH, 4, FF_PAD)], axis=1)
    gt["w_ffn_dw"] = dwf[:, :, :FF_SHARD].transpose(1, 0, 2)
    gw_up = jnp.concatenate([_mm_tn(h3, dup_g, shard_cols=FF_PAD, name="tn_up_g"),
                             _mm_tn(h3, dup_v, shard_cols=FF_PAD, name="tn_up_v")], axis=0)
    gw["w_up"] = gw_up[:, :, :FF_SHARD]
    dh3 = _mm(dup_g, w_up_p, trans_b=True, b_shards=(0, 4), name="mm_up_g_t")
    dh3 = _mm(dup_v, w_up_p, trans_b=True, b_shards=(4, 4), add=dh3, name="mm_up_v_t")
    dx2, gs["ffn_norm_g"] = _rowwise(_b_rms, [x2, dh3, dy], [g_ffn], [(D_MODEL, F32)], [(1, D_MODEL)],
                                     name="rms_ffn_bwd")

    gw["w_mem_o"] = _mm_tn(om, dx2, name="tn_memo").reshape(N_DEV, -1, D_MODEL)
    dom = _mm(dx2, w_mo, trans_b=True, name="mm_memo_t")
    n_mem = mems.shape[0]
    dqm0, dkm, dvm, gs["mem_q_norm_g"] = _rowwise(
        _b_mem_attn, [dom, qm0], [km, vm, g_mq], [(D_MODEL, F32)],
        [(n_mem, D_MODEL), (n_mem, D_MODEL), (1, MEM_HEAD_DIM)], name="mem_attn_bwd")
    gw["w_mem_q"] = _mm_tn(hq, dqm0, name="tn_memq").reshape(N_DEV, -1, D_MODEL)
    dhq = _mm(dqm0, w_mq, trans_b=True, name="mm_memq_t")
    dx1, gs["mem_norm_x_g"] = _rowwise(_b_rms, [x1, dhq, dx2], [g_memx], [(D_MODEL, F32)],
                                       [(1, D_MODEL)], name="rms_memx_bwd")
    dkk, gs["mem_k_norm_g"] = _rowwise(_b_mem_k, [(kvm0, D_MODEL, 0, 0), dkm], [g_mk],
                                       [(D_MODEL, F32)], [(1, MEM_HEAD_DIM)], name="mem_k_bwd", tm=n_mem)
    dkvm0 = jnp.concatenate([dkk, dvm], axis=1)
    gw["w_mem_kv"] = _col_shards(_mm_tn(hm, dkvm0, name="tn_memkv", ts=n_mem))
    dhm = _mm(dkvm0, w_mkv, trans_b=True, name="mm_memkv_t", tm=n_mem)
    _, gs["mem_norm_m_g"] = _rowwise(_b_rms_nores, [mems, dhm], [g_memm], [(D_MODEL, F32)],
                                     [(1, D_MODEL)], name="rms_memm_bwd", tm=n_mem)

    gw_out_u = _mm_tn(u, dx1, name="tn_out_u")
    gw_out_a = _mm_tn(attn, dx1, name="tn_out_a")
    gw["w_out"] = jnp.concatenate([gw_out_u, _unpad_heads(gw_out_a, MLA_V, 0)], axis=0).reshape(N_DEV, -1, D_MODEL)
    du = _mm(dx1, w_out_u, trans_b=True, name="mm_out_u_t")
    dattn = _mm(dx1, w_out_a, trans_b=True, name="mm_out_a_t")
    dc1, gs["b_conv_dw"], gs["conv_ln_g"], gs["conv_ln_b"] = _rowwise(
        _b_ln_silu, [c1, du], [b_cdw, ln_g, ln_b], [(CONV_CH, F32)], [(1, CONV_CH)] * 3, name="ln_silu_bwd")
    du0, g_cdw = _conv_bwd(dc1, u0, w_cdw, "conv31_bwd")
    gt["w_conv_dw"] = _col_shards(g_cdw)
    dqh, delta_row = _flash_bwd_dq(qh, kh, v0, dattn, attn, lse, "flash_bwd_dq")
    dkh, dv0 = _flash_bwd_dkv(qh, kh, v0, dattn, lse_row, delta_row, "flash_bwd_dkv")
    dq0, dkn0, dkr, dgq, dgk = _rowwise(
        _b_qk_prep, qk_rows + [dqh, dkh], [g_q, g_k],
        [(MLA_HEADS * LANES, F32)] * 2 + [(LANES, F32)], [(1, LANES)] * 2, name="qk_prep_bwd")
    gs["q_norm_g"], gs["k_norm_g"] = dgq[:, :MLA_QK], dgk[:, :MLA_QK]
    gw["w_uq"] = _col_shards(_mm_tn(cqn, dq0, name="tn_uq"))[:, :, :MLA_QK]
    g_uk = _col_shards(_mm_tn(ckvn, dkn0, name="tn_uk"))[:, :, :MLA_NOPE]
    g_uv = _col_shards(_mm_tn(ckvn, dv0, name="tn_uv"))[:, :, :MLA_V]
    gw["w_ukv"] = jnp.concatenate([g_uk, g_uv], axis=2)
    dcqn = _mm(dq0, w_uq_p, trans_b=True, name="mm_uq_t")
    dckvn = _mm(dkn0, w_uk_p, trans_b=True, name="mm_uk_t")
    dckvn = _mm(dv0, w_uv_p, trans_b=True, add=dckvn, name="mm_uv_t")
    dz, dba, dbg, gs["q_lat_norm_g"], gs["kv_lat_norm_g"] = _rowwise(
        _b_mix_pre, z_rows + [du0, dcqn, dckvn, dkr], [b_in_a, b_in_g, g_qlat, g_kvlat],
        [(IN_PAD, F32)], [(1, CONV_CH)] * 2 + [(1, MLA_Q_RANK), (1, MLA_KV_RANK)], name="mix_pre_bwd")
    gs["b_conv_in"] = jnp.concatenate([dba, dbg], axis=1)
    gw_in = _mm_tn(h1, dz, name="tn_in")
    gw["w_in"] = _col_shards(jnp.concatenate([gw_in[:, :s3], gw_in[:, s3 + MLA_NOPE:s3 + MLA_QK]], axis=1))
    dh1 = _mm(dz, w_in_p, trans_b=True, name="mm_in_t")
    dx, gs["mix_norm_g"] = _rowwise(_b_rms, [xs, dh1, dx1], [g_mix], [(D_MODEL, F32)], [(1, D_MODEL)],
                                    name="rms_mix_bwd")

    big = [n for n, _ in BIG]
    grads = [gw[n] for n in big]
    pos = jnp.stack([2 * lax.axis_index("x") + lax.axis_index("y"), lax.axis_index("c")]).astype(jnp.int32)
    gots = _swap_sibling(grads, "rs_sibling")
    sums = _rs_add(grads, gots, pos, "rs_add")
    recvs = _swap_chips(sums, "rs_chips")
    flat = _adamw_big(sums, recvs, [a[n] for n in big], [a["m_" + n] for n in big],
                      [a["v_" + n] for n in big], pos, "adamw_big")
    res = [{n: flat[4 * i + k] for i, n in enumerate(big)} for k in range(4)]

    part = jnp.concatenate(
        [_pack_small(gs)] + [_tiny_rows(gt[n], rows).reshape(N_DEV * rows, LANES) for n, _, rows in TINY], axis=0)
    (parts,) = _all_gather([part], "ag_small_grads")
    small_in = [_pack_small({n: a[p + n] for n, _ in SMALL}) for p in ("", "m_", "v_")]
    tiny_in = [[_tiny_rows(a[p + n][0], rows) for p in ("", "m_", "v_")] for n, _, rows in TINY]
    flat = _adamw_small(parts, small_in, tiny_in, "adamw_small")
    for k in range(4):
        res[k].update(_unpack_small(flat[k]))
        for i, (n, shape, _) in enumerate(TINY):
            res[k][n] = flat[4 * (i + 1) + k].reshape(-1)[:math.prod(shape)].reshape((1,) + shape)

    return (loss, dx.reshape(1, seq, D_MODEL), *[res[k][n] for k in range(4) for n in WEIGHTS])
```

```python
import functools
import math

import jax
import jax.numpy as jnp
from jax import lax
from jax.experimental import pallas as pl
from jax.experimental.pallas import tpu as pltpu

F32 = jnp.float32
BF16 = jnp.bfloat16
EPS = 1e-6
LANES = 128
N_DEV = 8
D_MODEL = 1024
CONV_CH = 512
CONV_WIDTH = 31
MLA_HEADS = 8
MLA_NOPE = 64
MLA_ROPE = 32
MLA_V = 64
MLA_QK = MLA_NOPE + MLA_ROPE
MLA_Q_RANK = 256
MLA_KV_RANK = 128
ROPE_THETA = 10000.0
IN_COLS = 2 * CONV_CH + MLA_Q_RANK + MLA_KV_RANK + MLA_ROPE
IN_PAD = 2 * CONV_CH + MLA_Q_RANK + MLA_KV_RANK + LANES
MEM_HEADS = 4
MEM_HEAD_DIM = 256
D_FF = 2816
FFN_CONV_WIDTH = 3
CHUNK = 64
ATT_SCALE = 1.0 / math.sqrt(MLA_QK)
MEM_SCALE = 1.0 / math.sqrt(MEM_HEAD_DIM)
ADAM_LR, ADAM_B1, ADAM_B2, ADAM_EPS, ADAM_WD, ADAM_STEP = 0.001, 0.9, 0.999, 1e-08, 0.01, 10

TM = 512
TQ = 512
HEADS_PER_STEP = 2
CONV_ROWS = 256
NEG = -1e30
VMEM_LIMIT = 56 * 1024 * 1024

MESH = pl.DeviceIdType.MESH
ANY = pl.BlockSpec(memory_space=pl.ANY)
NT_DIMS = (((1,), (1,)), ((), ()))

BIG = [
    ("w_in", (1024, 180)), ("w_uq", (256, 96)), ("w_ukv", (128, 128)), ("w_out", (128, 1024)),
    ("w_mem_q", (128, 1024)), ("w_mem_kv", (1024, 256)), ("w_mem_o", (128, 1024)),
    ("w_up", (1024, 704)), ("w_down", (352, 1024)),
]
TINY = [("w_conv_dw", (31, 64), 16), ("w_ffn_dw", (3, 704), 24)]
ROW_STEPS = 4
FF_SHARD = D_FF // 4
FF_PAD = 768
D_FF_PAD = 4 * FF_PAD
SMALL = [
    ("mix_norm_g", 1024), ("b_conv_in", 1024), ("b_conv_dw", 512), ("conv_ln_g", 512),
    ("conv_ln_b", 512), ("q_lat_norm_g", 256), ("kv_lat_norm_g", 128), ("q_norm_g", 96),
    ("k_norm_g", 96), ("mem_norm_x_g", 1024), ("mem_norm_m_g", 1024), ("mem_q_norm_g", 256),
    ("mem_k_norm_g", 256), ("ffn_norm_g", 1024), ("b_ffn_dw", 5632),
]
WEIGHTS = [
    "mix_norm_g", "w_in", "b_conv_in", "w_conv_dw", "b_conv_dw", "conv_ln_g", "conv_ln_b",
    "q_lat_norm_g", "w_uq", "kv_lat_norm_g", "w_ukv", "q_norm_g", "k_norm_g", "w_out",
    "mem_norm_x_g", "mem_norm_m_g", "w_mem_q", "w_mem_kv", "mem_q_norm_g", "mem_k_norm_g",
    "w_mem_o", "ffn_norm_g", "w_up", "w_ffn_dw", "b_ffn_dw", "w_down",
]


SMALL_PAD = [(-(-n // LANES)) * LANES for _, n in SMALL]
SMALL_ROWS = -(-sum(SMALL_PAD) // (8 * LANES)) * 8
TINY_BASE = [SMALL_ROWS + N_DEV * sum(r for _, _, r in TINY[:i]) for i in range(len(TINY))]
PART_ROWS = SMALL_ROWS + N_DEV * sum(r for _, _, r in TINY)


def _call(body, **kw):
    return pl.pallas_call(body, **kw)


def _params(*sem):
    return pltpu.CompilerParams(dimension_semantics=sem, vmem_limit_bytes=VMEM_LIMIT)


class _Exchange:
    def __init__(self, inputs, out_shape, scratch, start, finish):
        self.inputs, self.out_shape, self.scratch = list(inputs), list(out_shape), list(scratch)
        self.start, self.finish = start, finish


def _run_exchange(ex, name):
    n_in, n_out = len(ex.inputs), len(ex.out_shape)

    def body(*refs):
        parts = refs[:n_in], refs[n_in:n_in + n_out], refs[n_in + n_out:]
        ex.start(*parts)
        ex.finish(*parts)

    return _call(body, name=name, out_shape=ex.out_shape, in_specs=[ANY] * n_in,
                 out_specs=[ANY] * n_out, scratch_shapes=ex.scratch)(*ex.inputs)


def _plan_all_gather(xs):
    n = len(xs)

    def copies(x_refs, out_refs, sems):
        send_sems, recv_sems, local_sems = sems
        x, y, c = lax.axis_index("x"), lax.axis_index("y"), lax.axis_index("c")
        me, sibling = (x, y, c), (x, y, 1 - c)
        chips = [(1 - x, y), (x, 1 - y), (1 - x, 1 - y)]

        def slot(o, px, py, pc):
            return out_refs[o].at[4 * px + 2 * py + pc]

        def copy(o, k, block, to, src=None):
            return pltpu.make_async_remote_copy(
                src_ref=slot(o, *block) if src is None else src, dst_ref=slot(o, *block),
                send_sem=send_sems.at[o, k], recv_sem=recv_sems.at[o, k],
                device_id=to, device_id_type=MESH)

        mine = [pltpu.make_async_copy(x_refs[o], slot(o, *me), local_sems.at[o]) for o in range(n)]
        first = [copy(o, 0, me, sibling, src=x_refs[o]) for o in range(n)]
        first += [copy(o, 1 + j, me, (*chip, c), src=x_refs[o])
                  for j, chip in enumerate(chips) for o in range(n)]
        return me, sibling, chips, copy, mine, first

    def start(x_refs, out_refs, sems):
        _, _, _, _, mine, first = copies(x_refs, out_refs, sems)
        for cp in mine + first:
            cp.start()

    def finish(x_refs, out_refs, sems):
        me, sibling, chips, copy, mine, first = copies(x_refs, out_refs, sems)
        c = me[2]
        passed = []
        for j, chip in enumerate(chips):
            for o in range(n):
                copy(o, 1 + j, (*chip, c), me).wait_recv()
                passed.append(copy(o, 4 + j, (*chip, c), sibling))
                passed[-1].start()
        for o in range(n):
            copy(o, 0, sibling, me).wait_recv()
        for j, chip in enumerate(chips):
            for o in range(n):
                copy(o, 4 + j, (*chip, 1 - c), me).wait_recv()
        for cp in first + passed:
            cp.wait_send()
        for cp in mine:
            cp.wait()

    return _Exchange(
        xs, [jax.ShapeDtypeStruct((N_DEV,) + v.shape, v.dtype) for v in xs],
        [pltpu.SemaphoreType.DMA((n, 7)), pltpu.SemaphoreType.DMA((n, 7)), pltpu.SemaphoreType.DMA((n,))],
        start, finish)


def _all_gather(xs, name):
    return _run_exchange(_plan_all_gather(xs), name)


def _swap_sibling(grads, name):
    n = len(grads)

    def body(*refs):
        g_refs, got_refs = refs[:n], refs[n:2 * n]
        send_sems, recv_sems = refs[2 * n:]
        x, y, c = lax.axis_index("x"), lax.axis_index("y"), lax.axis_index("c")
        copies = [
            pltpu.make_async_remote_copy(
                src_ref=g_refs[o].at[2 * chip + 1 - c], dst_ref=got_refs[o].at[chip],
                send_sem=send_sems.at[o, chip], recv_sem=recv_sems.at[o, chip],
                device_id=(x, y, 1 - c), device_id_type=MESH)
            for o in range(n) for chip in range(4)]
        for cp in copies:
            cp.start()
        for cp in copies:
            cp.wait()

    return _call(
        body, name=name,
        out_shape=[jax.ShapeDtypeStruct((4,) + g.shape[1:], g.dtype) for g in grads],
        in_specs=[ANY] * n, out_specs=[ANY] * n,
        scratch_shapes=[pltpu.SemaphoreType.DMA((n, 4)), pltpu.SemaphoreType.DMA((n, 4))],
    )(*grads)


def _plan_swap_chips(sums):
    n = len(sums)

    def copies(a_refs, r_refs, sems):
        send_sems, recv_sems = sems
        x, y, c = lax.axis_index("x"), lax.axis_index("y"), lax.axis_index("c")
        peers = [(x, 1 - y), (1 - x, y), (1 - x, 1 - y)]
        return [
            pltpu.make_async_remote_copy(
                src_ref=a_refs[o].at[2 * px + py], dst_ref=r_refs[o].at[k],
                send_sem=send_sems.at[o, k], recv_sem=recv_sems.at[o, k],
                device_id=(px, py, c), device_id_type=MESH)
            for k, (px, py) in enumerate(peers) for o in range(n)]

    def start(a_refs, r_refs, sems):
        for cp in copies(a_refs, r_refs, sems):
            cp.start()

    def finish(a_refs, r_refs, sems):
        for cp in copies(a_refs, r_refs, sems):
            cp.wait()

    return _Exchange(
        sums, [jax.ShapeDtypeStruct((3,) + a.shape[1:], a.dtype) for a in sums],
        [pltpu.SemaphoreType.DMA((n, 3)), pltpu.SemaphoreType.DMA((n, 3))], start, finish)


def _shard_block(shape):
    return (None, shape[-2] // ROW_STEPS, shape[-1])


def _rs_add(grads, gots, pos, name):
    n = len(grads)

    def body(pos_ref, *refs):
        for g_ref, t_ref, o_ref in zip(refs[:n], refs[n:2 * n], refs[2 * n:]):
            o_ref[...] = g_ref[...] + t_ref[...]

    in_specs = [pl.BlockSpec(_shard_block(g.shape), lambda a, t, pos: (2 * a + pos[1], t, 0)) for g in grads]
    in_specs += [pl.BlockSpec(_shard_block(g.shape), lambda a, t, pos: (a, t, 0)) for g in gots]
    return _call(
        body, name=name,
        grid_spec=pltpu.PrefetchScalarGridSpec(
            num_scalar_prefetch=1, grid=(4, ROW_STEPS), in_specs=in_specs,
            out_specs=[pl.BlockSpec(_shard_block(g.shape), lambda a, t, pos: (a, t, 0)) for g in gots]),
        out_shape=[jax.ShapeDtypeStruct(g.shape, g.dtype) for g in gots],
        compiler_params=_params("arbitrary", "arbitrary"),
    )(pos, *grads, *gots)


def _adamw_big(sums, recvs, ws, ms, vs, pos, name):
    n = len(sums)

    def body(pos_ref, *refs):
        ins, outs = refs[:7 * n], refs[7 * n:]
        for o in range(n):
            own, r1, r2, r3, w, m, v = [r[...] for r in ins[7 * o:7 * o + 7]]
            g = ((own + r1) + r2) + r3
            for ref, val in zip(outs[4 * o:4 * o + 4], (g,) + _adamw(w, g, m, v)):
                ref[...] = val

    in_specs, args, out_specs, out_shape = [], [], [], []
    for s_, r_, w_, m_, v_ in zip(sums, recvs, ws, ms, vs):
        blk = _shard_block(w_.shape)
        in_specs.append(pl.BlockSpec(blk, lambda t, pos: (pos[0], t, 0)))
        in_specs += [pl.BlockSpec(blk, lambda t, pos, k=k: (k, t, 0)) for k in range(3)]
        in_specs += [pl.BlockSpec(blk, lambda t, pos: (0, t, 0))] * 3
        args += [s_, r_, r_, r_, w_, m_, v_]
        out_specs += [pl.BlockSpec(blk, lambda t, pos: (0, t, 0))] * 4
        out_shape += [jax.ShapeDtypeStruct(w_.shape, F32)] * 4
    return _call(
        body, name=name,
        grid_spec=pltpu.PrefetchScalarGridSpec(
            num_scalar_prefetch=1, grid=(ROW_STEPS,), in_specs=in_specs, out_specs=out_specs),
        out_shape=out_shape, compiler_params=_params("arbitrary"),
    )(pos, *args)


def _tile(n, prefs):
    for t in prefs:
        if n % t == 0:
            return t
    return n


def _mm(a, b, *, name, trans_b=False, add=None, out_dtype=F32, tm=TM, b_shards=None):
    m, k = a.shape
    if b_shards is None:
        n = b.shape[0] if trans_b else b.shape[1]
        tn = _tile(n, (1536, 1408, 1024, 768, 512, 256, 128))
        tk = _tile(k, (1408, 1024, 768, 512, 256, 128))
    elif trans_b:
        n, tn, tk = b.shape[1], b.shape[1], b.shape[2]
    else:
        n, tn, tk = b_shards[1] * b.shape[2], b.shape[2], _tile(k, (1024, 512))
    nk = k // tk
    has_add = add is not None

    def body(*refs):
        a_ref, b_ref = refs[0], refs[1]
        add_ref = refs[2] if has_add else None
        o_ref = refs[2 + has_add]
        av = a_ref[...].astype(BF16)
        bv = b_ref[...].astype(BF16)
        if trans_b:
            part = lax.dot_general(av, bv, NT_DIMS, preferred_element_type=F32)
        else:
            part = jnp.dot(av, bv, preferred_element_type=F32)

        def finish(acc):
            if has_add:
                acc = acc + add_ref[...].astype(F32)
            o_ref[...] = acc.astype(o_ref.dtype)

        if nk == 1:
            finish(part)
        else:
            acc_ref = refs[3 + has_add]
            kk = pl.program_id(2)

            @pl.when(kk == 0)
            def _():
                acc_ref[...] = part

            @pl.when(kk > 0)
            def _():
                acc_ref[...] += part

            @pl.when(kk == nk - 1)
            def _():
                finish(acc_ref[...])

    in_specs = [pl.BlockSpec((tm, tk), lambda i, j, kk: (i, kk))]
    if b_shards is not None and trans_b:
        in_specs.append(pl.BlockSpec((None, tn, tk), lambda i, j, kk: (b_shards[0] + kk, j, 0)))
    elif b_shards is not None:
        in_specs.append(pl.BlockSpec((None, tk, tn), lambda i, j, kk: (b_shards[0] + j, kk, 0)))
    elif trans_b:
        in_specs.append(pl.BlockSpec((tn, tk), lambda i, j, kk: (j, kk)))
    else:
        in_specs.append(pl.BlockSpec((tk, tn), lambda i, j, kk: (kk, j)))
    args = [a, b]
    if has_add:
        in_specs.append(pl.BlockSpec((tm, tn), lambda i, j, kk: (i, j)))
        args.append(add)
    return _call(
        body, name=name, grid=(m // tm, n // tn, nk), in_specs=in_specs,
        out_specs=pl.BlockSpec((tm, tn), lambda i, j, kk: (i, j)),
        out_shape=jax.ShapeDtypeStruct((m, n), out_dtype),
        scratch_shapes=[pltpu.VMEM((tm, tn), F32)] if nk > 1 else [],
        compiler_params=_params("parallel", "parallel", "arbitrary"),
    )(*args)


def _mm_tn(a, b, *, name, ts=TM, shard_cols=None):
    s, m = a.shape
    n = b.shape[1]
    tm = _tile(m, (1408, 1024, 768, 512, 256, 128))
    tn = shard_cols or _tile(n, (1536, 1408, 1024, 768, 512, 256, 128))
    if shard_cols:
        out_spec = pl.BlockSpec((None, tm, tn), lambda i, j, kk: (j, i, 0))
        out_shape = jax.ShapeDtypeStruct((n // tn, m, tn), F32)
    else:
        out_spec = pl.BlockSpec((tm, tn), lambda i, j, kk: (i, j))
        out_shape = jax.ShapeDtypeStruct((m, n), F32)

    def body(a_ref, b_ref, o_ref):
        kk = pl.program_id(2)
        part = jnp.dot(a_ref[...].astype(BF16).T, b_ref[...].astype(BF16),
                       preferred_element_type=F32)

        @pl.when(kk == 0)
        def _():
            o_ref[...] = part

        @pl.when(kk > 0)
        def _():
            o_ref[...] += part

    return _call(
        body, name=name, grid=(m // tm, n // tn, s // ts),
        in_specs=[pl.BlockSpec((ts, tm), lambda i, j, kk: (kk, i)),
                  pl.BlockSpec((ts, tn), lambda i, j, kk: (kk, j))],
        out_specs=out_spec, out_shape=out_shape,
        compiler_params=_params("parallel", "parallel", "arbitrary"),
    )(a, b)


def _rowwise(fn, rows, consts, row_outs, acc_outs, *, name, tm=TM, n_rows=None):
    rows = [r if isinstance(r, tuple) else (r, r.shape[1], 0, 0) for r in rows]
    s = n_rows or rows[0][0].shape[0]
    nr, nc, no, na = len(rows), len(consts), len(row_outs), len(acc_outs)

    def body(*refs):
        r_in, c_in = refs[:nr], refs[nr:nr + nc]
        o_refs, a_refs = refs[nr + nc:nr + nc + no], refs[nr + nc + no:]
        outs = fn(*[r[...] for r in r_in], *[c[...] for c in c_in])
        for r, v in zip(o_refs, outs[:no]):
            r[...] = v.astype(r.dtype)
        if na:
            i = pl.program_id(0)

            @pl.when(i == 0)
            def _():
                for r, v in zip(a_refs, outs[no:]):
                    r[...] = v.astype(F32)

            @pl.when(i > 0)
            def _():
                for r, v in zip(a_refs, outs[no:]):
                    r[...] += v.astype(F32)

    in_specs = [pl.BlockSpec((tm, w), lambda i, cb=cb, rb=rb: (i + rb, cb)) for _, w, cb, rb in rows]
    in_specs += [pl.BlockSpec(c.shape, lambda i: (0, 0)) for c in consts]
    out_specs = [pl.BlockSpec((tm, w), lambda i: (i, 0)) for w, _ in row_outs]
    out_specs += [pl.BlockSpec(sh, lambda i: (0, 0)) for sh in acc_outs]
    out_shape = [jax.ShapeDtypeStruct((s, w), dt) for w, dt in row_outs]
    out_shape += [jax.ShapeDtypeStruct(sh, F32) for sh in acc_outs]
    return _call(
        body, name=name, grid=(s // tm,), in_specs=in_specs, out_specs=out_specs,
        out_shape=out_shape, compiler_params=_params("arbitrary"),
    )(*[r[0] for r in rows], *consts)


def _rms(x, g, n=None):
    ms = jnp.sum(x * x, axis=-1, keepdims=True) / float(n or x.shape[-1])
    return x * lax.rsqrt(ms + EPS) * g


def _layer_norm(x, g, b):
    mu = jnp.sum(x, axis=-1, keepdims=True) / float(x.shape[-1])
    xc = x - mu
    var = jnp.sum(xc * xc, axis=-1, keepdims=True) / float(x.shape[-1])
    return xc * lax.rsqrt(var + EPS) * g + b


def _silu(x):
    return x * jax.nn.sigmoid(x)


@jax.custom_vjp
def _rope(y, cos, sin_a, sin_b):
    return y * cos + pltpu.roll(y, 112, 1) * sin_a + pltpu.roll(y, 16, 1) * sin_b


def _rope_fwd(y, cos, sin_a, sin_b):
    return _rope(y, cos, sin_a, sin_b), (cos, sin_a, sin_b)


def _rope_bwd(res, ct):
    cos, sin_a, sin_b = res
    dy = ct * cos + pltpu.roll(ct * sin_a, 16, 1) + pltpu.roll(ct * sin_b, 112, 1)
    return dy, jnp.zeros_like(cos), jnp.zeros_like(sin_a), jnp.zeros_like(sin_b)


_rope.defvjp(_rope_fwd, _rope_bwd)


def _qk_head(xh, g, cos, sin_a, sin_b):
    return _rope(_rms(xh, g, MLA_QK), cos, sin_a, sin_b)


def _heads(x, width):
    return [x[:, h * width:(h + 1) * width] for h in range(x.shape[1] // width)]


def _f_rms(x, g):
    return (_rms(x, g),)


def _f_rope_tab(pos, inv_freq):
    ang = pos.astype(F32) * inv_freq
    lane = lax.broadcasted_iota(jnp.int32, ang.shape, 1)
    sn = jnp.sin(ang)
    first = (lane >= MLA_NOPE) & (lane < MLA_NOPE + MLA_ROPE // 2)
    second = (lane >= MLA_NOPE + MLA_ROPE // 2) & (lane < MLA_QK)
    return jnp.cos(ang), jnp.where(first, -sn, 0.0), jnp.where(second, sn, 0.0)


def _mix_pre(za, zg, zcq, zckv, ba, bg, gq, gkv):
    u0 = (za + ba) * jax.nn.sigmoid(zg + bg)
    return u0, _rms(zcq, gq), _rms(zckv, gkv)


def _ln_silu(c1, bdw, lg, lb):
    return _silu(_layer_norm(c1 + bdw, lg, lb))


def _f_qk_prep(q0, kn, kr, cos, sa, sb, gq, gk):
    qs = [_qk_head(xh, gq, cos, sa, sb) * ATT_SCALE for xh in _heads(q0, LANES)]
    ks = [_qk_head(xh + kr, gk, cos, sa, sb) for xh in _heads(kn, LANES)]
    return jnp.concatenate(qs, axis=1), jnp.concatenate(ks, axis=1)


def _act(cg, cv, bg, bv):
    return _silu(cg + bg) * (cv + bv)


def _f_mem_k(kk, g):
    return (jnp.concatenate([_rms(xh, g) for xh in _heads(kk, MEM_HEAD_DIM)], axis=1),)


def _mem_probs(qn, kmh):
    s = lax.dot_general(qn.astype(BF16), kmh, NT_DIMS, preferred_element_type=F32) * MEM_SCALE
    e = jnp.exp(s - jnp.max(s, axis=-1, keepdims=True))
    return e / jnp.sum(e, axis=-1, keepdims=True)


def _f_mem_attn(qm0, km, vm, g):
    outs = []
    for h, xh in enumerate(_heads(qm0, MEM_HEAD_DIM)):
        sl = slice(h * MEM_HEAD_DIM, (h + 1) * MEM_HEAD_DIM)
        p = _mem_probs(_rms(xh, g), km[:, sl])
        outs.append(jnp.dot(p.astype(BF16), vm[:, sl].astype(BF16), preferred_element_type=F32))
    return (jnp.concatenate(outs, axis=1),)


def _f_loss(y, t):
    e = y - t
    return e * (1.0 / D_MODEL), jnp.sum(e * e, axis=0, keepdims=True)


def _b_rms(x, dh, dres, g):
    _, vjp = jax.vjp(_rms, x, g)
    dx, dg = vjp(dh)
    return dx + dres, dg


def _b_rms_nores(x, dh, g):
    _, vjp = jax.vjp(_rms, x, g)
    dx, dg = vjp(dh)
    return dx, dg


def _b_mix_pre(za, zg, zcq, zckv, du0, dcqn, dckvn, dkr, ba, bg, gq, gkv):
    _, vjp = jax.vjp(_mix_pre, za, zg, zcq, zckv, ba, bg, gq, gkv)
    dza, dzg, dzcq, dzckv, dba, dbg, dgq, dgkv = vjp((du0, dcqn, dckvn))
    return jnp.concatenate([dza, dzg, dzcq, dzckv, dkr], axis=1), dba, dbg, dgq, dgkv


def _b_ln_silu(c1, du, bdw, lg, lb):
    _, vjp = jax.vjp(_ln_silu, c1, bdw, lg, lb)
    return vjp(du)


def _b_qk_prep(q0, kn, kr, cos, sa, sb, dq, dk, gq, gk):
    head = lambda xh, g: _qk_head(xh, g, cos, sa, sb)
    dq = dq * ATT_SCALE
    dq0, dkn = [], []
    dkr = jnp.zeros_like(kr)
    dgq = jnp.zeros_like(gq)
    dgk = jnp.zeros_like(gk)
    for xh, ct in zip(_heads(q0, LANES), _heads(dq, LANES)):
        _, vjp = jax.vjp(head, xh, gq)
        dx, dg = vjp(ct)
        dq0.append(dx)
        dgq = dgq + dg
    for xh, ct in zip(_heads(kn, LANES), _heads(dk, LANES)):
        _, vjp = jax.vjp(head, xh + kr, gk)
        dx, dg = vjp(ct)
        dkn.append(dx)
        dkr = dkr + dx
        dgk = dgk + dg
    return jnp.concatenate(dq0, axis=1), jnp.concatenate(dkn, axis=1), dkr, dgq, dgk


def _b_act(cg, cv, dact, bg, bv):
    _, vjp = jax.vjp(_act, cg, cv, bg, bv)
    return vjp(dact)


def _b_mem_k(kk, dkm, g):
    dkk = []
    dg = jnp.zeros_like(g)
    for xh, ct in zip(_heads(kk, MEM_HEAD_DIM), _heads(dkm, MEM_HEAD_DIM)):
        _, vjp = jax.vjp(_rms, xh, g)
        dx, dgh = vjp(ct)
        dkk.append(dx)
        dg = dg + dgh
    return jnp.concatenate(dkk, axis=1), dg


def _b_mem_attn(dom, qm0, km, vm, g):
    dq0, dkm, dvm = [], [], []
    dg = jnp.zeros_like(g)
    for h, (xh, doh) in enumerate(zip(_heads(qm0, MEM_HEAD_DIM), _heads(dom, MEM_HEAD_DIM))):
        sl = slice(h * MEM_HEAD_DIM, (h + 1) * MEM_HEAD_DIM)
        kmh, vmh = km[:, sl], vm[:, sl].astype(BF16)
        qn, vjp = jax.vjp(_rms, xh, g)
        p = _mem_probs(qn, kmh)
        dob = doh.astype(BF16)
        dp = lax.dot_general(dob, vmh, NT_DIMS, preferred_element_type=F32)
        ds = (p * (dp - jnp.sum(dp * p, axis=-1, keepdims=True)) * MEM_SCALE).astype(BF16)
        dqn = jnp.dot(ds, kmh, preferred_element_type=F32)
        dkm.append(jnp.dot(ds.T, qn.astype(BF16), preferred_element_type=F32))
        dvm.append(jnp.dot(p.astype(BF16).T, dob, preferred_element_type=F32))
        dx, dgh = vjp(dqn)
        dq0.append(dx)
        dg = dg + dgh
    return (jnp.concatenate(dq0, axis=1), jnp.concatenate(dkm, axis=1),
            jnp.concatenate(dvm, axis=1), dg)


def _adamw(w, g, m, v):
    m = ADAM_B1 * m + (1.0 - ADAM_B1) * g
    v = ADAM_B2 * v + (1.0 - ADAM_B2) * jnp.square(g)
    m_hat = m / (1.0 - ADAM_B1 ** ADAM_STEP)
    v_hat = v / (1.0 - ADAM_B2 ** ADAM_STEP)
    delta = -ADAM_LR * (m_hat / (jnp.sqrt(v_hat) + ADAM_EPS) + ADAM_WD * w)
    return delta, m, v


def _adamw_small(parts, small, tiny, name):
    def body(*refs):
        p_ref, ins, outs = refs[0], refs[1:4 + 3 * len(TINY)], refs[4 + 3 * len(TINY):]
        me = 4 * lax.axis_index("x") + 2 * lax.axis_index("y") + lax.axis_index("c")
        groups = [(0, SMALL_ROWS)]
        groups += [(pl.multiple_of(base + me * rows, 8), rows) for base, (_, _, rows) in zip(TINY_BASE, TINY)]
        for k, (start, rows) in enumerate(groups):
            g = p_ref[0, pl.ds(start, rows), :]
            for d in range(1, N_DEV):
                g = g + p_ref[d, pl.ds(start, rows), :]
            w, m, v = [r[...] for r in ins[3 * k:3 * k + 3]]
            for ref, val in zip(outs[4 * k:4 * k + 4], (g,) + _adamw(w, g, m, v)):
                ref[...] = val

    args = list(small) + [t for grp in tiny for t in grp]
    out_shape = []
    for grp in [small] + list(tiny):
        out_shape += [jax.ShapeDtypeStruct(grp[0].shape, F32)] * 4
    return _call(body, name=name, out_shape=out_shape)(parts, *args)


def _conv_fwd(x, w, name):
    s, ch = x.shape
    kw = w.shape[0]
    halo = -(-(kw - 1) // 8) * 8
    r = CONV_ROWS
    n = s // r

    def chunk(window, wv):
        acc = jnp.zeros((r, LANES), F32)
        for k in range(kw):
            shift = kw - 1 - k
            sh = window if shift == 0 else pltpu.roll(window, shift, 0)
            acc = acc + sh[halo:halo + r] * wv[k:k + 1]
        return acc

    def body(x_ref, w_ref, y_ref):
        wv = w_ref[...]
        first = jnp.concatenate([jnp.zeros((halo, LANES), F32), x_ref[0:r]], axis=0)
        y_ref[0:r] = chunk(first, wv)

        def step(i, carry):
            base = pl.multiple_of(i * r, 8)
            y_ref[pl.ds(base, r)] = chunk(x_ref[pl.ds(base - halo, r + halo)], wv)
            return carry

        lax.fori_loop(1, n, step, 0)

    return _call(
        body, name=name, grid=(ch // LANES,),
        in_specs=[pl.BlockSpec((s, LANES), lambda c: (0, c)), pl.BlockSpec((kw, LANES), lambda c: (0, c))],
        out_specs=pl.BlockSpec((s, LANES), lambda c: (0, c)),
        out_shape=jax.ShapeDtypeStruct((s, ch), F32), compiler_params=_params("parallel"),
    )(x, w)


def _conv_bwd(dy, x, w, name):
    s, ch = x.shape
    kw = w.shape[0]
    halo = -(-(kw - 1) // 8) * 8
    r = CONV_ROWS
    n = s // r

    def dx_chunk(window, wv):
        acc = jnp.zeros((r, LANES), F32)
        for k in range(kw):
            shift = kw - 1 - k
            sh = window if shift == 0 else pltpu.roll(window, r + halo - shift, 0)
            acc = acc + sh[0:r] * wv[k:k + 1]
        return acc

    def dw_chunk(xwin, dyc, acc_ref):
        for k in range(kw):
            shift = kw - 1 - k
            sh = xwin if shift == 0 else pltpu.roll(xwin, shift, 0)
            prod = sh[halo:halo + r] * dyc
            acc_ref[k] += jnp.sum(prod.reshape(r // 8, 8, LANES), axis=0)

    def body(dy_ref, x_ref, w_ref, dx_ref, dw_ref, acc_ref):
        wv = w_ref[...]
        acc_ref[...] = jnp.zeros_like(acc_ref)
        xfirst = jnp.concatenate([jnp.zeros((halo, LANES), F32), x_ref[0:r]], axis=0)
        dw_chunk(xfirst, dy_ref[0:r], acc_ref)
        last = jnp.concatenate([dy_ref[s - r:s], jnp.zeros((halo, LANES), F32)], axis=0)
        dx_ref[s - r:s] = dx_chunk(last, wv)

        def step(i, carry):
            base = pl.multiple_of(i * r, 8)
            dw_chunk(x_ref[pl.ds(base - halo, r + halo)], dy_ref[pl.ds(base, r)], acc_ref)
            prev = pl.multiple_of((i - 1) * r, 8)
            dx_ref[pl.ds(prev, r)] = dx_chunk(dy_ref[pl.ds(prev, r + halo)], wv)
            return carry

        lax.fori_loop(1, n, step, 0)
        dw_ref[...] = jnp.sum(acc_ref[...], axis=1)

    spec = pl.BlockSpec((s, LANES), lambda c: (0, c))
    wspec = pl.BlockSpec((kw, LANES), lambda c: (0, c))
    return _call(
        body, name=name, grid=(ch // LANES,), in_specs=[spec, spec, wspec], out_specs=[spec, wspec],
        out_shape=[jax.ShapeDtypeStruct((s, ch), F32), jax.ShapeDtypeStruct((kw, ch), F32)],
        scratch_shapes=[pltpu.VMEM((kw, 8, LANES), F32)], compiler_params=_params("parallel"),
    )(dy, x, w)


def _chunk_mask(rows_are_queries):
    a = lax.broadcasted_iota(jnp.int32, (TQ, TQ), 0) // CHUNK
    b = lax.broadcasted_iota(jnp.int32, (TQ, TQ), 1) // CHUNK
    return (b <= a) if rows_are_queries else (a <= b)


def _head_lanes(hh):
    return slice(hh * LANES, (hh + 1) * LANES)


def _to_row(col):
    return jnp.broadcast_to(col, (TQ, LANES)).T[0:1, :]


def _flash_specs(s):
    width = HEADS_PER_STEP * LANES
    tile = pl.BlockSpec((TQ, width), lambda h, i: (i, h))
    whole = pl.BlockSpec((s, width), lambda h, i: (0, h))
    row_tile = pl.BlockSpec((HEADS_PER_STEP, 1, 1, TQ), lambda h, i: (h, i, 0, 0))
    row_whole = pl.BlockSpec((HEADS_PER_STEP, s // TQ, 1, TQ), lambda h, i: (h, 0, 0, 0))
    return tile, whole, row_tile, row_whole


def _split_refs(refs, n_in, n_out, ex):
    e_in, e_out = (len(ex.inputs), len(ex.out_shape)) if ex else (0, 0)
    a, b, c = n_in + e_in, n_in + e_in + n_out, n_in + e_in + n_out + e_out
    return refs[:n_in], refs[a:b], (refs[n_in:a], refs[b:c], refs[c:])


def _hosted(ex, ex_refs, grid, when_first):
    if ex is None:
        return
    ids = [pl.program_id(d) for d in range(len(grid))]
    cond = functools.reduce(
        lambda p, q_: p & q_, [i == (0 if when_first else g - 1) for i, g in zip(ids, grid)])

    @pl.when(cond)
    def _():
        (ex.start if when_first else ex.finish)(*ex_refs)


def _host_call(body, ex, name, grid, in_specs, out_specs, out_shape, args):
    e_in, e_out = (len(ex.inputs), len(ex.out_shape)) if ex else (0, 0)
    res = _call(
        body, name=name, grid=grid, in_specs=list(in_specs) + [ANY] * e_in,
        out_specs=list(out_specs) + [ANY] * e_out,
        out_shape=list(out_shape) + (ex.out_shape if ex else []),
        scratch_shapes=ex.scratch if ex else [],
        compiler_params=_params(*["arbitrary"] * len(grid)),
    )(*args, *(ex.inputs if ex else []))
    return res[:len(out_shape)], res[len(out_shape):]


def _flash_fwd(q, k, v, name, ex=None):
    s = q.shape[0]
    nq = s // TQ
    grid = (MLA_HEADS // HEADS_PER_STEP, nq)

    def body(*refs):
        (q_ref, k_ref, v_ref), (o_ref, lse_ref, lse_row_ref), ex_refs = _split_refs(refs, 3, 3, ex)
        _hosted(ex, ex_refs, grid, True)
        i = pl.program_id(1)
        qs = [q_ref[:, _head_lanes(hh)] for hh in range(HEADS_PER_STEP)]

        def step(j, carry, masked):
            base = pl.multiple_of(j * TQ, TQ)
            out = []
            for hh in range(HEADS_PER_STEP):
                m_prev, l_prev, acc = carry[hh]
                kj = k_ref[pl.ds(base, TQ), _head_lanes(hh)]
                vj = v_ref[pl.ds(base, TQ), _head_lanes(hh)]
                sc = lax.dot_general(qs[hh], kj, NT_DIMS, preferred_element_type=F32)
                if masked:
                    sc = jnp.where(_chunk_mask(True), sc, NEG)
                m_new = jnp.maximum(m_prev, jnp.max(sc, axis=-1, keepdims=True))
                alpha = jnp.exp(m_prev - m_new)
                p = jnp.exp(sc - m_new)
                l_new = alpha * l_prev + jnp.sum(p, axis=-1, keepdims=True)
                acc = acc * alpha + jnp.dot(p.astype(BF16), vj, preferred_element_type=F32)
                out.append((m_new, l_new, acc))
            return tuple(out)

        init = tuple((jnp.full((TQ, 1), NEG, F32), jnp.zeros((TQ, 1), F32), jnp.zeros((TQ, LANES), F32))
                     for _ in range(HEADS_PER_STEP))
        carry = lax.fori_loop(0, i, functools.partial(step, masked=False), init)
        for hh, (m_fin, l_fin, acc) in enumerate(step(i, carry, True)):
            o_ref[:, _head_lanes(hh)] = acc / l_fin
            lse = m_fin + jnp.log(l_fin)
            lse_ref[:, _head_lanes(hh)] = jnp.broadcast_to(lse, (TQ, LANES))
            lse_row_ref[hh, 0] = _to_row(lse)
        _hosted(ex, ex_refs, grid, False)

    tile, whole, row_tile, _ = _flash_specs(s)
    wide = jax.ShapeDtypeStruct((s, MLA_HEADS * LANES), F32)
    return _host_call(
        body, ex, name, grid, [tile, whole, whole], [tile, tile, row_tile],
        [wide, wide, jax.ShapeDtypeStruct((MLA_HEADS, nq, 1, TQ), F32)], (q, k, v))


def _flash_bwd_dq(q, k, v, do, o, lse, name, ex=None):
    s = q.shape[0]
    nq = s // TQ
    grid = (MLA_HEADS // HEADS_PER_STEP, nq)

    def body(*refs):
        (q_ref, k_ref, v_ref, do_ref, o_ref, lse_ref), (dq_ref, delta_row_ref), ex_refs = _split_refs(refs, 6, 2, ex)
        _hosted(ex, ex_refs, grid, True)
        i = pl.program_id(1)
        qs, dobs, deltas, lses = [], [], [], []
        for hh in range(HEADS_PER_STEP):
            dov = do_ref[:, _head_lanes(hh)]
            qs.append(q_ref[:, _head_lanes(hh)])
            dobs.append(dov.astype(BF16))
            deltas.append(jnp.sum(dov * o_ref[:, _head_lanes(hh)], axis=-1, keepdims=True))
            lses.append(lse_ref[:, _head_lanes(hh)][:, 0:1])

        def step(j, carry, masked):
            base = pl.multiple_of(j * TQ, TQ)
            out = []
            for hh in range(HEADS_PER_STEP):
                kj = k_ref[pl.ds(base, TQ), _head_lanes(hh)]
                vj = v_ref[pl.ds(base, TQ), _head_lanes(hh)]
                sc = lax.dot_general(qs[hh], kj, NT_DIMS, preferred_element_type=F32)
                if masked:
                    sc = jnp.where(_chunk_mask(True), sc, NEG)
                p = jnp.exp(sc - lses[hh])
                dp = lax.dot_general(dobs[hh], vj, NT_DIMS, preferred_element_type=F32)
                ds = (p * (dp - deltas[hh])).astype(BF16)
                out.append(carry[hh] + jnp.dot(ds, kj, preferred_element_type=F32))
            return tuple(out)

        init = tuple(jnp.zeros((TQ, LANES), F32) for _ in range(HEADS_PER_STEP))
        carry = lax.fori_loop(0, i, functools.partial(step, masked=False), init)
        for hh, dq in enumerate(step(i, carry, True)):
            dq_ref[:, _head_lanes(hh)] = dq
            delta_row_ref[hh, 0] = _to_row(deltas[hh])
        _hosted(ex, ex_refs, grid, False)

    tile, whole, row_tile, _ = _flash_specs(s)
    return _host_call(
        body, ex, name, grid, [tile, whole, whole, tile, tile, tile], [tile, row_tile],
        [jax.ShapeDtypeStruct((s, MLA_HEADS * LANES), F32),
         jax.ShapeDtypeStruct((MLA_HEADS, nq, 1, TQ), F32)], (q, k, v, do, o, lse))


def _flash_bwd_dkv(q, k, v, do, lse_row, delta_row, name):
    s = q.shape[0]
    nq = s // TQ

    def body(q_ref, k_ref, v_ref, do_ref, lse_row_ref, delta_row_ref, dk_ref, dv_ref):
        j = pl.program_id(1)
        kjs = [k_ref[:, _head_lanes(hh)] for hh in range(HEADS_PER_STEP)]
        vjs = [v_ref[:, _head_lanes(hh)] for hh in range(HEADS_PER_STEP)]

        def step(i, carry, masked):
            base = pl.multiple_of(i * TQ, TQ)
            out = []
            for hh in range(HEADS_PER_STEP):
                dk, dv = carry[hh]
                qi = q_ref[pl.ds(base, TQ), _head_lanes(hh)]
                dob = do_ref[pl.ds(base, TQ), _head_lanes(hh)].astype(BF16)
                sc_t = lax.dot_general(kjs[hh], qi, NT_DIMS, preferred_element_type=F32)
                if masked:
                    sc_t = jnp.where(_chunk_mask(False), sc_t, NEG)
                p_t = jnp.exp(sc_t - lse_row_ref[hh, i])
                dv = dv + jnp.dot(p_t.astype(BF16), dob, preferred_element_type=F32)
                dp_t = lax.dot_general(vjs[hh], dob, NT_DIMS, preferred_element_type=F32)
                ds_t = (p_t * (dp_t - delta_row_ref[hh, i])).astype(BF16)
                dk = dk + jnp.dot(ds_t, qi, preferred_element_type=F32)
                out.append((dk, dv))
            return tuple(out)

        zero = jnp.zeros((TQ, LANES), F32)
        carry = step(j, tuple((zero, zero) for _ in range(HEADS_PER_STEP)), True)
        carry = lax.fori_loop(j + 1, nq, functools.partial(step, masked=False), carry)
        for hh, (dk, dv) in enumerate(carry):
            dk_ref[:, _head_lanes(hh)] = dk
            dv_ref[:, _head_lanes(hh)] = dv

    tile, whole, _, row_whole = _flash_specs(s)
    return _call(
        body, name=name, grid=(MLA_HEADS // HEADS_PER_STEP, nq),
        in_specs=[whole, tile, tile, whole, row_whole, row_whole], out_specs=[tile, tile],
        out_shape=[jax.ShapeDtypeStruct((s, MLA_HEADS * LANES), F32)] * 2,
        compiler_params=_params("parallel", "arbitrary"),
    )(q, k, v, do, lse_row, delta_row)


def _side_by_side(g):
    return g.transpose(1, 0, 2).reshape(g.shape[1], N_DEV * g.shape[2])


def _col_shards(g):
    return g.reshape(g.shape[0], N_DEV, g.shape[1] // N_DEV).transpose(1, 0, 2)


def _pad_last(v, to):
    return jnp.pad(v, [(0, 0)] * (v.ndim - 1) + [(0, to - v.shape[-1])])


def _tiny_rows(v, rows):
    flat = v.reshape(v.shape[:-2] + (-1,))
    return _pad_last(flat, rows * LANES).reshape(v.shape[:-2] + (rows, LANES))


def _pack_small(vals):
    parts = []
    for (n, size), pad in zip(SMALL, SMALL_PAD):
        parts.append(jnp.pad(vals[n].reshape(-1), (0, pad - size)))
    flat = jnp.concatenate(parts)
    return jnp.pad(flat, (0, SMALL_ROWS * LANES - flat.shape[0])).reshape(SMALL_ROWS, LANES)


def _unpack_small(packed):
    flat = packed.reshape(-1)
    out, off = {}, 0
    for (n, size), pad in zip(SMALL, SMALL_PAD):
        out[n] = flat[off:off + size].reshape(1, size)
        off += pad
    return out


def _pad_heads(w, per_head, axis):
    shape = list(w.shape)
    shape[axis:axis + 1] = [MLA_HEADS, per_head]
    w = w.reshape(shape)
    pad = [(0, 0)] * len(shape)
    pad[axis + 1] = (0, LANES - per_head)
    w = jnp.pad(w, pad)
    shape[axis:axis + 2] = [MLA_HEADS * LANES]
    return w.reshape(shape)


def _unpad_heads(w, per_head, axis):
    shape = list(w.shape)
    shape[axis:axis + 1] = [MLA_HEADS, LANES]
    w = w.reshape(shape)
    w = lax.slice_in_dim(w, 0, per_head, axis=axis + 1)
    shape[axis:axis + 2] = [MLA_HEADS * per_head]
    return w.reshape(shape)


def _row(v, pad_to=None):
    v = v.reshape(1, -1)
    if pad_to is not None:
        v = jnp.pad(v, ((0, 0), (0, pad_to - v.shape[1])))
    return v


def kernel(x, mem, positions, mix_norm_g, w_in, b_conv_in, w_conv_dw, b_conv_dw, conv_ln_g, conv_ln_b, q_lat_norm_g, w_uq, kv_lat_norm_g, w_ukv, q_norm_g, k_norm_g, w_out, mem_norm_x_g, mem_norm_m_g, w_mem_q, w_mem_kv, mem_q_norm_g, mem_k_norm_g, w_mem_o, ffn_norm_g, w_up, w_ffn_dw, b_ffn_dw, w_down, loss_target, m_mix_norm_g, m_w_in, m_b_conv_in, m_w_conv_dw, m_b_conv_dw, m_conv_ln_g, m_conv_ln_b, m_q_lat_norm_g, m_w_uq, m_kv_lat_norm_g, m_w_ukv, m_q_norm_g, m_k_norm_g, m_w_out, m_mem_norm_x_g, m_mem_norm_m_g, m_w_mem_q, m_w_mem_kv, m_mem_q_norm_g, m_mem_k_norm_g, m_w_mem_o, m_ffn_norm_g, m_w_up, m_w_ffn_dw, m_b_ffn_dw, m_w_down, v_mix_norm_g, v_w_in, v_b_conv_in, v_w_conv_dw, v_b_conv_dw, v_conv_ln_g, v_conv_ln_b, v_q_lat_norm_g, v_w_uq, v_kv_lat_norm_g, v_w_ukv, v_q_norm_g, v_k_norm_g, v_w_out, v_mem_norm_x_g, v_mem_norm_m_g, v_w_mem_q, v_w_mem_kv, v_mem_q_norm_g, v_mem_k_norm_g, v_w_mem_o, v_ffn_norm_g, v_w_up, v_w_ffn_dw, v_b_ffn_dw, v_w_down):
    a = dict(locals())
    seq = x.shape[1]
    xs = x.reshape(seq, D_MODEL)
    mems = mem.reshape(-1, D_MODEL)
    target = loss_target.reshape(seq, D_MODEL)

    tiny = [n for n, _, _ in TINY]
    shard = lambda n: a[n][0] if n in tiny else a[n][0].astype(BF16)
    pos = jnp.stack([2 * lax.axis_index("x") + lax.axis_index("y"), lax.axis_index("c")]).astype(jnp.int32)
    ag_first = ["w_in", "w_uq", "w_ukv", "w_conv_dw"]
    ag_later = [n for n in [b for b, _ in BIG] + tiny if n not in ag_first]
    wg = dict(zip(ag_first, _all_gather([shard(n) for n in ag_first], "ag_weights_first")))
    wi = _side_by_side(wg["w_in"])
    s3 = 2 * CONV_CH + MLA_Q_RANK + MLA_KV_RANK
    w_in_p = jnp.concatenate([
        wi[:, :s3], jnp.zeros((D_MODEL, MLA_NOPE), BF16), wi[:, s3:],
        jnp.zeros((D_MODEL, LANES - MLA_QK), BF16)], axis=1)
    w_uq_p = _side_by_side(_pad_last(wg["w_uq"], LANES))
    w_uk_p = _side_by_side(_pad_last(wg["w_ukv"][:, :, :MLA_NOPE], LANES))
    w_uv_p = _side_by_side(_pad_last(wg["w_ukv"][:, :, MLA_NOPE:], LANES))
    w_cdw = _side_by_side(wg["w_conv_dw"])

    g_mix, g_qlat, g_kvlat = _row(mix_norm_g), _row(q_lat_norm_g), _row(kv_lat_norm_g)
    b_in = _row(b_conv_in)
    b_in_a, b_in_g = b_in[:, :CONV_CH], b_in[:, CONV_CH:]
    b_cdw, ln_g, ln_b = _row(b_conv_dw), _row(conv_ln_g), _row(conv_ln_b)
    g_q, g_k = _row(q_norm_g, LANES), _row(k_norm_g, LANES)
    g_memx, g_memm = _row(mem_norm_x_g), _row(mem_norm_m_g)
    g_mq, g_mk, g_ffn = _row(mem_q_norm_g), _row(mem_k_norm_g), _row(ffn_norm_g)
    b_f = _pad_last(b_ffn_dw.reshape(N_DEV, FF_SHARD), FF_PAD)
    b_f_g, b_f_v = b_f[:4].reshape(1, D_FF_PAD), b_f[4:].reshape(1, D_FF_PAD)

    freq = ROPE_THETA ** (-jnp.arange(0, MLA_ROPE, 2, dtype=F32) / MLA_ROPE)
    inv_freq = jnp.concatenate([jnp.zeros((MLA_NOPE,), F32), freq, freq,
                                jnp.zeros((LANES - MLA_QK,), F32)]).reshape(1, LANES)
    cos, sin_a, sin_b = _rowwise(_f_rope_tab, [positions.reshape(seq, 1)], [inv_freq],
                                 [(LANES, F32)] * 3, [], name="rope_tables")

    (h1,) = _rowwise(_f_rms, [xs], [g_mix], [(D_MODEL, BF16)], [], name="rms_mix")
    z = _mm(h1, w_in_p, name="mm_in")
    z_rows = [(z, CONV_CH, 0, 0), (z, CONV_CH, 1, 0), (z, MLA_Q_RANK, 4, 0), (z, MLA_KV_RANK, 10, 0)]
    z_kr = (z, LANES, 11, 0)
    u0, cqn, ckvn = _rowwise(
        _mix_pre, z_rows, [b_in_a, b_in_g, g_qlat, g_kvlat],
        [(CONV_CH, F32), (MLA_Q_RANK, BF16), (MLA_KV_RANK, BF16)], [], name="mix_pre")
    c1 = _conv_fwd(u0, w_cdw, "conv31_fwd")
    (u,) = _rowwise(lambda c, b, g, bb: (_ln_silu(c, b, g, bb),), [c1], [b_cdw, ln_g, ln_b],
                    [(CONV_CH, BF16)], [], name="ln_silu")
    q0 = _mm(cqn, w_uq_p, name="mm_uq")
    kn0 = _mm(ckvn, w_uk_p, name="mm_uk")
    v0 = _mm(ckvn, w_uv_p, out_dtype=BF16, name="mm_uv")
    qk_rows = [q0, kn0, z_kr, cos, sin_a, sin_b]
    qh, kh = _rowwise(_f_qk_prep, qk_rows, [g_q, g_k],
                      [(MLA_HEADS * LANES, BF16)] * 2, [], name="qk_prep")
    (attn, lse, lse_row), later = _flash_fwd(
        qh, kh, v0, "flash_fwd", _plan_all_gather([shard(n) for n in ag_later]))
    wg.update(zip(ag_later, later))
    w_out = wg["w_out"].reshape(D_MODEL, D_MODEL)
    w_out_u = w_out[:CONV_CH]
    w_out_a = _pad_heads(w_out[CONV_CH:], MLA_V, 0)
    w_mq, w_mo = wg["w_mem_q"].reshape(D_MODEL, D_MODEL), wg["w_mem_o"].reshape(D_MODEL, D_MODEL)
    w_mkv = _side_by_side(wg["w_mem_kv"])
    w_up_p = _pad_last(wg["w_up"], FF_PAD)
    w_dn = jnp.pad(wg["w_down"].reshape(4, FF_SHARD, D_MODEL),
                   ((0, 0), (0, FF_PAD - FF_SHARD), (0, 0))).reshape(D_FF_PAD, D_MODEL)
    w_fdw = _pad_last(wg["w_ffn_dw"], FF_PAD)
    w_fdw_g = w_fdw[:4].transpose(1, 0, 2).reshape(FFN_CONV_WIDTH, D_FF_PAD)
    w_fdw_v = w_fdw[4:].transpose(1, 0, 2).reshape(FFN_CONV_WIDTH, D_FF_PAD)
    x1 = _mm(u, w_out_u, add=xs, name="mm_out_u")
    x1 = _mm(attn, w_out_a, add=x1, name="mm_out_a")

    (hq,) = _rowwise(_f_rms, [x1], [g_memx], [(D_MODEL, BF16)], [], name="rms_memx")
    (hm,) = _rowwise(_f_rms, [mems], [g_memm], [(D_MODEL, BF16)], [], name="rms_memm", tm=mems.shape[0])
    qm0 = _mm(hq, w_mq, name="mm_memq")
    kvm0 = _mm(hm, w_mkv, name="mm_memkv", tm=mems.shape[0])
    (km,) = _rowwise(_f_mem_k, [(kvm0, D_MODEL, 0, 0)], [g_mk], [(D_MODEL, BF16)], [],
                     name="mem_k", tm=mems.shape[0])
    vm = kvm0[:, D_MODEL:]
    (om,) = _rowwise(_f_mem_attn, [qm0], [km, vm, g_mq], [(D_MODEL, BF16)], [], name="mem_attn")
    x2 = _mm(om, w_mo, add=x1, name="mm_memo")

    (h3,) = _rowwise(_f_rms, [x2], [g_ffn], [(D_MODEL, BF16)], [], name="rms_ffn")
    up_g = _mm(h3, w_up_p, b_shards=(0, 4), name="mm_up_g")
    up_v = _mm(h3, w_up_p, b_shards=(4, 4), name="mm_up_v")
    cg = _conv_fwd(up_g, w_fdw_g, "conv3_g_fwd")
    cv = _conv_fwd(up_v, w_fdw_v, "conv3_v_fwd")
    (act,) = _rowwise(lambda g_, v_, bg, bv: (_act(g_, v_, bg, bv),), [cg, cv], [b_f_g, b_f_v],
                      [(D_FF_PAD, BF16)], [], name="ffn_act", tm=256)
    y = _mm(act, w_dn, add=x2, name="mm_down")
    dy, sq = _rowwise(_f_loss, [y, target], [], [(D_MODEL, F32)], [(1, D_MODEL)], name="loss")
    loss = lax.psum(0.5 * jnp.sum(sq) / D_MODEL, ("x", "y", "c"))

    gw, gs, gt = {}, {}, {}
    gw_dn = _mm_tn(act, dy, name="tn_down").reshape(4, FF_PAD, D_MODEL)
    gw["w_down"] = gw_dn[:, :FF_SHARD].reshape(N_DEV, FF_SHARD // 2, D_MODEL)
    dact = _mm(dy, w_dn, trans_b=True, name="mm_down_t")
    dcg, dcv, db_g, db_v = _rowwise(_b_act, [cg, cv, dact], [b_f_g, b_f_v],
                                    [(D_FF_PAD, F32)] * 2, [(1, D_FF_PAD)] * 2, name="ffn_act_bwd", tm=256)
    db_f = jnp.concatenate([db_g.reshape(4, FF_PAD), db_v.reshape(4, FF_PAD)], axis=0)
    gs["b_ffn_dw"] = db_f[:, :FF_SHARD].reshape(1, 2 * D_FF)
    dup_g, dwf_g = _conv_bwd(dcg, up_g, w_fdw_g, "conv3_g_bwd")
    dup_v, dwf_v = _conv_bwd(dcv, up_v, w_fdw_v, "conv3_v_bwd")
    dwf = jnp.concatenate([dwf_g.reshape(FFN_CONV_WIDTH, 4, FF_PAD), dwf_v.reshape(FFN_CONV_WIDTH, 4, FF_PAD)], axis=1)
    gt["w_ffn_dw"] = dwf[:, :, :FF_SHARD].transpose(1, 0, 2)
    gw_up = jnp.concatenate([_mm_tn(h3, dup_g, shard_cols=FF_PAD, name="tn_up_g"),
                             _mm_tn(h3, dup_v, shard_cols=FF_PAD, name="tn_up_v")], axis=0)
    gw["w_up"] = gw_up[:, :, :FF_SHARD]
    dh3 = _mm(dup_g, w_up_p, trans_b=True, b_shards=(0, 4), name="mm_up_g_t")
    dh3 = _mm(dup_v, w_up_p, trans_b=True, b_shards=(4, 4), add=dh3, name="mm_up_v_t")
    dx2, gs["ffn_norm_g"] = _rowwise(_b_rms, [x2, dh3, dy], [g_ffn], [(D_MODEL, F32)], [(1, D_MODEL)],
                                     name="rms_ffn_bwd")

    gw["w_mem_o"] = _mm_tn(om, dx2, name="tn_memo").reshape(N_DEV, -1, D_MODEL)
    dom = _mm(dx2, w_mo, trans_b=True, name="mm_memo_t")
    n_mem = mems.shape[0]
    dqm0, dkm, dvm, gs["mem_q_norm_g"] = _rowwise(
        _b_mem_attn, [dom, qm0], [km, vm, g_mq], [(D_MODEL, F32)],
        [(n_mem, D_MODEL), (n_mem, D_MODEL), (1, MEM_HEAD_DIM)], name="mem_attn_bwd")
    gw["w_mem_q"] = _mm_tn(hq, dqm0, name="tn_memq").reshape(N_DEV, -1, D_MODEL)
    dhq = _mm(dqm0, w_mq, trans_b=True, name="mm_memq_t")
    dx1, gs["mem_norm_x_g"] = _rowwise(_b_rms, [x1, dhq, dx2], [g_memx], [(D_MODEL, F32)],
                                       [(1, D_MODEL)], name="rms_memx_bwd")
    dkk, gs["mem_k_norm_g"] = _rowwise(_b_mem_k, [(kvm0, D_MODEL, 0, 0), dkm], [g_mk],
                                       [(D_MODEL, F32)], [(1, MEM_HEAD_DIM)], name="mem_k_bwd", tm=n_mem)
    dkvm0 = jnp.concatenate([dkk, dvm], axis=1)
    gw["w_mem_kv"] = _col_shards(_mm_tn(hm, dkvm0, name="tn_memkv", ts=n_mem))
    dhm = _mm(dkvm0, w_mkv, trans_b=True, name="mm_memkv_t", tm=n_mem)
    _, gs["mem_norm_m_g"] = _rowwise(_b_rms_nores, [mems, dhm], [g_memm], [(D_MODEL, F32)],
                                     [(1, D_MODEL)], name="rms_memm_bwd", tm=n_mem)

    gw_out_u = _mm_tn(u, dx1, name="tn_out_u")
    gw_out_a = _mm_tn(attn, dx1, name="tn_out_a")
    gw["w_out"] = jnp.concatenate([gw_out_u, _unpad_heads(gw_out_a, MLA_V, 0)], axis=0).reshape(N_DEV, -1, D_MODEL)
    du = _mm(dx1, w_out_u, trans_b=True, name="mm_out_u_t")
    dattn = _mm(dx1, w_out_a, trans_b=True, name="mm_out_a_t")
    dc1, gs["b_conv_dw"], gs["conv_ln_g"], gs["conv_ln_b"] = _rowwise(
        _b_ln_silu, [c1, du], [b_cdw, ln_g, ln_b], [(CONV_CH, F32)], [(1, CONV_CH)] * 3, name="ln_silu_bwd")
    du0, g_cdw = _conv_bwd(dc1, u0, w_cdw, "conv31_bwd")
    gt["w_conv_dw"] = _col_shards(g_cdw)
    rs_first = ["w_up", "w_down", "w_mem_o", "w_mem_q", "w_mem_kv", "w_out"]
    grads = [gw[n] for n in rs_first]
    sums_first = _rs_add(grads, _swap_sibling(grads, "rs_sibling_first"), pos, "rs_add_first")
    (dqh, delta_row), recvs_first = _flash_bwd_dq(
        qh, kh, v0, dattn, attn, lse, "flash_bwd_dq", _plan_swap_chips(sums_first))
    dkh, dv0 = _flash_bwd_dkv(qh, kh, v0, dattn, lse_row, delta_row, "flash_bwd_dkv")
    dq0, dkn0, dkr, dgq, dgk = _rowwise(
        _b_qk_prep, qk_rows + [dqh, dkh], [g_q, g_k],
        [(MLA_HEADS * LANES, F32)] * 2 + [(LANES, F32)], [(1, LANES)] * 2, name="qk_prep_bwd")
    gs["q_norm_g"], gs["k_norm_g"] = dgq[:, :MLA_QK], dgk[:, :MLA_QK]
    gw["w_uq"] = _col_shards(_mm_tn(cqn, dq0, name="tn_uq"))[:, :, :MLA_QK]
    g_uk = _col_shards(_mm_tn(ckvn, dkn0, name="tn_uk"))[:, :, :MLA_NOPE]
    g_uv = _col_shards(_mm_tn(ckvn, dv0, name="tn_uv"))[:, :, :MLA_V]
    gw["w_ukv"] = jnp.concatenate([g_uk, g_uv], axis=2)
    dcqn = _mm(dq0, w_uq_p, trans_b=True, name="mm_uq_t")
    dckvn = _mm(dkn0, w_uk_p, trans_b=True, name="mm_uk_t")
    dckvn = _mm(dv0, w_uv_p, trans_b=True, add=dckvn, name="mm_uv_t")
    dz, dba, dbg, gs["q_lat_norm_g"], gs["kv_lat_norm_g"] = _rowwise(
        _b_mix_pre, z_rows + [du0, dcqn, dckvn, dkr], [b_in_a, b_in_g, g_qlat, g_kvlat],
        [(IN_PAD, F32)], [(1, CONV_CH)] * 2 + [(1, MLA_Q_RANK), (1, MLA_KV_RANK)], name="mix_pre_bwd")
    gs["b_conv_in"] = jnp.concatenate([dba, dbg], axis=1)
    gw_in = _mm_tn(h1, dz, name="tn_in")
    gw["w_in"] = _col_shards(jnp.concatenate([gw_in[:, :s3], gw_in[:, s3 + MLA_NOPE:s3 + MLA_QK]], axis=1))
    dh1 = _mm(dz, w_in_p, trans_b=True, name="mm_in_t")
    dx, gs["mix_norm_g"] = _rowwise(_b_rms, [xs, dh1, dx1], [g_mix], [(D_MODEL, F32)], [(1, D_MODEL)],
                                    name="rms_mix_bwd")

    rs_last = [n for n, _ in BIG if n not in rs_first]
    grads = [gw[n] for n in rs_last]
    sums_last = _rs_add(grads, _swap_sibling(grads, "rs_sibling_last"), pos, "rs_add_last")
    recvs_last = _run_exchange(_plan_swap_chips(sums_last), "rs_chips_last")
    big = rs_first + rs_last
    flat = _adamw_big(list(sums_first) + list(sums_last), list(recvs_first) + list(recvs_last),
                      [a[n] for n in big], [a["m_" + n] for n in big], [a["v_" + n] for n in big],
                      pos, "adamw_big")
    res = [{n: flat[4 * i + k] for i, n in enumerate(big)} for k in range(4)]

    part = jnp.concatenate(
        [_pack_small(gs)] + [_tiny_rows(gt[n], rows).reshape(N_DEV * rows, LANES) for n, _, rows in TINY], axis=0)
    (parts,) = _all_gather([part], "ag_small_grads")
    small_in = [_pack_small({n: a[p + n] for n, _ in SMALL}) for p in ("", "m_", "v_")]
    tiny_in = [[_tiny_rows(a[p + n][0], rows) for p in ("", "m_", "v_")] for n, _, rows in TINY]
    flat = _adamw_small(parts, small_in, tiny_in, "adamw_small")
    for k in range(4):
        res[k].update(_unpack_small(flat[k]))
        for i, (n, shape, _) in enumerate(TINY):
            res[k][n] = flat[4 * (i + 1) + k].reshape(-1)[:math.prod(shape)].reshape((1,) + shape)

    return (loss, dx.reshape(1, seq, D_MODEL), *[res[k][n] for k in range(4) for n in WEIGHTS])
```

```python
import functools
import math

import jax
import jax.numpy as jnp
from jax import lax
from jax.experimental import pallas as pl
from jax.experimental.pallas import tpu as pltpu

F32 = jnp.float32
BF16 = jnp.bfloat16
EPS = 1e-6
LANES = 128
N_DEV = 8
D_MODEL = 1024
CONV_CH = 512
CONV_WIDTH = 31
MLA_HEADS = 8
MLA_NOPE = 64
MLA_ROPE = 32
MLA_V = 64
MLA_QK = MLA_NOPE + MLA_ROPE
MLA_Q_RANK = 256
MLA_KV_RANK = 128
ROPE_THETA = 10000.0
IN_COLS = 2 * CONV_CH + MLA_Q_RANK + MLA_KV_RANK + MLA_ROPE
IN_PAD = 2 * CONV_CH + MLA_Q_RANK + MLA_KV_RANK + LANES
MEM_HEADS = 4
MEM_HEAD_DIM = 256
D_FF = 2816
FFN_CONV_WIDTH = 3
CHUNK = 64
ATT_SCALE = 1.0 / math.sqrt(MLA_QK)
LN2 = math.log(2.0)
Q_SCALE = ATT_SCALE / LN2
MEM_SCALE = 1.0 / math.sqrt(MEM_HEAD_DIM)
ADAM_LR, ADAM_B1, ADAM_B2, ADAM_EPS, ADAM_WD, ADAM_STEP = 0.001, 0.9, 0.999, 1e-08, 0.01, 10

TM = 512
MM_ROWS = 1024
TQ = 512
HEADS_PER_STEP = 2
CONV_ROWS = 256
NEG = -1e30
VMEM_LIMIT = 56 * 1024 * 1024

MESH = pl.DeviceIdType.MESH
ANY = pl.BlockSpec(memory_space=pl.ANY)
NT_DIMS = (((1,), (1,)), ((), ()))

BIG = [
    ("w_in", (1024, 180)), ("w_uq", (256, 96)), ("w_ukv", (128, 128)), ("w_out", (128, 1024)),
    ("w_mem_q", (128, 1024)), ("w_mem_kv", (1024, 256)), ("w_mem_o", (128, 1024)),
    ("w_up", (1024, 704)), ("w_down", (352, 1024)),
]
TINY = [("w_conv_dw", (31, 64), 16), ("w_ffn_dw", (3, 704), 24)]
ROW_STEPS = 4
FF_SHARD = D_FF // 4
FF_PAD = 768
D_FF_PAD = 4 * FF_PAD
SMALL = [
    ("mix_norm_g", 1024), ("b_conv_in", 1024), ("b_conv_dw", 512), ("conv_ln_g", 512),
    ("conv_ln_b", 512), ("q_lat_norm_g", 256), ("kv_lat_norm_g", 128), ("q_norm_g", 96),
    ("k_norm_g", 96), ("mem_norm_x_g", 1024), ("mem_norm_m_g", 1024), ("mem_q_norm_g", 256),
    ("mem_k_norm_g", 256), ("ffn_norm_g", 1024), ("b_ffn_dw", 5632),
]
WEIGHTS = [
    "mix_norm_g", "w_in", "b_conv_in", "w_conv_dw", "b_conv_dw", "conv_ln_g", "conv_ln_b",
    "q_lat_norm_g", "w_uq", "kv_lat_norm_g", "w_ukv", "q_norm_g", "k_norm_g", "w_out",
    "mem_norm_x_g", "mem_norm_m_g", "w_mem_q", "w_mem_kv", "mem_q_norm_g", "mem_k_norm_g",
    "w_mem_o", "ffn_norm_g", "w_up", "w_ffn_dw", "b_ffn_dw", "w_down",
]


SMALL_PAD = [(-(-n // LANES)) * LANES for _, n in SMALL]
SMALL_ROWS = -(-sum(SMALL_PAD) // (8 * LANES)) * 8
TINY_BASE = [SMALL_ROWS + N_DEV * sum(r for _, _, r in TINY[:i]) for i in range(len(TINY))]
PART_ROWS = SMALL_ROWS + N_DEV * sum(r for _, _, r in TINY)


def _call(body, **kw):
    return pl.pallas_call(body, **kw)


def _params(*sem):
    return pltpu.CompilerParams(dimension_semantics=sem, vmem_limit_bytes=VMEM_LIMIT)


class _Exchange:
    def __init__(self, inputs, out_shape, scratch, start, finish):
        self.inputs, self.out_shape, self.scratch = list(inputs), list(out_shape), list(scratch)
        self.start, self.finish = start, finish


def _run_exchange(ex, name):
    n_in, n_out = len(ex.inputs), len(ex.out_shape)

    def body(*refs):
        parts = refs[:n_in], refs[n_in:n_in + n_out], refs[n_in + n_out:]
        ex.start(*parts)
        ex.finish(*parts)

    return _call(body, name=name, out_shape=ex.out_shape, in_specs=[ANY] * n_in,
                 out_specs=[ANY] * n_out, scratch_shapes=ex.scratch)(*ex.inputs)


def _plan_all_gather(xs):
    n = len(xs)

    def copies(x_refs, out_refs, sems):
        send_sems, recv_sems, local_sems = sems
        x, y, c = lax.axis_index("x"), lax.axis_index("y"), lax.axis_index("c")
        me, sibling = (x, y, c), (x, y, 1 - c)
        chips = [(1 - x, y), (x, 1 - y), (1 - x, 1 - y)]

        def slot(o, px, py, pc):
            return out_refs[o].at[4 * px + 2 * py + pc]

        def copy(o, k, block, to, src=None):
            return pltpu.make_async_remote_copy(
                src_ref=slot(o, *block) if src is None else src, dst_ref=slot(o, *block),
                send_sem=send_sems.at[o, k], recv_sem=recv_sems.at[o, k],
                device_id=to, device_id_type=MESH)

        mine = [pltpu.make_async_copy(x_refs[o], slot(o, *me), local_sems.at[o]) for o in range(n)]
        first = [copy(o, 0, me, sibling, src=x_refs[o]) for o in range(n)]
        first += [copy(o, 1 + j, me, (*chip, c), src=x_refs[o])
                  for j, chip in enumerate(chips) for o in range(n)]
        return me, sibling, chips, copy, mine, first

    def start(x_refs, out_refs, sems):
        _, _, _, _, mine, first = copies(x_refs, out_refs, sems)
        for cp in mine + first:
            cp.start()

    def finish(x_refs, out_refs, sems):
        me, sibling, chips, copy, mine, first = copies(x_refs, out_refs, sems)
        c = me[2]
        passed = []
        for j, chip in enumerate(chips):
            for o in range(n):
                copy(o, 1 + j, (*chip, c), me).wait_recv()
                passed.append(copy(o, 4 + j, (*chip, c), sibling))
                passed[-1].start()
        for o in range(n):
            copy(o, 0, sibling, me).wait_recv()
        for j, chip in enumerate(chips):
            for o in range(n):
                copy(o, 4 + j, (*chip, 1 - c), me).wait_recv()
        for cp in first + passed:
            cp.wait_send()
        for cp in mine:
            cp.wait()

    return _Exchange(
        xs, [jax.ShapeDtypeStruct((N_DEV,) + v.shape, v.dtype) for v in xs],
        [pltpu.SemaphoreType.DMA((n, 7)), pltpu.SemaphoreType.DMA((n, 7)), pltpu.SemaphoreType.DMA((n,))],
        start, finish)


def _all_gather(xs, name):
    return _run_exchange(_plan_all_gather(xs), name)


def _swap_sibling(grads, name):
    n = len(grads)

    def body(*refs):
        g_refs, got_refs = refs[:n], refs[n:2 * n]
        send_sems, recv_sems = refs[2 * n:]
        x, y, c = lax.axis_index("x"), lax.axis_index("y"), lax.axis_index("c")
        copies = [
            pltpu.make_async_remote_copy(
                src_ref=g_refs[o].at[2 * chip + 1 - c], dst_ref=got_refs[o].at[chip],
                send_sem=send_sems.at[o, chip], recv_sem=recv_sems.at[o, chip],
                device_id=(x, y, 1 - c), device_id_type=MESH)
            for o in range(n) for chip in range(4)]
        for cp in copies:
            cp.start()
        for cp in copies:
            cp.wait()

    return _call(
        body, name=name,
        out_shape=[jax.ShapeDtypeStruct((4,) + g.shape[1:], g.dtype) for g in grads],
        in_specs=[ANY] * n, out_specs=[ANY] * n,
        scratch_shapes=[pltpu.SemaphoreType.DMA((n, 4)), pltpu.SemaphoreType.DMA((n, 4))],
    )(*grads)


def _plan_swap_chips(sums):
    n = len(sums)

    def copies(a_refs, r_refs, sems):
        send_sems, recv_sems = sems
        x, y, c = lax.axis_index("x"), lax.axis_index("y"), lax.axis_index("c")
        peers = [(x, 1 - y), (1 - x, y), (1 - x, 1 - y)]
        return [
            pltpu.make_async_remote_copy(
                src_ref=a_refs[o].at[2 * px + py], dst_ref=r_refs[o].at[k],
                send_sem=send_sems.at[o, k], recv_sem=recv_sems.at[o, k],
                device_id=(px, py, c), device_id_type=MESH)
            for k, (px, py) in enumerate(peers) for o in range(n)]

    def start(a_refs, r_refs, sems):
        for cp in copies(a_refs, r_refs, sems):
            cp.start()

    def finish(a_refs, r_refs, sems):
        for cp in copies(a_refs, r_refs, sems):
            cp.wait()

    return _Exchange(
        sums, [jax.ShapeDtypeStruct((3,) + a.shape[1:], a.dtype) for a in sums],
        [pltpu.SemaphoreType.DMA((n, 3)), pltpu.SemaphoreType.DMA((n, 3))], start, finish)


def _shard_block(shape):
    return (None, shape[-2] // ROW_STEPS, shape[-1])


def _rs_add(grads, gots, pos, name):
    n = len(grads)

    def body(pos_ref, *refs):
        for g_ref, t_ref, o_ref in zip(refs[:n], refs[n:2 * n], refs[2 * n:]):
            o_ref[...] = g_ref[...] + t_ref[...]

    in_specs = [pl.BlockSpec(_shard_block(g.shape), lambda a, t, pos: (2 * a + pos[1], t, 0)) for g in grads]
    in_specs += [pl.BlockSpec(_shard_block(g.shape), lambda a, t, pos: (a, t, 0)) for g in gots]
    return _call(
        body, name=name,
        grid_spec=pltpu.PrefetchScalarGridSpec(
            num_scalar_prefetch=1, grid=(4, ROW_STEPS), in_specs=in_specs,
            out_specs=[pl.BlockSpec(_shard_block(g.shape), lambda a, t, pos: (a, t, 0)) for g in gots]),
        out_shape=[jax.ShapeDtypeStruct(g.shape, g.dtype) for g in gots],
        compiler_params=_params("arbitrary", "arbitrary"),
    )(pos, *grads, *gots)


def _adamw_big(sums, recvs, ws, ms, vs, pos, name):
    n = len(sums)

    def body(pos_ref, *refs):
        ins, outs = refs[:7 * n], refs[7 * n:]
        for o in range(n):
            own, r1, r2, r3, w, m, v = [r[...] for r in ins[7 * o:7 * o + 7]]
            g = ((own + r1) + r2) + r3
            for ref, val in zip(outs[4 * o:4 * o + 4], (g,) + _adamw(w, g, m, v)):
                ref[...] = val

    in_specs, args, out_specs, out_shape = [], [], [], []
    for s_, r_, w_, m_, v_ in zip(sums, recvs, ws, ms, vs):
        blk = _shard_block(w_.shape)
        in_specs.append(pl.BlockSpec(blk, lambda t, pos: (pos[0], t, 0)))
        in_specs += [pl.BlockSpec(blk, lambda t, pos, k=k: (k, t, 0)) for k in range(3)]
        in_specs += [pl.BlockSpec(blk, lambda t, pos: (0, t, 0))] * 3
        args += [s_, r_, r_, r_, w_, m_, v_]
        out_specs += [pl.BlockSpec(blk, lambda t, pos: (0, t, 0))] * 4
        out_shape += [jax.ShapeDtypeStruct(w_.shape, F32)] * 4
    return _call(
        body, name=name,
        grid_spec=pltpu.PrefetchScalarGridSpec(
            num_scalar_prefetch=1, grid=(ROW_STEPS,), in_specs=in_specs, out_specs=out_specs),
        out_shape=out_shape, compiler_params=_params("arbitrary"),
    )(pos, *args)


def _tile(n, prefs):
    for t in prefs:
        if n % t == 0:
            return t
    return n


def _mm(a, b, *, name, trans_b=False, add=None, out_dtype=F32, tm=MM_ROWS, b_shards=None):
    m, k = a.shape
    if b_shards is None:
        n = b.shape[0] if trans_b else b.shape[1]
        tn = _tile(n, (1536, 1408, 1024, 768, 512, 256, 128))
        tk = _tile(k, (1408, 1024, 768, 512, 256, 128))
    elif trans_b:
        n, tn, tk = b.shape[1], b.shape[1], b.shape[2]
    else:
        n, tn, tk = b_shards[1] * b.shape[2], b.shape[2], _tile(k, (1024, 512))
    nk = k // tk
    has_add = add is not None

    def body(*refs):
        a_ref, b_ref = refs[0], refs[1]
        add_ref = refs[2] if has_add else None
        o_ref = refs[2 + has_add]
        av = a_ref[...].astype(BF16)
        bv = b_ref[...].astype(BF16)
        if trans_b:
            part = lax.dot_general(av, bv, NT_DIMS, preferred_element_type=F32)
        else:
            part = jnp.dot(av, bv, preferred_element_type=F32)

        def finish(acc):
            if has_add:
                acc = acc + add_ref[...].astype(F32)
            o_ref[...] = acc.astype(o_ref.dtype)

        if nk == 1:
            finish(part)
        else:
            acc_ref = refs[3 + has_add]
            kk = pl.program_id(2)

            @pl.when(kk == 0)
            def _():
                acc_ref[...] = part

            @pl.when(kk > 0)
            def _():
                acc_ref[...] += part

            @pl.when(kk == nk - 1)
            def _():
                finish(acc_ref[...])

    in_specs = [pl.BlockSpec((tm, tk), lambda i, j, kk: (i, kk))]
    if b_shards is not None and trans_b:
        in_specs.append(pl.BlockSpec((None, tn, tk), lambda i, j, kk: (b_shards[0] + kk, j, 0)))
    elif b_shards is not None:
        in_specs.append(pl.BlockSpec((None, tk, tn), lambda i, j, kk: (b_shards[0] + j, kk, 0)))
    elif trans_b:
        in_specs.append(pl.BlockSpec((tn, tk), lambda i, j, kk: (j, kk)))
    else:
        in_specs.append(pl.BlockSpec((tk, tn), lambda i, j, kk: (kk, j)))
    args = [a, b]
    if has_add:
        in_specs.append(pl.BlockSpec((tm, tn), lambda i, j, kk: (i, j)))
        args.append(add)
    return _call(
        body, name=name, grid=(m // tm, n // tn, nk), in_specs=in_specs,
        out_specs=pl.BlockSpec((tm, tn), lambda i, j, kk: (i, j)),
        out_shape=jax.ShapeDtypeStruct((m, n), out_dtype),
        scratch_shapes=[pltpu.VMEM((tm, tn), F32)] if nk > 1 else [],
        compiler_params=_params("parallel", "parallel", "arbitrary"),
    )(*args)


def _mm_tn(a, b, *, name, ts=MM_ROWS, shard_cols=None):
    s, m = a.shape
    n = b.shape[1]
    tm = _tile(m, (1408, 1024, 768, 512, 256, 128))
    tn = shard_cols or _tile(n, (1536, 1408, 1024, 768, 512, 256, 128))
    if shard_cols:
        out_spec = pl.BlockSpec((None, tm, tn), lambda i, j, kk: (j, i, 0))
        out_shape = jax.ShapeDtypeStruct((n // tn, m, tn), F32)
    else:
        out_spec = pl.BlockSpec((tm, tn), lambda i, j, kk: (i, j))
        out_shape = jax.ShapeDtypeStruct((m, n), F32)

    def body(a_ref, b_ref, o_ref):
        kk = pl.program_id(2)
        part = jnp.dot(a_ref[...].astype(BF16).T, b_ref[...].astype(BF16),
                       preferred_element_type=F32)

        @pl.when(kk == 0)
        def _():
            o_ref[...] = part

        @pl.when(kk > 0)
        def _():
            o_ref[...] += part

    return _call(
        body, name=name, grid=(m // tm, n // tn, s // ts),
        in_specs=[pl.BlockSpec((ts, tm), lambda i, j, kk: (kk, i)),
                  pl.BlockSpec((ts, tn), lambda i, j, kk: (kk, j))],
        out_specs=out_spec, out_shape=out_shape,
        compiler_params=_params("parallel", "parallel", "arbitrary"),
    )(a, b)


def _rowwise(fn, rows, consts, row_outs, acc_outs, *, name, tm=TM, n_rows=None):
    rows = [r if isinstance(r, tuple) else (r, r.shape[1], 0, 0) for r in rows]
    s = n_rows or rows[0][0].shape[0]
    nr, nc, no, na = len(rows), len(consts), len(row_outs), len(acc_outs)

    def body(*refs):
        r_in, c_in = refs[:nr], refs[nr:nr + nc]
        o_refs, a_refs = refs[nr + nc:nr + nc + no], refs[nr + nc + no:]
        outs = fn(*[r[...] for r in r_in], *[c[...] for c in c_in])
        for r, v in zip(o_refs, outs[:no]):
            r[...] = v.astype(r.dtype)
        if na:
            i = pl.program_id(0)

            @pl.when(i == 0)
            def _():
                for r, v in zip(a_refs, outs[no:]):
                    r[...] = v.astype(F32)

            @pl.when(i > 0)
            def _():
                for r, v in zip(a_refs, outs[no:]):
                    r[...] += v.astype(F32)

    in_specs = [pl.BlockSpec((tm, w), lambda i, cb=cb, rb=rb: (i + rb, cb)) for _, w, cb, rb in rows]
    in_specs += [pl.BlockSpec(c.shape, lambda i: (0, 0)) for c in consts]
    out_specs = [pl.BlockSpec((tm, w), lambda i: (i, 0)) for w, _ in row_outs]
    out_specs += [pl.BlockSpec(sh, lambda i: (0, 0)) for sh in acc_outs]
    out_shape = [jax.ShapeDtypeStruct((s, w), dt) for w, dt in row_outs]
    out_shape += [jax.ShapeDtypeStruct(sh, F32) for sh in acc_outs]
    return _call(
        body, name=name, grid=(s // tm,), in_specs=in_specs, out_specs=out_specs,
        out_shape=out_shape, compiler_params=_params("arbitrary"),
    )(*[r[0] for r in rows], *consts)


def _rms(x, g, n=None):
    ms = jnp.sum(x * x, axis=-1, keepdims=True) / float(n or x.shape[-1])
    return x * lax.rsqrt(ms + EPS) * g


def _layer_norm(x, g, b):
    mu = jnp.sum(x, axis=-1, keepdims=True) / float(x.shape[-1])
    xc = x - mu
    var = jnp.sum(xc * xc, axis=-1, keepdims=True) / float(x.shape[-1])
    return xc * lax.rsqrt(var + EPS) * g + b


def _silu(x):
    return x * jax.nn.sigmoid(x)


@jax.custom_vjp
def _rope(y, cos, sin_a, sin_b):
    return y * cos + pltpu.roll(y, 112, 1) * sin_a + pltpu.roll(y, 16, 1) * sin_b


def _rope_fwd(y, cos, sin_a, sin_b):
    return _rope(y, cos, sin_a, sin_b), (cos, sin_a, sin_b)


def _rope_bwd(res, ct):
    cos, sin_a, sin_b = res
    dy = ct * cos + pltpu.roll(ct * sin_a, 16, 1) + pltpu.roll(ct * sin_b, 112, 1)
    return dy, jnp.zeros_like(cos), jnp.zeros_like(sin_a), jnp.zeros_like(sin_b)


_rope.defvjp(_rope_fwd, _rope_bwd)


def _qk_head(xh, g, cos, sin_a, sin_b):
    return _rope(_rms(xh, g, MLA_QK), cos, sin_a, sin_b)


def _heads(x, width):
    return [x[:, h * width:(h + 1) * width] for h in range(x.shape[1] // width)]


def _f_rms(x, g):
    return (_rms(x, g),)


def _f_rope_tab(pos, inv_freq):
    ang = pos.astype(F32) * inv_freq
    lane = lax.broadcasted_iota(jnp.int32, ang.shape, 1)
    sn = jnp.sin(ang)
    first = (lane >= MLA_NOPE) & (lane < MLA_NOPE + MLA_ROPE // 2)
    second = (lane >= MLA_NOPE + MLA_ROPE // 2) & (lane < MLA_QK)
    return jnp.cos(ang), jnp.where(first, -sn, 0.0), jnp.where(second, sn, 0.0)


def _mix_pre(za, zg, zcq, zckv, ba, bg, gq, gkv):
    u0 = (za + ba) * jax.nn.sigmoid(zg + bg)
    return u0, _rms(zcq, gq), _rms(zckv, gkv)


def _ln_silu(c1, bdw, lg, lb):
    return _silu(_layer_norm(c1 + bdw, lg, lb))


def _f_qk_prep(q0, kn, kr, cos, sa, sb, gq, gk):
    qs = [_qk_head(xh, gq, cos, sa, sb) * Q_SCALE for xh in _heads(q0, LANES)]
    ks = [_qk_head(xh + kr, gk, cos, sa, sb) for xh in _heads(kn, LANES)]
    return jnp.concatenate(qs, axis=1), jnp.concatenate(ks, axis=1)


def _act(cg, cv, bg, bv):
    return _silu(cg + bg) * (cv + bv)


def _f_mem_k(kk, g):
    return (jnp.concatenate([_rms(xh, g) for xh in _heads(kk, MEM_HEAD_DIM)], axis=1),)


def _mem_probs(qn, kmh):
    s = lax.dot_general(qn.astype(BF16), kmh, NT_DIMS, preferred_element_type=F32) * MEM_SCALE
    e = jnp.exp(s - jnp.max(s, axis=-1, keepdims=True))
    return e / jnp.sum(e, axis=-1, keepdims=True)


def _f_mem_attn(qm0, km, vm, g):
    outs = []
    for h, xh in enumerate(_heads(qm0, MEM_HEAD_DIM)):
        sl = slice(h * MEM_HEAD_DIM, (h + 1) * MEM_HEAD_DIM)
        p = _mem_probs(_rms(xh, g), km[:, sl])
        outs.append(jnp.dot(p.astype(BF16), vm[:, sl].astype(BF16), preferred_element_type=F32))
    return (jnp.concatenate(outs, axis=1),)


def _f_loss(y, t):
    e = y - t
    return e * (1.0 / D_MODEL), jnp.sum(e * e, axis=0, keepdims=True)


def _b_rms(x, dh, dres, g):
    _, vjp = jax.vjp(_rms, x, g)
    dx, dg = vjp(dh)
    return dx + dres, dg


def _b_rms_nores(x, dh, g):
    _, vjp = jax.vjp(_rms, x, g)
    dx, dg = vjp(dh)
    return dx, dg


def _b_mix_pre(za, zg, zcq, zckv, du0, dcqn, dckvn, dkr, ba, bg, gq, gkv):
    _, vjp = jax.vjp(_mix_pre, za, zg, zcq, zckv, ba, bg, gq, gkv)
    dza, dzg, dzcq, dzckv, dba, dbg, dgq, dgkv = vjp((du0, dcqn, dckvn))
    return jnp.concatenate([dza, dzg, dzcq, dzckv, dkr], axis=1), dba, dbg, dgq, dgkv


def _b_ln_silu(c1, du, bdw, lg, lb):
    _, vjp = jax.vjp(_ln_silu, c1, bdw, lg, lb)
    return vjp(du)


def _b_qk_prep(q0, kn, kr, cos, sa, sb, dq, dk, gq, gk):
    head = lambda xh, g: _qk_head(xh, g, cos, sa, sb)
    dq = dq * Q_SCALE
    dq0, dkn = [], []
    dkr = jnp.zeros_like(kr)
    dgq = jnp.zeros_like(gq)
    dgk = jnp.zeros_like(gk)
    for xh, ct in zip(_heads(q0, LANES), _heads(dq, LANES)):
        _, vjp = jax.vjp(head, xh, gq)
        dx, dg = vjp(ct)
        dq0.append(dx)
        dgq = dgq + dg
    for xh, ct in zip(_heads(kn, LANES), _heads(dk, LANES)):
        _, vjp = jax.vjp(head, xh + kr, gk)
        dx, dg = vjp(ct)
        dkn.append(dx)
        dkr = dkr + dx
        dgk = dgk + dg
    return jnp.concatenate(dq0, axis=1), jnp.concatenate(dkn, axis=1), dkr, dgq, dgk


def _b_act(cg, cv, dact, bg, bv):
    _, vjp = jax.vjp(_act, cg, cv, bg, bv)
    return vjp(dact)


def _b_mem_k(kk, dkm, g):
    dkk = []
    dg = jnp.zeros_like(g)
    for xh, ct in zip(_heads(kk, MEM_HEAD_DIM), _heads(dkm, MEM_HEAD_DIM)):
        _, vjp = jax.vjp(_rms, xh, g)
        dx, dgh = vjp(ct)
        dkk.append(dx)
        dg = dg + dgh
    return jnp.concatenate(dkk, axis=1), dg


def _b_mem_attn(dom, qm0, km, vm, g):
    dq0, dkm, dvm = [], [], []
    dg = jnp.zeros_like(g)
    for h, (xh, doh) in enumerate(zip(_heads(qm0, MEM_HEAD_DIM), _heads(dom, MEM_HEAD_DIM))):
        sl = slice(h * MEM_HEAD_DIM, (h + 1) * MEM_HEAD_DIM)
        kmh, vmh = km[:, sl], vm[:, sl].astype(BF16)
        qn, vjp = jax.vjp(_rms, xh, g)
        p = _mem_probs(qn, kmh)
        dob = doh.astype(BF16)
        dp = lax.dot_general(dob, vmh, NT_DIMS, preferred_element_type=F32)
        ds = (p * (dp - jnp.sum(dp * p, axis=-1, keepdims=True)) * MEM_SCALE).astype(BF16)
        dqn = jnp.dot(ds, kmh, preferred_element_type=F32)
        dkm.append(jnp.dot(ds.T, qn.astype(BF16), preferred_element_type=F32))
        dvm.append(jnp.dot(p.astype(BF16).T, dob, preferred_element_type=F32))
        dx, dgh = vjp(dqn)
        dq0.append(dx)
        dg = dg + dgh
    return (jnp.concatenate(dq0, axis=1), jnp.concatenate(dkm, axis=1),
            jnp.concatenate(dvm, axis=1), dg)


def _adamw(w, g, m, v):
    m = ADAM_B1 * m + (1.0 - ADAM_B1) * g
    v = ADAM_B2 * v + (1.0 - ADAM_B2) * jnp.square(g)
    m_hat = m / (1.0 - ADAM_B1 ** ADAM_STEP)
    v_hat = v / (1.0 - ADAM_B2 ** ADAM_STEP)
    delta = -ADAM_LR * (m_hat / (jnp.sqrt(v_hat) + ADAM_EPS) + ADAM_WD * w)
    return delta, m, v


def _adamw_small(parts, small, tiny, name):
    def body(*refs):
        p_ref, ins, outs = refs[0], refs[1:4 + 3 * len(TINY)], refs[4 + 3 * len(TINY):]
        me = 4 * lax.axis_index("x") + 2 * lax.axis_index("y") + lax.axis_index("c")
        groups = [(0, SMALL_ROWS)]
        groups += [(pl.multiple_of(base + me * rows, 8), rows) for base, (_, _, rows) in zip(TINY_BASE, TINY)]
        for k, (start, rows) in enumerate(groups):
            g = p_ref[0, pl.ds(start, rows), :]
            for d in range(1, N_DEV):
                g = g + p_ref[d, pl.ds(start, rows), :]
            w, m, v = [r[...] for r in ins[3 * k:3 * k + 3]]
            for ref, val in zip(outs[4 * k:4 * k + 4], (g,) + _adamw(w, g, m, v)):
                ref[...] = val

    args = list(small) + [t for grp in tiny for t in grp]
    out_shape = []
    for grp in [small] + list(tiny):
        out_shape += [jax.ShapeDtypeStruct(grp[0].shape, F32)] * 4
    return _call(body, name=name, out_shape=out_shape)(parts, *args)


def _conv_fwd(x, w, name):
    s, ch = x.shape
    kw = w.shape[0]
    halo = -(-(kw - 1) // 8) * 8
    r = CONV_ROWS
    n = s // r

    def chunk(window, wv):
        acc = jnp.zeros((r, LANES), F32)
        for k in range(kw):
            shift = kw - 1 - k
            sh = window if shift == 0 else pltpu.roll(window, shift, 0)
            acc = acc + sh[halo:halo + r] * wv[k:k + 1]
        return acc

    def body(x_ref, w_ref, y_ref):
        wv = w_ref[...]
        first = jnp.concatenate([jnp.zeros((halo, LANES), F32), x_ref[0:r]], axis=0)
        y_ref[0:r] = chunk(first, wv)

        def step(i, carry):
            base = pl.multiple_of(i * r, 8)
            y_ref[pl.ds(base, r)] = chunk(x_ref[pl.ds(base - halo, r + halo)], wv)
            return carry

        lax.fori_loop(1, n, step, 0)

    return _call(
        body, name=name, grid=(ch // LANES,),
        in_specs=[pl.BlockSpec((s, LANES), lambda c: (0, c)), pl.BlockSpec((kw, LANES), lambda c: (0, c))],
        out_specs=pl.BlockSpec((s, LANES), lambda c: (0, c)),
        out_shape=jax.ShapeDtypeStruct((s, ch), F32), compiler_params=_params("parallel"),
    )(x, w)


def _conv_bwd(dy, x, w, name):
    s, ch = x.shape
    kw = w.shape[0]
    halo = -(-(kw - 1) // 8) * 8
    r = CONV_ROWS
    n = s // r

    def dx_chunk(window, wv):
        acc = jnp.zeros((r, LANES), F32)
        for k in range(kw):
            shift = kw - 1 - k
            sh = window if shift == 0 else pltpu.roll(window, r + halo - shift, 0)
            acc = acc + sh[0:r] * wv[k:k + 1]
        return acc

    def dw_chunk(xwin, dyc, acc_ref):
        for k in range(kw):
            shift = kw - 1 - k
            sh = xwin if shift == 0 else pltpu.roll(xwin, shift, 0)
            prod = sh[halo:halo + r] * dyc
            acc_ref[k] += jnp.sum(prod.reshape(r // 8, 8, LANES), axis=0)

    def body(dy_ref, x_ref, w_ref, dx_ref, dw_ref, acc_ref):
        wv = w_ref[...]
        acc_ref[...] = jnp.zeros_like(acc_ref)
        xfirst = jnp.concatenate([jnp.zeros((halo, LANES), F32), x_ref[0:r]], axis=0)
        dw_chunk(xfirst, dy_ref[0:r], acc_ref)
        last = jnp.concatenate([dy_ref[s - r:s], jnp.zeros((halo, LANES), F32)], axis=0)
        dx_ref[s - r:s] = dx_chunk(last, wv)

        def step(i, carry):
            base = pl.multiple_of(i * r, 8)
            dw_chunk(x_ref[pl.ds(base - halo, r + halo)], dy_ref[pl.ds(base, r)], acc_ref)
            prev = pl.multiple_of((i - 1) * r, 8)
            dx_ref[pl.ds(prev, r)] = dx_chunk(dy_ref[pl.ds(prev, r + halo)], wv)
            return carry

        lax.fori_loop(1, n, step, 0)
        dw_ref[...] = jnp.sum(acc_ref[...], axis=1)

    spec = pl.BlockSpec((s, LANES), lambda c: (0, c))
    wspec = pl.BlockSpec((kw, LANES), lambda c: (0, c))
    return _call(
        body, name=name, grid=(ch // LANES,), in_specs=[spec, spec, wspec], out_specs=[spec, wspec],
        out_shape=[jax.ShapeDtypeStruct((s, ch), F32), jax.ShapeDtypeStruct((kw, ch), F32)],
        scratch_shapes=[pltpu.VMEM((kw, 8, LANES), F32)], compiler_params=_params("parallel"),
    )(dy, x, w)


HALO = 8


def _conv3(win, w, rows):
    return (pltpu.roll(win, 2, 0)[HALO:HALO + rows] * w[0:1] + pltpu.roll(win, 1, 0)[HALO:HALO + rows] * w[1:2]
            + win[HALO:HALO + rows] * w[2:3])


def _ffn_mid_fwd(up_g, up_v, w_g, w_v, b_g, b_v, name):
    s, width = up_g.shape
    tm, tc = TM, FF_PAD

    def body(pg_ref, g_ref, pv_ref, v_ref, wg_ref, wv_ref, bg_ref, bv_ref, act_ref):
        keep = (pl.program_id(0) > 0).astype(F32)
        cg = _conv3(jnp.concatenate([pg_ref[...] * keep, g_ref[...]], axis=0), wg_ref[...], tm)
        cv = _conv3(jnp.concatenate([pv_ref[...] * keep, v_ref[...]], axis=0), wv_ref[...], tm)
        act_ref[...] = _act(cg, cv, bg_ref[...], bv_ref[...]).astype(act_ref.dtype)

    tile = pl.BlockSpec((tm, tc), lambda i, c: (i, c))
    prev = pl.BlockSpec((HALO, tc), lambda i, c: (jnp.maximum(i * (tm // HALO) - 1, 0), c))
    wspec = pl.BlockSpec((FFN_CONV_WIDTH, tc), lambda i, c: (0, c))
    bspec = pl.BlockSpec((1, tc), lambda i, c: (0, c))
    return _call(
        body, name=name, grid=(s // tm, width // tc),
        in_specs=[prev, tile, prev, tile, wspec, wspec, bspec, bspec], out_specs=tile,
        out_shape=jax.ShapeDtypeStruct((s, width), BF16), compiler_params=_params("parallel", "parallel"),
    )(up_g, up_g, up_v, up_v, w_g, w_v, b_g, b_v)


def _ffn_mid_bwd(up_g, up_v, dact, w_g, w_v, b_g, b_v, name):
    s, width = up_g.shape
    tm, tc = TM, FF_PAD
    n_row = s // tm

    def body(pg_ref, g_ref, ng_ref, pv_ref, v_ref, nv_ref, d_ref, nd_ref, wg_ref, wv_ref, bg_ref, bv_ref,
             dug_ref, duv_ref, dwg_ref, dwv_ref, dbg_ref, dbv_ref):
        i = pl.program_id(1)
        first = (i > 0).astype(F32)
        last = (i < n_row - 1).astype(F32)
        ext = tm + HALO
        wg, wv = wg_ref[...], wv_ref[...]
        xg = jnp.concatenate([pg_ref[...] * first, g_ref[...], ng_ref[...]], axis=0)
        xv = jnp.concatenate([pv_ref[...] * first, v_ref[...], nv_ref[...]], axis=0)
        d_ext = jnp.concatenate([d_ref[...], nd_ref[...] * last], axis=0)
        _, vjp = jax.vjp(lambda cg_, cv_: _act(cg_, cv_, bg_ref[...], bv_ref[...]),
                         _conv3(xg, wg, ext), _conv3(xv, wv, ext))
        dcg, dcv = vjp(d_ext)
        results = []
        for x, w, dc in ((xg, wg, dcg), (xv, wv, dcv)):
            dup = (dc[:tm] * w[2:3] + pltpu.roll(dc, ext - 1, 0)[:tm] * w[1:2]
                   + pltpu.roll(dc, ext - 2, 0)[:tm] * w[0:1])
            own = dc[:tm]
            dw = jnp.concatenate([
                jnp.sum(own * pltpu.roll(x, 2, 0)[HALO:HALO + tm], axis=0, keepdims=True),
                jnp.sum(own * pltpu.roll(x, 1, 0)[HALO:HALO + tm], axis=0, keepdims=True),
                jnp.sum(own * x[HALO:HALO + tm], axis=0, keepdims=True)], axis=0)
            results.append((dup, dw, jnp.sum(own, axis=0, keepdims=True)))
        (dug, dwg, dbg), (duv, dwv, dbv) = results
        dug_ref[...] = dug
        duv_ref[...] = duv

        @pl.when(i == 0)
        def _():
            dwg_ref[...], dwv_ref[...], dbg_ref[...], dbv_ref[...] = dwg, dwv, dbg, dbv

        @pl.when(i > 0)
        def _():
            dwg_ref[...] += dwg
            dwv_ref[...] += dwv
            dbg_ref[...] += dbg
            dbv_ref[...] += dbv

    per = tm // HALO
    tile = pl.BlockSpec((tm, tc), lambda c, i: (i, c))
    prev = pl.BlockSpec((HALO, tc), lambda c, i: (jnp.maximum(i * per - 1, 0), c))
    nxt = pl.BlockSpec((HALO, tc), lambda c, i: (jnp.minimum((i + 1) * per, s // HALO - 1), c))
    wspec = pl.BlockSpec((FFN_CONV_WIDTH, tc), lambda c, i: (0, c))
    bspec = pl.BlockSpec((1, tc), lambda c, i: (0, c))
    wide = jax.ShapeDtypeStruct((s, width), F32)
    return _call(
        body, name=name, grid=(width // tc, n_row),
        in_specs=[prev, tile, nxt, prev, tile, nxt, tile, nxt, wspec, wspec, bspec, bspec],
        out_specs=[tile, tile, wspec, wspec, bspec, bspec],
        out_shape=[wide, wide] + [jax.ShapeDtypeStruct((FFN_CONV_WIDTH, width), F32)] * 2
        + [jax.ShapeDtypeStruct((1, width), F32)] * 2,
        compiler_params=_params("parallel", "arbitrary"),
    )(up_g, up_g, up_g, up_v, up_v, up_v, dact, dact, w_g, w_v, b_g, b_v)


def _chunk_mask(rows_are_queries):
    a = lax.broadcasted_iota(jnp.int32, (TQ, TQ), 0) // CHUNK
    b = lax.broadcasted_iota(jnp.int32, (TQ, TQ), 1) // CHUNK
    return (b <= a) if rows_are_queries else (a <= b)


def _head_lanes(hh):
    return slice(hh * LANES, (hh + 1) * LANES)


def _to_row(col):
    return jnp.broadcast_to(col, (TQ, LANES)).T[0:1, :]


def _flash_specs(s):
    width = HEADS_PER_STEP * LANES
    tile = pl.BlockSpec((TQ, width), lambda h, i: (i, h))
    whole = pl.BlockSpec((s, width), lambda h, i: (0, h))
    row_tile = pl.BlockSpec((HEADS_PER_STEP, 1, 1, TQ), lambda h, i: (h, i, 0, 0))
    row_whole = pl.BlockSpec((HEADS_PER_STEP, s // TQ, 1, TQ), lambda h, i: (h, 0, 0, 0))
    return tile, whole, row_tile, row_whole


def _split_refs(refs, n_in, n_out, ex):
    e_in, e_out = (len(ex.inputs), len(ex.out_shape)) if ex else (0, 0)
    a, b, c = n_in + e_in, n_in + e_in + n_out, n_in + e_in + n_out + e_out
    return refs[:n_in], refs[a:b], (refs[n_in:a], refs[b:c], refs[c:])


def _hosted(ex, ex_refs, grid, when_first):
    if ex is None:
        return
    ids = [pl.program_id(d) for d in range(len(grid))]
    cond = functools.reduce(
        lambda p, q_: p & q_, [i == (0 if when_first else g - 1) for i, g in zip(ids, grid)])

    @pl.when(cond)
    def _():
        (ex.start if when_first else ex.finish)(*ex_refs)


def _host_call(body, ex, name, grid, in_specs, out_specs, out_shape, args):
    e_in, e_out = (len(ex.inputs), len(ex.out_shape)) if ex else (0, 0)
    res = _call(
        body, name=name, grid=grid, in_specs=list(in_specs) + [ANY] * e_in,
        out_specs=list(out_specs) + [ANY] * e_out,
        out_shape=list(out_shape) + (ex.out_shape if ex else []),
        scratch_shapes=ex.scratch if ex else [],
        compiler_params=_params(*["arbitrary"] * len(grid)),
    )(*args, *(ex.inputs if ex else []))
    return res[:len(out_shape)], res[len(out_shape):]


def _flash_fwd(q, k, v, name, ex=None):
    s = q.shape[0]
    nq = s // TQ
    grid = (MLA_HEADS // HEADS_PER_STEP, nq)

    def body(*refs):
        (q_ref, k_ref, v_ref), (o_ref, lse_ref, lse_row_ref), ex_refs = _split_refs(refs, 3, 3, ex)
        _hosted(ex, ex_refs, grid, True)
        i = pl.program_id(1)
        qs = [q_ref[:, _head_lanes(hh)] for hh in range(HEADS_PER_STEP)]

        def step(j, carry, masked):
            base = pl.multiple_of(j * TQ, TQ)
            out = []
            for hh in range(HEADS_PER_STEP):
                m_prev, l_prev, acc = carry[hh]
                kj = k_ref[pl.ds(base, TQ), _head_lanes(hh)]
                vj = v_ref[pl.ds(base, TQ), _head_lanes(hh)]
                sc = lax.dot_general(qs[hh], kj, NT_DIMS, preferred_element_type=F32)
                if masked:
                    sc = jnp.where(_chunk_mask(True), sc, NEG)
                m_new = jnp.maximum(m_prev, jnp.max(sc, axis=-1, keepdims=True))
                alpha = jnp.exp2(m_prev - m_new)
                p = jnp.exp2(sc - m_new)
                l_new = alpha * l_prev + jnp.sum(p, axis=-1, keepdims=True)
                acc = acc * alpha + jnp.dot(p.astype(BF16), vj, preferred_element_type=F32)
                out.append((m_new, l_new, acc))
            return tuple(out)

        init = tuple((jnp.full((TQ, 1), NEG, F32), jnp.zeros((TQ, 1), F32), jnp.zeros((TQ, LANES), F32))
                     for _ in range(HEADS_PER_STEP))
        carry = lax.fori_loop(0, i, functools.partial(step, masked=False), init)
        for hh, (m_fin, l_fin, acc) in enumerate(step(i, carry, True)):
            o_ref[:, _head_lanes(hh)] = acc / l_fin
            lse = m_fin + jnp.log2(l_fin)
            lse_ref[:, _head_lanes(hh)] = jnp.broadcast_to(lse, (TQ, LANES))
            lse_row_ref[hh, 0] = _to_row(lse)
        _hosted(ex, ex_refs, grid, False)

    tile, whole, row_tile, _ = _flash_specs(s)
    wide = jax.ShapeDtypeStruct((s, MLA_HEADS * LANES), F32)
    return _host_call(
        body, ex, name, grid, [tile, whole, whole], [tile, tile, row_tile],
        [wide, wide, jax.ShapeDtypeStruct((MLA_HEADS, nq, 1, TQ), F32)], (q, k, v))


def _flash_bwd_dq(q, k, v, do, o, lse, name, ex=None):
    s = q.shape[0]
    nq = s // TQ
    grid = (MLA_HEADS // HEADS_PER_STEP, nq)

    def body(*refs):
        (q_ref, k_ref, v_ref, do_ref, o_ref, lse_ref), (dq_ref, delta_row_ref), ex_refs = _split_refs(refs, 6, 2, ex)
        _hosted(ex, ex_refs, grid, True)
        i = pl.program_id(1)
        qs, dobs, deltas, lses = [], [], [], []
        for hh in range(HEADS_PER_STEP):
            dov = do_ref[:, _head_lanes(hh)]
            qs.append(q_ref[:, _head_lanes(hh)])
            dobs.append(dov.astype(BF16))
            deltas.append(jnp.sum(dov * o_ref[:, _head_lanes(hh)], axis=-1, keepdims=True))
            lses.append(lse_ref[:, _head_lanes(hh)][:, 0:1])

        def step(j, carry, masked):
            base = pl.multiple_of(j * TQ, TQ)
            out = []
            for hh in range(HEADS_PER_STEP):
                kj = k_ref[pl.ds(base, TQ), _head_lanes(hh)]
                vj = v_ref[pl.ds(base, TQ), _head_lanes(hh)]
                sc = lax.dot_general(qs[hh], kj, NT_DIMS, preferred_element_type=F32)
                if masked:
                    sc = jnp.where(_chunk_mask(True), sc, NEG)
                p = jnp.exp2(sc - lses[hh])
                dp = lax.dot_general(dobs[hh], vj, NT_DIMS, preferred_element_type=F32)
                ds = (p * (dp - deltas[hh])).astype(BF16)
                out.append(carry[hh] + jnp.dot(ds, kj, preferred_element_type=F32))
            return tuple(out)

        init = tuple(jnp.zeros((TQ, LANES), F32) for _ in range(HEADS_PER_STEP))
        carry = lax.fori_loop(0, i, functools.partial(step, masked=False), init)
        for hh, dq in enumerate(step(i, carry, True)):
            dq_ref[:, _head_lanes(hh)] = dq * LN2
            delta_row_ref[hh, 0] = _to_row(deltas[hh])
        _hosted(ex, ex_refs, grid, False)

    tile, whole, row_tile, _ = _flash_specs(s)
    return _host_call(
        body, ex, name, grid, [tile, whole, whole, tile, tile, tile], [tile, row_tile],
        [jax.ShapeDtypeStruct((s, MLA_HEADS * LANES), F32),
         jax.ShapeDtypeStruct((MLA_HEADS, nq, 1, TQ), F32)], (q, k, v, do, o, lse))


def _flash_bwd_dkv(q, k, v, do, lse_row, delta_row, name):
    s = q.shape[0]
    nq = s // TQ

    def body(q_ref, k_ref, v_ref, do_ref, lse_row_ref, delta_row_ref, dk_ref, dv_ref):
        j = pl.program_id(1)
        kjs = [k_ref[:, _head_lanes(hh)] for hh in range(HEADS_PER_STEP)]
        vjs = [v_ref[:, _head_lanes(hh)] for hh in range(HEADS_PER_STEP)]

        def step(i, carry, masked):
            base = pl.multiple_of(i * TQ, TQ)
            out = []
            for hh in range(HEADS_PER_STEP):
                dk, dv = carry[hh]
                qi = q_ref[pl.ds(base, TQ), _head_lanes(hh)]
                dob = do_ref[pl.ds(base, TQ), _head_lanes(hh)].astype(BF16)
                sc_t = lax.dot_general(kjs[hh], qi, NT_DIMS, preferred_element_type=F32)
                if masked:
                    sc_t = jnp.where(_chunk_mask(False), sc_t, NEG)
                p_t = jnp.exp2(sc_t - lse_row_ref[hh, i])
                dv = dv + jnp.dot(p_t.astype(BF16), dob, preferred_element_type=F32)
                dp_t = lax.dot_general(vjs[hh], dob, NT_DIMS, preferred_element_type=F32)
                ds_t = (p_t * (dp_t - delta_row_ref[hh, i])).astype(BF16)
                dk = dk + jnp.dot(ds_t, qi, preferred_element_type=F32)
                out.append((dk, dv))
            return tuple(out)

        zero = jnp.zeros((TQ, LANES), F32)
        carry = step(j, tuple((zero, zero) for _ in range(HEADS_PER_STEP)), True)
        carry = lax.fori_loop(j + 1, nq, functools.partial(step, masked=False), carry)
        for hh, (dk, dv) in enumerate(carry):
            dk_ref[:, _head_lanes(hh)] = dk * LN2
            dv_ref[:, _head_lanes(hh)] = dv

    tile, whole, _, row_whole = _flash_specs(s)
    return _call(
        body, name=name, grid=(MLA_HEADS // HEADS_PER_STEP, nq),
        in_specs=[whole, tile, tile, whole, row_whole, row_whole], out_specs=[tile, tile],
        out_shape=[jax.ShapeDtypeStruct((s, MLA_HEADS * LANES), F32)] * 2,
        compiler_params=_params("parallel", "arbitrary"),
    )(q, k, v, do, lse_row, delta_row)


def _side_by_side(g):
    return g.transpose(1, 0, 2).reshape(g.shape[1], N_DEV * g.shape[2])


def _col_shards(g):
    return g.reshape(g.shape[0], N_DEV, g.shape[1] // N_DEV).transpose(1, 0, 2)


def _pad_last(v, to):
    return jnp.pad(v, [(0, 0)] * (v.ndim - 1) + [(0, to - v.shape[-1])])


def _tiny_rows(v, rows):
    flat = v.reshape(v.shape[:-2] + (-1,))
    return _pad_last(flat, rows * LANES).reshape(v.shape[:-2] + (rows, LANES))


def _pack_small(vals):
    parts = []
    for (n, size), pad in zip(SMALL, SMALL_PAD):
        parts.append(jnp.pad(vals[n].reshape(-1), (0, pad - size)))
    flat = jnp.concatenate(parts)
    return jnp.pad(flat, (0, SMALL_ROWS * LANES - flat.shape[0])).reshape(SMALL_ROWS, LANES)


def _unpack_small(packed):
    flat = packed.reshape(-1)
    out, off = {}, 0
    for (n, size), pad in zip(SMALL, SMALL_PAD):
        out[n] = flat[off:off + size].reshape(1, size)
        off += pad
    return out


def _pad_heads(w, per_head, axis):
    shape = list(w.shape)
    shape[axis:axis + 1] = [MLA_HEADS, per_head]
    w = w.reshape(shape)
    pad = [(0, 0)] * len(shape)
    pad[axis + 1] = (0, LANES - per_head)
    w = jnp.pad(w, pad)
    shape[axis:axis + 2] = [MLA_HEADS * LANES]
    return w.reshape(shape)


def _unpad_heads(w, per_head, axis):
    shape = list(w.shape)
    shape[axis:axis + 1] = [MLA_HEADS, LANES]
    w = w.reshape(shape)
    w = lax.slice_in_dim(w, 0, per_head, axis=axis + 1)
    shape[axis:axis + 2] = [MLA_HEADS * per_head]
    return w.reshape(shape)


def _row(v, pad_to=None):
    v = v.reshape(1, -1)
    if pad_to is not None:
        v = jnp.pad(v, ((0, 0), (0, pad_to - v.shape[1])))
    return v


def kernel(x, mem, positions, mix_norm_g, w_in, b_conv_in, w_conv_dw, b_conv_dw, conv_ln_g, conv_ln_b, q_lat_norm_g, w_uq, kv_lat_norm_g, w_ukv, q_norm_g, k_norm_g, w_out, mem_norm_x_g, mem_norm_m_g, w_mem_q, w_mem_kv, mem_q_norm_g, mem_k_norm_g, w_mem_o, ffn_norm_g, w_up, w_ffn_dw, b_ffn_dw, w_down, loss_target, m_mix_norm_g, m_w_in, m_b_conv_in, m_w_conv_dw, m_b_conv_dw, m_conv_ln_g, m_conv_ln_b, m_q_lat_norm_g, m_w_uq, m_kv_lat_norm_g, m_w_ukv, m_q_norm_g, m_k_norm_g, m_w_out, m_mem_norm_x_g, m_mem_norm_m_g, m_w_mem_q, m_w_mem_kv, m_mem_q_norm_g, m_mem_k_norm_g, m_w_mem_o, m_ffn_norm_g, m_w_up, m_w_ffn_dw, m_b_ffn_dw, m_w_down, v_mix_norm_g, v_w_in, v_b_conv_in, v_w_conv_dw, v_b_conv_dw, v_conv_ln_g, v_conv_ln_b, v_q_lat_norm_g, v_w_uq, v_kv_lat_norm_g, v_w_ukv, v_q_norm_g, v_k_norm_g, v_w_out, v_mem_norm_x_g, v_mem_norm_m_g, v_w_mem_q, v_w_mem_kv, v_mem_q_norm_g, v_mem_k_norm_g, v_w_mem_o, v_ffn_norm_g, v_w_up, v_w_ffn_dw, v_b_ffn_dw, v_w_down):
    a = dict(locals())
    seq = x.shape[1]
    xs = x.reshape(seq, D_MODEL)
    mems = mem.reshape(-1, D_MODEL)
    target = loss_target.reshape(seq, D_MODEL)

    tiny = [n for n, _, _ in TINY]
    shard = lambda n: a[n][0] if n in tiny else a[n][0].astype(BF16)
    pos = jnp.stack([2 * lax.axis_index("x") + lax.axis_index("y"), lax.axis_index("c")]).astype(jnp.int32)
    ag_first = ["w_in", "w_uq", "w_ukv", "w_conv_dw"]
    ag_later = [n for n in [b for b, _ in BIG] + tiny if n not in ag_first]
    wg = dict(zip(ag_first, _all_gather([shard(n) for n in ag_first], "ag_weights_first")))
    wi = _side_by_side(wg["w_in"])
    s3 = 2 * CONV_CH + MLA_Q_RANK + MLA_KV_RANK
    w_in_p = jnp.concatenate([
        wi[:, :s3], jnp.zeros((D_MODEL, MLA_NOPE), BF16), wi[:, s3:],
        jnp.zeros((D_MODEL, LANES - MLA_QK), BF16)], axis=1)
    w_uq_p = _side_by_side(_pad_last(wg["w_uq"], LANES))
    w_uk_p = _side_by_side(_pad_last(wg["w_ukv"][:, :, :MLA_NOPE], LANES))
    w_uv_p = _side_by_side(_pad_last(wg["w_ukv"][:, :, MLA_NOPE:], LANES))
    w_cdw = _side_by_side(wg["w_conv_dw"])

    g_mix, g_qlat, g_kvlat = _row(mix_norm_g), _row(q_lat_norm_g), _row(kv_lat_norm_g)
    b_in = _row(b_conv_in)
    b_in_a, b_in_g = b_in[:, :CONV_CH], b_in[:, CONV_CH:]
    b_cdw, ln_g, ln_b = _row(b_conv_dw), _row(conv_ln_g), _row(conv_ln_b)
    g_q, g_k = _row(q_norm_g, LANES), _row(k_norm_g, LANES)
    g_memx, g_memm = _row(mem_norm_x_g), _row(mem_norm_m_g)
    g_mq, g_mk, g_ffn = _row(mem_q_norm_g), _row(mem_k_norm_g), _row(ffn_norm_g)
    b_f = _pad_last(b_ffn_dw.reshape(N_DEV, FF_SHARD), FF_PAD)
    b_f_g, b_f_v = b_f[:4].reshape(1, D_FF_PAD), b_f[4:].reshape(1, D_FF_PAD)

    freq = ROPE_THETA ** (-jnp.arange(0, MLA_ROPE, 2, dtype=F32) / MLA_ROPE)
    inv_freq = jnp.concatenate([jnp.zeros((MLA_NOPE,), F32), freq, freq,
                                jnp.zeros((LANES - MLA_QK,), F32)]).reshape(1, LANES)
    cos, sin_a, sin_b = _rowwise(_f_rope_tab, [positions.reshape(seq, 1)], [inv_freq],
                                 [(LANES, F32)] * 3, [], name="rope_tables")

    (h1,) = _rowwise(_f_rms, [xs], [g_mix], [(D_MODEL, BF16)], [], name="rms_mix")
    z = _mm(h1, w_in_p, name="mm_in")
    z_rows = [(z, CONV_CH, 0, 0), (z, CONV_CH, 1, 0), (z, MLA_Q_RANK, 4, 0), (z, MLA_KV_RANK, 10, 0)]
    z_kr = (z, LANES, 11, 0)
    u0, cqn, ckvn = _rowwise(
        _mix_pre, z_rows, [b_in_a, b_in_g, g_qlat, g_kvlat],
        [(CONV_CH, F32), (MLA_Q_RANK, BF16), (MLA_KV_RANK, BF16)], [], name="mix_pre")
    c1 = _conv_fwd(u0, w_cdw, "conv31_fwd")
    (u,) = _rowwise(lambda c, b, g, bb: (_ln_silu(c, b, g, bb),), [c1], [b_cdw, ln_g, ln_b],
                    [(CONV_CH, BF16)], [], name="ln_silu")
    q0 = _mm(cqn, w_uq_p, name="mm_uq")
    kn0 = _mm(ckvn, w_uk_p, name="mm_uk")
    v0 = _mm(ckvn, w_uv_p, out_dtype=BF16, name="mm_uv")
    qk_rows = [q0, kn0, z_kr, cos, sin_a, sin_b]
    qh, kh = _rowwise(_f_qk_prep, qk_rows, [g_q, g_k],
                      [(MLA_HEADS * LANES, BF16)] * 2, [], name="qk_prep")
    (attn, lse, lse_row), later = _flash_fwd(
        qh, kh, v0, "flash_fwd", _plan_all_gather([shard(n) for n in ag_later]))
    wg.update(zip(ag_later, later))
    w_out = wg["w_out"].reshape(D_MODEL, D_MODEL)
    w_out_u = w_out[:CONV_CH]
    w_out_a = _pad_heads(w_out[CONV_CH:], MLA_V, 0)
    w_mq, w_mo = wg["w_mem_q"].reshape(D_MODEL, D_MODEL), wg["w_mem_o"].reshape(D_MODEL, D_MODEL)
    w_mkv = _side_by_side(wg["w_mem_kv"])
    w_up_p = _pad_last(wg["w_up"], FF_PAD)
    w_dn = jnp.pad(wg["w_down"].reshape(4, FF_SHARD, D_MODEL),
                   ((0, 0), (0, FF_PAD - FF_SHARD), (0, 0))).reshape(D_FF_PAD, D_MODEL)
    w_fdw = _pad_last(wg["w_ffn_dw"], FF_PAD)
    w_fdw_g = w_fdw[:4].transpose(1, 0, 2).reshape(FFN_CONV_WIDTH, D_FF_PAD)
    w_fdw_v = w_fdw[4:].transpose(1, 0, 2).reshape(FFN_CONV_WIDTH, D_FF_PAD)
    x1 = _mm(u, w_out_u, add=xs, name="mm_out_u")
    x1 = _mm(attn, w_out_a, add=x1, name="mm_out_a")

    (hq,) = _rowwise(_f_rms, [x1], [g_memx], [(D_MODEL, BF16)], [], name="rms_memx")
    (hm,) = _rowwise(_f_rms, [mems], [g_memm], [(D_MODEL, BF16)], [], name="rms_memm", tm=mems.shape[0])
    qm0 = _mm(hq, w_mq, name="mm_memq")
    kvm0 = _mm(hm, w_mkv, name="mm_memkv", tm=mems.shape[0])
    (km,) = _rowwise(_f_mem_k, [(kvm0, D_MODEL, 0, 0)], [g_mk], [(D_MODEL, BF16)], [],
                     name="mem_k", tm=mems.shape[0])
    vm = kvm0[:, D_MODEL:]
    (om,) = _rowwise(_f_mem_attn, [qm0], [km, vm, g_mq], [(D_MODEL, BF16)], [], name="mem_attn")
    x2 = _mm(om, w_mo, add=x1, name="mm_memo")

    (h3,) = _rowwise(_f_rms, [x2], [g_ffn], [(D_MODEL, BF16)], [], name="rms_ffn")
    up_g = _mm(h3, w_up_p, b_shards=(0, 4), name="mm_up_g")
    up_v = _mm(h3, w_up_p, b_shards=(4, 4), name="mm_up_v")
    act = _ffn_mid_fwd(up_g, up_v, w_fdw_g, w_fdw_v, b_f_g, b_f_v, "ffn_mid")
    y = _mm(act, w_dn, add=x2, name="mm_down")
    dy, sq = _rowwise(_f_loss, [y, target], [], [(D_MODEL, F32)], [(1, D_MODEL)], name="loss")
    loss = lax.psum(0.5 * jnp.sum(sq) / D_MODEL, ("x", "y", "c"))

    gw, gs, gt = {}, {}, {}
    gw_dn = _mm_tn(act, dy, name="tn_down").reshape(4, FF_PAD, D_MODEL)
    gw["w_down"] = gw_dn[:, :FF_SHARD].reshape(N_DEV, FF_SHARD // 2, D_MODEL)
    dact = _mm(dy, w_dn, trans_b=True, name="mm_down_t")
    dup_g, dup_v, dwf_g, dwf_v, db_g, db_v = _ffn_mid_bwd(
        up_g, up_v, dact, w_fdw_g, w_fdw_v, b_f_g, b_f_v, "ffn_mid_bwd")
    db_f = jnp.concatenate([db_g.reshape(4, FF_PAD), db_v.reshape(4, FF_PAD)], axis=0)
    gs["b_ffn_dw"] = db_f[:, :FF_SHARD].reshape(1, 2 * D_FF)
    dwf = jnp.concatenate([dwf_g.reshape(FFN_CONV_WIDTH, 4, FF_PAD), dwf_v.reshape(FFN_CONV_WIDTH, 4, FF_PAD)], axis=1)
    gt["w_ffn_dw"] = dwf[:, :, :FF_SHARD].transpose(1, 0, 2)
    gw_up = jnp.concatenate([_mm_tn(h3, dup_g, shard_cols=FF_PAD, name="tn_up_g"),
                             _mm_tn(h3, dup_v, shard_cols=FF_PAD, name="tn_up_v")], axis=0)
    gw["w_up"] = gw_up[:, :, :FF_SHARD]
    dh3 = _mm(dup_g, w_up_p, trans_b=True, b_shards=(0, 4), name="mm_up_g_t")
    dh3 = _mm(dup_v, w_up_p, trans_b=True, b_shards=(4, 4), add=dh3, name="mm_up_v_t")
    dx2, gs["ffn_norm_g"] = _rowwise(_b_rms, [x2, dh3, dy], [g_ffn], [(D_MODEL, F32)], [(1, D_MODEL)],
                                     name="rms_ffn_bwd")

    gw["w_mem_o"] = _mm_tn(om, dx2, name="tn_memo").reshape(N_DEV, -1, D_MODEL)
    dom = _mm(dx2, w_mo, trans_b=True, name="mm_memo_t")
    n_mem = mems.shape[0]
    dqm0, dkm, dvm, gs["mem_q_norm_g"] = _rowwise(
        _b_mem_attn, [dom, qm0], [km, vm, g_mq], [(D_MODEL, F32)],
        [(n_mem, D_MODEL), (n_mem, D_MODEL), (1, MEM_HEAD_DIM)], name="mem_attn_bwd")
    gw["w_mem_q"] = _mm_tn(hq, dqm0, name="tn_memq").reshape(N_DEV, -1, D_MODEL)
    dhq = _mm(dqm0, w_mq, trans_b=True, name="mm_memq_t")
    dx1, gs["mem_norm_x_g"] = _rowwise(_b_rms, [x1, dhq, dx2], [g_memx], [(D_MODEL, F32)],
                                       [(1, D_MODEL)], name="rms_memx_bwd")
    dkk, gs["mem_k_norm_g"] = _rowwise(_b_mem_k, [(kvm0, D_MODEL, 0, 0), dkm], [g_mk],
                                       [(D_MODEL, F32)], [(1, MEM_HEAD_DIM)], name="mem_k_bwd", tm=n_mem)
    dkvm0 = jnp.concatenate([dkk, dvm], axis=1)
    gw["w_mem_kv"] = _col_shards(_mm_tn(hm, dkvm0, name="tn_memkv", ts=n_mem))
    dhm = _mm(dkvm0, w_mkv, trans_b=True, name="mm_memkv_t", tm=n_mem)
    _, gs["mem_norm_m_g"] = _rowwise(_b_rms_nores, [mems, dhm], [g_memm], [(D_MODEL, F32)],
                                     [(1, D_MODEL)], name="rms_memm_bwd", tm=n_mem)

    gw_out_u = _mm_tn(u, dx1, name="tn_out_u")
    gw_out_a = _mm_tn(attn, dx1, name="tn_out_a")
    gw["w_out"] = jnp.concatenate([gw_out_u, _unpad_heads(gw_out_a, MLA_V, 0)], axis=0).reshape(N_DEV, -1, D_MODEL)
    du = _mm(dx1, w_out_u, trans_b=True, name="mm_out_u_t")
    dattn = _mm(dx1, w_out_a, trans_b=True, name="mm_out_a_t")
    dc1, gs["b_conv_dw"], gs["conv_ln_g"], gs["conv_ln_b"] = _rowwise(
        _b_ln_silu, [c1, du], [b_cdw, ln_g, ln_b], [(CONV_CH, F32)], [(1, CONV_CH)] * 3, name="ln_silu_bwd")
    du0, g_cdw = _conv_bwd(dc1, u0, w_cdw, "conv31_bwd")
    gt["w_conv_dw"] = _col_shards(g_cdw)
    rs_first = ["w_up", "w_down", "w_mem_o", "w_mem_q", "w_mem_kv", "w_out"]
    grads = [gw[n] for n in rs_first]
    sums_first = _rs_add(grads, _swap_sibling(grads, "rs_sibling_first"), pos, "rs_add_first")
    (dqh, delta_row), recvs_first = _flash_bwd_dq(
        qh, kh, v0, dattn, attn, lse, "flash_bwd_dq", _plan_swap_chips(sums_first))
    dkh, dv0 = _flash_bwd_dkv(qh, kh, v0, dattn, lse_row, delta_row, "flash_bwd_dkv")
    dq0, dkn0, dkr, dgq, dgk = _rowwise(
        _b_qk_prep, qk_rows + [dqh, dkh], [g_q, g_k],
        [(MLA_HEADS * LANES, F32)] * 2 + [(LANES, F32)], [(1, LANES)] * 2, name="qk_prep_bwd")
    gs["q_norm_g"], gs["k_norm_g"] = dgq[:, :MLA_QK], dgk[:, :MLA_QK]
    gw["w_uq"] = _col_shards(_mm_tn(cqn, dq0, name="tn_uq"))[:, :, :MLA_QK]
    g_uk = _col_shards(_mm_tn(ckvn, dkn0, name="tn_uk"))[:, :, :MLA_NOPE]
    g_uv = _col_shards(_mm_tn(ckvn, dv0, name="tn_uv"))[:, :, :MLA_V]
    gw["w_ukv"] = jnp.concatenate([g_uk, g_uv], axis=2)
    dcqn = _mm(dq0, w_uq_p, trans_b=True, name="mm_uq_t")
    dckvn = _mm(dkn0, w_uk_p, trans_b=True, name="mm_uk_t")
    dckvn = _mm(dv0, w_uv_p, trans_b=True, add=dckvn, name="mm_uv_t")
    dz, dba, dbg, gs["q_lat_norm_g"], gs["kv_lat_norm_g"] = _rowwise(
        _b_mix_pre, z_rows + [du0, dcqn, dckvn, dkr], [b_in_a, b_in_g, g_qlat, g_kvlat],
        [(IN_PAD, F32)], [(1, CONV_CH)] * 2 + [(1, MLA_Q_RANK), (1, MLA_KV_RANK)], name="mix_pre_bwd")
    gs["b_conv_in"] = jnp.concatenate([dba, dbg], axis=1)
    gw_in = _mm_tn(h1, dz, name="tn_in")
    gw["w_in"] = _col_shards(jnp.concatenate([gw_in[:, :s3], gw_in[:, s3 + MLA_NOPE:s3 + MLA_QK]], axis=1))
    dh1 = _mm(dz, w_in_p, trans_b=True, name="mm_in_t")
    dx, gs["mix_norm_g"] = _rowwise(_b_rms, [xs, dh1, dx1], [g_mix], [(D_MODEL, F32)], [(1, D_MODEL)],
                                    name="rms_mix_bwd")

    rs_last = [n for n, _ in BIG if n not in rs_first]
    grads = [gw[n] for n in rs_last]
    sums_last = _rs_add(grads, _swap_sibling(grads, "rs_sibling_last"), pos, "rs_add_last")
    recvs_last = _run_exchange(_plan_swap_chips(sums_last), "rs_chips_last")
    big = rs_first + rs_last
    flat = _adamw_big(list(sums_first) + list(sums_last), list(recvs_first) + list(recvs_last),
                      [a[n] for n in big], [a["m_" + n] for n in big], [a["v_" + n] for n in big],
                      pos, "adamw_big")
    res = [{n: flat[4 * i + k] for i, n in enumerate(big)} for k in range(4)]

    part = jnp.concatenate(
        [_pack_small(gs)] + [_tiny_rows(gt[n], rows).reshape(N_DEV * rows, LANES) for n, _, rows in TINY], axis=0)
    (parts,) = _all_gather([part], "ag_small_grads")
    small_in = [_pack_small({n: a[p + n] for n, _ in SMALL}) for p in ("", "m_", "v_")]
    tiny_in = [[_tiny_rows(a[p + n][0], rows) for p in ("", "m_", "v_")] for n, _, rows in TINY]
    flat = _adamw_small(parts, small_in, tiny_in, "adamw_small")
    for k in range(4):
        res[k].update(_unpack_small(flat[k]))
        for i, (n, shape, _) in enumerate(TINY):
            res[k][n] = flat[4 * (i + 1) + k].reshape(-1)[:math.prod(shape)].reshape((1,) + shape)

    return (loss, dx.reshape(1, seq, D_MODEL), *[res[k][n] for k in range(4) for n in WEIGHTS])
```

```python
import functools
import math

import jax
import jax.numpy as jnp
from jax import lax
from jax.experimental import pallas as pl
from jax.experimental.pallas import tpu as pltpu

F32 = jnp.float32
BF16 = jnp.bfloat16
EPS = 1e-6
LANES = 128
N_DEV = 8
D_MODEL = 1024
CONV_CH = 512
CONV_WIDTH = 31
MLA_HEADS = 8
MLA_NOPE = 64
MLA_ROPE = 32
MLA_V = 64
MLA_QK = MLA_NOPE + MLA_ROPE
MLA_Q_RANK = 256
MLA_KV_RANK = 128
ROPE_THETA = 10000.0
IN_COLS = 2 * CONV_CH + MLA_Q_RANK + MLA_KV_RANK + MLA_ROPE
IN_PAD = 2 * CONV_CH + MLA_Q_RANK + MLA_KV_RANK + LANES
MEM_HEADS = 4
MEM_HEAD_DIM = 256
D_FF = 2816
FFN_CONV_WIDTH = 3
CHUNK = 64
ATT_SCALE = 1.0 / math.sqrt(MLA_QK)
LN2 = math.log(2.0)
Q_SCALE = ATT_SCALE / LN2
MEM_SCALE = 1.0 / math.sqrt(MEM_HEAD_DIM)
ADAM_LR, ADAM_B1, ADAM_B2, ADAM_EPS, ADAM_WD, ADAM_STEP = 0.001, 0.9, 0.999, 1e-08, 0.01, 10

TM = 512
MM_ROWS = 1024
TQ = 512
HEADS_PER_STEP = 2
CONV_ROWS = 256
NEG = -1e30
VMEM_LIMIT = 56 * 1024 * 1024

MESH = pl.DeviceIdType.MESH
ANY = pl.BlockSpec(memory_space=pl.ANY)
NT_DIMS = (((1,), (1,)), ((), ()))

BIG = [
    ("w_in", (1024, 180)), ("w_uq", (256, 96)), ("w_ukv", (128, 128)), ("w_out", (128, 1024)),
    ("w_mem_q", (128, 1024)), ("w_mem_kv", (1024, 256)), ("w_mem_o", (128, 1024)),
    ("w_up", (1024, 704)), ("w_down", (352, 1024)),
]
TINY = [("w_conv_dw", (31, 64), 16), ("w_ffn_dw", (3, 704), 24)]
ROW_STEPS = 4
FF_SHARD = D_FF // 4
FF_PAD = 768
D_FF_PAD = 4 * FF_PAD
SMALL = [
    ("mix_norm_g", 1024), ("b_conv_in", 1024), ("b_conv_dw", 512), ("conv_ln_g", 512),
    ("conv_ln_b", 512), ("q_lat_norm_g", 256), ("kv_lat_norm_g", 128), ("q_norm_g", 96),
    ("k_norm_g", 96), ("mem_norm_x_g", 1024), ("mem_norm_m_g", 1024), ("mem_q_norm_g", 256),
    ("mem_k_norm_g", 256), ("ffn_norm_g", 1024), ("b_ffn_dw", 5632),
]
WEIGHTS = [
    "mix_norm_g", "w_in", "b_conv_in", "w_conv_dw", "b_conv_dw", "conv_ln_g", "conv_ln_b",
    "q_lat_norm_g", "w_uq", "kv_lat_norm_g", "w_ukv", "q_norm_g", "k_norm_g", "w_out",
    "mem_norm_x_g", "mem_norm_m_g", "w_mem_q", "w_mem_kv", "mem_q_norm_g", "mem_k_norm_g",
    "w_mem_o", "ffn_norm_g", "w_up", "w_ffn_dw", "b_ffn_dw", "w_down",
]


SMALL_PAD = [(-(-n // LANES)) * LANES for _, n in SMALL]
SMALL_ROWS = -(-sum(SMALL_PAD) // (8 * LANES)) * 8
TINY_BASE = [SMALL_ROWS + N_DEV * sum(r for _, _, r in TINY[:i]) for i in range(len(TINY))]
PART_ROWS = SMALL_ROWS + N_DEV * sum(r for _, _, r in TINY)


def _call(body, **kw):
    return pl.pallas_call(body, **kw)


def _params(*sem):
    return pltpu.CompilerParams(dimension_semantics=sem, vmem_limit_bytes=VMEM_LIMIT)


class _Exchange:
    def __init__(self, inputs, out_shape, scratch, start, finish):
        self.inputs, self.out_shape, self.scratch = list(inputs), list(out_shape), list(scratch)
        self.start, self.finish = start, finish


def _run_exchange(ex, name):
    n_in, n_out = len(ex.inputs), len(ex.out_shape)

    def body(*refs):
        parts = refs[:n_in], refs[n_in:n_in + n_out], refs[n_in + n_out:]
        ex.start(*parts)
        ex.finish(*parts)

    return _call(body, name=name, out_shape=ex.out_shape, in_specs=[ANY] * n_in,
                 out_specs=[ANY] * n_out, scratch_shapes=ex.scratch)(*ex.inputs)


def _plan_all_gather(xs):
    n = len(xs)

    def copies(x_refs, out_refs, sems):
        send_sems, recv_sems, local_sems = sems
        x, y, c = lax.axis_index("x"), lax.axis_index("y"), lax.axis_index("c")
        me, sibling = (x, y, c), (x, y, 1 - c)
        chips = [(1 - x, y), (x, 1 - y), (1 - x, 1 - y)]

        def slot(o, px, py, pc):
            return out_refs[o].at[4 * px + 2 * py + pc]

        def copy(o, k, block, to, src=None):
            return pltpu.make_async_remote_copy(
                src_ref=slot(o, *block) if src is None else src, dst_ref=slot(o, *block),
                send_sem=send_sems.at[o, k], recv_sem=recv_sems.at[o, k],
                device_id=to, device_id_type=MESH)

        mine = [pltpu.make_async_copy(x_refs[o], slot(o, *me), local_sems.at[o]) for o in range(n)]
        first = [copy(o, 0, me, sibling, src=x_refs[o]) for o in range(n)]
        first += [copy(o, 1 + j, me, (*chip, c), src=x_refs[o])
                  for j, chip in enumerate(chips) for o in range(n)]
        return me, sibling, chips, copy, mine, first

    def start(x_refs, out_refs, sems):
        _, _, _, _, mine, first = copies(x_refs, out_refs, sems)
        for cp in mine + first:
            cp.start()

    def finish(x_refs, out_refs, sems):
        me, sibling, chips, copy, mine, first = copies(x_refs, out_refs, sems)
        c = me[2]
        passed = []
        for j, chip in enumerate(chips):
            for o in range(n):
                copy(o, 1 + j, (*chip, c), me).wait_recv()
                passed.append(copy(o, 4 + j, (*chip, c), sibling))
                passed[-1].start()
        for o in range(n):
            copy(o, 0, sibling, me).wait_recv()
        for j, chip in enumerate(chips):
            for o in range(n):
                copy(o, 4 + j, (*chip, 1 - c), me).wait_recv()
        for cp in first + passed:
            cp.wait_send()
        for cp in mine:
            cp.wait()

    return _Exchange(
        xs, [jax.ShapeDtypeStruct((N_DEV,) + v.shape, v.dtype) for v in xs],
        [pltpu.SemaphoreType.DMA((n, 7)), pltpu.SemaphoreType.DMA((n, 7)), pltpu.SemaphoreType.DMA((n,))],
        start, finish)


def _all_gather(xs, name):
    return _run_exchange(_plan_all_gather(xs), name)


def _swap_sibling(grads, name):
    n = len(grads)

    def body(*refs):
        g_refs, got_refs = refs[:n], refs[n:2 * n]
        send_sems, recv_sems = refs[2 * n:]
        x, y, c = lax.axis_index("x"), lax.axis_index("y"), lax.axis_index("c")
        copies = [
            pltpu.make_async_remote_copy(
                src_ref=g_refs[o].at[2 * chip + 1 - c], dst_ref=got_refs[o].at[chip],
                send_sem=send_sems.at[o, chip], recv_sem=recv_sems.at[o, chip],
                device_id=(x, y, 1 - c), device_id_type=MESH)
            for o in range(n) for chip in range(4)]
        for cp in copies:
            cp.start()
        for cp in copies:
            cp.wait()

    return _call(
        body, name=name,
        out_shape=[jax.ShapeDtypeStruct((4,) + g.shape[1:], g.dtype) for g in grads],
        in_specs=[ANY] * n, out_specs=[ANY] * n,
        scratch_shapes=[pltpu.SemaphoreType.DMA((n, 4)), pltpu.SemaphoreType.DMA((n, 4))],
    )(*grads)


def _plan_swap_chips(sums):
    n = len(sums)

    def copies(a_refs, r_refs, sems):
        send_sems, recv_sems = sems
        x, y, c = lax.axis_index("x"), lax.axis_index("y"), lax.axis_index("c")
        peers = [(x, 1 - y), (1 - x, y), (1 - x, 1 - y)]
        return [
            pltpu.make_async_remote_copy(
                src_ref=a_refs[o].at[2 * px + py], dst_ref=r_refs[o].at[k],
                send_sem=send_sems.at[o, k], recv_sem=recv_sems.at[o, k],
                device_id=(px, py, c), device_id_type=MESH)
            for k, (px, py) in enumerate(peers) for o in range(n)]

    def start(a_refs, r_refs, sems):
        for cp in copies(a_refs, r_refs, sems):
            cp.start()

    def finish(a_refs, r_refs, sems):
        for cp in copies(a_refs, r_refs, sems):
            cp.wait()

    return _Exchange(
        sums, [jax.ShapeDtypeStruct((3,) + a.shape[1:], a.dtype) for a in sums],
        [pltpu.SemaphoreType.DMA((n, 3)), pltpu.SemaphoreType.DMA((n, 3))], start, finish)


def _shard_block(shape):
    return (None, shape[-2] // ROW_STEPS, shape[-1])


def _rs_add(grads, gots, pos, name):
    n = len(grads)

    def body(pos_ref, *refs):
        for g_ref, t_ref, o_ref in zip(refs[:n], refs[n:2 * n], refs[2 * n:]):
            o_ref[...] = g_ref[...] + t_ref[...]

    in_specs = [pl.BlockSpec(_shard_block(g.shape), lambda a, t, pos: (2 * a + pos[1], t, 0)) for g in grads]
    in_specs += [pl.BlockSpec(_shard_block(g.shape), lambda a, t, pos: (a, t, 0)) for g in gots]
    return _call(
        body, name=name,
        grid_spec=pltpu.PrefetchScalarGridSpec(
            num_scalar_prefetch=1, grid=(4, ROW_STEPS), in_specs=in_specs,
            out_specs=[pl.BlockSpec(_shard_block(g.shape), lambda a, t, pos: (a, t, 0)) for g in gots]),
        out_shape=[jax.ShapeDtypeStruct(g.shape, g.dtype) for g in gots],
        compiler_params=_params("arbitrary", "arbitrary"),
    )(pos, *grads, *gots)


def _adamw_big(sums, recvs, ws, ms, vs, pos, name):
    n = len(sums)

    def body(pos_ref, *refs):
        ins, outs = refs[:7 * n], refs[7 * n:]
        for o in range(n):
            own, r1, r2, r3, w, m, v = [r[...] for r in ins[7 * o:7 * o + 7]]
            g = ((own + r1) + r2) + r3
            for ref, val in zip(outs[4 * o:4 * o + 4], (g,) + _adamw(w, g, m, v)):
                ref[...] = val

    in_specs, args, out_specs, out_shape = [], [], [], []
    for s_, r_, w_, m_, v_ in zip(sums, recvs, ws, ms, vs):
        blk = _shard_block(w_.shape)
        in_specs.append(pl.BlockSpec(blk, lambda t, pos: (pos[0], t, 0)))
        in_specs += [pl.BlockSpec(blk, lambda t, pos, k=k: (k, t, 0)) for k in range(3)]
        in_specs += [pl.BlockSpec(blk, lambda t, pos: (0, t, 0))] * 3
        args += [s_, r_, r_, r_, w_, m_, v_]
        out_specs += [pl.BlockSpec(blk, lambda t, pos: (0, t, 0))] * 4
        out_shape += [jax.ShapeDtypeStruct(w_.shape, F32)] * 4
    return _call(
        body, name=name,
        grid_spec=pltpu.PrefetchScalarGridSpec(
            num_scalar_prefetch=1, grid=(ROW_STEPS,), in_specs=in_specs, out_specs=out_specs),
        out_shape=out_shape, compiler_params=_params("arbitrary"),
    )(pos, *args)


def _tile(n, prefs):
    for t in prefs:
        if n % t == 0:
            return t
    return n


def _mm(a, b, *, name, trans_b=False, add=None, out_dtype=F32, tm=MM_ROWS, b_shards=None):
    m, k = a.shape
    if b_shards is None:
        n = b.shape[0] if trans_b else b.shape[1]
        tn = _tile(n, (1536, 1408, 1024, 768, 512, 256, 128))
        tk = _tile(k, (1408, 1024, 768, 512, 256, 128))
    elif trans_b:
        n, tn, tk = b.shape[1], b.shape[1], b.shape[2]
    else:
        n, tn, tk = b_shards[1] * b.shape[2], b.shape[2], _tile(k, (1024, 512))
    nk = k // tk
    has_add = add is not None

    def body(*refs):
        a_ref, b_ref = refs[0], refs[1]
        add_ref = refs[2] if has_add else None
        o_ref = refs[2 + has_add]
        av = a_ref[...].astype(BF16)
        bv = b_ref[...].astype(BF16)
        if trans_b:
            part = lax.dot_general(av, bv, NT_DIMS, preferred_element_type=F32)
        else:
            part = jnp.dot(av, bv, preferred_element_type=F32)

        def finish(acc):
            if has_add:
                acc = acc + add_ref[...].astype(F32)
            o_ref[...] = acc.astype(o_ref.dtype)

        if nk == 1:
            finish(part)
        else:
            acc_ref = refs[3 + has_add]
            kk = pl.program_id(2)

            @pl.when(kk == 0)
            def _():
                acc_ref[...] = part

            @pl.when(kk > 0)
            def _():
                acc_ref[...] += part

            @pl.when(kk == nk - 1)
            def _():
                finish(acc_ref[...])

    in_specs = [pl.BlockSpec((tm, tk), lambda i, j, kk: (i, kk))]
    if b_shards is not None and trans_b:
        in_specs.append(pl.BlockSpec((None, tn, tk), lambda i, j, kk: (b_shards[0] + kk, j, 0)))
    elif b_shards is not None:
        in_specs.append(pl.BlockSpec((None, tk, tn), lambda i, j, kk: (b_shards[0] + j, kk, 0)))
    elif trans_b:
        in_specs.append(pl.BlockSpec((tn, tk), lambda i, j, kk: (j, kk)))
    else:
        in_specs.append(pl.BlockSpec((tk, tn), lambda i, j, kk: (kk, j)))
    args = [a, b]
    if has_add:
        in_specs.append(pl.BlockSpec((tm, tn), lambda i, j, kk: (i, j)))
        args.append(add)
    return _call(
        body, name=name, grid=(m // tm, n // tn, nk), in_specs=in_specs,
        out_specs=pl.BlockSpec((tm, tn), lambda i, j, kk: (i, j)),
        out_shape=jax.ShapeDtypeStruct((m, n), out_dtype),
        scratch_shapes=[pltpu.VMEM((tm, tn), F32)] if nk > 1 else [],
        compiler_params=_params("parallel", "parallel", "arbitrary"),
    )(*args)


def _mm_tn(a, b, *, name, ts=MM_ROWS, shard_cols=None):
    s, m = a.shape
    n = b.shape[1]
    tm = _tile(m, (1408, 1024, 768, 512, 256, 128))
    tn = shard_cols or _tile(n, (1536, 1408, 1024, 768, 512, 256, 128))
    if shard_cols:
        out_spec = pl.BlockSpec((None, tm, tn), lambda i, j, kk: (j, i, 0))
        out_shape = jax.ShapeDtypeStruct((n // tn, m, tn), F32)
    else:
        out_spec = pl.BlockSpec((tm, tn), lambda i, j, kk: (i, j))
        out_shape = jax.ShapeDtypeStruct((m, n), F32)

    def body(a_ref, b_ref, o_ref):
        kk = pl.program_id(2)
        part = jnp.dot(a_ref[...].astype(BF16).T, b_ref[...].astype(BF16),
                       preferred_element_type=F32)

        @pl.when(kk == 0)
        def _():
            o_ref[...] = part

        @pl.when(kk > 0)
        def _():
            o_ref[...] += part

    return _call(
        body, name=name, grid=(m // tm, n // tn, s // ts),
        in_specs=[pl.BlockSpec((ts, tm), lambda i, j, kk: (kk, i)),
                  pl.BlockSpec((ts, tn), lambda i, j, kk: (kk, j))],
        out_specs=out_spec, out_shape=out_shape,
        compiler_params=_params("parallel", "parallel", "arbitrary"),
    )(a, b)


def _rowwise(fn, rows, consts, row_outs, acc_outs, *, name, tm=TM, n_rows=None):
    rows = [r if isinstance(r, tuple) else (r, r.shape[1], 0, 0) for r in rows]
    s = n_rows or rows[0][0].shape[0]
    nr, nc, no, na = len(rows), len(consts), len(row_outs), len(acc_outs)

    def body(*refs):
        r_in, c_in = refs[:nr], refs[nr:nr + nc]
        o_refs, a_refs = refs[nr + nc:nr + nc + no], refs[nr + nc + no:]
        outs = fn(*[r[...] for r in r_in], *[c[...] for c in c_in])
        for r, v in zip(o_refs, outs[:no]):
            r[...] = v.astype(r.dtype)
        if na:
            i = pl.program_id(0)

            @pl.when(i == 0)
            def _():
                for r, v in zip(a_refs, outs[no:]):
                    r[...] = v.astype(F32)

            @pl.when(i > 0)
            def _():
                for r, v in zip(a_refs, outs[no:]):
                    r[...] += v.astype(F32)

    in_specs = [r[1] if isinstance(r[1], pl.BlockSpec) else
                pl.BlockSpec((tm, r[1]), lambda i, cb=r[2], rb=r[3]: (i + rb, cb)) for r in rows]
    in_specs += [pl.BlockSpec(c.shape, lambda i: (0, 0)) for c in consts]
    out_specs = [pl.BlockSpec((tm, w), lambda i: (i, 0)) for w, _ in row_outs]
    out_specs += [pl.BlockSpec(sh, lambda i: (0, 0)) for sh in acc_outs]
    out_shape = [jax.ShapeDtypeStruct((s, w), dt) for w, dt in row_outs]
    out_shape += [jax.ShapeDtypeStruct(sh, F32) for sh in acc_outs]
    return _call(
        body, name=name, grid=(s // tm,), in_specs=in_specs, out_specs=out_specs,
        out_shape=out_shape, compiler_params=_params("arbitrary"),
    )(*[r[0] for r in rows], *consts)


def _rms(x, g, n=None):
    ms = jnp.sum(x * x, axis=-1, keepdims=True) / float(n or x.shape[-1])
    return x * lax.rsqrt(ms + EPS) * g


def _layer_norm(x, g, b):
    mu = jnp.sum(x, axis=-1, keepdims=True) / float(x.shape[-1])
    xc = x - mu
    var = jnp.sum(xc * xc, axis=-1, keepdims=True) / float(x.shape[-1])
    return xc * lax.rsqrt(var + EPS) * g + b


def _silu(x):
    return x * jax.nn.sigmoid(x)


@jax.custom_vjp
def _rope(y, cos, sin_a, sin_b):
    return y * cos + pltpu.roll(y, 112, 1) * sin_a + pltpu.roll(y, 16, 1) * sin_b


def _rope_fwd(y, cos, sin_a, sin_b):
    return _rope(y, cos, sin_a, sin_b), (cos, sin_a, sin_b)


def _rope_bwd(res, ct):
    cos, sin_a, sin_b = res
    dy = ct * cos + pltpu.roll(ct * sin_a, 16, 1) + pltpu.roll(ct * sin_b, 112, 1)
    return dy, jnp.zeros_like(cos), jnp.zeros_like(sin_a), jnp.zeros_like(sin_b)


_rope.defvjp(_rope_fwd, _rope_bwd)


def _qk_head(xh, g, cos, sin_a, sin_b):
    return _rope(_rms(xh, g, MLA_QK), cos, sin_a, sin_b)


def _heads(x, width):
    return [x[:, h * width:(h + 1) * width] for h in range(x.shape[1] // width)]


def _f_rms(x, g):
    return (_rms(x, g),)


def _f_rope_tab(pos, inv_freq):
    ang = pos.astype(F32) * inv_freq
    lane = lax.broadcasted_iota(jnp.int32, ang.shape, 1)
    sn = jnp.sin(ang)
    first = (lane >= MLA_NOPE) & (lane < MLA_NOPE + MLA_ROPE // 2)
    second = (lane >= MLA_NOPE + MLA_ROPE // 2) & (lane < MLA_QK)
    return jnp.cos(ang), jnp.where(first, -sn, 0.0), jnp.where(second, sn, 0.0)


def _mix_pre(za, zg, zcq, zckv, ba, bg, gq, gkv):
    u0 = (za + ba) * jax.nn.sigmoid(zg + bg)
    return u0, _rms(zcq, gq), _rms(zckv, gkv)


def _ln_silu(c1, bdw, lg, lb):
    return _silu(_layer_norm(c1 + bdw, lg, lb))


def _f_qk_prep(q0, kn, kr, cos, sa, sb, gq, gk):
    qs = [_qk_head(xh, gq, cos, sa, sb) * Q_SCALE for xh in _heads(q0, LANES)]
    ks = [_qk_head(xh + kr, gk, cos, sa, sb) for xh in _heads(kn, LANES)]
    return jnp.concatenate(qs, axis=1), jnp.concatenate(ks, axis=1)


def _act(cg, cv, bg, bv):
    return _silu(cg + bg) * (cv + bv)


def _f_mem_k(kk, g):
    return (jnp.concatenate([_rms(xh, g) for xh in _heads(kk, MEM_HEAD_DIM)], axis=1),)


def _mem_probs(qn, kmh):
    s = lax.dot_general(qn.astype(BF16), kmh, NT_DIMS, preferred_element_type=F32) * MEM_SCALE
    e = jnp.exp(s - jnp.max(s, axis=-1, keepdims=True))
    return e / jnp.sum(e, axis=-1, keepdims=True)


def _f_mem_attn(qm0, km, vm, g):
    outs = []
    for h, xh in enumerate(_heads(qm0, MEM_HEAD_DIM)):
        sl = slice(h * MEM_HEAD_DIM, (h + 1) * MEM_HEAD_DIM)
        p = _mem_probs(_rms(xh, g), km[:, sl])
        outs.append(jnp.dot(p.astype(BF16), vm[:, sl].astype(BF16), preferred_element_type=F32))
    return (jnp.concatenate(outs, axis=1),)


def _f_loss(y, t):
    e = y - t
    return e * (1.0 / D_MODEL), jnp.sum(e * e, axis=0, keepdims=True)


def _b_rms(x, dh, dres, g):
    _, vjp = jax.vjp(_rms, x, g)
    dx, dg = vjp(dh)
    return dx + dres, dg


def _b_rms_nores(x, dh, g):
    _, vjp = jax.vjp(_rms, x, g)
    dx, dg = vjp(dh)
    return dx, dg


def _b_mix_pre(za, zg, zcq, zckv, du0, dcqn, dckvn, dkr, ba, bg, gq, gkv):
    _, vjp = jax.vjp(_mix_pre, za, zg, zcq, zckv, ba, bg, gq, gkv)
    dza, dzg, dzcq, dzckv, dba, dbg, dgq, dgkv = vjp((du0, dcqn, dckvn))
    return jnp.concatenate([dza, dzg, dzcq, dzckv, dkr], axis=1), dba, dbg, dgq, dgkv


def _b_ln_silu(c1, du, bdw, lg, lb):
    _, vjp = jax.vjp(_ln_silu, c1, bdw, lg, lb)
    return vjp(du)


def _b_qk_prep(q0, kn, kr, cos, sa, sb, dq, dk, gq, gk):
    head = lambda xh, g: _qk_head(xh, g, cos, sa, sb)
    dk = dk * LN2
    dq0, dkn = [], []
    dkr = jnp.zeros_like(kr)
    dgq = jnp.zeros_like(gq)
    dgk = jnp.zeros_like(gk)
    for h, xh in enumerate(_heads(q0, LANES)):
        _, vjp = jax.vjp(head, xh, gq)
        dx, dg = vjp(dq[h, 0].T * ATT_SCALE)
        dq0.append(dx)
        dgq = dgq + dg
    for xh, ct in zip(_heads(kn, LANES), _heads(dk, LANES)):
        _, vjp = jax.vjp(head, xh + kr, gk)
        dx, dg = vjp(ct)
        dkn.append(dx)
        dkr = dkr + dx
        dgk = dgk + dg
    return jnp.concatenate(dq0, axis=1), jnp.concatenate(dkn, axis=1), dkr, dgq, dgk


def _b_mem_k(kk, dkm, g):
    dkk = []
    dg = jnp.zeros_like(g)
    for xh, ct in zip(_heads(kk, MEM_HEAD_DIM), _heads(dkm, MEM_HEAD_DIM)):
        _, vjp = jax.vjp(_rms, xh, g)
        dx, dgh = vjp(ct)
        dkk.append(dx)
        dg = dg + dgh
    return jnp.concatenate(dkk, axis=1), dg


def _b_mem_attn(dom, qm0, km, vm, g):
    dq0, dkm, dvm = [], [], []
    dg = jnp.zeros_like(g)
    for h, (xh, doh) in enumerate(zip(_heads(qm0, MEM_HEAD_DIM), _heads(dom, MEM_HEAD_DIM))):
        sl = slice(h * MEM_HEAD_DIM, (h + 1) * MEM_HEAD_DIM)
        kmh, vmh = km[:, sl], vm[:, sl].astype(BF16)
        qn, vjp = jax.vjp(_rms, xh, g)
        p = _mem_probs(qn, kmh)
        dob = doh.astype(BF16)
        dp = lax.dot_general(dob, vmh, NT_DIMS, preferred_element_type=F32)
        ds = (p * (dp - jnp.sum(dp * p, axis=-1, keepdims=True)) * MEM_SCALE).astype(BF16)
        dqn = jnp.dot(ds, kmh, preferred_element_type=F32)
        dkm.append(jnp.dot(ds.T, qn.astype(BF16), preferred_element_type=F32))
        dvm.append(jnp.dot(p.astype(BF16).T, dob, preferred_element_type=F32))
        dx, dgh = vjp(dqn)
        dq0.append(dx)
        dg = dg + dgh
    return (jnp.concatenate(dq0, axis=1), jnp.concatenate(dkm, axis=1),
            jnp.concatenate(dvm, axis=1), dg)


def _adamw(w, g, m, v):
    m = ADAM_B1 * m + (1.0 - ADAM_B1) * g
    v = ADAM_B2 * v + (1.0 - ADAM_B2) * jnp.square(g)
    m_hat = m / (1.0 - ADAM_B1 ** ADAM_STEP)
    v_hat = v / (1.0 - ADAM_B2 ** ADAM_STEP)
    delta = -ADAM_LR * (m_hat / (jnp.sqrt(v_hat) + ADAM_EPS) + ADAM_WD * w)
    return delta, m, v


def _adamw_small(parts, small, tiny, name):
    def body(*refs):
        p_ref, ins, outs = refs[0], refs[1:4 + 3 * len(TINY)], refs[4 + 3 * len(TINY):]
        me = 4 * lax.axis_index("x") + 2 * lax.axis_index("y") + lax.axis_index("c")
        groups = [(0, SMALL_ROWS)]
        groups += [(pl.multiple_of(base + me * rows, 8), rows) for base, (_, _, rows) in zip(TINY_BASE, TINY)]
        for k, (start, rows) in enumerate(groups):
            g = p_ref[0, pl.ds(start, rows), :]
            for d in range(1, N_DEV):
                g = g + p_ref[d, pl.ds(start, rows), :]
            w, m, v = [r[...] for r in ins[3 * k:3 * k + 3]]
            for ref, val in zip(outs[4 * k:4 * k + 4], (g,) + _adamw(w, g, m, v)):
                ref[...] = val

    args = list(small) + [t for grp in tiny for t in grp]
    out_shape = []
    for grp in [small] + list(tiny):
        out_shape += [jax.ShapeDtypeStruct(grp[0].shape, F32)] * 4
    return _call(body, name=name, out_shape=out_shape)(parts, *args)


def _conv_fwd(x, w, name):
    s, ch = x.shape
    kw = w.shape[0]
    halo = -(-(kw - 1) // 8) * 8
    r = CONV_ROWS
    n = s // r

    def chunk(window, wv):
        acc = jnp.zeros((r, LANES), F32)
        for k in range(kw):
            shift = kw - 1 - k
            sh = window if shift == 0 else pltpu.roll(window, shift, 0)
            acc = acc + sh[halo:halo + r] * wv[k:k + 1]
        return acc

    def body(x_ref, w_ref, y_ref):
        wv = w_ref[...]
        first = jnp.concatenate([jnp.zeros((halo, LANES), F32), x_ref[0:r]], axis=0)
        y_ref[0:r] = chunk(first, wv)

        def step(i, carry):
            base = pl.multiple_of(i * r, 8)
            y_ref[pl.ds(base, r)] = chunk(x_ref[pl.ds(base - halo, r + halo)], wv)
            return carry

        lax.fori_loop(1, n, step, 0)

    return _call(
        body, name=name, grid=(ch // LANES,),
        in_specs=[pl.BlockSpec((s, LANES), lambda c: (0, c)), pl.BlockSpec((kw, LANES), lambda c: (0, c))],
        out_specs=pl.BlockSpec((s, LANES), lambda c: (0, c)),
        out_shape=jax.ShapeDtypeStruct((s, ch), F32), compiler_params=_params("parallel"),
    )(x, w)


def _conv_bwd(dy, x, w, name):
    s, ch = x.shape
    kw = w.shape[0]
    halo = -(-(kw - 1) // 8) * 8
    r = CONV_ROWS
    n = s // r

    def dx_chunk(window, wv):
        acc = jnp.zeros((r, LANES), F32)
        for k in range(kw):
            shift = kw - 1 - k
            sh = window if shift == 0 else pltpu.roll(window, r + halo - shift, 0)
            acc = acc + sh[0:r] * wv[k:k + 1]
        return acc

    def dw_chunk(xwin, dyc, acc_ref):
        for k in range(kw):
            shift = kw - 1 - k
            sh = xwin if shift == 0 else pltpu.roll(xwin, shift, 0)
            prod = sh[halo:halo + r] * dyc
            acc_ref[k] += jnp.sum(prod.reshape(r // 8, 8, LANES), axis=0)

    def body(dy_ref, x_ref, w_ref, dx_ref, dw_ref, acc_ref):
        wv = w_ref[...]
        acc_ref[...] = jnp.zeros_like(acc_ref)
        xfirst = jnp.concatenate([jnp.zeros((halo, LANES), F32), x_ref[0:r]], axis=0)
        dw_chunk(xfirst, dy_ref[0:r], acc_ref)
        last = jnp.concatenate([dy_ref[s - r:s], jnp.zeros((halo, LANES), F32)], axis=0)
        dx_ref[s - r:s] = dx_chunk(last, wv)

        def step(i, carry):
            base = pl.multiple_of(i * r, 8)
            dw_chunk(x_ref[pl.ds(base - halo, r + halo)], dy_ref[pl.ds(base, r)], acc_ref)
            prev = pl.multiple_of((i - 1) * r, 8)
            dx_ref[pl.ds(prev, r)] = dx_chunk(dy_ref[pl.ds(prev, r + halo)], wv)
            return carry

        lax.fori_loop(1, n, step, 0)
        dw_ref[...] = jnp.sum(acc_ref[...], axis=1)

    spec = pl.BlockSpec((s, LANES), lambda c: (0, c))
    wspec = pl.BlockSpec((kw, LANES), lambda c: (0, c))
    return _call(
        body, name=name, grid=(ch // LANES,), in_specs=[spec, spec, wspec], out_specs=[spec, wspec],
        out_shape=[jax.ShapeDtypeStruct((s, ch), F32), jax.ShapeDtypeStruct((kw, ch), F32)],
        scratch_shapes=[pltpu.VMEM((kw, 8, LANES), F32)], compiler_params=_params("parallel"),
    )(dy, x, w)


HALO = 8


def _conv3(win, w, rows):
    return (pltpu.roll(win, 2, 0)[HALO:HALO + rows] * w[0:1] + pltpu.roll(win, 1, 0)[HALO:HALO + rows] * w[1:2]
            + win[HALO:HALO + rows] * w[2:3])


def _ffn_mid_fwd(up_g, up_v, w_g, w_v, b_g, b_v, name):
    s, width = up_g.shape
    tm, tc = TM, FF_PAD

    def body(pg_ref, g_ref, pv_ref, v_ref, wg_ref, wv_ref, bg_ref, bv_ref, act_ref):
        keep = (pl.program_id(0) > 0).astype(F32)
        cg = _conv3(jnp.concatenate([pg_ref[...] * keep, g_ref[...]], axis=0), wg_ref[...], tm)
        cv = _conv3(jnp.concatenate([pv_ref[...] * keep, v_ref[...]], axis=0), wv_ref[...], tm)
        act_ref[...] = _act(cg, cv, bg_ref[...], bv_ref[...]).astype(act_ref.dtype)

    tile = pl.BlockSpec((tm, tc), lambda i, c: (i, c))
    prev = pl.BlockSpec((HALO, tc), lambda i, c: (jnp.maximum(i * (tm // HALO) - 1, 0), c))
    wspec = pl.BlockSpec((FFN_CONV_WIDTH, tc), lambda i, c: (0, c))
    bspec = pl.BlockSpec((1, tc), lambda i, c: (0, c))
    return _call(
        body, name=name, grid=(s // tm, width // tc),
        in_specs=[prev, tile, prev, tile, wspec, wspec, bspec, bspec], out_specs=tile,
        out_shape=jax.ShapeDtypeStruct((s, width), BF16), compiler_params=_params("parallel", "parallel"),
    )(up_g, up_g, up_v, up_v, w_g, w_v, b_g, b_v)


def _ffn_mid_bwd(up_g, up_v, dact, w_g, w_v, b_g, b_v, name):
    s, width = up_g.shape
    tm, tc = TM, FF_PAD
    n_row = s // tm

    def body(pg_ref, g_ref, ng_ref, pv_ref, v_ref, nv_ref, d_ref, nd_ref, wg_ref, wv_ref, bg_ref, bv_ref,
             dug_ref, duv_ref, dwg_ref, dwv_ref, dbg_ref, dbv_ref):
        i = pl.program_id(1)
        first = (i > 0).astype(F32)
        last = (i < n_row - 1).astype(F32)
        ext = tm + HALO
        wg, wv = wg_ref[...], wv_ref[...]
        xg = jnp.concatenate([pg_ref[...] * first, g_ref[...], ng_ref[...]], axis=0)
        xv = jnp.concatenate([pv_ref[...] * first, v_ref[...], nv_ref[...]], axis=0)
        d_ext = jnp.concatenate([d_ref[...], nd_ref[...] * last], axis=0)
        _, vjp = jax.vjp(lambda cg_, cv_: _act(cg_, cv_, bg_ref[...], bv_ref[...]),
                         _conv3(xg, wg, ext), _conv3(xv, wv, ext))
        dcg, dcv = vjp(d_ext)
        results = []
        for x, w, dc in ((xg, wg, dcg), (xv, wv, dcv)):
            dup = (dc[:tm] * w[2:3] + pltpu.roll(dc, ext - 1, 0)[:tm] * w[1:2]
                   + pltpu.roll(dc, ext - 2, 0)[:tm] * w[0:1])
            own = dc[:tm]
            dw = jnp.concatenate([
                jnp.sum(own * pltpu.roll(x, 2, 0)[HALO:HALO + tm], axis=0, keepdims=True),
                jnp.sum(own * pltpu.roll(x, 1, 0)[HALO:HALO + tm], axis=0, keepdims=True),
                jnp.sum(own * x[HALO:HALO + tm], axis=0, keepdims=True)], axis=0)
            results.append((dup, dw, jnp.sum(own, axis=0, keepdims=True)))
        (dug, dwg, dbg), (duv, dwv, dbv) = results
        dug_ref[...] = dug
        duv_ref[...] = duv

        @pl.when(i == 0)
        def _():
            dwg_ref[...], dwv_ref[...], dbg_ref[...], dbv_ref[...] = dwg, dwv, dbg, dbv

        @pl.when(i > 0)
        def _():
            dwg_ref[...] += dwg
            dwv_ref[...] += dwv
            dbg_ref[...] += dbg
            dbv_ref[...] += dbv

    per = tm // HALO
    tile = pl.BlockSpec((tm, tc), lambda c, i: (i, c))
    prev = pl.BlockSpec((HALO, tc), lambda c, i: (jnp.maximum(i * per - 1, 0), c))
    nxt = pl.BlockSpec((HALO, tc), lambda c, i: (jnp.minimum((i + 1) * per, s // HALO - 1), c))
    wspec = pl.BlockSpec((FFN_CONV_WIDTH, tc), lambda c, i: (0, c))
    bspec = pl.BlockSpec((1, tc), lambda c, i: (0, c))
    wide = jax.ShapeDtypeStruct((s, width), F32)
    return _call(
        body, name=name, grid=(width // tc, n_row),
        in_specs=[prev, tile, nxt, prev, tile, nxt, tile, nxt, wspec, wspec, bspec, bspec],
        out_specs=[tile, tile, wspec, wspec, bspec, bspec],
        out_shape=[wide, wide] + [jax.ShapeDtypeStruct((FFN_CONV_WIDTH, width), F32)] * 2
        + [jax.ShapeDtypeStruct((1, width), F32)] * 2,
        compiler_params=_params("parallel", "arbitrary"),
    )(up_g, up_g, up_g, up_v, up_v, up_v, dact, dact, w_g, w_v, b_g, b_v)


def _chunk_mask(rows_are_queries):
    a = lax.broadcasted_iota(jnp.int32, (TQ, TQ), 0) // CHUNK
    b = lax.broadcasted_iota(jnp.int32, (TQ, TQ), 1) // CHUNK
    return (b <= a) if rows_are_queries else (a <= b)


def _head_lanes(hh):
    return slice(hh * LANES, (hh + 1) * LANES)


def _to_row(col):
    return jnp.broadcast_to(col, (TQ, LANES)).T[0:1, :]


def _flash_specs(s):
    width = HEADS_PER_STEP * LANES
    tile = pl.BlockSpec((TQ, width), lambda h, i: (i, h))
    whole = pl.BlockSpec((s, width), lambda h, i: (0, h))
    row_tile = pl.BlockSpec((HEADS_PER_STEP, 1, 1, TQ), lambda h, i: (h, i, 0, 0))
    row_whole = pl.BlockSpec((HEADS_PER_STEP, s // TQ, 1, TQ), lambda h, i: (h, 0, 0, 0))
    return tile, whole, row_tile, row_whole


def _split_refs(refs, n_in, n_out, ex):
    e_in, e_out = (len(ex.inputs), len(ex.out_shape)) if ex else (0, 0)
    a, b, c = n_in + e_in, n_in + e_in + n_out, n_in + e_in + n_out + e_out
    return refs[:n_in], refs[a:b], (refs[n_in:a], refs[b:c], refs[c:])


def _hosted(ex, ex_refs, grid, when_first):
    if ex is None:
        return
    ids = [pl.program_id(d) for d in range(len(grid))]
    cond = functools.reduce(
        lambda p, q_: p & q_, [i == (0 if when_first else g - 1) for i, g in zip(ids, grid)])

    @pl.when(cond)
    def _():
        (ex.start if when_first else ex.finish)(*ex_refs)


def _host_call(body, ex, name, grid, in_specs, out_specs, out_shape, args):
    e_in, e_out = (len(ex.inputs), len(ex.out_shape)) if ex else (0, 0)
    res = _call(
        body, name=name, grid=grid, in_specs=list(in_specs) + [ANY] * e_in,
        out_specs=list(out_specs) + [ANY] * e_out,
        out_shape=list(out_shape) + (ex.out_shape if ex else []),
        scratch_shapes=ex.scratch if ex else [],
        compiler_params=_params(*["arbitrary"] * len(grid)),
    )(*args, *(ex.inputs if ex else []))
    return res[:len(out_shape)], res[len(out_shape):]


def _flash_fwd(q, k, v, name, ex=None):
    s = q.shape[0]
    nq = s // TQ
    grid = (MLA_HEADS // HEADS_PER_STEP, nq)

    def body(*refs):
        (q_ref, k_ref, v_ref), (o_ref, lse_row_ref), ex_refs = _split_refs(refs, 3, 2, ex)
        _hosted(ex, ex_refs, grid, True)
        i = pl.program_id(1)
        qs = [q_ref[:, _head_lanes(hh)] for hh in range(HEADS_PER_STEP)]

        def step(j, carry, masked):
            base = pl.multiple_of(j * TQ, TQ)
            out = []
            for hh in range(HEADS_PER_STEP):
                m_prev, l_prev, acc = carry[hh]
                kj = k_ref[pl.ds(base, TQ), _head_lanes(hh)]
                vj = v_ref[pl.ds(base, TQ), _head_lanes(hh)]
                sc = lax.dot_general(qs[hh], kj, NT_DIMS, preferred_element_type=F32)
                if masked:
                    sc = jnp.where(_chunk_mask(True), sc, NEG)
                m_new = jnp.maximum(m_prev, jnp.max(sc, axis=-1, keepdims=True))
                alpha = jnp.exp2(m_prev - m_new)
                p = jnp.exp2(sc - m_new)
                l_new = alpha * l_prev + jnp.sum(p, axis=-1, keepdims=True)
                acc = acc * alpha + jnp.dot(p.astype(BF16), vj, preferred_element_type=F32)
                out.append((m_new, l_new, acc))
            return tuple(out)

        init = tuple((jnp.full((TQ, 1), NEG, F32), jnp.zeros((TQ, 1), F32), jnp.zeros((TQ, LANES), F32))
                     for _ in range(HEADS_PER_STEP))
        carry = lax.fori_loop(0, i, functools.partial(step, masked=False), init)
        for hh, (m_fin, l_fin, acc) in enumerate(step(i, carry, True)):
            o_ref[:, _head_lanes(hh)] = acc / l_fin
            lse_row_ref[hh, 0] = _to_row(m_fin + jnp.log2(l_fin))
        _hosted(ex, ex_refs, grid, False)

    tile, whole, row_tile, _ = _flash_specs(s)
    return _host_call(
        body, ex, name, grid, [tile, whole, whole], [tile, row_tile],
        [jax.ShapeDtypeStruct((s, MLA_HEADS * LANES), F32),
         jax.ShapeDtypeStruct((MLA_HEADS, nq, 1, TQ), F32)], (q, k, v))


def _attn_delta(do, o, name):
    s = do.shape[0]

    def body(do_ref, o_ref, d_ref):
        for h in range(MLA_HEADS):
            prod = do_ref[:, _head_lanes(h)] * o_ref[:, _head_lanes(h)]
            d_ref[h, 0] = _to_row(jnp.sum(prod, axis=-1, keepdims=True))

    tile = pl.BlockSpec((TQ, MLA_HEADS * LANES), lambda i: (i, 0))
    return _call(
        body, name=name, grid=(s // TQ,), in_specs=[tile, tile],
        out_specs=pl.BlockSpec((MLA_HEADS, 1, 1, TQ), lambda i: (0, i, 0, 0)),
        out_shape=jax.ShapeDtypeStruct((MLA_HEADS, s // TQ, 1, TQ), F32),
        compiler_params=_params("parallel"),
    )(do, o)


def _flash_bwd(q, k, v, do, lse_row, delta_row, name, ex=None):
    s = q.shape[0]
    nq = s // TQ
    grid = (MLA_HEADS // HEADS_PER_STEP, nq)

    def body(*refs):
        ins, (dqt_ref, dk_ref, dv_ref), ex_refs = _split_refs(refs, 6, 3, ex)
        q_ref, k_ref, v_ref, do_ref, lse_row_ref, delta_row_ref = ins
        _hosted(ex, ex_refs, grid, True)
        j = pl.program_id(1)

        @pl.when(j == 0)
        def _():
            dqt_ref[...] = jnp.zeros_like(dqt_ref)

        kjs = [k_ref[:, _head_lanes(hh)] for hh in range(HEADS_PER_STEP)]
        vjs = [v_ref[:, _head_lanes(hh)] for hh in range(HEADS_PER_STEP)]
        kts = [kj.T for kj in kjs]

        def step(i, carry, masked):
            base = pl.multiple_of(i * TQ, TQ)
            out = []
            for hh in range(HEADS_PER_STEP):
                dk, dv = carry[hh]
                qi = q_ref[pl.ds(base, TQ), _head_lanes(hh)]
                dob = do_ref[pl.ds(base, TQ), _head_lanes(hh)].astype(BF16)
                sc_t = lax.dot_general(kjs[hh], qi, NT_DIMS, preferred_element_type=F32)
                if masked:
                    sc_t = jnp.where(_chunk_mask(False), sc_t, NEG)
                p_t = jnp.exp2(sc_t - lse_row_ref[hh, i])
                dv = dv + jnp.dot(p_t.astype(BF16), dob, preferred_element_type=F32)
                dp_t = lax.dot_general(vjs[hh], dob, NT_DIMS, preferred_element_type=F32)
                ds_t = (p_t * (dp_t - delta_row_ref[hh, i])).astype(BF16)
                dk = dk + jnp.dot(ds_t, qi, preferred_element_type=F32)
                dqt_ref[hh, i] += jnp.dot(kts[hh], ds_t, preferred_element_type=F32)
                out.append((dk, dv))
            return tuple(out)

        zero = jnp.zeros((TQ, LANES), F32)
        carry = step(j, tuple((zero, zero) for _ in range(HEADS_PER_STEP)), True)
        carry = lax.fori_loop(j + 1, nq, functools.partial(step, masked=False), carry)
        for hh, (dk, dv) in enumerate(carry):
            dk_ref[:, _head_lanes(hh)] = dk
            dv_ref[:, _head_lanes(hh)] = dv
        _hosted(ex, ex_refs, grid, False)

    tile, whole, _, row_whole = _flash_specs(s)
    dqt_spec = pl.BlockSpec((HEADS_PER_STEP, nq, LANES, TQ), lambda h, j: (h, 0, 0, 0))
    wide = jax.ShapeDtypeStruct((s, MLA_HEADS * LANES), F32)
    return _host_call(
        body, ex, name, grid, [whole, tile, tile, whole, row_whole, row_whole], [dqt_spec, tile, tile],
        [jax.ShapeDtypeStruct((MLA_HEADS, nq, LANES, TQ), F32), wide, wide],
        (q, k, v, do, lse_row, delta_row))


def _side_by_side(g):
    return g.transpose(1, 0, 2).reshape(g.shape[1], N_DEV * g.shape[2])


def _col_shards(g):
    return g.reshape(g.shape[0], N_DEV, g.shape[1] // N_DEV).transpose(1, 0, 2)


def _pad_last(v, to):
    return jnp.pad(v, [(0, 0)] * (v.ndim - 1) + [(0, to - v.shape[-1])])


def _tiny_rows(v, rows):
    flat = v.reshape(v.shape[:-2] + (-1,))
    return _pad_last(flat, rows * LANES).reshape(v.shape[:-2] + (rows, LANES))


def _pack_small(vals):
    parts = []
    for (n, size), pad in zip(SMALL, SMALL_PAD):
        parts.append(jnp.pad(vals[n].reshape(-1), (0, pad - size)))
    flat = jnp.concatenate(parts)
    return jnp.pad(flat, (0, SMALL_ROWS * LANES - flat.shape[0])).reshape(SMALL_ROWS, LANES)


def _unpack_small(packed):
    flat = packed.reshape(-1)
    out, off = {}, 0
    for (n, size), pad in zip(SMALL, SMALL_PAD):
        out[n] = flat[off:off + size].reshape(1, size)
        off += pad
    return out


def _pad_heads(w, per_head, axis):
    shape = list(w.shape)
    shape[axis:axis + 1] = [MLA_HEADS, per_head]
    w = w.reshape(shape)
    pad = [(0, 0)] * len(shape)
    pad[axis + 1] = (0, LANES - per_head)
    w = jnp.pad(w, pad)
    shape[axis:axis + 2] = [MLA_HEADS * LANES]
    return w.reshape(shape)


def _unpad_heads(w, per_head, axis):
    shape = list(w.shape)
    shape[axis:axis + 1] = [MLA_HEADS, LANES]
    w = w.reshape(shape)
    w = lax.slice_in_dim(w, 0, per_head, axis=axis + 1)
    shape[axis:axis + 2] = [MLA_HEADS * per_head]
    return w.reshape(shape)


def _row(v, pad_to=None):
    v = v.reshape(1, -1)
    if pad_to is not None:
        v = jnp.pad(v, ((0, 0), (0, pad_to - v.shape[1])))
    return v


def kernel(x, mem, positions, mix_norm_g, w_in, b_conv_in, w_conv_dw, b_conv_dw, conv_ln_g, conv_ln_b, q_lat_norm_g, w_uq, kv_lat_norm_g, w_ukv, q_norm_g, k_norm_g, w_out, mem_norm_x_g, mem_norm_m_g, w_mem_q, w_mem_kv, mem_q_norm_g, mem_k_norm_g, w_mem_o, ffn_norm_g, w_up, w_ffn_dw, b_ffn_dw, w_down, loss_target, m_mix_norm_g, m_w_in, m_b_conv_in, m_w_conv_dw, m_b_conv_dw, m_conv_ln_g, m_conv_ln_b, m_q_lat_norm_g, m_w_uq, m_kv_lat_norm_g, m_w_ukv, m_q_norm_g, m_k_norm_g, m_w_out, m_mem_norm_x_g, m_mem_norm_m_g, m_w_mem_q, m_w_mem_kv, m_mem_q_norm_g, m_mem_k_norm_g, m_w_mem_o, m_ffn_norm_g, m_w_up, m_w_ffn_dw, m_b_ffn_dw, m_w_down, v_mix_norm_g, v_w_in, v_b_conv_in, v_w_conv_dw, v_b_conv_dw, v_conv_ln_g, v_conv_ln_b, v_q_lat_norm_g, v_w_uq, v_kv_lat_norm_g, v_w_ukv, v_q_norm_g, v_k_norm_g, v_w_out, v_mem_norm_x_g, v_mem_norm_m_g, v_w_mem_q, v_w_mem_kv, v_mem_q_norm_g, v_mem_k_norm_g, v_w_mem_o, v_ffn_norm_g, v_w_up, v_w_ffn_dw, v_b_ffn_dw, v_w_down):
    a = dict(locals())
    seq = x.shape[1]
    xs = x.reshape(seq, D_MODEL)
    mems = mem.reshape(-1, D_MODEL)
    target = loss_target.reshape(seq, D_MODEL)

    tiny = [n for n, _, _ in TINY]
    shard = lambda n: a[n][0] if n in tiny else a[n][0].astype(BF16)
    pos = jnp.stack([2 * lax.axis_index("x") + lax.axis_index("y"), lax.axis_index("c")]).astype(jnp.int32)
    ag_first = ["w_in", "w_uq", "w_ukv", "w_conv_dw"]
    ag_later = [n for n in [b for b, _ in BIG] + tiny if n not in ag_first]
    wg = dict(zip(ag_first, _all_gather([shard(n) for n in ag_first], "ag_weights_first")))
    wi = _side_by_side(wg["w_in"])
    s3 = 2 * CONV_CH + MLA_Q_RANK + MLA_KV_RANK
    w_in_p = jnp.concatenate([
        wi[:, :s3], jnp.zeros((D_MODEL, MLA_NOPE), BF16), wi[:, s3:],
        jnp.zeros((D_MODEL, LANES - MLA_QK), BF16)], axis=1)
    w_uq_p = _side_by_side(_pad_last(wg["w_uq"], LANES))
    w_uk_p = _side_by_side(_pad_last(wg["w_ukv"][:, :, :MLA_NOPE], LANES))
    w_uv_p = _side_by_side(_pad_last(wg["w_ukv"][:, :, MLA_NOPE:], LANES))
    w_cdw = _side_by_side(wg["w_conv_dw"])

    g_mix, g_qlat, g_kvlat = _row(mix_norm_g), _row(q_lat_norm_g), _row(kv_lat_norm_g)
    b_in = _row(b_conv_in)
    b_in_a, b_in_g = b_in[:, :CONV_CH], b_in[:, CONV_CH:]
    b_cdw, ln_g, ln_b = _row(b_conv_dw), _row(conv_ln_g), _row(conv_ln_b)
    g_q, g_k = _row(q_norm_g, LANES), _row(k_norm_g, LANES)
    g_memx, g_memm = _row(mem_norm_x_g), _row(mem_norm_m_g)
    g_mq, g_mk, g_ffn = _row(mem_q_norm_g), _row(mem_k_norm_g), _row(ffn_norm_g)
    b_f = _pad_last(b_ffn_dw.reshape(N_DEV, FF_SHARD), FF_PAD)
    b_f_g, b_f_v = b_f[:4].reshape(1, D_FF_PAD), b_f[4:].reshape(1, D_FF_PAD)

    freq = ROPE_THETA ** (-jnp.arange(0, MLA_ROPE, 2, dtype=F32) / MLA_ROPE)
    inv_freq = jnp.concatenate([jnp.zeros((MLA_NOPE,), F32), freq, freq,
                                jnp.zeros((LANES - MLA_QK,), F32)]).reshape(1, LANES)
    cos, sin_a, sin_b = _rowwise(_f_rope_tab, [positions.reshape(seq, 1)], [inv_freq],
                                 [(LANES, F32)] * 3, [], name="rope_tables")

    (h1,) = _rowwise(_f_rms, [xs], [g_mix], [(D_MODEL, BF16)], [], name="rms_mix")
    z = _mm(h1, w_in_p, name="mm_in")
    z_rows = [(z, CONV_CH, 0, 0), (z, CONV_CH, 1, 0), (z, MLA_Q_RANK, 4, 0), (z, MLA_KV_RANK, 10, 0)]
    z_kr = (z, LANES, 11, 0)
    u0, cqn, ckvn = _rowwise(
        _mix_pre, z_rows, [b_in_a, b_in_g, g_qlat, g_kvlat],
        [(CONV_CH, F32), (MLA_Q_RANK, BF16), (MLA_KV_RANK, BF16)], [], name="mix_pre")
    c1 = _conv_fwd(u0, w_cdw, "conv31_fwd")
    (u,) = _rowwise(lambda c, b, g, bb: (_ln_silu(c, b, g, bb),), [c1], [b_cdw, ln_g, ln_b],
                    [(CONV_CH, BF16)], [], name="ln_silu")
    q0 = _mm(cqn, w_uq_p, name="mm_uq")
    kn0 = _mm(ckvn, w_uk_p, name="mm_uk")
    v0 = _mm(ckvn, w_uv_p, out_dtype=BF16, name="mm_uv")
    qk_rows = [q0, kn0, z_kr, cos, sin_a, sin_b]
    qh, kh = _rowwise(_f_qk_prep, qk_rows, [g_q, g_k],
                      [(MLA_HEADS * LANES, BF16)] * 2, [], name="qk_prep")
    (attn, lse_row), later = _flash_fwd(
        qh, kh, v0, "flash_fwd", _plan_all_gather([shard(n) for n in ag_later]))
    wg.update(zip(ag_later, later))
    w_out = wg["w_out"].reshape(D_MODEL, D_MODEL)
    w_out_u = w_out[:CONV_CH]
    w_out_a = _pad_heads(w_out[CONV_CH:], MLA_V, 0)
    w_mq, w_mo = wg["w_mem_q"].reshape(D_MODEL, D_MODEL), wg["w_mem_o"].reshape(D_MODEL, D_MODEL)
    w_mkv = _side_by_side(wg["w_mem_kv"])
    w_up_p = _pad_last(wg["w_up"], FF_PAD)
    w_dn = jnp.pad(wg["w_down"].reshape(4, FF_SHARD, D_MODEL),
                   ((0, 0), (0, FF_PAD - FF_SHARD), (0, 0))).reshape(D_FF_PAD, D_MODEL)
    w_fdw = _pad_last(wg["w_ffn_dw"], FF_PAD)
    w_fdw_g = w_fdw[:4].transpose(1, 0, 2).reshape(FFN_CONV_WIDTH, D_FF_PAD)
    w_fdw_v = w_fdw[4:].transpose(1, 0, 2).reshape(FFN_CONV_WIDTH, D_FF_PAD)
    x1 = _mm(u, w_out_u, add=xs, name="mm_out_u")
    x1 = _mm(attn, w_out_a, add=x1, name="mm_out_a")

    (hq,) = _rowwise(_f_rms, [x1], [g_memx], [(D_MODEL, BF16)], [], name="rms_memx")
    (hm,) = _rowwise(_f_rms, [mems], [g_memm], [(D_MODEL, BF16)], [], name="rms_memm", tm=mems.shape[0])
    qm0 = _mm(hq, w_mq, name="mm_memq")
    kvm0 = _mm(hm, w_mkv, name="mm_memkv", tm=mems.shape[0])
    (km,) = _rowwise(_f_mem_k, [(kvm0, D_MODEL, 0, 0)], [g_mk], [(D_MODEL, BF16)], [],
                     name="mem_k", tm=mems.shape[0])
    vm = kvm0[:, D_MODEL:]
    (om,) = _rowwise(_f_mem_attn, [qm0], [km, vm, g_mq], [(D_MODEL, BF16)], [], name="mem_attn")
    x2 = _mm(om, w_mo, add=x1, name="mm_memo")

    (h3,) = _rowwise(_f_rms, [x2], [g_ffn], [(D_MODEL, BF16)], [], name="rms_ffn")
    up_g = _mm(h3, w_up_p, b_shards=(0, 4), name="mm_up_g")
    up_v = _mm(h3, w_up_p, b_shards=(4, 4), name="mm_up_v")
    act = _ffn_mid_fwd(up_g, up_v, w_fdw_g, w_fdw_v, b_f_g, b_f_v, "ffn_mid")
    y = _mm(act, w_dn, add=x2, name="mm_down")
    dy, sq = _rowwise(_f_loss, [y, target], [], [(D_MODEL, F32)], [(1, D_MODEL)], name="loss")
    loss = lax.psum(0.5 * jnp.sum(sq) / D_MODEL, ("x", "y", "c"))

    gw, gs, gt = {}, {}, {}
    gw_dn = _mm_tn(act, dy, name="tn_down").reshape(4, FF_PAD, D_MODEL)
    gw["w_down"] = gw_dn[:, :FF_SHARD].reshape(N_DEV, FF_SHARD // 2, D_MODEL)
    dact = _mm(dy, w_dn, trans_b=True, name="mm_down_t")
    dup_g, dup_v, dwf_g, dwf_v, db_g, db_v = _ffn_mid_bwd(
        up_g, up_v, dact, w_fdw_g, w_fdw_v, b_f_g, b_f_v, "ffn_mid_bwd")
    db_f = jnp.concatenate([db_g.reshape(4, FF_PAD), db_v.reshape(4, FF_PAD)], axis=0)
    gs["b_ffn_dw"] = db_f[:, :FF_SHARD].reshape(1, 2 * D_FF)
    dwf = jnp.concatenate([dwf_g.reshape(FFN_CONV_WIDTH, 4, FF_PAD), dwf_v.reshape(FFN_CONV_WIDTH, 4, FF_PAD)], axis=1)
    gt["w_ffn_dw"] = dwf[:, :, :FF_SHARD].transpose(1, 0, 2)
    gw_up = jnp.concatenate([_mm_tn(h3, dup_g, shard_cols=FF_PAD, name="tn_up_g"),
                             _mm_tn(h3, dup_v, shard_cols=FF_PAD, name="tn_up_v")], axis=0)
    gw["w_up"] = gw_up[:, :, :FF_SHARD]
    dh3 = _mm(dup_g, w_up_p, trans_b=True, b_shards=(0, 4), name="mm_up_g_t")
    dh3 = _mm(dup_v, w_up_p, trans_b=True, b_shards=(4, 4), add=dh3, name="mm_up_v_t")
    dx2, gs["ffn_norm_g"] = _rowwise(_b_rms, [x2, dh3, dy], [g_ffn], [(D_MODEL, F32)], [(1, D_MODEL)],
                                     name="rms_ffn_bwd")

    gw["w_mem_o"] = _mm_tn(om, dx2, name="tn_memo").reshape(N_DEV, -1, D_MODEL)
    dom = _mm(dx2, w_mo, trans_b=True, name="mm_memo_t")
    n_mem = mems.shape[0]
    dqm0, dkm, dvm, gs["mem_q_norm_g"] = _rowwise(
        _b_mem_attn, [dom, qm0], [km, vm, g_mq], [(D_MODEL, F32)],
        [(n_mem, D_MODEL), (n_mem, D_MODEL), (1, MEM_HEAD_DIM)], name="mem_attn_bwd")
    gw["w_mem_q"] = _mm_tn(hq, dqm0, name="tn_memq").reshape(N_DEV, -1, D_MODEL)
    dhq = _mm(dqm0, w_mq, trans_b=True, name="mm_memq_t")
    dx1, gs["mem_norm_x_g"] = _rowwise(_b_rms, [x1, dhq, dx2], [g_memx], [(D_MODEL, F32)],
                                       [(1, D_MODEL)], name="rms_memx_bwd")
    dkk, gs["mem_k_norm_g"] = _rowwise(_b_mem_k, [(kvm0, D_MODEL, 0, 0), dkm], [g_mk],
                                       [(D_MODEL, F32)], [(1, MEM_HEAD_DIM)], name="mem_k_bwd", tm=n_mem)
    dkvm0 = jnp.concatenate([dkk, dvm], axis=1)
    gw["w_mem_kv"] = _col_shards(_mm_tn(hm, dkvm0, name="tn_memkv", ts=n_mem))
    dhm = _mm(dkvm0, w_mkv, trans_b=True, name="mm_memkv_t", tm=n_mem)
    _, gs["mem_norm_m_g"] = _rowwise(_b_rms_nores, [mems, dhm], [g_memm], [(D_MODEL, F32)],
                                     [(1, D_MODEL)], name="rms_memm_bwd", tm=n_mem)

    gw_out_u = _mm_tn(u, dx1, name="tn_out_u")
    gw_out_a = _mm_tn(attn, dx1, name="tn_out_a")
    gw["w_out"] = jnp.concatenate([gw_out_u, _unpad_heads(gw_out_a, MLA_V, 0)], axis=0).reshape(N_DEV, -1, D_MODEL)
    du = _mm(dx1, w_out_u, trans_b=True, name="mm_out_u_t")
    dattn = _mm(dx1, w_out_a, trans_b=True, name="mm_out_a_t")
    dc1, gs["b_conv_dw"], gs["conv_ln_g"], gs["conv_ln_b"] = _rowwise(
        _b_ln_silu, [c1, du], [b_cdw, ln_g, ln_b], [(CONV_CH, F32)], [(1, CONV_CH)] * 3, name="ln_silu_bwd")
    du0, g_cdw = _conv_bwd(dc1, u0, w_cdw, "conv31_bwd")
    gt["w_conv_dw"] = _col_shards(g_cdw)
    rs_first = ["w_up", "w_down", "w_mem_o", "w_mem_q", "w_mem_kv", "w_out"]
    grads = [gw[n] for n in rs_first]
    sums_first = _rs_add(grads, _swap_sibling(grads, "rs_sibling_first"), pos, "rs_add_first")
    delta_row = _attn_delta(dattn, attn, "attn_delta")
    (dqt, dkh, dv0), recvs_first = _flash_bwd(
        qh, kh, v0, dattn, lse_row, delta_row, "flash_bwd", _plan_swap_chips(sums_first))
    dqt_row = (dqt, pl.BlockSpec((MLA_HEADS, 1, LANES, TQ), lambda i: (0, i, 0, 0)))
    dq0, dkn0, dkr, dgq, dgk = _rowwise(
        _b_qk_prep, qk_rows + [dqt_row, dkh], [g_q, g_k],
        [(MLA_HEADS * LANES, F32)] * 2 + [(LANES, F32)], [(1, LANES)] * 2, name="qk_prep_bwd")
    gs["q_norm_g"], gs["k_norm_g"] = dgq[:, :MLA_QK], dgk[:, :MLA_QK]
    gw["w_uq"] = _col_shards(_mm_tn(cqn, dq0, name="tn_uq"))[:, :, :MLA_QK]
    g_uk = _col_shards(_mm_tn(ckvn, dkn0, name="tn_uk"))[:, :, :MLA_NOPE]
    g_uv = _col_shards(_mm_tn(ckvn, dv0, name="tn_uv"))[:, :, :MLA_V]
    gw["w_ukv"] = jnp.concatenate([g_uk, g_uv], axis=2)
    dcqn = _mm(dq0, w_uq_p, trans_b=True, name="mm_uq_t")
    dckvn = _mm(dkn0, w_uk_p, trans_b=True, name="mm_uk_t")
    dckvn = _mm(dv0, w_uv_p, trans_b=True, add=dckvn, name="mm_uv_t")
    dz, dba, dbg, gs["q_lat_norm_g"], gs["kv_lat_norm_g"] = _rowwise(
        _b_mix_pre, z_rows + [du0, dcqn, dckvn, dkr], [b_in_a, b_in_g, g_qlat, g_kvlat],
        [(IN_PAD, F32)], [(1, CONV_CH)] * 2 + [(1, MLA_Q_RANK), (1, MLA_KV_RANK)], name="mix_pre_bwd")
    gs["b_conv_in"] = jnp.concatenate([dba, dbg], axis=1)
    gw_in = _mm_tn(h1, dz, name="tn_in")
    gw["w_in"] = _col_shards(jnp.concatenate([gw_in[:, :s3], gw_in[:, s3 + MLA_NOPE:s3 + MLA_QK]], axis=1))
    dh1 = _mm(dz, w_in_p, trans_b=True, name="mm_in_t")
    dx, gs["mix_norm_g"] = _rowwise(_b_rms, [xs, dh1, dx1], [g_mix], [(D_MODEL, F32)], [(1, D_MODEL)],
                                    name="rms_mix_bwd")

    rs_last = [n for n, _ in BIG if n not in rs_first]
    grads = [gw[n] for n in rs_last]
    sums_last = _rs_add(grads, _swap_sibling(grads, "rs_sibling_last"), pos, "rs_add_last")
    recvs_last = _run_exchange(_plan_swap_chips(sums_last), "rs_chips_last")
    big = rs_first + rs_last
    flat = _adamw_big(list(sums_first) + list(sums_last), list(recvs_first) + list(recvs_last),
                      [a[n] for n in big], [a["m_" + n] for n in big], [a["v_" + n] for n in big],
                      pos, "adamw_big")
    res = [{n: flat[4 * i + k] for i, n in enumerate(big)} for k in range(4)]

    part = jnp.concatenate(
        [_pack_small(gs)] + [_tiny_rows(gt[n], rows).reshape(N_DEV * rows, LANES) for n, _, rows in TINY], axis=0)
    (parts,) = _all_gather([part], "ag_small_grads")
    small_in = [_pack_small({n: a[p + n] for n, _ in SMALL}) for p in ("", "m_", "v_")]
    tiny_in = [[_tiny_rows(a[p + n][0], rows) for p in ("", "m_", "v_")] for n, _, rows in TINY]
    flat = _adamw_small(parts, small_in, tiny_in, "adamw_small")
    for k in range(4):
        res[k].update(_unpack_small(flat[k]))
        for i, (n, shape, _) in enumerate(TINY):
            res[k][n] = flat[4 * (i + 1) + k].reshape(-1)[:math.prod(shape)].reshape((1,) + shape)

    return (loss, dx.reshape(1, seq, D_MODEL), *[res[k][n] for k in range(4) for n in WEIGHTS])
```

```python
import functools
import math

import jax
import jax.numpy as jnp
from jax import lax
from jax.experimental import pallas as pl
from jax.experimental.pallas import tpu as pltpu

F32 = jnp.float32
BF16 = jnp.bfloat16
EPS = 1e-6
LANES = 128
N_DEV = 8
D_MODEL = 1024
CONV_CH = 512
CONV_WIDTH = 31
MLA_HEADS = 8
MLA_NOPE = 64
MLA_ROPE = 32
MLA_V = 64
MLA_QK = MLA_NOPE + MLA_ROPE
MLA_Q_RANK = 256
MLA_KV_RANK = 128
ROPE_THETA = 10000.0
IN_COLS = 2 * CONV_CH + MLA_Q_RANK + MLA_KV_RANK + MLA_ROPE
IN_PAD = 2 * CONV_CH + MLA_Q_RANK + MLA_KV_RANK + LANES
MEM_HEADS = 4
MEM_HEAD_DIM = 256
D_FF = 2816
FFN_CONV_WIDTH = 3
CHUNK = 64
ATT_SCALE = 1.0 / math.sqrt(MLA_QK)
LN2 = math.log(2.0)
Q_SCALE = ATT_SCALE / LN2
MEM_SCALE = 1.0 / math.sqrt(MEM_HEAD_DIM)
ADAM_LR, ADAM_B1, ADAM_B2, ADAM_EPS, ADAM_WD, ADAM_STEP = 0.001, 0.9, 0.999, 1e-08, 0.01, 10

TM = 512
MM_ROWS = 1024
TQ = 512
HEADS_PER_STEP = 2
FWD_HEADS = 4
CONV_ROWS = 256
NEG = -1e30
VMEM_LIMIT = 56 * 1024 * 1024

MESH = pl.DeviceIdType.MESH
ANY = pl.BlockSpec(memory_space=pl.ANY)
NT_DIMS = (((1,), (1,)), ((), ()))

BIG = [
    ("w_in", (1024, 180)), ("w_uq", (256, 96)), ("w_ukv", (128, 128)), ("w_out", (128, 1024)),
    ("w_mem_q", (128, 1024)), ("w_mem_kv", (1024, 256)), ("w_mem_o", (128, 1024)),
    ("w_up", (1024, 704)), ("w_down", (352, 1024)),
]
TINY = [("w_conv_dw", (31, 64), 16), ("w_ffn_dw", (3, 704), 24)]
ROW_STEPS = 4
FF_SHARD = D_FF // 4
FF_PAD = 768
D_FF_PAD = 4 * FF_PAD
SMALL = [
    ("mix_norm_g", 1024), ("b_conv_in", 1024), ("b_conv_dw", 512), ("conv_ln_g", 512),
    ("conv_ln_b", 512), ("q_lat_norm_g", 256), ("kv_lat_norm_g", 128), ("q_norm_g", 96),
    ("k_norm_g", 96), ("mem_norm_x_g", 1024), ("mem_norm_m_g", 1024), ("mem_q_norm_g", 256),
    ("mem_k_norm_g", 256), ("ffn_norm_g", 1024), ("b_ffn_dw", 5632),
]
WEIGHTS = [
    "mix_norm_g", "w_in", "b_conv_in", "w_conv_dw", "b_conv_dw", "conv_ln_g", "conv_ln_b",
    "q_lat_norm_g", "w_uq", "kv_lat_norm_g", "w_ukv", "q_norm_g", "k_norm_g", "w_out",
    "mem_norm_x_g", "mem_norm_m_g", "w_mem_q", "w_mem_kv", "mem_q_norm_g", "mem_k_norm_g",
    "w_mem_o", "ffn_norm_g", "w_up", "w_ffn_dw", "b_ffn_dw", "w_down",
]


SMALL_PAD = [(-(-n // LANES)) * LANES for _, n in SMALL]
SMALL_ROWS = -(-sum(SMALL_PAD) // (8 * LANES)) * 8
TINY_BASE = [SMALL_ROWS + N_DEV * sum(r for _, _, r in TINY[:i]) for i in range(len(TINY))]
PART_ROWS = SMALL_ROWS + N_DEV * sum(r for _, _, r in TINY)


def _call(body, **kw):
    return pl.pallas_call(body, **kw)


def _params(*sem):
    return pltpu.CompilerParams(dimension_semantics=sem, vmem_limit_bytes=VMEM_LIMIT)


class _Exchange:
    def __init__(self, inputs, out_shape, scratch, start, finish):
        self.inputs, self.out_shape, self.scratch = list(inputs), list(out_shape), list(scratch)
        self.start, self.finish = start, finish


def _run_exchange(ex, name):
    n_in, n_out = len(ex.inputs), len(ex.out_shape)

    def body(*refs):
        parts = refs[:n_in], refs[n_in:n_in + n_out], refs[n_in + n_out:]
        ex.start(*parts)
        ex.finish(*parts)

    return _call(body, name=name, out_shape=ex.out_shape, in_specs=[ANY] * n_in,
                 out_specs=[ANY] * n_out, scratch_shapes=ex.scratch)(*ex.inputs)


def _plan_all_gather(xs):
    n = len(xs)

    def copies(x_refs, out_refs, sems):
        send_sems, recv_sems, local_sems = sems
        x, y, c = lax.axis_index("x"), lax.axis_index("y"), lax.axis_index("c")
        me, sibling = (x, y, c), (x, y, 1 - c)
        chips = [(1 - x, y), (x, 1 - y), (1 - x, 1 - y)]

        def slot(o, px, py, pc):
            return out_refs[o].at[4 * px + 2 * py + pc]

        def copy(o, k, block, to, src=None):
            return pltpu.make_async_remote_copy(
                src_ref=slot(o, *block) if src is None else src, dst_ref=slot(o, *block),
                send_sem=send_sems.at[o, k], recv_sem=recv_sems.at[o, k],
                device_id=to, device_id_type=MESH)

        mine = [pltpu.make_async_copy(x_refs[o], slot(o, *me), local_sems.at[o]) for o in range(n)]
        first = [copy(o, 0, me, sibling, src=x_refs[o]) for o in range(n)]
        first += [copy(o, 1 + j, me, (*chip, c), src=x_refs[o])
                  for j, chip in enumerate(chips) for o in range(n)]
        return me, sibling, chips, copy, mine, first

    def start(x_refs, out_refs, sems):
        _, _, _, _, mine, first = copies(x_refs, out_refs, sems)
        for cp in mine + first:
            cp.start()

    def finish(x_refs, out_refs, sems):
        me, sibling, chips, copy, mine, first = copies(x_refs, out_refs, sems)
        c = me[2]
        passed = []
        for j, chip in enumerate(chips):
            for o in range(n):
                copy(o, 1 + j, (*chip, c), me).wait_recv()
                passed.append(copy(o, 4 + j, (*chip, c), sibling))
                passed[-1].start()
        for o in range(n):
            copy(o, 0, sibling, me).wait_recv()
        for j, chip in enumerate(chips):
            for o in range(n):
                copy(o, 4 + j, (*chip, 1 - c), me).wait_recv()
        for cp in first + passed:
            cp.wait_send()
        for cp in mine:
            cp.wait()

    return _Exchange(
        xs, [jax.ShapeDtypeStruct((N_DEV,) + v.shape, v.dtype) for v in xs],
        [pltpu.SemaphoreType.DMA((n, 7)), pltpu.SemaphoreType.DMA((n, 7)), pltpu.SemaphoreType.DMA((n,))],
        start, finish)


def _all_gather(xs, name):
    return _run_exchange(_plan_all_gather(xs), name)


def _swap_sibling(grads, name):
    n = len(grads)

    def body(*refs):
        g_refs, got_refs = refs[:n], refs[n:2 * n]
        send_sems, recv_sems = refs[2 * n:]
        x, y, c = lax.axis_index("x"), lax.axis_index("y"), lax.axis_index("c")
        copies = [
            pltpu.make_async_remote_copy(
                src_ref=g_refs[o].at[2 * chip + 1 - c], dst_ref=got_refs[o].at[chip],
                send_sem=send_sems.at[o, chip], recv_sem=recv_sems.at[o, chip],
                device_id=(x, y, 1 - c), device_id_type=MESH)
            for o in range(n) for chip in range(4)]
        for cp in copies:
            cp.start()
        for cp in copies:
            cp.wait()

    return _call(
        body, name=name,
        out_shape=[jax.ShapeDtypeStruct((4,) + g.shape[1:], g.dtype) for g in grads],
        in_specs=[ANY] * n, out_specs=[ANY] * n,
        scratch_shapes=[pltpu.SemaphoreType.DMA((n, 4)), pltpu.SemaphoreType.DMA((n, 4))],
    )(*grads)


def _plan_swap_chips(sums):
    n = len(sums)

    def copies(a_refs, r_refs, sems):
        send_sems, recv_sems = sems
        x, y, c = lax.axis_index("x"), lax.axis_index("y"), lax.axis_index("c")
        peers = [(x, 1 - y), (1 - x, y), (1 - x, 1 - y)]
        return [
            pltpu.make_async_remote_copy(
                src_ref=a_refs[o].at[2 * px + py], dst_ref=r_refs[o].at[k],
                send_sem=send_sems.at[o, k], recv_sem=recv_sems.at[o, k],
                device_id=(px, py, c), device_id_type=MESH)
            for k, (px, py) in enumerate(peers) for o in range(n)]

    def start(a_refs, r_refs, sems):
        for cp in copies(a_refs, r_refs, sems):
            cp.start()

    def finish(a_refs, r_refs, sems):
        for cp in copies(a_refs, r_refs, sems):
            cp.wait()

    return _Exchange(
        sums, [jax.ShapeDtypeStruct((3,) + a.shape[1:], a.dtype) for a in sums],
        [pltpu.SemaphoreType.DMA((n, 3)), pltpu.SemaphoreType.DMA((n, 3))], start, finish)


def _shard_block(shape):
    return (None, shape[-2] // ROW_STEPS, shape[-1])


def _rs_add(grads, gots, pos, name):
    n = len(grads)

    def body(pos_ref, *refs):
        for g_ref, t_ref, o_ref in zip(refs[:n], refs[n:2 * n], refs[2 * n:]):
            o_ref[...] = g_ref[...] + t_ref[...]

    in_specs = [pl.BlockSpec(_shard_block(g.shape), lambda a, t, pos: (2 * a + pos[1], t, 0)) for g in grads]
    in_specs += [pl.BlockSpec(_shard_block(g.shape), lambda a, t, pos: (a, t, 0)) for g in gots]
    return _call(
        body, name=name,
        grid_spec=pltpu.PrefetchScalarGridSpec(
            num_scalar_prefetch=1, grid=(4, ROW_STEPS), in_specs=in_specs,
            out_specs=[pl.BlockSpec(_shard_block(g.shape), lambda a, t, pos: (a, t, 0)) for g in gots]),
        out_shape=[jax.ShapeDtypeStruct(g.shape, g.dtype) for g in gots],
        compiler_params=_params("arbitrary", "arbitrary"),
    )(pos, *grads, *gots)


def _adamw_big(sums, recvs, ws, ms, vs, pos, name):
    n = len(sums)

    def body(pos_ref, *refs):
        ins, outs = refs[:7 * n], refs[7 * n:]
        for o in range(n):
            own, r1, r2, r3, w, m, v = [r[...] for r in ins[7 * o:7 * o + 7]]
            g = ((own + r1) + r2) + r3
            for ref, val in zip(outs[4 * o:4 * o + 4], (g,) + _adamw(w, g, m, v)):
                ref[...] = val

    in_specs, args, out_specs, out_shape = [], [], [], []
    for s_, r_, w_, m_, v_ in zip(sums, recvs, ws, ms, vs):
        blk = _shard_block(w_.shape)
        in_specs.append(pl.BlockSpec(blk, lambda t, pos: (pos[0], t, 0)))
        in_specs += [pl.BlockSpec(blk, lambda t, pos, k=k: (k, t, 0)) for k in range(3)]
        in_specs += [pl.BlockSpec(blk, lambda t, pos: (0, t, 0))] * 3
        args += [s_, r_, r_, r_, w_, m_, v_]
        out_specs += [pl.BlockSpec(blk, lambda t, pos: (0, t, 0))] * 4
        out_shape += [jax.ShapeDtypeStruct(w_.shape, F32)] * 4
    return _call(
        body, name=name,
        grid_spec=pltpu.PrefetchScalarGridSpec(
            num_scalar_prefetch=1, grid=(ROW_STEPS,), in_specs=in_specs, out_specs=out_specs),
        out_shape=out_shape, compiler_params=_params("arbitrary"),
    )(pos, *args)


def _tile(n, prefs):
    for t in prefs:
        if n % t == 0:
            return t
    return n


def _mm(a, b, *, name, trans_b=False, add=None, out_dtype=F32, tm=MM_ROWS, b_shards=None):
    m, k = a.shape
    if b_shards is None:
        n = b.shape[0] if trans_b else b.shape[1]
        tn = _tile(n, (1536, 1408, 1024, 768, 512, 256, 128))
        tk = _tile(k, (1408, 1024, 768, 512, 256, 128))
    elif trans_b:
        n, tn, tk = b.shape[1], b.shape[1], b.shape[2]
    else:
        n, tn, tk = b_shards[1] * b.shape[2], b.shape[2], _tile(k, (1024, 512))
    nk = k // tk
    has_add = add is not None

    def body(*refs):
        a_ref, b_ref = refs[0], refs[1]
        add_ref = refs[2] if has_add else None
        o_ref = refs[2 + has_add]
        av = a_ref[...].astype(BF16)
        bv = b_ref[...].astype(BF16)
        if trans_b:
            part = lax.dot_general(av, bv, NT_DIMS, preferred_element_type=F32)
        else:
            part = jnp.dot(av, bv, preferred_element_type=F32)

        def finish(acc):
            if has_add:
                acc = acc + add_ref[...].astype(F32)
            o_ref[...] = acc.astype(o_ref.dtype)

        if nk == 1:
            finish(part)
        else:
            acc_ref = refs[3 + has_add]
            kk = pl.program_id(2)

            @pl.when(kk == 0)
            def _():
                acc_ref[...] = part

            @pl.when(kk > 0)
            def _():
                acc_ref[...] += part

            @pl.when(kk == nk - 1)
            def _():
                finish(acc_ref[...])

    in_specs = [pl.BlockSpec((tm, tk), lambda i, j, kk: (i, kk))]
    if b_shards is not None and trans_b:
        in_specs.append(pl.BlockSpec((None, tn, tk), lambda i, j, kk: (b_shards[0] + kk, j, 0)))
    elif b_shards is not None:
        in_specs.append(pl.BlockSpec((None, tk, tn), lambda i, j, kk: (b_shards[0] + j, kk, 0)))
    elif trans_b:
        in_specs.append(pl.BlockSpec((tn, tk), lambda i, j, kk: (j, kk)))
    else:
        in_specs.append(pl.BlockSpec((tk, tn), lambda i, j, kk: (kk, j)))
    args = [a, b]
    if has_add:
        in_specs.append(pl.BlockSpec((tm, tn), lambda i, j, kk: (i, j)))
        args.append(add)
    return _call(
        body, name=name, grid=(m // tm, n // tn, nk), in_specs=in_specs,
        out_specs=pl.BlockSpec((tm, tn), lambda i, j, kk: (i, j)),
        out_shape=jax.ShapeDtypeStruct((m, n), out_dtype),
        scratch_shapes=[pltpu.VMEM((tm, tn), F32)] if nk > 1 else [],
        compiler_params=_params("parallel", "parallel", "arbitrary"),
    )(*args)


def _mm_tn(a, b, *, name, ts=MM_ROWS, shard_cols=None):
    s, m = a.shape
    n = b.shape[1]
    tm = _tile(m, (1408, 1024, 768, 512, 256, 128))
    tn = shard_cols or _tile(n, (1536, 1408, 1024, 768, 512, 256, 128))
    if shard_cols:
        out_spec = pl.BlockSpec((None, tm, tn), lambda i, j, kk: (j, i, 0))
        out_shape = jax.ShapeDtypeStruct((n // tn, m, tn), F32)
    else:
        out_spec = pl.BlockSpec((tm, tn), lambda i, j, kk: (i, j))
        out_shape = jax.ShapeDtypeStruct((m, n), F32)

    def body(a_ref, b_ref, o_ref):
        kk = pl.program_id(2)
        part = jnp.dot(a_ref[...].astype(BF16).T, b_ref[...].astype(BF16),
                       preferred_element_type=F32)

        @pl.when(kk == 0)
        def _():
            o_ref[...] = part

        @pl.when(kk > 0)
        def _():
            o_ref[...] += part

    return _call(
        body, name=name, grid=(m // tm, n // tn, s // ts),
        in_specs=[pl.BlockSpec((ts, tm), lambda i, j, kk: (kk, i)),
                  pl.BlockSpec((ts, tn), lambda i, j, kk: (kk, j))],
        out_specs=out_spec, out_shape=out_shape,
        compiler_params=_params("parallel", "parallel", "arbitrary"),
    )(a, b)


def _rowwise(fn, rows, consts, row_outs, acc_outs, *, name, tm=TM, n_rows=None):
    rows = [r if isinstance(r, tuple) else (r, r.shape[1], 0, 0) for r in rows]
    s = n_rows or rows[0][0].shape[0]
    nr, nc, no, na = len(rows), len(consts), len(row_outs), len(acc_outs)

    def body(*refs):
        r_in, c_in = refs[:nr], refs[nr:nr + nc]
        o_refs, a_refs = refs[nr + nc:nr + nc + no], refs[nr + nc + no:]
        outs = fn(*[r[...] for r in r_in], *[c[...] for c in c_in])
        for r, v in zip(o_refs, outs[:no]):
            r[...] = v.astype(r.dtype)
        if na:
            i = pl.program_id(0)

            @pl.when(i == 0)
            def _():
                for r, v in zip(a_refs, outs[no:]):
                    r[...] = v.astype(F32)

            @pl.when(i > 0)
            def _():
                for r, v in zip(a_refs, outs[no:]):
                    r[...] += v.astype(F32)

    in_specs = [r[1] if isinstance(r[1], pl.BlockSpec) else
                pl.BlockSpec((tm, r[1]), lambda i, cb=r[2], rb=r[3]: (i + rb, cb)) for r in rows]
    in_specs += [pl.BlockSpec(c.shape, lambda i: (0, 0)) for c in consts]
    out_specs = [pl.BlockSpec((tm, w), lambda i: (i, 0)) for w, _ in row_outs]
    out_specs += [pl.BlockSpec(sh, lambda i: (0, 0)) for sh in acc_outs]
    out_shape = [jax.ShapeDtypeStruct((s, w), dt) for w, dt in row_outs]
    out_shape += [jax.ShapeDtypeStruct(sh, F32) for sh in acc_outs]
    return _call(
        body, name=name, grid=(s // tm,), in_specs=in_specs, out_specs=out_specs,
        out_shape=out_shape, compiler_params=_params("arbitrary"),
    )(*[r[0] for r in rows], *consts)


def _rms(x, g, n=None):
    ms = jnp.sum(x * x, axis=-1, keepdims=True) / float(n or x.shape[-1])
    return x * lax.rsqrt(ms + EPS) * g


def _layer_norm(x, g, b):
    mu = jnp.sum(x, axis=-1, keepdims=True) / float(x.shape[-1])
    xc = x - mu
    var = jnp.sum(xc * xc, axis=-1, keepdims=True) / float(x.shape[-1])
    return xc * lax.rsqrt(var + EPS) * g + b


def _silu(x):
    return x * jax.nn.sigmoid(x)


@jax.custom_vjp
def _rope(y, cos, sin_a, sin_b):
    return y * cos + pltpu.roll(y, 112, 1) * sin_a + pltpu.roll(y, 16, 1) * sin_b


def _rope_fwd(y, cos, sin_a, sin_b):
    return _rope(y, cos, sin_a, sin_b), (cos, sin_a, sin_b)


def _rope_bwd(res, ct):
    cos, sin_a, sin_b = res
    dy = ct * cos + pltpu.roll(ct * sin_a, 16, 1) + pltpu.roll(ct * sin_b, 112, 1)
    return dy, jnp.zeros_like(cos), jnp.zeros_like(sin_a), jnp.zeros_like(sin_b)


_rope.defvjp(_rope_fwd, _rope_bwd)


def _qk_head(xh, g, cos, sin_a, sin_b):
    return _rope(_rms(xh, g, MLA_QK), cos, sin_a, sin_b)


def _heads(x, width):
    return [x[:, h * width:(h + 1) * width] for h in range(x.shape[1] // width)]


def _f_rms(x, g):
    return (_rms(x, g),)


def _f_rope_tab(pos, inv_freq):
    ang = pos.astype(F32) * inv_freq
    lane = lax.broadcasted_iota(jnp.int32, ang.shape, 1)
    sn = jnp.sin(ang)
    first = (lane >= MLA_NOPE) & (lane < MLA_NOPE + MLA_ROPE // 2)
    second = (lane >= MLA_NOPE + MLA_ROPE // 2) & (lane < MLA_QK)
    return jnp.cos(ang), jnp.where(first, -sn, 0.0), jnp.where(second, sn, 0.0)


def _mix_pre(za, zg, zcq, zckv, ba, bg, gq, gkv):
    u0 = (za + ba) * jax.nn.sigmoid(zg + bg)
    return u0, _rms(zcq, gq), _rms(zckv, gkv)


def _ln_silu(c1, bdw, lg, lb):
    return _silu(_layer_norm(c1 + bdw, lg, lb))


def _f_qk_prep(q0, kn, kr, cos, sa, sb, gq, gk):
    qs = [_qk_head(xh, gq, cos, sa, sb) * Q_SCALE for xh in _heads(q0, LANES)]
    ks = [_qk_head(xh + kr, gk, cos, sa, sb) for xh in _heads(kn, LANES)]
    return jnp.concatenate(qs, axis=1), jnp.concatenate(ks, axis=1)


def _act(cg, cv, bg, bv):
    return _silu(cg + bg) * (cv + bv)


def _f_mem_k(kk, g):
    return (jnp.concatenate([_rms(xh, g) for xh in _heads(kk, MEM_HEAD_DIM)], axis=1),)


def _mem_probs(qn, kmh):
    s = lax.dot_general(qn.astype(BF16), kmh, NT_DIMS, preferred_element_type=F32) * MEM_SCALE
    e = jnp.exp(s - jnp.max(s, axis=-1, keepdims=True))
    return e / jnp.sum(e, axis=-1, keepdims=True)


def _f_mem_attn(qm0, km, vm, g):
    outs = []
    for h, xh in enumerate(_heads(qm0, MEM_HEAD_DIM)):
        sl = slice(h * MEM_HEAD_DIM, (h + 1) * MEM_HEAD_DIM)
        p = _mem_probs(_rms(xh, g), km[:, sl])
        outs.append(jnp.dot(p.astype(BF16), vm[:, sl].astype(BF16), preferred_element_type=F32))
    return (jnp.concatenate(outs, axis=1),)


def _f_loss(y, t):
    e = y - t
    return e * (1.0 / D_MODEL), jnp.sum(e * e, axis=0, keepdims=True)


def _b_rms(x, dh, dres, g):
    _, vjp = jax.vjp(_rms, x, g)
    dx, dg = vjp(dh)
    return dx + dres, dg


def _b_rms_nores(x, dh, g):
    _, vjp = jax.vjp(_rms, x, g)
    dx, dg = vjp(dh)
    return dx, dg


def _b_mix_pre(za, zg, zcq, zckv, du0, dcqn, dckvn, dkr, ba, bg, gq, gkv):
    _, vjp = jax.vjp(_mix_pre, za, zg, zcq, zckv, ba, bg, gq, gkv)
    dza, dzg, dzcq, dzckv, dba, dbg, dgq, dgkv = vjp((du0, dcqn, dckvn))
    return jnp.concatenate([dza, dzg, dzcq, dzckv, dkr], axis=1), dba, dbg, dgq, dgkv


def _b_ln_silu(c1, du, bdw, lg, lb):
    _, vjp = jax.vjp(_ln_silu, c1, bdw, lg, lb)
    return vjp(du)


def _b_qk_prep(q0, kn, kr, cos, sa, sb, dq, dk, gq, gk):
    head = lambda xh, g: _qk_head(xh, g, cos, sa, sb)
    dk = dk * LN2
    dq0, dkn = [], []
    dkr = jnp.zeros_like(kr)
    dgq = jnp.zeros_like(gq)
    dgk = jnp.zeros_like(gk)
    for h, xh in enumerate(_heads(q0, LANES)):
        _, vjp = jax.vjp(head, xh, gq)
        dx, dg = vjp(dq[h, 0].T * ATT_SCALE)
        dq0.append(dx)
        dgq = dgq + dg
    for xh, ct in zip(_heads(kn, LANES), _heads(dk, LANES)):
        _, vjp = jax.vjp(head, xh + kr, gk)
        dx, dg = vjp(ct)
        dkn.append(dx)
        dkr = dkr + dx
        dgk = dgk + dg
    return jnp.concatenate(dq0, axis=1), jnp.concatenate(dkn, axis=1), dkr, dgq, dgk


def _b_mem_k(kk, dkm, g):
    dkk = []
    dg = jnp.zeros_like(g)
    for xh, ct in zip(_heads(kk, MEM_HEAD_DIM), _heads(dkm, MEM_HEAD_DIM)):
        _, vjp = jax.vjp(_rms, xh, g)
        dx, dgh = vjp(ct)
        dkk.append(dx)
        dg = dg + dgh
    return jnp.concatenate(dkk, axis=1), dg


def _b_mem_attn(dom, qm0, km, vm, g):
    dq0, dkm, dvm = [], [], []
    dg = jnp.zeros_like(g)
    for h, (xh, doh) in enumerate(zip(_heads(qm0, MEM_HEAD_DIM), _heads(dom, MEM_HEAD_DIM))):
        sl = slice(h * MEM_HEAD_DIM, (h + 1) * MEM_HEAD_DIM)
        kmh, vmh = km[:, sl], vm[:, sl].astype(BF16)
        qn, vjp = jax.vjp(_rms, xh, g)
        p = _mem_probs(qn, kmh)
        dob = doh.astype(BF16)
        dp = lax.dot_general(dob, vmh, NT_DIMS, preferred_element_type=F32)
        ds = (p * (dp - jnp.sum(dp * p, axis=-1, keepdims=True)) * MEM_SCALE).astype(BF16)
        dqn = jnp.dot(ds, kmh, preferred_element_type=F32)
        dkm.append(jnp.dot(ds.T, qn.astype(BF16), preferred_element_type=F32))
        dvm.append(jnp.dot(p.astype(BF16).T, dob, preferred_element_type=F32))
        dx, dgh = vjp(dqn)
        dq0.append(dx)
        dg = dg + dgh
    return (jnp.concatenate(dq0, axis=1), jnp.concatenate(dkm, axis=1),
            jnp.concatenate(dvm, axis=1), dg)


def _adamw(w, g, m, v):
    m = ADAM_B1 * m + (1.0 - ADAM_B1) * g
    v = ADAM_B2 * v + (1.0 - ADAM_B2) * jnp.square(g)
    m_hat = m / (1.0 - ADAM_B1 ** ADAM_STEP)
    v_hat = v / (1.0 - ADAM_B2 ** ADAM_STEP)
    delta = -ADAM_LR * (m_hat / (jnp.sqrt(v_hat) + ADAM_EPS) + ADAM_WD * w)
    return delta, m, v


def _adamw_small(parts, small, tiny, name):
    def body(*refs):
        p_ref, ins, outs = refs[0], refs[1:4 + 3 * len(TINY)], refs[4 + 3 * len(TINY):]
        me = 4 * lax.axis_index("x") + 2 * lax.axis_index("y") + lax.axis_index("c")
        groups = [(0, SMALL_ROWS)]
        groups += [(pl.multiple_of(base + me * rows, 8), rows) for base, (_, _, rows) in zip(TINY_BASE, TINY)]
        for k, (start, rows) in enumerate(groups):
            g = p_ref[0, pl.ds(start, rows), :]
            for d in range(1, N_DEV):
                g = g + p_ref[d, pl.ds(start, rows), :]
            w, m, v = [r[...] for r in ins[3 * k:3 * k + 3]]
            for ref, val in zip(outs[4 * k:4 * k + 4], (g,) + _adamw(w, g, m, v)):
                ref[...] = val

    args = list(small) + [t for grp in tiny for t in grp]
    out_shape = []
    for grp in [small] + list(tiny):
        out_shape += [jax.ShapeDtypeStruct(grp[0].shape, F32)] * 4
    return _call(body, name=name, out_shape=out_shape)(parts, *args)


def _conv_fwd(x, w, name):
    s, ch = x.shape
    kw = w.shape[0]
    halo = -(-(kw - 1) // 8) * 8
    r = CONV_ROWS
    n = s // r

    def chunk(window, wv):
        acc = jnp.zeros((r, LANES), F32)
        for k in range(kw):
            shift = kw - 1 - k
            sh = window if shift == 0 else pltpu.roll(window, shift, 0)
            acc = acc + sh[halo:halo + r] * wv[k:k + 1]
        return acc

    def body(x_ref, w_ref, y_ref):
        wv = w_ref[...]
        first = jnp.concatenate([jnp.zeros((halo, LANES), F32), x_ref[0:r]], axis=0)
        y_ref[0:r] = chunk(first, wv)

        def step(i, carry):
            base = pl.multiple_of(i * r, 8)
            y_ref[pl.ds(base, r)] = chunk(x_ref[pl.ds(base - halo, r + halo)], wv)
            return carry

        lax.fori_loop(1, n, step, 0)

    return _call(
        body, name=name, grid=(ch // LANES,),
        in_specs=[pl.BlockSpec((s, LANES), lambda c: (0, c)), pl.BlockSpec((kw, LANES), lambda c: (0, c))],
        out_specs=pl.BlockSpec((s, LANES), lambda c: (0, c)),
        out_shape=jax.ShapeDtypeStruct((s, ch), F32), compiler_params=_params("parallel"),
    )(x, w)


def _conv_bwd(dy, x, w, name):
    s, ch = x.shape
    kw = w.shape[0]
    halo = -(-(kw - 1) // 8) * 8
    r = CONV_ROWS
    n = s // r

    def dx_chunk(window, wv):
        acc = jnp.zeros((r, LANES), F32)
        for k in range(kw):
            shift = kw - 1 - k
            sh = window if shift == 0 else pltpu.roll(window, r + halo - shift, 0)
            acc = acc + sh[0:r] * wv[k:k + 1]
        return acc

    def dw_chunk(xwin, dyc, acc_ref):
        for k in range(kw):
            shift = kw - 1 - k
            sh = xwin if shift == 0 else pltpu.roll(xwin, shift, 0)
            prod = sh[halo:halo + r] * dyc
            acc_ref[k] += jnp.sum(prod.reshape(r // 8, 8, LANES), axis=0)

    def body(dy_ref, x_ref, w_ref, dx_ref, dw_ref, acc_ref):
        wv = w_ref[...]
        acc_ref[...] = jnp.zeros_like(acc_ref)
        xfirst = jnp.concatenate([jnp.zeros((halo, LANES), F32), x_ref[0:r]], axis=0)
        dw_chunk(xfirst, dy_ref[0:r], acc_ref)
        last = jnp.concatenate([dy_ref[s - r:s], jnp.zeros((halo, LANES), F32)], axis=0)
        dx_ref[s - r:s] = dx_chunk(last, wv)

        def step(i, carry):
            base = pl.multiple_of(i * r, 8)
            dw_chunk(x_ref[pl.ds(base - halo, r + halo)], dy_ref[pl.ds(base, r)], acc_ref)
            prev = pl.multiple_of((i - 1) * r, 8)
            dx_ref[pl.ds(prev, r)] = dx_chunk(dy_ref[pl.ds(prev, r + halo)], wv)
            return carry

        lax.fori_loop(1, n, step, 0)
        dw_ref[...] = jnp.sum(acc_ref[...], axis=1)

    spec = pl.BlockSpec((s, LANES), lambda c: (0, c))
    wspec = pl.BlockSpec((kw, LANES), lambda c: (0, c))
    return _call(
        body, name=name, grid=(ch // LANES,), in_specs=[spec, spec, wspec], out_specs=[spec, wspec],
        out_shape=[jax.ShapeDtypeStruct((s, ch), F32), jax.ShapeDtypeStruct((kw, ch), F32)],
        scratch_shapes=[pltpu.VMEM((kw, 8, LANES), F32)], compiler_params=_params("parallel"),
    )(dy, x, w)


HALO = 8


def _conv3(win, w, rows):
    return (pltpu.roll(win, 2, 0)[HALO:HALO + rows] * w[0:1] + pltpu.roll(win, 1, 0)[HALO:HALO + rows] * w[1:2]
            + win[HALO:HALO + rows] * w[2:3])


def _ffn_mid_fwd(up_g, up_v, w_g, w_v, b_g, b_v, name):
    s, width = up_g.shape
    tm, tc = TM, FF_PAD

    def body(pg_ref, g_ref, pv_ref, v_ref, wg_ref, wv_ref, bg_ref, bv_ref, act_ref):
        keep = (pl.program_id(0) > 0).astype(F32)
        cg = _conv3(jnp.concatenate([pg_ref[...] * keep, g_ref[...]], axis=0), wg_ref[...], tm)
        cv = _conv3(jnp.concatenate([pv_ref[...] * keep, v_ref[...]], axis=0), wv_ref[...], tm)
        act_ref[...] = _act(cg, cv, bg_ref[...], bv_ref[...]).astype(act_ref.dtype)

    tile = pl.BlockSpec((tm, tc), lambda i, c: (i, c))
    prev = pl.BlockSpec((HALO, tc), lambda i, c: (jnp.maximum(i * (tm // HALO) - 1, 0), c))
    wspec = pl.BlockSpec((FFN_CONV_WIDTH, tc), lambda i, c: (0, c))
    bspec = pl.BlockSpec((1, tc), lambda i, c: (0, c))
    return _call(
        body, name=name, grid=(s // tm, width // tc),
        in_specs=[prev, tile, prev, tile, wspec, wspec, bspec, bspec], out_specs=tile,
        out_shape=jax.ShapeDtypeStruct((s, width), BF16), compiler_params=_params("parallel", "parallel"),
    )(up_g, up_g, up_v, up_v, w_g, w_v, b_g, b_v)


def _ffn_mid_bwd(up_g, up_v, dact, w_g, w_v, b_g, b_v, name):
    s, width = up_g.shape
    tm, tc = TM, FF_PAD
    n_row = s // tm

    def body(pg_ref, g_ref, ng_ref, pv_ref, v_ref, nv_ref, d_ref, nd_ref, wg_ref, wv_ref, bg_ref, bv_ref,
             dug_ref, duv_ref, dwg_ref, dwv_ref, dbg_ref, dbv_ref):
        i = pl.program_id(1)
        first = (i > 0).astype(F32)
        last = (i < n_row - 1).astype(F32)
        ext = tm + HALO
        wg, wv = wg_ref[...], wv_ref[...]
        xg = jnp.concatenate([pg_ref[...] * first, g_ref[...], ng_ref[...]], axis=0)
        xv = jnp.concatenate([pv_ref[...] * first, v_ref[...], nv_ref[...]], axis=0)
        d_ext = jnp.concatenate([d_ref[...], nd_ref[...] * last], axis=0)
        _, vjp = jax.vjp(lambda cg_, cv_: _act(cg_, cv_, bg_ref[...], bv_ref[...]),
                         _conv3(xg, wg, ext), _conv3(xv, wv, ext))
        dcg, dcv = vjp(d_ext)
        results = []
        for x, w, dc in ((xg, wg, dcg), (xv, wv, dcv)):
            dup = (dc[:tm] * w[2:3] + pltpu.roll(dc, ext - 1, 0)[:tm] * w[1:2]
                   + pltpu.roll(dc, ext - 2, 0)[:tm] * w[0:1])
            own = dc[:tm]
            dw = jnp.concatenate([
                jnp.sum(own * pltpu.roll(x, 2, 0)[HALO:HALO + tm], axis=0, keepdims=True),
                jnp.sum(own * pltpu.roll(x, 1, 0)[HALO:HALO + tm], axis=0, keepdims=True),
                jnp.sum(own * x[HALO:HALO + tm], axis=0, keepdims=True)], axis=0)
            results.append((dup, dw, jnp.sum(own, axis=0, keepdims=True)))
        (dug, dwg, dbg), (duv, dwv, dbv) = results
        dug_ref[...] = dug.astype(dug_ref.dtype)
        duv_ref[...] = duv.astype(duv_ref.dtype)

        @pl.when(i == 0)
        def _():
            dwg_ref[...], dwv_ref[...], dbg_ref[...], dbv_ref[...] = dwg, dwv, dbg, dbv

        @pl.when(i > 0)
        def _():
            dwg_ref[...] += dwg
            dwv_ref[...] += dwv
            dbg_ref[...] += dbg
            dbv_ref[...] += dbv

    per = tm // HALO
    tile = pl.BlockSpec((tm, tc), lambda c, i: (i, c))
    prev = pl.BlockSpec((HALO, tc), lambda c, i: (jnp.maximum(i * per - 1, 0), c))
    nxt = pl.BlockSpec((HALO, tc), lambda c, i: (jnp.minimum((i + 1) * per, s // HALO - 1), c))
    wspec = pl.BlockSpec((FFN_CONV_WIDTH, tc), lambda c, i: (0, c))
    bspec = pl.BlockSpec((1, tc), lambda c, i: (0, c))
    wide = jax.ShapeDtypeStruct((s, width), BF16)
    return _call(
        body, name=name, grid=(width // tc, n_row),
        in_specs=[prev, tile, nxt, prev, tile, nxt, tile, nxt, wspec, wspec, bspec, bspec],
        out_specs=[tile, tile, wspec, wspec, bspec, bspec],
        out_shape=[wide, wide] + [jax.ShapeDtypeStruct((FFN_CONV_WIDTH, width), F32)] * 2
        + [jax.ShapeDtypeStruct((1, width), F32)] * 2,
        compiler_params=_params("parallel", "arbitrary"),
    )(up_g, up_g, up_g, up_v, up_v, up_v, dact, dact, w_g, w_v, b_g, b_v)


def _chunk_mask(rows_are_queries):
    a = lax.broadcasted_iota(jnp.int32, (TQ, TQ), 0) // CHUNK
    b = lax.broadcasted_iota(jnp.int32, (TQ, TQ), 1) // CHUNK
    return (b <= a) if rows_are_queries else (a <= b)


def _head_lanes(hh):
    return slice(hh * LANES, (hh + 1) * LANES)


def _to_row(col):
    return jnp.broadcast_to(col, (TQ, LANES)).T[0:1, :]


def _flash_specs(s, heads=HEADS_PER_STEP):
    width = heads * LANES
    tile = pl.BlockSpec((TQ, width), lambda h, i: (i, h))
    whole = pl.BlockSpec((s, width), lambda h, i: (0, h))
    row_tile = pl.BlockSpec((heads, 1, 1, TQ), lambda h, i: (h, i, 0, 0))
    row_whole = pl.BlockSpec((heads, s // TQ, 1, TQ), lambda h, i: (h, 0, 0, 0))
    return tile, whole, row_tile, row_whole


def _split_refs(refs, n_in, n_out, ex):
    e_in, e_out = (len(ex.inputs), len(ex.out_shape)) if ex else (0, 0)
    a, b, c = n_in + e_in, n_in + e_in + n_out, n_in + e_in + n_out + e_out
    return refs[:n_in], refs[a:b], (refs[n_in:a], refs[b:c], refs[c:])


def _hosted(ex, ex_refs, grid, when_first):
    if ex is None:
        return
    ids = [pl.program_id(d) for d in range(len(grid))]
    cond = functools.reduce(
        lambda p, q_: p & q_, [i == (0 if when_first else g - 1) for i, g in zip(ids, grid)])

    @pl.when(cond)
    def _():
        (ex.start if when_first else ex.finish)(*ex_refs)


def _host_call(body, ex, name, grid, in_specs, out_specs, out_shape, args):
    e_in, e_out = (len(ex.inputs), len(ex.out_shape)) if ex else (0, 0)
    res = _call(
        body, name=name, grid=grid, in_specs=list(in_specs) + [ANY] * e_in,
        out_specs=list(out_specs) + [ANY] * e_out,
        out_shape=list(out_shape) + (ex.out_shape if ex else []),
        scratch_shapes=ex.scratch if ex else [],
        compiler_params=_params(*["arbitrary"] * len(grid)),
    )(*args, *(ex.inputs if ex else []))
    return res[:len(out_shape)], res[len(out_shape):]


def _flash_fwd(q, k, v, name, ex=None):
    s = q.shape[0]
    nq = s // TQ
    heads = FWD_HEADS
    grid = (MLA_HEADS // heads, nq)

    def body(*refs):
        (q_ref, k_ref, v_ref), (o_ref, lse_row_ref), ex_refs = _split_refs(refs, 3, 2, ex)
        _hosted(ex, ex_refs, grid, True)
        i = pl.program_id(1)
        qs = [q_ref[:, _head_lanes(hh)] for hh in range(heads)]

        def step(j, carry, masked):
            base = pl.multiple_of(j * TQ, TQ)
            out = []
            for hh in range(heads):
                m_prev, l_prev, acc = carry[hh]
                kj = k_ref[pl.ds(base, TQ), _head_lanes(hh)]
                vt = v_ref[pl.ds(base, TQ), _head_lanes(hh)].T
                sc = lax.dot_general(kj, qs[hh], NT_DIMS, preferred_element_type=F32)
                if masked:
                    sc = jnp.where(_chunk_mask(False), sc, NEG)
                m_new = jnp.maximum(m_prev, jnp.max(sc, axis=0, keepdims=True))
                alpha = jnp.exp2(m_prev - m_new)
                p = jnp.exp2(sc - m_new)
                l_new = alpha * l_prev + jnp.sum(p, axis=0, keepdims=True)
                acc = acc * alpha + jnp.dot(vt, p.astype(BF16), preferred_element_type=F32)
                out.append((m_new, l_new, acc))
            return tuple(out)

        init = tuple((jnp.full((1, TQ), NEG, F32), jnp.zeros((1, TQ), F32), jnp.zeros((LANES, TQ), F32))
                     for _ in range(heads))
        carry = lax.fori_loop(0, i, functools.partial(step, masked=False), init)
        for hh, (m_fin, l_fin, acc) in enumerate(step(i, carry, True)):
            o_ref[:, _head_lanes(hh)] = (acc / l_fin).T
            lse_row_ref[hh, 0] = m_fin + jnp.log2(l_fin)
        _hosted(ex, ex_refs, grid, False)

    tile, whole, row_tile, _ = _flash_specs(s, heads)
    return _host_call(
        body, ex, name, grid, [tile, whole, whole], [tile, row_tile],
        [jax.ShapeDtypeStruct((s, MLA_HEADS * LANES), F32),
         jax.ShapeDtypeStruct((MLA_HEADS, nq, 1, TQ), F32)], (q, k, v))


def _attn_delta(do, o, name):
    s = do.shape[0]

    def body(do_ref, o_ref, d_ref):
        for h in range(MLA_HEADS):
            prod = do_ref[:, _head_lanes(h)] * o_ref[:, _head_lanes(h)]
            d_ref[h, 0] = _to_row(jnp.sum(prod, axis=-1, keepdims=True))

    tile = pl.BlockSpec((TQ, MLA_HEADS * LANES), lambda i: (i, 0))
    return _call(
        body, name=name, grid=(s // TQ,), in_specs=[tile, tile],
        out_specs=pl.BlockSpec((MLA_HEADS, 1, 1, TQ), lambda i: (0, i, 0, 0)),
        out_shape=jax.ShapeDtypeStruct((MLA_HEADS, s // TQ, 1, TQ), F32),
        compiler_params=_params("parallel"),
    )(do, o)


def _flash_bwd(q, k, v, do, lse_row, delta_row, name, ex=None):
    s = q.shape[0]
    nq = s // TQ
    grid = (MLA_HEADS // HEADS_PER_STEP, nq)

    def body(*refs):
        ins, (dqt_ref, dk_ref, dv_ref), ex_refs = _split_refs(refs, 6, 3, ex)
        q_ref, k_ref, v_ref, do_ref, lse_row_ref, delta_row_ref = ins
        _hosted(ex, ex_refs, grid, True)
        j = pl.program_id(1)

        @pl.when(j == 0)
        def _():
            dqt_ref[...] = jnp.zeros_like(dqt_ref)

        kjs = [k_ref[:, _head_lanes(hh)] for hh in range(HEADS_PER_STEP)]
        vjs = [v_ref[:, _head_lanes(hh)] for hh in range(HEADS_PER_STEP)]
        kts = [kj.T for kj in kjs]

        def step(i, carry, masked):
            base = pl.multiple_of(i * TQ, TQ)
            out = []
            for hh in range(HEADS_PER_STEP):
                dk, dv = carry[hh]
                qi = q_ref[pl.ds(base, TQ), _head_lanes(hh)]
                dob = do_ref[pl.ds(base, TQ), _head_lanes(hh)].astype(BF16)
                sc_t = lax.dot_general(kjs[hh], qi, NT_DIMS, preferred_element_type=F32)
                if masked:
                    sc_t = jnp.where(_chunk_mask(False), sc_t, NEG)
                p_t = jnp.exp2(sc_t - lse_row_ref[hh, i])
                dv = dv + jnp.dot(p_t.astype(BF16), dob, preferred_element_type=F32)
                dp_t = lax.dot_general(vjs[hh], dob, NT_DIMS, preferred_element_type=F32)
                ds_t = (p_t * (dp_t - delta_row_ref[hh, i])).astype(BF16)
                dk = dk + jnp.dot(ds_t, qi, preferred_element_type=F32)
                dqt_ref[hh, i] += jnp.dot(kts[hh], ds_t, preferred_element_type=F32)
                out.append((dk, dv))
            return tuple(out)

        zero = jnp.zeros((TQ, LANES), F32)
        carry = step(j, tuple((zero, zero) for _ in range(HEADS_PER_STEP)), True)
        carry = lax.fori_loop(j + 1, nq, functools.partial(step, masked=False), carry)
        for hh, (dk, dv) in enumerate(carry):
            dk_ref[:, _head_lanes(hh)] = dk
            dv_ref[:, _head_lanes(hh)] = dv.astype(dv_ref.dtype)
        _hosted(ex, ex_refs, grid, False)

    tile, whole, _, row_whole = _flash_specs(s)
    dqt_spec = pl.BlockSpec((HEADS_PER_STEP, nq, LANES, TQ), lambda h, j: (h, 0, 0, 0))
    wide = lambda dt: jax.ShapeDtypeStruct((s, MLA_HEADS * LANES), dt)
    return _host_call(
        body, ex, name, grid, [whole, tile, tile, whole, row_whole, row_whole], [dqt_spec, tile, tile],
        [jax.ShapeDtypeStruct((MLA_HEADS, nq, LANES, TQ), F32), wide(F32), wide(BF16)],
        (q, k, v, do, lse_row, delta_row))


def _side_by_side(g):
    return g.transpose(1, 0, 2).reshape(g.shape[1], N_DEV * g.shape[2])


def _col_shards(g):
    return g.reshape(g.shape[0], N_DEV, g.shape[1] // N_DEV).transpose(1, 0, 2)


def _pad_last(v, to):
    return jnp.pad(v, [(0, 0)] * (v.ndim - 1) + [(0, to - v.shape[-1])])


def _tiny_rows(v, rows):
    flat = v.reshape(v.shape[:-2] + (-1,))
    return _pad_last(flat, rows * LANES).reshape(v.shape[:-2] + (rows, LANES))


def _pack_small(vals):
    parts = []
    for (n, size), pad in zip(SMALL, SMALL_PAD):
        parts.append(jnp.pad(vals[n].reshape(-1), (0, pad - size)))
    flat = jnp.concatenate(parts)
    return jnp.pad(flat, (0, SMALL_ROWS * LANES - flat.shape[0])).reshape(SMALL_ROWS, LANES)


def _unpack_small(packed):
    flat = packed.reshape(-1)
    out, off = {}, 0
    for (n, size), pad in zip(SMALL, SMALL_PAD):
        out[n] = flat[off:off + size].reshape(1, size)
        off += pad
    return out


def _pad_heads(w, per_head, axis):
    shape = list(w.shape)
    shape[axis:axis + 1] = [MLA_HEADS, per_head]
    w = w.reshape(shape)
    pad = [(0, 0)] * len(shape)
    pad[axis + 1] = (0, LANES - per_head)
    w = jnp.pad(w, pad)
    shape[axis:axis + 2] = [MLA_HEADS * LANES]
    return w.reshape(shape)


def _unpad_heads(w, per_head, axis):
    shape = list(w.shape)
    shape[axis:axis + 1] = [MLA_HEADS, LANES]
    w = w.reshape(shape)
    w = lax.slice_in_dim(w, 0, per_head, axis=axis + 1)
    shape[axis:axis + 2] = [MLA_HEADS * per_head]
    return w.reshape(shape)


def _row(v, pad_to=None):
    v = v.reshape(1, -1)
    if pad_to is not None:
        v = jnp.pad(v, ((0, 0), (0, pad_to - v.shape[1])))
    return v


def kernel(x, mem, positions, mix_norm_g, w_in, b_conv_in, w_conv_dw, b_conv_dw, conv_ln_g, conv_ln_b, q_lat_norm_g, w_uq, kv_lat_norm_g, w_ukv, q_norm_g, k_norm_g, w_out, mem_norm_x_g, mem_norm_m_g, w_mem_q, w_mem_kv, mem_q_norm_g, mem_k_norm_g, w_mem_o, ffn_norm_g, w_up, w_ffn_dw, b_ffn_dw, w_down, loss_target, m_mix_norm_g, m_w_in, m_b_conv_in, m_w_conv_dw, m_b_conv_dw, m_conv_ln_g, m_conv_ln_b, m_q_lat_norm_g, m_w_uq, m_kv_lat_norm_g, m_w_ukv, m_q_norm_g, m_k_norm_g, m_w_out, m_mem_norm_x_g, m_mem_norm_m_g, m_w_mem_q, m_w_mem_kv, m_mem_q_norm_g, m_mem_k_norm_g, m_w_mem_o, m_ffn_norm_g, m_w_up, m_w_ffn_dw, m_b_ffn_dw, m_w_down, v_mix_norm_g, v_w_in, v_b_conv_in, v_w_conv_dw, v_b_conv_dw, v_conv_ln_g, v_conv_ln_b, v_q_lat_norm_g, v_w_uq, v_kv_lat_norm_g, v_w_ukv, v_q_norm_g, v_k_norm_g, v_w_out, v_mem_norm_x_g, v_mem_norm_m_g, v_w_mem_q, v_w_mem_kv, v_mem_q_norm_g, v_mem_k_norm_g, v_w_mem_o, v_ffn_norm_g, v_w_up, v_w_ffn_dw, v_b_ffn_dw, v_w_down):
    a = dict(locals())
    seq = x.shape[1]
    xs = x.reshape(seq, D_MODEL)
    mems = mem.reshape(-1, D_MODEL)
    target = loss_target.reshape(seq, D_MODEL)

    tiny = [n for n, _, _ in TINY]
    shard = lambda n: a[n][0] if n in tiny else a[n][0].astype(BF16)
    pos = jnp.stack([2 * lax.axis_index("x") + lax.axis_index("y"), lax.axis_index("c")]).astype(jnp.int32)
    ag_first = ["w_in", "w_uq", "w_ukv", "w_conv_dw"]
    ag_later = [n for n in [b for b, _ in BIG] + tiny if n not in ag_first]
    wg = dict(zip(ag_first, _all_gather([shard(n) for n in ag_first], "ag_weights_first")))
    wi = _side_by_side(wg["w_in"])
    s3 = 2 * CONV_CH + MLA_Q_RANK + MLA_KV_RANK
    w_in_p = jnp.concatenate([
        wi[:, :s3], jnp.zeros((D_MODEL, MLA_NOPE), BF16), wi[:, s3:],
        jnp.zeros((D_MODEL, LANES - MLA_QK), BF16)], axis=1)
    w_uq_p = _side_by_side(_pad_last(wg["w_uq"], LANES))
    w_uk_p = _side_by_side(_pad_last(wg["w_ukv"][:, :, :MLA_NOPE], LANES))
    w_uv_p = _side_by_side(_pad_last(wg["w_ukv"][:, :, MLA_NOPE:], LANES))
    w_cdw = _side_by_side(wg["w_conv_dw"])

    g_mix, g_qlat, g_kvlat = _row(mix_norm_g), _row(q_lat_norm_g), _row(kv_lat_norm_g)
    b_in = _row(b_conv_in)
    b_in_a, b_in_g = b_in[:, :CONV_CH], b_in[:, CONV_CH:]
    b_cdw, ln_g, ln_b = _row(b_conv_dw), _row(conv_ln_g), _row(conv_ln_b)
    g_q, g_k = _row(q_norm_g, LANES), _row(k_norm_g, LANES)
    g_memx, g_memm = _row(mem_norm_x_g), _row(mem_norm_m_g)
    g_mq, g_mk, g_ffn = _row(mem_q_norm_g), _row(mem_k_norm_g), _row(ffn_norm_g)
    b_f = _pad_last(b_ffn_dw.reshape(N_DEV, FF_SHARD), FF_PAD)
    b_f_g, b_f_v = b_f[:4].reshape(1, D_FF_PAD), b_f[4:].reshape(1, D_FF_PAD)

    freq = ROPE_THETA ** (-jnp.arange(0, MLA_ROPE, 2, dtype=F32) / MLA_ROPE)
    inv_freq = jnp.concatenate([jnp.zeros((MLA_NOPE,), F32), freq, freq,
                                jnp.zeros((LANES - MLA_QK,), F32)]).reshape(1, LANES)
    cos, sin_a, sin_b = _rowwise(_f_rope_tab, [positions.reshape(seq, 1)], [inv_freq],
                                 [(LANES, F32)] * 3, [], name="rope_tables")

    (h1,) = _rowwise(_f_rms, [xs], [g_mix], [(D_MODEL, BF16)], [], name="rms_mix")
    z = _mm(h1, w_in_p, name="mm_in")
    z_rows = [(z, CONV_CH, 0, 0), (z, CONV_CH, 1, 0), (z, MLA_Q_RANK, 4, 0), (z, MLA_KV_RANK, 10, 0)]
    z_kr = (z, LANES, 11, 0)
    u0, cqn, ckvn = _rowwise(
        _mix_pre, z_rows, [b_in_a, b_in_g, g_qlat, g_kvlat],
        [(CONV_CH, F32), (MLA_Q_RANK, BF16), (MLA_KV_RANK, BF16)], [], name="mix_pre")
    c1 = _conv_fwd(u0, w_cdw, "conv31_fwd")
    (u,) = _rowwise(lambda c, b, g, bb: (_ln_silu(c, b, g, bb),), [c1], [b_cdw, ln_g, ln_b],
                    [(CONV_CH, BF16)], [], name="ln_silu")
    q0 = _mm(cqn, w_uq_p, name="mm_uq")
    kn0 = _mm(ckvn, w_uk_p, name="mm_uk")
    v0 = _mm(ckvn, w_uv_p, out_dtype=BF16, name="mm_uv")
    qk_rows = [q0, kn0, z_kr, cos, sin_a, sin_b]
    qh, kh = _rowwise(_f_qk_prep, qk_rows, [g_q, g_k],
                      [(MLA_HEADS * LANES, BF16)] * 2, [], name="qk_prep")
    (attn, lse_row), later = _flash_fwd(
        qh, kh, v0, "flash_fwd", _plan_all_gather([shard(n) for n in ag_later]))
    wg.update(zip(ag_later, later))
    w_out = wg["w_out"].reshape(D_MODEL, D_MODEL)
    w_out_u = w_out[:CONV_CH]
    w_out_a = _pad_heads(w_out[CONV_CH:], MLA_V, 0)
    w_mq, w_mo = wg["w_mem_q"].reshape(D_MODEL, D_MODEL), wg["w_mem_o"].reshape(D_MODEL, D_MODEL)
    w_mkv = _side_by_side(wg["w_mem_kv"])
    w_up_p = _pad_last(wg["w_up"], FF_PAD)
    w_dn = jnp.pad(wg["w_down"].reshape(4, FF_SHARD, D_MODEL),
                   ((0, 0), (0, FF_PAD - FF_SHARD), (0, 0))).reshape(D_FF_PAD, D_MODEL)
    w_fdw = _pad_last(wg["w_ffn_dw"], FF_PAD)
    w_fdw_g = w_fdw[:4].transpose(1, 0, 2).reshape(FFN_CONV_WIDTH, D_FF_PAD)
    w_fdw_v = w_fdw[4:].transpose(1, 0, 2).reshape(FFN_CONV_WIDTH, D_FF_PAD)
    x1 = _mm(u, w_out_u, add=xs, name="mm_out_u")
    x1 = _mm(attn, w_out_a, add=x1, name="mm_out_a")

    (hq,) = _rowwise(_f_rms, [x1], [g_memx], [(D_MODEL, BF16)], [], name="rms_memx")
    (hm,) = _rowwise(_f_rms, [mems], [g_memm], [(D_MODEL, BF16)], [], name="rms_memm", tm=mems.shape[0])
    qm0 = _mm(hq, w_mq, name="mm_memq")
    kvm0 = _mm(hm, w_mkv, name="mm_memkv", tm=mems.shape[0])
    (km,) = _rowwise(_f_mem_k, [(kvm0, D_MODEL, 0, 0)], [g_mk], [(D_MODEL, BF16)], [],
                     name="mem_k", tm=mems.shape[0])
    vm = kvm0[:, D_MODEL:]
    (om,) = _rowwise(_f_mem_attn, [qm0], [km, vm, g_mq], [(D_MODEL, BF16)], [], name="mem_attn")
    x2 = _mm(om, w_mo, add=x1, name="mm_memo")

    (h3,) = _rowwise(_f_rms, [x2], [g_ffn], [(D_MODEL, BF16)], [], name="rms_ffn")
    up_g = _mm(h3, w_up_p, b_shards=(0, 4), name="mm_up_g")
    up_v = _mm(h3, w_up_p, b_shards=(4, 4), name="mm_up_v")
    act = _ffn_mid_fwd(up_g, up_v, w_fdw_g, w_fdw_v, b_f_g, b_f_v, "ffn_mid")
    y = _mm(act, w_dn, add=x2, name="mm_down")
    dy, sq = _rowwise(_f_loss, [y, target], [], [(D_MODEL, F32)], [(1, D_MODEL)], name="loss")
    loss = lax.psum(0.5 * jnp.sum(sq) / D_MODEL, ("x", "y", "c"))

    gw, gs, gt = {}, {}, {}
    gw_dn = _mm_tn(act, dy, name="tn_down").reshape(4, FF_PAD, D_MODEL)
    gw["w_down"] = gw_dn[:, :FF_SHARD].reshape(N_DEV, FF_SHARD // 2, D_MODEL)
    dact = _mm(dy, w_dn, trans_b=True, name="mm_down_t")
    dup_g, dup_v, dwf_g, dwf_v, db_g, db_v = _ffn_mid_bwd(
        up_g, up_v, dact, w_fdw_g, w_fdw_v, b_f_g, b_f_v, "ffn_mid_bwd")
    db_f = jnp.concatenate([db_g.reshape(4, FF_PAD), db_v.reshape(4, FF_PAD)], axis=0)
    gs["b_ffn_dw"] = db_f[:, :FF_SHARD].reshape(1, 2 * D_FF)
    dwf = jnp.concatenate([dwf_g.reshape(FFN_CONV_WIDTH, 4, FF_PAD), dwf_v.reshape(FFN_CONV_WIDTH, 4, FF_PAD)], axis=1)
    gt["w_ffn_dw"] = dwf[:, :, :FF_SHARD].transpose(1, 0, 2)
    gw_up = jnp.concatenate([_mm_tn(h3, dup_g, shard_cols=FF_PAD, name="tn_up_g"),
                             _mm_tn(h3, dup_v, shard_cols=FF_PAD, name="tn_up_v")], axis=0)
    gw["w_up"] = gw_up[:, :, :FF_SHARD]
    dh3 = _mm(dup_g, w_up_p, trans_b=True, b_shards=(0, 4), name="mm_up_g_t")
    dh3 = _mm(dup_v, w_up_p, trans_b=True, b_shards=(4, 4), add=dh3, name="mm_up_v_t")
    dx2, gs["ffn_norm_g"] = _rowwise(_b_rms, [x2, dh3, dy], [g_ffn], [(D_MODEL, F32)], [(1, D_MODEL)],
                                     name="rms_ffn_bwd")

    gw["w_mem_o"] = _mm_tn(om, dx2, name="tn_memo").reshape(N_DEV, -1, D_MODEL)
    dom = _mm(dx2, w_mo, trans_b=True, out_dtype=BF16, name="mm_memo_t")
    n_mem = mems.shape[0]
    dqm0, dkm, dvm, gs["mem_q_norm_g"] = _rowwise(
        _b_mem_attn, [dom, qm0], [km, vm, g_mq], [(D_MODEL, BF16)],
        [(n_mem, D_MODEL), (n_mem, D_MODEL), (1, MEM_HEAD_DIM)], name="mem_attn_bwd")
    gw["w_mem_q"] = _mm_tn(hq, dqm0, name="tn_memq").reshape(N_DEV, -1, D_MODEL)
    dhq = _mm(dqm0, w_mq, trans_b=True, name="mm_memq_t")
    dx1, gs["mem_norm_x_g"] = _rowwise(_b_rms, [x1, dhq, dx2], [g_memx], [(D_MODEL, F32)],
                                       [(1, D_MODEL)], name="rms_memx_bwd")
    dkk, gs["mem_k_norm_g"] = _rowwise(_b_mem_k, [(kvm0, D_MODEL, 0, 0), dkm], [g_mk],
                                       [(D_MODEL, F32)], [(1, MEM_HEAD_DIM)], name="mem_k_bwd", tm=n_mem)
    dkvm0 = jnp.concatenate([dkk, dvm], axis=1)
    gw["w_mem_kv"] = _col_shards(_mm_tn(hm, dkvm0, name="tn_memkv", ts=n_mem))
    dhm = _mm(dkvm0, w_mkv, trans_b=True, name="mm_memkv_t", tm=n_mem)
    _, gs["mem_norm_m_g"] = _rowwise(_b_rms_nores, [mems, dhm], [g_memm], [(D_MODEL, F32)],
                                     [(1, D_MODEL)], name="rms_memm_bwd", tm=n_mem)

    gw_out_u = _mm_tn(u, dx1, name="tn_out_u")
    gw_out_a = _mm_tn(attn, dx1, name="tn_out_a")
    gw["w_out"] = jnp.concatenate([gw_out_u, _unpad_heads(gw_out_a, MLA_V, 0)], axis=0).reshape(N_DEV, -1, D_MODEL)
    du = _mm(dx1, w_out_u, trans_b=True, name="mm_out_u_t")
    dattn = _mm(dx1, w_out_a, trans_b=True, name="mm_out_a_t")
    dc1, gs["b_conv_dw"], gs["conv_ln_g"], gs["conv_ln_b"] = _rowwise(
        _b_ln_silu, [c1, du], [b_cdw, ln_g, ln_b], [(CONV_CH, F32)], [(1, CONV_CH)] * 3, name="ln_silu_bwd")
    du0, g_cdw = _conv_bwd(dc1, u0, w_cdw, "conv31_bwd")
    gt["w_conv_dw"] = _col_shards(g_cdw)
    rs_first = ["w_up", "w_down", "w_mem_o", "w_mem_q", "w_mem_kv", "w_out"]
    grads = [gw[n] for n in rs_first]
    sums_first = _rs_add(grads, _swap_sibling(grads, "rs_sibling_first"), pos, "rs_add_first")
    delta_row = _attn_delta(dattn, attn, "attn_delta")
    (dqt, dkh, dv0), recvs_first = _flash_bwd(
        qh, kh, v0, dattn, lse_row, delta_row, "flash_bwd", _plan_swap_chips(sums_first))
    dqt_row = (dqt, pl.BlockSpec((MLA_HEADS, 1, LANES, TQ), lambda i: (0, i, 0, 0)))
    dq0, dkn0, dkr, dgq, dgk = _rowwise(
        _b_qk_prep, qk_rows + [dqt_row, dkh], [g_q, g_k],
        [(MLA_HEADS * LANES, BF16)] * 2 + [(LANES, F32)], [(1, LANES)] * 2, name="qk_prep_bwd")
    gs["q_norm_g"], gs["k_norm_g"] = dgq[:, :MLA_QK], dgk[:, :MLA_QK]
    gw["w_uq"] = _col_shards(_mm_tn(cqn, dq0, name="tn_uq"))[:, :, :MLA_QK]
    g_uk = _col_shards(_mm_tn(ckvn, dkn0, name="tn_uk"))[:, :, :MLA_NOPE]
    g_uv = _col_shards(_mm_tn(ckvn, dv0, name="tn_uv"))[:, :, :MLA_V]
    gw["w_ukv"] = jnp.concatenate([g_uk, g_uv], axis=2)
    dcqn = _mm(dq0, w_uq_p, trans_b=True, name="mm_uq_t")
    dckvn = _mm(dkn0, w_uk_p, trans_b=True, name="mm_uk_t")
    dckvn = _mm(dv0, w_uv_p, trans_b=True, add=dckvn, name="mm_uv_t")
    dz, dba, dbg, gs["q_lat_norm_g"], gs["kv_lat_norm_g"] = _rowwise(
        _b_mix_pre, z_rows + [du0, dcqn, dckvn, dkr], [b_in_a, b_in_g, g_qlat, g_kvlat],
        [(IN_PAD, BF16)], [(1, CONV_CH)] * 2 + [(1, MLA_Q_RANK), (1, MLA_KV_RANK)], name="mix_pre_bwd")
    gs["b_conv_in"] = jnp.concatenate([dba, dbg], axis=1)
    gw_in = _mm_tn(h1, dz, name="tn_in")
    gw["w_in"] = _col_shards(jnp.concatenate([gw_in[:, :s3], gw_in[:, s3 + MLA_NOPE:s3 + MLA_QK]], axis=1))
    dh1 = _mm(dz, w_in_p, trans_b=True, name="mm_in_t")
    dx, gs["mix_norm_g"] = _rowwise(_b_rms, [xs, dh1, dx1], [g_mix], [(D_MODEL, F32)], [(1, D_MODEL)],
                                    name="rms_mix_bwd")

    rs_last = [n for n, _ in BIG if n not in rs_first]
    grads = [gw[n] for n in rs_last]
    sums_last = _rs_add(grads, _swap_sibling(grads, "rs_sibling_last"), pos, "rs_add_last")
    recvs_last = _run_exchange(_plan_swap_chips(sums_last), "rs_chips_last")
    big = rs_first + rs_last
    flat = _adamw_big(list(sums_first) + list(sums_last), list(recvs_first) + list(recvs_last),
                      [a[n] for n in big], [a["m_" + n] for n in big], [a["v_" + n] for n in big],
                      pos, "adamw_big")
    res = [{n: flat[4 * i + k] for i, n in enumerate(big)} for k in range(4)]

    part = jnp.concatenate(
        [_pack_small(gs)] + [_tiny_rows(gt[n], rows).reshape(N_DEV * rows, LANES) for n, _, rows in TINY], axis=0)
    (parts,) = _all_gather([part], "ag_small_grads")
    small_in = [_pack_small({n: a[p + n] for n, _ in SMALL}) for p in ("", "m_", "v_")]
    tiny_in = [[_tiny_rows(a[p + n][0], rows) for p in ("", "m_", "v_")] for n, _, rows in TINY]
    flat = _adamw_small(parts, small_in, tiny_in, "adamw_small")
    for k in range(4):
        res[k].update(_unpack_small(flat[k]))
        for i, (n, shape, _) in enumerate(TINY):
            res[k][n] = flat[4 * (i + 1) + k].reshape(-1)[:math.prod(shape)].reshape((1,) + shape)

    return (loss, dx.reshape(1, seq, D_MODEL), *[res[k][n] for k in range(4) for n in WEIGHTS])
```

```python
import functools
import math

import jax
import jax.numpy as jnp
from jax import lax
from jax.experimental import pallas as pl
from jax.experimental.pallas import tpu as pltpu

F32 = jnp.float32
BF16 = jnp.bfloat16
EPS = 1e-6
LANES = 128
N_DEV = 8
D_MODEL = 1024
CONV_CH = 512
CONV_WIDTH = 31
MLA_HEADS = 8
MLA_NOPE = 64
MLA_ROPE = 32
MLA_V = 64
MLA_QK = MLA_NOPE + MLA_ROPE
MLA_Q_RANK = 256
MLA_KV_RANK = 128
ROPE_THETA = 10000.0
IN_COLS = 2 * CONV_CH + MLA_Q_RANK + MLA_KV_RANK + MLA_ROPE
IN_PAD = 2 * CONV_CH + MLA_Q_RANK + MLA_KV_RANK + LANES
MEM_HEADS = 4
MEM_HEAD_DIM = 256
D_FF = 2816
FFN_CONV_WIDTH = 3
CHUNK = 64
ATT_SCALE = 1.0 / math.sqrt(MLA_QK)
LN2 = math.log(2.0)
Q_SCALE = ATT_SCALE / LN2
MEM_SCALE = 1.0 / math.sqrt(MEM_HEAD_DIM)
ADAM_LR, ADAM_B1, ADAM_B2, ADAM_EPS, ADAM_WD, ADAM_STEP = 0.001, 0.9, 0.999, 1e-08, 0.01, 10

TM = 512
MM_ROWS = 1024
TQ = 512
HEADS_PER_STEP = 2
FWD_HEADS = 4
CONV_ROWS = 256
NEG = -1e30
VMEM_LIMIT = 56 * 1024 * 1024

MESH = pl.DeviceIdType.MESH
ANY = pl.BlockSpec(memory_space=pl.ANY)
NT_DIMS = (((1,), (1,)), ((), ()))

BIG = [
    ("w_in", (1024, 180)), ("w_uq", (256, 96)), ("w_ukv", (128, 128)), ("w_out", (128, 1024)),
    ("w_mem_q", (128, 1024)), ("w_mem_kv", (1024, 256)), ("w_mem_o", (128, 1024)),
    ("w_up", (1024, 704)), ("w_down", (352, 1024)),
]
TINY = [("w_conv_dw", (31, 64), 16), ("w_ffn_dw", (3, 704), 24)]
ROW_STEPS = 4
FF_SHARD = D_FF // 4
FF_PAD = 768
D_FF_PAD = 4 * FF_PAD
SMALL = [
    ("mix_norm_g", 1024), ("b_conv_in", 1024), ("b_conv_dw", 512), ("conv_ln_g", 512),
    ("conv_ln_b", 512), ("q_lat_norm_g", 256), ("kv_lat_norm_g", 128), ("q_norm_g", 96),
    ("k_norm_g", 96), ("mem_norm_x_g", 1024), ("mem_norm_m_g", 1024), ("mem_q_norm_g", 256),
    ("mem_k_norm_g", 256), ("ffn_norm_g", 1024), ("b_ffn_dw", 5632),
]
WEIGHTS = [
    "mix_norm_g", "w_in", "b_conv_in", "w_conv_dw", "b_conv_dw", "conv_ln_g", "conv_ln_b",
    "q_lat_norm_g", "w_uq", "kv_lat_norm_g", "w_ukv", "q_norm_g", "k_norm_g", "w_out",
    "mem_norm_x_g", "mem_norm_m_g", "w_mem_q", "w_mem_kv", "mem_q_norm_g", "mem_k_norm_g",
    "w_mem_o", "ffn_norm_g", "w_up", "w_ffn_dw", "b_ffn_dw", "w_down",
]


SMALL_PAD = [(-(-n // LANES)) * LANES for _, n in SMALL]
SMALL_ROWS = -(-sum(SMALL_PAD) // (8 * LANES)) * 8
TINY_BASE = [SMALL_ROWS + N_DEV * sum(r for _, _, r in TINY[:i]) for i in range(len(TINY))]
PART_ROWS = SMALL_ROWS + N_DEV * sum(r for _, _, r in TINY)


def _call(body, **kw):
    return pl.pallas_call(body, **kw)


def _params(*sem):
    return pltpu.CompilerParams(dimension_semantics=sem, vmem_limit_bytes=VMEM_LIMIT)


class _Exchange:
    def __init__(self, inputs, out_shape, scratch, start, finish):
        self.inputs, self.out_shape, self.scratch = list(inputs), list(out_shape), list(scratch)
        self.start, self.finish = start, finish


def _run_exchange(ex, name):
    n_in, n_out = len(ex.inputs), len(ex.out_shape)

    def body(*refs):
        parts = refs[:n_in], refs[n_in:n_in + n_out], refs[n_in + n_out:]
        ex.start(*parts)
        ex.finish(*parts)

    return _call(body, name=name, out_shape=ex.out_shape, in_specs=[ANY] * n_in,
                 out_specs=[ANY] * n_out, scratch_shapes=ex.scratch)(*ex.inputs)


def _plan_all_gather(xs):
    n = len(xs)

    def copies(x_refs, out_refs, sems):
        send_sems, recv_sems, local_sems = sems
        x, y, c = lax.axis_index("x"), lax.axis_index("y"), lax.axis_index("c")
        me, sibling = (x, y, c), (x, y, 1 - c)
        chips = [(1 - x, y), (x, 1 - y), (1 - x, 1 - y)]

        def slot(o, px, py, pc):
            return out_refs[o].at[4 * px + 2 * py + pc]

        def copy(o, k, block, to, src=None):
            return pltpu.make_async_remote_copy(
                src_ref=slot(o, *block) if src is None else src, dst_ref=slot(o, *block),
                send_sem=send_sems.at[o, k], recv_sem=recv_sems.at[o, k],
                device_id=to, device_id_type=MESH)

        mine = [pltpu.make_async_copy(x_refs[o], slot(o, *me), local_sems.at[o]) for o in range(n)]
        first = [copy(o, 0, me, sibling, src=x_refs[o]) for o in range(n)]
        first += [copy(o, 1 + j, me, (*chip, c), src=x_refs[o])
                  for j, chip in enumerate(chips) for o in range(n)]
        return me, sibling, chips, copy, mine, first

    def start(x_refs, out_refs, sems):
        _, _, _, _, mine, first = copies(x_refs, out_refs, sems)
        for cp in mine + first:
            cp.start()

    def finish(x_refs, out_refs, sems):
        me, sibling, chips, copy, mine, first = copies(x_refs, out_refs, sems)
        c = me[2]
        passed = []
        for j, chip in enumerate(chips):
            for o in range(n):
                copy(o, 1 + j, (*chip, c), me).wait_recv()
                passed.append(copy(o, 4 + j, (*chip, c), sibling))
                passed[-1].start()
        for o in range(n):
            copy(o, 0, sibling, me).wait_recv()
        for j, chip in enumerate(chips):
            for o in range(n):
                copy(o, 4 + j, (*chip, 1 - c), me).wait_recv()
        for cp in first + passed:
            cp.wait_send()
        for cp in mine:
            cp.wait()

    return _Exchange(
        xs, [jax.ShapeDtypeStruct((N_DEV,) + v.shape, v.dtype) for v in xs],
        [pltpu.SemaphoreType.DMA((n, 7)), pltpu.SemaphoreType.DMA((n, 7)), pltpu.SemaphoreType.DMA((n,))],
        start, finish)


def _all_gather(xs, name):
    return _run_exchange(_plan_all_gather(xs), name)


def _swap_sibling(grads, name):
    n = len(grads)

    def body(*refs):
        g_refs, got_refs = refs[:n], refs[n:2 * n]
        send_sems, recv_sems = refs[2 * n:]
        x, y, c = lax.axis_index("x"), lax.axis_index("y"), lax.axis_index("c")
        copies = [
            pltpu.make_async_remote_copy(
                src_ref=g_refs[o].at[2 * chip + 1 - c], dst_ref=got_refs[o].at[chip],
                send_sem=send_sems.at[o, chip], recv_sem=recv_sems.at[o, chip],
                device_id=(x, y, 1 - c), device_id_type=MESH)
            for o in range(n) for chip in range(4)]
        for cp in copies:
            cp.start()
        for cp in copies:
            cp.wait()

    return _call(
        body, name=name,
        out_shape=[jax.ShapeDtypeStruct((4,) + g.shape[1:], g.dtype) for g in grads],
        in_specs=[ANY] * n, out_specs=[ANY] * n,
        scratch_shapes=[pltpu.SemaphoreType.DMA((n, 4)), pltpu.SemaphoreType.DMA((n, 4))],
    )(*grads)


def _plan_swap_chips(sums):
    n = len(sums)

    def copies(a_refs, r_refs, sems):
        send_sems, recv_sems = sems
        x, y, c = lax.axis_index("x"), lax.axis_index("y"), lax.axis_index("c")
        peers = [(x, 1 - y), (1 - x, y), (1 - x, 1 - y)]
        return [
            pltpu.make_async_remote_copy(
                src_ref=a_refs[o].at[2 * px + py], dst_ref=r_refs[o].at[k],
                send_sem=send_sems.at[o, k], recv_sem=recv_sems.at[o, k],
                device_id=(px, py, c), device_id_type=MESH)
            for k, (px, py) in enumerate(peers) for o in range(n)]

    def start(a_refs, r_refs, sems):
        for cp in copies(a_refs, r_refs, sems):
            cp.start()

    def finish(a_refs, r_refs, sems):
        for cp in copies(a_refs, r_refs, sems):
            cp.wait()

    return _Exchange(
        sums, [jax.ShapeDtypeStruct((3,) + a.shape[1:], a.dtype) for a in sums],
        [pltpu.SemaphoreType.DMA((n, 3)), pltpu.SemaphoreType.DMA((n, 3))], start, finish)


def _shard_block(shape):
    return (None, shape[-2] // ROW_STEPS, shape[-1])


def _rs_add(grads, gots, pos, name):
    n = len(grads)

    def body(pos_ref, *refs):
        for g_ref, t_ref, o_ref in zip(refs[:n], refs[n:2 * n], refs[2 * n:]):
            o_ref[...] = g_ref[...] + t_ref[...]

    in_specs = [pl.BlockSpec(_shard_block(g.shape), lambda a, t, pos: (2 * a + pos[1], t, 0)) for g in grads]
    in_specs += [pl.BlockSpec(_shard_block(g.shape), lambda a, t, pos: (a, t, 0)) for g in gots]
    return _call(
        body, name=name,
        grid_spec=pltpu.PrefetchScalarGridSpec(
            num_scalar_prefetch=1, grid=(4, ROW_STEPS), in_specs=in_specs,
            out_specs=[pl.BlockSpec(_shard_block(g.shape), lambda a, t, pos: (a, t, 0)) for g in gots]),
        out_shape=[jax.ShapeDtypeStruct(g.shape, g.dtype) for g in gots],
        compiler_params=_params("arbitrary", "arbitrary"),
    )(pos, *grads, *gots)


def _adamw_big(sums, recvs, ws, ms, vs, pos, name):
    n = len(sums)

    def body(pos_ref, *refs):
        ins, outs = refs[:7 * n], refs[7 * n:]
        for o in range(n):
            own, r1, r2, r3, w, m, v = [r[...] for r in ins[7 * o:7 * o + 7]]
            g = ((own + r1) + r2) + r3
            for ref, val in zip(outs[4 * o:4 * o + 4], (g,) + _adamw(w, g, m, v)):
                ref[...] = val

    in_specs, args, out_specs, out_shape = [], [], [], []
    for s_, r_, w_, m_, v_ in zip(sums, recvs, ws, ms, vs):
        blk = _shard_block(w_.shape)
        in_specs.append(pl.BlockSpec(blk, lambda t, pos: (pos[0], t, 0)))
        in_specs += [pl.BlockSpec(blk, lambda t, pos, k=k: (k, t, 0)) for k in range(3)]
        in_specs += [pl.BlockSpec(blk, lambda t, pos: (0, t, 0))] * 3
        args += [s_, r_, r_, r_, w_, m_, v_]
        out_specs += [pl.BlockSpec(blk, lambda t, pos: (0, t, 0))] * 4
        out_shape += [jax.ShapeDtypeStruct(w_.shape, F32)] * 4
    return _call(
        body, name=name,
        grid_spec=pltpu.PrefetchScalarGridSpec(
            num_scalar_prefetch=1, grid=(ROW_STEPS,), in_specs=in_specs, out_specs=out_specs),
        out_shape=out_shape, compiler_params=_params("arbitrary"),
    )(pos, *args)


def _tile(n, prefs):
    for t in prefs:
        if n % t == 0:
            return t
    return n


def _mm(a, b, *, name, trans_b=False, add=None, out_dtype=F32, tm=MM_ROWS, b_shards=None):
    m, k = a.shape
    if b_shards is None:
        n = b.shape[0] if trans_b else b.shape[1]
        tn = _tile(n, (1536, 1408, 1024, 768, 512, 256, 128))
        tk = _tile(k, (1408, 1024, 768, 512, 256, 128))
    elif trans_b:
        n, tn, tk = b.shape[1], b.shape[1], b.shape[2]
    else:
        n, tn, tk = b_shards[1] * b.shape[2], b.shape[2], _tile(k, (1024, 512))
    nk = k // tk
    has_add = add is not None

    def body(*refs):
        a_ref, b_ref = refs[0], refs[1]
        add_ref = refs[2] if has_add else None
        o_ref = refs[2 + has_add]
        av = a_ref[...].astype(BF16)
        bv = b_ref[...].astype(BF16)
        if trans_b:
            part = lax.dot_general(av, bv, NT_DIMS, preferred_element_type=F32)
        else:
            part = jnp.dot(av, bv, preferred_element_type=F32)

        def finish(acc):
            if has_add:
                acc = acc + add_ref[...].astype(F32)
            o_ref[...] = acc.astype(o_ref.dtype)

        if nk == 1:
            finish(part)
        else:
            acc_ref = refs[3 + has_add]
            kk = pl.program_id(2)

            @pl.when(kk == 0)
            def _():
                acc_ref[...] = part

            @pl.when(kk > 0)
            def _():
                acc_ref[...] += part

            @pl.when(kk == nk - 1)
            def _():
                finish(acc_ref[...])

    in_specs = [pl.BlockSpec((tm, tk), lambda i, j, kk: (i, kk))]
    if b_shards is not None and trans_b:
        in_specs.append(pl.BlockSpec((None, tn, tk), lambda i, j, kk: (b_shards[0] + kk, j, 0)))
    elif b_shards is not None:
        in_specs.append(pl.BlockSpec((None, tk, tn), lambda i, j, kk: (b_shards[0] + j, kk, 0)))
    elif trans_b:
        in_specs.append(pl.BlockSpec((tn, tk), lambda i, j, kk: (j, kk)))
    else:
        in_specs.append(pl.BlockSpec((tk, tn), lambda i, j, kk: (kk, j)))
    args = [a, b]
    if has_add:
        in_specs.append(pl.BlockSpec((tm, tn), lambda i, j, kk: (i, j)))
        args.append(add)
    return _call(
        body, name=name, grid=(m // tm, n // tn, nk), in_specs=in_specs,
        out_specs=pl.BlockSpec((tm, tn), lambda i, j, kk: (i, j)),
        out_shape=jax.ShapeDtypeStruct((m, n), out_dtype),
        scratch_shapes=[pltpu.VMEM((tm, tn), F32)] if nk > 1 else [],
        compiler_params=_params("parallel", "parallel", "arbitrary"),
    )(*args)


def _mm_tn(a, b, *, name, ts=MM_ROWS, shard_cols=None):
    s, m = a.shape
    n = b.shape[1]
    tm = _tile(m, (1408, 1024, 768, 512, 256, 128))
    tn = shard_cols or _tile(n, (1536, 1408, 1024, 768, 512, 256, 128))
    if shard_cols:
        out_spec = pl.BlockSpec((None, tm, tn), lambda i, j, kk: (j, i, 0))
        out_shape = jax.ShapeDtypeStruct((n // tn, m, tn), F32)
    else:
        out_spec = pl.BlockSpec((tm, tn), lambda i, j, kk: (i, j))
        out_shape = jax.ShapeDtypeStruct((m, n), F32)

    def body(a_ref, b_ref, o_ref):
        kk = pl.program_id(2)
        part = jnp.dot(a_ref[...].astype(BF16).T, b_ref[...].astype(BF16),
                       preferred_element_type=F32)

        @pl.when(kk == 0)
        def _():
            o_ref[...] = part

        @pl.when(kk > 0)
        def _():
            o_ref[...] += part

    return _call(
        body, name=name, grid=(m // tm, n // tn, s // ts),
        in_specs=[pl.BlockSpec((ts, tm), lambda i, j, kk: (kk, i)),
                  pl.BlockSpec((ts, tn), lambda i, j, kk: (kk, j))],
        out_specs=out_spec, out_shape=out_shape,
        compiler_params=_params("parallel", "parallel", "arbitrary"),
    )(a, b)


def _rowwise(fn, rows, consts, row_outs, acc_outs, *, name, tm=TM, n_rows=None):
    rows = [r if isinstance(r, tuple) else (r, r.shape[1], 0, 0) for r in rows]
    s = n_rows or rows[0][0].shape[0]
    nr, nc, no, na = len(rows), len(consts), len(row_outs), len(acc_outs)

    def body(*refs):
        r_in, c_in = refs[:nr], refs[nr:nr + nc]
        o_refs, a_refs = refs[nr + nc:nr + nc + no], refs[nr + nc + no:]
        outs = fn(*[r[...] for r in r_in], *[c[...] for c in c_in])
        for r, v in zip(o_refs, outs[:no]):
            r[...] = v.astype(r.dtype)
        if na:
            i = pl.program_id(0)

            @pl.when(i == 0)
            def _():
                for r, v in zip(a_refs, outs[no:]):
                    r[...] = v.astype(F32)

            @pl.when(i > 0)
            def _():
                for r, v in zip(a_refs, outs[no:]):
                    r[...] += v.astype(F32)

    in_specs = [r[1] if isinstance(r[1], pl.BlockSpec) else
                pl.BlockSpec((tm, r[1]), lambda i, cb=r[2], rb=r[3]: (i + rb, cb)) for r in rows]
    in_specs += [pl.BlockSpec(c.shape, lambda i: (0, 0)) for c in consts]
    out_specs = [pl.BlockSpec((tm, w), lambda i: (i, 0)) for w, _ in row_outs]
    out_specs += [pl.BlockSpec(sh, lambda i: (0, 0)) for sh in acc_outs]
    out_shape = [jax.ShapeDtypeStruct((s, w), dt) for w, dt in row_outs]
    out_shape += [jax.ShapeDtypeStruct(sh, F32) for sh in acc_outs]
    return _call(
        body, name=name, grid=(s // tm,), in_specs=in_specs, out_specs=out_specs,
        out_shape=out_shape, compiler_params=_params("arbitrary"),
    )(*[r[0] for r in rows], *consts)


def _rms(x, g, n=None):
    ms = jnp.sum(x * x, axis=-1, keepdims=True) / float(n or x.shape[-1])
    return x * lax.rsqrt(ms + EPS) * g


def _layer_norm(x, g, b):
    mu = jnp.sum(x, axis=-1, keepdims=True) / float(x.shape[-1])
    xc = x - mu
    var = jnp.sum(xc * xc, axis=-1, keepdims=True) / float(x.shape[-1])
    return xc * lax.rsqrt(var + EPS) * g + b


def _silu(x):
    return x * jax.nn.sigmoid(x)


@jax.custom_vjp
def _rope(y, cos, sin_a, sin_b):
    return y * cos + pltpu.roll(y, 112, 1) * sin_a + pltpu.roll(y, 16, 1) * sin_b


def _rope_fwd(y, cos, sin_a, sin_b):
    return _rope(y, cos, sin_a, sin_b), (cos, sin_a, sin_b)


def _rope_bwd(res, ct):
    cos, sin_a, sin_b = res
    dy = ct * cos + pltpu.roll(ct * sin_a, 16, 1) + pltpu.roll(ct * sin_b, 112, 1)
    return dy, jnp.zeros_like(cos), jnp.zeros_like(sin_a), jnp.zeros_like(sin_b)


_rope.defvjp(_rope_fwd, _rope_bwd)


def _qk_head(xh, g, cos, sin_a, sin_b):
    return _rope(_rms(xh, g, MLA_QK), cos, sin_a, sin_b)


def _heads(x, width):
    return [x[:, h * width:(h + 1) * width] for h in range(x.shape[1] // width)]


def _f_rms(x, g):
    return (_rms(x, g),)


def _f_rope_tab(pos, inv_freq):
    ang = pos.astype(F32) * inv_freq
    lane = lax.broadcasted_iota(jnp.int32, ang.shape, 1)
    sn = jnp.sin(ang)
    first = (lane >= MLA_NOPE) & (lane < MLA_NOPE + MLA_ROPE // 2)
    second = (lane >= MLA_NOPE + MLA_ROPE // 2) & (lane < MLA_QK)
    return jnp.cos(ang), jnp.where(first, -sn, 0.0), jnp.where(second, sn, 0.0)


def _mix_pre(za, zg, zcq, zckv, ba, bg, gq, gkv):
    u0 = (za + ba) * jax.nn.sigmoid(zg + bg)
    return u0, _rms(zcq, gq), _rms(zckv, gkv)


def _ln_silu(c1, bdw, lg, lb):
    return _silu(_layer_norm(c1 + bdw, lg, lb))


def _f_qk_prep(q0, kn, kr, cos, sa, sb, gq, gk):
    qs = [_qk_head(xh, gq, cos, sa, sb) * Q_SCALE for xh in _heads(q0, LANES)]
    ks = [_qk_head(xh + kr, gk, cos, sa, sb) for xh in _heads(kn, LANES)]
    return jnp.concatenate(qs, axis=1), jnp.concatenate(ks, axis=1)


def _act(cg, cv, bg, bv):
    return _silu(cg + bg) * (cv + bv)


def _f_mem_k(kk, g):
    return (jnp.concatenate([_rms(xh, g) for xh in _heads(kk, MEM_HEAD_DIM)], axis=1),)


def _mem_probs(qn, kmh):
    s = lax.dot_general(qn.astype(BF16), kmh, NT_DIMS, preferred_element_type=F32) * MEM_SCALE
    e = jnp.exp(s - jnp.max(s, axis=-1, keepdims=True))
    return e / jnp.sum(e, axis=-1, keepdims=True)


def _f_mem_attn(qm0, km, vm, g):
    outs = []
    for h, xh in enumerate(_heads(qm0, MEM_HEAD_DIM)):
        sl = slice(h * MEM_HEAD_DIM, (h + 1) * MEM_HEAD_DIM)
        p = _mem_probs(_rms(xh, g), km[:, sl])
        outs.append(jnp.dot(p.astype(BF16), vm[:, sl].astype(BF16), preferred_element_type=F32))
    return (jnp.concatenate(outs, axis=1),)


def _f_loss(y, t):
    e = y - t
    return e * (1.0 / D_MODEL), jnp.sum(e * e, axis=0, keepdims=True)


def _b_rms(x, dh, dres, g):
    _, vjp = jax.vjp(_rms, x, g)
    dx, dg = vjp(dh)
    return dx + dres, dg


def _b_rms_nores(x, dh, g):
    _, vjp = jax.vjp(_rms, x, g)
    dx, dg = vjp(dh)
    return dx, dg


def _b_mix_pre(za, zg, zcq, zckv, du0, dcqn, dckvn, dkr, ba, bg, gq, gkv):
    _, vjp = jax.vjp(_mix_pre, za, zg, zcq, zckv, ba, bg, gq, gkv)
    dza, dzg, dzcq, dzckv, dba, dbg, dgq, dgkv = vjp((du0, dcqn, dckvn))
    return jnp.concatenate([dza, dzg, dzcq, dzckv, dkr], axis=1), dba, dbg, dgq, dgkv


def _b_ln_silu(c1, du, bdw, lg, lb):
    _, vjp = jax.vjp(_ln_silu, c1, bdw, lg, lb)
    return vjp(du)


def _b_qk_prep(q0, kn, kr, cos, sa, sb, dq, dk, gq, gk):
    head = lambda xh, g: _qk_head(xh, g, cos, sa, sb)
    dk = dk * LN2
    dq0, dkn = [], []
    dkr = jnp.zeros_like(kr)
    dgq = jnp.zeros_like(gq)
    dgk = jnp.zeros_like(gk)
    for h, xh in enumerate(_heads(q0, LANES)):
        _, vjp = jax.vjp(head, xh, gq)
        dx, dg = vjp(dq[h, 0].T * ATT_SCALE)
        dq0.append(dx)
        dgq = dgq + dg
    for xh, ct in zip(_heads(kn, LANES), _heads(dk, LANES)):
        _, vjp = jax.vjp(head, xh + kr, gk)
        dx, dg = vjp(ct)
        dkn.append(dx)
        dkr = dkr + dx
        dgk = dgk + dg
    return jnp.concatenate(dq0, axis=1), jnp.concatenate(dkn, axis=1), dkr, dgq, dgk


def _b_mem_k(kk, dkm, g):
    dkk = []
    dg = jnp.zeros_like(g)
    for xh, ct in zip(_heads(kk, MEM_HEAD_DIM), _heads(dkm, MEM_HEAD_DIM)):
        _, vjp = jax.vjp(_rms, xh, g)
        dx, dgh = vjp(ct)
        dkk.append(dx)
        dg = dg + dgh
    return jnp.concatenate(dkk, axis=1), dg


def _b_mem_attn(dom, qm0, km, vm, g):
    dq0, dkm, dvm = [], [], []
    dg = jnp.zeros_like(g)
    for h, (xh, doh) in enumerate(zip(_heads(qm0, MEM_HEAD_DIM), _heads(dom, MEM_HEAD_DIM))):
        sl = slice(h * MEM_HEAD_DIM, (h + 1) * MEM_HEAD_DIM)
        kmh, vmh = km[:, sl], vm[:, sl].astype(BF16)
        qn, vjp = jax.vjp(_rms, xh, g)
        p = _mem_probs(qn, kmh)
        dob = doh.astype(BF16)
        dp = lax.dot_general(dob, vmh, NT_DIMS, preferred_element_type=F32)
        ds = (p * (dp - jnp.sum(dp * p, axis=-1, keepdims=True)) * MEM_SCALE).astype(BF16)
        dqn = jnp.dot(ds, kmh, preferred_element_type=F32)
        dkm.append(jnp.dot(ds.T, qn.astype(BF16), preferred_element_type=F32))
        dvm.append(jnp.dot(p.astype(BF16).T, dob, preferred_element_type=F32))
        dx, dgh = vjp(dqn)
        dq0.append(dx)
        dg = dg + dgh
    return (jnp.concatenate(dq0, axis=1), jnp.concatenate(dkm, axis=1),
            jnp.concatenate(dvm, axis=1), dg)


def _adamw(w, g, m, v):
    m = ADAM_B1 * m + (1.0 - ADAM_B1) * g
    v = ADAM_B2 * v + (1.0 - ADAM_B2) * jnp.square(g)
    m_hat = m / (1.0 - ADAM_B1 ** ADAM_STEP)
    v_hat = v / (1.0 - ADAM_B2 ** ADAM_STEP)
    delta = -ADAM_LR * (m_hat / (jnp.sqrt(v_hat) + ADAM_EPS) + ADAM_WD * w)
    return delta, m, v


def _adamw_small(parts, small, tiny, name):
    def body(*refs):
        p_ref, ins, outs = refs[0], refs[1:4 + 3 * len(TINY)], refs[4 + 3 * len(TINY):]
        me = 4 * lax.axis_index("x") + 2 * lax.axis_index("y") + lax.axis_index("c")
        groups = [(0, SMALL_ROWS)]
        groups += [(pl.multiple_of(base + me * rows, 8), rows) for base, (_, _, rows) in zip(TINY_BASE, TINY)]
        for k, (start, rows) in enumerate(groups):
            g = p_ref[0, pl.ds(start, rows), :]
            for d in range(1, N_DEV):
                g = g + p_ref[d, pl.ds(start, rows), :]
            w, m, v = [r[...] for r in ins[3 * k:3 * k + 3]]
            for ref, val in zip(outs[4 * k:4 * k + 4], (g,) + _adamw(w, g, m, v)):
                ref[...] = val

    args = list(small) + [t for grp in tiny for t in grp]
    out_shape = []
    for grp in [small] + list(tiny):
        out_shape += [jax.ShapeDtypeStruct(grp[0].shape, F32)] * 4
    return _call(body, name=name, out_shape=out_shape)(parts, *args)


def _conv_fwd(x, w, name):
    s, ch = x.shape
    kw = w.shape[0]
    halo = -(-(kw - 1) // 8) * 8
    r = CONV_ROWS
    n = s // r

    def chunk(window, wv):
        acc = jnp.zeros((r, LANES), F32)
        for k in range(kw):
            shift = kw - 1 - k
            sh = window if shift == 0 else pltpu.roll(window, shift, 0)
            acc = acc + sh[halo:halo + r] * wv[k:k + 1]
        return acc

    def body(x_ref, w_ref, y_ref):
        wv = w_ref[...]
        first = jnp.concatenate([jnp.zeros((halo, LANES), F32), x_ref[0:r]], axis=0)
        y_ref[0:r] = chunk(first, wv)

        def step(i, carry):
            base = pl.multiple_of(i * r, 8)
            y_ref[pl.ds(base, r)] = chunk(x_ref[pl.ds(base - halo, r + halo)], wv)
            return carry

        lax.fori_loop(1, n, step, 0)

    return _call(
        body, name=name, grid=(ch // LANES,),
        in_specs=[pl.BlockSpec((s, LANES), lambda c: (0, c)), pl.BlockSpec((kw, LANES), lambda c: (0, c))],
        out_specs=pl.BlockSpec((s, LANES), lambda c: (0, c)),
        out_shape=jax.ShapeDtypeStruct((s, ch), F32), compiler_params=_params("parallel"),
    )(x, w)


def _conv_bwd(dy, x, w, name):
    s, ch = x.shape
    kw = w.shape[0]
    halo = -(-(kw - 1) // 8) * 8
    r = CONV_ROWS
    n = s // r

    def dx_chunk(window, wv):
        acc = jnp.zeros((r, LANES), F32)
        for k in range(kw):
            shift = kw - 1 - k
            sh = window if shift == 0 else pltpu.roll(window, r + halo - shift, 0)
            acc = acc + sh[0:r] * wv[k:k + 1]
        return acc

    def dw_chunk(xwin, dyc, acc_ref):
        for k in range(kw):
            shift = kw - 1 - k
            sh = xwin if shift == 0 else pltpu.roll(xwin, shift, 0)
            prod = sh[halo:halo + r] * dyc
            acc_ref[k] += jnp.sum(prod.reshape(r // 8, 8, LANES), axis=0)

    def body(dy_ref, x_ref, w_ref, dx_ref, dw_ref, acc_ref):
        wv = w_ref[...]
        acc_ref[...] = jnp.zeros_like(acc_ref)
        xfirst = jnp.concatenate([jnp.zeros((halo, LANES), F32), x_ref[0:r]], axis=0)
        dw_chunk(xfirst, dy_ref[0:r], acc_ref)
        last = jnp.concatenate([dy_ref[s - r:s], jnp.zeros((halo, LANES), F32)], axis=0)
        dx_ref[s - r:s] = dx_chunk(last, wv)

        def step(i, carry):
            base = pl.multiple_of(i * r, 8)
            dw_chunk(x_ref[pl.ds(base - halo, r + halo)], dy_ref[pl.ds(base, r)], acc_ref)
            prev = pl.multiple_of((i - 1) * r, 8)
            dx_ref[pl.ds(prev, r)] = dx_chunk(dy_ref[pl.ds(prev, r + halo)], wv)
            return carry

        lax.fori_loop(1, n, step, 0)
        dw_ref[...] = jnp.sum(acc_ref[...], axis=1)

    spec = pl.BlockSpec((s, LANES), lambda c: (0, c))
    wspec = pl.BlockSpec((kw, LANES), lambda c: (0, c))
    return _call(
        body, name=name, grid=(ch // LANES,), in_specs=[spec, spec, wspec], out_specs=[spec, wspec],
        out_shape=[jax.ShapeDtypeStruct((s, ch), F32), jax.ShapeDtypeStruct((kw, ch), F32)],
        scratch_shapes=[pltpu.VMEM((kw, 8, LANES), F32)], compiler_params=_params("parallel"),
    )(dy, x, w)


HALO = 8


def _conv3(win, w, rows):
    return (pltpu.roll(win, 2, 0)[HALO:HALO + rows] * w[0:1] + pltpu.roll(win, 1, 0)[HALO:HALO + rows] * w[1:2]
            + win[HALO:HALO + rows] * w[2:3])


def _ffn_mid_bwd(up_g, up_v, dact, w_g, w_v, b_g, b_v, name):
    s, width = up_g.shape
    tm, tc = TM, FF_PAD
    n_row = s // tm

    def body(pg_ref, g_ref, ng_ref, pv_ref, v_ref, nv_ref, d_ref, nd_ref, wg_ref, wv_ref, bg_ref, bv_ref,
             dug_ref, duv_ref, dwg_ref, dwv_ref, dbg_ref, dbv_ref):
        i = pl.program_id(1)
        first = (i > 0).astype(F32)
        last = (i < n_row - 1).astype(F32)
        ext = tm + HALO
        wg, wv = wg_ref[...], wv_ref[...]
        xg = jnp.concatenate([pg_ref[...] * first, g_ref[...], ng_ref[...]], axis=0)
        xv = jnp.concatenate([pv_ref[...] * first, v_ref[...], nv_ref[...]], axis=0)
        d_ext = jnp.concatenate([d_ref[...], nd_ref[...] * last], axis=0)
        _, vjp = jax.vjp(lambda cg_, cv_: _act(cg_, cv_, bg_ref[...], bv_ref[...]),
                         _conv3(xg, wg, ext), _conv3(xv, wv, ext))
        dcg, dcv = vjp(d_ext)
        results = []
        for x, w, dc in ((xg, wg, dcg), (xv, wv, dcv)):
            dup = (dc[:tm] * w[2:3] + pltpu.roll(dc, ext - 1, 0)[:tm] * w[1:2]
                   + pltpu.roll(dc, ext - 2, 0)[:tm] * w[0:1])
            own = dc[:tm]
            dw = jnp.concatenate([
                jnp.sum(own * pltpu.roll(x, 2, 0)[HALO:HALO + tm], axis=0, keepdims=True),
                jnp.sum(own * pltpu.roll(x, 1, 0)[HALO:HALO + tm], axis=0, keepdims=True),
                jnp.sum(own * x[HALO:HALO + tm], axis=0, keepdims=True)], axis=0)
            results.append((dup, dw, jnp.sum(own, axis=0, keepdims=True)))
        (dug, dwg, dbg), (duv, dwv, dbv) = results
        dug_ref[...] = dug.astype(dug_ref.dtype)
        duv_ref[...] = duv.astype(duv_ref.dtype)

        @pl.when(i == 0)
        def _():
            dwg_ref[...], dwv_ref[...], dbg_ref[...], dbv_ref[...] = dwg, dwv, dbg, dbv

        @pl.when(i > 0)
        def _():
            dwg_ref[...] += dwg
            dwv_ref[...] += dwv
            dbg_ref[...] += dbg
            dbv_ref[...] += dbv

    per = tm // HALO
    tile = pl.BlockSpec((tm, tc), lambda c, i: (i, c))
    prev = pl.BlockSpec((HALO, tc), lambda c, i: (jnp.maximum(i * per - 1, 0), c))
    nxt = pl.BlockSpec((HALO, tc), lambda c, i: (jnp.minimum((i + 1) * per, s // HALO - 1), c))
    wspec = pl.BlockSpec((FFN_CONV_WIDTH, tc), lambda c, i: (0, c))
    bspec = pl.BlockSpec((1, tc), lambda c, i: (0, c))
    wide = jax.ShapeDtypeStruct((s, width), BF16)
    return _call(
        body, name=name, grid=(width // tc, n_row),
        in_specs=[prev, tile, nxt, prev, tile, nxt, tile, nxt, wspec, wspec, bspec, bspec],
        out_specs=[tile, tile, wspec, wspec, bspec, bspec],
        out_shape=[wide, wide] + [jax.ShapeDtypeStruct((FFN_CONV_WIDTH, width), F32)] * 2
        + [jax.ShapeDtypeStruct((1, width), F32)] * 2,
        compiler_params=_params("parallel", "arbitrary"),
    )(up_g, up_g, up_g, up_v, up_v, up_v, dact, dact, w_g, w_v, b_g, b_v)


def _ffn_fwd(h3, w_up, w_g, w_v, b_g, b_v, w_dn, x2, target, name):
    s = h3.shape[0]
    tm, tc = TM, FF_PAD
    n_col = D_FF_PAD // tc

    def body(h_ref, wug_ref, wuv_ref, wg_ref, wv_ref, bg_ref, bv_ref, wd_ref, x_ref, t_ref,
             ug_ref, uv_ref, act_ref, dy_ref, sq_ref, halo_g, halo_v, y_acc):
        i, c = pl.program_id(0), pl.program_id(1)
        h = h_ref[...]
        up_g = jnp.dot(h, wug_ref[...], preferred_element_type=F32)
        up_v = jnp.dot(h, wuv_ref[...], preferred_element_type=F32)
        ug_ref[...] = up_g
        uv_ref[...] = up_v
        has_prev = i > 0
        prev_g = jnp.where(has_prev, halo_g[c], 0.0)
        prev_v = jnp.where(has_prev, halo_v[c], 0.0)
        halo_g[c] = up_g[tm - HALO:]
        halo_v[c] = up_v[tm - HALO:]
        cg = _conv3(jnp.concatenate([prev_g, up_g], axis=0), wg_ref[...], tm)
        cv = _conv3(jnp.concatenate([prev_v, up_v], axis=0), wv_ref[...], tm)
        act = _act(cg, cv, bg_ref[...], bv_ref[...]).astype(BF16)
        act_ref[...] = act
        part = jnp.dot(act, wd_ref[...], preferred_element_type=F32)

        @pl.when(c == 0)
        def _():
            y_acc[...] = part

        @pl.when(c > 0)
        def _():
            y_acc[...] += part

        @pl.when(c == n_col - 1)
        def _():
            dy, sq = _f_loss(x_ref[...] + y_acc[...], t_ref[...])
            dy_ref[...] = dy

            @pl.when(i == 0)
            def _():
                sq_ref[...] = sq

            @pl.when(i > 0)
            def _():
                sq_ref[...] += sq

    row = lambda w: pl.BlockSpec((tm, w), lambda i, c: (i, 0))
    tile = pl.BlockSpec((tm, tc), lambda i, c: (i, c))
    wspec = pl.BlockSpec((FFN_CONV_WIDTH, tc), lambda i, c: (0, c))
    bspec = pl.BlockSpec((1, tc), lambda i, c: (0, c))
    wide = jax.ShapeDtypeStruct((s, D_FF_PAD), F32)
    return _call(
        body, name=name, grid=(s // tm, n_col),
        in_specs=[row(D_MODEL),
                  pl.BlockSpec((None, D_MODEL, tc), lambda i, c: (c, 0, 0)),
                  pl.BlockSpec((None, D_MODEL, tc), lambda i, c: (n_col + c, 0, 0)),
                  wspec, wspec, bspec, bspec, pl.BlockSpec((tc, D_MODEL), lambda i, c: (c, 0)),
                  row(D_MODEL), row(D_MODEL)],
        out_specs=[tile, tile, tile, row(D_MODEL), pl.BlockSpec((1, D_MODEL), lambda i, c: (0, 0))],
        out_shape=[wide, wide, jax.ShapeDtypeStruct((s, D_FF_PAD), BF16),
                   jax.ShapeDtypeStruct((s, D_MODEL), F32), jax.ShapeDtypeStruct((1, D_MODEL), F32)],
        scratch_shapes=[pltpu.VMEM((n_col, HALO, tc), F32), pltpu.VMEM((n_col, HALO, tc), F32),
                        pltpu.VMEM((tm, D_MODEL), F32)],
        compiler_params=_params("arbitrary", "arbitrary"),
    )(h3, w_up, w_up, w_g, w_v, b_g, b_v, w_dn, x2, target)


def _ffn_bwd(h3, up_g, up_v, dy, w_g, w_v, b_g, b_v, w_dn, name):
    s = h3.shape[0]
    tm, tc = 256, FF_PAD
    n_row, n_col = s // tm, D_FF_PAD // tc
    ext = tm + HALO

    def body(h_ref, dy_ref, ndy_ref, wd_ref, pg_ref, g_ref, ng_ref, pv_ref, v_ref, nv_ref,
             wg_ref, wv_ref, bg_ref, bv_ref,
             dug_ref, duv_ref, dwg_ref, dwv_ref, dbg_ref, dbv_ref, gdn_ref, gug_ref, guv_ref):
        i = pl.program_id(1)
        first = (i > 0).astype(F32)
        last = (i < n_row - 1).astype(F32)
        wg, wv, wd = wg_ref[...], wv_ref[...], wd_ref[...]
        dyb = dy_ref[...].astype(BF16)
        ndy = jnp.concatenate([ndy_ref[...] * last, jnp.zeros((HALO, D_MODEL), F32)], axis=0).astype(BF16)
        d_ext = jnp.concatenate([
            lax.dot_general(dyb, wd, NT_DIMS, preferred_element_type=F32),
            lax.dot_general(ndy, wd, NT_DIMS, preferred_element_type=F32)[:HALO]], axis=0)
        xg = jnp.concatenate([pg_ref[...] * first, g_ref[...], ng_ref[...]], axis=0)
        xv = jnp.concatenate([pv_ref[...] * first, v_ref[...], nv_ref[...]], axis=0)
        act_fn = lambda cg_, cv_: _act(cg_, cv_, bg_ref[...], bv_ref[...])
        act_ext, vjp = jax.vjp(act_fn, _conv3(xg, wg, ext), _conv3(xv, wv, ext))
        dcg, dcv = vjp(d_ext)
        results = []
        for x, w, dc in ((xg, wg, dcg), (xv, wv, dcv)):
            dup = (dc[:tm] * w[2:3] + pltpu.roll(dc, ext - 1, 0)[:tm] * w[1:2]
                   + pltpu.roll(dc, ext - 2, 0)[:tm] * w[0:1])
            own = dc[:tm]
            dw = jnp.concatenate([
                jnp.sum(own * pltpu.roll(x, 2, 0)[HALO:HALO + tm], axis=0, keepdims=True),
                jnp.sum(own * pltpu.roll(x, 1, 0)[HALO:HALO + tm], axis=0, keepdims=True),
                jnp.sum(own * x[HALO:HALO + tm], axis=0, keepdims=True)], axis=0)
            results.append((dup.astype(BF16), dw, jnp.sum(own, axis=0, keepdims=True)))
        (dug, dwg, dbg), (duv, dwv, dbv) = results
        dug_ref[...] = dug
        duv_ref[...] = duv
        ht = h_ref[...].T
        sums = [
            (dwg_ref, dwg), (dwv_ref, dwv), (dbg_ref, dbg), (dbv_ref, dbv),
            (gdn_ref, jnp.dot(act_ext[:tm].astype(BF16).T, dyb, preferred_element_type=F32)),
            (gug_ref, jnp.dot(ht, dug, preferred_element_type=F32)),
            (guv_ref, jnp.dot(ht, duv, preferred_element_type=F32))]

        @pl.when(i == 0)
        def _():
            for ref, val in sums:
                ref[...] = val

        @pl.when(i > 0)
        def _():
            for ref, val in sums:
                ref[...] += val

    per = tm // HALO
    row = lambda w: pl.BlockSpec((tm, w), lambda c, i: (i, 0))
    tile = pl.BlockSpec((tm, tc), lambda c, i: (i, c))
    prev = pl.BlockSpec((HALO, tc), lambda c, i: (jnp.maximum(i * per - 1, 0), c))
    nxt = pl.BlockSpec((HALO, tc), lambda c, i: (jnp.minimum((i + 1) * per, s // HALO - 1), c))
    ndy = pl.BlockSpec((HALO, D_MODEL), lambda c, i: (jnp.minimum((i + 1) * per, s // HALO - 1), 0))
    wspec = pl.BlockSpec((FFN_CONV_WIDTH, tc), lambda c, i: (0, c))
    bspec = pl.BlockSpec((1, tc), lambda c, i: (0, c))
    dn_spec = pl.BlockSpec((tc, D_MODEL), lambda c, i: (c, 0))
    up_spec = pl.BlockSpec((None, D_MODEL, tc), lambda c, i: (c, 0, 0))
    wide = jax.ShapeDtypeStruct((s, D_FF_PAD), BF16)
    small = lambda r: jax.ShapeDtypeStruct((r, D_FF_PAD), F32)
    return _call(
        body, name=name, grid=(n_col, n_row),
        in_specs=[row(D_MODEL), row(D_MODEL), ndy, dn_spec, prev, tile, nxt, prev, tile, nxt,
                  wspec, wspec, bspec, bspec],
        out_specs=[tile, tile, wspec, wspec, bspec, bspec, dn_spec, up_spec, up_spec],
        out_shape=[wide, wide, small(FFN_CONV_WIDTH), small(FFN_CONV_WIDTH), small(1), small(1),
                   jax.ShapeDtypeStruct((D_FF_PAD, D_MODEL), F32),
                   jax.ShapeDtypeStruct((n_col, D_MODEL, tc), F32), jax.ShapeDtypeStruct((n_col, D_MODEL, tc), F32)],
        compiler_params=_params("arbitrary", "arbitrary"),
    )(h3, dy, dy, w_dn, up_g, up_g, up_g, up_v, up_v, up_v, w_g, w_v, b_g, b_v)


def _chunk_mask(rows_are_queries):
    a = lax.broadcasted_iota(jnp.int32, (TQ, TQ), 0) // CHUNK
    b = lax.broadcasted_iota(jnp.int32, (TQ, TQ), 1) // CHUNK
    return (b <= a) if rows_are_queries else (a <= b)


def _head_lanes(hh):
    return slice(hh * LANES, (hh + 1) * LANES)


def _to_row(col):
    return jnp.broadcast_to(col, (TQ, LANES)).T[0:1, :]


def _flash_specs(s, heads=HEADS_PER_STEP):
    width = heads * LANES
    tile = pl.BlockSpec((TQ, width), lambda h, i: (i, h))
    whole = pl.BlockSpec((s, width), lambda h, i: (0, h))
    row_tile = pl.BlockSpec((heads, 1, 1, TQ), lambda h, i: (h, i, 0, 0))
    row_whole = pl.BlockSpec((heads, s // TQ, 1, TQ), lambda h, i: (h, 0, 0, 0))
    return tile, whole, row_tile, row_whole


def _split_refs(refs, n_in, n_out, ex):
    e_in, e_out = (len(ex.inputs), len(ex.out_shape)) if ex else (0, 0)
    a, b, c = n_in + e_in, n_in + e_in + n_out, n_in + e_in + n_out + e_out
    return refs[:n_in], refs[a:b], (refs[n_in:a], refs[b:c], refs[c:])


def _hosted(ex, ex_refs, grid, when_first):
    if ex is None:
        return
    ids = [pl.program_id(d) for d in range(len(grid))]
    cond = functools.reduce(
        lambda p, q_: p & q_, [i == (0 if when_first else g - 1) for i, g in zip(ids, grid)])

    @pl.when(cond)
    def _():
        (ex.start if when_first else ex.finish)(*ex_refs)


def _host_call(body, ex, name, grid, in_specs, out_specs, out_shape, args):
    e_in, e_out = (len(ex.inputs), len(ex.out_shape)) if ex else (0, 0)
    res = _call(
        body, name=name, grid=grid, in_specs=list(in_specs) + [ANY] * e_in,
        out_specs=list(out_specs) + [ANY] * e_out,
        out_shape=list(out_shape) + (ex.out_shape if ex else []),
        scratch_shapes=ex.scratch if ex else [],
        compiler_params=_params(*["arbitrary"] * len(grid)),
    )(*args, *(ex.inputs if ex else []))
    return res[:len(out_shape)], res[len(out_shape):]


def _flash_fwd(q, k, v, name, ex=None):
    s = q.shape[0]
    nq = s // TQ
    heads = FWD_HEADS
    grid = (MLA_HEADS // heads, nq)

    def body(*refs):
        (q_ref, k_ref, v_ref), (o_ref, lse_row_ref), ex_refs = _split_refs(refs, 3, 2, ex)
        _hosted(ex, ex_refs, grid, True)
        i = pl.program_id(1)
        qs = [q_ref[:, _head_lanes(hh)] for hh in range(heads)]

        def scores(j, hh):
            kj = k_ref[pl.ds(pl.multiple_of(j * TQ, TQ), TQ), _head_lanes(hh)]
            return lax.dot_general(kj, qs[hh], NT_DIMS, preferred_element_type=F32)

        def update(j, sc, m_prev, l_prev, acc, hh):
            vt = v_ref[pl.ds(pl.multiple_of(j * TQ, TQ), TQ), _head_lanes(hh)].T
            m_new = jnp.maximum(m_prev, jnp.max(sc, axis=0, keepdims=True))
            alpha = jnp.exp2(m_prev - m_new)
            p = jnp.exp2(sc - m_new)
            l_new = alpha * l_prev + jnp.sum(p, axis=0, keepdims=True)
            acc = acc * alpha + jnp.dot(vt, p.astype(BF16), preferred_element_type=F32)
            return m_new, l_new, acc

        def step(j, carry):
            out = []
            for hh in range(heads):
                sc, m_prev, l_prev, acc = carry[hh]
                out.append((scores(j + 1, hh),) + update(j, sc, m_prev, l_prev, acc, hh))
            return tuple(out)

        init = tuple((scores(0, hh), jnp.full((1, TQ), NEG, F32), jnp.zeros((1, TQ), F32),
                      jnp.zeros((LANES, TQ), F32)) for hh in range(heads))
        carry = lax.fori_loop(0, i, step, init)
        for hh, (sc, m_prev, l_prev, acc) in enumerate(carry):
            sc = jnp.where(_chunk_mask(False), sc, NEG)
            m_fin, l_fin, acc = update(i, sc, m_prev, l_prev, acc, hh)
            o_ref[:, _head_lanes(hh)] = (acc / l_fin).T
            lse_row_ref[hh, 0] = m_fin + jnp.log2(l_fin)
        _hosted(ex, ex_refs, grid, False)

    tile, whole, row_tile, _ = _flash_specs(s, heads)
    return _host_call(
        body, ex, name, grid, [tile, whole, whole], [tile, row_tile],
        [jax.ShapeDtypeStruct((s, MLA_HEADS * LANES), F32),
         jax.ShapeDtypeStruct((MLA_HEADS, nq, 1, TQ), F32)], (q, k, v))


def _attn_delta(do, o, name):
    s = do.shape[0]

    def body(do_ref, o_ref, d_ref):
        for h in range(MLA_HEADS):
            prod = do_ref[:, _head_lanes(h)] * o_ref[:, _head_lanes(h)]
            d_ref[h, 0] = _to_row(jnp.sum(prod, axis=-1, keepdims=True))

    tile = pl.BlockSpec((TQ, MLA_HEADS * LANES), lambda i: (i, 0))
    return _call(
        body, name=name, grid=(s // TQ,), in_specs=[tile, tile],
        out_specs=pl.BlockSpec((MLA_HEADS, 1, 1, TQ), lambda i: (0, i, 0, 0)),
        out_shape=jax.ShapeDtypeStruct((MLA_HEADS, s // TQ, 1, TQ), F32),
        compiler_params=_params("parallel"),
    )(do, o)


def _flash_bwd(q, k, v, do, lse_row, delta_row, name, ex=None):
    s = q.shape[0]
    nq = s // TQ
    grid = (MLA_HEADS // HEADS_PER_STEP, nq)

    def body(*refs):
        ins, (dqt_ref, dk_ref, dv_ref), ex_refs = _split_refs(refs, 6, 3, ex)
        q_ref, k_ref, v_ref, do_ref, lse_row_ref, delta_row_ref = ins
        _hosted(ex, ex_refs, grid, True)
        j = pl.program_id(1)

        @pl.when(j == 0)
        def _():
            dqt_ref[...] = jnp.zeros_like(dqt_ref)

        kjs = [k_ref[:, _head_lanes(hh)] for hh in range(HEADS_PER_STEP)]
        vjs = [v_ref[:, _head_lanes(hh)] for hh in range(HEADS_PER_STEP)]
        kts = [kj.T for kj in kjs]

        def step(i, carry, masked):
            base = pl.multiple_of(i * TQ, TQ)
            out = []
            for hh in range(HEADS_PER_STEP):
                dk, dv = carry[hh]
                qi = q_ref[pl.ds(base, TQ), _head_lanes(hh)]
                dob = do_ref[pl.ds(base, TQ), _head_lanes(hh)].astype(BF16)
                sc_t = lax.dot_general(kjs[hh], qi, NT_DIMS, preferred_element_type=F32)
                if masked:
                    sc_t = jnp.where(_chunk_mask(False), sc_t, NEG)
                p_t = jnp.exp2(sc_t - lse_row_ref[hh, i])
                dv = dv + jnp.dot(p_t.astype(BF16), dob, preferred_element_type=F32)
                dp_t = lax.dot_general(vjs[hh], dob, NT_DIMS, preferred_element_type=F32)
                ds_t = (p_t * (dp_t - delta_row_ref[hh, i])).astype(BF16)
                dk = dk + jnp.dot(ds_t, qi, preferred_element_type=F32)
                dqt_ref[hh, i] += jnp.dot(kts[hh], ds_t, preferred_element_type=F32)
                out.append((dk, dv))
            return tuple(out)

        zero = jnp.zeros((TQ, LANES), F32)
        carry = step(j, tuple((zero, zero) for _ in range(HEADS_PER_STEP)), True)
        carry = lax.fori_loop(j + 1, nq, functools.partial(step, masked=False), carry)
        for hh, (dk, dv) in enumerate(carry):
            dk_ref[:, _head_lanes(hh)] = dk
            dv_ref[:, _head_lanes(hh)] = dv.astype(dv_ref.dtype)
        _hosted(ex, ex_refs, grid, False)

    tile, whole, _, row_whole = _flash_specs(s)
    dqt_spec = pl.BlockSpec((HEADS_PER_STEP, nq, LANES, TQ), lambda h, j: (h, 0, 0, 0))
    wide = lambda dt: jax.ShapeDtypeStruct((s, MLA_HEADS * LANES), dt)
    return _host_call(
        body, ex, name, grid, [whole, tile, tile, whole, row_whole, row_whole], [dqt_spec, tile, tile],
        [jax.ShapeDtypeStruct((MLA_HEADS, nq, LANES, TQ), F32), wide(F32), wide(BF16)],
        (q, k, v, do, lse_row, delta_row))


def _side_by_side(g):
    return g.transpose(1, 0, 2).reshape(g.shape[1], N_DEV * g.shape[2])


def _col_shards(g):
    return g.reshape(g.shape[0], N_DEV, g.shape[1] // N_DEV).transpose(1, 0, 2)


def _pad_last(v, to):
    return jnp.pad(v, [(0, 0)] * (v.ndim - 1) + [(0, to - v.shape[-1])])


def _tiny_rows(v, rows):
    flat = v.reshape(v.shape[:-2] + (-1,))
    return _pad_last(flat, rows * LANES).reshape(v.shape[:-2] + (rows, LANES))


def _pack_small(vals):
    parts = []
    for (n, size), pad in zip(SMALL, SMALL_PAD):
        parts.append(jnp.pad(vals[n].reshape(-1), (0, pad - size)))
    flat = jnp.concatenate(parts)
    return jnp.pad(flat, (0, SMALL_ROWS * LANES - flat.shape[0])).reshape(SMALL_ROWS, LANES)


def _unpack_small(packed):
    flat = packed.reshape(-1)
    out, off = {}, 0
    for (n, size), pad in zip(SMALL, SMALL_PAD):
        out[n] = flat[off:off + size].reshape(1, size)
        off += pad
    return out


def _pad_heads(w, per_head, axis):
    shape = list(w.shape)
    shape[axis:axis + 1] = [MLA_HEADS, per_head]
    w = w.reshape(shape)
    pad = [(0, 0)] * len(shape)
    pad[axis + 1] = (0, LANES - per_head)
    w = jnp.pad(w, pad)
    shape[axis:axis + 2] = [MLA_HEADS * LANES]
    return w.reshape(shape)


def _unpad_heads(w, per_head, axis):
    shape = list(w.shape)
    shape[axis:axis + 1] = [MLA_HEADS, LANES]
    w = w.reshape(shape)
    w = lax.slice_in_dim(w, 0, per_head, axis=axis + 1)
    shape[axis:axis + 2] = [MLA_HEADS * per_head]
    return w.reshape(shape)


def _row(v, pad_to=None):
    v = v.reshape(1, -1)
    if pad_to is not None:
        v = jnp.pad(v, ((0, 0), (0, pad_to - v.shape[1])))
    return v


def kernel(x, mem, positions, mix_norm_g, w_in, b_conv_in, w_conv_dw, b_conv_dw, conv_ln_g, conv_ln_b, q_lat_norm_g, w_uq, kv_lat_norm_g, w_ukv, q_norm_g, k_norm_g, w_out, mem_norm_x_g, mem_norm_m_g, w_mem_q, w_mem_kv, mem_q_norm_g, mem_k_norm_g, w_mem_o, ffn_norm_g, w_up, w_ffn_dw, b_ffn_dw, w_down, loss_target, m_mix_norm_g, m_w_in, m_b_conv_in, m_w_conv_dw, m_b_conv_dw, m_conv_ln_g, m_conv_ln_b, m_q_lat_norm_g, m_w_uq, m_kv_lat_norm_g, m_w_ukv, m_q_norm_g, m_k_norm_g, m_w_out, m_mem_norm_x_g, m_mem_norm_m_g, m_w_mem_q, m_w_mem_kv, m_mem_q_norm_g, m_mem_k_norm_g, m_w_mem_o, m_ffn_norm_g, m_w_up, m_w_ffn_dw, m_b_ffn_dw, m_w_down, v_mix_norm_g, v_w_in, v_b_conv_in, v_w_conv_dw, v_b_conv_dw, v_conv_ln_g, v_conv_ln_b, v_q_lat_norm_g, v_w_uq, v_kv_lat_norm_g, v_w_ukv, v_q_norm_g, v_k_norm_g, v_w_out, v_mem_norm_x_g, v_mem_norm_m_g, v_w_mem_q, v_w_mem_kv, v_mem_q_norm_g, v_mem_k_norm_g, v_w_mem_o, v_ffn_norm_g, v_w_up, v_w_ffn_dw, v_b_ffn_dw, v_w_down):
    a = dict(locals())
    seq = x.shape[1]
    xs = x.reshape(seq, D_MODEL)
    mems = mem.reshape(-1, D_MODEL)
    target = loss_target.reshape(seq, D_MODEL)

    tiny = [n for n, _, _ in TINY]
    shard = lambda n: a[n][0] if n in tiny else a[n][0].astype(BF16)
    pos = jnp.stack([2 * lax.axis_index("x") + lax.axis_index("y"), lax.axis_index("c")]).astype(jnp.int32)
    ag_first = ["w_in", "w_uq", "w_ukv", "w_conv_dw"]
    ag_later = [n for n in [b for b, _ in BIG] + tiny if n not in ag_first]
    wg = dict(zip(ag_first, _all_gather([shard(n) for n in ag_first], "ag_weights_first")))
    wi = _side_by_side(wg["w_in"])
    s3 = 2 * CONV_CH + MLA_Q_RANK + MLA_KV_RANK
    w_in_p = jnp.concatenate([
        wi[:, :s3], jnp.zeros((D_MODEL, MLA_NOPE), BF16), wi[:, s3:],
        jnp.zeros((D_MODEL, LANES - MLA_QK), BF16)], axis=1)
    w_uq_p = _side_by_side(_pad_last(wg["w_uq"], LANES))
    w_uk_p = _side_by_side(_pad_last(wg["w_ukv"][:, :, :MLA_NOPE], LANES))
    w_uv_p = _side_by_side(_pad_last(wg["w_ukv"][:, :, MLA_NOPE:], LANES))
    w_cdw = _side_by_side(wg["w_conv_dw"])

    g_mix, g_qlat, g_kvlat = _row(mix_norm_g), _row(q_lat_norm_g), _row(kv_lat_norm_g)
    b_in = _row(b_conv_in)
    b_in_a, b_in_g = b_in[:, :CONV_CH], b_in[:, CONV_CH:]
    b_cdw, ln_g, ln_b = _row(b_conv_dw), _row(conv_ln_g), _row(conv_ln_b)
    g_q, g_k = _row(q_norm_g, LANES), _row(k_norm_g, LANES)
    g_memx, g_memm = _row(mem_norm_x_g), _row(mem_norm_m_g)
    g_mq, g_mk, g_ffn = _row(mem_q_norm_g), _row(mem_k_norm_g), _row(ffn_norm_g)
    b_f = _pad_last(b_ffn_dw.reshape(N_DEV, FF_SHARD), FF_PAD)
    b_f_g, b_f_v = b_f[:4].reshape(1, D_FF_PAD), b_f[4:].reshape(1, D_FF_PAD)

    freq = ROPE_THETA ** (-jnp.arange(0, MLA_ROPE, 2, dtype=F32) / MLA_ROPE)
    inv_freq = jnp.concatenate([jnp.zeros((MLA_NOPE,), F32), freq, freq,
                                jnp.zeros((LANES - MLA_QK,), F32)]).reshape(1, LANES)
    cos, sin_a, sin_b = _rowwise(_f_rope_tab, [positions.reshape(seq, 1)], [inv_freq],
                                 [(LANES, F32)] * 3, [], name="rope_tables")

    (h1,) = _rowwise(_f_rms, [xs], [g_mix], [(D_MODEL, BF16)], [], name="rms_mix")
    z = _mm(h1, w_in_p, name="mm_in")
    z_rows = [(z, CONV_CH, 0, 0), (z, CONV_CH, 1, 0), (z, MLA_Q_RANK, 4, 0), (z, MLA_KV_RANK, 10, 0)]
    z_kr = (z, LANES, 11, 0)
    u0, cqn, ckvn = _rowwise(
        _mix_pre, z_rows, [b_in_a, b_in_g, g_qlat, g_kvlat],
        [(CONV_CH, F32), (MLA_Q_RANK, BF16), (MLA_KV_RANK, BF16)], [], name="mix_pre")
    c1 = _conv_fwd(u0, w_cdw, "conv31_fwd")
    (u,) = _rowwise(lambda c, b, g, bb: (_ln_silu(c, b, g, bb),), [c1], [b_cdw, ln_g, ln_b],
                    [(CONV_CH, BF16)], [], name="ln_silu")
    q0 = _mm(cqn, w_uq_p, name="mm_uq")
    kn0 = _mm(ckvn, w_uk_p, name="mm_uk")
    v0 = _mm(ckvn, w_uv_p, out_dtype=BF16, name="mm_uv")
    qk_rows = [q0, kn0, z_kr, cos, sin_a, sin_b]
    qh, kh = _rowwise(_f_qk_prep, qk_rows, [g_q, g_k],
                      [(MLA_HEADS * LANES, BF16)] * 2, [], name="qk_prep")
    (attn, lse_row), later = _flash_fwd(
        qh, kh, v0, "flash_fwd", _plan_all_gather([shard(n) for n in ag_later]))
    wg.update(zip(ag_later, later))
    w_out = wg["w_out"].reshape(D_MODEL, D_MODEL)
    w_out_u = w_out[:CONV_CH]
    w_out_a = _pad_heads(w_out[CONV_CH:], MLA_V, 0)
    w_mq, w_mo = wg["w_mem_q"].reshape(D_MODEL, D_MODEL), wg["w_mem_o"].reshape(D_MODEL, D_MODEL)
    w_mkv = _side_by_side(wg["w_mem_kv"])
    w_up_p = _pad_last(wg["w_up"], FF_PAD)
    w_dn = jnp.pad(wg["w_down"].reshape(4, FF_SHARD, D_MODEL),
                   ((0, 0), (0, FF_PAD - FF_SHARD), (0, 0))).reshape(D_FF_PAD, D_MODEL)
    w_fdw = _pad_last(wg["w_ffn_dw"], FF_PAD)
    w_fdw_g = w_fdw[:4].transpose(1, 0, 2).reshape(FFN_CONV_WIDTH, D_FF_PAD)
    w_fdw_v = w_fdw[4:].transpose(1, 0, 2).reshape(FFN_CONV_WIDTH, D_FF_PAD)
    x1 = _mm(u, w_out_u, add=xs, name="mm_out_u")
    x1 = _mm(attn, w_out_a, add=x1, name="mm_out_a")

    (hq,) = _rowwise(_f_rms, [x1], [g_memx], [(D_MODEL, BF16)], [], name="rms_memx")
    (hm,) = _rowwise(_f_rms, [mems], [g_memm], [(D_MODEL, BF16)], [], name="rms_memm", tm=mems.shape[0])
    qm0 = _mm(hq, w_mq, name="mm_memq")
    kvm0 = _mm(hm, w_mkv, name="mm_memkv", tm=mems.shape[0])
    (km,) = _rowwise(_f_mem_k, [(kvm0, D_MODEL, 0, 0)], [g_mk], [(D_MODEL, BF16)], [],
                     name="mem_k", tm=mems.shape[0])
    vm = kvm0[:, D_MODEL:]
    (om,) = _rowwise(_f_mem_attn, [qm0], [km, vm, g_mq], [(D_MODEL, BF16)], [], name="mem_attn")
    x2 = _mm(om, w_mo, add=x1, name="mm_memo")

    (h3,) = _rowwise(_f_rms, [x2], [g_ffn], [(D_MODEL, BF16)], [], name="rms_ffn")
    up_g, up_v, act, dy, sq = _ffn_fwd(h3, w_up_p, w_fdw_g, w_fdw_v, b_f_g, b_f_v, w_dn, x2, target, "ffn_fwd")
    loss = lax.psum(0.5 * jnp.sum(sq) / D_MODEL, ("x", "y", "c"))

    gw, gs, gt = {}, {}, {}
    gw_dn = _mm_tn(act, dy, name="tn_down").reshape(4, FF_PAD, D_MODEL)
    gw["w_down"] = gw_dn[:, :FF_SHARD].reshape(N_DEV, FF_SHARD // 2, D_MODEL)
    dact = _mm(dy, w_dn, trans_b=True, name="mm_down_t")
    dup_g, dup_v, dwf_g, dwf_v, db_g, db_v = _ffn_mid_bwd(
        up_g, up_v, dact, w_fdw_g, w_fdw_v, b_f_g, b_f_v, "ffn_mid_bwd")
    db_f = jnp.concatenate([db_g.reshape(4, FF_PAD), db_v.reshape(4, FF_PAD)], axis=0)
    gs["b_ffn_dw"] = db_f[:, :FF_SHARD].reshape(1, 2 * D_FF)
    dwf = jnp.concatenate([dwf_g.reshape(FFN_CONV_WIDTH, 4, FF_PAD), dwf_v.reshape(FFN_CONV_WIDTH, 4, FF_PAD)], axis=1)
    gt["w_ffn_dw"] = dwf[:, :, :FF_SHARD].transpose(1, 0, 2)
    gw_up = jnp.concatenate([_mm_tn(h3, dup_g, shard_cols=FF_PAD, name="tn_up_g"),
                             _mm_tn(h3, dup_v, shard_cols=FF_PAD, name="tn_up_v")], axis=0)
    gw["w_up"] = gw_up[:, :, :FF_SHARD]
    dh3 = _mm(dup_g, w_up_p, trans_b=True, b_shards=(0, 4), name="mm_up_g_t")
    dh3 = _mm(dup_v, w_up_p, trans_b=True, b_shards=(4, 4), add=dh3, name="mm_up_v_t")
    dx2, gs["ffn_norm_g"] = _rowwise(_b_rms, [x2, dh3, dy], [g_ffn], [(D_MODEL, F32)], [(1, D_MODEL)],
                                     name="rms_ffn_bwd")

    gw["w_mem_o"] = _mm_tn(om, dx2, name="tn_memo").reshape(N_DEV, -1, D_MODEL)
    dom = _mm(dx2, w_mo, trans_b=True, out_dtype=BF16, name="mm_memo_t")
    n_mem = mems.shape[0]
    dqm0, dkm, dvm, gs["mem_q_norm_g"] = _rowwise(
        _b_mem_attn, [dom, qm0], [km, vm, g_mq], [(D_MODEL, BF16)],
        [(n_mem, D_MODEL), (n_mem, D_MODEL), (1, MEM_HEAD_DIM)], name="mem_attn_bwd")
    gw["w_mem_q"] = _mm_tn(hq, dqm0, name="tn_memq").reshape(N_DEV, -1, D_MODEL)
    dhq = _mm(dqm0, w_mq, trans_b=True, name="mm_memq_t")
    dx1, gs["mem_norm_x_g"] = _rowwise(_b_rms, [x1, dhq, dx2], [g_memx], [(D_MODEL, F32)],
                                       [(1, D_MODEL)], name="rms_memx_bwd")
    dkk, gs["mem_k_norm_g"] = _rowwise(_b_mem_k, [(kvm0, D_MODEL, 0, 0), dkm], [g_mk],
                                       [(D_MODEL, F32)], [(1, MEM_HEAD_DIM)], name="mem_k_bwd", tm=n_mem)
    dkvm0 = jnp.concatenate([dkk, dvm], axis=1)
    gw["w_mem_kv"] = _col_shards(_mm_tn(hm, dkvm0, name="tn_memkv", ts=n_mem))
    dhm = _mm(dkvm0, w_mkv, trans_b=True, name="mm_memkv_t", tm=n_mem)
    _, gs["mem_norm_m_g"] = _rowwise(_b_rms_nores, [mems, dhm], [g_memm], [(D_MODEL, F32)],
                                     [(1, D_MODEL)], name="rms_memm_bwd", tm=n_mem)

    gw_out_u = _mm_tn(u, dx1, name="tn_out_u")
    gw_out_a = _mm_tn(attn, dx1, name="tn_out_a")
    gw["w_out"] = jnp.concatenate([gw_out_u, _unpad_heads(gw_out_a, MLA_V, 0)], axis=0).reshape(N_DEV, -1, D_MODEL)
    du = _mm(dx1, w_out_u, trans_b=True, name="mm_out_u_t")
    dattn = _mm(dx1, w_out_a, trans_b=True, name="mm_out_a_t")
    dc1, gs["b_conv_dw"], gs["conv_ln_g"], gs["conv_ln_b"] = _rowwise(
        _b_ln_silu, [c1, du], [b_cdw, ln_g, ln_b], [(CONV_CH, F32)], [(1, CONV_CH)] * 3, name="ln_silu_bwd")
    du0, g_cdw = _conv_bwd(dc1, u0, w_cdw, "conv31_bwd")
    gt["w_conv_dw"] = _col_shards(g_cdw)
    rs_first = ["w_up", "w_down", "w_mem_o", "w_mem_q", "w_mem_kv", "w_out"]
    grads = [gw[n] for n in rs_first]
    sums_first = _rs_add(grads, _swap_sibling(grads, "rs_sibling_first"), pos, "rs_add_first")
    delta_row = _attn_delta(dattn, attn, "attn_delta")
    (dqt, dkh, dv0), recvs_first = _flash_bwd(
        qh, kh, v0, dattn, lse_row, delta_row, "flash_bwd", _plan_swap_chips(sums_first))
    dqt_row = (dqt, pl.BlockSpec((MLA_HEADS, 1, LANES, TQ), lambda i: (0, i, 0, 0)))
    dq0, dkn0, dkr, dgq, dgk = _rowwise(
        _b_qk_prep, qk_rows + [dqt_row, dkh], [g_q, g_k],
        [(MLA_HEADS * LANES, BF16)] * 2 + [(LANES, F32)], [(1, LANES)] * 2, name="qk_prep_bwd")
    gs["q_norm_g"], gs["k_norm_g"] = dgq[:, :MLA_QK], dgk[:, :MLA_QK]
    gw["w_uq"] = _col_shards(_mm_tn(cqn, dq0, name="tn_uq"))[:, :, :MLA_QK]
    g_uk = _col_shards(_mm_tn(ckvn, dkn0, name="tn_uk"))[:, :, :MLA_NOPE]
    g_uv = _col_shards(_mm_tn(ckvn, dv0, name="tn_uv"))[:, :, :MLA_V]
    gw["w_ukv"] = jnp.concatenate([g_uk, g_uv], axis=2)
    dcqn = _mm(dq0, w_uq_p, trans_b=True, name="mm_uq_t")
    dckvn = _mm(dkn0, w_uk_p, trans_b=True, name="mm_uk_t")
    dckvn = _mm(dv0, w_uv_p, trans_b=True, add=dckvn, name="mm_uv_t")
    dz, dba, dbg, gs["q_lat_norm_g"], gs["kv_lat_norm_g"] = _rowwise(
        _b_mix_pre, z_rows + [du0, dcqn, dckvn, dkr], [b_in_a, b_in_g, g_qlat, g_kvlat],
        [(IN_PAD, BF16)], [(1, CONV_CH)] * 2 + [(1, MLA_Q_RANK), (1, MLA_KV_RANK)], name="mix_pre_bwd")
    gs["b_conv_in"] = jnp.concatenate([dba, dbg], axis=1)
    gw_in = _mm_tn(h1, dz, name="tn_in")
    gw["w_in"] = _col_shards(jnp.concatenate([gw_in[:, :s3], gw_in[:, s3 + MLA_NOPE:s3 + MLA_QK]], axis=1))
    dh1 = _mm(dz, w_in_p, trans_b=True, name="mm_in_t")
    dx, gs["mix_norm_g"] = _rowwise(_b_rms, [xs, dh1, dx1], [g_mix], [(D_MODEL, F32)], [(1, D_MODEL)],
                                    name="rms_mix_bwd")

    rs_last = [n for n, _ in BIG if n not in rs_first]
    grads = [gw[n] for n in rs_last]
    sums_last = _rs_add(grads, _swap_sibling(grads, "rs_sibling_last"), pos, "rs_add_last")
    recvs_last = _run_exchange(_plan_swap_chips(sums_last), "rs_chips_last")
    big = rs_first + rs_last
    flat = _adamw_big(list(sums_first) + list(sums_last), list(recvs_first) + list(recvs_last),
                      [a[n] for n in big], [a["m_" + n] for n in big], [a["v_" + n] for n in big],
                      pos, "adamw_big")
    res = [{n: flat[4 * i + k] for i, n in enumerate(big)} for k in range(4)]

    part = jnp.concatenate(
        [_pack_small(gs)] + [_tiny_rows(gt[n], rows).reshape(N_DEV * rows, LANES) for n, _, rows in TINY], axis=0)
    (parts,) = _all_gather([part], "ag_small_grads")
    small_in = [_pack_small({n: a[p + n] for n, _ in SMALL}) for p in ("", "m_", "v_")]
    tiny_in = [[_tiny_rows(a[p + n][0], rows) for p in ("", "m_", "v_")] for n, _, rows in TINY]
    flat = _adamw_small(parts, small_in, tiny_in, "adamw_small")
    for k in range(4):
        res[k].update(_unpack_small(flat[k]))
        for i, (n, shape, _) in enumerate(TINY):
            res[k][n] = flat[4 * (i + 1) + k].reshape(-1)[:math.prod(shape)].reshape((1,) + shape)

    return (loss, dx.reshape(1, seq, D_MODEL), *[res[k][n] for k in range(4) for n in WEIGHTS])
```

```python
import functools
import math

import jax
import jax.numpy as jnp
from jax import lax
from jax.experimental import pallas as pl
from jax.experimental.pallas import tpu as pltpu

F32 = jnp.float32
BF16 = jnp.bfloat16
EPS = 1e-6
LANES = 128
N_DEV = 8
D_MODEL = 1024
CONV_CH = 512
CONV_WIDTH = 31
MLA_HEADS = 8
MLA_NOPE = 64
MLA_ROPE = 32
MLA_V = 64
MLA_QK = MLA_NOPE + MLA_ROPE
MLA_Q_RANK = 256
MLA_KV_RANK = 128
ROPE_THETA = 10000.0
IN_COLS = 2 * CONV_CH + MLA_Q_RANK + MLA_KV_RANK + MLA_ROPE
IN_PAD = 2 * CONV_CH + MLA_Q_RANK + MLA_KV_RANK + LANES
MEM_HEADS = 4
MEM_HEAD_DIM = 256
D_FF = 2816
FFN_CONV_WIDTH = 3
CHUNK = 64
ATT_SCALE = 1.0 / math.sqrt(MLA_QK)
LN2 = math.log(2.0)
Q_SCALE = ATT_SCALE / LN2
MEM_SCALE = 1.0 / math.sqrt(MEM_HEAD_DIM)
ADAM_LR, ADAM_B1, ADAM_B2, ADAM_EPS, ADAM_WD, ADAM_STEP = 0.001, 0.9, 0.999, 1e-08, 0.01, 10

TM = 512
MM_ROWS = 1024
TQ = 512
HEADS_PER_STEP = 2
FWD_HEADS = 4
CONV_ROWS = 256
NEG = -1e30
VMEM_LIMIT = 56 * 1024 * 1024

MESH = pl.DeviceIdType.MESH
ANY = pl.BlockSpec(memory_space=pl.ANY)
NT_DIMS = (((1,), (1,)), ((), ()))

BIG = [
    ("w_in", (1024, 180)), ("w_uq", (256, 96)), ("w_ukv", (128, 128)), ("w_out", (128, 1024)),
    ("w_mem_q", (128, 1024)), ("w_mem_kv", (1024, 256)), ("w_mem_o", (128, 1024)),
    ("w_up", (1024, 704)), ("w_down", (352, 1024)),
]
TINY = [("w_conv_dw", (31, 64), 16), ("w_ffn_dw", (3, 704), 24)]
ROW_STEPS = 4
FF_SHARD = D_FF // 4
FF_PAD = 768
D_FF_PAD = 4 * FF_PAD
SMALL = [
    ("mix_norm_g", 1024), ("b_conv_in", 1024), ("b_conv_dw", 512), ("conv_ln_g", 512),
    ("conv_ln_b", 512), ("q_lat_norm_g", 256), ("kv_lat_norm_g", 128), ("q_norm_g", 96),
    ("k_norm_g", 96), ("mem_norm_x_g", 1024), ("mem_norm_m_g", 1024), ("mem_q_norm_g", 256),
    ("mem_k_norm_g", 256), ("ffn_norm_g", 1024), ("b_ffn_dw", 5632),
]
WEIGHTS = [
    "mix_norm_g", "w_in", "b_conv_in", "w_conv_dw", "b_conv_dw", "conv_ln_g", "conv_ln_b",
    "q_lat_norm_g", "w_uq", "kv_lat_norm_g", "w_ukv", "q_norm_g", "k_norm_g", "w_out",
    "mem_norm_x_g", "mem_norm_m_g", "w_mem_q", "w_mem_kv", "mem_q_norm_g", "mem_k_norm_g",
    "w_mem_o", "ffn_norm_g", "w_up", "w_ffn_dw", "b_ffn_dw", "w_down",
]


SMALL_PAD = [(-(-n // LANES)) * LANES for _, n in SMALL]
SMALL_ROWS = -(-sum(SMALL_PAD) // (8 * LANES)) * 8
TINY_BASE = [SMALL_ROWS + N_DEV * sum(r for _, _, r in TINY[:i]) for i in range(len(TINY))]
PART_ROWS = SMALL_ROWS + N_DEV * sum(r for _, _, r in TINY)


def _call(body, **kw):
    return pl.pallas_call(body, **kw)


def _params(*sem):
    return pltpu.CompilerParams(dimension_semantics=sem, vmem_limit_bytes=VMEM_LIMIT)


class _Exchange:
    def __init__(self, inputs, out_shape, scratch, start, finish):
        self.inputs, self.out_shape, self.scratch = list(inputs), list(out_shape), list(scratch)
        self.start, self.finish = start, finish


def _run_exchange(ex, name):
    n_in, n_out = len(ex.inputs), len(ex.out_shape)

    def body(*refs):
        parts = refs[:n_in], refs[n_in:n_in + n_out], refs[n_in + n_out:]
        ex.start(*parts)
        ex.finish(*parts)

    return _call(body, name=name, out_shape=ex.out_shape, in_specs=[ANY] * n_in,
                 out_specs=[ANY] * n_out, scratch_shapes=ex.scratch)(*ex.inputs)


def _plan_all_gather(xs):
    n = len(xs)

    def copies(x_refs, out_refs, sems):
        send_sems, recv_sems, local_sems = sems
        x, y, c = lax.axis_index("x"), lax.axis_index("y"), lax.axis_index("c")
        me, sibling = (x, y, c), (x, y, 1 - c)
        chips = [(1 - x, y), (x, 1 - y), (1 - x, 1 - y)]

        def slot(o, px, py, pc):
            return out_refs[o].at[4 * px + 2 * py + pc]

        def copy(o, k, block, to, src=None):
            return pltpu.make_async_remote_copy(
                src_ref=slot(o, *block) if src is None else src, dst_ref=slot(o, *block),
                send_sem=send_sems.at[o, k], recv_sem=recv_sems.at[o, k],
                device_id=to, device_id_type=MESH)

        mine = [pltpu.make_async_copy(x_refs[o], slot(o, *me), local_sems.at[o]) for o in range(n)]
        first = [copy(o, 0, me, sibling, src=x_refs[o]) for o in range(n)]
        first += [copy(o, 1 + j, me, (*chip, c), src=x_refs[o])
                  for j, chip in enumerate(chips) for o in range(n)]
        return me, sibling, chips, copy, mine, first

    def start(x_refs, out_refs, sems):
        _, _, _, _, mine, first = copies(x_refs, out_refs, sems)
        for cp in mine + first:
            cp.start()

    def finish(x_refs, out_refs, sems):
        me, sibling, chips, copy, mine, first = copies(x_refs, out_refs, sems)
        c = me[2]
        passed = []
        for j, chip in enumerate(chips):
            for o in range(n):
                copy(o, 1 + j, (*chip, c), me).wait_recv()
                passed.append(copy(o, 4 + j, (*chip, c), sibling))
                passed[-1].start()
        for o in range(n):
            copy(o, 0, sibling, me).wait_recv()
        for j, chip in enumerate(chips):
            for o in range(n):
                copy(o, 4 + j, (*chip, 1 - c), me).wait_recv()
        for cp in first + passed:
            cp.wait_send()
        for cp in mine:
            cp.wait()

    return _Exchange(
        xs, [jax.ShapeDtypeStruct((N_DEV,) + v.shape, v.dtype) for v in xs],
        [pltpu.SemaphoreType.DMA((n, 7)), pltpu.SemaphoreType.DMA((n, 7)), pltpu.SemaphoreType.DMA((n,))],
        start, finish)


def _all_gather(xs, name):
    return _run_exchange(_plan_all_gather(xs), name)


def _plan_swap_sibling(grads):
    n = len(grads)

    def copies(g_refs, got_refs, sems):
        send_sems, recv_sems = sems
        x, y, c = lax.axis_index("x"), lax.axis_index("y"), lax.axis_index("c")
        return [
            pltpu.make_async_remote_copy(
                src_ref=g_refs[o].at[2 * chip + 1 - c], dst_ref=got_refs[o].at[chip],
                send_sem=send_sems.at[o, chip], recv_sem=recv_sems.at[o, chip],
                device_id=(x, y, 1 - c), device_id_type=MESH)
            for o in range(n) for chip in range(4)]

    def start(g_refs, got_refs, sems):
        for cp in copies(g_refs, got_refs, sems):
            cp.start()

    def finish(g_refs, got_refs, sems):
        for cp in copies(g_refs, got_refs, sems):
            cp.wait()

    return _Exchange(
        grads, [jax.ShapeDtypeStruct((4,) + g.shape[1:], g.dtype) for g in grads],
        [pltpu.SemaphoreType.DMA((n, 4)), pltpu.SemaphoreType.DMA((n, 4))], start, finish)


def _plan_swap_chips(sums):
    n = len(sums)

    def copies(a_refs, r_refs, sems):
        send_sems, recv_sems = sems
        x, y, c = lax.axis_index("x"), lax.axis_index("y"), lax.axis_index("c")
        peers = [(x, 1 - y), (1 - x, y), (1 - x, 1 - y)]
        return [
            pltpu.make_async_remote_copy(
                src_ref=a_refs[o].at[2 * px + py], dst_ref=r_refs[o].at[k],
                send_sem=send_sems.at[o, k], recv_sem=recv_sems.at[o, k],
                device_id=(px, py, c), device_id_type=MESH)
            for k, (px, py) in enumerate(peers) for o in range(n)]

    def start(a_refs, r_refs, sems):
        for cp in copies(a_refs, r_refs, sems):
            cp.start()

    def finish(a_refs, r_refs, sems):
        for cp in copies(a_refs, r_refs, sems):
            cp.wait()

    return _Exchange(
        sums, [jax.ShapeDtypeStruct((3,) + a.shape[1:], a.dtype) for a in sums],
        [pltpu.SemaphoreType.DMA((n, 3)), pltpu.SemaphoreType.DMA((n, 3))], start, finish)


def _shard_block(shape):
    return (None, shape[-2] // ROW_STEPS, shape[-1])


def _rs_add(grads, gots, pos, name):
    n = len(grads)

    def body(pos_ref, *refs):
        for g_ref, t_ref, o_ref in zip(refs[:n], refs[n:2 * n], refs[2 * n:]):
            o_ref[...] = g_ref[...] + t_ref[...]

    in_specs = [pl.BlockSpec(_shard_block(g.shape), lambda a, t, pos: (2 * a + pos[1], t, 0)) for g in grads]
    in_specs += [pl.BlockSpec(_shard_block(g.shape), lambda a, t, pos: (a, t, 0)) for g in gots]
    return _call(
        body, name=name,
        grid_spec=pltpu.PrefetchScalarGridSpec(
            num_scalar_prefetch=1, grid=(4, ROW_STEPS), in_specs=in_specs,
            out_specs=[pl.BlockSpec(_shard_block(g.shape), lambda a, t, pos: (a, t, 0)) for g in gots]),
        out_shape=[jax.ShapeDtypeStruct(g.shape, g.dtype) for g in gots],
        compiler_params=_params("arbitrary", "arbitrary"),
    )(pos, *grads, *gots)


def _adamw_big(sums, recvs, ws, ms, vs, pos, name):
    n = len(sums)

    def body(pos_ref, *refs):
        ins, outs = refs[:7 * n], refs[7 * n:]
        for o in range(n):
            own, r1, r2, r3, w, m, v = [r[...] for r in ins[7 * o:7 * o + 7]]
            g = ((own + r1) + r2) + r3
            for ref, val in zip(outs[4 * o:4 * o + 4], (g,) + _adamw(w, g, m, v)):
                ref[...] = val

    in_specs, args, out_specs, out_shape = [], [], [], []
    for s_, r_, w_, m_, v_ in zip(sums, recvs, ws, ms, vs):
        blk = _shard_block(w_.shape)
        in_specs.append(pl.BlockSpec(blk, lambda t, pos: (pos[0], t, 0)))
        in_specs += [pl.BlockSpec(blk, lambda t, pos, k=k: (k, t, 0)) for k in range(3)]
        in_specs += [pl.BlockSpec(blk, lambda t, pos: (0, t, 0))] * 3
        args += [s_, r_, r_, r_, w_, m_, v_]
        out_specs += [pl.BlockSpec(blk, lambda t, pos: (0, t, 0))] * 4
        out_shape += [jax.ShapeDtypeStruct(w_.shape, F32)] * 4
    return _call(
        body, name=name,
        grid_spec=pltpu.PrefetchScalarGridSpec(
            num_scalar_prefetch=1, grid=(ROW_STEPS,), in_specs=in_specs, out_specs=out_specs),
        out_shape=out_shape, compiler_params=_params("arbitrary"),
    )(pos, *args)


def _tile(n, prefs):
    for t in prefs:
        if n % t == 0:
            return t
    return n


def _mm(a, b, *, name, trans_b=False, add=None, out_dtype=F32, tm=MM_ROWS, b_shards=None, ex=None):
    m, k = a.shape
    if b_shards is None:
        n = b.shape[0] if trans_b else b.shape[1]
        tn = _tile(n, (1536, 1408, 1024, 768, 512, 256, 128))
        tk = _tile(k, (1408, 1024, 768, 512, 256, 128))
    elif trans_b:
        n, tn, tk = b.shape[1], b.shape[1], b.shape[2]
    else:
        n, tn, tk = b_shards[1] * b.shape[2], b.shape[2], _tile(k, (1024, 512))
    nk = k // tk
    has_add = add is not None
    grid = (m // tm, n // tn, nk)

    def body(*refs):
        ins, outs, ex_refs = _split_refs(refs, 2 + has_add, 1, ex, int(nk > 1))
        _hosted(ex, ex_refs, grid, True)
        a_ref, b_ref = ins[0], ins[1]
        add_ref = ins[2] if has_add else None
        o_ref = outs[0]
        av = a_ref[...].astype(BF16)
        bv = b_ref[...].astype(BF16)
        if trans_b:
            part = lax.dot_general(av, bv, NT_DIMS, preferred_element_type=F32)
        else:
            part = jnp.dot(av, bv, preferred_element_type=F32)

        def finish(acc):
            if has_add:
                acc = acc + add_ref[...].astype(F32)
            o_ref[...] = acc.astype(o_ref.dtype)

        if nk == 1:
            finish(part)
        else:
            acc_ref = outs[1]
            kk = pl.program_id(2)

            @pl.when(kk == 0)
            def _():
                acc_ref[...] = part

            @pl.when(kk > 0)
            def _():
                acc_ref[...] += part

            @pl.when(kk == nk - 1)
            def _():
                finish(acc_ref[...])
        _hosted(ex, ex_refs, grid, False)

    in_specs = [pl.BlockSpec((tm, tk), lambda i, j, kk: (i, kk))]
    if b_shards is not None and trans_b:
        in_specs.append(pl.BlockSpec((None, tn, tk), lambda i, j, kk: (b_shards[0] + kk, j, 0)))
    elif b_shards is not None:
        in_specs.append(pl.BlockSpec((None, tk, tn), lambda i, j, kk: (b_shards[0] + j, kk, 0)))
    elif trans_b:
        in_specs.append(pl.BlockSpec((tn, tk), lambda i, j, kk: (j, kk)))
    else:
        in_specs.append(pl.BlockSpec((tk, tn), lambda i, j, kk: (kk, j)))
    args = [a, b]
    if has_add:
        in_specs.append(pl.BlockSpec((tm, tn), lambda i, j, kk: (i, j)))
        args.append(add)
    (out,), hosted = _host_call(
        body, ex, name, grid, in_specs, [pl.BlockSpec((tm, tn), lambda i, j, kk: (i, j))],
        [jax.ShapeDtypeStruct((m, n), out_dtype)], args,
        scratch=[pltpu.VMEM((tm, tn), F32)] if nk > 1 else [])
    return (out, hosted) if ex else out


def _mm_tn(a, b, *, name, ts=MM_ROWS, shard_cols=None):
    s, m = a.shape
    n = b.shape[1]
    tm = _tile(m, (1408, 1024, 768, 512, 256, 128))
    tn = shard_cols or _tile(n, (1536, 1408, 1024, 768, 512, 256, 128))
    if shard_cols:
        out_spec = pl.BlockSpec((None, tm, tn), lambda i, j, kk: (j, i, 0))
        out_shape = jax.ShapeDtypeStruct((n // tn, m, tn), F32)
    else:
        out_spec = pl.BlockSpec((tm, tn), lambda i, j, kk: (i, j))
        out_shape = jax.ShapeDtypeStruct((m, n), F32)

    def body(a_ref, b_ref, o_ref):
        kk = pl.program_id(2)
        part = jnp.dot(a_ref[...].astype(BF16).T, b_ref[...].astype(BF16),
                       preferred_element_type=F32)

        @pl.when(kk == 0)
        def _():
            o_ref[...] = part

        @pl.when(kk > 0)
        def _():
            o_ref[...] += part

    return _call(
        body, name=name, grid=(m // tm, n // tn, s // ts),
        in_specs=[pl.BlockSpec((ts, tm), lambda i, j, kk: (kk, i)),
                  pl.BlockSpec((ts, tn), lambda i, j, kk: (kk, j))],
        out_specs=out_spec, out_shape=out_shape,
        compiler_params=_params("parallel", "parallel", "arbitrary"),
    )(a, b)


def _rowwise(fn, rows, consts, row_outs, acc_outs, *, name, tm=TM, n_rows=None):
    rows = [r if isinstance(r, tuple) else (r, r.shape[1], 0, 0) for r in rows]
    s = n_rows or rows[0][0].shape[0]
    nr, nc, no, na = len(rows), len(consts), len(row_outs), len(acc_outs)

    def body(*refs):
        r_in, c_in = refs[:nr], refs[nr:nr + nc]
        o_refs, a_refs = refs[nr + nc:nr + nc + no], refs[nr + nc + no:]
        outs = fn(*[r[...] for r in r_in], *[c[...] for c in c_in])
        for r, v in zip(o_refs, outs[:no]):
            r[...] = v.astype(r.dtype)
        if na:
            i = pl.program_id(0)

            @pl.when(i == 0)
            def _():
                for r, v in zip(a_refs, outs[no:]):
                    r[...] = v.astype(F32)

            @pl.when(i > 0)
            def _():
                for r, v in zip(a_refs, outs[no:]):
                    r[...] += v.astype(F32)

    in_specs = [r[1] if isinstance(r[1], pl.BlockSpec) else
                pl.BlockSpec((tm, r[1]), lambda i, cb=r[2], rb=r[3]: (i + rb, cb)) for r in rows]
    in_specs += [pl.BlockSpec(c.shape, lambda i: (0, 0)) for c in consts]
    out_specs = [pl.BlockSpec((tm, w), lambda i: (i, 0)) for w, _ in row_outs]
    out_specs += [pl.BlockSpec(sh, lambda i: (0, 0)) for sh in acc_outs]
    out_shape = [jax.ShapeDtypeStruct((s, w), dt) for w, dt in row_outs]
    out_shape += [jax.ShapeDtypeStruct(sh, F32) for sh in acc_outs]
    return _call(
        body, name=name, grid=(s // tm,), in_specs=in_specs, out_specs=out_specs,
        out_shape=out_shape, compiler_params=_params("arbitrary"),
    )(*[r[0] for r in rows], *consts)


def _rms(x, g, n=None):
    ms = jnp.sum(x * x, axis=-1, keepdims=True) / float(n or x.shape[-1])
    return x * lax.rsqrt(ms + EPS) * g


def _layer_norm(x, g, b):
    mu = jnp.sum(x, axis=-1, keepdims=True) / float(x.shape[-1])
    xc = x - mu
    var = jnp.sum(xc * xc, axis=-1, keepdims=True) / float(x.shape[-1])
    return xc * lax.rsqrt(var + EPS) * g + b


def _silu(x):
    return x * jax.nn.sigmoid(x)


@jax.custom_vjp
def _rope(y, cos, sin_a, sin_b):
    return y * cos + pltpu.roll(y, 112, 1) * sin_a + pltpu.roll(y, 16, 1) * sin_b


def _rope_fwd(y, cos, sin_a, sin_b):
    return _rope(y, cos, sin_a, sin_b), (cos, sin_a, sin_b)


def _rope_bwd(res, ct):
    cos, sin_a, sin_b = res
    dy = ct * cos + pltpu.roll(ct * sin_a, 16, 1) + pltpu.roll(ct * sin_b, 112, 1)
    return dy, jnp.zeros_like(cos), jnp.zeros_like(sin_a), jnp.zeros_like(sin_b)


_rope.defvjp(_rope_fwd, _rope_bwd)


def _qk_head(xh, g, cos, sin_a, sin_b):
    return _rope(_rms(xh, g, MLA_QK), cos, sin_a, sin_b)


def _heads(x, width):
    return [x[:, h * width:(h + 1) * width] for h in range(x.shape[1] // width)]


def _f_rms(x, g):
    return (_rms(x, g),)


def _f_rope_tab(pos, inv_freq):
    ang = pos.astype(F32) * inv_freq
    lane = lax.broadcasted_iota(jnp.int32, ang.shape, 1)
    sn = jnp.sin(ang)
    first = (lane >= MLA_NOPE) & (lane < MLA_NOPE + MLA_ROPE // 2)
    second = (lane >= MLA_NOPE + MLA_ROPE // 2) & (lane < MLA_QK)
    return jnp.cos(ang), jnp.where(first, -sn, 0.0), jnp.where(second, sn, 0.0)


def _mix_pre(za, zg, zcq, zckv, ba, bg, gq, gkv):
    u0 = (za + ba) * jax.nn.sigmoid(zg + bg)
    return u0, _rms(zcq, gq), _rms(zckv, gkv)


def _ln_silu(c1, bdw, lg, lb):
    return _silu(_layer_norm(c1 + bdw, lg, lb))


def _f_qk_prep(q0, kn, kr, cos, sa, sb, gq, gk):
    qs = [_qk_head(xh, gq, cos, sa, sb) * Q_SCALE for xh in _heads(q0, LANES)]
    ks = [_qk_head(xh + kr, gk, cos, sa, sb) for xh in _heads(kn, LANES)]
    return jnp.concatenate(qs, axis=1), jnp.concatenate(ks, axis=1)


def _act(cg, cv, bg, bv):
    return _silu(cg + bg) * (cv + bv)


def _f_mem_k(kk, g):
    return (jnp.concatenate([_rms(xh, g) for xh in _heads(kk, MEM_HEAD_DIM)], axis=1),)


def _mem_probs(qn, kmh):
    s = lax.dot_general(qn.astype(BF16), kmh, NT_DIMS, preferred_element_type=F32) * MEM_SCALE
    e = jnp.exp(s - jnp.max(s, axis=-1, keepdims=True))
    return e / jnp.sum(e, axis=-1, keepdims=True)


def _f_mem_attn(qm0, km, vm, g):
    outs = []
    for h, xh in enumerate(_heads(qm0, MEM_HEAD_DIM)):
        sl = slice(h * MEM_HEAD_DIM, (h + 1) * MEM_HEAD_DIM)
        p = _mem_probs(_rms(xh, g), km[:, sl])
        outs.append(jnp.dot(p.astype(BF16), vm[:, sl].astype(BF16), preferred_element_type=F32))
    return (jnp.concatenate(outs, axis=1),)


def _f_loss(y, t):
    e = y - t
    return e * (1.0 / D_MODEL), jnp.sum(e * e, axis=0, keepdims=True)


def _b_rms(x, dh, dres, g):
    _, vjp = jax.vjp(_rms, x, g)
    dx, dg = vjp(dh)
    return dx + dres, dg


def _b_rms_nores(x, dh, g):
    _, vjp = jax.vjp(_rms, x, g)
    dx, dg = vjp(dh)
    return dx, dg


def _b_mix_pre(za, zg, zcq, zckv, du0, dcqn, dckvn, dkr, ba, bg, gq, gkv):
    _, vjp = jax.vjp(_mix_pre, za, zg, zcq, zckv, ba, bg, gq, gkv)
    dza, dzg, dzcq, dzckv, dba, dbg, dgq, dgkv = vjp((du0, dcqn, dckvn))
    return jnp.concatenate([dza, dzg, dzcq, dzckv, dkr], axis=1), dba, dbg, dgq, dgkv


def _b_ln_silu(c1, du, bdw, lg, lb):
    _, vjp = jax.vjp(_ln_silu, c1, bdw, lg, lb)
    return vjp(du)


def _b_qk_prep(q0, kn, kr, cos, sa, sb, dq, dk, gq, gk):
    head = lambda xh, g: _qk_head(xh, g, cos, sa, sb)
    dk = dk * LN2
    dq0, dkn = [], []
    dkr = jnp.zeros_like(kr)
    dgq = jnp.zeros_like(gq)
    dgk = jnp.zeros_like(gk)
    for h, xh in enumerate(_heads(q0, LANES)):
        _, vjp = jax.vjp(head, xh, gq)
        dx, dg = vjp(dq[h, 0].T * ATT_SCALE)
        dq0.append(dx)
        dgq = dgq + dg
    for xh, ct in zip(_heads(kn, LANES), _heads(dk, LANES)):
        _, vjp = jax.vjp(head, xh + kr, gk)
        dx, dg = vjp(ct)
        dkn.append(dx)
        dkr = dkr + dx
        dgk = dgk + dg
    return jnp.concatenate(dq0, axis=1), jnp.concatenate(dkn, axis=1), dkr, dgq, dgk


def _b_mem_k(kk, dkm, g):
    dkk = []
    dg = jnp.zeros_like(g)
    for xh, ct in zip(_heads(kk, MEM_HEAD_DIM), _heads(dkm, MEM_HEAD_DIM)):
        _, vjp = jax.vjp(_rms, xh, g)
        dx, dgh = vjp(ct)
        dkk.append(dx)
        dg = dg + dgh
    return jnp.concatenate(dkk, axis=1), dg


def _b_mem_attn(dom, qm0, km, vm, g):
    dq0, dkm, dvm = [], [], []
    dg = jnp.zeros_like(g)
    for h, (xh, doh) in enumerate(zip(_heads(qm0, MEM_HEAD_DIM), _heads(dom, MEM_HEAD_DIM))):
        sl = slice(h * MEM_HEAD_DIM, (h + 1) * MEM_HEAD_DIM)
        kmh, vmh = km[:, sl], vm[:, sl].astype(BF16)
        qn, vjp = jax.vjp(_rms, xh, g)
        p = _mem_probs(qn, kmh)
        dob = doh.astype(BF16)
        dp = lax.dot_general(dob, vmh, NT_DIMS, preferred_element_type=F32)
        ds = (p * (dp - jnp.sum(dp * p, axis=-1, keepdims=True)) * MEM_SCALE).astype(BF16)
        dqn = jnp.dot(ds, kmh, preferred_element_type=F32)
        dkm.append(jnp.dot(ds.T, qn.astype(BF16), preferred_element_type=F32))
        dvm.append(jnp.dot(p.astype(BF16).T, dob, preferred_element_type=F32))
        dx, dgh = vjp(dqn)
        dq0.append(dx)
        dg = dg + dgh
    return (jnp.concatenate(dq0, axis=1), jnp.concatenate(dkm, axis=1),
            jnp.concatenate(dvm, axis=1), dg)


def _adamw(w, g, m, v):
    m = ADAM_B1 * m + (1.0 - ADAM_B1) * g
    v = ADAM_B2 * v + (1.0 - ADAM_B2) * jnp.square(g)
    m_hat = m / (1.0 - ADAM_B1 ** ADAM_STEP)
    v_hat = v / (1.0 - ADAM_B2 ** ADAM_STEP)
    delta = -ADAM_LR * (m_hat / (jnp.sqrt(v_hat) + ADAM_EPS) + ADAM_WD * w)
    return delta, m, v


def _adamw_small(parts, small, tiny, name):
    def body(*refs):
        p_ref, ins, outs = refs[0], refs[1:4 + 3 * len(TINY)], refs[4 + 3 * len(TINY):]
        me = 4 * lax.axis_index("x") + 2 * lax.axis_index("y") + lax.axis_index("c")
        groups = [(0, SMALL_ROWS)]
        groups += [(pl.multiple_of(base + me * rows, 8), rows) for base, (_, _, rows) in zip(TINY_BASE, TINY)]
        for k, (start, rows) in enumerate(groups):
            g = p_ref[0, pl.ds(start, rows), :]
            for d in range(1, N_DEV):
                g = g + p_ref[d, pl.ds(start, rows), :]
            w, m, v = [r[...] for r in ins[3 * k:3 * k + 3]]
            for ref, val in zip(outs[4 * k:4 * k + 4], (g,) + _adamw(w, g, m, v)):
                ref[...] = val

    args = list(small) + [t for grp in tiny for t in grp]
    out_shape = []
    for grp in [small] + list(tiny):
        out_shape += [jax.ShapeDtypeStruct(grp[0].shape, F32)] * 4
    return _call(body, name=name, out_shape=out_shape)(parts, *args)


def _conv_fwd(x, w, name):
    s, ch = x.shape
    kw = w.shape[0]
    halo = -(-(kw - 1) // 8) * 8
    r = CONV_ROWS
    n = s // r

    def chunk(window, wv):
        acc = jnp.zeros((r, LANES), F32)
        for k in range(kw):
            shift = kw - 1 - k
            sh = window if shift == 0 else pltpu.roll(window, shift, 0)
            acc = acc + sh[halo:halo + r] * wv[k:k + 1]
        return acc

    def body(x_ref, w_ref, y_ref):
        wv = w_ref[...]
        first = jnp.concatenate([jnp.zeros((halo, LANES), F32), x_ref[0:r]], axis=0)
        y_ref[0:r] = chunk(first, wv)

        def step(i, carry):
            base = pl.multiple_of(i * r, 8)
            y_ref[pl.ds(base, r)] = chunk(x_ref[pl.ds(base - halo, r + halo)], wv)
            return carry

        lax.fori_loop(1, n, step, 0)

    return _call(
        body, name=name, grid=(ch // LANES,),
        in_specs=[pl.BlockSpec((s, LANES), lambda c: (0, c)), pl.BlockSpec((kw, LANES), lambda c: (0, c))],
        out_specs=pl.BlockSpec((s, LANES), lambda c: (0, c)),
        out_shape=jax.ShapeDtypeStruct((s, ch), F32), compiler_params=_params("parallel"),
    )(x, w)


def _conv_bwd(dy, x, w, name, ex=None):
    s, ch = x.shape
    kw = w.shape[0]
    halo = -(-(kw - 1) // 8) * 8
    r = CONV_ROWS
    n = s // r

    def dx_chunk(window, wv):
        acc = jnp.zeros((r, LANES), F32)
        for k in range(kw):
            shift = kw - 1 - k
            sh = window if shift == 0 else pltpu.roll(window, r + halo - shift, 0)
            acc = acc + sh[0:r] * wv[k:k + 1]
        return acc

    def dw_chunk(xwin, dyc, acc_ref):
        for k in range(kw):
            shift = kw - 1 - k
            sh = xwin if shift == 0 else pltpu.roll(xwin, shift, 0)
            prod = sh[halo:halo + r] * dyc
            acc_ref[k] += jnp.sum(prod.reshape(r // 8, 8, LANES), axis=0)

    grid = (ch // LANES,)

    def body(*refs):
        (dy_ref, x_ref, w_ref), (dx_ref, dw_ref, acc_ref), ex_refs = _split_refs(refs, 3, 2, ex, 1)
        _hosted(ex, ex_refs, grid, True)
        wv = w_ref[...]
        acc_ref[...] = jnp.zeros_like(acc_ref)
        xfirst = jnp.concatenate([jnp.zeros((halo, LANES), F32), x_ref[0:r]], axis=0)
        dw_chunk(xfirst, dy_ref[0:r], acc_ref)
        last = jnp.concatenate([dy_ref[s - r:s], jnp.zeros((halo, LANES), F32)], axis=0)
        dx_ref[s - r:s] = dx_chunk(last, wv)

        def step(i, carry):
            base = pl.multiple_of(i * r, 8)
            dw_chunk(x_ref[pl.ds(base - halo, r + halo)], dy_ref[pl.ds(base, r)], acc_ref)
            prev = pl.multiple_of((i - 1) * r, 8)
            dx_ref[pl.ds(prev, r)] = dx_chunk(dy_ref[pl.ds(prev, r + halo)], wv)
            return carry

        lax.fori_loop(1, n, step, 0)
        dw_ref[...] = jnp.sum(acc_ref[...], axis=1)
        _hosted(ex, ex_refs, grid, False)

    spec = pl.BlockSpec((s, LANES), lambda c: (0, c))
    wspec = pl.BlockSpec((kw, LANES), lambda c: (0, c))
    return _host_call(
        body, ex, name, grid, [spec, spec, wspec], [spec, wspec],
        [jax.ShapeDtypeStruct((s, ch), F32), jax.ShapeDtypeStruct((kw, ch), F32)], (dy, x, w),
        scratch=[pltpu.VMEM((kw, 8, LANES), F32)])


HALO = 8


def _conv3(win, w, rows):
    return (pltpu.roll(win, 2, 0)[HALO:HALO + rows] * w[0:1] + pltpu.roll(win, 1, 0)[HALO:HALO + rows] * w[1:2]
            + win[HALO:HALO + rows] * w[2:3])


def _ffn_mid_bwd(up_g, up_v, dy, w_dn, w_g, w_v, b_g, b_v, name):
    s, width = up_g.shape
    tm, tc = TM, FF_PAD
    n_row = s // tm

    def body(dy_ref, ndy_ref, wd_ref, pg_ref, g_ref, ng_ref, pv_ref, v_ref, nv_ref, wg_ref, wv_ref,
             bg_ref, bv_ref, dug_ref, duv_ref, dwg_ref, dwv_ref, dbg_ref, dbv_ref):
        i = pl.program_id(1)
        first = (i > 0).astype(F32)
        last = (i < n_row - 1).astype(F32)
        ext = tm + HALO
        wg, wv, wd = wg_ref[...], wv_ref[...], wd_ref[...]
        ndy = jnp.concatenate([ndy_ref[...] * last, jnp.zeros((HALO, D_MODEL), F32)], axis=0)
        d_ext = jnp.concatenate([
            lax.dot_general(dy_ref[...].astype(BF16), wd, NT_DIMS, preferred_element_type=F32),
            lax.dot_general(ndy.astype(BF16), wd, NT_DIMS, preferred_element_type=F32)[:HALO]], axis=0)
        xg = jnp.concatenate([pg_ref[...] * first, g_ref[...], ng_ref[...]], axis=0)
        xv = jnp.concatenate([pv_ref[...] * first, v_ref[...], nv_ref[...]], axis=0)
        taps = [[pltpu.roll(x, 2, 0), pltpu.roll(x, 1, 0), x] for x in (xg, xv)]
        conv = [sum(t[HALO:HALO + ext] * w[k:k + 1] for k, t in enumerate(tp)) for tp, w in zip(taps, (wg, wv))]
        _, vjp = jax.vjp(lambda cg_, cv_: _act(cg_, cv_, bg_ref[...], bv_ref[...]), *conv)
        dcg, dcv = vjp(d_ext)
        results = []
        for tp, w, dc in ((taps[0], wg, dcg), (taps[1], wv, dcv)):
            dup = (dc[:tm] * w[2:3] + pltpu.roll(dc, ext - 1, 0)[:tm] * w[1:2]
                   + pltpu.roll(dc, ext - 2, 0)[:tm] * w[0:1])
            own = dc[:tm]
            dw = jnp.concatenate(
                [jnp.sum(own * t[HALO:HALO + tm], axis=0, keepdims=True) for t in tp], axis=0)
            results.append((dup, dw, jnp.sum(own, axis=0, keepdims=True)))
        (dug, dwg, dbg), (duv, dwv, dbv) = results
        dug_ref[...] = dug.astype(dug_ref.dtype)
        duv_ref[...] = duv.astype(duv_ref.dtype)

        @pl.when(i == 0)
        def _():
            dwg_ref[...], dwv_ref[...], dbg_ref[...], dbv_ref[...] = dwg, dwv, dbg, dbv

        @pl.when(i > 0)
        def _():
            dwg_ref[...] += dwg
            dwv_ref[...] += dwv
            dbg_ref[...] += dbg
            dbv_ref[...] += dbv

    per = tm // HALO
    tile = pl.BlockSpec((tm, tc), lambda c, i: (i, c))
    prev = pl.BlockSpec((HALO, tc), lambda c, i: (jnp.maximum(i * per - 1, 0), c))
    nxt = pl.BlockSpec((HALO, tc), lambda c, i: (jnp.minimum((i + 1) * per, s // HALO - 1), c))
    wspec = pl.BlockSpec((FFN_CONV_WIDTH, tc), lambda c, i: (0, c))
    bspec = pl.BlockSpec((1, tc), lambda c, i: (0, c))
    wide = jax.ShapeDtypeStruct((s, width), BF16)
    dy_tile = pl.BlockSpec((tm, D_MODEL), lambda c, i: (i, 0))
    dy_next = pl.BlockSpec((HALO, D_MODEL), lambda c, i: (jnp.minimum((i + 1) * per, s // HALO - 1), 0))
    return _call(
        body, name=name, grid=(width // tc, n_row),
        in_specs=[dy_tile, dy_next, pl.BlockSpec((tc, D_MODEL), lambda c, i: (c, 0)),
                  prev, tile, nxt, prev, tile, nxt, wspec, wspec, bspec, bspec],
        out_specs=[tile, tile, wspec, wspec, bspec, bspec],
        out_shape=[wide, wide] + [jax.ShapeDtypeStruct((FFN_CONV_WIDTH, width), F32)] * 2
        + [jax.ShapeDtypeStruct((1, width), F32)] * 2,
        compiler_params=_params("parallel", "arbitrary"),
    )(dy, dy, w_dn, up_g, up_g, up_g, up_v, up_v, up_v, w_g, w_v, b_g, b_v)


def _ffn_fwd(h3, w_up, w_g, w_v, b_g, b_v, w_dn, x2, target, name):
    s = h3.shape[0]
    tm, tc = TM, FF_PAD
    n_col = D_FF_PAD // tc

    def body(h_ref, wug_ref, wuv_ref, wg_ref, wv_ref, bg_ref, bv_ref, wd_ref, x_ref, t_ref,
             ug_ref, uv_ref, act_ref, dy_ref, sq_ref, halo_g, halo_v, y_acc):
        i, c = pl.program_id(0), pl.program_id(1)
        h = h_ref[...]
        up_g = jnp.dot(h, wug_ref[...], preferred_element_type=F32)
        up_v = jnp.dot(h, wuv_ref[...], preferred_element_type=F32)
        ug_ref[...] = up_g
        uv_ref[...] = up_v
        has_prev = i > 0
        prev_g = jnp.where(has_prev, halo_g[c], 0.0)
        prev_v = jnp.where(has_prev, halo_v[c], 0.0)
        halo_g[c] = up_g[tm - HALO:]
        halo_v[c] = up_v[tm - HALO:]
        cg = _conv3(jnp.concatenate([prev_g, up_g], axis=0), wg_ref[...], tm)
        cv = _conv3(jnp.concatenate([prev_v, up_v], axis=0), wv_ref[...], tm)
        act = _act(cg, cv, bg_ref[...], bv_ref[...]).astype(BF16)
        act_ref[...] = act
        part = jnp.dot(act, wd_ref[...], preferred_element_type=F32)

        @pl.when(c == 0)
        def _():
            y_acc[...] = part

        @pl.when(c > 0)
        def _():
            y_acc[...] += part

        @pl.when(c == n_col - 1)
        def _():
            dy, sq = _f_loss(x_ref[...] + y_acc[...], t_ref[...])
            dy_ref[...] = dy

            @pl.when(i == 0)
            def _():
                sq_ref[...] = sq

            @pl.when(i > 0)
            def _():
                sq_ref[...] += sq

    row = lambda w: pl.BlockSpec((tm, w), lambda i, c: (i, 0))
    tile = pl.BlockSpec((tm, tc), lambda i, c: (i, c))
    wspec = pl.BlockSpec((FFN_CONV_WIDTH, tc), lambda i, c: (0, c))
    bspec = pl.BlockSpec((1, tc), lambda i, c: (0, c))
    wide = jax.ShapeDtypeStruct((s, D_FF_PAD), F32)
    return _call(
        body, name=name, grid=(s // tm, n_col),
        in_specs=[row(D_MODEL),
                  pl.BlockSpec((None, D_MODEL, tc), lambda i, c: (c, 0, 0)),
                  pl.BlockSpec((None, D_MODEL, tc), lambda i, c: (n_col + c, 0, 0)),
                  wspec, wspec, bspec, bspec, pl.BlockSpec((tc, D_MODEL), lambda i, c: (c, 0)),
                  row(D_MODEL), row(D_MODEL)],
        out_specs=[tile, tile, tile, row(D_MODEL), pl.BlockSpec((1, D_MODEL), lambda i, c: (0, 0))],
        out_shape=[wide, wide, jax.ShapeDtypeStruct((s, D_FF_PAD), BF16),
                   jax.ShapeDtypeStruct((s, D_MODEL), F32), jax.ShapeDtypeStruct((1, D_MODEL), F32)],
        scratch_shapes=[pltpu.VMEM((n_col, HALO, tc), F32), pltpu.VMEM((n_col, HALO, tc), F32),
                        pltpu.VMEM((tm, D_MODEL), F32)],
        compiler_params=_params("arbitrary", "arbitrary"),
    )(h3, w_up, w_up, w_g, w_v, b_g, b_v, w_dn, x2, target)


def _chunk_mask(rows_are_queries):
    a = lax.broadcasted_iota(jnp.int32, (TQ, TQ), 0) // CHUNK
    b = lax.broadcasted_iota(jnp.int32, (TQ, TQ), 1) // CHUNK
    return (b <= a) if rows_are_queries else (a <= b)


def _head_lanes(hh):
    return slice(hh * LANES, (hh + 1) * LANES)


def _to_row(col):
    return jnp.broadcast_to(col, (TQ, LANES)).T[0:1, :]


def _flash_specs(s, heads=HEADS_PER_STEP):
    width = heads * LANES
    tile = pl.BlockSpec((TQ, width), lambda h, i: (i, h))
    whole = pl.BlockSpec((s, width), lambda h, i: (0, h))
    row_tile = pl.BlockSpec((heads, 1, 1, TQ), lambda h, i: (h, i, 0, 0))
    row_whole = pl.BlockSpec((heads, s // TQ, 1, TQ), lambda h, i: (h, 0, 0, 0))
    return tile, whole, row_tile, row_whole


def _split_refs(refs, n_in, n_out, ex, n_scratch=0):
    e_in, e_out = (len(ex.inputs), len(ex.out_shape)) if ex else (0, 0)
    a, b, c = n_in + e_in, n_in + e_in + n_out, n_in + e_in + n_out + e_out
    return refs[:n_in], refs[a:b] + refs[c:c + n_scratch], (refs[n_in:a], refs[b:c], refs[c + n_scratch:])


def _hosted(ex, ex_refs, grid, when_first):
    if ex is None:
        return
    ids = [pl.program_id(d) for d in range(len(grid))]
    cond = functools.reduce(
        lambda p, q_: p & q_, [i == (0 if when_first else g - 1) for i, g in zip(ids, grid)])

    @pl.when(cond)
    def _():
        (ex.start if when_first else ex.finish)(*ex_refs)


def _host_call(body, ex, name, grid, in_specs, out_specs, out_shape, args, scratch=()):
    e_in, e_out = (len(ex.inputs), len(ex.out_shape)) if ex else (0, 0)
    res = _call(
        body, name=name, grid=grid, in_specs=list(in_specs) + [ANY] * e_in,
        out_specs=list(out_specs) + [ANY] * e_out,
        out_shape=list(out_shape) + (ex.out_shape if ex else []),
        scratch_shapes=list(scratch) + (ex.scratch if ex else []),
        compiler_params=_params(*["arbitrary"] * len(grid)),
    )(*args, *(ex.inputs if ex else []))
    return res[:len(out_shape)], res[len(out_shape):]


def _flash_fwd(q, k, v, name, ex=None):
    s = q.shape[0]
    nq = s // TQ
    heads = FWD_HEADS
    grid = (MLA_HEADS // heads, nq)

    def body(*refs):
        (q_ref, k_ref, v_ref), (o_ref, lse_row_ref), ex_refs = _split_refs(refs, 3, 2, ex)
        _hosted(ex, ex_refs, grid, True)
        i = pl.program_id(1)
        qs = [q_ref[:, _head_lanes(hh)] for hh in range(heads)]

        def scores(j, hh):
            kj = k_ref[pl.ds(pl.multiple_of(j * TQ, TQ), TQ), _head_lanes(hh)]
            return lax.dot_general(kj, qs[hh], NT_DIMS, preferred_element_type=F32)

        def update(j, sc, m_prev, l_prev, acc, hh):
            vt = v_ref[pl.ds(pl.multiple_of(j * TQ, TQ), TQ), _head_lanes(hh)].T
            m_new = jnp.maximum(m_prev, jnp.max(sc, axis=0, keepdims=True))
            alpha = jnp.exp2(m_prev - m_new)
            p = jnp.exp2(sc - m_new)
            l_new = alpha * l_prev + jnp.sum(p, axis=0, keepdims=True)
            acc = acc * alpha + jnp.dot(vt, p.astype(BF16), preferred_element_type=F32)
            return m_new, l_new, acc

        def step(j, carry):
            out = []
            for hh in range(heads):
                sc, m_prev, l_prev, acc = carry[hh]
                out.append((scores(j + 1, hh),) + update(j, sc, m_prev, l_prev, acc, hh))
            return tuple(out)

        init = tuple((scores(0, hh), jnp.full((1, TQ), NEG, F32), jnp.zeros((1, TQ), F32),
                      jnp.zeros((LANES, TQ), F32)) for hh in range(heads))
        carry = lax.fori_loop(0, i, step, init)
        for hh, (sc, m_prev, l_prev, acc) in enumerate(carry):
            sc = jnp.where(_chunk_mask(False), sc, NEG)
            m_fin, l_fin, acc = update(i, sc, m_prev, l_prev, acc, hh)
            o_ref[:, _head_lanes(hh)] = (acc / l_fin).T
            lse_row_ref[hh, 0] = m_fin + jnp.log2(l_fin)
        _hosted(ex, ex_refs, grid, False)

    tile, whole, row_tile, _ = _flash_specs(s, heads)
    return _host_call(
        body, ex, name, grid, [tile, whole, whole], [tile, row_tile],
        [jax.ShapeDtypeStruct((s, MLA_HEADS * LANES), F32),
         jax.ShapeDtypeStruct((MLA_HEADS, nq, 1, TQ), F32)], (q, k, v))


def _attn_delta(do, o, name):
    s = do.shape[0]

    def body(do_ref, o_ref, d_ref):
        for h in range(MLA_HEADS):
            prod = do_ref[:, _head_lanes(h)] * o_ref[:, _head_lanes(h)]
            d_ref[h, 0] = _to_row(jnp.sum(prod, axis=-1, keepdims=True))

    tile = pl.BlockSpec((TQ, MLA_HEADS * LANES), lambda i: (i, 0))
    return _call(
        body, name=name, grid=(s // TQ,), in_specs=[tile, tile],
        out_specs=pl.BlockSpec((MLA_HEADS, 1, 1, TQ), lambda i: (0, i, 0, 0)),
        out_shape=jax.ShapeDtypeStruct((MLA_HEADS, s // TQ, 1, TQ), F32),
        compiler_params=_params("parallel"),
    )(do, o)


def _flash_bwd(q, k, v, do, lse_row, delta_row, name, ex=None):
    s = q.shape[0]
    nq = s // TQ
    grid = (MLA_HEADS // HEADS_PER_STEP, nq)

    def body(*refs):
        ins, (dqt_ref, dk_ref, dv_ref), ex_refs = _split_refs(refs, 6, 3, ex)
        q_ref, k_ref, v_ref, do_ref, lse_row_ref, delta_row_ref = ins
        _hosted(ex, ex_refs, grid, True)
        j = pl.program_id(1)

        @pl.when(j == 0)
        def _():
            dqt_ref[...] = jnp.zeros_like(dqt_ref)

        kjs = [k_ref[:, _head_lanes(hh)] for hh in range(HEADS_PER_STEP)]
        vjs = [v_ref[:, _head_lanes(hh)] for hh in range(HEADS_PER_STEP)]
        kts = [kj.T for kj in kjs]

        def step(i, carry, masked):
            base = pl.multiple_of(i * TQ, TQ)
            out = []
            for hh in range(HEADS_PER_STEP):
                dk, dv = carry[hh]
                qi = q_ref[pl.ds(base, TQ), _head_lanes(hh)]
                dob = do_ref[pl.ds(base, TQ), _head_lanes(hh)].astype(BF16)
                sc_t = lax.dot_general(kjs[hh], qi, NT_DIMS, preferred_element_type=F32)
                if masked:
                    sc_t = jnp.where(_chunk_mask(False), sc_t, NEG)
                p_t = jnp.exp2(sc_t - lse_row_ref[hh, i])
                dv = dv + jnp.dot(p_t.astype(BF16), dob, preferred_element_type=F32)
                dp_t = lax.dot_general(vjs[hh], dob, NT_DIMS, preferred_element_type=F32)
                ds_t = (p_t * (dp_t - delta_row_ref[hh, i])).astype(BF16)
                dk = dk + jnp.dot(ds_t, qi, preferred_element_type=F32)
                dqt_ref[hh, i] += jnp.dot(kts[hh], ds_t, preferred_element_type=F32)
                out.append((dk, dv))
            return tuple(out)

        zero = jnp.zeros((TQ, LANES), F32)
        carry = step(j, tuple((zero, zero) for _ in range(HEADS_PER_STEP)), True)
        carry = lax.fori_loop(j + 1, nq, functools.partial(step, masked=False), carry)
        for hh, (dk, dv) in enumerate(carry):
            dk_ref[:, _head_lanes(hh)] = dk
            dv_ref[:, _head_lanes(hh)] = dv.astype(dv_ref.dtype)
        _hosted(ex, ex_refs, grid, False)

    tile, whole, _, row_whole = _flash_specs(s)
    dqt_spec = pl.BlockSpec((HEADS_PER_STEP, nq, LANES, TQ), lambda h, j: (h, 0, 0, 0))
    wide = lambda dt: jax.ShapeDtypeStruct((s, MLA_HEADS * LANES), dt)
    return _host_call(
        body, ex, name, grid, [whole, tile, tile, whole, row_whole, row_whole], [dqt_spec, tile, tile],
        [jax.ShapeDtypeStruct((MLA_HEADS, nq, LANES, TQ), F32), wide(F32), wide(BF16)],
        (q, k, v, do, lse_row, delta_row))


def _side_by_side(g):
    return g.transpose(1, 0, 2).reshape(g.shape[1], N_DEV * g.shape[2])


def _col_shards(g):
    return g.reshape(g.shape[0], N_DEV, g.shape[1] // N_DEV).transpose(1, 0, 2)


def _pad_last(v, to):
    return jnp.pad(v, [(0, 0)] * (v.ndim - 1) + [(0, to - v.shape[-1])])


def _tiny_rows(v, rows):
    flat = v.reshape(v.shape[:-2] + (-1,))
    return _pad_last(flat, rows * LANES).reshape(v.shape[:-2] + (rows, LANES))


def _pack_small(vals):
    parts = []
    for (n, size), pad in zip(SMALL, SMALL_PAD):
        parts.append(jnp.pad(vals[n].reshape(-1), (0, pad - size)))
    flat = jnp.concatenate(parts)
    return jnp.pad(flat, (0, SMALL_ROWS * LANES - flat.shape[0])).reshape(SMALL_ROWS, LANES)


def _unpack_small(packed):
    flat = packed.reshape(-1)
    out, off = {}, 0
    for (n, size), pad in zip(SMALL, SMALL_PAD):
        out[n] = flat[off:off + size].reshape(1, size)
        off += pad
    return out


def _pad_heads(w, per_head, axis):
    shape = list(w.shape)
    shape[axis:axis + 1] = [MLA_HEADS, per_head]
    w = w.reshape(shape)
    pad = [(0, 0)] * len(shape)
    pad[axis + 1] = (0, LANES - per_head)
    w = jnp.pad(w, pad)
    shape[axis:axis + 2] = [MLA_HEADS * LANES]
    return w.reshape(shape)


def _unpad_heads(w, per_head, axis):
    shape = list(w.shape)
    shape[axis:axis + 1] = [MLA_HEADS, LANES]
    w = w.reshape(shape)
    w = lax.slice_in_dim(w, 0, per_head, axis=axis + 1)
    shape[axis:axis + 2] = [MLA_HEADS * per_head]
    return w.reshape(shape)


def _row(v, pad_to=None):
    v = v.reshape(1, -1)
    if pad_to is not None:
        v = jnp.pad(v, ((0, 0), (0, pad_to - v.shape[1])))
    return v


def kernel(x, mem, positions, mix_norm_g, w_in, b_conv_in, w_conv_dw, b_conv_dw, conv_ln_g, conv_ln_b, q_lat_norm_g, w_uq, kv_lat_norm_g, w_ukv, q_norm_g, k_norm_g, w_out, mem_norm_x_g, mem_norm_m_g, w_mem_q, w_mem_kv, mem_q_norm_g, mem_k_norm_g, w_mem_o, ffn_norm_g, w_up, w_ffn_dw, b_ffn_dw, w_down, loss_target, m_mix_norm_g, m_w_in, m_b_conv_in, m_w_conv_dw, m_b_conv_dw, m_conv_ln_g, m_conv_ln_b, m_q_lat_norm_g, m_w_uq, m_kv_lat_norm_g, m_w_ukv, m_q_norm_g, m_k_norm_g, m_w_out, m_mem_norm_x_g, m_mem_norm_m_g, m_w_mem_q, m_w_mem_kv, m_mem_q_norm_g, m_mem_k_norm_g, m_w_mem_o, m_ffn_norm_g, m_w_up, m_w_ffn_dw, m_b_ffn_dw, m_w_down, v_mix_norm_g, v_w_in, v_b_conv_in, v_w_conv_dw, v_b_conv_dw, v_conv_ln_g, v_conv_ln_b, v_q_lat_norm_g, v_w_uq, v_kv_lat_norm_g, v_w_ukv, v_q_norm_g, v_k_norm_g, v_w_out, v_mem_norm_x_g, v_mem_norm_m_g, v_w_mem_q, v_w_mem_kv, v_mem_q_norm_g, v_mem_k_norm_g, v_w_mem_o, v_ffn_norm_g, v_w_up, v_w_ffn_dw, v_b_ffn_dw, v_w_down):
    a = dict(locals())
    seq = x.shape[1]
    xs = x.reshape(seq, D_MODEL)
    mems = mem.reshape(-1, D_MODEL)
    target = loss_target.reshape(seq, D_MODEL)

    tiny = [n for n, _, _ in TINY]
    shard = lambda n: a[n][0] if n in tiny else a[n][0].astype(BF16)
    pos = jnp.stack([2 * lax.axis_index("x") + lax.axis_index("y"), lax.axis_index("c")]).astype(jnp.int32)
    ag_first = ["w_in", "w_uq", "w_ukv", "w_conv_dw"]
    ag_later = [n for n in [b for b, _ in BIG] + tiny if n not in ag_first]
    wg = dict(zip(ag_first, _all_gather([shard(n) for n in ag_first], "ag_weights_first")))
    wi = _side_by_side(wg["w_in"])
    s3 = 2 * CONV_CH + MLA_Q_RANK + MLA_KV_RANK
    w_in_p = jnp.concatenate([
        wi[:, :s3], jnp.zeros((D_MODEL, MLA_NOPE), BF16), wi[:, s3:],
        jnp.zeros((D_MODEL, LANES - MLA_QK), BF16)], axis=1)
    w_uq_p = _side_by_side(_pad_last(wg["w_uq"], LANES))
    w_uk_p = _side_by_side(_pad_last(wg["w_ukv"][:, :, :MLA_NOPE], LANES))
    w_uv_p = _side_by_side(_pad_last(wg["w_ukv"][:, :, MLA_NOPE:], LANES))
    w_cdw = _side_by_side(wg["w_conv_dw"])

    g_mix, g_qlat, g_kvlat = _row(mix_norm_g), _row(q_lat_norm_g), _row(kv_lat_norm_g)
    b_in = _row(b_conv_in)
    b_in_a, b_in_g = b_in[:, :CONV_CH], b_in[:, CONV_CH:]
    b_cdw, ln_g, ln_b = _row(b_conv_dw), _row(conv_ln_g), _row(conv_ln_b)
    g_q, g_k = _row(q_norm_g, LANES), _row(k_norm_g, LANES)
    g_memx, g_memm = _row(mem_norm_x_g), _row(mem_norm_m_g)
    g_mq, g_mk, g_ffn = _row(mem_q_norm_g), _row(mem_k_norm_g), _row(ffn_norm_g)
    b_f = _pad_last(b_ffn_dw.reshape(N_DEV, FF_SHARD), FF_PAD)
    b_f_g, b_f_v = b_f[:4].reshape(1, D_FF_PAD), b_f[4:].reshape(1, D_FF_PAD)

    freq = ROPE_THETA ** (-jnp.arange(0, MLA_ROPE, 2, dtype=F32) / MLA_ROPE)
    inv_freq = jnp.concatenate([jnp.zeros((MLA_NOPE,), F32), freq, freq,
                                jnp.zeros((LANES - MLA_QK,), F32)]).reshape(1, LANES)
    cos, sin_a, sin_b = _rowwise(_f_rope_tab, [positions.reshape(seq, 1)], [inv_freq],
                                 [(LANES, F32)] * 3, [], name="rope_tables")

    (h1,) = _rowwise(_f_rms, [xs], [g_mix], [(D_MODEL, BF16)], [], name="rms_mix")
    z = _mm(h1, w_in_p, name="mm_in")
    z_rows = [(z, CONV_CH, 0, 0), (z, CONV_CH, 1, 0), (z, MLA_Q_RANK, 4, 0), (z, MLA_KV_RANK, 10, 0)]
    z_kr = (z, LANES, 11, 0)
    u0, cqn, ckvn = _rowwise(
        _mix_pre, z_rows, [b_in_a, b_in_g, g_qlat, g_kvlat],
        [(CONV_CH, F32), (MLA_Q_RANK, BF16), (MLA_KV_RANK, BF16)], [], name="mix_pre")
    c1 = _conv_fwd(u0, w_cdw, "conv31_fwd")
    (u,) = _rowwise(lambda c, b, g, bb: (_ln_silu(c, b, g, bb),), [c1], [b_cdw, ln_g, ln_b],
                    [(CONV_CH, BF16)], [], name="ln_silu")
    q0 = _mm(cqn, w_uq_p, name="mm_uq")
    kn0 = _mm(ckvn, w_uk_p, name="mm_uk")
    v0 = _mm(ckvn, w_uv_p, out_dtype=BF16, name="mm_uv")
    qk_rows = [q0, kn0, z_kr, cos, sin_a, sin_b]
    qh, kh = _rowwise(_f_qk_prep, qk_rows, [g_q, g_k],
                      [(MLA_HEADS * LANES, BF16)] * 2, [], name="qk_prep")
    (attn, lse_row), later = _flash_fwd(
        qh, kh, v0, "flash_fwd", _plan_all_gather([shard(n) for n in ag_later]))
    wg.update(zip(ag_later, later))
    w_out = wg["w_out"].reshape(D_MODEL, D_MODEL)
    w_out_u = w_out[:CONV_CH]
    w_out_a = _pad_heads(w_out[CONV_CH:], MLA_V, 0)
    w_mq, w_mo = wg["w_mem_q"].reshape(D_MODEL, D_MODEL), wg["w_mem_o"].reshape(D_MODEL, D_MODEL)
    w_mkv = _side_by_side(wg["w_mem_kv"])
    w_up_p = _pad_last(wg["w_up"], FF_PAD)
    w_dn = jnp.pad(wg["w_down"].reshape(4, FF_SHARD, D_MODEL),
                   ((0, 0), (0, FF_PAD - FF_SHARD), (0, 0))).reshape(D_FF_PAD, D_MODEL)
    w_fdw = _pad_last(wg["w_ffn_dw"], FF_PAD)
    w_fdw_g = w_fdw[:4].transpose(1, 0, 2).reshape(FFN_CONV_WIDTH, D_FF_PAD)
    w_fdw_v = w_fdw[4:].transpose(1, 0, 2).reshape(FFN_CONV_WIDTH, D_FF_PAD)
    x1 = _mm(u, w_out_u, add=xs, name="mm_out_u")
    x1 = _mm(attn, w_out_a, add=x1, name="mm_out_a")

    (hq,) = _rowwise(_f_rms, [x1], [g_memx], [(D_MODEL, BF16)], [], name="rms_memx")
    (hm,) = _rowwise(_f_rms, [mems], [g_memm], [(D_MODEL, BF16)], [], name="rms_memm", tm=mems.shape[0])
    qm0 = _mm(hq, w_mq, name="mm_memq")
    kvm0 = _mm(hm, w_mkv, name="mm_memkv", tm=mems.shape[0])
    (km,) = _rowwise(_f_mem_k, [(kvm0, D_MODEL, 0, 0)], [g_mk], [(D_MODEL, BF16)], [],
                     name="mem_k", tm=mems.shape[0])
    vm = kvm0[:, D_MODEL:]
    (om,) = _rowwise(_f_mem_attn, [qm0], [km, vm, g_mq], [(D_MODEL, BF16)], [], name="mem_attn")
    x2 = _mm(om, w_mo, add=x1, name="mm_memo")

    (h3,) = _rowwise(_f_rms, [x2], [g_ffn], [(D_MODEL, BF16)], [], name="rms_ffn")
    up_g, up_v, act, dy, sq = _ffn_fwd(h3, w_up_p, w_fdw_g, w_fdw_v, b_f_g, b_f_v, w_dn, x2, target, "ffn_fwd")
    loss = lax.psum(0.5 * jnp.sum(sq) / D_MODEL, ("x", "y", "c"))

    gw, gs, gt = {}, {}, {}
    gw_dn = _mm_tn(act, dy, name="tn_down").reshape(4, FF_PAD, D_MODEL)
    gw["w_down"] = gw_dn[:, :FF_SHARD].reshape(N_DEV, FF_SHARD // 2, D_MODEL)
    dup_g, dup_v, dwf_g, dwf_v, db_g, db_v = _ffn_mid_bwd(
        up_g, up_v, dy, w_dn, w_fdw_g, w_fdw_v, b_f_g, b_f_v, "ffn_mid_bwd")
    db_f = jnp.concatenate([db_g.reshape(4, FF_PAD), db_v.reshape(4, FF_PAD)], axis=0)
    gs["b_ffn_dw"] = db_f[:, :FF_SHARD].reshape(1, 2 * D_FF)
    dwf = jnp.concatenate([dwf_g.reshape(FFN_CONV_WIDTH, 4, FF_PAD), dwf_v.reshape(FFN_CONV_WIDTH, 4, FF_PAD)], axis=1)
    gt["w_ffn_dw"] = dwf[:, :, :FF_SHARD].transpose(1, 0, 2)
    gw_up = jnp.concatenate([_mm_tn(h3, dup_g, shard_cols=FF_PAD, name="tn_up_g"),
                             _mm_tn(h3, dup_v, shard_cols=FF_PAD, name="tn_up_v")], axis=0)
    gw["w_up"] = gw_up[:, :, :FF_SHARD]
    dh3 = _mm(dup_g, w_up_p, trans_b=True, b_shards=(0, 4), name="mm_up_g_t")
    dh3 = _mm(dup_v, w_up_p, trans_b=True, b_shards=(4, 4), add=dh3, name="mm_up_v_t")
    dx2, gs["ffn_norm_g"] = _rowwise(_b_rms, [x2, dh3, dy], [g_ffn], [(D_MODEL, F32)], [(1, D_MODEL)],
                                     name="rms_ffn_bwd")

    gw["w_mem_o"] = _mm_tn(om, dx2, name="tn_memo").reshape(N_DEV, -1, D_MODEL)
    dom = _mm(dx2, w_mo, trans_b=True, out_dtype=BF16, name="mm_memo_t")
    n_mem = mems.shape[0]
    dqm0, dkm, dvm, gs["mem_q_norm_g"] = _rowwise(
        _b_mem_attn, [dom, qm0], [km, vm, g_mq], [(D_MODEL, BF16)],
        [(n_mem, D_MODEL), (n_mem, D_MODEL), (1, MEM_HEAD_DIM)], name="mem_attn_bwd")
    gw["w_mem_q"] = _mm_tn(hq, dqm0, name="tn_memq").reshape(N_DEV, -1, D_MODEL)
    dhq = _mm(dqm0, w_mq, trans_b=True, name="mm_memq_t")
    dx1, gs["mem_norm_x_g"] = _rowwise(_b_rms, [x1, dhq, dx2], [g_memx], [(D_MODEL, F32)],
                                       [(1, D_MODEL)], name="rms_memx_bwd")
    dkk, gs["mem_k_norm_g"] = _rowwise(_b_mem_k, [(kvm0, D_MODEL, 0, 0), dkm], [g_mk],
                                       [(D_MODEL, F32)], [(1, MEM_HEAD_DIM)], name="mem_k_bwd", tm=n_mem)
    dkvm0 = jnp.concatenate([dkk, dvm], axis=1)
    gw["w_mem_kv"] = _col_shards(_mm_tn(hm, dkvm0, name="tn_memkv", ts=n_mem))
    dhm = _mm(dkvm0, w_mkv, trans_b=True, name="mm_memkv_t", tm=n_mem)
    _, gs["mem_norm_m_g"] = _rowwise(_b_rms_nores, [mems, dhm], [g_memm], [(D_MODEL, F32)],
                                     [(1, D_MODEL)], name="rms_memm_bwd", tm=n_mem)

    gw_out_u = _mm_tn(u, dx1, name="tn_out_u")
    gw_out_a = _mm_tn(attn, dx1, name="tn_out_a")
    gw["w_out"] = jnp.concatenate([gw_out_u, _unpad_heads(gw_out_a, MLA_V, 0)], axis=0).reshape(N_DEV, -1, D_MODEL)
    du = _mm(dx1, w_out_u, trans_b=True, name="mm_out_u_t")
    dattn = _mm(dx1, w_out_a, trans_b=True, name="mm_out_a_t")
    dc1, gs["b_conv_dw"], gs["conv_ln_g"], gs["conv_ln_b"] = _rowwise(
        _b_ln_silu, [c1, du], [b_cdw, ln_g, ln_b], [(CONV_CH, F32)], [(1, CONV_CH)] * 3, name="ln_silu_bwd")
    rs_first = ["w_up", "w_down", "w_mem_o", "w_mem_q", "w_mem_kv", "w_out"]
    grads = [gw[n] for n in rs_first]
    (du0, g_cdw), gots = _conv_bwd(dc1, u0, w_cdw, "conv31_bwd", _plan_swap_sibling(grads))
    gt["w_conv_dw"] = _col_shards(g_cdw)
    sums_first = _rs_add(grads, gots, pos, "rs_add_first")
    delta_row = _attn_delta(dattn, attn, "attn_delta")
    (dqt, dkh, dv0), recvs_first = _flash_bwd(
        qh, kh, v0, dattn, lse_row, delta_row, "flash_bwd", _plan_swap_chips(sums_first))
    dqt_row = (dqt, pl.BlockSpec((MLA_HEADS, 1, LANES, TQ), lambda i: (0, i, 0, 0)))
    dq0, dkn0, dkr, dgq, dgk = _rowwise(
        _b_qk_prep, qk_rows + [dqt_row, dkh], [g_q, g_k],
        [(MLA_HEADS * LANES, BF16)] * 2 + [(LANES, F32)], [(1, LANES)] * 2, name="qk_prep_bwd")
    gs["q_norm_g"], gs["k_norm_g"] = dgq[:, :MLA_QK], dgk[:, :MLA_QK]
    gw["w_uq"] = _col_shards(_mm_tn(cqn, dq0, name="tn_uq"))[:, :, :MLA_QK]
    g_uk = _col_shards(_mm_tn(ckvn, dkn0, name="tn_uk"))[:, :, :MLA_NOPE]
    g_uv = _col_shards(_mm_tn(ckvn, dv0, name="tn_uv"))[:, :, :MLA_V]
    gw["w_ukv"] = jnp.concatenate([g_uk, g_uv], axis=2)
    dcqn = _mm(dq0, w_uq_p, trans_b=True, name="mm_uq_t")
    dckvn = _mm(dkn0, w_uk_p, trans_b=True, name="mm_uk_t")
    dckvn = _mm(dv0, w_uv_p, trans_b=True, add=dckvn, name="mm_uv_t")
    dz, dba, dbg, gs["q_lat_norm_g"], gs["kv_lat_norm_g"] = _rowwise(
        _b_mix_pre, z_rows + [du0, dcqn, dckvn, dkr], [b_in_a, b_in_g, g_qlat, g_kvlat],
        [(IN_PAD, BF16)], [(1, CONV_CH)] * 2 + [(1, MLA_Q_RANK), (1, MLA_KV_RANK)], name="mix_pre_bwd")
    gs["b_conv_in"] = jnp.concatenate([dba, dbg], axis=1)
    gw_in = _mm_tn(h1, dz, name="tn_in")
    gw["w_in"] = _col_shards(jnp.concatenate([gw_in[:, :s3], gw_in[:, s3 + MLA_NOPE:s3 + MLA_QK]], axis=1))
    rs_last = [n for n, _ in BIG if n not in rs_first]
    grads = [gw[n] for n in rs_last]
    gots = _run_exchange(_plan_swap_sibling(grads), "rs_sibling_last")
    sums_last = _rs_add(grads, gots, pos, "rs_add_last")
    dh1, recvs_last = _mm(dz, w_in_p, trans_b=True, name="mm_in_t", ex=_plan_swap_chips(sums_last))
    dx, gs["mix_norm_g"] = _rowwise(_b_rms, [xs, dh1, dx1], [g_mix], [(D_MODEL, F32)], [(1, D_MODEL)],
                                    name="rms_mix_bwd")
    big = rs_first + rs_last
    flat = _adamw_big(list(sums_first) + list(sums_last), list(recvs_first) + list(recvs_last),
                      [a[n] for n in big], [a["m_" + n] for n in big], [a["v_" + n] for n in big],
                      pos, "adamw_big")
    res = [{n: flat[4 * i + k] for i, n in enumerate(big)} for k in range(4)]

    part = jnp.concatenate(
        [_pack_small(gs)] + [_tiny_rows(gt[n], rows).reshape(N_DEV * rows, LANES) for n, _, rows in TINY], axis=0)
    (parts,) = _all_gather([part], "ag_small_grads")
    small_in = [_pack_small({n: a[p + n] for n, _ in SMALL}) for p in ("", "m_", "v_")]
    tiny_in = [[_tiny_rows(a[p + n][0], rows) for p in ("", "m_", "v_")] for n, _, rows in TINY]
    flat = _adamw_small(parts, small_in, tiny_in, "adamw_small")
    for k in range(4):
        res[k].update(_unpack_small(flat[k]))
        for i, (n, shape, _) in enumerate(TINY):
            res[k][n] = flat[4 * (i + 1) + k].reshape(-1)[:math.prod(shape)].reshape((1,) + shape)

    return (loss, dx.reshape(1, seq, D_MODEL), *[res[k][n] for k in range(4) for n in WEIGHTS])
```

```python
import functools
import math

import jax
import jax.numpy as jnp
from jax import lax
from jax.experimental import pallas as pl
from jax.experimental.pallas import tpu as pltpu

F32 = jnp.float32
BF16 = jnp.bfloat16
EPS = 1e-6
LANES = 128
N_DEV = 8
D_MODEL = 1024
CONV_CH = 512
CONV_WIDTH = 31
MLA_HEADS = 8
MLA_NOPE = 64
MLA_ROPE = 32
MLA_V = 64
MLA_QK = MLA_NOPE + MLA_ROPE
MLA_Q_RANK = 256
MLA_KV_RANK = 128
ROPE_THETA = 10000.0
IN_COLS = 2 * CONV_CH + MLA_Q_RANK + MLA_KV_RANK + MLA_ROPE
IN_PAD = 2 * CONV_CH + MLA_Q_RANK + MLA_KV_RANK + LANES
MEM_HEADS = 4
MEM_HEAD_DIM = 256
D_FF = 2816
FFN_CONV_WIDTH = 3
CHUNK = 64
ATT_SCALE = 1.0 / math.sqrt(MLA_QK)
LN2 = math.log(2.0)
Q_SCALE = ATT_SCALE / LN2
MEM_SCALE = 1.0 / math.sqrt(MEM_HEAD_DIM)
ADAM_LR, ADAM_B1, ADAM_B2, ADAM_EPS, ADAM_WD, ADAM_STEP = 0.001, 0.9, 0.999, 1e-08, 0.01, 10

TM = 512
MM_ROWS = 1024
TQ = 512
HEADS_PER_STEP = 2
FWD_HEADS = 4
CONV_ROWS = 256
NEG = -1e30
VMEM_LIMIT = 56 * 1024 * 1024

MESH = pl.DeviceIdType.MESH
ANY = pl.BlockSpec(memory_space=pl.ANY)
NT_DIMS = (((1,), (1,)), ((), ()))

BIG = [
    ("w_in", (1024, 180)), ("w_uq", (256, 96)), ("w_ukv", (128, 128)), ("w_out", (128, 1024)),
    ("w_mem_q", (128, 1024)), ("w_mem_kv", (1024, 256)), ("w_mem_o", (128, 1024)),
    ("w_up", (1024, 704)), ("w_down", (352, 1024)),
]
TINY = [("w_conv_dw", (31, 64), 16), ("w_ffn_dw", (3, 704), 24)]
ROW_STEPS = 4
FF_SHARD = D_FF // 4
FF_PAD = 768
D_FF_PAD = 4 * FF_PAD
SMALL = [
    ("mix_norm_g", 1024), ("b_conv_in", 1024), ("b_conv_dw", 512), ("conv_ln_g", 512),
    ("conv_ln_b", 512), ("q_lat_norm_g", 256), ("kv_lat_norm_g", 128), ("q_norm_g", 96),
    ("k_norm_g", 96), ("mem_norm_x_g", 1024), ("mem_norm_m_g", 1024), ("mem_q_norm_g", 256),
    ("mem_k_norm_g", 256), ("ffn_norm_g", 1024), ("b_ffn_dw", 5632),
]
WEIGHTS = [
    "mix_norm_g", "w_in", "b_conv_in", "w_conv_dw", "b_conv_dw", "conv_ln_g", "conv_ln_b",
    "q_lat_norm_g", "w_uq", "kv_lat_norm_g", "w_ukv", "q_norm_g", "k_norm_g", "w_out",
    "mem_norm_x_g", "mem_norm_m_g", "w_mem_q", "w_mem_kv", "mem_q_norm_g", "mem_k_norm_g",
    "w_mem_o", "ffn_norm_g", "w_up", "w_ffn_dw", "b_ffn_dw", "w_down",
]


SMALL_PAD = [(-(-n // LANES)) * LANES for _, n in SMALL]
SMALL_ROWS = -(-sum(SMALL_PAD) // (8 * LANES)) * 8
TINY_BASE = [SMALL_ROWS + N_DEV * sum(r for _, _, r in TINY[:i]) for i in range(len(TINY))]
PART_ROWS = SMALL_ROWS + N_DEV * sum(r for _, _, r in TINY)


def _call(body, **kw):
    return pl.pallas_call(body, **kw)


def _params(*sem):
    return pltpu.CompilerParams(dimension_semantics=sem, vmem_limit_bytes=VMEM_LIMIT)


class _Exchange:
    def __init__(self, inputs, out_shape, scratch, start, finish):
        self.inputs, self.out_shape, self.scratch = list(inputs), list(out_shape), list(scratch)
        self.start, self.finish = start, finish


def _run_exchange(ex, name):
    n_in, n_out = len(ex.inputs), len(ex.out_shape)

    def body(*refs):
        parts = refs[:n_in], refs[n_in:n_in + n_out], refs[n_in + n_out:]
        ex.start(*parts)
        ex.finish(*parts)

    return _call(body, name=name, out_shape=ex.out_shape, in_specs=[ANY] * n_in,
                 out_specs=[ANY] * n_out, scratch_shapes=ex.scratch)(*ex.inputs)


def _plan_all_gather(xs):
    n = len(xs)

    def copies(x_refs, out_refs, sems):
        send_sems, recv_sems, local_sems = sems
        x, y, c = lax.axis_index("x"), lax.axis_index("y"), lax.axis_index("c")
        me, sibling = (x, y, c), (x, y, 1 - c)
        chips = [(1 - x, y), (x, 1 - y), (1 - x, 1 - y)]

        def slot(o, px, py, pc):
            return out_refs[o].at[4 * px + 2 * py + pc]

        def copy(o, k, block, to, src=None):
            return pltpu.make_async_remote_copy(
                src_ref=slot(o, *block) if src is None else src, dst_ref=slot(o, *block),
                send_sem=send_sems.at[o, k], recv_sem=recv_sems.at[o, k],
                device_id=to, device_id_type=MESH)

        mine = [pltpu.make_async_copy(x_refs[o], slot(o, *me), local_sems.at[o]) for o in range(n)]
        first = [copy(o, 0, me, sibling, src=x_refs[o]) for o in range(n)]
        first += [copy(o, 1 + j, me, (*chip, c), src=x_refs[o])
                  for j, chip in enumerate(chips) for o in range(n)]
        return me, sibling, chips, copy, mine, first

    def start(x_refs, out_refs, sems):
        _, _, _, _, mine, first = copies(x_refs, out_refs, sems)
        for cp in mine + first:
            cp.start()

    def finish(x_refs, out_refs, sems):
        me, sibling, chips, copy, mine, first = copies(x_refs, out_refs, sems)
        c = me[2]
        passed = []
        for j, chip in enumerate(chips):
            for o in range(n):
                copy(o, 1 + j, (*chip, c), me).wait_recv()
                passed.append(copy(o, 4 + j, (*chip, c), sibling))
                passed[-1].start()
        for o in range(n):
            copy(o, 0, sibling, me).wait_recv()
        for j, chip in enumerate(chips):
            for o in range(n):
                copy(o, 4 + j, (*chip, 1 - c), me).wait_recv()
        for cp in first + passed:
            cp.wait_send()
        for cp in mine:
            cp.wait()

    return _Exchange(
        xs, [jax.ShapeDtypeStruct((N_DEV,) + v.shape, v.dtype) for v in xs],
        [pltpu.SemaphoreType.DMA((n, 7)), pltpu.SemaphoreType.DMA((n, 7)), pltpu.SemaphoreType.DMA((n,))],
        start, finish)


def _all_gather(xs, name):
    return _run_exchange(_plan_all_gather(xs), name)


def _plan_swap_sibling(grads):
    n = len(grads)

    def copies(g_refs, got_refs, sems):
        send_sems, recv_sems = sems
        x, y, c = lax.axis_index("x"), lax.axis_index("y"), lax.axis_index("c")
        return [
            pltpu.make_async_remote_copy(
                src_ref=g_refs[o].at[2 * chip + 1 - c], dst_ref=got_refs[o].at[chip],
                send_sem=send_sems.at[o, chip], recv_sem=recv_sems.at[o, chip],
                device_id=(x, y, 1 - c), device_id_type=MESH)
            for o in range(n) for chip in range(4)]

    def start(g_refs, got_refs, sems):
        for cp in copies(g_refs, got_refs, sems):
            cp.start()

    def finish(g_refs, got_refs, sems):
        for cp in copies(g_refs, got_refs, sems):
            cp.wait()

    return _Exchange(
        grads, [jax.ShapeDtypeStruct((4,) + g.shape[1:], g.dtype) for g in grads],
        [pltpu.SemaphoreType.DMA((n, 4)), pltpu.SemaphoreType.DMA((n, 4))], start, finish)


def _plan_swap_chips(sums):
    n = len(sums)

    def copies(a_refs, r_refs, sems):
        send_sems, recv_sems = sems
        x, y, c = lax.axis_index("x"), lax.axis_index("y"), lax.axis_index("c")
        peers = [(x, 1 - y), (1 - x, y), (1 - x, 1 - y)]
        return [
            pltpu.make_async_remote_copy(
                src_ref=a_refs[o].at[2 * px + py], dst_ref=r_refs[o].at[k],
                send_sem=send_sems.at[o, k], recv_sem=recv_sems.at[o, k],
                device_id=(px, py, c), device_id_type=MESH)
            for k, (px, py) in enumerate(peers) for o in range(n)]

    def start(a_refs, r_refs, sems):
        for cp in copies(a_refs, r_refs, sems):
            cp.start()

    def finish(a_refs, r_refs, sems):
        for cp in copies(a_refs, r_refs, sems):
            cp.wait()

    return _Exchange(
        sums, [jax.ShapeDtypeStruct((3,) + a.shape[1:], a.dtype) for a in sums],
        [pltpu.SemaphoreType.DMA((n, 3)), pltpu.SemaphoreType.DMA((n, 3))], start, finish)


def _shard_block(shape):
    return (None, shape[-2] // ROW_STEPS, shape[-1])


def _rs_add(grads, gots, pos, name):
    n = len(grads)

    def body(pos_ref, *refs):
        for g_ref, t_ref, o_ref in zip(refs[:n], refs[n:2 * n], refs[2 * n:]):
            o_ref[...] = g_ref[...] + t_ref[...]

    in_specs = [pl.BlockSpec(_shard_block(g.shape), lambda a, t, pos: (2 * a + pos[1], t, 0)) for g in grads]
    in_specs += [pl.BlockSpec(_shard_block(g.shape), lambda a, t, pos: (a, t, 0)) for g in gots]
    return _call(
        body, name=name,
        grid_spec=pltpu.PrefetchScalarGridSpec(
            num_scalar_prefetch=1, grid=(4, ROW_STEPS), in_specs=in_specs,
            out_specs=[pl.BlockSpec(_shard_block(g.shape), lambda a, t, pos: (a, t, 0)) for g in gots]),
        out_shape=[jax.ShapeDtypeStruct(g.shape, g.dtype) for g in gots],
        compiler_params=_params("arbitrary", "arbitrary"),
    )(pos, *grads, *gots)


def _adamw_big(sums, recvs, ws, ms, vs, pos, name):
    n = len(sums)

    def body(pos_ref, *refs):
        ins, outs = refs[:7 * n], refs[7 * n:]
        for o in range(n):
            own, r1, r2, r3, w, m, v = [r[...] for r in ins[7 * o:7 * o + 7]]
            g = ((own + r1) + r2) + r3
            for ref, val in zip(outs[4 * o:4 * o + 4], (g,) + _adamw(w, g, m, v)):
                ref[...] = val

    in_specs, args, out_specs, out_shape = [], [], [], []
    for s_, r_, w_, m_, v_ in zip(sums, recvs, ws, ms, vs):
        blk = _shard_block(w_.shape)
        in_specs.append(pl.BlockSpec(blk, lambda t, pos: (pos[0], t, 0)))
        in_specs += [pl.BlockSpec(blk, lambda t, pos, k=k: (k, t, 0)) for k in range(3)]
        in_specs += [pl.BlockSpec(blk, lambda t, pos: (0, t, 0))] * 3
        args += [s_, r_, r_, r_, w_, m_, v_]
        out_specs += [pl.BlockSpec(blk, lambda t, pos: (0, t, 0))] * 4
        out_shape += [jax.ShapeDtypeStruct(w_.shape, F32)] * 4
    return _call(
        body, name=name,
        grid_spec=pltpu.PrefetchScalarGridSpec(
            num_scalar_prefetch=1, grid=(ROW_STEPS,), in_specs=in_specs, out_specs=out_specs),
        out_shape=out_shape, compiler_params=_params("arbitrary"),
    )(pos, *args)


def _tile(n, prefs):
    for t in prefs:
        if n % t == 0:
            return t
    return n


def _mm(a, b, *, name, trans_b=False, add=None, out_dtype=F32, tm=MM_ROWS, b_kblock=0, ex=None,
        rms_bwd=None):
    m, k = a.shape
    n = b.shape[0] if trans_b else b.shape[1]
    tn = _tile(n, (1536, 1408, 1024, 768, 512, 256, 128))
    tk = _tile(k, (3072, 2048, 1536, 1408, 1024, 768, 512, 256, 128))
    nk = k // tk
    has_add = add is not None
    n_post = 3 if rms_bwd else 0
    assert not rms_bwd or tn == n
    grid = (m // tm, n // tn, nk)

    def body(*refs):
        ins, outs, ex_refs = _split_refs(refs, 2 + has_add + n_post, 1 + bool(rms_bwd), ex, int(nk > 1))
        _hosted(ex, ex_refs, grid, True)
        a_ref, b_ref = ins[0], ins[1]
        add_ref = ins[2] if has_add else None
        o_ref = outs[0]
        av = a_ref[...].astype(BF16)
        bv = b_ref[...].astype(BF16)
        if trans_b:
            part = lax.dot_general(av, bv, NT_DIMS, preferred_element_type=F32)
        else:
            part = jnp.dot(av, bv, preferred_element_type=F32)

        def finish(acc):
            if has_add:
                acc = acc + add_ref[...].astype(F32)
            if rms_bwd:
                x_ref, dres_ref, g_ref = ins[2 + has_add:]
                acc, dg = _b_rms(x_ref[...], acc, dres_ref[...], g_ref[...])
                first_rows = pl.program_id(0) == 0

                @pl.when(first_rows)
                def _():
                    outs[1][...] = dg

                @pl.when(jnp.logical_not(first_rows))
                def _():
                    outs[1][...] += dg
            o_ref[...] = acc.astype(o_ref.dtype)

        if nk == 1:
            finish(part)
        else:
            acc_ref = outs[-1]
            kk = pl.program_id(2)

            @pl.when(kk == 0)
            def _():
                acc_ref[...] = part

            @pl.when(kk > 0)
            def _():
                acc_ref[...] += part

            @pl.when(kk == nk - 1)
            def _():
                finish(acc_ref[...])
        _hosted(ex, ex_refs, grid, False)

    in_specs = [pl.BlockSpec((tm, tk), lambda i, j, kk: (i, kk))]
    if trans_b:
        in_specs.append(pl.BlockSpec((tn, tk), lambda i, j, kk: (j, kk + b_kblock * nk)))
    else:
        in_specs.append(pl.BlockSpec((tk, tn), lambda i, j, kk: (kk, j)))
    args = [a, b]
    tile = pl.BlockSpec((tm, tn), lambda i, j, kk: (i, j))
    out_specs, out_shape = [tile], [jax.ShapeDtypeStruct((m, n), out_dtype)]
    if has_add:
        in_specs.append(tile)
        args.append(add)
    if rms_bwd:
        in_specs += [tile, tile, pl.BlockSpec((1, n), lambda i, j, kk: (0, 0))]
        args += list(rms_bwd)
        out_specs.append(pl.BlockSpec((1, n), lambda i, j, kk: (0, 0)))
        out_shape.append(jax.ShapeDtypeStruct((1, n), F32))
    outs, hosted = _host_call(
        body, ex, name, grid, in_specs, out_specs, out_shape, args,
        scratch=[pltpu.VMEM((tm, tn), F32)] if nk > 1 else [])
    res = tuple(outs) if rms_bwd else outs[0]
    return (res, hosted) if ex else res


def _mm_tn(a, b, *, name, ts=MM_ROWS, shard_cols=None):
    s, m = a.shape
    n = b.shape[1]
    tm = _tile(m, (1408, 1024, 768, 512, 256, 128))
    tn = shard_cols or _tile(n, (1536, 1408, 1024, 768, 512, 256, 128))
    if shard_cols:
        out_spec = pl.BlockSpec((None, tm, tn), lambda i, j, kk: (j, i, 0))
        out_shape = jax.ShapeDtypeStruct((n // tn, m, tn), F32)
    else:
        out_spec = pl.BlockSpec((tm, tn), lambda i, j, kk: (i, j))
        out_shape = jax.ShapeDtypeStruct((m, n), F32)

    def body(a_ref, b_ref, o_ref):
        kk = pl.program_id(2)
        part = jnp.dot(a_ref[...].astype(BF16).T, b_ref[...].astype(BF16),
                       preferred_element_type=F32)

        @pl.when(kk == 0)
        def _():
            o_ref[...] = part

        @pl.when(kk > 0)
        def _():
            o_ref[...] += part

    return _call(
        body, name=name, grid=(m // tm, n // tn, s // ts),
        in_specs=[pl.BlockSpec((ts, tm), lambda i, j, kk: (kk, i)),
                  pl.BlockSpec((ts, tn), lambda i, j, kk: (kk, j))],
        out_specs=out_spec, out_shape=out_shape,
        compiler_params=_params("parallel", "parallel", "arbitrary"),
    )(a, b)


def _rowwise(fn, rows, consts, row_outs, acc_outs, *, name, tm=TM, n_rows=None):
    rows = [r if isinstance(r, tuple) else (r, r.shape[1], 0, 0) for r in rows]
    s = n_rows or rows[0][0].shape[0]
    nr, nc, no, na = len(rows), len(consts), len(row_outs), len(acc_outs)

    def body(*refs):
        r_in, c_in = refs[:nr], refs[nr:nr + nc]
        o_refs, a_refs = refs[nr + nc:nr + nc + no], refs[nr + nc + no:]
        outs = fn(*[r[...] for r in r_in], *[c[...] for c in c_in])
        for r, v in zip(o_refs, outs[:no]):
            r[...] = v.astype(r.dtype)
        if na:
            i = pl.program_id(0)

            @pl.when(i == 0)
            def _():
                for r, v in zip(a_refs, outs[no:]):
                    r[...] = v.astype(F32)

            @pl.when(i > 0)
            def _():
                for r, v in zip(a_refs, outs[no:]):
                    r[...] += v.astype(F32)

    in_specs = [r[1] if isinstance(r[1], pl.BlockSpec) else
                pl.BlockSpec((tm, r[1]), lambda i, cb=r[2], rb=r[3]: (i + rb, cb)) for r in rows]
    in_specs += [pl.BlockSpec(c.shape, lambda i: (0, 0)) for c in consts]
    out_specs = [pl.BlockSpec((tm, w), lambda i: (i, 0)) for w, _ in row_outs]
    out_specs += [pl.BlockSpec(sh, lambda i: (0, 0)) for sh in acc_outs]
    out_shape = [jax.ShapeDtypeStruct((s, w), dt) for w, dt in row_outs]
    out_shape += [jax.ShapeDtypeStruct(sh, F32) for sh in acc_outs]
    return _call(
        body, name=name, grid=(s // tm,), in_specs=in_specs, out_specs=out_specs,
        out_shape=out_shape, compiler_params=_params("arbitrary"),
    )(*[r[0] for r in rows], *consts)


def _rms(x, g, n=None):
    ms = jnp.sum(x * x, axis=-1, keepdims=True) / float(n or x.shape[-1])
    return x * lax.rsqrt(ms + EPS) * g


def _layer_norm(x, g, b):
    mu = jnp.sum(x, axis=-1, keepdims=True) / float(x.shape[-1])
    xc = x - mu
    var = jnp.sum(xc * xc, axis=-1, keepdims=True) / float(x.shape[-1])
    return xc * lax.rsqrt(var + EPS) * g + b


def _silu(x):
    return x * jax.nn.sigmoid(x)


@jax.custom_vjp
def _rope(y, cos, sin_a, sin_b):
    return y * cos + pltpu.roll(y, 112, 1) * sin_a + pltpu.roll(y, 16, 1) * sin_b


def _rope_fwd(y, cos, sin_a, sin_b):
    return _rope(y, cos, sin_a, sin_b), (cos, sin_a, sin_b)


def _rope_bwd(res, ct):
    cos, sin_a, sin_b = res
    dy = ct * cos + pltpu.roll(ct * sin_a, 16, 1) + pltpu.roll(ct * sin_b, 112, 1)
    return dy, jnp.zeros_like(cos), jnp.zeros_like(sin_a), jnp.zeros_like(sin_b)


_rope.defvjp(_rope_fwd, _rope_bwd)


def _qk_head(xh, g, cos, sin_a, sin_b):
    return _rope(_rms(xh, g, MLA_QK), cos, sin_a, sin_b)


def _heads(x, width):
    return [x[:, h * width:(h + 1) * width] for h in range(x.shape[1] // width)]


def _f_rms(x, g):
    return (_rms(x, g),)


def _f_rope_tab(pos, inv_freq):
    ang = pos.astype(F32) * inv_freq
    lane = lax.broadcasted_iota(jnp.int32, ang.shape, 1)
    sn = jnp.sin(ang)
    first = (lane >= MLA_NOPE) & (lane < MLA_NOPE + MLA_ROPE // 2)
    second = (lane >= MLA_NOPE + MLA_ROPE // 2) & (lane < MLA_QK)
    return jnp.cos(ang), jnp.where(first, -sn, 0.0), jnp.where(second, sn, 0.0)


def _mix_pre(za, zg, zcq, zckv, ba, bg, gq, gkv):
    u0 = (za + ba) * jax.nn.sigmoid(zg + bg)
    return u0, _rms(zcq, gq), _rms(zckv, gkv)


def _ln_silu(c1, bdw, lg, lb):
    return _silu(_layer_norm(c1 + bdw, lg, lb))


def _f_qk_prep(q0, kn, kr, cos, sa, sb, gq, gk):
    qs = [_qk_head(xh, gq, cos, sa, sb) * Q_SCALE for xh in _heads(q0, LANES)]
    ks = [_qk_head(xh + kr, gk, cos, sa, sb) for xh in _heads(kn, LANES)]
    return jnp.concatenate(qs, axis=1), jnp.concatenate(ks, axis=1)


def _act(cg, cv, bg, bv):
    return _silu(cg + bg) * (cv + bv)


def _f_mem_k(kk, g):
    return (jnp.concatenate([_rms(xh, g) for xh in _heads(kk, MEM_HEAD_DIM)], axis=1),)


def _mem_probs(qn, kmh):
    s = lax.dot_general(qn.astype(BF16), kmh, NT_DIMS, preferred_element_type=F32) * MEM_SCALE
    e = jnp.exp(s - jnp.max(s, axis=-1, keepdims=True))
    return e / jnp.sum(e, axis=-1, keepdims=True)


def _f_mem_attn(qm0, km, vm, g):
    outs = []
    for h, xh in enumerate(_heads(qm0, MEM_HEAD_DIM)):
        sl = slice(h * MEM_HEAD_DIM, (h + 1) * MEM_HEAD_DIM)
        p = _mem_probs(_rms(xh, g), km[:, sl])
        outs.append(jnp.dot(p.astype(BF16), vm[:, sl].astype(BF16), preferred_element_type=F32))
    return (jnp.concatenate(outs, axis=1),)


def _f_loss(y, t):
    e = y - t
    return e * (1.0 / D_MODEL), jnp.sum(e * e, axis=0, keepdims=True)


def _b_rms(x, dh, dres, g):
    _, vjp = jax.vjp(_rms, x, g)
    dx, dg = vjp(dh)
    return dx + dres, dg


def _b_rms_nores(x, dh, g):
    _, vjp = jax.vjp(_rms, x, g)
    dx, dg = vjp(dh)
    return dx, dg


def _b_mix_pre(za, zg, zcq, zckv, du0, dcqn, dckvn, dkr, ba, bg, gq, gkv):
    _, vjp = jax.vjp(_mix_pre, za, zg, zcq, zckv, ba, bg, gq, gkv)
    dza, dzg, dzcq, dzckv, dba, dbg, dgq, dgkv = vjp((du0, dcqn, dckvn))
    return jnp.concatenate([dza, dzg, dzcq, dzckv, dkr], axis=1), dba, dbg, dgq, dgkv


def _b_ln_silu(c1, du, bdw, lg, lb):
    _, vjp = jax.vjp(_ln_silu, c1, bdw, lg, lb)
    return vjp(du)


def _b_qk_prep(q0, kn, kr, cos, sa, sb, dq, dk, gq, gk):
    head = lambda xh, g: _qk_head(xh, g, cos, sa, sb)
    dk = dk * LN2
    dq0, dkn = [], []
    dkr = jnp.zeros_like(kr)
    dgq = jnp.zeros_like(gq)
    dgk = jnp.zeros_like(gk)
    for h, xh in enumerate(_heads(q0, LANES)):
        _, vjp = jax.vjp(head, xh, gq)
        dx, dg = vjp(dq[h, 0].T * ATT_SCALE)
        dq0.append(dx)
        dgq = dgq + dg
    for xh, ct in zip(_heads(kn, LANES), _heads(dk, LANES)):
        _, vjp = jax.vjp(head, xh + kr, gk)
        dx, dg = vjp(ct)
        dkn.append(dx)
        dkr = dkr + dx
        dgk = dgk + dg
    return jnp.concatenate(dq0, axis=1), jnp.concatenate(dkn, axis=1), dkr, dgq, dgk


def _b_mem_k(kk, dkm, g):
    dkk = []
    dg = jnp.zeros_like(g)
    for xh, ct in zip(_heads(kk, MEM_HEAD_DIM), _heads(dkm, MEM_HEAD_DIM)):
        _, vjp = jax.vjp(_rms, xh, g)
        dx, dgh = vjp(ct)
        dkk.append(dx)
        dg = dg + dgh
    return jnp.concatenate(dkk, axis=1), dg


def _b_mem_attn(dom, qm0, km, vm, g):
    dq0, dkm, dvm = [], [], []
    dg = jnp.zeros_like(g)
    for h, (xh, doh) in enumerate(zip(_heads(qm0, MEM_HEAD_DIM), _heads(dom, MEM_HEAD_DIM))):
        sl = slice(h * MEM_HEAD_DIM, (h + 1) * MEM_HEAD_DIM)
        kmh, vmh = km[:, sl], vm[:, sl].astype(BF16)
        qn, vjp = jax.vjp(_rms, xh, g)
        p = _mem_probs(qn, kmh)
        dob = doh.astype(BF16)
        dp = lax.dot_general(dob, vmh, NT_DIMS, preferred_element_type=F32)
        ds = (p * (dp - jnp.sum(dp * p, axis=-1, keepdims=True)) * MEM_SCALE).astype(BF16)
        dqn = jnp.dot(ds, kmh, preferred_element_type=F32)
        dkm.append(jnp.dot(ds.T, qn.astype(BF16), preferred_element_type=F32))
        dvm.append(jnp.dot(p.astype(BF16).T, dob, preferred_element_type=F32))
        dx, dgh = vjp(dqn)
        dq0.append(dx)
        dg = dg + dgh
    return (jnp.concatenate(dq0, axis=1), jnp.concatenate(dkm, axis=1),
            jnp.concatenate(dvm, axis=1), dg)


def _adamw(w, g, m, v):
    m = ADAM_B1 * m + (1.0 - ADAM_B1) * g
    v = ADAM_B2 * v + (1.0 - ADAM_B2) * jnp.square(g)
    m_hat = m / (1.0 - ADAM_B1 ** ADAM_STEP)
    v_hat = v / (1.0 - ADAM_B2 ** ADAM_STEP)
    delta = -ADAM_LR * (m_hat / (jnp.sqrt(v_hat) + ADAM_EPS) + ADAM_WD * w)
    return delta, m, v


def _adamw_small(parts, small, tiny, name):
    def body(*refs):
        p_ref, ins, outs = refs[0], refs[1:4 + 3 * len(TINY)], refs[4 + 3 * len(TINY):]
        me = 4 * lax.axis_index("x") + 2 * lax.axis_index("y") + lax.axis_index("c")
        groups = [(0, SMALL_ROWS)]
        groups += [(pl.multiple_of(base + me * rows, 8), rows) for base, (_, _, rows) in zip(TINY_BASE, TINY)]
        for k, (start, rows) in enumerate(groups):
            g = p_ref[0, pl.ds(start, rows), :]
            for d in range(1, N_DEV):
                g = g + p_ref[d, pl.ds(start, rows), :]
            w, m, v = [r[...] for r in ins[3 * k:3 * k + 3]]
            for ref, val in zip(outs[4 * k:4 * k + 4], (g,) + _adamw(w, g, m, v)):
                ref[...] = val

    args = list(small) + [t for grp in tiny for t in grp]
    out_shape = []
    for grp in [small] + list(tiny):
        out_shape += [jax.ShapeDtypeStruct(grp[0].shape, F32)] * 4
    return _call(body, name=name, out_shape=out_shape)(parts, *args)


def _conv_fwd(x, w, name):
    s, ch = x.shape
    kw = w.shape[0]
    halo = -(-(kw - 1) // 8) * 8
    r = CONV_ROWS
    n = s // r

    def chunk(window, wv):
        acc = jnp.zeros((r, LANES), F32)
        for k in range(kw):
            shift = kw - 1 - k
            sh = window if shift == 0 else pltpu.roll(window, shift, 0)
            acc = acc + sh[halo:halo + r] * wv[k:k + 1]
        return acc

    def body(x_ref, w_ref, y_ref):
        wv = w_ref[...]
        first = jnp.concatenate([jnp.zeros((halo, LANES), F32), x_ref[0:r]], axis=0)
        y_ref[0:r] = chunk(first, wv)

        def step(i, carry):
            base = pl.multiple_of(i * r, 8)
            y_ref[pl.ds(base, r)] = chunk(x_ref[pl.ds(base - halo, r + halo)], wv)
            return carry

        lax.fori_loop(1, n, step, 0)

    return _call(
        body, name=name, grid=(ch // LANES,),
        in_specs=[pl.BlockSpec((s, LANES), lambda c: (0, c)), pl.BlockSpec((kw, LANES), lambda c: (0, c))],
        out_specs=pl.BlockSpec((s, LANES), lambda c: (0, c)),
        out_shape=jax.ShapeDtypeStruct((s, ch), F32), compiler_params=_params("parallel"),
    )(x, w)


def _conv_bwd(dy, x, w, name, ex=None):
    s, ch = x.shape
    kw = w.shape[0]
    halo = -(-(kw - 1) // 8) * 8
    r = CONV_ROWS
    n = s // r

    def dx_chunk(window, wv):
        acc = jnp.zeros((r, LANES), F32)
        for k in range(kw):
            shift = kw - 1 - k
            sh = window if shift == 0 else pltpu.roll(window, r + halo - shift, 0)
            acc = acc + sh[0:r] * wv[k:k + 1]
        return acc

    def dw_chunk(xwin, dyc, acc_ref):
        for k in range(kw):
            shift = kw - 1 - k
            sh = xwin if shift == 0 else pltpu.roll(xwin, shift, 0)
            prod = sh[halo:halo + r] * dyc
            acc_ref[k] += jnp.sum(prod.reshape(r // 8, 8, LANES), axis=0)

    grid = (ch // LANES,)

    def body(*refs):
        (dy_ref, x_ref, w_ref), (dx_ref, dw_ref, acc_ref), ex_refs = _split_refs(refs, 3, 2, ex, 1)
        _hosted(ex, ex_refs, grid, True)
        wv = w_ref[...]
        acc_ref[...] = jnp.zeros_like(acc_ref)
        xfirst = jnp.concatenate([jnp.zeros((halo, LANES), F32), x_ref[0:r]], axis=0)
        dw_chunk(xfirst, dy_ref[0:r], acc_ref)
        last = jnp.concatenate([dy_ref[s - r:s], jnp.zeros((halo, LANES), F32)], axis=0)
        dx_ref[s - r:s] = dx_chunk(last, wv)

        def step(i, carry):
            base = pl.multiple_of(i * r, 8)
            dw_chunk(x_ref[pl.ds(base - halo, r + halo)], dy_ref[pl.ds(base, r)], acc_ref)
            prev = pl.multiple_of((i - 1) * r, 8)
            dx_ref[pl.ds(prev, r)] = dx_chunk(dy_ref[pl.ds(prev, r + halo)], wv)
            return carry

        lax.fori_loop(1, n, step, 0)
        dw_ref[...] = jnp.sum(acc_ref[...], axis=1)
        _hosted(ex, ex_refs, grid, False)

    spec = pl.BlockSpec((s, LANES), lambda c: (0, c))
    wspec = pl.BlockSpec((kw, LANES), lambda c: (0, c))
    return _host_call(
        body, ex, name, grid, [spec, spec, wspec], [spec, wspec],
        [jax.ShapeDtypeStruct((s, ch), F32), jax.ShapeDtypeStruct((kw, ch), F32)], (dy, x, w),
        scratch=[pltpu.VMEM((kw, 8, LANES), F32)])


HALO = 8


def _conv3(win, w, rows):
    return (pltpu.roll(win, 2, 0)[HALO:HALO + rows] * w[0:1] + pltpu.roll(win, 1, 0)[HALO:HALO + rows] * w[1:2]
            + win[HALO:HALO + rows] * w[2:3])


def _ffn_mid_bwd(up_g, up_v, dy, w_dn, w_g, w_v, b_g, b_v, name):
    s, width = up_g.shape
    tm, tc = TM, FF_PAD
    n_row = s // tm

    def body(dy_ref, ndy_ref, wd_ref, pg_ref, g_ref, ng_ref, pv_ref, v_ref, nv_ref, wg_ref, wv_ref,
             bg_ref, bv_ref, dug_ref, duv_ref, dwg_ref, dwv_ref, dbg_ref, dbv_ref):
        i = pl.program_id(1)
        first = (i > 0).astype(F32)
        last = (i < n_row - 1).astype(F32)
        ext = tm + HALO
        wg, wv, wd = wg_ref[...], wv_ref[...], wd_ref[...]
        ndy = jnp.concatenate([ndy_ref[...] * last, jnp.zeros((HALO, D_MODEL), F32)], axis=0)
        d_ext = jnp.concatenate([
            lax.dot_general(dy_ref[...].astype(BF16), wd, NT_DIMS, preferred_element_type=F32),
            lax.dot_general(ndy.astype(BF16), wd, NT_DIMS, preferred_element_type=F32)[:HALO]], axis=0)
        xg = jnp.concatenate([pg_ref[...] * first, g_ref[...], ng_ref[...]], axis=0)
        xv = jnp.concatenate([pv_ref[...] * first, v_ref[...], nv_ref[...]], axis=0)
        taps = [[pltpu.roll(x, 2, 0), pltpu.roll(x, 1, 0), x] for x in (xg, xv)]
        conv = [sum(t[HALO:HALO + ext] * w[k:k + 1] for k, t in enumerate(tp)) for tp, w in zip(taps, (wg, wv))]
        _, vjp = jax.vjp(lambda cg_, cv_: _act(cg_, cv_, bg_ref[...], bv_ref[...]), *conv)
        dcg, dcv = vjp(d_ext)
        results = []
        for tp, w, dc in ((taps[0], wg, dcg), (taps[1], wv, dcv)):
            dup = (dc[:tm] * w[2:3] + pltpu.roll(dc, ext - 1, 0)[:tm] * w[1:2]
                   + pltpu.roll(dc, ext - 2, 0)[:tm] * w[0:1])
            own = dc[:tm]
            dw = jnp.concatenate(
                [jnp.sum(own * t[HALO:HALO + tm], axis=0, keepdims=True) for t in tp], axis=0)
            results.append((dup, dw, jnp.sum(own, axis=0, keepdims=True)))
        (dug, dwg, dbg), (duv, dwv, dbv) = results
        dug_ref[...] = dug.astype(dug_ref.dtype)
        duv_ref[...] = duv.astype(duv_ref.dtype)

        @pl.when(i == 0)
        def _():
            dwg_ref[...], dwv_ref[...], dbg_ref[...], dbv_ref[...] = dwg, dwv, dbg, dbv

        @pl.when(i > 0)
        def _():
            dwg_ref[...] += dwg
            dwv_ref[...] += dwv
            dbg_ref[...] += dbg
            dbv_ref[...] += dbv

    per = tm // HALO
    tile = pl.BlockSpec((tm, tc), lambda c, i: (i, c))
    prev = pl.BlockSpec((HALO, tc), lambda c, i: (jnp.maximum(i * per - 1, 0), c))
    nxt = pl.BlockSpec((HALO, tc), lambda c, i: (jnp.minimum((i + 1) * per, s // HALO - 1), c))
    wspec = pl.BlockSpec((FFN_CONV_WIDTH, tc), lambda c, i: (0, c))
    bspec = pl.BlockSpec((1, tc), lambda c, i: (0, c))
    wide = jax.ShapeDtypeStruct((s, width), BF16)
    dy_tile = pl.BlockSpec((tm, D_MODEL), lambda c, i: (i, 0))
    dy_next = pl.BlockSpec((HALO, D_MODEL), lambda c, i: (jnp.minimum((i + 1) * per, s // HALO - 1), 0))
    return _call(
        body, name=name, grid=(width // tc, n_row),
        in_specs=[dy_tile, dy_next, pl.BlockSpec((tc, D_MODEL), lambda c, i: (c, 0)),
                  prev, tile, nxt, prev, tile, nxt, wspec, wspec, bspec, bspec],
        out_specs=[tile, tile, wspec, wspec, bspec, bspec],
        out_shape=[wide, wide] + [jax.ShapeDtypeStruct((FFN_CONV_WIDTH, width), F32)] * 2
        + [jax.ShapeDtypeStruct((1, width), F32)] * 2,
        compiler_params=_params("parallel", "arbitrary"),
    )(dy, dy, w_dn, up_g, up_g, up_g, up_v, up_v, up_v, w_g, w_v, b_g, b_v)


def _ffn_fwd(h3, w_up, w_g, w_v, b_g, b_v, w_dn, x2, target, name):
    s = h3.shape[0]
    tm, tc = TM, FF_PAD
    n_col = D_FF_PAD // tc

    def body(h_ref, wug_ref, wuv_ref, wg_ref, wv_ref, bg_ref, bv_ref, wd_ref, x_ref, t_ref,
             ug_ref, uv_ref, act_ref, dy_ref, sq_ref, halo_g, halo_v, y_acc):
        i, c = pl.program_id(0), pl.program_id(1)
        h = h_ref[...]
        up_g = jnp.dot(h, wug_ref[...], preferred_element_type=F32)
        up_v = jnp.dot(h, wuv_ref[...], preferred_element_type=F32)
        ug_ref[...] = up_g
        uv_ref[...] = up_v
        has_prev = i > 0
        prev_g = jnp.where(has_prev, halo_g[c], 0.0)
        prev_v = jnp.where(has_prev, halo_v[c], 0.0)
        halo_g[c] = up_g[tm - HALO:]
        halo_v[c] = up_v[tm - HALO:]
        cg = _conv3(jnp.concatenate([prev_g, up_g], axis=0), wg_ref[...], tm)
        cv = _conv3(jnp.concatenate([prev_v, up_v], axis=0), wv_ref[...], tm)
        act = _act(cg, cv, bg_ref[...], bv_ref[...]).astype(BF16)
        act_ref[...] = act
        part = jnp.dot(act, wd_ref[...], preferred_element_type=F32)

        @pl.when(c == 0)
        def _():
            y_acc[...] = part

        @pl.when(c > 0)
        def _():
            y_acc[...] += part

        @pl.when(c == n_col - 1)
        def _():
            dy, sq = _f_loss(x_ref[...] + y_acc[...], t_ref[...])
            dy_ref[...] = dy

            @pl.when(i == 0)
            def _():
                sq_ref[...] = sq

            @pl.when(i > 0)
            def _():
                sq_ref[...] += sq

    row = lambda w: pl.BlockSpec((tm, w), lambda i, c: (i, 0))
    tile = pl.BlockSpec((tm, tc), lambda i, c: (i, c))
    wspec = pl.BlockSpec((FFN_CONV_WIDTH, tc), lambda i, c: (0, c))
    bspec = pl.BlockSpec((1, tc), lambda i, c: (0, c))
    wide = jax.ShapeDtypeStruct((s, D_FF_PAD), F32)
    return _call(
        body, name=name, grid=(s // tm, n_col),
        in_specs=[row(D_MODEL),
                  pl.BlockSpec((None, D_MODEL, tc), lambda i, c: (c, 0, 0)),
                  pl.BlockSpec((None, D_MODEL, tc), lambda i, c: (n_col + c, 0, 0)),
                  wspec, wspec, bspec, bspec, pl.BlockSpec((tc, D_MODEL), lambda i, c: (c, 0)),
                  row(D_MODEL), row(D_MODEL)],
        out_specs=[tile, tile, tile, row(D_MODEL), pl.BlockSpec((1, D_MODEL), lambda i, c: (0, 0))],
        out_shape=[wide, wide, jax.ShapeDtypeStruct((s, D_FF_PAD), BF16),
                   jax.ShapeDtypeStruct((s, D_MODEL), F32), jax.ShapeDtypeStruct((1, D_MODEL), F32)],
        scratch_shapes=[pltpu.VMEM((n_col, HALO, tc), F32), pltpu.VMEM((n_col, HALO, tc), F32),
                        pltpu.VMEM((tm, D_MODEL), F32)],
        compiler_params=_params("arbitrary", "arbitrary"),
    )(h3, w_up, w_up, w_g, w_v, b_g, b_v, w_dn, x2, target)


def _chunk_mask(rows_are_queries):
    a = lax.broadcasted_iota(jnp.int32, (TQ, TQ), 0) // CHUNK
    b = lax.broadcasted_iota(jnp.int32, (TQ, TQ), 1) // CHUNK
    return (b <= a) if rows_are_queries else (a <= b)


def _head_lanes(hh):
    return slice(hh * LANES, (hh + 1) * LANES)


def _to_row(col):
    return jnp.broadcast_to(col, (TQ, LANES)).T[0:1, :]


def _flash_specs(s, heads=HEADS_PER_STEP):
    width = heads * LANES
    tile = pl.BlockSpec((TQ, width), lambda h, i: (i, h))
    whole = pl.BlockSpec((s, width), lambda h, i: (0, h))
    row_tile = pl.BlockSpec((heads, 1, 1, TQ), lambda h, i: (h, i, 0, 0))
    row_whole = pl.BlockSpec((heads, s // TQ, 1, TQ), lambda h, i: (h, 0, 0, 0))
    return tile, whole, row_tile, row_whole


def _split_refs(refs, n_in, n_out, ex, n_scratch=0):
    e_in, e_out = (len(ex.inputs), len(ex.out_shape)) if ex else (0, 0)
    a, b, c = n_in + e_in, n_in + e_in + n_out, n_in + e_in + n_out + e_out
    return refs[:n_in], refs[a:b] + refs[c:c + n_scratch], (refs[n_in:a], refs[b:c], refs[c + n_scratch:])


def _hosted(ex, ex_refs, grid, when_first):
    if ex is None:
        return
    ids = [pl.program_id(d) for d in range(len(grid))]
    cond = functools.reduce(
        lambda p, q_: p & q_, [i == (0 if when_first else g - 1) for i, g in zip(ids, grid)])

    @pl.when(cond)
    def _():
        (ex.start if when_first else ex.finish)(*ex_refs)


def _host_call(body, ex, name, grid, in_specs, out_specs, out_shape, args, scratch=()):
    e_in, e_out = (len(ex.inputs), len(ex.out_shape)) if ex else (0, 0)
    res = _call(
        body, name=name, grid=grid, in_specs=list(in_specs) + [ANY] * e_in,
        out_specs=list(out_specs) + [ANY] * e_out,
        out_shape=list(out_shape) + (ex.out_shape if ex else []),
        scratch_shapes=list(scratch) + (ex.scratch if ex else []),
        compiler_params=_params(*["arbitrary"] * len(grid)),
    )(*args, *(ex.inputs if ex else []))
    return res[:len(out_shape)], res[len(out_shape):]


def _flash_fwd(q, k, v, name, ex=None):
    s = q.shape[0]
    nq = s // TQ
    heads = FWD_HEADS
    grid = (MLA_HEADS // heads, nq)

    def body(*refs):
        (q_ref, k_ref, v_ref), (o_ref, lse_row_ref), ex_refs = _split_refs(refs, 3, 2, ex)
        _hosted(ex, ex_refs, grid, True)
        i = pl.program_id(1)
        qs = [q_ref[:, _head_lanes(hh)] for hh in range(heads)]

        def scores(j, hh):
            kj = k_ref[pl.ds(pl.multiple_of(j * TQ, TQ), TQ), _head_lanes(hh)]
            return lax.dot_general(kj, qs[hh], NT_DIMS, preferred_element_type=F32)

        def update(j, sc, m_prev, l_prev, acc, hh):
            vt = v_ref[pl.ds(pl.multiple_of(j * TQ, TQ), TQ), _head_lanes(hh)].T
            m_new = jnp.maximum(m_prev, jnp.max(sc, axis=0, keepdims=True))
            alpha = jnp.exp2(m_prev - m_new)
            p = jnp.exp2(sc - m_new)
            l_new = alpha * l_prev + jnp.sum(p, axis=0, keepdims=True)
            acc = acc * alpha + jnp.dot(vt, p.astype(BF16), preferred_element_type=F32)
            return m_new, l_new, acc

        def step(j, carry):
            out = []
            for hh in range(heads):
                sc, m_prev, l_prev, acc = carry[hh]
                out.append((scores(j + 1, hh),) + update(j, sc, m_prev, l_prev, acc, hh))
            return tuple(out)

        init = tuple((scores(0, hh), jnp.full((1, TQ), NEG, F32), jnp.zeros((1, TQ), F32),
                      jnp.zeros((LANES, TQ), F32)) for hh in range(heads))
        carry = lax.fori_loop(0, i, step, init)
        for hh, (sc, m_prev, l_prev, acc) in enumerate(carry):
            sc = jnp.where(_chunk_mask(False), sc, NEG)
            m_fin, l_fin, acc = update(i, sc, m_prev, l_prev, acc, hh)
            o_ref[:, _head_lanes(hh)] = (acc / l_fin).T
            lse_row_ref[hh, 0] = m_fin + jnp.log2(l_fin)
        _hosted(ex, ex_refs, grid, False)

    tile, whole, row_tile, _ = _flash_specs(s, heads)
    return _host_call(
        body, ex, name, grid, [tile, whole, whole], [tile, row_tile],
        [jax.ShapeDtypeStruct((s, MLA_HEADS * LANES), F32),
         jax.ShapeDtypeStruct((MLA_HEADS, nq, 1, TQ), F32)], (q, k, v))


def _attn_delta(do, o, name):
    s = do.shape[0]

    def body(do_ref, o_ref, d_ref):
        for h in range(MLA_HEADS):
            prod = do_ref[:, _head_lanes(h)] * o_ref[:, _head_lanes(h)]
            d_ref[h, 0] = _to_row(jnp.sum(prod, axis=-1, keepdims=True))

    tile = pl.BlockSpec((TQ, MLA_HEADS * LANES), lambda i: (i, 0))
    return _call(
        body, name=name, grid=(s // TQ,), in_specs=[tile, tile],
        out_specs=pl.BlockSpec((MLA_HEADS, 1, 1, TQ), lambda i: (0, i, 0, 0)),
        out_shape=jax.ShapeDtypeStruct((MLA_HEADS, s // TQ, 1, TQ), F32),
        compiler_params=_params("parallel"),
    )(do, o)


def _flash_bwd(q, k, v, do, lse_row, delta_row, name, ex=None):
    s = q.shape[0]
    nq = s // TQ
    grid = (MLA_HEADS // HEADS_PER_STEP, nq)

    def body(*refs):
        ins, (dqt_ref, dk_ref, dv_ref), ex_refs = _split_refs(refs, 6, 3, ex)
        q_ref, k_ref, v_ref, do_ref, lse_row_ref, delta_row_ref = ins
        _hosted(ex, ex_refs, grid, True)
        j = pl.program_id(1)

        @pl.when(j == 0)
        def _():
            dqt_ref[...] = jnp.zeros_like(dqt_ref)

        kjs = [k_ref[:, _head_lanes(hh)] for hh in range(HEADS_PER_STEP)]
        vjs = [v_ref[:, _head_lanes(hh)] for hh in range(HEADS_PER_STEP)]
        kts = [kj.T for kj in kjs]

        def step(i, carry, masked):
            base = pl.multiple_of(i * TQ, TQ)
            out = []
            for hh in range(HEADS_PER_STEP):
                dk, dv = carry[hh]
                qi = q_ref[pl.ds(base, TQ), _head_lanes(hh)]
                dob = do_ref[pl.ds(base, TQ), _head_lanes(hh)].astype(BF16)
                sc_t = lax.dot_general(kjs[hh], qi, NT_DIMS, preferred_element_type=F32)
                if masked:
                    sc_t = jnp.where(_chunk_mask(False), sc_t, NEG)
                p_t = jnp.exp2(sc_t - lse_row_ref[hh, i])
                dv = dv + jnp.dot(p_t.astype(BF16), dob, preferred_element_type=F32)
                dp_t = lax.dot_general(vjs[hh], dob, NT_DIMS, preferred_element_type=F32)
                ds_t = (p_t * (dp_t - delta_row_ref[hh, i])).astype(BF16)
                dk = dk + jnp.dot(ds_t, qi, preferred_element_type=F32)
                dqt_ref[hh, i] += jnp.dot(kts[hh], ds_t, preferred_element_type=F32)
                out.append((dk, dv))
            return tuple(out)

        zero = jnp.zeros((TQ, LANES), F32)
        carry = step(j, tuple((zero, zero) for _ in range(HEADS_PER_STEP)), True)
        carry = lax.fori_loop(j + 1, nq, functools.partial(step, masked=False), carry)
        for hh, (dk, dv) in enumerate(carry):
            dk_ref[:, _head_lanes(hh)] = dk
            dv_ref[:, _head_lanes(hh)] = dv.astype(dv_ref.dtype)
        _hosted(ex, ex_refs, grid, False)

    tile, whole, _, row_whole = _flash_specs(s)
    dqt_spec = pl.BlockSpec((HEADS_PER_STEP, nq, LANES, TQ), lambda h, j: (h, 0, 0, 0))
    wide = lambda dt: jax.ShapeDtypeStruct((s, MLA_HEADS * LANES), dt)
    return _host_call(
        body, ex, name, grid, [whole, tile, tile, whole, row_whole, row_whole], [dqt_spec, tile, tile],
        [jax.ShapeDtypeStruct((MLA_HEADS, nq, LANES, TQ), F32), wide(F32), wide(BF16)],
        (q, k, v, do, lse_row, delta_row))


def _side_by_side(g):
    return g.transpose(1, 0, 2).reshape(g.shape[1], N_DEV * g.shape[2])


def _col_shards(g):
    return g.reshape(g.shape[0], N_DEV, g.shape[1] // N_DEV).transpose(1, 0, 2)


def _pad_last(v, to):
    return jnp.pad(v, [(0, 0)] * (v.ndim - 1) + [(0, to - v.shape[-1])])


def _tiny_rows(v, rows):
    flat = v.reshape(v.shape[:-2] + (-1,))
    return _pad_last(flat, rows * LANES).reshape(v.shape[:-2] + (rows, LANES))


def _pack_small(vals):
    parts = []
    for (n, size), pad in zip(SMALL, SMALL_PAD):
        parts.append(jnp.pad(vals[n].reshape(-1), (0, pad - size)))
    flat = jnp.concatenate(parts)
    return jnp.pad(flat, (0, SMALL_ROWS * LANES - flat.shape[0])).reshape(SMALL_ROWS, LANES)


def _unpack_small(packed):
    flat = packed.reshape(-1)
    out, off = {}, 0
    for (n, size), pad in zip(SMALL, SMALL_PAD):
        out[n] = flat[off:off + size].reshape(1, size)
        off += pad
    return out


def _pad_heads(w, per_head, axis):
    shape = list(w.shape)
    shape[axis:axis + 1] = [MLA_HEADS, per_head]
    w = w.reshape(shape)
    pad = [(0, 0)] * len(shape)
    pad[axis + 1] = (0, LANES - per_head)
    w = jnp.pad(w, pad)
    shape[axis:axis + 2] = [MLA_HEADS * LANES]
    return w.reshape(shape)


def _unpad_heads(w, per_head, axis):
    shape = list(w.shape)
    shape[axis:axis + 1] = [MLA_HEADS, LANES]
    w = w.reshape(shape)
    w = lax.slice_in_dim(w, 0, per_head, axis=axis + 1)
    shape[axis:axis + 2] = [MLA_HEADS * per_head]
    return w.reshape(shape)


def _row(v, pad_to=None):
    v = v.reshape(1, -1)
    if pad_to is not None:
        v = jnp.pad(v, ((0, 0), (0, pad_to - v.shape[1])))
    return v


def kernel(x, mem, positions, mix_norm_g, w_in, b_conv_in, w_conv_dw, b_conv_dw, conv_ln_g, conv_ln_b, q_lat_norm_g, w_uq, kv_lat_norm_g, w_ukv, q_norm_g, k_norm_g, w_out, mem_norm_x_g, mem_norm_m_g, w_mem_q, w_mem_kv, mem_q_norm_g, mem_k_norm_g, w_mem_o, ffn_norm_g, w_up, w_ffn_dw, b_ffn_dw, w_down, loss_target, m_mix_norm_g, m_w_in, m_b_conv_in, m_w_conv_dw, m_b_conv_dw, m_conv_ln_g, m_conv_ln_b, m_q_lat_norm_g, m_w_uq, m_kv_lat_norm_g, m_w_ukv, m_q_norm_g, m_k_norm_g, m_w_out, m_mem_norm_x_g, m_mem_norm_m_g, m_w_mem_q, m_w_mem_kv, m_mem_q_norm_g, m_mem_k_norm_g, m_w_mem_o, m_ffn_norm_g, m_w_up, m_w_ffn_dw, m_b_ffn_dw, m_w_down, v_mix_norm_g, v_w_in, v_b_conv_in, v_w_conv_dw, v_b_conv_dw, v_conv_ln_g, v_conv_ln_b, v_q_lat_norm_g, v_w_uq, v_kv_lat_norm_g, v_w_ukv, v_q_norm_g, v_k_norm_g, v_w_out, v_mem_norm_x_g, v_mem_norm_m_g, v_w_mem_q, v_w_mem_kv, v_mem_q_norm_g, v_mem_k_norm_g, v_w_mem_o, v_ffn_norm_g, v_w_up, v_w_ffn_dw, v_b_ffn_dw, v_w_down):
    a = dict(locals())
    seq = x.shape[1]
    xs = x.reshape(seq, D_MODEL)
    mems = mem.reshape(-1, D_MODEL)
    target = loss_target.reshape(seq, D_MODEL)

    tiny = [n for n, _, _ in TINY]
    shard = lambda n: a[n][0] if n in tiny else a[n][0].astype(BF16)
    pos = jnp.stack([2 * lax.axis_index("x") + lax.axis_index("y"), lax.axis_index("c")]).astype(jnp.int32)
    ag_first = ["w_in", "w_uq", "w_ukv", "w_conv_dw"]
    ag_later = [n for n in [b for b, _ in BIG] + tiny if n not in ag_first]
    wg = dict(zip(ag_first, _all_gather([shard(n) for n in ag_first], "ag_weights_first")))
    wi = _side_by_side(wg["w_in"])
    s3 = 2 * CONV_CH + MLA_Q_RANK + MLA_KV_RANK
    w_in_p = jnp.concatenate([
        wi[:, :s3], jnp.zeros((D_MODEL, MLA_NOPE), BF16), wi[:, s3:],
        jnp.zeros((D_MODEL, LANES - MLA_QK), BF16)], axis=1)
    w_uq_p = _side_by_side(_pad_last(wg["w_uq"], LANES))
    w_uk_p = _side_by_side(_pad_last(wg["w_ukv"][:, :, :MLA_NOPE], LANES))
    w_uv_p = _side_by_side(_pad_last(wg["w_ukv"][:, :, MLA_NOPE:], LANES))
    w_cdw = _side_by_side(wg["w_conv_dw"])

    g_mix, g_qlat, g_kvlat = _row(mix_norm_g), _row(q_lat_norm_g), _row(kv_lat_norm_g)
    b_in = _row(b_conv_in)
    b_in_a, b_in_g = b_in[:, :CONV_CH], b_in[:, CONV_CH:]
    b_cdw, ln_g, ln_b = _row(b_conv_dw), _row(conv_ln_g), _row(conv_ln_b)
    g_q, g_k = _row(q_norm_g, LANES), _row(k_norm_g, LANES)
    g_memx, g_memm = _row(mem_norm_x_g), _row(mem_norm_m_g)
    g_mq, g_mk, g_ffn = _row(mem_q_norm_g), _row(mem_k_norm_g), _row(ffn_norm_g)
    b_f = _pad_last(b_ffn_dw.reshape(N_DEV, FF_SHARD), FF_PAD)
    b_f_g, b_f_v = b_f[:4].reshape(1, D_FF_PAD), b_f[4:].reshape(1, D_FF_PAD)

    freq = ROPE_THETA ** (-jnp.arange(0, MLA_ROPE, 2, dtype=F32) / MLA_ROPE)
    inv_freq = jnp.concatenate([jnp.zeros((MLA_NOPE,), F32), freq, freq,
                                jnp.zeros((LANES - MLA_QK,), F32)]).reshape(1, LANES)
    cos, sin_a, sin_b = _rowwise(_f_rope_tab, [positions.reshape(seq, 1)], [inv_freq],
                                 [(LANES, F32)] * 3, [], name="rope_tables")

    (h1,) = _rowwise(_f_rms, [xs], [g_mix], [(D_MODEL, BF16)], [], name="rms_mix")
    z = _mm(h1, w_in_p, name="mm_in")
    z_rows = [(z, CONV_CH, 0, 0), (z, CONV_CH, 1, 0), (z, MLA_Q_RANK, 4, 0), (z, MLA_KV_RANK, 10, 0)]
    z_kr = (z, LANES, 11, 0)
    u0, cqn, ckvn = _rowwise(
        _mix_pre, z_rows, [b_in_a, b_in_g, g_qlat, g_kvlat],
        [(CONV_CH, F32), (MLA_Q_RANK, BF16), (MLA_KV_RANK, BF16)], [], name="mix_pre")
    c1 = _conv_fwd(u0, w_cdw, "conv31_fwd")
    (u,) = _rowwise(lambda c, b, g, bb: (_ln_silu(c, b, g, bb),), [c1], [b_cdw, ln_g, ln_b],
                    [(CONV_CH, BF16)], [], name="ln_silu")
    q0 = _mm(cqn, w_uq_p, name="mm_uq")
    kn0 = _mm(ckvn, w_uk_p, name="mm_uk")
    v0 = _mm(ckvn, w_uv_p, out_dtype=BF16, name="mm_uv")
    qk_rows = [q0, kn0, z_kr, cos, sin_a, sin_b]
    qh, kh = _rowwise(_f_qk_prep, qk_rows, [g_q, g_k],
                      [(MLA_HEADS * LANES, BF16)] * 2, [], name="qk_prep")
    (attn, lse_row), later = _flash_fwd(
        qh, kh, v0, "flash_fwd", _plan_all_gather([shard(n) for n in ag_later]))
    wg.update(zip(ag_later, later))
    w_out = wg["w_out"].reshape(D_MODEL, D_MODEL)
    w_out_u = w_out[:CONV_CH]
    w_out_a = _pad_heads(w_out[CONV_CH:], MLA_V, 0)
    w_mq, w_mo = wg["w_mem_q"].reshape(D_MODEL, D_MODEL), wg["w_mem_o"].reshape(D_MODEL, D_MODEL)
    w_mkv = _side_by_side(wg["w_mem_kv"])
    w_up_p = _pad_last(wg["w_up"], FF_PAD)
    w_dn = jnp.pad(wg["w_down"].reshape(4, FF_SHARD, D_MODEL),
                   ((0, 0), (0, FF_PAD - FF_SHARD), (0, 0))).reshape(D_FF_PAD, D_MODEL)
    w_fdw = _pad_last(wg["w_ffn_dw"], FF_PAD)
    w_fdw_g = w_fdw[:4].transpose(1, 0, 2).reshape(FFN_CONV_WIDTH, D_FF_PAD)
    w_fdw_v = w_fdw[4:].transpose(1, 0, 2).reshape(FFN_CONV_WIDTH, D_FF_PAD)
    x1 = _mm(u, w_out_u, add=xs, name="mm_out_u")
    x1 = _mm(attn, w_out_a, add=x1, name="mm_out_a")

    (hq,) = _rowwise(_f_rms, [x1], [g_memx], [(D_MODEL, BF16)], [], name="rms_memx")
    (hm,) = _rowwise(_f_rms, [mems], [g_memm], [(D_MODEL, BF16)], [], name="rms_memm", tm=mems.shape[0])
    qm0 = _mm(hq, w_mq, name="mm_memq")
    kvm0 = _mm(hm, w_mkv, name="mm_memkv", tm=mems.shape[0])
    (km,) = _rowwise(_f_mem_k, [(kvm0, D_MODEL, 0, 0)], [g_mk], [(D_MODEL, BF16)], [],
                     name="mem_k", tm=mems.shape[0])
    vm = kvm0[:, D_MODEL:]
    (om,) = _rowwise(_f_mem_attn, [qm0], [km, vm, g_mq], [(D_MODEL, BF16)], [], name="mem_attn")
    x2 = _mm(om, w_mo, add=x1, name="mm_memo")

    (h3,) = _rowwise(_f_rms, [x2], [g_ffn], [(D_MODEL, BF16)], [], name="rms_ffn")
    up_g, up_v, act, dy, sq = _ffn_fwd(h3, w_up_p, w_fdw_g, w_fdw_v, b_f_g, b_f_v, w_dn, x2, target, "ffn_fwd")
    loss = lax.psum(0.5 * jnp.sum(sq) / D_MODEL, ("x", "y", "c"))

    gw, gs, gt = {}, {}, {}
    gw_dn = _mm_tn(act, dy, name="tn_down").reshape(4, FF_PAD, D_MODEL)
    gw["w_down"] = gw_dn[:, :FF_SHARD].reshape(N_DEV, FF_SHARD // 2, D_MODEL)
    dup_g, dup_v, dwf_g, dwf_v, db_g, db_v = _ffn_mid_bwd(
        up_g, up_v, dy, w_dn, w_fdw_g, w_fdw_v, b_f_g, b_f_v, "ffn_mid_bwd")
    db_f = jnp.concatenate([db_g.reshape(4, FF_PAD), db_v.reshape(4, FF_PAD)], axis=0)
    gs["b_ffn_dw"] = db_f[:, :FF_SHARD].reshape(1, 2 * D_FF)
    dwf = jnp.concatenate([dwf_g.reshape(FFN_CONV_WIDTH, 4, FF_PAD), dwf_v.reshape(FFN_CONV_WIDTH, 4, FF_PAD)], axis=1)
    gt["w_ffn_dw"] = dwf[:, :, :FF_SHARD].transpose(1, 0, 2)
    gw_up = jnp.concatenate([_mm_tn(h3, dup_g, shard_cols=FF_PAD, name="tn_up_g"),
                             _mm_tn(h3, dup_v, shard_cols=FF_PAD, name="tn_up_v")], axis=0)
    gw["w_up"] = gw_up[:, :, :FF_SHARD]
    w_up_flat = _side_by_side(w_up_p)
    dh3 = _mm(dup_g, w_up_flat, trans_b=True, b_kblock=0, name="mm_up_g_t")
    dx2, gs["ffn_norm_g"] = _mm(dup_v, w_up_flat, trans_b=True, b_kblock=1, add=dh3, tm=TM,
                                rms_bwd=(x2, dy, g_ffn), name="mm_up_v_t")

    gw["w_mem_o"] = _mm_tn(om, dx2, name="tn_memo").reshape(N_DEV, -1, D_MODEL)
    dom = _mm(dx2, w_mo, trans_b=True, out_dtype=BF16, name="mm_memo_t")
    n_mem = mems.shape[0]
    dqm0, dkm, dvm, gs["mem_q_norm_g"] = _rowwise(
        _b_mem_attn, [dom, qm0], [km, vm, g_mq], [(D_MODEL, BF16)],
        [(n_mem, D_MODEL), (n_mem, D_MODEL), (1, MEM_HEAD_DIM)], name="mem_attn_bwd")
    gw["w_mem_q"] = _mm_tn(hq, dqm0, name="tn_memq").reshape(N_DEV, -1, D_MODEL)
    dx1, gs["mem_norm_x_g"] = _mm(dqm0, w_mq, trans_b=True, rms_bwd=(x1, dx2, g_memx), name="mm_memq_t")
    dkk, gs["mem_k_norm_g"] = _rowwise(_b_mem_k, [(kvm0, D_MODEL, 0, 0), dkm], [g_mk],
                                       [(D_MODEL, F32)], [(1, MEM_HEAD_DIM)], name="mem_k_bwd", tm=n_mem)
    dkvm0 = jnp.concatenate([dkk, dvm], axis=1)
    gw["w_mem_kv"] = _col_shards(_mm_tn(hm, dkvm0, name="tn_memkv", ts=n_mem))
    dhm = _mm(dkvm0, w_mkv, trans_b=True, name="mm_memkv_t", tm=n_mem)
    _, gs["mem_norm_m_g"] = _rowwise(_b_rms_nores, [mems, dhm], [g_memm], [(D_MODEL, F32)],
                                     [(1, D_MODEL)], name="rms_memm_bwd", tm=n_mem)

    gw_out_u = _mm_tn(u, dx1, name="tn_out_u")
    gw_out_a = _mm_tn(attn, dx1, name="tn_out_a")
    gw["w_out"] = jnp.concatenate([gw_out_u, _unpad_heads(gw_out_a, MLA_V, 0)], axis=0).reshape(N_DEV, -1, D_MODEL)
    du = _mm(dx1, w_out_u, trans_b=True, name="mm_out_u_t")
    dattn = _mm(dx1, w_out_a, trans_b=True, name="mm_out_a_t")
    dc1, gs["b_conv_dw"], gs["conv_ln_g"], gs["conv_ln_b"] = _rowwise(
        _b_ln_silu, [c1, du], [b_cdw, ln_g, ln_b], [(CONV_CH, F32)], [(1, CONV_CH)] * 3, name="ln_silu_bwd")
    rs_first = ["w_up", "w_down", "w_mem_o", "w_mem_q", "w_mem_kv", "w_out"]
    grads = [gw[n] for n in rs_first]
    (du0, g_cdw), gots = _conv_bwd(dc1, u0, w_cdw, "conv31_bwd", _plan_swap_sibling(grads))
    gt["w_conv_dw"] = _col_shards(g_cdw)
    sums_first = _rs_add(grads, gots, pos, "rs_add_first")
    delta_row = _attn_delta(dattn, attn, "attn_delta")
    (dqt, dkh, dv0), recvs_first = _flash_bwd(
        qh, kh, v0, dattn, lse_row, delta_row, "flash_bwd", _plan_swap_chips(sums_first))
    dqt_row = (dqt, pl.BlockSpec((MLA_HEADS, 1, LANES, TQ), lambda i: (0, i, 0, 0)))
    dq0, dkn0, dkr, dgq, dgk = _rowwise(
        _b_qk_prep, qk_rows + [dqt_row, dkh], [g_q, g_k],
        [(MLA_HEADS * LANES, BF16)] * 2 + [(LANES, F32)], [(1, LANES)] * 2, name="qk_prep_bwd")
    gs["q_norm_g"], gs["k_norm_g"] = dgq[:, :MLA_QK], dgk[:, :MLA_QK]
    gw["w_uq"] = _col_shards(_mm_tn(cqn, dq0, name="tn_uq"))[:, :, :MLA_QK]
    g_uk = _col_shards(_mm_tn(ckvn, dkn0, name="tn_uk"))[:, :, :MLA_NOPE]
    g_uv = _col_shards(_mm_tn(ckvn, dv0, name="tn_uv"))[:, :, :MLA_V]
    gw["w_ukv"] = jnp.concatenate([g_uk, g_uv], axis=2)
    dcqn = _mm(dq0, w_uq_p, trans_b=True, name="mm_uq_t")
    dckvn = _mm(dkn0, w_uk_p, trans_b=True, name="mm_uk_t")
    dckvn = _mm(dv0, w_uv_p, trans_b=True, add=dckvn, name="mm_uv_t")
    dz, dba, dbg, gs["q_lat_norm_g"], gs["kv_lat_norm_g"] = _rowwise(
        _b_mix_pre, z_rows + [du0, dcqn, dckvn, dkr], [b_in_a, b_in_g, g_qlat, g_kvlat],
        [(IN_PAD, BF16)], [(1, CONV_CH)] * 2 + [(1, MLA_Q_RANK), (1, MLA_KV_RANK)], name="mix_pre_bwd")
    gs["b_conv_in"] = jnp.concatenate([dba, dbg], axis=1)
    gw_in = _mm_tn(h1, dz, name="tn_in")
    gw["w_in"] = _col_shards(jnp.concatenate([gw_in[:, :s3], gw_in[:, s3 + MLA_NOPE:s3 + MLA_QK]], axis=1))
    rs_last = [n for n, _ in BIG if n not in rs_first]
    grads = [gw[n] for n in rs_last]
    gots = _run_exchange(_plan_swap_sibling(grads), "rs_sibling_last")
    sums_last = _rs_add(grads, gots, pos, "rs_add_last")
    (dx, gs["mix_norm_g"]), recvs_last = _mm(dz, w_in_p, trans_b=True, rms_bwd=(xs, dx1, g_mix),
                                             name="mm_in_t", ex=_plan_swap_chips(sums_last))
    big = rs_first + rs_last
    flat = _adamw_big(list(sums_first) + list(sums_last), list(recvs_first) + list(recvs_last),
                      [a[n] for n in big], [a["m_" + n] for n in big], [a["v_" + n] for n in big],
                      pos, "adamw_big")
    res = [{n: flat[4 * i + k] for i, n in enumerate(big)} for k in range(4)]

    part = jnp.concatenate(
        [_pack_small(gs)] + [_tiny_rows(gt[n], rows).reshape(N_DEV * rows, LANES) for n, _, rows in TINY], axis=0)
    (parts,) = _all_gather([part], "ag_small_grads")
    small_in = [_pack_small({n: a[p + n] for n, _ in SMALL}) for p in ("", "m_", "v_")]
    tiny_in = [[_tiny_rows(a[p + n][0], rows) for p in ("", "m_", "v_")] for n, _, rows in TINY]
    flat = _adamw_small(parts, small_in, tiny_in, "adamw_small")
    for k in range(4):
        res[k].update(_unpack_small(flat[k]))
        for i, (n, shape, _) in enumerate(TINY):
            res[k][n] = flat[4 * (i + 1) + k].reshape(-1)[:math.prod(shape)].reshape((1,) + shape)

    return (loss, dx.reshape(1, seq, D_MODEL), *[res[k][n] for k in range(4) for n in WEIGHTS])
```

```python
import functools
import math

import jax
import jax.numpy as jnp
from jax import lax
from jax.experimental import pallas as pl
from jax.experimental.pallas import tpu as pltpu

F32 = jnp.float32
BF16 = jnp.bfloat16
EPS = 1e-6
LANES = 128
N_DEV = 8
D_MODEL = 1024
CONV_CH = 512
CONV_WIDTH = 31
MLA_HEADS = 8
MLA_NOPE = 64
MLA_ROPE = 32
MLA_V = 64
MLA_QK = MLA_NOPE + MLA_ROPE
MLA_Q_RANK = 256
MLA_KV_RANK = 128
ROPE_THETA = 10000.0
IN_COLS = 2 * CONV_CH + MLA_Q_RANK + MLA_KV_RANK + MLA_ROPE
IN_PAD = 2 * CONV_CH + MLA_Q_RANK + MLA_KV_RANK + LANES
MEM_HEADS = 4
MEM_HEAD_DIM = 256
D_FF = 2816
FFN_CONV_WIDTH = 3
CHUNK = 64
ATT_SCALE = 1.0 / math.sqrt(MLA_QK)
LN2 = math.log(2.0)
Q_SCALE = ATT_SCALE / LN2
MEM_SCALE = 1.0 / math.sqrt(MEM_HEAD_DIM)
ADAM_LR, ADAM_B1, ADAM_B2, ADAM_EPS, ADAM_WD, ADAM_STEP = 0.001, 0.9, 0.999, 1e-08, 0.01, 10

TM = 512
MM_ROWS = 1024
TN_ROWS = 2048
TQ = 512
HEADS_PER_STEP = 2
FWD_HEADS = 4
CONV_ROWS = 256
NEG = -1e30
VMEM_LIMIT = 56 * 1024 * 1024

MESH = pl.DeviceIdType.MESH
ANY = pl.BlockSpec(memory_space=pl.ANY)
NT_DIMS = (((1,), (1,)), ((), ()))

BIG = [
    ("w_in", (1024, 180)), ("w_uq", (256, 96)), ("w_ukv", (128, 128)), ("w_out", (128, 1024)),
    ("w_mem_q", (128, 1024)), ("w_mem_kv", (1024, 256)), ("w_mem_o", (128, 1024)),
    ("w_up", (1024, 704)), ("w_down", (352, 1024)),
]
TINY = [("w_conv_dw", (31, 64), 16), ("w_ffn_dw", (3, 704), 24)]
ROW_STEPS = 4
FF_SHARD = D_FF // 4
FF_PAD = 768
D_FF_PAD = 4 * FF_PAD
SMALL = [
    ("mix_norm_g", 1024), ("b_conv_in", 1024), ("b_conv_dw", 512), ("conv_ln_g", 512),
    ("conv_ln_b", 512), ("q_lat_norm_g", 256), ("kv_lat_norm_g", 128), ("q_norm_g", 96),
    ("k_norm_g", 96), ("mem_norm_x_g", 1024), ("mem_norm_m_g", 1024), ("mem_q_norm_g", 256),
    ("mem_k_norm_g", 256), ("ffn_norm_g", 1024), ("b_ffn_dw", 5632),
]
WEIGHTS = [
    "mix_norm_g", "w_in", "b_conv_in", "w_conv_dw", "b_conv_dw", "conv_ln_g", "conv_ln_b",
    "q_lat_norm_g", "w_uq", "kv_lat_norm_g", "w_ukv", "q_norm_g", "k_norm_g", "w_out",
    "mem_norm_x_g", "mem_norm_m_g", "w_mem_q", "w_mem_kv", "mem_q_norm_g", "mem_k_norm_g",
    "w_mem_o", "ffn_norm_g", "w_up", "w_ffn_dw", "b_ffn_dw", "w_down",
]


SMALL_PAD = [(-(-n // LANES)) * LANES for _, n in SMALL]
SMALL_ROWS = -(-sum(SMALL_PAD) // (8 * LANES)) * 8
TINY_BASE = [SMALL_ROWS + N_DEV * sum(r for _, _, r in TINY[:i]) for i in range(len(TINY))]
PART_ROWS = SMALL_ROWS + N_DEV * sum(r for _, _, r in TINY)


def _call(body, **kw):
    return pl.pallas_call(body, **kw)


def _params(*sem):
    return pltpu.CompilerParams(dimension_semantics=sem, vmem_limit_bytes=VMEM_LIMIT)


class _Exchange:
    def __init__(self, inputs, out_shape, scratch, start, finish):
        self.inputs, self.out_shape, self.scratch = list(inputs), list(out_shape), list(scratch)
        self.start, self.finish = start, finish


def _run_exchange(ex, name):
    n_in, n_out = len(ex.inputs), len(ex.out_shape)

    def body(*refs):
        parts = refs[:n_in], refs[n_in:n_in + n_out], refs[n_in + n_out:]
        ex.start(*parts)
        ex.finish(*parts)

    return _call(body, name=name, out_shape=ex.out_shape, in_specs=[ANY] * n_in,
                 out_specs=[ANY] * n_out, scratch_shapes=ex.scratch)(*ex.inputs)


def _plan_all_gather(xs):
    n = len(xs)

    def copies(x_refs, out_refs, sems):
        send_sems, recv_sems, local_sems = sems
        x, y, c = lax.axis_index("x"), lax.axis_index("y"), lax.axis_index("c")
        me, sibling = (x, y, c), (x, y, 1 - c)
        chips = [(1 - x, y), (x, 1 - y), (1 - x, 1 - y)]

        def slot(o, px, py, pc):
            return out_refs[o].at[4 * px + 2 * py + pc]

        def copy(o, k, block, to, src=None):
            return pltpu.make_async_remote_copy(
                src_ref=slot(o, *block) if src is None else src, dst_ref=slot(o, *block),
                send_sem=send_sems.at[o, k], recv_sem=recv_sems.at[o, k],
                device_id=to, device_id_type=MESH)

        mine = [pltpu.make_async_copy(x_refs[o], slot(o, *me), local_sems.at[o]) for o in range(n)]
        first = [copy(o, 0, me, sibling, src=x_refs[o]) for o in range(n)]
        first += [copy(o, 1 + j, me, (*chip, c), src=x_refs[o])
                  for j, chip in enumerate(chips) for o in range(n)]
        return me, sibling, chips, copy, mine, first

    def start(x_refs, out_refs, sems):
        _, _, _, _, mine, first = copies(x_refs, out_refs, sems)
        for cp in mine + first:
            cp.start()

    def finish(x_refs, out_refs, sems):
        me, sibling, chips, copy, mine, first = copies(x_refs, out_refs, sems)
        c = me[2]
        passed = []
        for j, chip in enumerate(chips):
            for o in range(n):
                copy(o, 1 + j, (*chip, c), me).wait_recv()
                passed.append(copy(o, 4 + j, (*chip, c), sibling))
                passed[-1].start()
        for o in range(n):
            copy(o, 0, sibling, me).wait_recv()
        for j, chip in enumerate(chips):
            for o in range(n):
                copy(o, 4 + j, (*chip, 1 - c), me).wait_recv()
        for cp in first + passed:
            cp.wait_send()
        for cp in mine:
            cp.wait()

    return _Exchange(
        xs, [jax.ShapeDtypeStruct((N_DEV,) + v.shape, v.dtype) for v in xs],
        [pltpu.SemaphoreType.DMA((n, 7)), pltpu.SemaphoreType.DMA((n, 7)), pltpu.SemaphoreType.DMA((n,))],
        start, finish)


def _all_gather(xs, name):
    return _run_exchange(_plan_all_gather(xs), name)


def _plan_swap_sibling(grads):
    n = len(grads)

    def copies(g_refs, got_refs, sems):
        send_sems, recv_sems = sems
        x, y, c = lax.axis_index("x"), lax.axis_index("y"), lax.axis_index("c")
        return [
            pltpu.make_async_remote_copy(
                src_ref=g_refs[o].at[2 * chip + 1 - c], dst_ref=got_refs[o].at[chip],
                send_sem=send_sems.at[o, chip], recv_sem=recv_sems.at[o, chip],
                device_id=(x, y, 1 - c), device_id_type=MESH)
            for o in range(n) for chip in range(4)]

    def start(g_refs, got_refs, sems):
        for cp in copies(g_refs, got_refs, sems):
            cp.start()

    def finish(g_refs, got_refs, sems):
        for cp in copies(g_refs, got_refs, sems):
            cp.wait()

    return _Exchange(
        grads, [jax.ShapeDtypeStruct((4,) + g.shape[1:], g.dtype) for g in grads],
        [pltpu.SemaphoreType.DMA((n, 4)), pltpu.SemaphoreType.DMA((n, 4))], start, finish)


def _plan_swap_chips(sums):
    n = len(sums)

    def copies(a_refs, r_refs, sems):
        send_sems, recv_sems = sems
        x, y, c = lax.axis_index("x"), lax.axis_index("y"), lax.axis_index("c")
        peers = [(x, 1 - y), (1 - x, y), (1 - x, 1 - y)]
        return [
            pltpu.make_async_remote_copy(
                src_ref=a_refs[o].at[2 * px + py], dst_ref=r_refs[o].at[k],
                send_sem=send_sems.at[o, k], recv_sem=recv_sems.at[o, k],
                device_id=(px, py, c), device_id_type=MESH)
            for k, (px, py) in enumerate(peers) for o in range(n)]

    def start(a_refs, r_refs, sems):
        for cp in copies(a_refs, r_refs, sems):
            cp.start()

    def finish(a_refs, r_refs, sems):
        for cp in copies(a_refs, r_refs, sems):
            cp.wait()

    return _Exchange(
        sums, [jax.ShapeDtypeStruct((3,) + a.shape[1:], a.dtype) for a in sums],
        [pltpu.SemaphoreType.DMA((n, 3)), pltpu.SemaphoreType.DMA((n, 3))], start, finish)


def _shard_block(shape):
    return (None, shape[-2] // ROW_STEPS, shape[-1])


def _rs_add(grads, gots, pos, name):
    n = len(grads)

    def body(pos_ref, *refs):
        for g_ref, t_ref, o_ref in zip(refs[:n], refs[n:2 * n], refs[2 * n:]):
            o_ref[...] = g_ref[...] + t_ref[...]

    in_specs = [pl.BlockSpec(_shard_block(g.shape), lambda a, t, pos: (2 * a + pos[1], t, 0)) for g in grads]
    in_specs += [pl.BlockSpec(_shard_block(g.shape), lambda a, t, pos: (a, t, 0)) for g in gots]
    return _call(
        body, name=name,
        grid_spec=pltpu.PrefetchScalarGridSpec(
            num_scalar_prefetch=1, grid=(4, ROW_STEPS), in_specs=in_specs,
            out_specs=[pl.BlockSpec(_shard_block(g.shape), lambda a, t, pos: (a, t, 0)) for g in gots]),
        out_shape=[jax.ShapeDtypeStruct(g.shape, g.dtype) for g in gots],
        compiler_params=_params("arbitrary", "arbitrary"),
    )(pos, *grads, *gots)


def _adamw_big(sums, recvs, ws, ms, vs, pos, name):
    n = len(sums)

    def body(pos_ref, *refs):
        ins, outs = refs[:7 * n], refs[7 * n:]
        for o in range(n):
            own, r1, r2, r3, w, m, v = [r[...] for r in ins[7 * o:7 * o + 7]]
            g = ((own + r1) + r2) + r3
            for ref, val in zip(outs[4 * o:4 * o + 4], (g,) + _adamw(w, g, m, v)):
                ref[...] = val

    in_specs, args, out_specs, out_shape = [], [], [], []
    for s_, r_, w_, m_, v_ in zip(sums, recvs, ws, ms, vs):
        blk = _shard_block(w_.shape)
        in_specs.append(pl.BlockSpec(blk, lambda t, pos: (pos[0], t, 0)))
        in_specs += [pl.BlockSpec(blk, lambda t, pos, k=k: (k, t, 0)) for k in range(3)]
        in_specs += [pl.BlockSpec(blk, lambda t, pos: (0, t, 0))] * 3
        args += [s_, r_, r_, r_, w_, m_, v_]
        out_specs += [pl.BlockSpec(blk, lambda t, pos: (0, t, 0))] * 4
        out_shape += [jax.ShapeDtypeStruct(w_.shape, F32)] * 4
    return _call(
        body, name=name,
        grid_spec=pltpu.PrefetchScalarGridSpec(
            num_scalar_prefetch=1, grid=(ROW_STEPS,), in_specs=in_specs, out_specs=out_specs),
        out_shape=out_shape, compiler_params=_params("arbitrary"),
    )(pos, *args)


def _tile(n, prefs):
    for t in prefs:
        if n % t == 0:
            return t
    return n


def _mm(a, b, *, name, trans_b=False, add=None, out_dtype=F32, tm=MM_ROWS, b_kblock=0, ex=None,
        rms_bwd=None):
    m, k = a.shape
    n = b.shape[0] if trans_b else b.shape[1]
    tn = _tile(n, (1536, 1408, 1024, 768, 512, 256, 128))
    tk = _tile(k, (3072, 2048, 1536, 1408, 1024, 768, 512, 256, 128))
    nk = k // tk
    has_add = add is not None
    n_post = 3 if rms_bwd else 0
    assert not rms_bwd or tn == n
    grid = (m // tm, n // tn, nk)

    def body(*refs):
        ins, outs, ex_refs = _split_refs(refs, 2 + has_add + n_post, 1 + bool(rms_bwd), ex, int(nk > 1))
        _hosted(ex, ex_refs, grid, True)
        a_ref, b_ref = ins[0], ins[1]
        add_ref = ins[2] if has_add else None
        o_ref = outs[0]
        av = a_ref[...].astype(BF16)
        bv = b_ref[...].astype(BF16)
        if trans_b:
            part = lax.dot_general(av, bv, NT_DIMS, preferred_element_type=F32)
        else:
            part = jnp.dot(av, bv, preferred_element_type=F32)

        def finish(acc):
            if has_add:
                acc = acc + add_ref[...].astype(F32)
            if rms_bwd:
                x_ref, dres_ref, g_ref = ins[2 + has_add:]
                acc, dg = _b_rms(x_ref[...], acc, dres_ref[...], g_ref[...])
                first_rows = pl.program_id(0) == 0

                @pl.when(first_rows)
                def _():
                    outs[1][...] = dg

                @pl.when(jnp.logical_not(first_rows))
                def _():
                    outs[1][...] += dg
            o_ref[...] = acc.astype(o_ref.dtype)

        if nk == 1:
            finish(part)
        else:
            acc_ref = outs[-1]
            kk = pl.program_id(2)

            @pl.when(kk == 0)
            def _():
                acc_ref[...] = part

            @pl.when(kk > 0)
            def _():
                acc_ref[...] += part

            @pl.when(kk == nk - 1)
            def _():
                finish(acc_ref[...])
        _hosted(ex, ex_refs, grid, False)

    in_specs = [pl.BlockSpec((tm, tk), lambda i, j, kk: (i, kk))]
    if trans_b:
        in_specs.append(pl.BlockSpec((tn, tk), lambda i, j, kk: (j, kk + b_kblock * nk)))
    else:
        in_specs.append(pl.BlockSpec((tk, tn), lambda i, j, kk: (kk, j)))
    args = [a, b]
    tile = pl.BlockSpec((tm, tn), lambda i, j, kk: (i, j))
    out_specs, out_shape = [tile], [jax.ShapeDtypeStruct((m, n), out_dtype)]
    if has_add:
        in_specs.append(tile)
        args.append(add)
    if rms_bwd:
        in_specs += [tile, tile, pl.BlockSpec((1, n), lambda i, j, kk: (0, 0))]
        args += list(rms_bwd)
        out_specs.append(pl.BlockSpec((1, n), lambda i, j, kk: (0, 0)))
        out_shape.append(jax.ShapeDtypeStruct((1, n), F32))
    outs, hosted = _host_call(
        body, ex, name, grid, in_specs, out_specs, out_shape, args,
        scratch=[pltpu.VMEM((tm, tn), F32)] if nk > 1 else [])
    res = tuple(outs) if rms_bwd else outs[0]
    return (res, hosted) if ex else res


def _mm_tn(a, b, *, name, ts=TN_ROWS, shard_cols=None):
    s, m = a.shape
    n = b.shape[1]
    tm = _tile(m, (1408, 1024, 768, 512, 256, 128))
    tn = shard_cols or _tile(n, (1536, 1408, 1024, 768, 512, 256, 128))
    if shard_cols:
        out_spec = pl.BlockSpec((None, tm, tn), lambda i, j, kk: (j, i, 0))
        out_shape = jax.ShapeDtypeStruct((n // tn, m, tn), F32)
    else:
        out_spec = pl.BlockSpec((tm, tn), lambda i, j, kk: (i, j))
        out_shape = jax.ShapeDtypeStruct((m, n), F32)

    def body(a_ref, b_ref, o_ref):
        kk = pl.program_id(2)
        part = jnp.dot(a_ref[...].astype(BF16).T, b_ref[...].astype(BF16),
                       preferred_element_type=F32)

        @pl.when(kk == 0)
        def _():
            o_ref[...] = part

        @pl.when(kk > 0)
        def _():
            o_ref[...] += part

    return _call(
        body, name=name, grid=(m // tm, n // tn, s // ts),
        in_specs=[pl.BlockSpec((ts, tm), lambda i, j, kk: (kk, i)),
                  pl.BlockSpec((ts, tn), lambda i, j, kk: (kk, j))],
        out_specs=out_spec, out_shape=out_shape,
        compiler_params=_params("parallel", "parallel", "arbitrary"),
    )(a, b)


def _rowwise(fn, rows, consts, row_outs, acc_outs, *, name, tm=TM, n_rows=None):
    rows = [r if isinstance(r, tuple) else (r, r.shape[1], 0, 0) for r in rows]
    s = n_rows or rows[0][0].shape[0]
    nr, nc, no, na = len(rows), len(consts), len(row_outs), len(acc_outs)

    def body(*refs):
        r_in, c_in = refs[:nr], refs[nr:nr + nc]
        o_refs, a_refs = refs[nr + nc:nr + nc + no], refs[nr + nc + no:]
        outs = fn(*[r[...] for r in r_in], *[c[...] for c in c_in])
        for r, v in zip(o_refs, outs[:no]):
            r[...] = v.astype(r.dtype)
        if na:
            i = pl.program_id(0)

            @pl.when(i == 0)
            def _():
                for r, v in zip(a_refs, outs[no:]):
                    r[...] = v.astype(F32)

            @pl.when(i > 0)
            def _():
                for r, v in zip(a_refs, outs[no:]):
                    r[...] += v.astype(F32)

    in_specs = [r[1] if isinstance(r[1], pl.BlockSpec) else
                pl.BlockSpec((tm, r[1]), lambda i, cb=r[2], rb=r[3]: (i + rb, cb)) for r in rows]
    in_specs += [pl.BlockSpec(c.shape, lambda i: (0, 0)) for c in consts]
    out_specs = [pl.BlockSpec((tm, w), lambda i: (i, 0)) for w, _ in row_outs]
    out_specs += [pl.BlockSpec(sh, lambda i: (0, 0)) for sh in acc_outs]
    out_shape = [jax.ShapeDtypeStruct((s, w), dt) for w, dt in row_outs]
    out_shape += [jax.ShapeDtypeStruct(sh, F32) for sh in acc_outs]
    return _call(
        body, name=name, grid=(s // tm,), in_specs=in_specs, out_specs=out_specs,
        out_shape=out_shape, compiler_params=_params("arbitrary"),
    )(*[r[0] for r in rows], *consts)


def _rms(x, g, n=None):
    ms = jnp.sum(x * x, axis=-1, keepdims=True) / float(n or x.shape[-1])
    return x * lax.rsqrt(ms + EPS) * g


def _layer_norm(x, g, b):
    mu = jnp.sum(x, axis=-1, keepdims=True) / float(x.shape[-1])
    xc = x - mu
    var = jnp.sum(xc * xc, axis=-1, keepdims=True) / float(x.shape[-1])
    return xc * lax.rsqrt(var + EPS) * g + b


def _silu(x):
    return x * jax.nn.sigmoid(x)


@jax.custom_vjp
def _rope(y, cos, sin_a, sin_b):
    return y * cos + pltpu.roll(y, 112, 1) * sin_a + pltpu.roll(y, 16, 1) * sin_b


def _rope_fwd(y, cos, sin_a, sin_b):
    return _rope(y, cos, sin_a, sin_b), (cos, sin_a, sin_b)


def _rope_bwd(res, ct):
    cos, sin_a, sin_b = res
    dy = ct * cos + pltpu.roll(ct * sin_a, 16, 1) + pltpu.roll(ct * sin_b, 112, 1)
    return dy, jnp.zeros_like(cos), jnp.zeros_like(sin_a), jnp.zeros_like(sin_b)


_rope.defvjp(_rope_fwd, _rope_bwd)


def _qk_head(xh, g, cos, sin_a, sin_b):
    return _rope(_rms(xh, g, MLA_QK), cos, sin_a, sin_b)


def _heads(x, width):
    return [x[:, h * width:(h + 1) * width] for h in range(x.shape[1] // width)]


def _f_rms(x, g):
    return (_rms(x, g),)


def _f_rope_tab(pos, inv_freq):
    ang = pos.astype(F32) * inv_freq
    lane = lax.broadcasted_iota(jnp.int32, ang.shape, 1)
    sn = jnp.sin(ang)
    first = (lane >= MLA_NOPE) & (lane < MLA_NOPE + MLA_ROPE // 2)
    second = (lane >= MLA_NOPE + MLA_ROPE // 2) & (lane < MLA_QK)
    return jnp.cos(ang), jnp.where(first, -sn, 0.0), jnp.where(second, sn, 0.0)


def _mix_pre(za, zg, zcq, zckv, ba, bg, gq, gkv):
    u0 = (za + ba) * jax.nn.sigmoid(zg + bg)
    return u0, _rms(zcq, gq), _rms(zckv, gkv)


def _ln_silu(c1, bdw, lg, lb):
    return _silu(_layer_norm(c1 + bdw, lg, lb))


def _f_qk_prep(q0, kn, kr, cos, sa, sb, gq, gk):
    qs = [_qk_head(xh, gq, cos, sa, sb) * Q_SCALE for xh in _heads(q0, LANES)]
    ks = [_qk_head(xh + kr, gk, cos, sa, sb) for xh in _heads(kn, LANES)]
    return jnp.concatenate(qs, axis=1), jnp.concatenate(ks, axis=1)


def _act(cg, cv, bg, bv):
    return _silu(cg + bg) * (cv + bv)


def _f_mem_k(kk, g):
    return (jnp.concatenate([_rms(xh, g) for xh in _heads(kk, MEM_HEAD_DIM)], axis=1),)


def _mem_probs(qn, kmh):
    s = lax.dot_general(qn.astype(BF16), kmh, NT_DIMS, preferred_element_type=F32) * MEM_SCALE
    e = jnp.exp(s - jnp.max(s, axis=-1, keepdims=True))
    return e / jnp.sum(e, axis=-1, keepdims=True)


def _f_mem_attn(qm0, km, vm, g):
    outs = []
    for h, xh in enumerate(_heads(qm0, MEM_HEAD_DIM)):
        sl = slice(h * MEM_HEAD_DIM, (h + 1) * MEM_HEAD_DIM)
        p = _mem_probs(_rms(xh, g), km[:, sl])
        outs.append(jnp.dot(p.astype(BF16), vm[:, sl].astype(BF16), preferred_element_type=F32))
    return (jnp.concatenate(outs, axis=1),)


def _f_loss(y, t):
    e = y - t
    return e * (1.0 / D_MODEL), jnp.sum(e * e, axis=0, keepdims=True)


def _b_rms(x, dh, dres, g):
    _, vjp = jax.vjp(_rms, x, g)
    dx, dg = vjp(dh)
    return dx + dres, dg


def _b_rms_nores(x, dh, g):
    _, vjp = jax.vjp(_rms, x, g)
    dx, dg = vjp(dh)
    return dx, dg


def _b_mix_pre(za, zg, zcq, zckv, du0, dcqn, dckvn, dkr, ba, bg, gq, gkv):
    _, vjp = jax.vjp(_mix_pre, za, zg, zcq, zckv, ba, bg, gq, gkv)
    dza, dzg, dzcq, dzckv, dba, dbg, dgq, dgkv = vjp((du0, dcqn, dckvn))
    return jnp.concatenate([dza, dzg, dzcq, dzckv, dkr], axis=1), dba, dbg, dgq, dgkv


def _b_ln_silu(c1, du, bdw, lg, lb):
    _, vjp = jax.vjp(_ln_silu, c1, bdw, lg, lb)
    return vjp(du)


def _b_qk_prep(q0, kn, kr, cos, sa, sb, dq, dk, gq, gk):
    head = lambda xh, g: _qk_head(xh, g, cos, sa, sb)
    dk = dk * LN2
    dq0, dkn = [], []
    dkr = jnp.zeros_like(kr)
    dgq = jnp.zeros_like(gq)
    dgk = jnp.zeros_like(gk)
    for h, xh in enumerate(_heads(q0, LANES)):
        _, vjp = jax.vjp(head, xh, gq)
        dx, dg = vjp(dq[h, 0].T * ATT_SCALE)
        dq0.append(dx)
        dgq = dgq + dg
    for xh, ct in zip(_heads(kn, LANES), _heads(dk, LANES)):
        _, vjp = jax.vjp(head, xh + kr, gk)
        dx, dg = vjp(ct)
        dkn.append(dx)
        dkr = dkr + dx
        dgk = dgk + dg
    return jnp.concatenate(dq0, axis=1), jnp.concatenate(dkn, axis=1), dkr, dgq, dgk


def _b_mem_k(kk, dkm, g):
    dkk = []
    dg = jnp.zeros_like(g)
    for xh, ct in zip(_heads(kk, MEM_HEAD_DIM), _heads(dkm, MEM_HEAD_DIM)):
        _, vjp = jax.vjp(_rms, xh, g)
        dx, dgh = vjp(ct)
        dkk.append(dx)
        dg = dg + dgh
    return jnp.concatenate(dkk, axis=1), dg


def _b_mem_attn(dom, qm0, km, vm, g):
    dq0, dkm, dvm = [], [], []
    dg = jnp.zeros_like(g)
    for h, (xh, doh) in enumerate(zip(_heads(qm0, MEM_HEAD_DIM), _heads(dom, MEM_HEAD_DIM))):
        sl = slice(h * MEM_HEAD_DIM, (h + 1) * MEM_HEAD_DIM)
        kmh, vmh = km[:, sl], vm[:, sl].astype(BF16)
        qn, vjp = jax.vjp(_rms, xh, g)
        p = _mem_probs(qn, kmh)
        dob = doh.astype(BF16)
        dp = lax.dot_general(dob, vmh, NT_DIMS, preferred_element_type=F32)
        ds = (p * (dp - jnp.sum(dp * p, axis=-1, keepdims=True)) * MEM_SCALE).astype(BF16)
        dqn = jnp.dot(ds, kmh, preferred_element_type=F32)
        dkm.append(jnp.dot(ds.T, qn.astype(BF16), preferred_element_type=F32))
        dvm.append(jnp.dot(p.astype(BF16).T, dob, preferred_element_type=F32))
        dx, dgh = vjp(dqn)
        dq0.append(dx)
        dg = dg + dgh
    return (jnp.concatenate(dq0, axis=1), jnp.concatenate(dkm, axis=1),
            jnp.concatenate(dvm, axis=1), dg)


def _adamw(w, g, m, v):
    m = ADAM_B1 * m + (1.0 - ADAM_B1) * g
    v = ADAM_B2 * v + (1.0 - ADAM_B2) * jnp.square(g)
    m_hat = m / (1.0 - ADAM_B1 ** ADAM_STEP)
    v_hat = v / (1.0 - ADAM_B2 ** ADAM_STEP)
    delta = -ADAM_LR * (m_hat / (jnp.sqrt(v_hat) + ADAM_EPS) + ADAM_WD * w)
    return delta, m, v


def _adamw_small(parts, small, tiny, name):
    def body(*refs):
        p_ref, ins, outs = refs[0], refs[1:4 + 3 * len(TINY)], refs[4 + 3 * len(TINY):]
        me = 4 * lax.axis_index("x") + 2 * lax.axis_index("y") + lax.axis_index("c")
        groups = [(0, SMALL_ROWS)]
        groups += [(pl.multiple_of(base + me * rows, 8), rows) for base, (_, _, rows) in zip(TINY_BASE, TINY)]
        for k, (start, rows) in enumerate(groups):
            g = p_ref[0, pl.ds(start, rows), :]
            for d in range(1, N_DEV):
                g = g + p_ref[d, pl.ds(start, rows), :]
            w, m, v = [r[...] for r in ins[3 * k:3 * k + 3]]
            for ref, val in zip(outs[4 * k:4 * k + 4], (g,) + _adamw(w, g, m, v)):
                ref[...] = val

    args = list(small) + [t for grp in tiny for t in grp]
    out_shape = []
    for grp in [small] + list(tiny):
        out_shape += [jax.ShapeDtypeStruct(grp[0].shape, F32)] * 4
    return _call(body, name=name, out_shape=out_shape)(parts, *args)


def _conv_fwd(x, w, name):
    s, ch = x.shape
    kw = w.shape[0]
    halo = -(-(kw - 1) // 8) * 8
    r = CONV_ROWS
    n = s // r

    def chunk(window, wv):
        acc = jnp.zeros((r, LANES), F32)
        for k in range(kw):
            shift = kw - 1 - k
            sh = window if shift == 0 else pltpu.roll(window, shift, 0)
            acc = acc + sh[halo:halo + r] * wv[k:k + 1]
        return acc

    def body(x_ref, w_ref, y_ref):
        wv = w_ref[...]
        first = jnp.concatenate([jnp.zeros((halo, LANES), F32), x_ref[0:r]], axis=0)
        y_ref[0:r] = chunk(first, wv)

        def step(i, carry):
            base = pl.multiple_of(i * r, 8)
            y_ref[pl.ds(base, r)] = chunk(x_ref[pl.ds(base - halo, r + halo)], wv)
            return carry

        lax.fori_loop(1, n, step, 0)

    return _call(
        body, name=name, grid=(ch // LANES,),
        in_specs=[pl.BlockSpec((s, LANES), lambda c: (0, c)), pl.BlockSpec((kw, LANES), lambda c: (0, c))],
        out_specs=pl.BlockSpec((s, LANES), lambda c: (0, c)),
        out_shape=jax.ShapeDtypeStruct((s, ch), F32), compiler_params=_params("parallel"),
    )(x, w)


def _conv_bwd(dy, x, w, name, ex=None):
    s, ch = x.shape
    kw = w.shape[0]
    halo = -(-(kw - 1) // 8) * 8
    r = CONV_ROWS
    n = s // r

    def dx_chunk(window, wv):
        acc = jnp.zeros((r, LANES), F32)
        for k in range(kw):
            shift = kw - 1 - k
            sh = window if shift == 0 else pltpu.roll(window, r + halo - shift, 0)
            acc = acc + sh[0:r] * wv[k:k + 1]
        return acc

    def dw_chunk(xwin, dyc, acc_ref):
        for k in range(kw):
            shift = kw - 1 - k
            sh = xwin if shift == 0 else pltpu.roll(xwin, shift, 0)
            prod = sh[halo:halo + r] * dyc
            acc_ref[k] += jnp.sum(prod.reshape(r // 8, 8, LANES), axis=0)

    grid = (ch // LANES,)

    def body(*refs):
        (dy_ref, x_ref, w_ref), (dx_ref, dw_ref, acc_ref), ex_refs = _split_refs(refs, 3, 2, ex, 1)
        _hosted(ex, ex_refs, grid, True)
        wv = w_ref[...]
        acc_ref[...] = jnp.zeros_like(acc_ref)
        xfirst = jnp.concatenate([jnp.zeros((halo, LANES), F32), x_ref[0:r]], axis=0)
        dw_chunk(xfirst, dy_ref[0:r], acc_ref)
        last = jnp.concatenate([dy_ref[s - r:s], jnp.zeros((halo, LANES), F32)], axis=0)
        dx_ref[s - r:s] = dx_chunk(last, wv)

        def step(i, carry):
            base = pl.multiple_of(i * r, 8)
            dw_chunk(x_ref[pl.ds(base - halo, r + halo)], dy_ref[pl.ds(base, r)], acc_ref)
            prev = pl.multiple_of((i - 1) * r, 8)
            dx_ref[pl.ds(prev, r)] = dx_chunk(dy_ref[pl.ds(prev, r + halo)], wv)
            return carry

        lax.fori_loop(1, n, step, 0)
        dw_ref[...] = jnp.sum(acc_ref[...], axis=1)
        _hosted(ex, ex_refs, grid, False)

    spec = pl.BlockSpec((s, LANES), lambda c: (0, c))
    wspec = pl.BlockSpec((kw, LANES), lambda c: (0, c))
    return _host_call(
        body, ex, name, grid, [spec, spec, wspec], [spec, wspec],
        [jax.ShapeDtypeStruct((s, ch), F32), jax.ShapeDtypeStruct((kw, ch), F32)], (dy, x, w),
        scratch=[pltpu.VMEM((kw, 8, LANES), F32)])


HALO = 8


def _conv3(win, w, rows):
    return (pltpu.roll(win, 2, 0)[HALO:HALO + rows] * w[0:1] + pltpu.roll(win, 1, 0)[HALO:HALO + rows] * w[1:2]
            + win[HALO:HALO + rows] * w[2:3])


def _ffn_mid_bwd(up_g, up_v, dy, w_dn, w_g, w_v, b_g, b_v, name):
    s, width = up_g.shape
    tm, tc = TM, FF_PAD
    n_row = s // tm

    def body(dy_ref, ndy_ref, wd_ref, pg_ref, g_ref, ng_ref, pv_ref, v_ref, nv_ref, wg_ref, wv_ref,
             bg_ref, bv_ref, dug_ref, duv_ref, dwg_ref, dwv_ref, dbg_ref, dbv_ref):
        i = pl.program_id(1)
        first = (i > 0).astype(F32)
        last = (i < n_row - 1).astype(F32)
        ext = tm + HALO
        wg, wv, wd = wg_ref[...], wv_ref[...], wd_ref[...]
        ndy = jnp.concatenate([ndy_ref[...] * last, jnp.zeros((HALO, D_MODEL), F32)], axis=0)
        d_ext = jnp.concatenate([
            lax.dot_general(dy_ref[...].astype(BF16), wd, NT_DIMS, preferred_element_type=F32),
            lax.dot_general(ndy.astype(BF16), wd, NT_DIMS, preferred_element_type=F32)[:HALO]], axis=0)
        xg = jnp.concatenate([pg_ref[...] * first, g_ref[...], ng_ref[...]], axis=0)
        xv = jnp.concatenate([pv_ref[...] * first, v_ref[...], nv_ref[...]], axis=0)
        taps = [[pltpu.roll(x, 2, 0), pltpu.roll(x, 1, 0), x] for x in (xg, xv)]
        conv = [sum(t[HALO:HALO + ext] * w[k:k + 1] for k, t in enumerate(tp)) for tp, w in zip(taps, (wg, wv))]
        _, vjp = jax.vjp(lambda cg_, cv_: _act(cg_, cv_, bg_ref[...], bv_ref[...]), *conv)
        dcg, dcv = vjp(d_ext)
        results = []
        for tp, w, dc in ((taps[0], wg, dcg), (taps[1], wv, dcv)):
            dup = (dc[:tm] * w[2:3] + pltpu.roll(dc, ext - 1, 0)[:tm] * w[1:2]
                   + pltpu.roll(dc, ext - 2, 0)[:tm] * w[0:1])
            own = dc[:tm]
            dw = jnp.concatenate(
                [jnp.sum(own * t[HALO:HALO + tm], axis=0, keepdims=True) for t in tp], axis=0)
            results.append((dup, dw, jnp.sum(own, axis=0, keepdims=True)))
        (dug, dwg, dbg), (duv, dwv, dbv) = results
        dug_ref[...] = dug.astype(dug_ref.dtype)
        duv_ref[...] = duv.astype(duv_ref.dtype)

        @pl.when(i == 0)
        def _():
            dwg_ref[...], dwv_ref[...], dbg_ref[...], dbv_ref[...] = dwg, dwv, dbg, dbv

        @pl.when(i > 0)
        def _():
            dwg_ref[...] += dwg
            dwv_ref[...] += dwv
            dbg_ref[...] += dbg
            dbv_ref[...] += dbv

    per = tm // HALO
    tile = pl.BlockSpec((tm, tc), lambda c, i: (i, c))
    prev = pl.BlockSpec((HALO, tc), lambda c, i: (jnp.maximum(i * per - 1, 0), c))
    nxt = pl.BlockSpec((HALO, tc), lambda c, i: (jnp.minimum((i + 1) * per, s // HALO - 1), c))
    wspec = pl.BlockSpec((FFN_CONV_WIDTH, tc), lambda c, i: (0, c))
    bspec = pl.BlockSpec((1, tc), lambda c, i: (0, c))
    wide = jax.ShapeDtypeStruct((s, width), BF16)
    dy_tile = pl.BlockSpec((tm, D_MODEL), lambda c, i: (i, 0))
    dy_next = pl.BlockSpec((HALO, D_MODEL), lambda c, i: (jnp.minimum((i + 1) * per, s // HALO - 1), 0))
    return _call(
        body, name=name, grid=(width // tc, n_row),
        in_specs=[dy_tile, dy_next, pl.BlockSpec((tc, D_MODEL), lambda c, i: (c, 0)),
                  prev, tile, nxt, prev, tile, nxt, wspec, wspec, bspec, bspec],
        out_specs=[tile, tile, wspec, wspec, bspec, bspec],
        out_shape=[wide, wide] + [jax.ShapeDtypeStruct((FFN_CONV_WIDTH, width), F32)] * 2
        + [jax.ShapeDtypeStruct((1, width), F32)] * 2,
        compiler_params=_params("parallel", "arbitrary"),
    )(dy, dy, w_dn, up_g, up_g, up_g, up_v, up_v, up_v, w_g, w_v, b_g, b_v)


def _ffn_fwd(h3, w_up, w_g, w_v, b_g, b_v, w_dn, x2, target, name):
    s = h3.shape[0]
    tm, tc = TM, FF_PAD
    n_col = D_FF_PAD // tc

    def body(h_ref, wug_ref, wuv_ref, wg_ref, wv_ref, bg_ref, bv_ref, wd_ref, x_ref, t_ref,
             ug_ref, uv_ref, act_ref, dy_ref, sq_ref, halo_g, halo_v, y_acc):
        i, c = pl.program_id(0), pl.program_id(1)
        h = h_ref[...]
        up_g = jnp.dot(h, wug_ref[...], preferred_element_type=F32)
        up_v = jnp.dot(h, wuv_ref[...], preferred_element_type=F32)
        ug_ref[...] = up_g
        uv_ref[...] = up_v
        has_prev = i > 0
        prev_g = jnp.where(has_prev, halo_g[c], 0.0)
        prev_v = jnp.where(has_prev, halo_v[c], 0.0)
        halo_g[c] = up_g[tm - HALO:]
        halo_v[c] = up_v[tm - HALO:]
        cg = _conv3(jnp.concatenate([prev_g, up_g], axis=0), wg_ref[...], tm)
        cv = _conv3(jnp.concatenate([prev_v, up_v], axis=0), wv_ref[...], tm)
        act = _act(cg, cv, bg_ref[...], bv_ref[...]).astype(BF16)
        act_ref[...] = act
        part = jnp.dot(act, wd_ref[...], preferred_element_type=F32)

        @pl.when(c == 0)
        def _():
            y_acc[...] = part

        @pl.when(c > 0)
        def _():
            y_acc[...] += part

        @pl.when(c == n_col - 1)
        def _():
            dy, sq = _f_loss(x_ref[...] + y_acc[...], t_ref[...])
            dy_ref[...] = dy

            @pl.when(i == 0)
            def _():
                sq_ref[...] = sq

            @pl.when(i > 0)
            def _():
                sq_ref[...] += sq

    row = lambda w: pl.BlockSpec((tm, w), lambda i, c: (i, 0))
    tile = pl.BlockSpec((tm, tc), lambda i, c: (i, c))
    wspec = pl.BlockSpec((FFN_CONV_WIDTH, tc), lambda i, c: (0, c))
    bspec = pl.BlockSpec((1, tc), lambda i, c: (0, c))
    wide = jax.ShapeDtypeStruct((s, D_FF_PAD), F32)
    return _call(
        body, name=name, grid=(s // tm, n_col),
        in_specs=[row(D_MODEL),
                  pl.BlockSpec((None, D_MODEL, tc), lambda i, c: (c, 0, 0)),
                  pl.BlockSpec((None, D_MODEL, tc), lambda i, c: (n_col + c, 0, 0)),
                  wspec, wspec, bspec, bspec, pl.BlockSpec((tc, D_MODEL), lambda i, c: (c, 0)),
                  row(D_MODEL), row(D_MODEL)],
        out_specs=[tile, tile, tile, row(D_MODEL), pl.BlockSpec((1, D_MODEL), lambda i, c: (0, 0))],
        out_shape=[wide, wide, jax.ShapeDtypeStruct((s, D_FF_PAD), BF16),
                   jax.ShapeDtypeStruct((s, D_MODEL), F32), jax.ShapeDtypeStruct((1, D_MODEL), F32)],
        scratch_shapes=[pltpu.VMEM((n_col, HALO, tc), F32), pltpu.VMEM((n_col, HALO, tc), F32),
                        pltpu.VMEM((tm, D_MODEL), F32)],
        compiler_params=_params("arbitrary", "arbitrary"),
    )(h3, w_up, w_up, w_g, w_v, b_g, b_v, w_dn, x2, target)


def _chunk_mask(rows_are_queries):
    a = lax.broadcasted_iota(jnp.int32, (TQ, TQ), 0) // CHUNK
    b = lax.broadcasted_iota(jnp.int32, (TQ, TQ), 1) // CHUNK
    return (b <= a) if rows_are_queries else (a <= b)


def _head_lanes(hh):
    return slice(hh * LANES, (hh + 1) * LANES)


def _to_row(col):
    return jnp.broadcast_to(col, (TQ, LANES)).T[0:1, :]


def _flash_specs(s, heads=HEADS_PER_STEP):
    width = heads * LANES
    tile = pl.BlockSpec((TQ, width), lambda h, i: (i, h))
    whole = pl.BlockSpec((s, width), lambda h, i: (0, h))
    row_tile = pl.BlockSpec((heads, 1, 1, TQ), lambda h, i: (h, i, 0, 0))
    row_whole = pl.BlockSpec((heads, s // TQ, 1, TQ), lambda h, i: (h, 0, 0, 0))
    return tile, whole, row_tile, row_whole


def _split_refs(refs, n_in, n_out, ex, n_scratch=0):
    e_in, e_out = (len(ex.inputs), len(ex.out_shape)) if ex else (0, 0)
    a, b, c = n_in + e_in, n_in + e_in + n_out, n_in + e_in + n_out + e_out
    return refs[:n_in], refs[a:b] + refs[c:c + n_scratch], (refs[n_in:a], refs[b:c], refs[c + n_scratch:])


def _hosted(ex, ex_refs, grid, when_first):
    if ex is None:
        return
    ids = [pl.program_id(d) for d in range(len(grid))]
    cond = functools.reduce(
        lambda p, q_: p & q_, [i == (0 if when_first else g - 1) for i, g in zip(ids, grid)])

    @pl.when(cond)
    def _():
        (ex.start if when_first else ex.finish)(*ex_refs)


def _host_call(body, ex, name, grid, in_specs, out_specs, out_shape, args, scratch=()):
    e_in, e_out = (len(ex.inputs), len(ex.out_shape)) if ex else (0, 0)
    res = _call(
        body, name=name, grid=grid, in_specs=list(in_specs) + [ANY] * e_in,
        out_specs=list(out_specs) + [ANY] * e_out,
        out_shape=list(out_shape) + (ex.out_shape if ex else []),
        scratch_shapes=list(scratch) + (ex.scratch if ex else []),
        compiler_params=_params(*["arbitrary"] * len(grid)),
    )(*args, *(ex.inputs if ex else []))
    return res[:len(out_shape)], res[len(out_shape):]


def _flash_fwd(q, k, v, name, ex=None):
    s = q.shape[0]
    nq = s // TQ
    heads = FWD_HEADS
    grid = (MLA_HEADS // heads, nq)

    def body(*refs):
        (q_ref, k_ref, v_ref), (o_ref, lse_row_ref), ex_refs = _split_refs(refs, 3, 2, ex)
        _hosted(ex, ex_refs, grid, True)
        i = pl.program_id(1)
        qs = [q_ref[:, _head_lanes(hh)] for hh in range(heads)]

        def scores(j, hh):
            kj = k_ref[pl.ds(pl.multiple_of(j * TQ, TQ), TQ), _head_lanes(hh)]
            return lax.dot_general(kj, qs[hh], NT_DIMS, preferred_element_type=F32)

        def update(j, sc, m_prev, l_prev, acc, hh):
            vt = v_ref[pl.ds(pl.multiple_of(j * TQ, TQ), TQ), _head_lanes(hh)].T
            m_new = jnp.maximum(m_prev, jnp.max(sc, axis=0, keepdims=True))
            alpha = jnp.exp2(m_prev - m_new)
            p = jnp.exp2(sc - m_new)
            l_new = alpha * l_prev + jnp.sum(p, axis=0, keepdims=True)
            acc = acc * alpha + jnp.dot(vt, p.astype(BF16), preferred_element_type=F32)
            return m_new, l_new, acc

        def step(j, carry):
            out = []
            for hh in range(heads):
                sc, m_prev, l_prev, acc = carry[hh]
                out.append((scores(j + 1, hh),) + update(j, sc, m_prev, l_prev, acc, hh))
            return tuple(out)

        init = tuple((scores(0, hh), jnp.full((1, TQ), NEG, F32), jnp.zeros((1, TQ), F32),
                      jnp.zeros((LANES, TQ), F32)) for hh in range(heads))
        carry = lax.fori_loop(0, i, step, init)
        for hh, (sc, m_prev, l_prev, acc) in enumerate(carry):
            sc = jnp.where(_chunk_mask(False), sc, NEG)
            m_fin, l_fin, acc = update(i, sc, m_prev, l_prev, acc, hh)
            o_ref[:, _head_lanes(hh)] = (acc / l_fin).T
            lse_row_ref[hh, 0] = m_fin + jnp.log2(l_fin)
        _hosted(ex, ex_refs, grid, False)

    tile, whole, row_tile, _ = _flash_specs(s, heads)
    return _host_call(
        body, ex, name, grid, [tile, whole, whole], [tile, row_tile],
        [jax.ShapeDtypeStruct((s, MLA_HEADS * LANES), F32),
         jax.ShapeDtypeStruct((MLA_HEADS, nq, 1, TQ), F32)], (q, k, v))


def _attn_delta(do, o, name):
    s = do.shape[0]

    def body(do_ref, o_ref, d_ref):
        for h in range(MLA_HEADS):
            prod = do_ref[:, _head_lanes(h)] * o_ref[:, _head_lanes(h)]
            d_ref[h, 0] = _to_row(jnp.sum(prod, axis=-1, keepdims=True))

    tile = pl.BlockSpec((TQ, MLA_HEADS * LANES), lambda i: (i, 0))
    return _call(
        body, name=name, grid=(s // TQ,), in_specs=[tile, tile],
        out_specs=pl.BlockSpec((MLA_HEADS, 1, 1, TQ), lambda i: (0, i, 0, 0)),
        out_shape=jax.ShapeDtypeStruct((MLA_HEADS, s // TQ, 1, TQ), F32),
        compiler_params=_params("parallel"),
    )(do, o)


def _flash_bwd(q, k, v, do, lse_row, delta_row, name, ex=None):
    s = q.shape[0]
    nq = s // TQ
    grid = (MLA_HEADS // HEADS_PER_STEP, nq)

    def body(*refs):
        ins, (dqt_ref, dk_ref, dv_ref), ex_refs = _split_refs(refs, 6, 3, ex)
        q_ref, k_ref, v_ref, do_ref, lse_row_ref, delta_row_ref = ins
        _hosted(ex, ex_refs, grid, True)
        j = pl.program_id(1)

        @pl.when(j == 0)
        def _():
            dqt_ref[...] = jnp.zeros_like(dqt_ref)

        kjs = [k_ref[:, _head_lanes(hh)] for hh in range(HEADS_PER_STEP)]
        vjs = [v_ref[:, _head_lanes(hh)] for hh in range(HEADS_PER_STEP)]
        kts = [kj.T for kj in kjs]

        def step(i, carry, masked):
            base = pl.multiple_of(i * TQ, TQ)
            out = []
            for hh in range(HEADS_PER_STEP):
                dk, dv = carry[hh]
                qi = q_ref[pl.ds(base, TQ), _head_lanes(hh)]
                dob = do_ref[pl.ds(base, TQ), _head_lanes(hh)].astype(BF16)
                sc_t = lax.dot_general(kjs[hh], qi, NT_DIMS, preferred_element_type=F32)
                if masked:
                    sc_t = jnp.where(_chunk_mask(False), sc_t, NEG)
                p_t = jnp.exp2(sc_t - lse_row_ref[hh, i])
                dv = dv + jnp.dot(p_t.astype(BF16), dob, preferred_element_type=F32)
                dp_t = lax.dot_general(vjs[hh], dob, NT_DIMS, preferred_element_type=F32)
                ds_t = (p_t * (dp_t - delta_row_ref[hh, i])).astype(BF16)
                dk = dk + jnp.dot(ds_t, qi, preferred_element_type=F32)
                dqt_ref[hh, i] += jnp.dot(kts[hh], ds_t, preferred_element_type=F32)
                out.append((dk, dv))
            return tuple(out)

        zero = jnp.zeros((TQ, LANES), F32)
        carry = step(j, tuple((zero, zero) for _ in range(HEADS_PER_STEP)), True)
        carry = lax.fori_loop(j + 1, nq, functools.partial(step, masked=False), carry)
        for hh, (dk, dv) in enumerate(carry):
            dk_ref[:, _head_lanes(hh)] = dk
            dv_ref[:, _head_lanes(hh)] = dv.astype(dv_ref.dtype)
        _hosted(ex, ex_refs, grid, False)

    tile, whole, _, row_whole = _flash_specs(s)
    dqt_spec = pl.BlockSpec((HEADS_PER_STEP, nq, LANES, TQ), lambda h, j: (h, 0, 0, 0))
    wide = lambda dt: jax.ShapeDtypeStruct((s, MLA_HEADS * LANES), dt)
    return _host_call(
        body, ex, name, grid, [whole, tile, tile, whole, row_whole, row_whole], [dqt_spec, tile, tile],
        [jax.ShapeDtypeStruct((MLA_HEADS, nq, LANES, TQ), F32), wide(F32), wide(BF16)],
        (q, k, v, do, lse_row, delta_row))


def _side_by_side(g):
    return g.transpose(1, 0, 2).reshape(g.shape[1], N_DEV * g.shape[2])


def _col_shards(g):
    return g.reshape(g.shape[0], N_DEV, g.shape[1] // N_DEV).transpose(1, 0, 2)


def _pad_last(v, to):
    return jnp.pad(v, [(0, 0)] * (v.ndim - 1) + [(0, to - v.shape[-1])])


def _tiny_rows(v, rows):
    flat = v.reshape(v.shape[:-2] + (-1,))
    return _pad_last(flat, rows * LANES).reshape(v.shape[:-2] + (rows, LANES))


def _pack_small(vals):
    parts = []
    for (n, size), pad in zip(SMALL, SMALL_PAD):
        parts.append(jnp.pad(vals[n].reshape(-1), (0, pad - size)))
    flat = jnp.concatenate(parts)
    return jnp.pad(flat, (0, SMALL_ROWS * LANES - flat.shape[0])).reshape(SMALL_ROWS, LANES)


def _unpack_small(packed):
    flat = packed.reshape(-1)
    out, off = {}, 0
    for (n, size), pad in zip(SMALL, SMALL_PAD):
        out[n] = flat[off:off + size].reshape(1, size)
        off += pad
    return out


def _pad_heads(w, per_head, axis):
    shape = list(w.shape)
    shape[axis:axis + 1] = [MLA_HEADS, per_head]
    w = w.reshape(shape)
    pad = [(0, 0)] * len(shape)
    pad[axis + 1] = (0, LANES - per_head)
    w = jnp.pad(w, pad)
    shape[axis:axis + 2] = [MLA_HEADS * LANES]
    return w.reshape(shape)


def _unpad_heads(w, per_head, axis):
    shape = list(w.shape)
    shape[axis:axis + 1] = [MLA_HEADS, LANES]
    w = w.reshape(shape)
    w = lax.slice_in_dim(w, 0, per_head, axis=axis + 1)
    shape[axis:axis + 2] = [MLA_HEADS * per_head]
    return w.reshape(shape)


def _row(v, pad_to=None):
    v = v.reshape(1, -1)
    if pad_to is not None:
        v = jnp.pad(v, ((0, 0), (0, pad_to - v.shape[1])))
    return v


def kernel(x, mem, positions, mix_norm_g, w_in, b_conv_in, w_conv_dw, b_conv_dw, conv_ln_g, conv_ln_b, q_lat_norm_g, w_uq, kv_lat_norm_g, w_ukv, q_norm_g, k_norm_g, w_out, mem_norm_x_g, mem_norm_m_g, w_mem_q, w_mem_kv, mem_q_norm_g, mem_k_norm_g, w_mem_o, ffn_norm_g, w_up, w_ffn_dw, b_ffn_dw, w_down, loss_target, m_mix_norm_g, m_w_in, m_b_conv_in, m_w_conv_dw, m_b_conv_dw, m_conv_ln_g, m_conv_ln_b, m_q_lat_norm_g, m_w_uq, m_kv_lat_norm_g, m_w_ukv, m_q_norm_g, m_k_norm_g, m_w_out, m_mem_norm_x_g, m_mem_norm_m_g, m_w_mem_q, m_w_mem_kv, m_mem_q_norm_g, m_mem_k_norm_g, m_w_mem_o, m_ffn_norm_g, m_w_up, m_w_ffn_dw, m_b_ffn_dw, m_w_down, v_mix_norm_g, v_w_in, v_b_conv_in, v_w_conv_dw, v_b_conv_dw, v_conv_ln_g, v_conv_ln_b, v_q_lat_norm_g, v_w_uq, v_kv_lat_norm_g, v_w_ukv, v_q_norm_g, v_k_norm_g, v_w_out, v_mem_norm_x_g, v_mem_norm_m_g, v_w_mem_q, v_w_mem_kv, v_mem_q_norm_g, v_mem_k_norm_g, v_w_mem_o, v_ffn_norm_g, v_w_up, v_w_ffn_dw, v_b_ffn_dw, v_w_down):
    a = dict(locals())
    seq = x.shape[1]
    xs = x.reshape(seq, D_MODEL)
    mems = mem.reshape(-1, D_MODEL)
    target = loss_target.reshape(seq, D_MODEL)

    tiny = [n for n, _, _ in TINY]
    shard = lambda n: a[n][0] if n in tiny else a[n][0].astype(BF16)
    pos = jnp.stack([2 * lax.axis_index("x") + lax.axis_index("y"), lax.axis_index("c")]).astype(jnp.int32)
    ag_first = ["w_in", "w_uq", "w_ukv", "w_conv_dw"]
    ag_later = [n for n in [b for b, _ in BIG] + tiny if n not in ag_first]
    wg = dict(zip(ag_first, _all_gather([shard(n) for n in ag_first], "ag_weights_first")))
    wi = _side_by_side(wg["w_in"])
    s3 = 2 * CONV_CH + MLA_Q_RANK + MLA_KV_RANK
    w_in_p = jnp.concatenate([
        wi[:, :s3], jnp.zeros((D_MODEL, MLA_NOPE), BF16), wi[:, s3:],
        jnp.zeros((D_MODEL, LANES - MLA_QK), BF16)], axis=1)
    w_uq_p = _side_by_side(_pad_last(wg["w_uq"], LANES))
    w_uk_p = _side_by_side(_pad_last(wg["w_ukv"][:, :, :MLA_NOPE], LANES))
    w_uv_p = _side_by_side(_pad_last(wg["w_ukv"][:, :, MLA_NOPE:], LANES))
    w_cdw = _side_by_side(wg["w_conv_dw"])

    g_mix, g_qlat, g_kvlat = _row(mix_norm_g), _row(q_lat_norm_g), _row(kv_lat_norm_g)
    b_in = _row(b_conv_in)
    b_in_a, b_in_g = b_in[:, :CONV_CH], b_in[:, CONV_CH:]
    b_cdw, ln_g, ln_b = _row(b_conv_dw), _row(conv_ln_g), _row(conv_ln_b)
    g_q, g_k = _row(q_norm_g, LANES), _row(k_norm_g, LANES)
    g_memx, g_memm = _row(mem_norm_x_g), _row(mem_norm_m_g)
    g_mq, g_mk, g_ffn = _row(mem_q_norm_g), _row(mem_k_norm_g), _row(ffn_norm_g)
    b_f = _pad_last(b_ffn_dw.reshape(N_DEV, FF_SHARD), FF_PAD)
    b_f_g, b_f_v = b_f[:4].reshape(1, D_FF_PAD), b_f[4:].reshape(1, D_FF_PAD)

    freq = ROPE_THETA ** (-jnp.arange(0, MLA_ROPE, 2, dtype=F32) / MLA_ROPE)
    inv_freq = jnp.concatenate([jnp.zeros((MLA_NOPE,), F32), freq, freq,
                                jnp.zeros((LANES - MLA_QK,), F32)]).reshape(1, LANES)
    cos, sin_a, sin_b = _rowwise(_f_rope_tab, [positions.reshape(seq, 1)], [inv_freq],
                                 [(LANES, F32)] * 3, [], name="rope_tables")

    (h1,) = _rowwise(_f_rms, [xs], [g_mix], [(D_MODEL, BF16)], [], name="rms_mix")
    z = _mm(h1, w_in_p, name="mm_in")
    z_rows = [(z, CONV_CH, 0, 0), (z, CONV_CH, 1, 0), (z, MLA_Q_RANK, 4, 0), (z, MLA_KV_RANK, 10, 0)]
    z_kr = (z, LANES, 11, 0)
    u0, cqn, ckvn = _rowwise(
        _mix_pre, z_rows, [b_in_a, b_in_g, g_qlat, g_kvlat],
        [(CONV_CH, F32), (MLA_Q_RANK, BF16), (MLA_KV_RANK, BF16)], [], name="mix_pre")
    c1 = _conv_fwd(u0, w_cdw, "conv31_fwd")
    (u,) = _rowwise(lambda c, b, g, bb: (_ln_silu(c, b, g, bb),), [c1], [b_cdw, ln_g, ln_b],
                    [(CONV_CH, BF16)], [], name="ln_silu")
    q0 = _mm(cqn, w_uq_p, name="mm_uq")
    kn0 = _mm(ckvn, w_uk_p, name="mm_uk")
    v0 = _mm(ckvn, w_uv_p, out_dtype=BF16, name="mm_uv")
    qk_rows = [q0, kn0, z_kr, cos, sin_a, sin_b]
    qh, kh = _rowwise(_f_qk_prep, qk_rows, [g_q, g_k],
                      [(MLA_HEADS * LANES, BF16)] * 2, [], name="qk_prep")
    (attn, lse_row), later = _flash_fwd(
        qh, kh, v0, "flash_fwd", _plan_all_gather([shard(n) for n in ag_later]))
    wg.update(zip(ag_later, later))
    w_out = wg["w_out"].reshape(D_MODEL, D_MODEL)
    w_out_u = w_out[:CONV_CH]
    w_out_a = _pad_heads(w_out[CONV_CH:], MLA_V, 0)
    w_mq, w_mo = wg["w_mem_q"].reshape(D_MODEL, D_MODEL), wg["w_mem_o"].reshape(D_MODEL, D_MODEL)
    w_mkv = _side_by_side(wg["w_mem_kv"])
    w_up_p = _pad_last(wg["w_up"], FF_PAD)
    w_dn = jnp.pad(wg["w_down"].reshape(4, FF_SHARD, D_MODEL),
                   ((0, 0), (0, FF_PAD - FF_SHARD), (0, 0))).reshape(D_FF_PAD, D_MODEL)
    w_fdw = _pad_last(wg["w_ffn_dw"], FF_PAD)
    w_fdw_g = w_fdw[:4].transpose(1, 0, 2).reshape(FFN_CONV_WIDTH, D_FF_PAD)
    w_fdw_v = w_fdw[4:].transpose(1, 0, 2).reshape(FFN_CONV_WIDTH, D_FF_PAD)
    x1 = _mm(u, w_out_u, add=xs, name="mm_out_u")
    x1 = _mm(attn, w_out_a, add=x1, name="mm_out_a")

    (hq,) = _rowwise(_f_rms, [x1], [g_memx], [(D_MODEL, BF16)], [], name="rms_memx")
    (hm,) = _rowwise(_f_rms, [mems], [g_memm], [(D_MODEL, BF16)], [], name="rms_memm", tm=mems.shape[0])
    qm0 = _mm(hq, w_mq, name="mm_memq")
    kvm0 = _mm(hm, w_mkv, name="mm_memkv", tm=mems.shape[0])
    (km,) = _rowwise(_f_mem_k, [(kvm0, D_MODEL, 0, 0)], [g_mk], [(D_MODEL, BF16)], [],
                     name="mem_k", tm=mems.shape[0])
    vm = kvm0[:, D_MODEL:]
    (om,) = _rowwise(_f_mem_attn, [qm0], [km, vm, g_mq], [(D_MODEL, BF16)], [], name="mem_attn")
    x2 = _mm(om, w_mo, add=x1, name="mm_memo")

    (h3,) = _rowwise(_f_rms, [x2], [g_ffn], [(D_MODEL, BF16)], [], name="rms_ffn")
    up_g, up_v, act, dy, sq = _ffn_fwd(h3, w_up_p, w_fdw_g, w_fdw_v, b_f_g, b_f_v, w_dn, x2, target, "ffn_fwd")
    loss = lax.psum(0.5 * jnp.sum(sq) / D_MODEL, ("x", "y", "c"))

    gw, gs, gt = {}, {}, {}
    gw_dn = _mm_tn(act, dy, name="tn_down").reshape(4, FF_PAD, D_MODEL)
    gw["w_down"] = gw_dn[:, :FF_SHARD].reshape(N_DEV, FF_SHARD // 2, D_MODEL)
    dup_g, dup_v, dwf_g, dwf_v, db_g, db_v = _ffn_mid_bwd(
        up_g, up_v, dy, w_dn, w_fdw_g, w_fdw_v, b_f_g, b_f_v, "ffn_mid_bwd")
    db_f = jnp.concatenate([db_g.reshape(4, FF_PAD), db_v.reshape(4, FF_PAD)], axis=0)
    gs["b_ffn_dw"] = db_f[:, :FF_SHARD].reshape(1, 2 * D_FF)
    dwf = jnp.concatenate([dwf_g.reshape(FFN_CONV_WIDTH, 4, FF_PAD), dwf_v.reshape(FFN_CONV_WIDTH, 4, FF_PAD)], axis=1)
    gt["w_ffn_dw"] = dwf[:, :, :FF_SHARD].transpose(1, 0, 2)
    gw_up = jnp.concatenate([_mm_tn(h3, dup_g, shard_cols=FF_PAD, name="tn_up_g"),
                             _mm_tn(h3, dup_v, shard_cols=FF_PAD, name="tn_up_v")], axis=0)
    gw["w_up"] = gw_up[:, :, :FF_SHARD]
    w_up_flat = _side_by_side(w_up_p)
    dh3 = _mm(dup_g, w_up_flat, trans_b=True, b_kblock=0, name="mm_up_g_t")
    dx2, gs["ffn_norm_g"] = _mm(dup_v, w_up_flat, trans_b=True, b_kblock=1, add=dh3, tm=TM,
                                rms_bwd=(x2, dy, g_ffn), name="mm_up_v_t")

    gw["w_mem_o"] = _mm_tn(om, dx2, name="tn_memo").reshape(N_DEV, -1, D_MODEL)
    dom = _mm(dx2, w_mo, trans_b=True, out_dtype=BF16, name="mm_memo_t")
    n_mem = mems.shape[0]
    dqm0, dkm, dvm, gs["mem_q_norm_g"] = _rowwise(
        _b_mem_attn, [dom, qm0], [km, vm, g_mq], [(D_MODEL, BF16)],
        [(n_mem, D_MODEL), (n_mem, D_MODEL), (1, MEM_HEAD_DIM)], name="mem_attn_bwd")
    gw["w_mem_q"] = _mm_tn(hq, dqm0, name="tn_memq").reshape(N_DEV, -1, D_MODEL)
    dx1, gs["mem_norm_x_g"] = _mm(dqm0, w_mq, trans_b=True, rms_bwd=(x1, dx2, g_memx), name="mm_memq_t")
    dkk, gs["mem_k_norm_g"] = _rowwise(_b_mem_k, [(kvm0, D_MODEL, 0, 0), dkm], [g_mk],
                                       [(D_MODEL, F32)], [(1, MEM_HEAD_DIM)], name="mem_k_bwd", tm=n_mem)
    dkvm0 = jnp.concatenate([dkk, dvm], axis=1)
    gw["w_mem_kv"] = _col_shards(_mm_tn(hm, dkvm0, name="tn_memkv", ts=n_mem))
    dhm = _mm(dkvm0, w_mkv, trans_b=True, name="mm_memkv_t", tm=n_mem)
    _, gs["mem_norm_m_g"] = _rowwise(_b_rms_nores, [mems, dhm], [g_memm], [(D_MODEL, F32)],
                                     [(1, D_MODEL)], name="rms_memm_bwd", tm=n_mem)

    gw_out_u = _mm_tn(u, dx1, name="tn_out_u")
    gw_out_a = _mm_tn(attn, dx1, name="tn_out_a")
    gw["w_out"] = jnp.concatenate([gw_out_u, _unpad_heads(gw_out_a, MLA_V, 0)], axis=0).reshape(N_DEV, -1, D_MODEL)
    du = _mm(dx1, w_out_u, trans_b=True, name="mm_out_u_t")
    dattn = _mm(dx1, w_out_a, trans_b=True, name="mm_out_a_t")
    dc1, gs["b_conv_dw"], gs["conv_ln_g"], gs["conv_ln_b"] = _rowwise(
        _b_ln_silu, [c1, du], [b_cdw, ln_g, ln_b], [(CONV_CH, F32)], [(1, CONV_CH)] * 3, name="ln_silu_bwd")
    rs_first = ["w_up", "w_down", "w_mem_o", "w_mem_q", "w_mem_kv", "w_out"]
    grads = [gw[n] for n in rs_first]
    (du0, g_cdw), gots = _conv_bwd(dc1, u0, w_cdw, "conv31_bwd", _plan_swap_sibling(grads))
    gt["w_conv_dw"] = _col_shards(g_cdw)
    sums_first = _rs_add(grads, gots, pos, "rs_add_first")
    delta_row = _attn_delta(dattn, attn, "attn_delta")
    (dqt, dkh, dv0), recvs_first = _flash_bwd(
        qh, kh, v0, dattn, lse_row, delta_row, "flash_bwd", _plan_swap_chips(sums_first))
    dqt_row = (dqt, pl.BlockSpec((MLA_HEADS, 1, LANES, TQ), lambda i: (0, i, 0, 0)))
    dq0, dkn0, dkr, dgq, dgk = _rowwise(
        _b_qk_prep, qk_rows + [dqt_row, dkh], [g_q, g_k],
        [(MLA_HEADS * LANES, BF16)] * 2 + [(LANES, F32)], [(1, LANES)] * 2, name="qk_prep_bwd")
    gs["q_norm_g"], gs["k_norm_g"] = dgq[:, :MLA_QK], dgk[:, :MLA_QK]
    gw["w_uq"] = _col_shards(_mm_tn(cqn, dq0, name="tn_uq"))[:, :, :MLA_QK]
    g_uk = _col_shards(_mm_tn(ckvn, dkn0, name="tn_uk"))[:, :, :MLA_NOPE]
    g_uv = _col_shards(_mm_tn(ckvn, dv0, name="tn_uv"))[:, :, :MLA_V]
    gw["w_ukv"] = jnp.concatenate([g_uk, g_uv], axis=2)
    dcqn = _mm(dq0, w_uq_p, trans_b=True, name="mm_uq_t")
    dckvn = _mm(dkn0, w_uk_p, trans_b=True, name="mm_uk_t")
    dckvn = _mm(dv0, w_uv_p, trans_b=True, add=dckvn, name="mm_uv_t")
    dz, dba, dbg, gs["q_lat_norm_g"], gs["kv_lat_norm_g"] = _rowwise(
        _b_mix_pre, z_rows + [du0, dcqn, dckvn, dkr], [b_in_a, b_in_g, g_qlat, g_kvlat],
        [(IN_PAD, BF16)], [(1, CONV_CH)] * 2 + [(1, MLA_Q_RANK), (1, MLA_KV_RANK)], name="mix_pre_bwd")
    gs["b_conv_in"] = jnp.concatenate([dba, dbg], axis=1)
    gw_in = _mm_tn(h1, dz, name="tn_in")
    gw["w_in"] = _col_shards(jnp.concatenate([gw_in[:, :s3], gw_in[:, s3 + MLA_NOPE:s3 + MLA_QK]], axis=1))
    rs_last = [n for n, _ in BIG if n not in rs_first]
    grads = [gw[n] for n in rs_last]
    gots = _run_exchange(_plan_swap_sibling(grads), "rs_sibling_last")
    sums_last = _rs_add(grads, gots, pos, "rs_add_last")
    (dx, gs["mix_norm_g"]), recvs_last = _mm(dz, w_in_p, trans_b=True, rms_bwd=(xs, dx1, g_mix),
                                             name="mm_in_t", ex=_plan_swap_chips(sums_last))
    big = rs_first + rs_last
    flat = _adamw_big(list(sums_first) + list(sums_last), list(recvs_first) + list(recvs_last),
                      [a[n] for n in big], [a["m_" + n] for n in big], [a["v_" + n] for n in big],
                      pos, "adamw_big")
    res = [{n: flat[4 * i + k] for i, n in enumerate(big)} for k in range(4)]

    part = jnp.concatenate(
        [_pack_small(gs)] + [_tiny_rows(gt[n], rows).reshape(N_DEV * rows, LANES) for n, _, rows in TINY], axis=0)
    (parts,) = _all_gather([part], "ag_small_grads")
    small_in = [_pack_small({n: a[p + n] for n, _ in SMALL}) for p in ("", "m_", "v_")]
    tiny_in = [[_tiny_rows(a[p + n][0], rows) for p in ("", "m_", "v_")] for n, _, rows in TINY]
    flat = _adamw_small(parts, small_in, tiny_in, "adamw_small")
    for k in range(4):
        res[k].update(_unpack_small(flat[k]))
        for i, (n, shape, _) in enumerate(TINY):
            res[k][n] = flat[4 * (i + 1) + k].reshape(-1)[:math.prod(shape)].reshape((1,) + shape)

    return (loss, dx.reshape(1, seq, D_MODEL), *[res[k][n] for k in range(4) for n in WEIGHTS])
```

```python
import functools
import math

import jax
import jax.numpy as jnp
from jax import lax
from jax.experimental import pallas as pl
from jax.experimental.pallas import tpu as pltpu

F32 = jnp.float32
BF16 = jnp.bfloat16
EPS = 1e-6
LANES = 128
N_DEV = 8
D_MODEL = 1024
CONV_CH = 512
CONV_WIDTH = 31
MLA_HEADS = 8
MLA_NOPE = 64
MLA_ROPE = 32
MLA_V = 64
MLA_QK = MLA_NOPE + MLA_ROPE
MLA_Q_RANK = 256
MLA_KV_RANK = 128
ROPE_THETA = 10000.0
IN_COLS = 2 * CONV_CH + MLA_Q_RANK + MLA_KV_RANK + MLA_ROPE
IN_PAD = 2 * CONV_CH + MLA_Q_RANK + MLA_KV_RANK + LANES
MEM_HEADS = 4
MEM_HEAD_DIM = 256
D_FF = 2816
FFN_CONV_WIDTH = 3
CHUNK = 64
ATT_SCALE = 1.0 / math.sqrt(MLA_QK)
LN2 = math.log(2.0)
Q_SCALE = ATT_SCALE / LN2
MEM_SCALE = 1.0 / math.sqrt(MEM_HEAD_DIM)
ADAM_LR, ADAM_B1, ADAM_B2, ADAM_EPS, ADAM_WD, ADAM_STEP = 0.001, 0.9, 0.999, 1e-08, 0.01, 10

TM = 512
MM_ROWS = 1024
TN_ROWS = 2048
TQ = 512
HEADS_PER_STEP = 2
FWD_HEADS = 4
CONV_ROWS = 256
NEG = -1e30
VMEM_LIMIT = 56 * 1024 * 1024

MESH = pl.DeviceIdType.MESH
ANY = pl.BlockSpec(memory_space=pl.ANY)
NT_DIMS = (((1,), (1,)), ((), ()))

BIG = [
    ("w_in", (1024, 180)), ("w_uq", (256, 96)), ("w_ukv", (128, 128)), ("w_out", (128, 1024)),
    ("w_mem_q", (128, 1024)), ("w_mem_kv", (1024, 256)), ("w_mem_o", (128, 1024)),
    ("w_up", (1024, 704)), ("w_down", (352, 1024)),
]
TINY = [("w_conv_dw", (31, 64), 16), ("w_ffn_dw", (3, 704), 24)]
ROW_STEPS = 4
FF_SHARD = D_FF // 4
FF_PAD = 768
D_FF_PAD = 4 * FF_PAD
SMALL = [
    ("mix_norm_g", 1024), ("b_conv_in", 1024), ("b_conv_dw", 512), ("conv_ln_g", 512),
    ("conv_ln_b", 512), ("q_lat_norm_g", 256), ("kv_lat_norm_g", 128), ("q_norm_g", 96),
    ("k_norm_g", 96), ("mem_norm_x_g", 1024), ("mem_norm_m_g", 1024), ("mem_q_norm_g", 256),
    ("mem_k_norm_g", 256), ("ffn_norm_g", 1024), ("b_ffn_dw", 5632),
]
WEIGHTS = [
    "mix_norm_g", "w_in", "b_conv_in", "w_conv_dw", "b_conv_dw", "conv_ln_g", "conv_ln_b",
    "q_lat_norm_g", "w_uq", "kv_lat_norm_g", "w_ukv", "q_norm_g", "k_norm_g", "w_out",
    "mem_norm_x_g", "mem_norm_m_g", "w_mem_q", "w_mem_kv", "mem_q_norm_g", "mem_k_norm_g",
    "w_mem_o", "ffn_norm_g", "w_up", "w_ffn_dw", "b_ffn_dw", "w_down",
]


SMALL_PAD = [(-(-n // LANES)) * LANES for _, n in SMALL]
SMALL_ROWS = -(-sum(SMALL_PAD) // (8 * LANES)) * 8
TINY_BASE = [SMALL_ROWS + N_DEV * sum(r for _, _, r in TINY[:i]) for i in range(len(TINY))]
PART_ROWS = SMALL_ROWS + N_DEV * sum(r for _, _, r in TINY)


def _call(body, **kw):
    return pl.pallas_call(body, **kw)


def _params(*sem):
    return pltpu.CompilerParams(dimension_semantics=sem, vmem_limit_bytes=VMEM_LIMIT)


class _Exchange:
    def __init__(self, inputs, out_shape, scratch, start, finish):
        self.inputs, self.out_shape, self.scratch = list(inputs), list(out_shape), list(scratch)
        self.start, self.finish = start, finish


def _run_exchange(ex, name):
    n_in, n_out = len(ex.inputs), len(ex.out_shape)

    def body(*refs):
        parts = refs[:n_in], refs[n_in:n_in + n_out], refs[n_in + n_out:]
        ex.start(*parts)
        ex.finish(*parts)

    return _call(body, name=name, out_shape=ex.out_shape, in_specs=[ANY] * n_in,
                 out_specs=[ANY] * n_out, scratch_shapes=ex.scratch)(*ex.inputs)


def _plan_all_gather(xs):
    n = len(xs)

    def copies(x_refs, out_refs, sems):
        send_sems, recv_sems, local_sems = sems
        x, y, c = lax.axis_index("x"), lax.axis_index("y"), lax.axis_index("c")
        me, sibling = (x, y, c), (x, y, 1 - c)
        chips = [(1 - x, y), (x, 1 - y), (1 - x, 1 - y)]

        def slot(o, px, py, pc):
            return out_refs[o].at[4 * px + 2 * py + pc]

        def copy(o, k, block, to, src=None):
            return pltpu.make_async_remote_copy(
                src_ref=slot(o, *block) if src is None else src, dst_ref=slot(o, *block),
                send_sem=send_sems.at[o, k], recv_sem=recv_sems.at[o, k],
                device_id=to, device_id_type=MESH)

        mine = [pltpu.make_async_copy(x_refs[o], slot(o, *me), local_sems.at[o]) for o in range(n)]
        first = [copy(o, 0, me, sibling, src=x_refs[o]) for o in range(n)]
        first += [copy(o, 1 + j, me, (*chip, c), src=x_refs[o])
                  for j, chip in enumerate(chips) for o in range(n)]
        return me, sibling, chips, copy, mine, first

    def start(x_refs, out_refs, sems):
        _, _, _, _, mine, first = copies(x_refs, out_refs, sems)
        for cp in mine + first:
            cp.start()

    def finish(x_refs, out_refs, sems):
        me, sibling, chips, copy, mine, first = copies(x_refs, out_refs, sems)
        c = me[2]
        passed = []
        for j, chip in enumerate(chips):
            for o in range(n):
                copy(o, 1 + j, (*chip, c), me).wait_recv()
                passed.append(copy(o, 4 + j, (*chip, c), sibling))
                passed[-1].start()
        for o in range(n):
            copy(o, 0, sibling, me).wait_recv()
        for j, chip in enumerate(chips):
            for o in range(n):
                copy(o, 4 + j, (*chip, 1 - c), me).wait_recv()
        for cp in first + passed:
            cp.wait_send()
        for cp in mine:
            cp.wait()

    return _Exchange(
        xs, [jax.ShapeDtypeStruct((N_DEV,) + v.shape, v.dtype) for v in xs],
        [pltpu.SemaphoreType.DMA((n, 7)), pltpu.SemaphoreType.DMA((n, 7)), pltpu.SemaphoreType.DMA((n,))],
        start, finish)


def _all_gather(xs, name):
    return _run_exchange(_plan_all_gather(xs), name)


def _plan_swap_sibling(grads):
    n = len(grads)

    def copies(g_refs, got_refs, sems):
        send_sems, recv_sems = sems
        x, y, c = lax.axis_index("x"), lax.axis_index("y"), lax.axis_index("c")
        return [
            pltpu.make_async_remote_copy(
                src_ref=g_refs[o].at[2 * chip + 1 - c], dst_ref=got_refs[o].at[chip],
                send_sem=send_sems.at[o, chip], recv_sem=recv_sems.at[o, chip],
                device_id=(x, y, 1 - c), device_id_type=MESH)
            for o in range(n) for chip in range(4)]

    def start(g_refs, got_refs, sems):
        for cp in copies(g_refs, got_refs, sems):
            cp.start()

    def finish(g_refs, got_refs, sems):
        for cp in copies(g_refs, got_refs, sems):
            cp.wait()

    return _Exchange(
        grads, [jax.ShapeDtypeStruct((4,) + g.shape[1:], g.dtype) for g in grads],
        [pltpu.SemaphoreType.DMA((n, 4)), pltpu.SemaphoreType.DMA((n, 4))], start, finish)


def _plan_swap_chips(sums):
    n = len(sums)

    def copies(a_refs, r_refs, sems):
        send_sems, recv_sems = sems
        x, y, c = lax.axis_index("x"), lax.axis_index("y"), lax.axis_index("c")
        peers = [(x, 1 - y), (1 - x, y), (1 - x, 1 - y)]
        return [
            pltpu.make_async_remote_copy(
                src_ref=a_refs[o].at[2 * px + py], dst_ref=r_refs[o].at[k],
                send_sem=send_sems.at[o, k], recv_sem=recv_sems.at[o, k],
                device_id=(px, py, c), device_id_type=MESH)
            for k, (px, py) in enumerate(peers) for o in range(n)]

    def start(a_refs, r_refs, sems):
        for cp in copies(a_refs, r_refs, sems):
            cp.start()

    def finish(a_refs, r_refs, sems):
        for cp in copies(a_refs, r_refs, sems):
            cp.wait()

    return _Exchange(
        sums, [jax.ShapeDtypeStruct((3,) + a.shape[1:], a.dtype) for a in sums],
        [pltpu.SemaphoreType.DMA((n, 3)), pltpu.SemaphoreType.DMA((n, 3))], start, finish)


def _shard_block(shape):
    return (None, shape[-2] // ROW_STEPS, shape[-1])


def _rs_add(grads, gots, pos, name):
    n = len(grads)

    def body(pos_ref, *refs):
        for g_ref, t_ref, o_ref in zip(refs[:n], refs[n:2 * n], refs[2 * n:]):
            o_ref[...] = g_ref[...] + t_ref[...]

    in_specs = [pl.BlockSpec(_shard_block(g.shape), lambda a, t, pos: (2 * a + pos[1], t, 0)) for g in grads]
    in_specs += [pl.BlockSpec(_shard_block(g.shape), lambda a, t, pos: (a, t, 0)) for g in gots]
    return _call(
        body, name=name,
        grid_spec=pltpu.PrefetchScalarGridSpec(
            num_scalar_prefetch=1, grid=(4, ROW_STEPS), in_specs=in_specs,
            out_specs=[pl.BlockSpec(_shard_block(g.shape), lambda a, t, pos: (a, t, 0)) for g in gots]),
        out_shape=[jax.ShapeDtypeStruct(g.shape, g.dtype) for g in gots],
        compiler_params=_params("arbitrary", "arbitrary"),
    )(pos, *grads, *gots)


def _adamw_big(sums, recvs, ws, ms, vs, pos, name):
    n = len(sums)

    def body(pos_ref, *refs):
        ins, outs = refs[:7 * n], refs[7 * n:]
        for o in range(n):
            own, r1, r2, r3, w, m, v = [r[...] for r in ins[7 * o:7 * o + 7]]
            g = ((own + r1) + r2) + r3
            for ref, val in zip(outs[4 * o:4 * o + 4], (g,) + _adamw(w, g, m, v)):
                ref[...] = val

    in_specs, args, out_specs, out_shape = [], [], [], []
    for s_, r_, w_, m_, v_ in zip(sums, recvs, ws, ms, vs):
        blk = _shard_block(w_.shape)
        in_specs.append(pl.BlockSpec(blk, lambda t, pos: (pos[0], t, 0)))
        in_specs += [pl.BlockSpec(blk, lambda t, pos, k=k: (k, t, 0)) for k in range(3)]
        in_specs += [pl.BlockSpec(blk, lambda t, pos: (0, t, 0))] * 3
        args += [s_, r_, r_, r_, w_, m_, v_]
        out_specs += [pl.BlockSpec(blk, lambda t, pos: (0, t, 0))] * 4
        out_shape += [jax.ShapeDtypeStruct(w_.shape, F32)] * 4
    return _call(
        body, name=name,
        grid_spec=pltpu.PrefetchScalarGridSpec(
            num_scalar_prefetch=1, grid=(ROW_STEPS,), in_specs=in_specs, out_specs=out_specs),
        out_shape=out_shape, compiler_params=_params("arbitrary"),
    )(pos, *args)


def _tile(n, prefs):
    for t in prefs:
        if n % t == 0:
            return t
    return n


def _mm(a, b, *, name, trans_b=False, add=None, out_dtype=F32, tm=MM_ROWS, b_kblock=0, ex=None,
        rms_bwd=None):
    m, k = a.shape
    n = b.shape[0] if trans_b else b.shape[1]
    tn = _tile(n, (1536, 1408, 1024, 768, 512, 256, 128))
    tk = _tile(k, (3072, 2048, 1536, 1408, 1024, 768, 512, 256, 128))
    nk = k // tk
    has_add = add is not None
    n_post = 3 if rms_bwd else 0
    assert not rms_bwd or tn == n
    grid = (m // tm, n // tn, nk)

    def body(*refs):
        ins, outs, ex_refs = _split_refs(refs, 2 + has_add + n_post, 1 + bool(rms_bwd), ex, int(nk > 1))
        _hosted(ex, ex_refs, grid, True)
        a_ref, b_ref = ins[0], ins[1]
        add_ref = ins[2] if has_add else None
        o_ref = outs[0]
        av = a_ref[...].astype(BF16)
        bv = b_ref[...].astype(BF16)
        if trans_b:
            part = lax.dot_general(av, bv, NT_DIMS, preferred_element_type=F32)
        else:
            part = jnp.dot(av, bv, preferred_element_type=F32)

        def finish(acc):
            if has_add:
                acc = acc + add_ref[...].astype(F32)
            if rms_bwd:
                x_ref, dres_ref, g_ref = ins[2 + has_add:]
                acc, dg = _b_rms(x_ref[...], acc, dres_ref[...], g_ref[...])
                first_rows = pl.program_id(0) == 0

                @pl.when(first_rows)
                def _():
                    outs[1][...] = dg

                @pl.when(jnp.logical_not(first_rows))
                def _():
                    outs[1][...] += dg
            o_ref[...] = acc.astype(o_ref.dtype)

        if nk == 1:
            finish(part)
        else:
            acc_ref = outs[-1]
            kk = pl.program_id(2)

            @pl.when(kk == 0)
            def _():
                acc_ref[...] = part

            @pl.when(kk > 0)
            def _():
                acc_ref[...] += part

            @pl.when(kk == nk - 1)
            def _():
                finish(acc_ref[...])
        _hosted(ex, ex_refs, grid, False)

    in_specs = [pl.BlockSpec((tm, tk), lambda i, j, kk: (i, kk))]
    if trans_b:
        in_specs.append(pl.BlockSpec((tn, tk), lambda i, j, kk: (j, kk + b_kblock * nk)))
    else:
        in_specs.append(pl.BlockSpec((tk, tn), lambda i, j, kk: (kk, j)))
    args = [a, b]
    tile = pl.BlockSpec((tm, tn), lambda i, j, kk: (i, j))
    out_specs, out_shape = [tile], [jax.ShapeDtypeStruct((m, n), out_dtype)]
    if has_add:
        in_specs.append(tile)
        args.append(add)
    if rms_bwd:
        in_specs += [tile, tile, pl.BlockSpec((1, n), lambda i, j, kk: (0, 0))]
        args += list(rms_bwd)
        out_specs.append(pl.BlockSpec((1, n), lambda i, j, kk: (0, 0)))
        out_shape.append(jax.ShapeDtypeStruct((1, n), F32))
    outs, hosted = _host_call(
        body, ex, name, grid, in_specs, out_specs, out_shape, args,
        scratch=[pltpu.VMEM((tm, tn), F32)] if nk > 1 else [])
    res = tuple(outs) if rms_bwd else outs[0]
    return (res, hosted) if ex else res


def _mm_tn(a, b, *, name, ts=TN_ROWS, shard_cols=None):
    s, m = a.shape
    n = b.shape[1]
    tm = _tile(m, (1408, 1024, 768, 512, 256, 128))
    tn = shard_cols or _tile(n, (1536, 1408, 1024, 768, 512, 256, 128))
    if shard_cols:
        out_spec = pl.BlockSpec((None, tm, tn), lambda i, j, kk: (j, i, 0))
        out_shape = jax.ShapeDtypeStruct((n // tn, m, tn), F32)
    else:
        out_spec = pl.BlockSpec((tm, tn), lambda i, j, kk: (i, j))
        out_shape = jax.ShapeDtypeStruct((m, n), F32)

    def body(a_ref, b_ref, o_ref):
        kk = pl.program_id(2)
        part = jnp.dot(a_ref[...].astype(BF16).T, b_ref[...].astype(BF16),
                       preferred_element_type=F32)

        @pl.when(kk == 0)
        def _():
            o_ref[...] = part

        @pl.when(kk > 0)
        def _():
            o_ref[...] += part

    return _call(
        body, name=name, grid=(m // tm, n // tn, s // ts),
        in_specs=[pl.BlockSpec((ts, tm), lambda i, j, kk: (kk, i)),
                  pl.BlockSpec((ts, tn), lambda i, j, kk: (kk, j))],
        out_specs=out_spec, out_shape=out_shape,
        compiler_params=_params("parallel", "parallel", "arbitrary"),
    )(a, b)


def _rowwise(fn, rows, consts, row_outs, acc_outs, *, name, tm=TM, n_rows=None, ex=None):
    rows = [r if isinstance(r, tuple) else (r, r.shape[1], 0, 0) for r in rows]
    s = n_rows or rows[0][0].shape[0]
    nr, nc, no, na = len(rows), len(consts), len(row_outs), len(acc_outs)
    grid = (s // tm,)

    def body(*refs):
        ins, outs_, ex_refs = _split_refs(refs, nr + nc, no + na, ex)
        _hosted(ex, ex_refs, grid, True)
        r_in, c_in = ins[:nr], ins[nr:]
        o_refs, a_refs = outs_[:no], outs_[no:]
        outs = fn(*[r[...] for r in r_in], *[c[...] for c in c_in])
        for r, v in zip(o_refs, outs[:no]):
            r[...] = v.astype(r.dtype)
        if na:
            i = pl.program_id(0)

            @pl.when(i == 0)
            def _():
                for r, v in zip(a_refs, outs[no:]):
                    r[...] = v.astype(F32)

            @pl.when(i > 0)
            def _():
                for r, v in zip(a_refs, outs[no:]):
                    r[...] += v.astype(F32)
        _hosted(ex, ex_refs, grid, False)

    in_specs = [r[1] if isinstance(r[1], pl.BlockSpec) else
                pl.BlockSpec((tm, r[1]), lambda i, cb=r[2], rb=r[3]: (i + rb, cb)) for r in rows]
    in_specs += [pl.BlockSpec(c.shape, lambda i: (0, 0)) for c in consts]
    out_specs = [pl.BlockSpec((tm, w), lambda i: (i, 0)) for w, _ in row_outs]
    out_specs += [pl.BlockSpec(sh, lambda i: (0, 0)) for sh in acc_outs]
    out_shape = [jax.ShapeDtypeStruct((s, w), dt) for w, dt in row_outs]
    out_shape += [jax.ShapeDtypeStruct(sh, F32) for sh in acc_outs]
    res, hosted = _host_call(body, ex, name, grid, in_specs, out_specs, out_shape,
                             [r[0] for r in rows] + list(consts))
    return (res, hosted) if ex else res


def _rms(x, g, n=None):
    ms = jnp.sum(x * x, axis=-1, keepdims=True) / float(n or x.shape[-1])
    return x * lax.rsqrt(ms + EPS) * g


def _layer_norm(x, g, b):
    mu = jnp.sum(x, axis=-1, keepdims=True) / float(x.shape[-1])
    xc = x - mu
    var = jnp.sum(xc * xc, axis=-1, keepdims=True) / float(x.shape[-1])
    return xc * lax.rsqrt(var + EPS) * g + b


def _silu(x):
    return x * jax.nn.sigmoid(x)


@jax.custom_vjp
def _rope(y, cos, sin_a, sin_b):
    return y * cos + pltpu.roll(y, 112, 1) * sin_a + pltpu.roll(y, 16, 1) * sin_b


def _rope_fwd(y, cos, sin_a, sin_b):
    return _rope(y, cos, sin_a, sin_b), (cos, sin_a, sin_b)


def _rope_bwd(res, ct):
    cos, sin_a, sin_b = res
    dy = ct * cos + pltpu.roll(ct * sin_a, 16, 1) + pltpu.roll(ct * sin_b, 112, 1)
    return dy, jnp.zeros_like(cos), jnp.zeros_like(sin_a), jnp.zeros_like(sin_b)


_rope.defvjp(_rope_fwd, _rope_bwd)


def _qk_head(xh, g, cos, sin_a, sin_b):
    return _rope(_rms(xh, g, MLA_QK), cos, sin_a, sin_b)


def _heads(x, width):
    return [x[:, h * width:(h + 1) * width] for h in range(x.shape[1] // width)]


def _f_rms(x, g):
    return (_rms(x, g),)


def _f_rope_tab(pos, inv_freq):
    ang = pos.astype(F32) * inv_freq
    lane = lax.broadcasted_iota(jnp.int32, ang.shape, 1)
    sn = jnp.sin(ang)
    first = (lane >= MLA_NOPE) & (lane < MLA_NOPE + MLA_ROPE // 2)
    second = (lane >= MLA_NOPE + MLA_ROPE // 2) & (lane < MLA_QK)
    return jnp.cos(ang), jnp.where(first, -sn, 0.0), jnp.where(second, sn, 0.0)


def _mix_pre(za, zg, zcq, zckv, ba, bg, gq, gkv):
    u0 = (za + ba) * jax.nn.sigmoid(zg + bg)
    return u0, _rms(zcq, gq), _rms(zckv, gkv)


def _ln_silu(c1, bdw, lg, lb):
    return _silu(_layer_norm(c1 + bdw, lg, lb))


def _f_qk_prep(q0, kn, kr, cos, sa, sb, gq, gk):
    qs = [_qk_head(xh, gq, cos, sa, sb) * Q_SCALE for xh in _heads(q0, LANES)]
    ks = [_qk_head(xh + kr, gk, cos, sa, sb) for xh in _heads(kn, LANES)]
    return jnp.concatenate(qs, axis=1), jnp.concatenate(ks, axis=1)


def _act(cg, cv, bg, bv):
    return _silu(cg + bg) * (cv + bv)


def _f_mem_k(kk, g):
    return (jnp.concatenate([_rms(xh, g) for xh in _heads(kk, MEM_HEAD_DIM)], axis=1),)


def _mem_probs(qn, kmh):
    s = lax.dot_general(qn.astype(BF16), kmh, NT_DIMS, preferred_element_type=F32) * MEM_SCALE
    e = jnp.exp(s - jnp.max(s, axis=-1, keepdims=True))
    return e / jnp.sum(e, axis=-1, keepdims=True)


def _f_mem_attn(qm0, km, vm, g):
    outs = []
    for h, xh in enumerate(_heads(qm0, MEM_HEAD_DIM)):
        sl = slice(h * MEM_HEAD_DIM, (h + 1) * MEM_HEAD_DIM)
        p = _mem_probs(_rms(xh, g), km[:, sl])
        outs.append(jnp.dot(p.astype(BF16), vm[:, sl].astype(BF16), preferred_element_type=F32))
    return (jnp.concatenate(outs, axis=1),)


def _f_loss(y, t):
    e = y - t
    return e * (1.0 / D_MODEL), jnp.sum(e * e, axis=0, keepdims=True)


def _b_rms(x, dh, dres, g):
    _, vjp = jax.vjp(_rms, x, g)
    dx, dg = vjp(dh)
    return dx + dres, dg


def _b_rms_nores(x, dh, g):
    _, vjp = jax.vjp(_rms, x, g)
    dx, dg = vjp(dh)
    return dx, dg


def _b_mix_pre(za, zg, zcq, zckv, du0, dcqn, dckvn, dkr, ba, bg, gq, gkv):
    _, vjp = jax.vjp(_mix_pre, za, zg, zcq, zckv, ba, bg, gq, gkv)
    dza, dzg, dzcq, dzckv, dba, dbg, dgq, dgkv = vjp((du0, dcqn, dckvn))
    return jnp.concatenate([dza, dzg, dzcq, dzckv, dkr], axis=1), dba, dbg, dgq, dgkv


def _b_ln_silu(c1, du, bdw, lg, lb):
    _, vjp = jax.vjp(_ln_silu, c1, bdw, lg, lb)
    return vjp(du)


def _b_qk_prep(q0, kn, kr, cos, sa, sb, dq, dk, gq, gk):
    head = lambda xh, g: _qk_head(xh, g, cos, sa, sb)
    dk = dk * LN2
    dq0, dkn = [], []
    dkr = jnp.zeros_like(kr)
    dgq = jnp.zeros_like(gq)
    dgk = jnp.zeros_like(gk)
    for h, xh in enumerate(_heads(q0, LANES)):
        _, vjp = jax.vjp(head, xh, gq)
        dx, dg = vjp(dq[h, 0].T * ATT_SCALE)
        dq0.append(dx)
        dgq = dgq + dg
    for xh, ct in zip(_heads(kn, LANES), _heads(dk, LANES)):
        _, vjp = jax.vjp(head, xh + kr, gk)
        dx, dg = vjp(ct)
        dkn.append(dx)
        dkr = dkr + dx
        dgk = dgk + dg
    return jnp.concatenate(dq0, axis=1), jnp.concatenate(dkn, axis=1), dkr, dgq, dgk


def _b_mem_k(kk, dkm, g):
    dkk = []
    dg = jnp.zeros_like(g)
    for xh, ct in zip(_heads(kk, MEM_HEAD_DIM), _heads(dkm, MEM_HEAD_DIM)):
        _, vjp = jax.vjp(_rms, xh, g)
        dx, dgh = vjp(ct)
        dkk.append(dx)
        dg = dg + dgh
    return jnp.concatenate(dkk, axis=1), dg


def _b_mem_attn(dom, qm0, km, vm, g):
    dq0, dkm, dvm = [], [], []
    dg = jnp.zeros_like(g)
    for h, (xh, doh) in enumerate(zip(_heads(qm0, MEM_HEAD_DIM), _heads(dom, MEM_HEAD_DIM))):
        sl = slice(h * MEM_HEAD_DIM, (h + 1) * MEM_HEAD_DIM)
        kmh, vmh = km[:, sl], vm[:, sl].astype(BF16)
        qn, vjp = jax.vjp(_rms, xh, g)
        p = _mem_probs(qn, kmh)
        dob = doh.astype(BF16)
        dp = lax.dot_general(dob, vmh, NT_DIMS, preferred_element_type=F32)
        ds = (p * (dp - jnp.sum(dp * p, axis=-1, keepdims=True)) * MEM_SCALE).astype(BF16)
        dqn = jnp.dot(ds, kmh, preferred_element_type=F32)
        dkm.append(jnp.dot(ds.T, qn.astype(BF16), preferred_element_type=F32))
        dvm.append(jnp.dot(p.astype(BF16).T, dob, preferred_element_type=F32))
        dx, dgh = vjp(dqn)
        dq0.append(dx)
        dg = dg + dgh
    return (jnp.concatenate(dq0, axis=1), jnp.concatenate(dkm, axis=1),
            jnp.concatenate(dvm, axis=1), dg)


def _adamw(w, g, m, v):
    m = ADAM_B1 * m + (1.0 - ADAM_B1) * g
    v = ADAM_B2 * v + (1.0 - ADAM_B2) * jnp.square(g)
    m_hat = m / (1.0 - ADAM_B1 ** ADAM_STEP)
    v_hat = v / (1.0 - ADAM_B2 ** ADAM_STEP)
    delta = -ADAM_LR * (m_hat / (jnp.sqrt(v_hat) + ADAM_EPS) + ADAM_WD * w)
    return delta, m, v


def _adamw_small(parts, small, tiny, name):
    def body(*refs):
        p_ref, ins, outs = refs[0], refs[1:4 + 3 * len(TINY)], refs[4 + 3 * len(TINY):]
        me = 4 * lax.axis_index("x") + 2 * lax.axis_index("y") + lax.axis_index("c")
        groups = [(0, SMALL_ROWS)]
        groups += [(pl.multiple_of(base + me * rows, 8), rows) for base, (_, _, rows) in zip(TINY_BASE, TINY)]
        for k, (start, rows) in enumerate(groups):
            g = p_ref[0, pl.ds(start, rows), :]
            for d in range(1, N_DEV):
                g = g + p_ref[d, pl.ds(start, rows), :]
            w, m, v = [r[...] for r in ins[3 * k:3 * k + 3]]
            for ref, val in zip(outs[4 * k:4 * k + 4], (g,) + _adamw(w, g, m, v)):
                ref[...] = val

    args = list(small) + [t for grp in tiny for t in grp]
    out_shape = []
    for grp in [small] + list(tiny):
        out_shape += [jax.ShapeDtypeStruct(grp[0].shape, F32)] * 4
    return _call(body, name=name, out_shape=out_shape)(parts, *args)


def _conv_fwd(x, w, name):
    s, ch = x.shape
    kw = w.shape[0]
    halo = -(-(kw - 1) // 8) * 8
    r = CONV_ROWS
    n = s // r

    def chunk(window, wv):
        acc = jnp.zeros((r, LANES), F32)
        for k in range(kw):
            shift = kw - 1 - k
            sh = window if shift == 0 else pltpu.roll(window, shift, 0)
            acc = acc + sh[halo:halo + r] * wv[k:k + 1]
        return acc

    def body(x_ref, w_ref, y_ref):
        wv = w_ref[...]
        first = jnp.concatenate([jnp.zeros((halo, LANES), F32), x_ref[0:r]], axis=0)
        y_ref[0:r] = chunk(first, wv)

        def step(i, carry):
            base = pl.multiple_of(i * r, 8)
            y_ref[pl.ds(base, r)] = chunk(x_ref[pl.ds(base - halo, r + halo)], wv)
            return carry

        lax.fori_loop(1, n, step, 0)

    return _call(
        body, name=name, grid=(ch // LANES,),
        in_specs=[pl.BlockSpec((s, LANES), lambda c: (0, c)), pl.BlockSpec((kw, LANES), lambda c: (0, c))],
        out_specs=pl.BlockSpec((s, LANES), lambda c: (0, c)),
        out_shape=jax.ShapeDtypeStruct((s, ch), F32), compiler_params=_params("parallel"),
    )(x, w)


def _conv_bwd(dy, x, w, name, ex=None):
    s, ch = x.shape
    kw = w.shape[0]
    halo = -(-(kw - 1) // 8) * 8
    r = CONV_ROWS
    n = s // r

    def dx_chunk(window, wv):
        acc = jnp.zeros((r, LANES), F32)
        for k in range(kw):
            shift = kw - 1 - k
            sh = window if shift == 0 else pltpu.roll(window, r + halo - shift, 0)
            acc = acc + sh[0:r] * wv[k:k + 1]
        return acc

    def dw_chunk(xwin, dyc, acc_ref):
        for k in range(kw):
            shift = kw - 1 - k
            sh = xwin if shift == 0 else pltpu.roll(xwin, shift, 0)
            prod = sh[halo:halo + r] * dyc
            acc_ref[k] += jnp.sum(prod.reshape(r // 8, 8, LANES), axis=0)

    grid = (ch // LANES,)

    def body(*refs):
        (dy_ref, x_ref, w_ref), (dx_ref, dw_ref, acc_ref), ex_refs = _split_refs(refs, 3, 2, ex, 1)
        _hosted(ex, ex_refs, grid, True)
        wv = w_ref[...]
        acc_ref[...] = jnp.zeros_like(acc_ref)
        xfirst = jnp.concatenate([jnp.zeros((halo, LANES), F32), x_ref[0:r]], axis=0)
        dw_chunk(xfirst, dy_ref[0:r], acc_ref)
        last = jnp.concatenate([dy_ref[s - r:s], jnp.zeros((halo, LANES), F32)], axis=0)
        dx_ref[s - r:s] = dx_chunk(last, wv)

        def step(i, carry):
            base = pl.multiple_of(i * r, 8)
            dw_chunk(x_ref[pl.ds(base - halo, r + halo)], dy_ref[pl.ds(base, r)], acc_ref)
            prev = pl.multiple_of((i - 1) * r, 8)
            dx_ref[pl.ds(prev, r)] = dx_chunk(dy_ref[pl.ds(prev, r + halo)], wv)
            return carry

        lax.fori_loop(1, n, step, 0)
        dw_ref[...] = jnp.sum(acc_ref[...], axis=1)
        _hosted(ex, ex_refs, grid, False)

    spec = pl.BlockSpec((s, LANES), lambda c: (0, c))
    wspec = pl.BlockSpec((kw, LANES), lambda c: (0, c))
    return _host_call(
        body, ex, name, grid, [spec, spec, wspec], [spec, wspec],
        [jax.ShapeDtypeStruct((s, ch), F32), jax.ShapeDtypeStruct((kw, ch), F32)], (dy, x, w),
        scratch=[pltpu.VMEM((kw, 8, LANES), F32)])


HALO = 8


def _conv3(win, w, rows):
    return (pltpu.roll(win, 2, 0)[HALO:HALO + rows] * w[0:1] + pltpu.roll(win, 1, 0)[HALO:HALO + rows] * w[1:2]
            + win[HALO:HALO + rows] * w[2:3])


def _ffn_mid_bwd(up_g, up_v, dy, w_dn, w_g, w_v, b_g, b_v, name):
    s, width = up_g.shape
    tm, tc = TM, FF_PAD
    n_row = s // tm

    def body(dy_ref, ndy_ref, wd_ref, pg_ref, g_ref, ng_ref, pv_ref, v_ref, nv_ref, wg_ref, wv_ref,
             bg_ref, bv_ref, dug_ref, duv_ref, dwg_ref, dwv_ref, dbg_ref, dbv_ref):
        i = pl.program_id(1)
        first = (i > 0).astype(F32)
        last = (i < n_row - 1).astype(F32)
        ext = tm + HALO
        wg, wv, wd = wg_ref[...], wv_ref[...], wd_ref[...]
        ndy = jnp.concatenate([ndy_ref[...] * last, jnp.zeros((HALO, D_MODEL), F32)], axis=0)
        d_ext = jnp.concatenate([
            lax.dot_general(dy_ref[...].astype(BF16), wd, NT_DIMS, preferred_element_type=F32),
            lax.dot_general(ndy.astype(BF16), wd, NT_DIMS, preferred_element_type=F32)[:HALO]], axis=0)
        xg = jnp.concatenate([pg_ref[...] * first, g_ref[...], ng_ref[...]], axis=0)
        xv = jnp.concatenate([pv_ref[...] * first, v_ref[...], nv_ref[...]], axis=0)
        taps = [[pltpu.roll(x, 2, 0), pltpu.roll(x, 1, 0), x] for x in (xg, xv)]
        conv = [sum(t[HALO:HALO + ext] * w[k:k + 1] for k, t in enumerate(tp)) for tp, w in zip(taps, (wg, wv))]
        _, vjp = jax.vjp(lambda cg_, cv_: _act(cg_, cv_, bg_ref[...], bv_ref[...]), *conv)
        dcg, dcv = vjp(d_ext)
        results = []
        for tp, w, dc in ((taps[0], wg, dcg), (taps[1], wv, dcv)):
            dup = (dc[:tm] * w[2:3] + pltpu.roll(dc, ext - 1, 0)[:tm] * w[1:2]
                   + pltpu.roll(dc, ext - 2, 0)[:tm] * w[0:1])
            own = dc[:tm]
            dw = jnp.concatenate(
                [jnp.sum(own * t[HALO:HALO + tm], axis=0, keepdims=True) for t in tp], axis=0)
            results.append((dup, dw, jnp.sum(own, axis=0, keepdims=True)))
        (dug, dwg, dbg), (duv, dwv, dbv) = results
        dug_ref[...] = dug.astype(dug_ref.dtype)
        duv_ref[...] = duv.astype(duv_ref.dtype)

        @pl.when(i == 0)
        def _():
            dwg_ref[...], dwv_ref[...], dbg_ref[...], dbv_ref[...] = dwg, dwv, dbg, dbv

        @pl.when(i > 0)
        def _():
            dwg_ref[...] += dwg
            dwv_ref[...] += dwv
            dbg_ref[...] += dbg
            dbv_ref[...] += dbv

    per = tm // HALO
    tile = pl.BlockSpec((tm, tc), lambda c, i: (i, c))
    prev = pl.BlockSpec((HALO, tc), lambda c, i: (jnp.maximum(i * per - 1, 0), c))
    nxt = pl.BlockSpec((HALO, tc), lambda c, i: (jnp.minimum((i + 1) * per, s // HALO - 1), c))
    wspec = pl.BlockSpec((FFN_CONV_WIDTH, tc), lambda c, i: (0, c))
    bspec = pl.BlockSpec((1, tc), lambda c, i: (0, c))
    wide = jax.ShapeDtypeStruct((s, width), BF16)
    dy_tile = pl.BlockSpec((tm, D_MODEL), lambda c, i: (i, 0))
    dy_next = pl.BlockSpec((HALO, D_MODEL), lambda c, i: (jnp.minimum((i + 1) * per, s // HALO - 1), 0))
    return _call(
        body, name=name, grid=(width // tc, n_row),
        in_specs=[dy_tile, dy_next, pl.BlockSpec((tc, D_MODEL), lambda c, i: (c, 0)),
                  prev, tile, nxt, prev, tile, nxt, wspec, wspec, bspec, bspec],
        out_specs=[tile, tile, wspec, wspec, bspec, bspec],
        out_shape=[wide, wide] + [jax.ShapeDtypeStruct((FFN_CONV_WIDTH, width), F32)] * 2
        + [jax.ShapeDtypeStruct((1, width), F32)] * 2,
        compiler_params=_params("parallel", "arbitrary"),
    )(dy, dy, w_dn, up_g, up_g, up_g, up_v, up_v, up_v, w_g, w_v, b_g, b_v)


def _ffn_fwd(h3, w_up, w_g, w_v, b_g, b_v, w_dn, x2, target, name):
    s = h3.shape[0]
    tm, tc = TM, FF_PAD
    n_col = D_FF_PAD // tc

    def body(h_ref, wug_ref, wuv_ref, wg_ref, wv_ref, bg_ref, bv_ref, wd_ref, x_ref, t_ref,
             ug_ref, uv_ref, act_ref, dy_ref, sq_ref, halo_g, halo_v, y_acc):
        i, c = pl.program_id(0), pl.program_id(1)
        h = h_ref[...]
        up_g = jnp.dot(h, wug_ref[...], preferred_element_type=F32)
        up_v = jnp.dot(h, wuv_ref[...], preferred_element_type=F32)
        ug_ref[...] = up_g
        uv_ref[...] = up_v
        has_prev = i > 0
        prev_g = jnp.where(has_prev, halo_g[c], 0.0)
        prev_v = jnp.where(has_prev, halo_v[c], 0.0)
        halo_g[c] = up_g[tm - HALO:]
        halo_v[c] = up_v[tm - HALO:]
        cg = _conv3(jnp.concatenate([prev_g, up_g], axis=0), wg_ref[...], tm)
        cv = _conv3(jnp.concatenate([prev_v, up_v], axis=0), wv_ref[...], tm)
        act = _act(cg, cv, bg_ref[...], bv_ref[...]).astype(BF16)
        act_ref[...] = act
        part = jnp.dot(act, wd_ref[...], preferred_element_type=F32)

        @pl.when(c == 0)
        def _():
            y_acc[...] = part

        @pl.when(c > 0)
        def _():
            y_acc[...] += part

        @pl.when(c == n_col - 1)
        def _():
            dy, sq = _f_loss(x_ref[...] + y_acc[...], t_ref[...])
            dy_ref[...] = dy

            @pl.when(i == 0)
            def _():
                sq_ref[...] = sq

            @pl.when(i > 0)
            def _():
                sq_ref[...] += sq

    row = lambda w: pl.BlockSpec((tm, w), lambda i, c: (i, 0))
    tile = pl.BlockSpec((tm, tc), lambda i, c: (i, c))
    wspec = pl.BlockSpec((FFN_CONV_WIDTH, tc), lambda i, c: (0, c))
    bspec = pl.BlockSpec((1, tc), lambda i, c: (0, c))
    wide = jax.ShapeDtypeStruct((s, D_FF_PAD), F32)
    return _call(
        body, name=name, grid=(s // tm, n_col),
        in_specs=[row(D_MODEL),
                  pl.BlockSpec((None, D_MODEL, tc), lambda i, c: (c, 0, 0)),
                  pl.BlockSpec((None, D_MODEL, tc), lambda i, c: (n_col + c, 0, 0)),
                  wspec, wspec, bspec, bspec, pl.BlockSpec((tc, D_MODEL), lambda i, c: (c, 0)),
                  row(D_MODEL), row(D_MODEL)],
        out_specs=[tile, tile, tile, row(D_MODEL), pl.BlockSpec((1, D_MODEL), lambda i, c: (0, 0))],
        out_shape=[wide, wide, jax.ShapeDtypeStruct((s, D_FF_PAD), BF16),
                   jax.ShapeDtypeStruct((s, D_MODEL), F32), jax.ShapeDtypeStruct((1, D_MODEL), F32)],
        scratch_shapes=[pltpu.VMEM((n_col, HALO, tc), F32), pltpu.VMEM((n_col, HALO, tc), F32),
                        pltpu.VMEM((tm, D_MODEL), F32)],
        compiler_params=_params("arbitrary", "arbitrary"),
    )(h3, w_up, w_up, w_g, w_v, b_g, b_v, w_dn, x2, target)


def _chunk_mask(rows_are_queries):
    a = lax.broadcasted_iota(jnp.int32, (TQ, TQ), 0) // CHUNK
    b = lax.broadcasted_iota(jnp.int32, (TQ, TQ), 1) // CHUNK
    return (b <= a) if rows_are_queries else (a <= b)


def _head_lanes(hh):
    return slice(hh * LANES, (hh + 1) * LANES)


def _to_row(col):
    return jnp.broadcast_to(col, (TQ, LANES)).T[0:1, :]


def _flash_specs(s, heads=HEADS_PER_STEP):
    width = heads * LANES
    tile = pl.BlockSpec((TQ, width), lambda h, i: (i, h))
    whole = pl.BlockSpec((s, width), lambda h, i: (0, h))
    row_tile = pl.BlockSpec((heads, 1, 1, TQ), lambda h, i: (h, i, 0, 0))
    row_whole = pl.BlockSpec((heads, s // TQ, 1, TQ), lambda h, i: (h, 0, 0, 0))
    return tile, whole, row_tile, row_whole


def _split_refs(refs, n_in, n_out, ex, n_scratch=0):
    e_in, e_out = (len(ex.inputs), len(ex.out_shape)) if ex else (0, 0)
    a, b, c = n_in + e_in, n_in + e_in + n_out, n_in + e_in + n_out + e_out
    return refs[:n_in], refs[a:b] + refs[c:c + n_scratch], (refs[n_in:a], refs[b:c], refs[c + n_scratch:])


def _hosted(ex, ex_refs, grid, when_first):
    if ex is None:
        return
    ids = [pl.program_id(d) for d in range(len(grid))]
    cond = functools.reduce(
        lambda p, q_: p & q_, [i == (0 if when_first else g - 1) for i, g in zip(ids, grid)])

    @pl.when(cond)
    def _():
        (ex.start if when_first else ex.finish)(*ex_refs)


def _host_call(body, ex, name, grid, in_specs, out_specs, out_shape, args, scratch=()):
    e_in, e_out = (len(ex.inputs), len(ex.out_shape)) if ex else (0, 0)
    res = _call(
        body, name=name, grid=grid, in_specs=list(in_specs) + [ANY] * e_in,
        out_specs=list(out_specs) + [ANY] * e_out,
        out_shape=list(out_shape) + (ex.out_shape if ex else []),
        scratch_shapes=list(scratch) + (ex.scratch if ex else []),
        compiler_params=_params(*["arbitrary"] * len(grid)),
    )(*args, *(ex.inputs if ex else []))
    return res[:len(out_shape)], res[len(out_shape):]


def _flash_fwd(q, k, v, name, ex=None):
    s = q.shape[0]
    nq = s // TQ
    heads = FWD_HEADS
    grid = (MLA_HEADS // heads, nq)

    def body(*refs):
        (q_ref, k_ref, v_ref), (o_ref, lse_row_ref), ex_refs = _split_refs(refs, 3, 2, ex)
        _hosted(ex, ex_refs, grid, True)
        i = pl.program_id(1)
        qs = [q_ref[:, _head_lanes(hh)] for hh in range(heads)]

        def scores(j, hh):
            kj = k_ref[pl.ds(pl.multiple_of(j * TQ, TQ), TQ), _head_lanes(hh)]
            return lax.dot_general(kj, qs[hh], NT_DIMS, preferred_element_type=F32)

        def update(j, sc, m_prev, l_prev, acc, hh):
            vt = v_ref[pl.ds(pl.multiple_of(j * TQ, TQ), TQ), _head_lanes(hh)].T
            m_new = jnp.maximum(m_prev, jnp.max(sc, axis=0, keepdims=True))
            alpha = jnp.exp2(m_prev - m_new)
            p = jnp.exp2(sc - m_new)
            l_new = alpha * l_prev + jnp.sum(p, axis=0, keepdims=True)
            acc = acc * alpha + jnp.dot(vt, p.astype(BF16), preferred_element_type=F32)
            return m_new, l_new, acc

        def step(j, carry):
            out = []
            for hh in range(heads):
                sc, m_prev, l_prev, acc = carry[hh]
                out.append((scores(j + 1, hh),) + update(j, sc, m_prev, l_prev, acc, hh))
            return tuple(out)

        init = tuple((scores(0, hh), jnp.full((1, TQ), NEG, F32), jnp.zeros((1, TQ), F32),
                      jnp.zeros((LANES, TQ), F32)) for hh in range(heads))
        carry = lax.fori_loop(0, i, step, init)
        for hh, (sc, m_prev, l_prev, acc) in enumerate(carry):
            sc = jnp.where(_chunk_mask(False), sc, NEG)
            m_fin, l_fin, acc = update(i, sc, m_prev, l_prev, acc, hh)
            o_ref[:, _head_lanes(hh)] = (acc / l_fin).T
            lse_row_ref[hh, 0] = m_fin + jnp.log2(l_fin)
        _hosted(ex, ex_refs, grid, False)

    tile, whole, row_tile, _ = _flash_specs(s, heads)
    return _host_call(
        body, ex, name, grid, [tile, whole, whole], [tile, row_tile],
        [jax.ShapeDtypeStruct((s, MLA_HEADS * LANES), F32),
         jax.ShapeDtypeStruct((MLA_HEADS, nq, 1, TQ), F32)], (q, k, v))


def _attn_delta(do, o, name):
    s = do.shape[0]

    def body(do_ref, o_ref, d_ref):
        for h in range(MLA_HEADS):
            prod = do_ref[:, _head_lanes(h)] * o_ref[:, _head_lanes(h)]
            d_ref[h, 0] = _to_row(jnp.sum(prod, axis=-1, keepdims=True))

    tile = pl.BlockSpec((TQ, MLA_HEADS * LANES), lambda i: (i, 0))
    return _call(
        body, name=name, grid=(s // TQ,), in_specs=[tile, tile],
        out_specs=pl.BlockSpec((MLA_HEADS, 1, 1, TQ), lambda i: (0, i, 0, 0)),
        out_shape=jax.ShapeDtypeStruct((MLA_HEADS, s // TQ, 1, TQ), F32),
        compiler_params=_params("parallel"),
    )(do, o)


def _flash_bwd(q, k, v, do, lse_row, delta_row, name, ex=None):
    s = q.shape[0]
    nq = s // TQ
    grid = (MLA_HEADS // HEADS_PER_STEP, nq)

    def body(*refs):
        ins, (dqt_ref, dk_ref, dv_ref), ex_refs = _split_refs(refs, 6, 3, ex)
        q_ref, k_ref, v_ref, do_ref, lse_row_ref, delta_row_ref = ins
        _hosted(ex, ex_refs, grid, True)
        j = pl.program_id(1)

        @pl.when(j == 0)
        def _():
            dqt_ref[...] = jnp.zeros_like(dqt_ref)

        kjs = [k_ref[:, _head_lanes(hh)] for hh in range(HEADS_PER_STEP)]
        vjs = [v_ref[:, _head_lanes(hh)] for hh in range(HEADS_PER_STEP)]
        kts = [kj.T for kj in kjs]

        def step(i, carry, masked):
            base = pl.multiple_of(i * TQ, TQ)
            out = []
            for hh in range(HEADS_PER_STEP):
                dk, dv = carry[hh]
                qi = q_ref[pl.ds(base, TQ), _head_lanes(hh)]
                dob = do_ref[pl.ds(base, TQ), _head_lanes(hh)].astype(BF16)
                sc_t = lax.dot_general(kjs[hh], qi, NT_DIMS, preferred_element_type=F32)
                if masked:
                    sc_t = jnp.where(_chunk_mask(False), sc_t, NEG)
                p_t = jnp.exp2(sc_t - lse_row_ref[hh, i])
                dv = dv + jnp.dot(p_t.astype(BF16), dob, preferred_element_type=F32)
                dp_t = lax.dot_general(vjs[hh], dob, NT_DIMS, preferred_element_type=F32)
                ds_t = (p_t * (dp_t - delta_row_ref[hh, i])).astype(BF16)
                dk = dk + jnp.dot(ds_t, qi, preferred_element_type=F32)
                dqt_ref[hh, i] += jnp.dot(kts[hh], ds_t, preferred_element_type=F32)
                out.append((dk, dv))
            return tuple(out)

        zero = jnp.zeros((TQ, LANES), F32)
        carry = step(j, tuple((zero, zero) for _ in range(HEADS_PER_STEP)), True)
        carry = lax.fori_loop(j + 1, nq, functools.partial(step, masked=False), carry)
        for hh, (dk, dv) in enumerate(carry):
            dk_ref[:, _head_lanes(hh)] = dk
            dv_ref[:, _head_lanes(hh)] = dv.astype(dv_ref.dtype)
        _hosted(ex, ex_refs, grid, False)

    tile, whole, _, row_whole = _flash_specs(s)
    dqt_spec = pl.BlockSpec((HEADS_PER_STEP, nq, LANES, TQ), lambda h, j: (h, 0, 0, 0))
    wide = lambda dt: jax.ShapeDtypeStruct((s, MLA_HEADS * LANES), dt)
    return _host_call(
        body, ex, name, grid, [whole, tile, tile, whole, row_whole, row_whole], [dqt_spec, tile, tile],
        [jax.ShapeDtypeStruct((MLA_HEADS, nq, LANES, TQ), F32), wide(F32), wide(BF16)],
        (q, k, v, do, lse_row, delta_row))


def _side_by_side(g):
    return g.transpose(1, 0, 2).reshape(g.shape[1], N_DEV * g.shape[2])


def _col_shards(g):
    return g.reshape(g.shape[0], N_DEV, g.shape[1] // N_DEV).transpose(1, 0, 2)


def _pad_last(v, to):
    return jnp.pad(v, [(0, 0)] * (v.ndim - 1) + [(0, to - v.shape[-1])])


def _tiny_rows(v, rows):
    flat = v.reshape(v.shape[:-2] + (-1,))
    return _pad_last(flat, rows * LANES).reshape(v.shape[:-2] + (rows, LANES))


def _pack_small(vals):
    parts = []
    for (n, size), pad in zip(SMALL, SMALL_PAD):
        parts.append(jnp.pad(vals[n].reshape(-1), (0, pad - size)))
    flat = jnp.concatenate(parts)
    return jnp.pad(flat, (0, SMALL_ROWS * LANES - flat.shape[0])).reshape(SMALL_ROWS, LANES)


def _unpack_small(packed):
    flat = packed.reshape(-1)
    out, off = {}, 0
    for (n, size), pad in zip(SMALL, SMALL_PAD):
        out[n] = flat[off:off + size].reshape(1, size)
        off += pad
    return out


def _pad_heads(w, per_head, axis):
    shape = list(w.shape)
    shape[axis:axis + 1] = [MLA_HEADS, per_head]
    w = w.reshape(shape)
    pad = [(0, 0)] * len(shape)
    pad[axis + 1] = (0, LANES - per_head)
    w = jnp.pad(w, pad)
    shape[axis:axis + 2] = [MLA_HEADS * LANES]
    return w.reshape(shape)


def _unpad_heads(w, per_head, axis):
    shape = list(w.shape)
    shape[axis:axis + 1] = [MLA_HEADS, LANES]
    w = w.reshape(shape)
    w = lax.slice_in_dim(w, 0, per_head, axis=axis + 1)
    shape[axis:axis + 2] = [MLA_HEADS * per_head]
    return w.reshape(shape)


def _row(v, pad_to=None):
    v = v.reshape(1, -1)
    if pad_to is not None:
        v = jnp.pad(v, ((0, 0), (0, pad_to - v.shape[1])))
    return v


def kernel(x, mem, positions, mix_norm_g, w_in, b_conv_in, w_conv_dw, b_conv_dw, conv_ln_g, conv_ln_b, q_lat_norm_g, w_uq, kv_lat_norm_g, w_ukv, q_norm_g, k_norm_g, w_out, mem_norm_x_g, mem_norm_m_g, w_mem_q, w_mem_kv, mem_q_norm_g, mem_k_norm_g, w_mem_o, ffn_norm_g, w_up, w_ffn_dw, b_ffn_dw, w_down, loss_target, m_mix_norm_g, m_w_in, m_b_conv_in, m_w_conv_dw, m_b_conv_dw, m_conv_ln_g, m_conv_ln_b, m_q_lat_norm_g, m_w_uq, m_kv_lat_norm_g, m_w_ukv, m_q_norm_g, m_k_norm_g, m_w_out, m_mem_norm_x_g, m_mem_norm_m_g, m_w_mem_q, m_w_mem_kv, m_mem_q_norm_g, m_mem_k_norm_g, m_w_mem_o, m_ffn_norm_g, m_w_up, m_w_ffn_dw, m_b_ffn_dw, m_w_down, v_mix_norm_g, v_w_in, v_b_conv_in, v_w_conv_dw, v_b_conv_dw, v_conv_ln_g, v_conv_ln_b, v_q_lat_norm_g, v_w_uq, v_kv_lat_norm_g, v_w_ukv, v_q_norm_g, v_k_norm_g, v_w_out, v_mem_norm_x_g, v_mem_norm_m_g, v_w_mem_q, v_w_mem_kv, v_mem_q_norm_g, v_mem_k_norm_g, v_w_mem_o, v_ffn_norm_g, v_w_up, v_w_ffn_dw, v_b_ffn_dw, v_w_down):
    a = dict(locals())
    seq = x.shape[1]
    xs = x.reshape(seq, D_MODEL)
    mems = mem.reshape(-1, D_MODEL)
    target = loss_target.reshape(seq, D_MODEL)

    tiny = [n for n, _, _ in TINY]
    shard = lambda n: a[n][0] if n in tiny else a[n][0].astype(BF16)
    pos = jnp.stack([2 * lax.axis_index("x") + lax.axis_index("y"), lax.axis_index("c")]).astype(jnp.int32)
    ag_first = ["w_in", "w_uq", "w_ukv", "w_conv_dw"]
    ag_later = [n for n in [b for b, _ in BIG] + tiny if n not in ag_first]
    (h1,), first = _rowwise(_f_rms, [xs], [_row(mix_norm_g)], [(D_MODEL, BF16)], [], name="rms_mix",
                            ex=_plan_all_gather([shard(n) for n in ag_first]))
    wg = dict(zip(ag_first, first))
    wi = _side_by_side(wg["w_in"])
    s3 = 2 * CONV_CH + MLA_Q_RANK + MLA_KV_RANK
    w_in_p = jnp.concatenate([
        wi[:, :s3], jnp.zeros((D_MODEL, MLA_NOPE), BF16), wi[:, s3:],
        jnp.zeros((D_MODEL, LANES - MLA_QK), BF16)], axis=1)
    w_uq_p = _side_by_side(_pad_last(wg["w_uq"], LANES))
    w_uk_p = _side_by_side(_pad_last(wg["w_ukv"][:, :, :MLA_NOPE], LANES))
    w_uv_p = _side_by_side(_pad_last(wg["w_ukv"][:, :, MLA_NOPE:], LANES))
    w_cdw = _side_by_side(wg["w_conv_dw"])

    g_mix, g_qlat, g_kvlat = _row(mix_norm_g), _row(q_lat_norm_g), _row(kv_lat_norm_g)
    b_in = _row(b_conv_in)
    b_in_a, b_in_g = b_in[:, :CONV_CH], b_in[:, CONV_CH:]
    b_cdw, ln_g, ln_b = _row(b_conv_dw), _row(conv_ln_g), _row(conv_ln_b)
    g_q, g_k = _row(q_norm_g, LANES), _row(k_norm_g, LANES)
    g_memx, g_memm = _row(mem_norm_x_g), _row(mem_norm_m_g)
    g_mq, g_mk, g_ffn = _row(mem_q_norm_g), _row(mem_k_norm_g), _row(ffn_norm_g)
    b_f = _pad_last(b_ffn_dw.reshape(N_DEV, FF_SHARD), FF_PAD)
    b_f_g, b_f_v = b_f[:4].reshape(1, D_FF_PAD), b_f[4:].reshape(1, D_FF_PAD)

    freq = ROPE_THETA ** (-jnp.arange(0, MLA_ROPE, 2, dtype=F32) / MLA_ROPE)
    inv_freq = jnp.concatenate([jnp.zeros((MLA_NOPE,), F32), freq, freq,
                                jnp.zeros((LANES - MLA_QK,), F32)]).reshape(1, LANES)
    cos, sin_a, sin_b = _rowwise(_f_rope_tab, [positions.reshape(seq, 1)], [inv_freq],
                                 [(LANES, F32)] * 3, [], name="rope_tables")

    z = _mm(h1, w_in_p, name="mm_in")
    z_rows = [(z, CONV_CH, 0, 0), (z, CONV_CH, 1, 0), (z, MLA_Q_RANK, 4, 0), (z, MLA_KV_RANK, 10, 0)]
    z_kr = (z, LANES, 11, 0)
    u0, cqn, ckvn = _rowwise(
        _mix_pre, z_rows, [b_in_a, b_in_g, g_qlat, g_kvlat],
        [(CONV_CH, F32), (MLA_Q_RANK, BF16), (MLA_KV_RANK, BF16)], [], name="mix_pre")
    c1 = _conv_fwd(u0, w_cdw, "conv31_fwd")
    (u,) = _rowwise(lambda c, b, g, bb: (_ln_silu(c, b, g, bb),), [c1], [b_cdw, ln_g, ln_b],
                    [(CONV_CH, BF16)], [], name="ln_silu")
    q0 = _mm(cqn, w_uq_p, name="mm_uq")
    kn0 = _mm(ckvn, w_uk_p, name="mm_uk")
    v0 = _mm(ckvn, w_uv_p, out_dtype=BF16, name="mm_uv")
    qk_rows = [q0, kn0, z_kr, cos, sin_a, sin_b]
    qh, kh = _rowwise(_f_qk_prep, qk_rows, [g_q, g_k],
                      [(MLA_HEADS * LANES, BF16)] * 2, [], name="qk_prep")
    (attn, lse_row), later = _flash_fwd(
        qh, kh, v0, "flash_fwd", _plan_all_gather([shard(n) for n in ag_later]))
    wg.update(zip(ag_later, later))
    w_out = wg["w_out"].reshape(D_MODEL, D_MODEL)
    w_out_u = w_out[:CONV_CH]
    w_out_a = _pad_heads(w_out[CONV_CH:], MLA_V, 0)
    w_mq, w_mo = wg["w_mem_q"].reshape(D_MODEL, D_MODEL), wg["w_mem_o"].reshape(D_MODEL, D_MODEL)
    w_mkv = _side_by_side(wg["w_mem_kv"])
    w_up_p = _pad_last(wg["w_up"], FF_PAD)
    w_dn = jnp.pad(wg["w_down"].reshape(4, FF_SHARD, D_MODEL),
                   ((0, 0), (0, FF_PAD - FF_SHARD), (0, 0))).reshape(D_FF_PAD, D_MODEL)
    w_fdw = _pad_last(wg["w_ffn_dw"], FF_PAD)
    w_fdw_g = w_fdw[:4].transpose(1, 0, 2).reshape(FFN_CONV_WIDTH, D_FF_PAD)
    w_fdw_v = w_fdw[4:].transpose(1, 0, 2).reshape(FFN_CONV_WIDTH, D_FF_PAD)
    x1 = _mm(u, w_out_u, add=xs, name="mm_out_u")
    x1 = _mm(attn, w_out_a, add=x1, name="mm_out_a")

    (hq,) = _rowwise(_f_rms, [x1], [g_memx], [(D_MODEL, BF16)], [], name="rms_memx")
    (hm,) = _rowwise(_f_rms, [mems], [g_memm], [(D_MODEL, BF16)], [], name="rms_memm", tm=mems.shape[0])
    qm0 = _mm(hq, w_mq, name="mm_memq")
    kvm0 = _mm(hm, w_mkv, name="mm_memkv", tm=mems.shape[0])
    (km,) = _rowwise(_f_mem_k, [(kvm0, D_MODEL, 0, 0)], [g_mk], [(D_MODEL, BF16)], [],
                     name="mem_k", tm=mems.shape[0])
    vm = kvm0[:, D_MODEL:]
    (om,) = _rowwise(_f_mem_attn, [qm0], [km, vm, g_mq], [(D_MODEL, BF16)], [], name="mem_attn")
    x2 = _mm(om, w_mo, add=x1, name="mm_memo")

    (h3,) = _rowwise(_f_rms, [x2], [g_ffn], [(D_MODEL, BF16)], [], name="rms_ffn")
    up_g, up_v, act, dy, sq = _ffn_fwd(h3, w_up_p, w_fdw_g, w_fdw_v, b_f_g, b_f_v, w_dn, x2, target, "ffn_fwd")
    loss = lax.psum(0.5 * jnp.sum(sq) / D_MODEL, ("x", "y", "c"))

    gw, gs, gt = {}, {}, {}
    gw_dn = _mm_tn(act, dy, name="tn_down").reshape(4, FF_PAD, D_MODEL)
    gw["w_down"] = gw_dn[:, :FF_SHARD].reshape(N_DEV, FF_SHARD // 2, D_MODEL)
    dup_g, dup_v, dwf_g, dwf_v, db_g, db_v = _ffn_mid_bwd(
        up_g, up_v, dy, w_dn, w_fdw_g, w_fdw_v, b_f_g, b_f_v, "ffn_mid_bwd")
    db_f = jnp.concatenate([db_g.reshape(4, FF_PAD), db_v.reshape(4, FF_PAD)], axis=0)
    gs["b_ffn_dw"] = db_f[:, :FF_SHARD].reshape(1, 2 * D_FF)
    dwf = jnp.concatenate([dwf_g.reshape(FFN_CONV_WIDTH, 4, FF_PAD), dwf_v.reshape(FFN_CONV_WIDTH, 4, FF_PAD)], axis=1)
    gt["w_ffn_dw"] = dwf[:, :, :FF_SHARD].transpose(1, 0, 2)
    gw_up = jnp.concatenate([_mm_tn(h3, dup_g, shard_cols=FF_PAD, name="tn_up_g"),
                             _mm_tn(h3, dup_v, shard_cols=FF_PAD, name="tn_up_v")], axis=0)
    gw["w_up"] = gw_up[:, :, :FF_SHARD]
    w_up_flat = _side_by_side(w_up_p)
    dh3 = _mm(dup_g, w_up_flat, trans_b=True, b_kblock=0, name="mm_up_g_t")
    dx2, gs["ffn_norm_g"] = _mm(dup_v, w_up_flat, trans_b=True, b_kblock=1, add=dh3, tm=TM,
                                rms_bwd=(x2, dy, g_ffn), name="mm_up_v_t")

    gw["w_mem_o"] = _mm_tn(om, dx2, name="tn_memo").reshape(N_DEV, -1, D_MODEL)
    dom = _mm(dx2, w_mo, trans_b=True, out_dtype=BF16, name="mm_memo_t")
    n_mem = mems.shape[0]
    dqm0, dkm, dvm, gs["mem_q_norm_g"] = _rowwise(
        _b_mem_attn, [dom, qm0], [km, vm, g_mq], [(D_MODEL, BF16)],
        [(n_mem, D_MODEL), (n_mem, D_MODEL), (1, MEM_HEAD_DIM)], name="mem_attn_bwd")
    gw["w_mem_q"] = _mm_tn(hq, dqm0, name="tn_memq").reshape(N_DEV, -1, D_MODEL)
    dx1, gs["mem_norm_x_g"] = _mm(dqm0, w_mq, trans_b=True, rms_bwd=(x1, dx2, g_memx), name="mm_memq_t")
    dkk, gs["mem_k_norm_g"] = _rowwise(_b_mem_k, [(kvm0, D_MODEL, 0, 0), dkm], [g_mk],
                                       [(D_MODEL, F32)], [(1, MEM_HEAD_DIM)], name="mem_k_bwd", tm=n_mem)
    dkvm0 = jnp.concatenate([dkk, dvm], axis=1)
    gw["w_mem_kv"] = _col_shards(_mm_tn(hm, dkvm0, name="tn_memkv", ts=n_mem))
    dhm = _mm(dkvm0, w_mkv, trans_b=True, name="mm_memkv_t", tm=n_mem)
    _, gs["mem_norm_m_g"] = _rowwise(_b_rms_nores, [mems, dhm], [g_memm], [(D_MODEL, F32)],
                                     [(1, D_MODEL)], name="rms_memm_bwd", tm=n_mem)

    gw_out_u = _mm_tn(u, dx1, name="tn_out_u")
    gw_out_a = _mm_tn(attn, dx1, name="tn_out_a")
    gw["w_out"] = jnp.concatenate([gw_out_u, _unpad_heads(gw_out_a, MLA_V, 0)], axis=0).reshape(N_DEV, -1, D_MODEL)
    du = _mm(dx1, w_out_u, trans_b=True, name="mm_out_u_t")
    dattn = _mm(dx1, w_out_a, trans_b=True, name="mm_out_a_t")
    dc1, gs["b_conv_dw"], gs["conv_ln_g"], gs["conv_ln_b"] = _rowwise(
        _b_ln_silu, [c1, du], [b_cdw, ln_g, ln_b], [(CONV_CH, F32)], [(1, CONV_CH)] * 3, name="ln_silu_bwd")
    rs_first = ["w_up", "w_down", "w_mem_o", "w_mem_q", "w_mem_kv", "w_out"]
    grads = [gw[n] for n in rs_first]
    (du0, g_cdw), gots = _conv_bwd(dc1, u0, w_cdw, "conv31_bwd", _plan_swap_sibling(grads))
    gt["w_conv_dw"] = _col_shards(g_cdw)
    sums_first = _rs_add(grads, gots, pos, "rs_add_first")
    delta_row = _attn_delta(dattn, attn, "attn_delta")
    (dqt, dkh, dv0), recvs_first = _flash_bwd(
        qh, kh, v0, dattn, lse_row, delta_row, "flash_bwd", _plan_swap_chips(sums_first))
    dqt_row = (dqt, pl.BlockSpec((MLA_HEADS, 1, LANES, TQ), lambda i: (0, i, 0, 0)))
    dq0, dkn0, dkr, dgq, dgk = _rowwise(
        _b_qk_prep, qk_rows + [dqt_row, dkh], [g_q, g_k],
        [(MLA_HEADS * LANES, BF16)] * 2 + [(LANES, F32)], [(1, LANES)] * 2, name="qk_prep_bwd")
    gs["q_norm_g"], gs["k_norm_g"] = dgq[:, :MLA_QK], dgk[:, :MLA_QK]
    gw["w_uq"] = _col_shards(_mm_tn(cqn, dq0, name="tn_uq"))[:, :, :MLA_QK]
    g_uk = _col_shards(_mm_tn(ckvn, dkn0, name="tn_uk"))[:, :, :MLA_NOPE]
    g_uv = _col_shards(_mm_tn(ckvn, dv0, name="tn_uv"))[:, :, :MLA_V]
    gw["w_ukv"] = jnp.concatenate([g_uk, g_uv], axis=2)
    dcqn = _mm(dq0, w_uq_p, trans_b=True, name="mm_uq_t")
    dckvn = _mm(dkn0, w_uk_p, trans_b=True, name="mm_uk_t")
    dckvn = _mm(dv0, w_uv_p, trans_b=True, add=dckvn, name="mm_uv_t")
    dz, dba, dbg, gs["q_lat_norm_g"], gs["kv_lat_norm_g"] = _rowwise(
        _b_mix_pre, z_rows + [du0, dcqn, dckvn, dkr], [b_in_a, b_in_g, g_qlat, g_kvlat],
        [(IN_PAD, BF16)], [(1, CONV_CH)] * 2 + [(1, MLA_Q_RANK), (1, MLA_KV_RANK)], name="mix_pre_bwd")
    gs["b_conv_in"] = jnp.concatenate([dba, dbg], axis=1)
    gw_in = _mm_tn(h1, dz, name="tn_in")
    gw["w_in"] = _col_shards(jnp.concatenate([gw_in[:, :s3], gw_in[:, s3 + MLA_NOPE:s3 + MLA_QK]], axis=1))
    rs_last = [n for n, _ in BIG if n not in rs_first]
    grads = [gw[n] for n in rs_last]
    gots = _run_exchange(_plan_swap_sibling(grads), "rs_sibling_last")
    sums_last = _rs_add(grads, gots, pos, "rs_add_last")
    (dx, gs["mix_norm_g"]), recvs_last = _mm(dz, w_in_p, trans_b=True, rms_bwd=(xs, dx1, g_mix),
                                             name="mm_in_t", ex=_plan_swap_chips(sums_last))
    big = rs_first + rs_last
    flat = _adamw_big(list(sums_first) + list(sums_last), list(recvs_first) + list(recvs_last),
                      [a[n] for n in big], [a["m_" + n] for n in big], [a["v_" + n] for n in big],
                      pos, "adamw_big")
    res = [{n: flat[4 * i + k] for i, n in enumerate(big)} for k in range(4)]

    part = jnp.concatenate(
        [_pack_small(gs)] + [_tiny_rows(gt[n], rows).reshape(N_DEV * rows, LANES) for n, _, rows in TINY], axis=0)
    (parts,) = _all_gather([part], "ag_small_grads")
    small_in = [_pack_small({n: a[p + n] for n, _ in SMALL}) for p in ("", "m_", "v_")]
    tiny_in = [[_tiny_rows(a[p + n][0], rows) for p in ("", "m_", "v_")] for n, _, rows in TINY]
    flat = _adamw_small(parts, small_in, tiny_in, "adamw_small")
    for k in range(4):
        res[k].update(_unpack_small(flat[k]))
        for i, (n, shape, _) in enumerate(TINY):
            res[k][n] = flat[4 * (i + 1) + k].reshape(-1)[:math.prod(shape)].reshape((1,) + shape)

    return (loss, dx.reshape(1, seq, D_MODEL), *[res[k][n] for k in range(4) for n in WEIGHTS])
```

```python
import functools
import math

import jax
import jax.numpy as jnp
from jax import lax
from jax.experimental import pallas as pl
from jax.experimental.pallas import tpu as pltpu

F32 = jnp.float32
BF16 = jnp.bfloat16
EPS = 1e-6
LANES = 128
N_DEV = 8
D_MODEL = 1024
CONV_CH = 512
MLA_HEADS = 8
MLA_NOPE = 64
MLA_ROPE = 32
MLA_V = 64
MLA_QK = MLA_NOPE + MLA_ROPE
MLA_Q_RANK = 256
MLA_KV_RANK = 128
ROPE_THETA = 10000.0
IN_PAD = 2 * CONV_CH + MLA_Q_RANK + MLA_KV_RANK + LANES
MEM_HEAD_DIM = 256
D_FF = 2816
FFN_CONV_WIDTH = 3
CHUNK = 64
ATT_SCALE = 1.0 / math.sqrt(MLA_QK)
LN2 = math.log(2.0)
Q_SCALE = ATT_SCALE / LN2
MEM_SCALE = 1.0 / math.sqrt(MEM_HEAD_DIM)
ADAM_LR, ADAM_B1, ADAM_B2, ADAM_EPS, ADAM_WD, ADAM_STEP = 0.001, 0.9, 0.999, 1e-08, 0.01, 10

TM = 512
MM_ROWS = 1024
TN_ROWS = 2048
TQ = 512
HEADS_PER_STEP = 2
FWD_HEADS = 4
CONV_ROWS = 256
NEG = -1e30
VMEM_LIMIT = 56 * 1024 * 1024

MESH = pl.DeviceIdType.MESH
ANY = pl.BlockSpec(memory_space=pl.ANY)
NT_DIMS = (((1,), (1,)), ((), ()))

BIG = [
    ("w_in", (1024, 180)), ("w_uq", (256, 96)), ("w_ukv", (128, 128)), ("w_out", (128, 1024)),
    ("w_mem_q", (128, 1024)), ("w_mem_kv", (1024, 256)), ("w_mem_o", (128, 1024)),
    ("w_up", (1024, 704)), ("w_down", (352, 1024)),
]
TINY = [("w_conv_dw", (31, 64), 16), ("w_ffn_dw", (3, 704), 24)]
ROW_STEPS = 4
FF_SHARD = D_FF // 4
FF_PAD = 768
D_FF_PAD = 4 * FF_PAD
SMALL = [
    ("mix_norm_g", 1024), ("b_conv_in", 1024), ("b_conv_dw", 512), ("conv_ln_g", 512),
    ("conv_ln_b", 512), ("q_lat_norm_g", 256), ("kv_lat_norm_g", 128), ("q_norm_g", 96),
    ("k_norm_g", 96), ("mem_norm_x_g", 1024), ("mem_norm_m_g", 1024), ("mem_q_norm_g", 256),
    ("mem_k_norm_g", 256), ("ffn_norm_g", 1024), ("b_ffn_dw", 5632),
]
WEIGHTS = [
    "mix_norm_g", "w_in", "b_conv_in", "w_conv_dw", "b_conv_dw", "conv_ln_g", "conv_ln_b",
    "q_lat_norm_g", "w_uq", "kv_lat_norm_g", "w_ukv", "q_norm_g", "k_norm_g", "w_out",
    "mem_norm_x_g", "mem_norm_m_g", "w_mem_q", "w_mem_kv", "mem_q_norm_g", "mem_k_norm_g",
    "w_mem_o", "ffn_norm_g", "w_up", "w_ffn_dw", "b_ffn_dw", "w_down",
]


SMALL_PAD = [(-(-n // LANES)) * LANES for _, n in SMALL]
SMALL_ROWS = -(-sum(SMALL_PAD) // (8 * LANES)) * 8
TINY_BASE = [SMALL_ROWS + N_DEV * sum(r for _, _, r in TINY[:i]) for i in range(len(TINY))]
PART_ROWS = SMALL_ROWS + N_DEV * sum(r for _, _, r in TINY)


def _call(body, **kw):
    return pl.pallas_call(body, **kw)


def _params(*sem):
    return pltpu.CompilerParams(dimension_semantics=sem, vmem_limit_bytes=VMEM_LIMIT)


class _Exchange:
    def __init__(self, inputs, out_shape, scratch, start, finish):
        self.inputs, self.out_shape, self.scratch = list(inputs), list(out_shape), list(scratch)
        self.start, self.finish = start, finish


def _run_exchange(ex, name):
    n_in, n_out = len(ex.inputs), len(ex.out_shape)

    def body(*refs):
        parts = refs[:n_in], refs[n_in:n_in + n_out], refs[n_in + n_out:]
        ex.start(*parts)
        ex.finish(*parts)

    return _call(body, name=name, out_shape=ex.out_shape, in_specs=[ANY] * n_in,
                 out_specs=[ANY] * n_out, scratch_shapes=ex.scratch)(*ex.inputs)


def _plan_all_gather(xs):
    n = len(xs)

    def copies(x_refs, out_refs, sems):
        send_sems, recv_sems, local_sems = sems
        x, y, c = lax.axis_index("x"), lax.axis_index("y"), lax.axis_index("c")
        me, sibling = (x, y, c), (x, y, 1 - c)
        chips = [(1 - x, y), (x, 1 - y), (1 - x, 1 - y)]

        def slot(o, px, py, pc):
            return out_refs[o].at[4 * px + 2 * py + pc]

        def copy(o, k, block, to, src=None):
            return pltpu.make_async_remote_copy(
                src_ref=slot(o, *block) if src is None else src, dst_ref=slot(o, *block),
                send_sem=send_sems.at[o, k], recv_sem=recv_sems.at[o, k],
                device_id=to, device_id_type=MESH)

        mine = [pltpu.make_async_copy(x_refs[o], slot(o, *me), local_sems.at[o]) for o in range(n)]
        first = [copy(o, 0, me, sibling, src=x_refs[o]) for o in range(n)]
        first += [copy(o, 1 + j, me, (*chip, c), src=x_refs[o])
                  for j, chip in enumerate(chips) for o in range(n)]
        return me, sibling, chips, copy, mine, first

    def start(x_refs, out_refs, sems):
        _, _, _, _, mine, first = copies(x_refs, out_refs, sems)
        for cp in mine + first:
            cp.start()

    def finish(x_refs, out_refs, sems):
        me, sibling, chips, copy, mine, first = copies(x_refs, out_refs, sems)
        c = me[2]
        passed = []
        for j, chip in enumerate(chips):
            for o in range(n):
                copy(o, 1 + j, (*chip, c), me).wait_recv()
                passed.append(copy(o, 4 + j, (*chip, c), sibling))
                passed[-1].start()
        for o in range(n):
            copy(o, 0, sibling, me).wait_recv()
        for j, chip in enumerate(chips):
            for o in range(n):
                copy(o, 4 + j, (*chip, 1 - c), me).wait_recv()
        for cp in first + passed:
            cp.wait_send()
        for cp in mine:
            cp.wait()

    return _Exchange(
        xs, [jax.ShapeDtypeStruct((N_DEV,) + v.shape, v.dtype) for v in xs],
        [pltpu.SemaphoreType.DMA((n, 7)), pltpu.SemaphoreType.DMA((n, 7)), pltpu.SemaphoreType.DMA((n,))],
        start, finish)


def _all_gather(xs, name):
    return _run_exchange(_plan_all_gather(xs), name)


def _plan_swap_sibling(grads):
    n = len(grads)

    def copies(g_refs, got_refs, sems):
        send_sems, recv_sems = sems
        x, y, c = lax.axis_index("x"), lax.axis_index("y"), lax.axis_index("c")
        return [
            pltpu.make_async_remote_copy(
                src_ref=g_refs[o].at[2 * chip + 1 - c], dst_ref=got_refs[o].at[chip],
                send_sem=send_sems.at[o, chip], recv_sem=recv_sems.at[o, chip],
                device_id=(x, y, 1 - c), device_id_type=MESH)
            for o in range(n) for chip in range(4)]

    def start(g_refs, got_refs, sems):
        for cp in copies(g_refs, got_refs, sems):
            cp.start()

    def finish(g_refs, got_refs, sems):
        for cp in copies(g_refs, got_refs, sems):
            cp.wait()

    return _Exchange(
        grads, [jax.ShapeDtypeStruct((4,) + g.shape[1:], g.dtype) for g in grads],
        [pltpu.SemaphoreType.DMA((n, 4)), pltpu.SemaphoreType.DMA((n, 4))], start, finish)


def _plan_swap_chips(sums):
    n = len(sums)

    def copies(a_refs, r_refs, sems):
        send_sems, recv_sems = sems
        x, y, c = lax.axis_index("x"), lax.axis_index("y"), lax.axis_index("c")
        peers = [(x, 1 - y), (1 - x, y), (1 - x, 1 - y)]
        return [
            pltpu.make_async_remote_copy(
                src_ref=a_refs[o].at[2 * px + py], dst_ref=r_refs[o].at[k],
                send_sem=send_sems.at[o, k], recv_sem=recv_sems.at[o, k],
                device_id=(px, py, c), device_id_type=MESH)
            for k, (px, py) in enumerate(peers) for o in range(n)]

    def start(a_refs, r_refs, sems):
        for cp in copies(a_refs, r_refs, sems):
            cp.start()

    def finish(a_refs, r_refs, sems):
        for cp in copies(a_refs, r_refs, sems):
            cp.wait()

    return _Exchange(
        sums, [jax.ShapeDtypeStruct((3,) + a.shape[1:], a.dtype) for a in sums],
        [pltpu.SemaphoreType.DMA((n, 3)), pltpu.SemaphoreType.DMA((n, 3))], start, finish)


def _shard_block(shape):
    return (None, shape[-2] // ROW_STEPS, shape[-1])


def _rs_add(grads, gots, pos, name):
    n = len(grads)

    def body(pos_ref, *refs):
        for g_ref, t_ref, o_ref in zip(refs[:n], refs[n:2 * n], refs[2 * n:]):
            o_ref[...] = g_ref[...] + t_ref[...]

    in_specs = [pl.BlockSpec(_shard_block(g.shape), lambda a, t, pos: (2 * a + pos[1], t, 0)) for g in grads]
    in_specs += [pl.BlockSpec(_shard_block(g.shape), lambda a, t, pos: (a, t, 0)) for g in gots]
    return _call(
        body, name=name,
        grid_spec=pltpu.PrefetchScalarGridSpec(
            num_scalar_prefetch=1, grid=(4, ROW_STEPS), in_specs=in_specs,
            out_specs=[pl.BlockSpec(_shard_block(g.shape), lambda a, t, pos: (a, t, 0)) for g in gots]),
        out_shape=[jax.ShapeDtypeStruct(g.shape, g.dtype) for g in gots],
        compiler_params=_params("arbitrary", "arbitrary"),
    )(pos, *grads, *gots)


def _adamw_big(sums, recvs, ws, ms, vs, pos, name):
    n = len(sums)

    def body(pos_ref, *refs):
        ins, outs = refs[:7 * n], refs[7 * n:]
        for o in range(n):
            own, r1, r2, r3, w, m, v = [r[...] for r in ins[7 * o:7 * o + 7]]
            g = ((own + r1) + r2) + r3
            for ref, val in zip(outs[4 * o:4 * o + 4], (g,) + _adamw(w, g, m, v)):
                ref[...] = val

    in_specs, args, out_specs, out_shape = [], [], [], []
    for s_, r_, w_, m_, v_ in zip(sums, recvs, ws, ms, vs):
        blk = _shard_block(w_.shape)
        in_specs.append(pl.BlockSpec(blk, lambda t, pos: (pos[0], t, 0)))
        in_specs += [pl.BlockSpec(blk, lambda t, pos, k=k: (k, t, 0)) for k in range(3)]
        in_specs += [pl.BlockSpec(blk, lambda t, pos: (0, t, 0))] * 3
        args += [s_, r_, r_, r_, w_, m_, v_]
        out_specs += [pl.BlockSpec(blk, lambda t, pos: (0, t, 0))] * 4
        out_shape += [jax.ShapeDtypeStruct(w_.shape, F32)] * 4
    return _call(
        body, name=name,
        grid_spec=pltpu.PrefetchScalarGridSpec(
            num_scalar_prefetch=1, grid=(ROW_STEPS,), in_specs=in_specs, out_specs=out_specs),
        out_shape=out_shape, compiler_params=_params("arbitrary"),
    )(pos, *args)


def _tile(n, prefs):
    for t in prefs:
        if n % t == 0:
            return t
    return n


def _mm(a, b, *, name, trans_b=False, add=None, out_dtype=F32, tm=MM_ROWS, b_kblock=0, ex=None,
        rms_bwd=None):
    m, k = a.shape
    n = b.shape[0] if trans_b else b.shape[1]
    tn = _tile(n, (1536, 1408, 1024, 768, 512, 256, 128))
    tk = _tile(k, (3072, 2048, 1536, 1408, 1024, 768, 512, 256, 128))
    nk = k // tk
    has_add = add is not None
    n_post = 3 if rms_bwd else 0
    assert m % tm == 0 and (not rms_bwd or tn == n), (m, tm, n, tn)
    grid = (m // tm, n // tn, nk)

    def body(*refs):
        ins, outs, ex_refs = _split_refs(refs, 2 + has_add + n_post, 1 + bool(rms_bwd), ex, int(nk > 1))
        _hosted(ex, ex_refs, grid, True)
        a_ref, b_ref = ins[0], ins[1]
        add_ref = ins[2] if has_add else None
        o_ref = outs[0]
        av = a_ref[...].astype(BF16)
        bv = b_ref[...].astype(BF16)
        if trans_b:
            part = lax.dot_general(av, bv, NT_DIMS, preferred_element_type=F32)
        else:
            part = jnp.dot(av, bv, preferred_element_type=F32)

        def finish(acc):
            if has_add:
                acc = acc + add_ref[...].astype(F32)
            if rms_bwd:
                x_ref, dres_ref, g_ref = ins[2 + has_add:]
                acc, dg = _b_rms(x_ref[...], acc, dres_ref[...], g_ref[...])
                first_rows = pl.program_id(0) == 0

                @pl.when(first_rows)
                def _():
                    outs[1][...] = dg

                @pl.when(jnp.logical_not(first_rows))
                def _():
                    outs[1][...] += dg
            o_ref[...] = acc.astype(o_ref.dtype)

        if nk == 1:
            finish(part)
        else:
            acc_ref = outs[-1]
            kk = pl.program_id(2)

            @pl.when(kk == 0)
            def _():
                acc_ref[...] = part

            @pl.when(kk > 0)
            def _():
                acc_ref[...] += part

            @pl.when(kk == nk - 1)
            def _():
                finish(acc_ref[...])
        _hosted(ex, ex_refs, grid, False)

    in_specs = [pl.BlockSpec((tm, tk), lambda i, j, kk: (i, kk))]
    if trans_b:
        in_specs.append(pl.BlockSpec((tn, tk), lambda i, j, kk: (j, kk + b_kblock * nk)))
    else:
        in_specs.append(pl.BlockSpec((tk, tn), lambda i, j, kk: (kk, j)))
    args = [a, b]
    tile = pl.BlockSpec((tm, tn), lambda i, j, kk: (i, j))
    out_specs, out_shape = [tile], [jax.ShapeDtypeStruct((m, n), out_dtype)]
    if has_add:
        in_specs.append(tile)
        args.append(add)
    if rms_bwd:
        in_specs += [tile, tile, pl.BlockSpec((1, n), lambda i, j, kk: (0, 0))]
        args += list(rms_bwd)
        out_specs.append(pl.BlockSpec((1, n), lambda i, j, kk: (0, 0)))
        out_shape.append(jax.ShapeDtypeStruct((1, n), F32))
    outs, hosted = _host_call(
        body, ex, name, grid, in_specs, out_specs, out_shape, args,
        scratch=[pltpu.VMEM((tm, tn), F32)] if nk > 1 else [])
    res = tuple(outs) if rms_bwd else outs[0]
    return (res, hosted) if ex else res


def _mm_tn(a, b, *, name, ts=TN_ROWS, shard_cols=None):
    s, m = a.shape
    n = b.shape[1]
    assert s % ts == 0, (s, ts)
    tm = _tile(m, (1408, 1024, 768, 512, 256, 128))
    tn = shard_cols or _tile(n, (1536, 1408, 1024, 768, 512, 256, 128))
    if shard_cols:
        out_spec = pl.BlockSpec((None, tm, tn), lambda i, j, kk: (j, i, 0))
        out_shape = jax.ShapeDtypeStruct((n // tn, m, tn), F32)
    else:
        out_spec = pl.BlockSpec((tm, tn), lambda i, j, kk: (i, j))
        out_shape = jax.ShapeDtypeStruct((m, n), F32)

    def body(a_ref, b_ref, o_ref):
        kk = pl.program_id(2)
        part = jnp.dot(a_ref[...].astype(BF16).T, b_ref[...].astype(BF16),
                       preferred_element_type=F32)

        @pl.when(kk == 0)
        def _():
            o_ref[...] = part

        @pl.when(kk > 0)
        def _():
            o_ref[...] += part

    return _call(
        body, name=name, grid=(m // tm, n // tn, s // ts),
        in_specs=[pl.BlockSpec((ts, tm), lambda i, j, kk: (kk, i)),
                  pl.BlockSpec((ts, tn), lambda i, j, kk: (kk, j))],
        out_specs=out_spec, out_shape=out_shape,
        compiler_params=_params("parallel", "parallel", "arbitrary"),
    )(a, b)


def _rowwise(fn, rows, consts, row_outs, acc_outs, *, name, tm=TM, ex=None):
    rows = [r if isinstance(r, tuple) else (r, r.shape[1], 0) for r in rows]
    s = rows[0][0].shape[0]
    nr, nc, no, na = len(rows), len(consts), len(row_outs), len(acc_outs)
    grid = (s // tm,)

    def body(*refs):
        ins, outs_, ex_refs = _split_refs(refs, nr + nc, no + na, ex)
        _hosted(ex, ex_refs, grid, True)
        r_in, c_in = ins[:nr], ins[nr:]
        o_refs, a_refs = outs_[:no], outs_[no:]
        outs = fn(*[r[...] for r in r_in], *[c[...] for c in c_in])
        for r, v in zip(o_refs, outs[:no]):
            r[...] = v.astype(r.dtype)
        if na:
            i = pl.program_id(0)

            @pl.when(i == 0)
            def _():
                for r, v in zip(a_refs, outs[no:]):
                    r[...] = v.astype(F32)

            @pl.when(i > 0)
            def _():
                for r, v in zip(a_refs, outs[no:]):
                    r[...] += v.astype(F32)
        _hosted(ex, ex_refs, grid, False)

    in_specs = [r[1] if isinstance(r[1], pl.BlockSpec) else
                pl.BlockSpec((tm, r[1]), lambda i, cb=r[2]: (i, cb)) for r in rows]
    in_specs += [pl.BlockSpec(c.shape, lambda i: (0, 0)) for c in consts]
    out_specs = [pl.BlockSpec((tm, w), lambda i: (i, 0)) for w, _ in row_outs]
    out_specs += [pl.BlockSpec(sh, lambda i: (0, 0)) for sh in acc_outs]
    out_shape = [jax.ShapeDtypeStruct((s, w), dt) for w, dt in row_outs]
    out_shape += [jax.ShapeDtypeStruct(sh, F32) for sh in acc_outs]
    res, hosted = _host_call(body, ex, name, grid, in_specs, out_specs, out_shape,
                             [r[0] for r in rows] + list(consts))
    return (res, hosted) if ex else res


def _rms(x, g, n=None):
    ms = jnp.sum(x * x, axis=-1, keepdims=True) / float(n or x.shape[-1])
    return x * lax.rsqrt(ms + EPS) * g


def _layer_norm(x, g, b):
    mu = jnp.sum(x, axis=-1, keepdims=True) / float(x.shape[-1])
    xc = x - mu
    var = jnp.sum(xc * xc, axis=-1, keepdims=True) / float(x.shape[-1])
    return xc * lax.rsqrt(var + EPS) * g + b


def _silu(x):
    return x * jax.nn.sigmoid(x)


@jax.custom_vjp
def _rope(y, cos, sin_a, sin_b):
    return y * cos + pltpu.roll(y, 112, 1) * sin_a + pltpu.roll(y, 16, 1) * sin_b


def _rope_fwd(y, cos, sin_a, sin_b):
    return _rope(y, cos, sin_a, sin_b), (cos, sin_a, sin_b)


def _rope_bwd(res, ct):
    cos, sin_a, sin_b = res
    dy = ct * cos + pltpu.roll(ct * sin_a, 16, 1) + pltpu.roll(ct * sin_b, 112, 1)
    return dy, jnp.zeros_like(cos), jnp.zeros_like(sin_a), jnp.zeros_like(sin_b)


_rope.defvjp(_rope_fwd, _rope_bwd)


def _qk_head(xh, g, cos, sin_a, sin_b):
    return _rope(_rms(xh, g, MLA_QK), cos, sin_a, sin_b)


def _heads(x, width):
    return [x[:, h * width:(h + 1) * width] for h in range(x.shape[1] // width)]


def _f_rms(x, g):
    return (_rms(x, g),)


def _f_rope_tab(pos, inv_freq):
    ang = pos.astype(F32) * inv_freq
    lane = lax.broadcasted_iota(jnp.int32, ang.shape, 1)
    sn = jnp.sin(ang)
    first = (lane >= MLA_NOPE) & (lane < MLA_NOPE + MLA_ROPE // 2)
    second = (lane >= MLA_NOPE + MLA_ROPE // 2) & (lane < MLA_QK)
    return jnp.cos(ang), jnp.where(first, -sn, 0.0), jnp.where(second, sn, 0.0)


def _mix_pre(za, zg, zcq, zckv, ba, bg, gq, gkv):
    u0 = (za + ba) * jax.nn.sigmoid(zg + bg)
    return u0, _rms(zcq, gq), _rms(zckv, gkv)


def _ln_silu(c1, bdw, lg, lb):
    return _silu(_layer_norm(c1 + bdw, lg, lb))


def _f_qk_prep(q0, kn, kr, cos, sa, sb, gq, gk):
    qs = [_qk_head(xh, gq, cos, sa, sb) * Q_SCALE for xh in _heads(q0, LANES)]
    ks = [_qk_head(xh + kr, gk, cos, sa, sb) for xh in _heads(kn, LANES)]
    return jnp.concatenate(qs, axis=1), jnp.concatenate(ks, axis=1)


def _act(cg, cv, bg, bv):
    return _silu(cg + bg) * (cv + bv)


def _f_mem_k(kk, g):
    return (jnp.concatenate([_rms(xh, g) for xh in _heads(kk, MEM_HEAD_DIM)], axis=1),)


def _mem_probs(qn, kmh):
    s = lax.dot_general(qn.astype(BF16), kmh, NT_DIMS, preferred_element_type=F32) * MEM_SCALE
    e = jnp.exp(s - jnp.max(s, axis=-1, keepdims=True))
    return e / jnp.sum(e, axis=-1, keepdims=True)


def _f_mem_attn(qm0, km, vm, g):
    outs = []
    for h, xh in enumerate(_heads(qm0, MEM_HEAD_DIM)):
        sl = slice(h * MEM_HEAD_DIM, (h + 1) * MEM_HEAD_DIM)
        p = _mem_probs(_rms(xh, g), km[:, sl])
        outs.append(jnp.dot(p.astype(BF16), vm[:, sl].astype(BF16), preferred_element_type=F32))
    return (jnp.concatenate(outs, axis=1),)


def _f_loss(y, t):
    e = y - t
    return e * (1.0 / D_MODEL), jnp.sum(e * e, axis=0, keepdims=True)


def _b_rms(x, dh, dres, g):
    _, vjp = jax.vjp(_rms, x, g)
    dx, dg = vjp(dh)
    return dx + dres, dg


def _b_rms_nores(x, dh, g):
    _, vjp = jax.vjp(_rms, x, g)
    dx, dg = vjp(dh)
    return dx, dg


def _b_mix_pre(za, zg, zcq, zckv, du0, dcqn, dckvn, dkr, ba, bg, gq, gkv):
    _, vjp = jax.vjp(_mix_pre, za, zg, zcq, zckv, ba, bg, gq, gkv)
    dza, dzg, dzcq, dzckv, dba, dbg, dgq, dgkv = vjp((du0, dcqn, dckvn))
    return jnp.concatenate([dza, dzg, dzcq, dzckv, dkr], axis=1), dba, dbg, dgq, dgkv


def _b_ln_silu(c1, du, bdw, lg, lb):
    _, vjp = jax.vjp(_ln_silu, c1, bdw, lg, lb)
    return vjp(du)


def _b_qk_prep(q0, kn, kr, cos, sa, sb, dq, dk, gq, gk):
    head = lambda xh, g: _qk_head(xh, g, cos, sa, sb)
    dk = dk * LN2
    dq0, dkn = [], []
    dkr = jnp.zeros_like(kr)
    dgq = jnp.zeros_like(gq)
    dgk = jnp.zeros_like(gk)
    for h, xh in enumerate(_heads(q0, LANES)):
        _, vjp = jax.vjp(head, xh, gq)
        dx, dg = vjp(dq[h, 0].T * ATT_SCALE)
        dq0.append(dx)
        dgq = dgq + dg
    for xh, ct in zip(_heads(kn, LANES), _heads(dk, LANES)):
        _, vjp = jax.vjp(head, xh + kr, gk)
        dx, dg = vjp(ct)
        dkn.append(dx)
        dkr = dkr + dx
        dgk = dgk + dg
    return jnp.concatenate(dq0, axis=1), jnp.concatenate(dkn, axis=1), dkr, dgq, dgk


def _b_mem_k(kk, dkm, g):
    dkk = []
    dg = jnp.zeros_like(g)
    for xh, ct in zip(_heads(kk, MEM_HEAD_DIM), _heads(dkm, MEM_HEAD_DIM)):
        _, vjp = jax.vjp(_rms, xh, g)
        dx, dgh = vjp(ct)
        dkk.append(dx)
        dg = dg + dgh
    return jnp.concatenate(dkk, axis=1), dg


def _b_mem_attn(dom, qm0, km, vm, g):
    dq0, dkm, dvm = [], [], []
    dg = jnp.zeros_like(g)
    for h, (xh, doh) in enumerate(zip(_heads(qm0, MEM_HEAD_DIM), _heads(dom, MEM_HEAD_DIM))):
        sl = slice(h * MEM_HEAD_DIM, (h + 1) * MEM_HEAD_DIM)
        kmh, vmh = km[:, sl], vm[:, sl].astype(BF16)
        qn, vjp = jax.vjp(_rms, xh, g)
        p = _mem_probs(qn, kmh)
        dob = doh.astype(BF16)
        dp = lax.dot_general(dob, vmh, NT_DIMS, preferred_element_type=F32)
        ds = (p * (dp - jnp.sum(dp * p, axis=-1, keepdims=True)) * MEM_SCALE).astype(BF16)
        dqn = jnp.dot(ds, kmh, preferred_element_type=F32)
        dkm.append(jnp.dot(ds.T, qn.astype(BF16), preferred_element_type=F32))
        dvm.append(jnp.dot(p.astype(BF16).T, dob, preferred_element_type=F32))
        dx, dgh = vjp(dqn)
        dq0.append(dx)
        dg = dg + dgh
    return (jnp.concatenate(dq0, axis=1), jnp.concatenate(dkm, axis=1),
            jnp.concatenate(dvm, axis=1), dg)


def _adamw(w, g, m, v):
    m = ADAM_B1 * m + (1.0 - ADAM_B1) * g
    v = ADAM_B2 * v + (1.0 - ADAM_B2) * jnp.square(g)
    m_hat = m / (1.0 - ADAM_B1 ** ADAM_STEP)
    v_hat = v / (1.0 - ADAM_B2 ** ADAM_STEP)
    delta = -ADAM_LR * (m_hat / (jnp.sqrt(v_hat) + ADAM_EPS) + ADAM_WD * w)
    return delta, m, v


def _adamw_small(parts, small, tiny, name):
    def body(*refs):
        p_ref, ins, outs = refs[0], refs[1:4 + 3 * len(TINY)], refs[4 + 3 * len(TINY):]
        me = 4 * lax.axis_index("x") + 2 * lax.axis_index("y") + lax.axis_index("c")
        groups = [(0, SMALL_ROWS)]
        groups += [(pl.multiple_of(base + me * rows, 8), rows) for base, (_, _, rows) in zip(TINY_BASE, TINY)]
        for k, (start, rows) in enumerate(groups):
            g = p_ref[0, pl.ds(start, rows), :]
            for d in range(1, N_DEV):
                g = g + p_ref[d, pl.ds(start, rows), :]
            w, m, v = [r[...] for r in ins[3 * k:3 * k + 3]]
            for ref, val in zip(outs[4 * k:4 * k + 4], (g,) + _adamw(w, g, m, v)):
                ref[...] = val

    args = list(small) + [t for grp in tiny for t in grp]
    out_shape = []
    for grp in [small] + list(tiny):
        out_shape += [jax.ShapeDtypeStruct(grp[0].shape, F32)] * 4
    return _call(body, name=name, out_shape=out_shape)(parts, *args)


def _conv_fwd(x, w, name):
    s, ch = x.shape
    kw = w.shape[0]
    halo = -(-(kw - 1) // 8) * 8
    r = CONV_ROWS
    n = s // r

    def chunk(window, wv):
        acc = jnp.zeros((r, LANES), F32)
        for k in range(kw):
            shift = kw - 1 - k
            sh = window if shift == 0 else pltpu.roll(window, shift, 0)
            acc = acc + sh[halo:halo + r] * wv[k:k + 1]
        return acc

    def body(x_ref, w_ref, y_ref):
        wv = w_ref[...]
        first = jnp.concatenate([jnp.zeros((halo, LANES), F32), x_ref[0:r]], axis=0)
        y_ref[0:r] = chunk(first, wv)

        def step(i, carry):
            base = pl.multiple_of(i * r, 8)
            y_ref[pl.ds(base, r)] = chunk(x_ref[pl.ds(base - halo, r + halo)], wv)
            return carry

        lax.fori_loop(1, n, step, 0)

    return _call(
        body, name=name, grid=(ch // LANES,),
        in_specs=[pl.BlockSpec((s, LANES), lambda c: (0, c)), pl.BlockSpec((kw, LANES), lambda c: (0, c))],
        out_specs=pl.BlockSpec((s, LANES), lambda c: (0, c)),
        out_shape=jax.ShapeDtypeStruct((s, ch), F32), compiler_params=_params("parallel"),
    )(x, w)


def _conv_bwd(dy, x, w, name, ex=None):
    s, ch = x.shape
    kw = w.shape[0]
    halo = -(-(kw - 1) // 8) * 8
    r = CONV_ROWS
    n = s // r

    def dx_chunk(window, wv):
        acc = jnp.zeros((r, LANES), F32)
        for k in range(kw):
            shift = kw - 1 - k
            sh = window if shift == 0 else pltpu.roll(window, r + halo - shift, 0)
            acc = acc + sh[0:r] * wv[k:k + 1]
        return acc

    def dw_chunk(xwin, dyc, acc_ref):
        for k in range(kw):
            shift = kw - 1 - k
            sh = xwin if shift == 0 else pltpu.roll(xwin, shift, 0)
            prod = sh[halo:halo + r] * dyc
            acc_ref[k] += jnp.sum(prod.reshape(r // 8, 8, LANES), axis=0)

    grid = (ch // LANES,)

    def body(*refs):
        (dy_ref, x_ref, w_ref), (dx_ref, dw_ref, acc_ref), ex_refs = _split_refs(refs, 3, 2, ex, 1)
        _hosted(ex, ex_refs, grid, True)
        wv = w_ref[...]
        acc_ref[...] = jnp.zeros_like(acc_ref)
        xfirst = jnp.concatenate([jnp.zeros((halo, LANES), F32), x_ref[0:r]], axis=0)
        dw_chunk(xfirst, dy_ref[0:r], acc_ref)
        last = jnp.concatenate([dy_ref[s - r:s], jnp.zeros((halo, LANES), F32)], axis=0)
        dx_ref[s - r:s] = dx_chunk(last, wv)

        def step(i, carry):
            base = pl.multiple_of(i * r, 8)
            dw_chunk(x_ref[pl.ds(base - halo, r + halo)], dy_ref[pl.ds(base, r)], acc_ref)
            prev = pl.multiple_of((i - 1) * r, 8)
            dx_ref[pl.ds(prev, r)] = dx_chunk(dy_ref[pl.ds(prev, r + halo)], wv)
            return carry

        lax.fori_loop(1, n, step, 0)
        dw_ref[...] = jnp.sum(acc_ref[...], axis=1)
        _hosted(ex, ex_refs, grid, False)

    spec = pl.BlockSpec((s, LANES), lambda c: (0, c))
    wspec = pl.BlockSpec((kw, LANES), lambda c: (0, c))
    return _host_call(
        body, ex, name, grid, [spec, spec, wspec], [spec, wspec],
        [jax.ShapeDtypeStruct((s, ch), F32), jax.ShapeDtypeStruct((kw, ch), F32)], (dy, x, w),
        scratch=[pltpu.VMEM((kw, 8, LANES), F32)])


HALO = 8


def _conv3(win, w, rows):
    return (pltpu.roll(win, 2, 0)[HALO:HALO + rows] * w[0:1] + pltpu.roll(win, 1, 0)[HALO:HALO + rows] * w[1:2]
            + win[HALO:HALO + rows] * w[2:3])


def _ffn_mid_bwd(up_g, up_v, dy, w_dn, w_g, w_v, b_g, b_v, name):
    s, width = up_g.shape
    tm, tc = TM, FF_PAD
    n_row = s // tm

    def body(dy_ref, ndy_ref, wd_ref, pg_ref, g_ref, ng_ref, pv_ref, v_ref, nv_ref, wg_ref, wv_ref,
             bg_ref, bv_ref, dug_ref, duv_ref, dwg_ref, dwv_ref, dbg_ref, dbv_ref):
        i = pl.program_id(1)
        first = (i > 0).astype(F32)
        last = (i < n_row - 1).astype(F32)
        ext = tm + HALO
        wg, wv, wd = wg_ref[...], wv_ref[...], wd_ref[...]
        ndy = jnp.concatenate([ndy_ref[...] * last, jnp.zeros((HALO, D_MODEL), F32)], axis=0)
        d_ext = jnp.concatenate([
            lax.dot_general(dy_ref[...].astype(BF16), wd, NT_DIMS, preferred_element_type=F32),
            lax.dot_general(ndy.astype(BF16), wd, NT_DIMS, preferred_element_type=F32)[:HALO]], axis=0)
        xg = jnp.concatenate([pg_ref[...] * first, g_ref[...], ng_ref[...]], axis=0)
        xv = jnp.concatenate([pv_ref[...] * first, v_ref[...], nv_ref[...]], axis=0)
        taps = [[pltpu.roll(x, 2, 0), pltpu.roll(x, 1, 0), x] for x in (xg, xv)]
        conv = [sum(t[HALO:HALO + ext] * w[k:k + 1] for k, t in enumerate(tp)) for tp, w in zip(taps, (wg, wv))]
        _, vjp = jax.vjp(lambda cg_, cv_: _act(cg_, cv_, bg_ref[...], bv_ref[...]), *conv)
        dcg, dcv = vjp(d_ext)
        results = []
        for tp, w, dc in ((taps[0], wg, dcg), (taps[1], wv, dcv)):
            dup = (dc[:tm] * w[2:3] + pltpu.roll(dc, ext - 1, 0)[:tm] * w[1:2]
                   + pltpu.roll(dc, ext - 2, 0)[:tm] * w[0:1])
            own = dc[:tm]
            dw = jnp.concatenate(
                [jnp.sum(own * t[HALO:HALO + tm], axis=0, keepdims=True) for t in tp], axis=0)
            results.append((dup, dw, jnp.sum(own, axis=0, keepdims=True)))
        (dug, dwg, dbg), (duv, dwv, dbv) = results
        dug_ref[...] = dug.astype(dug_ref.dtype)
        duv_ref[...] = duv.astype(duv_ref.dtype)

        @pl.when(i == 0)
        def _():
            dwg_ref[...], dwv_ref[...], dbg_ref[...], dbv_ref[...] = dwg, dwv, dbg, dbv

        @pl.when(i > 0)
        def _():
            dwg_ref[...] += dwg
            dwv_ref[...] += dwv
            dbg_ref[...] += dbg
            dbv_ref[...] += dbv

    per = tm // HALO
    tile = pl.BlockSpec((tm, tc), lambda c, i: (i, c))
    prev = pl.BlockSpec((HALO, tc), lambda c, i: (jnp.maximum(i * per - 1, 0), c))
    nxt = pl.BlockSpec((HALO, tc), lambda c, i: (jnp.minimum((i + 1) * per, s // HALO - 1), c))
    wspec = pl.BlockSpec((FFN_CONV_WIDTH, tc), lambda c, i: (0, c))
    bspec = pl.BlockSpec((1, tc), lambda c, i: (0, c))
    wide = jax.ShapeDtypeStruct((s, width), BF16)
    dy_tile = pl.BlockSpec((tm, D_MODEL), lambda c, i: (i, 0))
    dy_next = pl.BlockSpec((HALO, D_MODEL), lambda c, i: (jnp.minimum((i + 1) * per, s // HALO - 1), 0))
    return _call(
        body, name=name, grid=(width // tc, n_row),
        in_specs=[dy_tile, dy_next, pl.BlockSpec((tc, D_MODEL), lambda c, i: (c, 0)),
                  prev, tile, nxt, prev, tile, nxt, wspec, wspec, bspec, bspec],
        out_specs=[tile, tile, wspec, wspec, bspec, bspec],
        out_shape=[wide, wide] + [jax.ShapeDtypeStruct((FFN_CONV_WIDTH, width), F32)] * 2
        + [jax.ShapeDtypeStruct((1, width), F32)] * 2,
        compiler_params=_params("parallel", "arbitrary"),
    )(dy, dy, w_dn, up_g, up_g, up_g, up_v, up_v, up_v, w_g, w_v, b_g, b_v)


def _ffn_fwd(h3, w_up, w_g, w_v, b_g, b_v, w_dn, x2, target, name):
    s = h3.shape[0]
    tm, tc = TM, FF_PAD
    n_col = D_FF_PAD // tc

    def body(h_ref, wug_ref, wuv_ref, wg_ref, wv_ref, bg_ref, bv_ref, wd_ref, x_ref, t_ref,
             ug_ref, uv_ref, act_ref, dy_ref, sq_ref, halo_g, halo_v, y_acc):
        i, c = pl.program_id(0), pl.program_id(1)
        h = h_ref[...]
        up_g = jnp.dot(h, wug_ref[...], preferred_element_type=F32)
        up_v = jnp.dot(h, wuv_ref[...], preferred_element_type=F32)
        ug_ref[...] = up_g
        uv_ref[...] = up_v
        has_prev = i > 0
        prev_g = jnp.where(has_prev, halo_g[c], 0.0)
        prev_v = jnp.where(has_prev, halo_v[c], 0.0)
        halo_g[c] = up_g[tm - HALO:]
        halo_v[c] = up_v[tm - HALO:]
        cg = _conv3(jnp.concatenate([prev_g, up_g], axis=0), wg_ref[...], tm)
        cv = _conv3(jnp.concatenate([prev_v, up_v], axis=0), wv_ref[...], tm)
        act = _act(cg, cv, bg_ref[...], bv_ref[...]).astype(BF16)
        act_ref[...] = act
        part = jnp.dot(act, wd_ref[...], preferred_element_type=F32)

        @pl.when(c == 0)
        def _():
            y_acc[...] = part

        @pl.when(c > 0)
        def _():
            y_acc[...] += part

        @pl.when(c == n_col - 1)
        def _():
            dy, sq = _f_loss(x_ref[...] + y_acc[...], t_ref[...])
            dy_ref[...] = dy

            @pl.when(i == 0)
            def _():
                sq_ref[...] = sq

            @pl.when(i > 0)
            def _():
                sq_ref[...] += sq

    row = lambda w: pl.BlockSpec((tm, w), lambda i, c: (i, 0))
    tile = pl.BlockSpec((tm, tc), lambda i, c: (i, c))
    wspec = pl.BlockSpec((FFN_CONV_WIDTH, tc), lambda i, c: (0, c))
    bspec = pl.BlockSpec((1, tc), lambda i, c: (0, c))
    wide = jax.ShapeDtypeStruct((s, D_FF_PAD), F32)
    return _call(
        body, name=name, grid=(s // tm, n_col),
        in_specs=[row(D_MODEL),
                  pl.BlockSpec((None, D_MODEL, tc), lambda i, c: (c, 0, 0)),
                  pl.BlockSpec((None, D_MODEL, tc), lambda i, c: (n_col + c, 0, 0)),
                  wspec, wspec, bspec, bspec, pl.BlockSpec((tc, D_MODEL), lambda i, c: (c, 0)),
                  row(D_MODEL), row(D_MODEL)],
        out_specs=[tile, tile, tile, row(D_MODEL), pl.BlockSpec((1, D_MODEL), lambda i, c: (0, 0))],
        out_shape=[wide, wide, jax.ShapeDtypeStruct((s, D_FF_PAD), BF16),
                   jax.ShapeDtypeStruct((s, D_MODEL), F32), jax.ShapeDtypeStruct((1, D_MODEL), F32)],
        scratch_shapes=[pltpu.VMEM((n_col, HALO, tc), F32), pltpu.VMEM((n_col, HALO, tc), F32),
                        pltpu.VMEM((tm, D_MODEL), F32)],
        compiler_params=_params("arbitrary", "arbitrary"),
    )(h3, w_up, w_up, w_g, w_v, b_g, b_v, w_dn, x2, target)


def _chunk_mask(rows_are_queries):
    a = lax.broadcasted_iota(jnp.int32, (TQ, TQ), 0) // CHUNK
    b = lax.broadcasted_iota(jnp.int32, (TQ, TQ), 1) // CHUNK
    return (b <= a) if rows_are_queries else (a <= b)


def _head_lanes(hh):
    return slice(hh * LANES, (hh + 1) * LANES)


def _to_row(col):
    return jnp.broadcast_to(col, (TQ, LANES)).T[0:1, :]


def _flash_specs(s, heads=HEADS_PER_STEP):
    width = heads * LANES
    tile = pl.BlockSpec((TQ, width), lambda h, i: (i, h))
    whole = pl.BlockSpec((s, width), lambda h, i: (0, h))
    row_tile = pl.BlockSpec((heads, 1, 1, TQ), lambda h, i: (h, i, 0, 0))
    row_whole = pl.BlockSpec((heads, s // TQ, 1, TQ), lambda h, i: (h, 0, 0, 0))
    return tile, whole, row_tile, row_whole


def _split_refs(refs, n_in, n_out, ex, n_scratch=0):
    e_in, e_out = (len(ex.inputs), len(ex.out_shape)) if ex else (0, 0)
    a, b, c = n_in + e_in, n_in + e_in + n_out, n_in + e_in + n_out + e_out
    return refs[:n_in], refs[a:b] + refs[c:c + n_scratch], (refs[n_in:a], refs[b:c], refs[c + n_scratch:])


def _hosted(ex, ex_refs, grid, when_first):
    if ex is None:
        return
    ids = [pl.program_id(d) for d in range(len(grid))]
    cond = functools.reduce(
        lambda p, q_: p & q_, [i == (0 if when_first else g - 1) for i, g in zip(ids, grid)])

    @pl.when(cond)
    def _():
        (ex.start if when_first else ex.finish)(*ex_refs)


def _host_call(body, ex, name, grid, in_specs, out_specs, out_shape, args, scratch=()):
    e_in, e_out = (len(ex.inputs), len(ex.out_shape)) if ex else (0, 0)
    res = _call(
        body, name=name, grid=grid, in_specs=list(in_specs) + [ANY] * e_in,
        out_specs=list(out_specs) + [ANY] * e_out,
        out_shape=list(out_shape) + (ex.out_shape if ex else []),
        scratch_shapes=list(scratch) + (ex.scratch if ex else []),
        compiler_params=_params(*["arbitrary"] * len(grid)),
    )(*args, *(ex.inputs if ex else []))
    return res[:len(out_shape)], res[len(out_shape):]


def _flash_fwd(q, k, v, name, ex=None):
    s = q.shape[0]
    nq = s // TQ
    heads = FWD_HEADS
    grid = (MLA_HEADS // heads, nq)

    def body(*refs):
        (q_ref, k_ref, v_ref), (o_ref, lse_row_ref), ex_refs = _split_refs(refs, 3, 2, ex)
        _hosted(ex, ex_refs, grid, True)
        i = pl.program_id(1)
        qs = [q_ref[:, _head_lanes(hh)] for hh in range(heads)]

        def scores(j, hh):
            kj = k_ref[pl.ds(pl.multiple_of(j * TQ, TQ), TQ), _head_lanes(hh)]
            return lax.dot_general(kj, qs[hh], NT_DIMS, preferred_element_type=F32)

        def update(j, sc, m_prev, l_prev, acc, hh):
            vt = v_ref[pl.ds(pl.multiple_of(j * TQ, TQ), TQ), _head_lanes(hh)].T
            m_new = jnp.maximum(m_prev, jnp.max(sc, axis=0, keepdims=True))
            alpha = jnp.exp2(m_prev - m_new)
            p = jnp.exp2(sc - m_new)
            l_new = alpha * l_prev + jnp.sum(p, axis=0, keepdims=True)
            acc = acc * alpha + jnp.dot(vt, p.astype(BF16), preferred_element_type=F32)
            return m_new, l_new, acc

        def step(j, carry):
            out = []
            for hh in range(heads):
                sc, m_prev, l_prev, acc = carry[hh]
                out.append((scores(j + 1, hh),) + update(j, sc, m_prev, l_prev, acc, hh))
            return tuple(out)

        init = tuple((scores(0, hh), jnp.full((1, TQ), NEG, F32), jnp.zeros((1, TQ), F32),
                      jnp.zeros((LANES, TQ), F32)) for hh in range(heads))
        carry = lax.fori_loop(0, i, step, init)
        for hh, (sc, m_prev, l_prev, acc) in enumerate(carry):
            sc = jnp.where(_chunk_mask(False), sc, NEG)
            m_fin, l_fin, acc = update(i, sc, m_prev, l_prev, acc, hh)
            o_ref[:, _head_lanes(hh)] = (acc / l_fin).T
            lse_row_ref[hh, 0] = m_fin + jnp.log2(l_fin)
        _hosted(ex, ex_refs, grid, False)

    tile, whole, row_tile, _ = _flash_specs(s, heads)
    return _host_call(
        body, ex, name, grid, [tile, whole, whole], [tile, row_tile],
        [jax.ShapeDtypeStruct((s, MLA_HEADS * LANES), F32),
         jax.ShapeDtypeStruct((MLA_HEADS, nq, 1, TQ), F32)], (q, k, v))


def _attn_delta(do, o, name):
    s = do.shape[0]

    def body(do_ref, o_ref, d_ref):
        for h in range(MLA_HEADS):
            prod = do_ref[:, _head_lanes(h)] * o_ref[:, _head_lanes(h)]
            d_ref[h, 0] = _to_row(jnp.sum(prod, axis=-1, keepdims=True))

    tile = pl.BlockSpec((TQ, MLA_HEADS * LANES), lambda i: (i, 0))
    return _call(
        body, name=name, grid=(s // TQ,), in_specs=[tile, tile],
        out_specs=pl.BlockSpec((MLA_HEADS, 1, 1, TQ), lambda i: (0, i, 0, 0)),
        out_shape=jax.ShapeDtypeStruct((MLA_HEADS, s // TQ, 1, TQ), F32),
        compiler_params=_params("parallel"),
    )(do, o)


def _flash_bwd(q, k, v, do, lse_row, delta_row, name, ex=None):
    s = q.shape[0]
    nq = s // TQ
    grid = (MLA_HEADS // HEADS_PER_STEP, nq)

    def body(*refs):
        ins, (dqt_ref, dk_ref, dv_ref), ex_refs = _split_refs(refs, 6, 3, ex)
        q_ref, k_ref, v_ref, do_ref, lse_row_ref, delta_row_ref = ins
        _hosted(ex, ex_refs, grid, True)
        j = pl.program_id(1)

        @pl.when(j == 0)
        def _():
            dqt_ref[...] = jnp.zeros_like(dqt_ref)

        kjs = [k_ref[:, _head_lanes(hh)] for hh in range(HEADS_PER_STEP)]
        vjs = [v_ref[:, _head_lanes(hh)] for hh in range(HEADS_PER_STEP)]
        kts = [kj.T for kj in kjs]

        def step(i, carry, masked):
            base = pl.multiple_of(i * TQ, TQ)
            out = []
            for hh in range(HEADS_PER_STEP):
                dk, dv = carry[hh]
                qi = q_ref[pl.ds(base, TQ), _head_lanes(hh)]
                dob = do_ref[pl.ds(base, TQ), _head_lanes(hh)].astype(BF16)
                sc_t = lax.dot_general(kjs[hh], qi, NT_DIMS, preferred_element_type=F32)
                if masked:
                    sc_t = jnp.where(_chunk_mask(False), sc_t, NEG)
                p_t = jnp.exp2(sc_t - lse_row_ref[hh, i])
                dv = dv + jnp.dot(p_t.astype(BF16), dob, preferred_element_type=F32)
                dp_t = lax.dot_general(vjs[hh], dob, NT_DIMS, preferred_element_type=F32)
                ds_t = (p_t * (dp_t - delta_row_ref[hh, i])).astype(BF16)
                dk = dk + jnp.dot(ds_t, qi, preferred_element_type=F32)
                dqt_ref[hh, i] += jnp.dot(kts[hh], ds_t, preferred_element_type=F32)
                out.append((dk, dv))
            return tuple(out)

        zero = jnp.zeros((TQ, LANES), F32)
        carry = step(j, tuple((zero, zero) for _ in range(HEADS_PER_STEP)), True)
        carry = lax.fori_loop(j + 1, nq, functools.partial(step, masked=False), carry)
        for hh, (dk, dv) in enumerate(carry):
            dk_ref[:, _head_lanes(hh)] = dk
            dv_ref[:, _head_lanes(hh)] = dv.astype(dv_ref.dtype)
        _hosted(ex, ex_refs, grid, False)

    tile, whole, _, row_whole = _flash_specs(s)
    dqt_spec = pl.BlockSpec((HEADS_PER_STEP, nq, LANES, TQ), lambda h, j: (h, 0, 0, 0))
    wide = lambda dt: jax.ShapeDtypeStruct((s, MLA_HEADS * LANES), dt)
    return _host_call(
        body, ex, name, grid, [whole, tile, tile, whole, row_whole, row_whole], [dqt_spec, tile, tile],
        [jax.ShapeDtypeStruct((MLA_HEADS, nq, LANES, TQ), F32), wide(F32), wide(BF16)],
        (q, k, v, do, lse_row, delta_row))


def _side_by_side(g):
    return g.transpose(1, 0, 2).reshape(g.shape[1], N_DEV * g.shape[2])


def _col_shards(g):
    return g.reshape(g.shape[0], N_DEV, g.shape[1] // N_DEV).transpose(1, 0, 2)


def _pad_last(v, to):
    return jnp.pad(v, [(0, 0)] * (v.ndim - 1) + [(0, to - v.shape[-1])])


def _tiny_rows(v, rows):
    flat = v.reshape(v.shape[:-2] + (-1,))
    return _pad_last(flat, rows * LANES).reshape(v.shape[:-2] + (rows, LANES))


def _pack_small(vals):
    parts = []
    for (n, size), pad in zip(SMALL, SMALL_PAD):
        parts.append(jnp.pad(vals[n].reshape(-1), (0, pad - size)))
    flat = jnp.concatenate(parts)
    return jnp.pad(flat, (0, SMALL_ROWS * LANES - flat.shape[0])).reshape(SMALL_ROWS, LANES)


def _unpack_small(packed):
    flat = packed.reshape(-1)
    out, off = {}, 0
    for (n, size), pad in zip(SMALL, SMALL_PAD):
        out[n] = flat[off:off + size].reshape(1, size)
        off += pad
    return out


def _pad_heads(w, per_head, axis):
    shape = list(w.shape)
    shape[axis:axis + 1] = [MLA_HEADS, per_head]
    w = w.reshape(shape)
    pad = [(0, 0)] * len(shape)
    pad[axis + 1] = (0, LANES - per_head)
    w = jnp.pad(w, pad)
    shape[axis:axis + 2] = [MLA_HEADS * LANES]
    return w.reshape(shape)


def _unpad_heads(w, per_head, axis):
    shape = list(w.shape)
    shape[axis:axis + 1] = [MLA_HEADS, LANES]
    w = w.reshape(shape)
    w = lax.slice_in_dim(w, 0, per_head, axis=axis + 1)
    shape[axis:axis + 2] = [MLA_HEADS * per_head]
    return w.reshape(shape)


def _row(v, pad_to=None):
    v = v.reshape(1, -1)
    if pad_to is not None:
        v = jnp.pad(v, ((0, 0), (0, pad_to - v.shape[1])))
    return v


def kernel(x, mem, positions, mix_norm_g, w_in, b_conv_in, w_conv_dw, b_conv_dw, conv_ln_g, conv_ln_b, q_lat_norm_g, w_uq, kv_lat_norm_g, w_ukv, q_norm_g, k_norm_g, w_out, mem_norm_x_g, mem_norm_m_g, w_mem_q, w_mem_kv, mem_q_norm_g, mem_k_norm_g, w_mem_o, ffn_norm_g, w_up, w_ffn_dw, b_ffn_dw, w_down, loss_target, m_mix_norm_g, m_w_in, m_b_conv_in, m_w_conv_dw, m_b_conv_dw, m_conv_ln_g, m_conv_ln_b, m_q_lat_norm_g, m_w_uq, m_kv_lat_norm_g, m_w_ukv, m_q_norm_g, m_k_norm_g, m_w_out, m_mem_norm_x_g, m_mem_norm_m_g, m_w_mem_q, m_w_mem_kv, m_mem_q_norm_g, m_mem_k_norm_g, m_w_mem_o, m_ffn_norm_g, m_w_up, m_w_ffn_dw, m_b_ffn_dw, m_w_down, v_mix_norm_g, v_w_in, v_b_conv_in, v_w_conv_dw, v_b_conv_dw, v_conv_ln_g, v_conv_ln_b, v_q_lat_norm_g, v_w_uq, v_kv_lat_norm_g, v_w_ukv, v_q_norm_g, v_k_norm_g, v_w_out, v_mem_norm_x_g, v_mem_norm_m_g, v_w_mem_q, v_w_mem_kv, v_mem_q_norm_g, v_mem_k_norm_g, v_w_mem_o, v_ffn_norm_g, v_w_up, v_w_ffn_dw, v_b_ffn_dw, v_w_down):
    a = dict(locals())
    seq = x.shape[1]
    xs = x.reshape(seq, D_MODEL)
    mems = mem.reshape(-1, D_MODEL)
    target = loss_target.reshape(seq, D_MODEL)

    tiny = [n for n, _, _ in TINY]
    shard = lambda n: a[n][0] if n in tiny else a[n][0].astype(BF16)
    pos = jnp.stack([2 * lax.axis_index("x") + lax.axis_index("y"), lax.axis_index("c")]).astype(jnp.int32)
    ag_first = ["w_in", "w_uq", "w_ukv", "w_conv_dw"]
    ag_later = [n for n in [b for b, _ in BIG] + tiny if n not in ag_first]
    (h1,), first = _rowwise(_f_rms, [xs], [_row(mix_norm_g)], [(D_MODEL, BF16)], [], name="rms_mix",
                            ex=_plan_all_gather([shard(n) for n in ag_first]))
    wg = dict(zip(ag_first, first))
    wi = _side_by_side(wg["w_in"])
    s3 = 2 * CONV_CH + MLA_Q_RANK + MLA_KV_RANK
    w_in_p = jnp.concatenate([
        wi[:, :s3], jnp.zeros((D_MODEL, MLA_NOPE), BF16), wi[:, s3:],
        jnp.zeros((D_MODEL, LANES - MLA_QK), BF16)], axis=1)
    w_uq_p = _side_by_side(_pad_last(wg["w_uq"], LANES))
    w_uk_p = _side_by_side(_pad_last(wg["w_ukv"][:, :, :MLA_NOPE], LANES))
    w_uv_p = _side_by_side(_pad_last(wg["w_ukv"][:, :, MLA_NOPE:], LANES))
    w_cdw = _side_by_side(wg["w_conv_dw"])

    g_mix, g_qlat, g_kvlat = _row(mix_norm_g), _row(q_lat_norm_g), _row(kv_lat_norm_g)
    b_in = _row(b_conv_in)
    b_in_a, b_in_g = b_in[:, :CONV_CH], b_in[:, CONV_CH:]
    b_cdw, ln_g, ln_b = _row(b_conv_dw), _row(conv_ln_g), _row(conv_ln_b)
    g_q, g_k = _row(q_norm_g, LANES), _row(k_norm_g, LANES)
    g_memx, g_memm = _row(mem_norm_x_g), _row(mem_norm_m_g)
    g_mq, g_mk, g_ffn = _row(mem_q_norm_g), _row(mem_k_norm_g), _row(ffn_norm_g)
    b_f = _pad_last(b_ffn_dw.reshape(N_DEV, FF_SHARD), FF_PAD)
    b_f_g, b_f_v = b_f[:4].reshape(1, D_FF_PAD), b_f[4:].reshape(1, D_FF_PAD)

    freq = ROPE_THETA ** (-jnp.arange(0, MLA_ROPE, 2, dtype=F32) / MLA_ROPE)
    inv_freq = jnp.concatenate([jnp.zeros((MLA_NOPE,), F32), freq, freq,
                                jnp.zeros((LANES - MLA_QK,), F32)]).reshape(1, LANES)
    cos, sin_a, sin_b = _rowwise(_f_rope_tab, [positions.reshape(seq, 1)], [inv_freq],
                                 [(LANES, F32)] * 3, [], name="rope_tables")

    z = _mm(h1, w_in_p, name="mm_in")
    z_rows = [(z, CONV_CH, 0), (z, CONV_CH, 1), (z, MLA_Q_RANK, 4), (z, MLA_KV_RANK, 10)]
    z_kr = (z, LANES, 11)
    u0, cqn, ckvn = _rowwise(
        _mix_pre, z_rows, [b_in_a, b_in_g, g_qlat, g_kvlat],
        [(CONV_CH, F32), (MLA_Q_RANK, BF16), (MLA_KV_RANK, BF16)], [], name="mix_pre")
    c1 = _conv_fwd(u0, w_cdw, "conv31_fwd")
    (u,) = _rowwise(lambda c, b, g, bb: (_ln_silu(c, b, g, bb),), [c1], [b_cdw, ln_g, ln_b],
                    [(CONV_CH, BF16)], [], name="ln_silu")
    q0 = _mm(cqn, w_uq_p, name="mm_uq")
    kn0 = _mm(ckvn, w_uk_p, name="mm_uk")
    v0 = _mm(ckvn, w_uv_p, out_dtype=BF16, name="mm_uv")
    qk_rows = [q0, kn0, z_kr, cos, sin_a, sin_b]
    qh, kh = _rowwise(_f_qk_prep, qk_rows, [g_q, g_k],
                      [(MLA_HEADS * LANES, BF16)] * 2, [], name="qk_prep")
    (attn, lse_row), later = _flash_fwd(
        qh, kh, v0, "flash_fwd", _plan_all_gather([shard(n) for n in ag_later]))
    wg.update(zip(ag_later, later))
    w_out = wg["w_out"].reshape(D_MODEL, D_MODEL)
    w_out_u = w_out[:CONV_CH]
    w_out_a = _pad_heads(w_out[CONV_CH:], MLA_V, 0)
    w_mq, w_mo = wg["w_mem_q"].reshape(D_MODEL, D_MODEL), wg["w_mem_o"].reshape(D_MODEL, D_MODEL)
    w_mkv = _side_by_side(wg["w_mem_kv"])
    w_up_p = _pad_last(wg["w_up"], FF_PAD)
    w_dn = jnp.pad(wg["w_down"].reshape(4, FF_SHARD, D_MODEL),
                   ((0, 0), (0, FF_PAD - FF_SHARD), (0, 0))).reshape(D_FF_PAD, D_MODEL)
    w_fdw = _pad_last(wg["w_ffn_dw"], FF_PAD)
    w_fdw_g = w_fdw[:4].transpose(1, 0, 2).reshape(FFN_CONV_WIDTH, D_FF_PAD)
    w_fdw_v = w_fdw[4:].transpose(1, 0, 2).reshape(FFN_CONV_WIDTH, D_FF_PAD)
    x1 = _mm(u, w_out_u, add=xs, name="mm_out_u")
    x1 = _mm(attn, w_out_a, add=x1, name="mm_out_a")

    (hq,) = _rowwise(_f_rms, [x1], [g_memx], [(D_MODEL, BF16)], [], name="rms_memx")
    (hm,) = _rowwise(_f_rms, [mems], [g_memm], [(D_MODEL, BF16)], [], name="rms_memm", tm=mems.shape[0])
    qm0 = _mm(hq, w_mq, name="mm_memq")
    kvm0 = _mm(hm, w_mkv, name="mm_memkv", tm=mems.shape[0])
    (km,) = _rowwise(_f_mem_k, [(kvm0, D_MODEL, 0)], [g_mk], [(D_MODEL, BF16)], [],
                     name="mem_k", tm=mems.shape[0])
    vm = kvm0[:, D_MODEL:]
    (om,) = _rowwise(_f_mem_attn, [qm0], [km, vm, g_mq], [(D_MODEL, BF16)], [], name="mem_attn")
    x2 = _mm(om, w_mo, add=x1, name="mm_memo")

    (h3,) = _rowwise(_f_rms, [x2], [g_ffn], [(D_MODEL, BF16)], [], name="rms_ffn")
    up_g, up_v, act, dy, sq = _ffn_fwd(h3, w_up_p, w_fdw_g, w_fdw_v, b_f_g, b_f_v, w_dn, x2, target, "ffn_fwd")
    loss = lax.psum(0.5 * jnp.sum(sq) / D_MODEL, ("x", "y", "c"))

    gw, gs, gt = {}, {}, {}
    gw_dn = _mm_tn(act, dy, name="tn_down").reshape(4, FF_PAD, D_MODEL)
    gw["w_down"] = gw_dn[:, :FF_SHARD].reshape(N_DEV, FF_SHARD // 2, D_MODEL)
    dup_g, dup_v, dwf_g, dwf_v, db_g, db_v = _ffn_mid_bwd(
        up_g, up_v, dy, w_dn, w_fdw_g, w_fdw_v, b_f_g, b_f_v, "ffn_mid_bwd")
    db_f = jnp.concatenate([db_g.reshape(4, FF_PAD), db_v.reshape(4, FF_PAD)], axis=0)
    gs["b_ffn_dw"] = db_f[:, :FF_SHARD].reshape(1, 2 * D_FF)
    dwf = jnp.concatenate([dwf_g.reshape(FFN_CONV_WIDTH, 4, FF_PAD), dwf_v.reshape(FFN_CONV_WIDTH, 4, FF_PAD)], axis=1)
    gt["w_ffn_dw"] = dwf[:, :, :FF_SHARD].transpose(1, 0, 2)
    gw_up = jnp.concatenate([_mm_tn(h3, dup_g, shard_cols=FF_PAD, name="tn_up_g"),
                             _mm_tn(h3, dup_v, shard_cols=FF_PAD, name="tn_up_v")], axis=0)
    gw["w_up"] = gw_up[:, :, :FF_SHARD]
    w_up_flat = _side_by_side(w_up_p)
    dh3 = _mm(dup_g, w_up_flat, trans_b=True, b_kblock=0, name="mm_up_g_t")
    dx2, gs["ffn_norm_g"] = _mm(dup_v, w_up_flat, trans_b=True, b_kblock=1, add=dh3, tm=TM,
                                rms_bwd=(x2, dy, g_ffn), name="mm_up_v_t")

    gw["w_mem_o"] = _mm_tn(om, dx2, name="tn_memo").reshape(N_DEV, -1, D_MODEL)
    dom = _mm(dx2, w_mo, trans_b=True, out_dtype=BF16, name="mm_memo_t")
    n_mem = mems.shape[0]
    dqm0, dkm, dvm, gs["mem_q_norm_g"] = _rowwise(
        _b_mem_attn, [dom, qm0], [km, vm, g_mq], [(D_MODEL, BF16)],
        [(n_mem, D_MODEL), (n_mem, D_MODEL), (1, MEM_HEAD_DIM)], name="mem_attn_bwd")
    gw["w_mem_q"] = _mm_tn(hq, dqm0, name="tn_memq").reshape(N_DEV, -1, D_MODEL)
    dx1, gs["mem_norm_x_g"] = _mm(dqm0, w_mq, trans_b=True, rms_bwd=(x1, dx2, g_memx), name="mm_memq_t")
    dkk, gs["mem_k_norm_g"] = _rowwise(_b_mem_k, [(kvm0, D_MODEL, 0), dkm], [g_mk],
                                       [(D_MODEL, F32)], [(1, MEM_HEAD_DIM)], name="mem_k_bwd", tm=n_mem)
    dkvm0 = jnp.concatenate([dkk, dvm], axis=1)
    gw["w_mem_kv"] = _col_shards(_mm_tn(hm, dkvm0, name="tn_memkv", ts=n_mem))
    dhm = _mm(dkvm0, w_mkv, trans_b=True, name="mm_memkv_t", tm=n_mem)
    _, gs["mem_norm_m_g"] = _rowwise(_b_rms_nores, [mems, dhm], [g_memm], [(D_MODEL, F32)],
                                     [(1, D_MODEL)], name="rms_memm_bwd", tm=n_mem)

    gw_out_u = _mm_tn(u, dx1, name="tn_out_u")
    gw_out_a = _mm_tn(attn, dx1, name="tn_out_a")
    gw["w_out"] = jnp.concatenate([gw_out_u, _unpad_heads(gw_out_a, MLA_V, 0)], axis=0).reshape(N_DEV, -1, D_MODEL)
    du = _mm(dx1, w_out_u, trans_b=True, name="mm_out_u_t")
    dattn = _mm(dx1, w_out_a, trans_b=True, name="mm_out_a_t")
    dc1, gs["b_conv_dw"], gs["conv_ln_g"], gs["conv_ln_b"] = _rowwise(
        _b_ln_silu, [c1, du], [b_cdw, ln_g, ln_b], [(CONV_CH, F32)], [(1, CONV_CH)] * 3, name="ln_silu_bwd")
    rs_first = ["w_up", "w_down", "w_mem_o", "w_mem_q", "w_mem_kv", "w_out"]
    grads = [gw[n] for n in rs_first]
    (du0, g_cdw), gots = _conv_bwd(dc1, u0, w_cdw, "conv31_bwd", _plan_swap_sibling(grads))
    gt["w_conv_dw"] = _col_shards(g_cdw)
    sums_first = _rs_add(grads, gots, pos, "rs_add_first")
    delta_row = _attn_delta(dattn, attn, "attn_delta")
    (dqt, dkh, dv0), recvs_first = _flash_bwd(
        qh, kh, v0, dattn, lse_row, delta_row, "flash_bwd", _plan_swap_chips(sums_first))
    dqt_row = (dqt, pl.BlockSpec((MLA_HEADS, 1, LANES, TQ), lambda i: (0, i, 0, 0)))
    dq0, dkn0, dkr, dgq, dgk = _rowwise(
        _b_qk_prep, qk_rows + [dqt_row, dkh], [g_q, g_k],
        [(MLA_HEADS * LANES, BF16)] * 2 + [(LANES, F32)], [(1, LANES)] * 2, name="qk_prep_bwd")
    gs["q_norm_g"], gs["k_norm_g"] = dgq[:, :MLA_QK], dgk[:, :MLA_QK]
    gw["w_uq"] = _col_shards(_mm_tn(cqn, dq0, name="tn_uq"))[:, :, :MLA_QK]
    g_uk = _col_shards(_mm_tn(ckvn, dkn0, name="tn_uk"))[:, :, :MLA_NOPE]
    g_uv = _col_shards(_mm_tn(ckvn, dv0, name="tn_uv"))[:, :, :MLA_V]
    gw["w_ukv"] = jnp.concatenate([g_uk, g_uv], axis=2)
    dcqn = _mm(dq0, w_uq_p, trans_b=True, name="mm_uq_t")
    dckvn = _mm(dkn0, w_uk_p, trans_b=True, name="mm_uk_t")
    dckvn = _mm(dv0, w_uv_p, trans_b=True, add=dckvn, name="mm_uv_t")
    dz, dba, dbg, gs["q_lat_norm_g"], gs["kv_lat_norm_g"] = _rowwise(
        _b_mix_pre, z_rows + [du0, dcqn, dckvn, dkr], [b_in_a, b_in_g, g_qlat, g_kvlat],
        [(IN_PAD, BF16)], [(1, CONV_CH)] * 2 + [(1, MLA_Q_RANK), (1, MLA_KV_RANK)], name="mix_pre_bwd")
    gs["b_conv_in"] = jnp.concatenate([dba, dbg], axis=1)
    gw_in = _mm_tn(h1, dz, name="tn_in")
    gw["w_in"] = _col_shards(jnp.concatenate([gw_in[:, :s3], gw_in[:, s3 + MLA_NOPE:s3 + MLA_QK]], axis=1))
    rs_last = [n for n, _ in BIG if n not in rs_first]
    grads = [gw[n] for n in rs_last]
    gots = _run_exchange(_plan_swap_sibling(grads), "rs_sibling_last")
    sums_last = _rs_add(grads, gots, pos, "rs_add_last")
    (dx, gs["mix_norm_g"]), recvs_last = _mm(dz, w_in_p, trans_b=True, rms_bwd=(xs, dx1, g_mix),
                                             name="mm_in_t", ex=_plan_swap_chips(sums_last))
    big = rs_first + rs_last
    flat = _adamw_big(list(sums_first) + list(sums_last), list(recvs_first) + list(recvs_last),
                      [a[n] for n in big], [a["m_" + n] for n in big], [a["v_" + n] for n in big],
                      pos, "adamw_big")
    res = [{n: flat[4 * i + k] for i, n in enumerate(big)} for k in range(4)]

    part = jnp.concatenate(
        [_pack_small(gs)] + [_tiny_rows(gt[n], rows).reshape(N_DEV * rows, LANES) for n, _, rows in TINY], axis=0)
    (parts,) = _all_gather([part], "ag_small_grads")
    small_in = [_pack_small({n: a[p + n] for n, _ in SMALL}) for p in ("", "m_", "v_")]
    tiny_in = [[_tiny_rows(a[p + n][0], rows) for p in ("", "m_", "v_")] for n, _, rows in TINY]
    flat = _adamw_small(parts, small_in, tiny_in, "adamw_small")
    for k in range(4):
        res[k].update(_unpack_small(flat[k]))
        for i, (n, shape, _) in enumerate(TINY):
            res[k][n] = flat[4 * (i + 1) + k].reshape(-1)[:math.prod(shape)].reshape((1,) + shape)

    return (loss, dx.reshape(1, seq, D_MODEL), *[res[k][n] for k in range(4) for n in WEIGHTS])
```

```python
import functools
import math

import jax
import jax.numpy as jnp
from jax import lax
from jax.experimental import pallas as pl
from jax.experimental.pallas import tpu as pltpu

F32 = jnp.float32
BF16 = jnp.bfloat16
EPS = 1e-6
LANES = 128
N_DEV = 8
D_MODEL = 1024
CONV_CH = 512
MLA_HEADS = 8
MLA_NOPE = 64
MLA_ROPE = 32
MLA_V = 64
MLA_QK = MLA_NOPE + MLA_ROPE
MLA_Q_RANK = 256
MLA_KV_RANK = 128
ROPE_THETA = 10000.0
IN_PAD = 2 * CONV_CH + MLA_Q_RANK + MLA_KV_RANK + LANES
MEM_HEAD_DIM = 256
D_FF = 2816
FFN_CONV_WIDTH = 3
CHUNK = 64
ATT_SCALE = 1.0 / math.sqrt(MLA_QK)
LN2 = math.log(2.0)
Q_SCALE = ATT_SCALE / LN2
MEM_SCALE = 1.0 / math.sqrt(MEM_HEAD_DIM)
ADAM_LR, ADAM_B1, ADAM_B2, ADAM_EPS, ADAM_WD, ADAM_STEP = 0.001, 0.9, 0.999, 1e-08, 0.01, 10

TM = 512
ROW_TILE = 1024
MM_ROWS = 1024
TN_ROWS = 2048
TQ = 512
HEADS_PER_STEP = 2
FWD_HEADS = 4
CONV_ROWS = 256
NEG = -1e30
VMEM_LIMIT = 56 * 1024 * 1024

MESH = pl.DeviceIdType.MESH
ANY = pl.BlockSpec(memory_space=pl.ANY)
NT_DIMS = (((1,), (1,)), ((), ()))

BIG = [
    ("w_in", (1024, 180)), ("w_uq", (256, 96)), ("w_ukv", (128, 128)), ("w_out", (128, 1024)),
    ("w_mem_q", (128, 1024)), ("w_mem_kv", (1024, 256)), ("w_mem_o", (128, 1024)),
    ("w_up", (1024, 704)), ("w_down", (352, 1024)),
]
TINY = [("w_conv_dw", (31, 64), 16), ("w_ffn_dw", (3, 704), 24)]
ROW_STEPS = 4
FF_SHARD = D_FF // 4
FF_PAD = 768
D_FF_PAD = 4 * FF_PAD
SMALL = [
    ("mix_norm_g", 1024), ("b_conv_in", 1024), ("b_conv_dw", 512), ("conv_ln_g", 512),
    ("conv_ln_b", 512), ("q_lat_norm_g", 256), ("kv_lat_norm_g", 128), ("q_norm_g", 96),
    ("k_norm_g", 96), ("mem_norm_x_g", 1024), ("mem_norm_m_g", 1024), ("mem_q_norm_g", 256),
    ("mem_k_norm_g", 256), ("ffn_norm_g", 1024), ("b_ffn_dw", 5632),
]
WEIGHTS = [
    "mix_norm_g", "w_in", "b_conv_in", "w_conv_dw", "b_conv_dw", "conv_ln_g", "conv_ln_b",
    "q_lat_norm_g", "w_uq", "kv_lat_norm_g", "w_ukv", "q_norm_g", "k_norm_g", "w_out",
    "mem_norm_x_g", "mem_norm_m_g", "w_mem_q", "w_mem_kv", "mem_q_norm_g", "mem_k_norm_g",
    "w_mem_o", "ffn_norm_g", "w_up", "w_ffn_dw", "b_ffn_dw", "w_down",
]


SMALL_PAD = [(-(-n // LANES)) * LANES for _, n in SMALL]
SMALL_ROWS = -(-sum(SMALL_PAD) // (8 * LANES)) * 8
TINY_BASE = [SMALL_ROWS + N_DEV * sum(r for _, _, r in TINY[:i]) for i in range(len(TINY))]
PART_ROWS = SMALL_ROWS + N_DEV * sum(r for _, _, r in TINY)


def _call(body, **kw):
    return pl.pallas_call(body, **kw)


def _params(*sem):
    return pltpu.CompilerParams(dimension_semantics=sem, vmem_limit_bytes=VMEM_LIMIT)


class _Exchange:
    def __init__(self, inputs, out_shape, scratch, start, finish):
        self.inputs, self.out_shape, self.scratch = list(inputs), list(out_shape), list(scratch)
        self.start, self.finish = start, finish


def _run_exchange(ex, name):
    n_in, n_out = len(ex.inputs), len(ex.out_shape)

    def body(*refs):
        parts = refs[:n_in], refs[n_in:n_in + n_out], refs[n_in + n_out:]
        ex.start(*parts)
        ex.finish(*parts)

    return _call(body, name=name, out_shape=ex.out_shape, in_specs=[ANY] * n_in,
                 out_specs=[ANY] * n_out, scratch_shapes=ex.scratch)(*ex.inputs)


def _plan_all_gather(xs):
    n = len(xs)

    def copies(x_refs, out_refs, sems):
        send_sems, recv_sems, local_sems = sems
        x, y, c = lax.axis_index("x"), lax.axis_index("y"), lax.axis_index("c")
        me, sibling = (x, y, c), (x, y, 1 - c)
        chips = [(1 - x, y), (x, 1 - y), (1 - x, 1 - y)]

        def slot(o, px, py, pc):
            return out_refs[o].at[4 * px + 2 * py + pc]

        def copy(o, k, block, to, src=None):
            return pltpu.make_async_remote_copy(
                src_ref=slot(o, *block) if src is None else src, dst_ref=slot(o, *block),
                send_sem=send_sems.at[o, k], recv_sem=recv_sems.at[o, k],
                device_id=to, device_id_type=MESH)

        mine = [pltpu.make_async_copy(x_refs[o], slot(o, *me), local_sems.at[o]) for o in range(n)]
        first = [copy(o, 0, me, sibling, src=x_refs[o]) for o in range(n)]
        first += [copy(o, 1 + j, me, (*chip, c), src=x_refs[o])
                  for j, chip in enumerate(chips) for o in range(n)]
        return me, sibling, chips, copy, mine, first

    def start(x_refs, out_refs, sems):
        _, _, _, _, mine, first = copies(x_refs, out_refs, sems)
        for cp in mine + first:
            cp.start()

    def finish(x_refs, out_refs, sems):
        me, sibling, chips, copy, mine, first = copies(x_refs, out_refs, sems)
        c = me[2]
        passed = []
        for j, chip in enumerate(chips):
            for o in range(n):
                copy(o, 1 + j, (*chip, c), me).wait_recv()
                passed.append(copy(o, 4 + j, (*chip, c), sibling))
                passed[-1].start()
        for o in range(n):
            copy(o, 0, sibling, me).wait_recv()
        for j, chip in enumerate(chips):
            for o in range(n):
                copy(o, 4 + j, (*chip, 1 - c), me).wait_recv()
        for cp in first + passed:
            cp.wait_send()
        for cp in mine:
            cp.wait()

    return _Exchange(
        xs, [jax.ShapeDtypeStruct((N_DEV,) + v.shape, v.dtype) for v in xs],
        [pltpu.SemaphoreType.DMA((n, 7)), pltpu.SemaphoreType.DMA((n, 7)), pltpu.SemaphoreType.DMA((n,))],
        start, finish)


def _all_gather(xs, name):
    return _run_exchange(_plan_all_gather(xs), name)


def _plan_swap_sibling(grads):
    n = len(grads)

    def copies(g_refs, got_refs, sems):
        send_sems, recv_sems = sems
        x, y, c = lax.axis_index("x"), lax.axis_index("y"), lax.axis_index("c")
        return [
            pltpu.make_async_remote_copy(
                src_ref=g_refs[o].at[2 * chip + 1 - c], dst_ref=got_refs[o].at[chip],
                send_sem=send_sems.at[o, chip], recv_sem=recv_sems.at[o, chip],
                device_id=(x, y, 1 - c), device_id_type=MESH)
            for o in range(n) for chip in range(4)]

    def start(g_refs, got_refs, sems):
        for cp in copies(g_refs, got_refs, sems):
            cp.start()

    def finish(g_refs, got_refs, sems):
        for cp in copies(g_refs, got_refs, sems):
            cp.wait()

    return _Exchange(
        grads, [jax.ShapeDtypeStruct((4,) + g.shape[1:], g.dtype) for g in grads],
        [pltpu.SemaphoreType.DMA((n, 4)), pltpu.SemaphoreType.DMA((n, 4))], start, finish)


def _plan_swap_chips(sums):
    n = len(sums)

    def copies(a_refs, r_refs, sems):
        send_sems, recv_sems = sems
        x, y, c = lax.axis_index("x"), lax.axis_index("y"), lax.axis_index("c")
        peers = [(x, 1 - y), (1 - x, y), (1 - x, 1 - y)]
        return [
            pltpu.make_async_remote_copy(
                src_ref=a_refs[o].at[2 * px + py], dst_ref=r_refs[o].at[k],
                send_sem=send_sems.at[o, k], recv_sem=recv_sems.at[o, k],
                device_id=(px, py, c), device_id_type=MESH)
            for k, (px, py) in enumerate(peers) for o in range(n)]

    def start(a_refs, r_refs, sems):
        for cp in copies(a_refs, r_refs, sems):
            cp.start()

    def finish(a_refs, r_refs, sems):
        for cp in copies(a_refs, r_refs, sems):
            cp.wait()

    return _Exchange(
        sums, [jax.ShapeDtypeStruct((3,) + a.shape[1:], a.dtype) for a in sums],
        [pltpu.SemaphoreType.DMA((n, 3)), pltpu.SemaphoreType.DMA((n, 3))], start, finish)


def _shard_block(shape):
    return (None, shape[-2] // ROW_STEPS, shape[-1])


def _rs_add(grads, gots, pos, name):
    n = len(grads)

    def body(pos_ref, *refs):
        for g_ref, t_ref, o_ref in zip(refs[:n], refs[n:2 * n], refs[2 * n:]):
            o_ref[...] = g_ref[...] + t_ref[...]

    in_specs = [pl.BlockSpec(_shard_block(g.shape), lambda a, t, pos: (2 * a + pos[1], t, 0)) for g in grads]
    in_specs += [pl.BlockSpec(_shard_block(g.shape), lambda a, t, pos: (a, t, 0)) for g in gots]
    return _call(
        body, name=name,
        grid_spec=pltpu.PrefetchScalarGridSpec(
            num_scalar_prefetch=1, grid=(4, ROW_STEPS), in_specs=in_specs,
            out_specs=[pl.BlockSpec(_shard_block(g.shape), lambda a, t, pos: (a, t, 0)) for g in gots]),
        out_shape=[jax.ShapeDtypeStruct(g.shape, g.dtype) for g in gots],
        compiler_params=_params("arbitrary", "arbitrary"),
    )(pos, *grads, *gots)


def _adamw_big(sums, recvs, ws, ms, vs, pos, name):
    n = len(sums)

    def body(pos_ref, *refs):
        ins, outs = refs[:7 * n], refs[7 * n:]
        for o in range(n):
            own, r1, r2, r3, w, m, v = [r[...] for r in ins[7 * o:7 * o + 7]]
            g = ((own + r1) + r2) + r3
            for ref, val in zip(outs[4 * o:4 * o + 4], (g,) + _adamw(w, g, m, v)):
                ref[...] = val

    in_specs, args, out_specs, out_shape = [], [], [], []
    for s_, r_, w_, m_, v_ in zip(sums, recvs, ws, ms, vs):
        blk = _shard_block(w_.shape)
        in_specs.append(pl.BlockSpec(blk, lambda t, pos: (pos[0], t, 0)))
        in_specs += [pl.BlockSpec(blk, lambda t, pos, k=k: (k, t, 0)) for k in range(3)]
        in_specs += [pl.BlockSpec(blk, lambda t, pos: (0, t, 0))] * 3
        args += [s_, r_, r_, r_, w_, m_, v_]
        out_specs += [pl.BlockSpec(blk, lambda t, pos: (0, t, 0))] * 4
        out_shape += [jax.ShapeDtypeStruct(w_.shape, F32)] * 4
    return _call(
        body, name=name,
        grid_spec=pltpu.PrefetchScalarGridSpec(
            num_scalar_prefetch=1, grid=(ROW_STEPS,), in_specs=in_specs, out_specs=out_specs),
        out_shape=out_shape, compiler_params=_params("arbitrary"),
    )(pos, *args)


def _tile(n, prefs):
    for t in prefs:
        if n % t == 0:
            return t
    return n


def _mm(a, b, *, name, trans_b=False, add=None, out_dtype=F32, tm=MM_ROWS, b_kblock=0, ex=None,
        rms_bwd=None):
    m, k = a.shape
    n = b.shape[0] if trans_b else b.shape[1]
    tn = _tile(n, (1536, 1408, 1024, 768, 512, 256, 128))
    tk = _tile(k, (3072, 2048, 1536, 1408, 1024, 768, 512, 256, 128))
    nk = k // tk
    has_add = add is not None
    n_post = 3 if rms_bwd else 0
    assert m % tm == 0 and (not rms_bwd or tn == n), (m, tm, n, tn)
    grid = (m // tm, n // tn, nk)

    def body(*refs):
        ins, outs, ex_refs = _split_refs(refs, 2 + has_add + n_post, 1 + bool(rms_bwd), ex, int(nk > 1))
        _hosted(ex, ex_refs, grid, True)
        a_ref, b_ref = ins[0], ins[1]
        add_ref = ins[2] if has_add else None
        o_ref = outs[0]
        av = a_ref[...].astype(BF16)
        bv = b_ref[...].astype(BF16)
        if trans_b:
            part = lax.dot_general(av, bv, NT_DIMS, preferred_element_type=F32)
        else:
            part = jnp.dot(av, bv, preferred_element_type=F32)

        def finish(acc):
            if has_add:
                acc = acc + add_ref[...].astype(F32)
            if rms_bwd:
                x_ref, dres_ref, g_ref = ins[2 + has_add:]
                acc, dg = _b_rms(x_ref[...], acc, dres_ref[...], g_ref[...])
                first_rows = pl.program_id(0) == 0

                @pl.when(first_rows)
                def _():
                    outs[1][...] = dg

                @pl.when(jnp.logical_not(first_rows))
                def _():
                    outs[1][...] += dg
            o_ref[...] = acc.astype(o_ref.dtype)

        if nk == 1:
            finish(part)
        else:
            acc_ref = outs[-1]
            kk = pl.program_id(2)

            @pl.when(kk == 0)
            def _():
                acc_ref[...] = part

            @pl.when(kk > 0)
            def _():
                acc_ref[...] += part

            @pl.when(kk == nk - 1)
            def _():
                finish(acc_ref[...])
        _hosted(ex, ex_refs, grid, False)

    in_specs = [pl.BlockSpec((tm, tk), lambda i, j, kk: (i, kk))]
    if trans_b:
        in_specs.append(pl.BlockSpec((tn, tk), lambda i, j, kk: (j, kk + b_kblock * nk)))
    else:
        in_specs.append(pl.BlockSpec((tk, tn), lambda i, j, kk: (kk, j)))
    args = [a, b]
    tile = pl.BlockSpec((tm, tn), lambda i, j, kk: (i, j))
    out_specs, out_shape = [tile], [jax.ShapeDtypeStruct((m, n), out_dtype)]
    if has_add:
        in_specs.append(tile)
        args.append(add)
    if rms_bwd:
        in_specs += [tile, tile, pl.BlockSpec((1, n), lambda i, j, kk: (0, 0))]
        args += list(rms_bwd)
        out_specs.append(pl.BlockSpec((1, n), lambda i, j, kk: (0, 0)))
        out_shape.append(jax.ShapeDtypeStruct((1, n), F32))
    outs, hosted = _host_call(
        body, ex, name, grid, in_specs, out_specs, out_shape, args,
        scratch=[pltpu.VMEM((tm, tn), F32)] if nk > 1 else [])
    res = tuple(outs) if rms_bwd else outs[0]
    return (res, hosted) if ex else res


def _mm_tn(a, b, *, name, ts=TN_ROWS, shard_cols=None):
    s, m = a.shape
    n = b.shape[1]
    assert s % ts == 0, (s, ts)
    tm = _tile(m, (1408, 1024, 768, 512, 256, 128))
    tn = shard_cols or _tile(n, (1536, 1408, 1024, 768, 512, 256, 128))
    if shard_cols:
        out_spec = pl.BlockSpec((None, tm, tn), lambda i, j, kk: (j, i, 0))
        out_shape = jax.ShapeDtypeStruct((n // tn, m, tn), F32)
    else:
        out_spec = pl.BlockSpec((tm, tn), lambda i, j, kk: (i, j))
        out_shape = jax.ShapeDtypeStruct((m, n), F32)

    def body(a_ref, b_ref, o_ref):
        kk = pl.program_id(2)
        part = jnp.dot(a_ref[...].astype(BF16).T, b_ref[...].astype(BF16),
                       preferred_element_type=F32)

        @pl.when(kk == 0)
        def _():
            o_ref[...] = part

        @pl.when(kk > 0)
        def _():
            o_ref[...] += part

    return _call(
        body, name=name, grid=(m // tm, n // tn, s // ts),
        in_specs=[pl.BlockSpec((ts, tm), lambda i, j, kk: (kk, i)),
                  pl.BlockSpec((ts, tn), lambda i, j, kk: (kk, j))],
        out_specs=out_spec, out_shape=out_shape,
        compiler_params=_params("parallel", "parallel", "arbitrary"),
    )(a, b)


def _rowwise(fn, rows, consts, row_outs, acc_outs, *, name, tm=ROW_TILE, ex=None):
    rows = [r if isinstance(r, tuple) else (r, r.shape[1], 0) for r in rows]
    s = rows[0][0].shape[0]
    nr, nc, no, na = len(rows), len(consts), len(row_outs), len(acc_outs)
    grid = (s // tm,)

    def body(*refs):
        ins, outs_, ex_refs = _split_refs(refs, nr + nc, no + na, ex)
        _hosted(ex, ex_refs, grid, True)
        r_in, c_in = ins[:nr], ins[nr:]
        o_refs, a_refs = outs_[:no], outs_[no:]
        outs = fn(*[r[...] for r in r_in], *[c[...] for c in c_in])
        for r, v in zip(o_refs, outs[:no]):
            r[...] = v.astype(r.dtype)
        if na:
            i = pl.program_id(0)

            @pl.when(i == 0)
            def _():
                for r, v in zip(a_refs, outs[no:]):
                    r[...] = v.astype(F32)

            @pl.when(i > 0)
            def _():
                for r, v in zip(a_refs, outs[no:]):
                    r[...] += v.astype(F32)
        _hosted(ex, ex_refs, grid, False)

    in_specs = [r[1] if isinstance(r[1], pl.BlockSpec) else
                pl.BlockSpec((tm, r[1]), lambda i, cb=r[2]: (i, cb)) for r in rows]
    in_specs += [pl.BlockSpec(c.shape, lambda i: (0, 0)) for c in consts]
    out_specs = [pl.BlockSpec((tm, w), lambda i: (i, 0)) for w, _ in row_outs]
    out_specs += [pl.BlockSpec(sh, lambda i: (0, 0)) for sh in acc_outs]
    out_shape = [jax.ShapeDtypeStruct((s, w), dt) for w, dt in row_outs]
    out_shape += [jax.ShapeDtypeStruct(sh, F32) for sh in acc_outs]
    res, hosted = _host_call(body, ex, name, grid, in_specs, out_specs, out_shape,
                             [r[0] for r in rows] + list(consts))
    return (res, hosted) if ex else res


def _rms(x, g, n=None):
    ms = jnp.sum(x * x, axis=-1, keepdims=True) / float(n or x.shape[-1])
    return x * lax.rsqrt(ms + EPS) * g


def _layer_norm(x, g, b):
    mu = jnp.sum(x, axis=-1, keepdims=True) / float(x.shape[-1])
    xc = x - mu
    var = jnp.sum(xc * xc, axis=-1, keepdims=True) / float(x.shape[-1])
    return xc * lax.rsqrt(var + EPS) * g + b


def _silu(x):
    return x * jax.nn.sigmoid(x)


@jax.custom_vjp
def _rope(y, cos, sin_a, sin_b):
    return y * cos + pltpu.roll(y, 112, 1) * sin_a + pltpu.roll(y, 16, 1) * sin_b


def _rope_fwd(y, cos, sin_a, sin_b):
    return _rope(y, cos, sin_a, sin_b), (cos, sin_a, sin_b)


def _rope_bwd(res, ct):
    cos, sin_a, sin_b = res
    dy = ct * cos + pltpu.roll(ct * sin_a, 16, 1) + pltpu.roll(ct * sin_b, 112, 1)
    return dy, jnp.zeros_like(cos), jnp.zeros_like(sin_a), jnp.zeros_like(sin_b)


_rope.defvjp(_rope_fwd, _rope_bwd)


def _qk_head(xh, g, cos, sin_a, sin_b):
    return _rope(_rms(xh, g, MLA_QK), cos, sin_a, sin_b)


def _heads(x, width):
    return [x[:, h * width:(h + 1) * width] for h in range(x.shape[1] // width)]


def _f_rms(x, g):
    return (_rms(x, g),)


def _f_rope_tab(pos, inv_freq):
    ang = pos.astype(F32) * inv_freq
    lane = lax.broadcasted_iota(jnp.int32, ang.shape, 1)
    sn = jnp.sin(ang)
    first = (lane >= MLA_NOPE) & (lane < MLA_NOPE + MLA_ROPE // 2)
    second = (lane >= MLA_NOPE + MLA_ROPE // 2) & (lane < MLA_QK)
    return jnp.cos(ang), jnp.where(first, -sn, 0.0), jnp.where(second, sn, 0.0)


def _mix_pre(za, zg, zcq, zckv, ba, bg, gq, gkv):
    u0 = (za + ba) * jax.nn.sigmoid(zg + bg)
    return u0, _rms(zcq, gq), _rms(zckv, gkv)


def _ln_silu(c1, bdw, lg, lb):
    return _silu(_layer_norm(c1 + bdw, lg, lb))


def _f_qk_prep(q0, kn, kr, cos, sa, sb, gq, gk):
    qs = [_qk_head(xh, gq, cos, sa, sb) * Q_SCALE for xh in _heads(q0, LANES)]
    ks = [_qk_head(xh + kr, gk, cos, sa, sb) for xh in _heads(kn, LANES)]
    return jnp.concatenate(qs, axis=1), jnp.concatenate(ks, axis=1)


def _act(cg, cv, bg, bv):
    return _silu(cg + bg) * (cv + bv)


def _f_mem_k(kk, g):
    return (jnp.concatenate([_rms(xh, g) for xh in _heads(kk, MEM_HEAD_DIM)], axis=1),)


def _mem_probs(qn, kmh):
    s = lax.dot_general(qn.astype(BF16), kmh, NT_DIMS, preferred_element_type=F32) * MEM_SCALE
    e = jnp.exp(s - jnp.max(s, axis=-1, keepdims=True))
    return e / jnp.sum(e, axis=-1, keepdims=True)


def _f_mem_attn(qm0, km, vm, g):
    outs = []
    for h, xh in enumerate(_heads(qm0, MEM_HEAD_DIM)):
        sl = slice(h * MEM_HEAD_DIM, (h + 1) * MEM_HEAD_DIM)
        p = _mem_probs(_rms(xh, g), km[:, sl])
        outs.append(jnp.dot(p.astype(BF16), vm[:, sl].astype(BF16), preferred_element_type=F32))
    return (jnp.concatenate(outs, axis=1),)


def _f_loss(y, t):
    e = y - t
    return e * (1.0 / D_MODEL), jnp.sum(e * e, axis=0, keepdims=True)


def _b_rms(x, dh, dres, g):
    _, vjp = jax.vjp(_rms, x, g)
    dx, dg = vjp(dh)
    return dx + dres, dg


def _b_rms_nores(x, dh, g):
    _, vjp = jax.vjp(_rms, x, g)
    dx, dg = vjp(dh)
    return dx, dg


def _b_mix_pre(za, zg, zcq, zckv, du0, dcqn, dckvn, dkr, ba, bg, gq, gkv):
    _, vjp = jax.vjp(_mix_pre, za, zg, zcq, zckv, ba, bg, gq, gkv)
    dza, dzg, dzcq, dzckv, dba, dbg, dgq, dgkv = vjp((du0, dcqn, dckvn))
    return jnp.concatenate([dza, dzg, dzcq, dzckv, dkr], axis=1), dba, dbg, dgq, dgkv


def _b_ln_silu(c1, du, bdw, lg, lb):
    _, vjp = jax.vjp(_ln_silu, c1, bdw, lg, lb)
    return vjp(du)


def _b_qk_prep(q0, kn, kr, cos, sa, sb, dq, dk, gq, gk):
    head = lambda xh, g: _qk_head(xh, g, cos, sa, sb)
    dk = dk * LN2
    dq0, dkn = [], []
    dkr = jnp.zeros_like(kr)
    dgq = jnp.zeros_like(gq)
    dgk = jnp.zeros_like(gk)
    for h, xh in enumerate(_heads(q0, LANES)):
        _, vjp = jax.vjp(head, xh, gq)
        dx, dg = vjp(dq[h, 0].T * ATT_SCALE)
        dq0.append(dx)
        dgq = dgq + dg
    for xh, ct in zip(_heads(kn, LANES), _heads(dk, LANES)):
        _, vjp = jax.vjp(head, xh + kr, gk)
        dx, dg = vjp(ct)
        dkn.append(dx)
        dkr = dkr + dx
        dgk = dgk + dg
    return jnp.concatenate(dq0, axis=1), jnp.concatenate(dkn, axis=1), dkr, dgq, dgk


def _b_mem_k(kk, dkm, g):
    dkk = []
    dg = jnp.zeros_like(g)
    for xh, ct in zip(_heads(kk, MEM_HEAD_DIM), _heads(dkm, MEM_HEAD_DIM)):
        _, vjp = jax.vjp(_rms, xh, g)
        dx, dgh = vjp(ct)
        dkk.append(dx)
        dg = dg + dgh
    return jnp.concatenate(dkk, axis=1), dg


def _b_mem_attn(dom, qm0, km, vm, g):
    dq0, dkm, dvm = [], [], []
    dg = jnp.zeros_like(g)
    for h, (xh, doh) in enumerate(zip(_heads(qm0, MEM_HEAD_DIM), _heads(dom, MEM_HEAD_DIM))):
        sl = slice(h * MEM_HEAD_DIM, (h + 1) * MEM_HEAD_DIM)
        kmh, vmh = km[:, sl], vm[:, sl].astype(BF16)
        qn, vjp = jax.vjp(_rms, xh, g)
        p = _mem_probs(qn, kmh)
        dob = doh.astype(BF16)
        dp = lax.dot_general(dob, vmh, NT_DIMS, preferred_element_type=F32)
        ds = (p * (dp - jnp.sum(dp * p, axis=-1, keepdims=True)) * MEM_SCALE).astype(BF16)
        dqn = jnp.dot(ds, kmh, preferred_element_type=F32)
        dkm.append(jnp.dot(ds.T, qn.astype(BF16), preferred_element_type=F32))
        dvm.append(jnp.dot(p.astype(BF16).T, dob, preferred_element_type=F32))
        dx, dgh = vjp(dqn)
        dq0.append(dx)
        dg = dg + dgh
    return (jnp.concatenate(dq0, axis=1), jnp.concatenate(dkm, axis=1),
            jnp.concatenate(dvm, axis=1), dg)


def _adamw(w, g, m, v):
    m = ADAM_B1 * m + (1.0 - ADAM_B1) * g
    v = ADAM_B2 * v + (1.0 - ADAM_B2) * jnp.square(g)
    m_hat = m / (1.0 - ADAM_B1 ** ADAM_STEP)
    v_hat = v / (1.0 - ADAM_B2 ** ADAM_STEP)
    delta = -ADAM_LR * (m_hat / (jnp.sqrt(v_hat) + ADAM_EPS) + ADAM_WD * w)
    return delta, m, v


def _adamw_small(parts, small, tiny, name):
    def body(*refs):
        p_ref, ins, outs = refs[0], refs[1:4 + 3 * len(TINY)], refs[4 + 3 * len(TINY):]
        me = 4 * lax.axis_index("x") + 2 * lax.axis_index("y") + lax.axis_index("c")
        groups = [(0, SMALL_ROWS)]
        groups += [(pl.multiple_of(base + me * rows, 8), rows) for base, (_, _, rows) in zip(TINY_BASE, TINY)]
        for k, (start, rows) in enumerate(groups):
            g = p_ref[0, pl.ds(start, rows), :]
            for d in range(1, N_DEV):
                g = g + p_ref[d, pl.ds(start, rows), :]
            w, m, v = [r[...] for r in ins[3 * k:3 * k + 3]]
            for ref, val in zip(outs[4 * k:4 * k + 4], (g,) + _adamw(w, g, m, v)):
                ref[...] = val

    args = list(small) + [t for grp in tiny for t in grp]
    out_shape = []
    for grp in [small] + list(tiny):
        out_shape += [jax.ShapeDtypeStruct(grp[0].shape, F32)] * 4
    return _call(body, name=name, out_shape=out_shape)(parts, *args)


def _conv_fwd(x, w, name):
    s, ch = x.shape
    kw = w.shape[0]
    halo = -(-(kw - 1) // 8) * 8
    r = CONV_ROWS
    n = s // r

    def chunk(window, wv):
        acc = jnp.zeros((r, LANES), F32)
        for k in range(kw):
            shift = kw - 1 - k
            sh = window if shift == 0 else pltpu.roll(window, shift, 0)
            acc = acc + sh[halo:halo + r] * wv[k:k + 1]
        return acc

    def body(x_ref, w_ref, y_ref):
        wv = w_ref[...]
        first = jnp.concatenate([jnp.zeros((halo, LANES), F32), x_ref[0:r]], axis=0)
        y_ref[0:r] = chunk(first, wv)

        def step(i, carry):
            base = pl.multiple_of(i * r, 8)
            y_ref[pl.ds(base, r)] = chunk(x_ref[pl.ds(base - halo, r + halo)], wv)
            return carry

        lax.fori_loop(1, n, step, 0)

    return _call(
        body, name=name, grid=(ch // LANES,),
        in_specs=[pl.BlockSpec((s, LANES), lambda c: (0, c)), pl.BlockSpec((kw, LANES), lambda c: (0, c))],
        out_specs=pl.BlockSpec((s, LANES), lambda c: (0, c)),
        out_shape=jax.ShapeDtypeStruct((s, ch), F32), compiler_params=_params("parallel"),
    )(x, w)


def _conv_bwd(dy, x, w, name, ex=None):
    s, ch = x.shape
    kw = w.shape[0]
    halo = -(-(kw - 1) // 8) * 8
    r = CONV_ROWS
    n = s // r

    def dx_chunk(window, wv):
        acc = jnp.zeros((r, LANES), F32)
        for k in range(kw):
            shift = kw - 1 - k
            sh = window if shift == 0 else pltpu.roll(window, r + halo - shift, 0)
            acc = acc + sh[0:r] * wv[k:k + 1]
        return acc

    def dw_chunk(xwin, dyc, acc_ref):
        for k in range(kw):
            shift = kw - 1 - k
            sh = xwin if shift == 0 else pltpu.roll(xwin, shift, 0)
            prod = sh[halo:halo + r] * dyc
            acc_ref[k] += jnp.sum(prod.reshape(r // 8, 8, LANES), axis=0)

    grid = (ch // LANES,)

    def body(*refs):
        (dy_ref, x_ref, w_ref), (dx_ref, dw_ref, acc_ref), ex_refs = _split_refs(refs, 3, 2, ex, 1)
        _hosted(ex, ex_refs, grid, True)
        wv = w_ref[...]
        acc_ref[...] = jnp.zeros_like(acc_ref)
        xfirst = jnp.concatenate([jnp.zeros((halo, LANES), F32), x_ref[0:r]], axis=0)
        dw_chunk(xfirst, dy_ref[0:r], acc_ref)
        last = jnp.concatenate([dy_ref[s - r:s], jnp.zeros((halo, LANES), F32)], axis=0)
        dx_ref[s - r:s] = dx_chunk(last, wv)

        def step(i, carry):
            base = pl.multiple_of(i * r, 8)
            dw_chunk(x_ref[pl.ds(base - halo, r + halo)], dy_ref[pl.ds(base, r)], acc_ref)
            prev = pl.multiple_of((i - 1) * r, 8)
            dx_ref[pl.ds(prev, r)] = dx_chunk(dy_ref[pl.ds(prev, r + halo)], wv)
            return carry

        lax.fori_loop(1, n, step, 0)
        dw_ref[...] = jnp.sum(acc_ref[...], axis=1)
        _hosted(ex, ex_refs, grid, False)

    spec = pl.BlockSpec((s, LANES), lambda c: (0, c))
    wspec = pl.BlockSpec((kw, LANES), lambda c: (0, c))
    return _host_call(
        body, ex, name, grid, [spec, spec, wspec], [spec, wspec],
        [jax.ShapeDtypeStruct((s, ch), F32), jax.ShapeDtypeStruct((kw, ch), F32)], (dy, x, w),
        scratch=[pltpu.VMEM((kw, 8, LANES), F32)])


HALO = 8


def _conv3(win, w, rows):
    return (pltpu.roll(win, 2, 0)[HALO:HALO + rows] * w[0:1] + pltpu.roll(win, 1, 0)[HALO:HALO + rows] * w[1:2]
            + win[HALO:HALO + rows] * w[2:3])


def _ffn_mid_bwd(up_g, up_v, dy, w_dn, w_g, w_v, b_g, b_v, name):
    s, width = up_g.shape
    tm, tc = TM, FF_PAD
    n_row = s // tm

    def body(dy_ref, ndy_ref, wd_ref, pg_ref, g_ref, ng_ref, pv_ref, v_ref, nv_ref, wg_ref, wv_ref,
             bg_ref, bv_ref, dug_ref, duv_ref, dwg_ref, dwv_ref, dbg_ref, dbv_ref):
        i = pl.program_id(1)
        first = (i > 0).astype(F32)
        last = (i < n_row - 1).astype(F32)
        ext = tm + HALO
        wg, wv, wd = wg_ref[...], wv_ref[...], wd_ref[...]
        ndy = jnp.concatenate([ndy_ref[...] * last, jnp.zeros((HALO, D_MODEL), F32)], axis=0)
        d_ext = jnp.concatenate([
            lax.dot_general(dy_ref[...].astype(BF16), wd, NT_DIMS, preferred_element_type=F32),
            lax.dot_general(ndy.astype(BF16), wd, NT_DIMS, preferred_element_type=F32)[:HALO]], axis=0)
        xg = jnp.concatenate([pg_ref[...] * first, g_ref[...], ng_ref[...]], axis=0)
        xv = jnp.concatenate([pv_ref[...] * first, v_ref[...], nv_ref[...]], axis=0)
        taps = [[pltpu.roll(x, 2, 0), pltpu.roll(x, 1, 0), x] for x in (xg, xv)]
        conv = [sum(t[HALO:HALO + ext] * w[k:k + 1] for k, t in enumerate(tp)) for tp, w in zip(taps, (wg, wv))]
        _, vjp = jax.vjp(lambda cg_, cv_: _act(cg_, cv_, bg_ref[...], bv_ref[...]), *conv)
        dcg, dcv = vjp(d_ext)
        results = []
        for tp, w, dc in ((taps[0], wg, dcg), (taps[1], wv, dcv)):
            dup = (dc[:tm] * w[2:3] + pltpu.roll(dc, ext - 1, 0)[:tm] * w[1:2]
                   + pltpu.roll(dc, ext - 2, 0)[:tm] * w[0:1])
            own = dc[:tm]
            dw = jnp.concatenate(
                [jnp.sum(own * t[HALO:HALO + tm], axis=0, keepdims=True) for t in tp], axis=0)
            results.append((dup, dw, jnp.sum(own, axis=0, keepdims=True)))
        (dug, dwg, dbg), (duv, dwv, dbv) = results
        dug_ref[...] = dug.astype(dug_ref.dtype)
        duv_ref[...] = duv.astype(duv_ref.dtype)

        @pl.when(i == 0)
        def _():
            dwg_ref[...], dwv_ref[...], dbg_ref[...], dbv_ref[...] = dwg, dwv, dbg, dbv

        @pl.when(i > 0)
        def _():
            dwg_ref[...] += dwg
            dwv_ref[...] += dwv
            dbg_ref[...] += dbg
            dbv_ref[...] += dbv

    per = tm // HALO
    tile = pl.BlockSpec((tm, tc), lambda c, i: (i, c))
    prev = pl.BlockSpec((HALO, tc), lambda c, i: (jnp.maximum(i * per - 1, 0), c))
    nxt = pl.BlockSpec((HALO, tc), lambda c, i: (jnp.minimum((i + 1) * per, s // HALO - 1), c))
    wspec = pl.BlockSpec((FFN_CONV_WIDTH, tc), lambda c, i: (0, c))
    bspec = pl.BlockSpec((1, tc), lambda c, i: (0, c))
    wide = jax.ShapeDtypeStruct((s, width), BF16)
    dy_tile = pl.BlockSpec((tm, D_MODEL), lambda c, i: (i, 0))
    dy_next = pl.BlockSpec((HALO, D_MODEL), lambda c, i: (jnp.minimum((i + 1) * per, s // HALO - 1), 0))
    return _call(
        body, name=name, grid=(width // tc, n_row),
        in_specs=[dy_tile, dy_next, pl.BlockSpec((tc, D_MODEL), lambda c, i: (c, 0)),
                  prev, tile, nxt, prev, tile, nxt, wspec, wspec, bspec, bspec],
        out_specs=[tile, tile, wspec, wspec, bspec, bspec],
        out_shape=[wide, wide] + [jax.ShapeDtypeStruct((FFN_CONV_WIDTH, width), F32)] * 2
        + [jax.ShapeDtypeStruct((1, width), F32)] * 2,
        compiler_params=_params("parallel", "arbitrary"),
    )(dy, dy, w_dn, up_g, up_g, up_g, up_v, up_v, up_v, w_g, w_v, b_g, b_v)


def _ffn_fwd(h3, w_up, w_g, w_v, b_g, b_v, w_dn, x2, target, name):
    s = h3.shape[0]
    tm, tc = TM, FF_PAD
    n_col = D_FF_PAD // tc

    def body(h_ref, wug_ref, wuv_ref, wg_ref, wv_ref, bg_ref, bv_ref, wd_ref, x_ref, t_ref,
             ug_ref, uv_ref, act_ref, dy_ref, sq_ref, halo_g, halo_v, y_acc):
        i, c = pl.program_id(0), pl.program_id(1)
        h = h_ref[...]
        up_g = jnp.dot(h, wug_ref[...], preferred_element_type=F32)
        up_v = jnp.dot(h, wuv_ref[...], preferred_element_type=F32)
        ug_ref[...] = up_g
        uv_ref[...] = up_v
        has_prev = i > 0
        prev_g = jnp.where(has_prev, halo_g[c], 0.0)
        prev_v = jnp.where(has_prev, halo_v[c], 0.0)
        halo_g[c] = up_g[tm - HALO:]
        halo_v[c] = up_v[tm - HALO:]
        cg = _conv3(jnp.concatenate([prev_g, up_g], axis=0), wg_ref[...], tm)
        cv = _conv3(jnp.concatenate([prev_v, up_v], axis=0), wv_ref[...], tm)
        act = _act(cg, cv, bg_ref[...], bv_ref[...]).astype(BF16)
        act_ref[...] = act
        part = jnp.dot(act, wd_ref[...], preferred_element_type=F32)

        @pl.when(c == 0)
        def _():
            y_acc[...] = part

        @pl.when(c > 0)
        def _():
            y_acc[...] += part

        @pl.when(c == n_col - 1)
        def _():
            dy, sq = _f_loss(x_ref[...] + y_acc[...], t_ref[...])
            dy_ref[...] = dy

            @pl.when(i == 0)
            def _():
                sq_ref[...] = sq

            @pl.when(i > 0)
            def _():
                sq_ref[...] += sq

    row = lambda w: pl.BlockSpec((tm, w), lambda i, c: (i, 0))
    tile = pl.BlockSpec((tm, tc), lambda i, c: (i, c))
    wspec = pl.BlockSpec((FFN_CONV_WIDTH, tc), lambda i, c: (0, c))
    bspec = pl.BlockSpec((1, tc), lambda i, c: (0, c))
    wide = jax.ShapeDtypeStruct((s, D_FF_PAD), F32)
    return _call(
        body, name=name, grid=(s // tm, n_col),
        in_specs=[row(D_MODEL),
                  pl.BlockSpec((None, D_MODEL, tc), lambda i, c: (c, 0, 0)),
                  pl.BlockSpec((None, D_MODEL, tc), lambda i, c: (n_col + c, 0, 0)),
                  wspec, wspec, bspec, bspec, pl.BlockSpec((tc, D_MODEL), lambda i, c: (c, 0)),
                  row(D_MODEL), row(D_MODEL)],
        out_specs=[tile, tile, tile, row(D_MODEL), pl.BlockSpec((1, D_MODEL), lambda i, c: (0, 0))],
        out_shape=[wide, wide, jax.ShapeDtypeStruct((s, D_FF_PAD), BF16),
                   jax.ShapeDtypeStruct((s, D_MODEL), F32), jax.ShapeDtypeStruct((1, D_MODEL), F32)],
        scratch_shapes=[pltpu.VMEM((n_col, HALO, tc), F32), pltpu.VMEM((n_col, HALO, tc), F32),
                        pltpu.VMEM((tm, D_MODEL), F32)],
        compiler_params=_params("arbitrary", "arbitrary"),
    )(h3, w_up, w_up, w_g, w_v, b_g, b_v, w_dn, x2, target)


def _chunk_mask(rows_are_queries):
    a = lax.broadcasted_iota(jnp.int32, (TQ, TQ), 0) // CHUNK
    b = lax.broadcasted_iota(jnp.int32, (TQ, TQ), 1) // CHUNK
    return (b <= a) if rows_are_queries else (a <= b)


def _head_lanes(hh):
    return slice(hh * LANES, (hh + 1) * LANES)


def _to_row(col):
    return jnp.broadcast_to(col, (TQ, LANES)).T[0:1, :]


def _flash_specs(s, heads=HEADS_PER_STEP):
    width = heads * LANES
    tile = pl.BlockSpec((TQ, width), lambda h, i: (i, h))
    whole = pl.BlockSpec((s, width), lambda h, i: (0, h))
    row_tile = pl.BlockSpec((heads, 1, 1, TQ), lambda h, i: (h, i, 0, 0))
    row_whole = pl.BlockSpec((heads, s // TQ, 1, TQ), lambda h, i: (h, 0, 0, 0))
    return tile, whole, row_tile, row_whole


def _split_refs(refs, n_in, n_out, ex, n_scratch=0):
    e_in, e_out = (len(ex.inputs), len(ex.out_shape)) if ex else (0, 0)
    a, b, c = n_in + e_in, n_in + e_in + n_out, n_in + e_in + n_out + e_out
    return refs[:n_in], refs[a:b] + refs[c:c + n_scratch], (refs[n_in:a], refs[b:c], refs[c + n_scratch:])


def _hosted(ex, ex_refs, grid, when_first):
    if ex is None:
        return
    ids = [pl.program_id(d) for d in range(len(grid))]
    cond = functools.reduce(
        lambda p, q_: p & q_, [i == (0 if when_first else g - 1) for i, g in zip(ids, grid)])

    @pl.when(cond)
    def _():
        (ex.start if when_first else ex.finish)(*ex_refs)


def _host_call(body, ex, name, grid, in_specs, out_specs, out_shape, args, scratch=()):
    e_in, e_out = (len(ex.inputs), len(ex.out_shape)) if ex else (0, 0)
    res = _call(
        body, name=name, grid=grid, in_specs=list(in_specs) + [ANY] * e_in,
        out_specs=list(out_specs) + [ANY] * e_out,
        out_shape=list(out_shape) + (ex.out_shape if ex else []),
        scratch_shapes=list(scratch) + (ex.scratch if ex else []),
        compiler_params=_params(*["arbitrary"] * len(grid)),
    )(*args, *(ex.inputs if ex else []))
    return res[:len(out_shape)], res[len(out_shape):]


def _flash_fwd(q, k, v, name, ex=None):
    s = q.shape[0]
    nq = s // TQ
    heads = FWD_HEADS
    grid = (MLA_HEADS // heads, nq)

    def body(*refs):
        (q_ref, k_ref, v_ref), (o_ref, lse_row_ref), ex_refs = _split_refs(refs, 3, 2, ex)
        _hosted(ex, ex_refs, grid, True)
        i = pl.program_id(1)
        qs = [q_ref[:, _head_lanes(hh)] for hh in range(heads)]

        def scores(j, hh):
            kj = k_ref[pl.ds(pl.multiple_of(j * TQ, TQ), TQ), _head_lanes(hh)]
            return lax.dot_general(kj, qs[hh], NT_DIMS, preferred_element_type=F32)

        def update(j, sc, m_prev, l_prev, acc, hh):
            vt = v_ref[pl.ds(pl.multiple_of(j * TQ, TQ), TQ), _head_lanes(hh)].T
            m_new = jnp.maximum(m_prev, jnp.max(sc, axis=0, keepdims=True))
            alpha = jnp.exp2(m_prev - m_new)
            p = jnp.exp2(sc - m_new)
            l_new = alpha * l_prev + jnp.sum(p, axis=0, keepdims=True)
            acc = acc * alpha + jnp.dot(vt, p.astype(BF16), preferred_element_type=F32)
            return m_new, l_new, acc

        def step(j, carry):
            out = []
            for hh in range(heads):
                sc, m_prev, l_prev, acc = carry[hh]
                out.append((scores(j + 1, hh),) + update(j, sc, m_prev, l_prev, acc, hh))
            return tuple(out)

        init = tuple((scores(0, hh), jnp.full((1, TQ), NEG, F32), jnp.zeros((1, TQ), F32),
                      jnp.zeros((LANES, TQ), F32)) for hh in range(heads))
        carry = lax.fori_loop(0, i, step, init)
        for hh, (sc, m_prev, l_prev, acc) in enumerate(carry):
            sc = jnp.where(_chunk_mask(False), sc, NEG)
            m_fin, l_fin, acc = update(i, sc, m_prev, l_prev, acc, hh)
            o_ref[:, _head_lanes(hh)] = (acc / l_fin).T
            lse_row_ref[hh, 0] = m_fin + jnp.log2(l_fin)
        _hosted(ex, ex_refs, grid, False)

    tile, whole, row_tile, _ = _flash_specs(s, heads)
    return _host_call(
        body, ex, name, grid, [tile, whole, whole], [tile, row_tile],
        [jax.ShapeDtypeStruct((s, MLA_HEADS * LANES), F32),
         jax.ShapeDtypeStruct((MLA_HEADS, nq, 1, TQ), F32)], (q, k, v))


def _attn_delta(do, o, name):
    s = do.shape[0]

    def body(do_ref, o_ref, d_ref):
        for h in range(MLA_HEADS):
            prod = do_ref[:, _head_lanes(h)] * o_ref[:, _head_lanes(h)]
            d_ref[h, 0] = _to_row(jnp.sum(prod, axis=-1, keepdims=True))

    tile = pl.BlockSpec((TQ, MLA_HEADS * LANES), lambda i: (i, 0))
    return _call(
        body, name=name, grid=(s // TQ,), in_specs=[tile, tile],
        out_specs=pl.BlockSpec((MLA_HEADS, 1, 1, TQ), lambda i: (0, i, 0, 0)),
        out_shape=jax.ShapeDtypeStruct((MLA_HEADS, s // TQ, 1, TQ), F32),
        compiler_params=_params("parallel"),
    )(do, o)


def _flash_bwd(q, k, v, do, lse_row, delta_row, name, ex=None):
    s = q.shape[0]
    nq = s // TQ
    grid = (MLA_HEADS // HEADS_PER_STEP, nq)

    def body(*refs):
        ins, (dqt_ref, dk_ref, dv_ref), ex_refs = _split_refs(refs, 6, 3, ex)
        q_ref, k_ref, v_ref, do_ref, lse_row_ref, delta_row_ref = ins
        _hosted(ex, ex_refs, grid, True)
        j = pl.program_id(1)

        @pl.when(j == 0)
        def _():
            dqt_ref[...] = jnp.zeros_like(dqt_ref)

        kjs = [k_ref[:, _head_lanes(hh)] for hh in range(HEADS_PER_STEP)]
        vjs = [v_ref[:, _head_lanes(hh)] for hh in range(HEADS_PER_STEP)]
        kts = [kj.T for kj in kjs]

        def step(i, carry, masked):
            base = pl.multiple_of(i * TQ, TQ)
            out = []
            for hh in range(HEADS_PER_STEP):
                dk, dv = carry[hh]
                qi = q_ref[pl.ds(base, TQ), _head_lanes(hh)]
                dob = do_ref[pl.ds(base, TQ), _head_lanes(hh)].astype(BF16)
                sc_t = lax.dot_general(kjs[hh], qi, NT_DIMS, preferred_element_type=F32)
                if masked:
                    sc_t = jnp.where(_chunk_mask(False), sc_t, NEG)
                p_t = jnp.exp2(sc_t - lse_row_ref[hh, i])
                dv = dv + jnp.dot(p_t.astype(BF16), dob, preferred_element_type=F32)
                dp_t = lax.dot_general(vjs[hh], dob, NT_DIMS, preferred_element_type=F32)
                ds_t = (p_t * (dp_t - delta_row_ref[hh, i])).astype(BF16)
                dk = dk + jnp.dot(ds_t, qi, preferred_element_type=F32)
                dqt_ref[hh, i] += jnp.dot(kts[hh], ds_t, preferred_element_type=F32)
                out.append((dk, dv))
            return tuple(out)

        zero = jnp.zeros((TQ, LANES), F32)
        carry = step(j, tuple((zero, zero) for _ in range(HEADS_PER_STEP)), True)
        carry = lax.fori_loop(j + 1, nq, functools.partial(step, masked=False), carry)
        for hh, (dk, dv) in enumerate(carry):
            dk_ref[:, _head_lanes(hh)] = dk
            dv_ref[:, _head_lanes(hh)] = dv.astype(dv_ref.dtype)
        _hosted(ex, ex_refs, grid, False)

    tile, whole, _, row_whole = _flash_specs(s)
    dqt_spec = pl.BlockSpec((HEADS_PER_STEP, nq, LANES, TQ), lambda h, j: (h, 0, 0, 0))
    wide = lambda dt: jax.ShapeDtypeStruct((s, MLA_HEADS * LANES), dt)
    return _host_call(
        body, ex, name, grid, [whole, tile, tile, whole, row_whole, row_whole], [dqt_spec, tile, tile],
        [jax.ShapeDtypeStruct((MLA_HEADS, nq, LANES, TQ), F32), wide(F32), wide(BF16)],
        (q, k, v, do, lse_row, delta_row))


def _side_by_side(g):
    return g.transpose(1, 0, 2).reshape(g.shape[1], N_DEV * g.shape[2])


def _col_shards(g):
    return g.reshape(g.shape[0], N_DEV, g.shape[1] // N_DEV).transpose(1, 0, 2)


def _pad_last(v, to):
    return jnp.pad(v, [(0, 0)] * (v.ndim - 1) + [(0, to - v.shape[-1])])


def _tiny_rows(v, rows):
    flat = v.reshape(v.shape[:-2] + (-1,))
    return _pad_last(flat, rows * LANES).reshape(v.shape[:-2] + (rows, LANES))


def _pack_small(vals):
    parts = []
    for (n, size), pad in zip(SMALL, SMALL_PAD):
        parts.append(jnp.pad(vals[n].reshape(-1), (0, pad - size)))
    flat = jnp.concatenate(parts)
    return jnp.pad(flat, (0, SMALL_ROWS * LANES - flat.shape[0])).reshape(SMALL_ROWS, LANES)


def _unpack_small(packed):
    flat = packed.reshape(-1)
    out, off = {}, 0
    for (n, size), pad in zip(SMALL, SMALL_PAD):
        out[n] = flat[off:off + size].reshape(1, size)
        off += pad
    return out


def _pad_heads(w, per_head, axis):
    shape = list(w.shape)
    shape[axis:axis + 1] = [MLA_HEADS, per_head]
    w = w.reshape(shape)
    pad = [(0, 0)] * len(shape)
    pad[axis + 1] = (0, LANES - per_head)
    w = jnp.pad(w, pad)
    shape[axis:axis + 2] = [MLA_HEADS * LANES]
    return w.reshape(shape)


def _unpad_heads(w, per_head, axis):
    shape = list(w.shape)
    shape[axis:axis + 1] = [MLA_HEADS, LANES]
    w = w.reshape(shape)
    w = lax.slice_in_dim(w, 0, per_head, axis=axis + 1)
    shape[axis:axis + 2] = [MLA_HEADS * per_head]
    return w.reshape(shape)


def _row(v, pad_to=None):
    v = v.reshape(1, -1)
    if pad_to is not None:
        v = jnp.pad(v, ((0, 0), (0, pad_to - v.shape[1])))
    return v


def kernel(x, mem, positions, mix_norm_g, w_in, b_conv_in, w_conv_dw, b_conv_dw, conv_ln_g, conv_ln_b, q_lat_norm_g, w_uq, kv_lat_norm_g, w_ukv, q_norm_g, k_norm_g, w_out, mem_norm_x_g, mem_norm_m_g, w_mem_q, w_mem_kv, mem_q_norm_g, mem_k_norm_g, w_mem_o, ffn_norm_g, w_up, w_ffn_dw, b_ffn_dw, w_down, loss_target, m_mix_norm_g, m_w_in, m_b_conv_in, m_w_conv_dw, m_b_conv_dw, m_conv_ln_g, m_conv_ln_b, m_q_lat_norm_g, m_w_uq, m_kv_lat_norm_g, m_w_ukv, m_q_norm_g, m_k_norm_g, m_w_out, m_mem_norm_x_g, m_mem_norm_m_g, m_w_mem_q, m_w_mem_kv, m_mem_q_norm_g, m_mem_k_norm_g, m_w_mem_o, m_ffn_norm_g, m_w_up, m_w_ffn_dw, m_b_ffn_dw, m_w_down, v_mix_norm_g, v_w_in, v_b_conv_in, v_w_conv_dw, v_b_conv_dw, v_conv_ln_g, v_conv_ln_b, v_q_lat_norm_g, v_w_uq, v_kv_lat_norm_g, v_w_ukv, v_q_norm_g, v_k_norm_g, v_w_out, v_mem_norm_x_g, v_mem_norm_m_g, v_w_mem_q, v_w_mem_kv, v_mem_q_norm_g, v_mem_k_norm_g, v_w_mem_o, v_ffn_norm_g, v_w_up, v_w_ffn_dw, v_b_ffn_dw, v_w_down):
    a = dict(locals())
    seq = x.shape[1]
    xs = x.reshape(seq, D_MODEL)
    mems = mem.reshape(-1, D_MODEL)
    target = loss_target.reshape(seq, D_MODEL)

    tiny = [n for n, _, _ in TINY]
    shard = lambda n: a[n][0] if n in tiny else a[n][0].astype(BF16)
    pos = jnp.stack([2 * lax.axis_index("x") + lax.axis_index("y"), lax.axis_index("c")]).astype(jnp.int32)
    ag_first = ["w_in", "w_uq", "w_ukv", "w_conv_dw"]
    ag_later = [n for n in [b for b, _ in BIG] + tiny if n not in ag_first]
    (h1,), first = _rowwise(_f_rms, [xs], [_row(mix_norm_g)], [(D_MODEL, BF16)], [], name="rms_mix",
                            ex=_plan_all_gather([shard(n) for n in ag_first]))
    wg = dict(zip(ag_first, first))
    wi = _side_by_side(wg["w_in"])
    s3 = 2 * CONV_CH + MLA_Q_RANK + MLA_KV_RANK
    w_in_p = jnp.concatenate([
        wi[:, :s3], jnp.zeros((D_MODEL, MLA_NOPE), BF16), wi[:, s3:],
        jnp.zeros((D_MODEL, LANES - MLA_QK), BF16)], axis=1)
    w_uq_p = _side_by_side(_pad_last(wg["w_uq"], LANES))
    w_uk_p = _side_by_side(_pad_last(wg["w_ukv"][:, :, :MLA_NOPE], LANES))
    w_uv_p = _side_by_side(_pad_last(wg["w_ukv"][:, :, MLA_NOPE:], LANES))
    w_cdw = _side_by_side(wg["w_conv_dw"])

    g_mix, g_qlat, g_kvlat = _row(mix_norm_g), _row(q_lat_norm_g), _row(kv_lat_norm_g)
    b_in = _row(b_conv_in)
    b_in_a, b_in_g = b_in[:, :CONV_CH], b_in[:, CONV_CH:]
    b_cdw, ln_g, ln_b = _row(b_conv_dw), _row(conv_ln_g), _row(conv_ln_b)
    g_q, g_k = _row(q_norm_g, LANES), _row(k_norm_g, LANES)
    g_memx, g_memm = _row(mem_norm_x_g), _row(mem_norm_m_g)
    g_mq, g_mk, g_ffn = _row(mem_q_norm_g), _row(mem_k_norm_g), _row(ffn_norm_g)
    b_f = _pad_last(b_ffn_dw.reshape(N_DEV, FF_SHARD), FF_PAD)
    b_f_g, b_f_v = b_f[:4].reshape(1, D_FF_PAD), b_f[4:].reshape(1, D_FF_PAD)

    freq = ROPE_THETA ** (-jnp.arange(0, MLA_ROPE, 2, dtype=F32) / MLA_ROPE)
    inv_freq = jnp.concatenate([jnp.zeros((MLA_NOPE,), F32), freq, freq,
                                jnp.zeros((LANES - MLA_QK,), F32)]).reshape(1, LANES)
    cos, sin_a, sin_b = _rowwise(_f_rope_tab, [positions.reshape(seq, 1)], [inv_freq],
                                 [(LANES, F32)] * 3, [], name="rope_tables")

    z = _mm(h1, w_in_p, name="mm_in")
    z_rows = [(z, CONV_CH, 0), (z, CONV_CH, 1), (z, MLA_Q_RANK, 4), (z, MLA_KV_RANK, 10)]
    z_kr = (z, LANES, 11)
    u0, cqn, ckvn = _rowwise(
        _mix_pre, z_rows, [b_in_a, b_in_g, g_qlat, g_kvlat],
        [(CONV_CH, F32), (MLA_Q_RANK, BF16), (MLA_KV_RANK, BF16)], [], name="mix_pre")
    c1 = _conv_fwd(u0, w_cdw, "conv31_fwd")
    (u,) = _rowwise(lambda c, b, g, bb: (_ln_silu(c, b, g, bb),), [c1], [b_cdw, ln_g, ln_b],
                    [(CONV_CH, BF16)], [], name="ln_silu")
    q0 = _mm(cqn, w_uq_p, name="mm_uq")
    kn0 = _mm(ckvn, w_uk_p, name="mm_uk")
    v0 = _mm(ckvn, w_uv_p, out_dtype=BF16, name="mm_uv")
    qk_rows = [q0, kn0, z_kr, cos, sin_a, sin_b]
    qh, kh = _rowwise(_f_qk_prep, qk_rows, [g_q, g_k],
                      [(MLA_HEADS * LANES, BF16)] * 2, [], name="qk_prep")
    (attn, lse_row), later = _flash_fwd(
        qh, kh, v0, "flash_fwd", _plan_all_gather([shard(n) for n in ag_later]))
    wg.update(zip(ag_later, later))
    w_out = wg["w_out"].reshape(D_MODEL, D_MODEL)
    w_out_u = w_out[:CONV_CH]
    w_out_a = _pad_heads(w_out[CONV_CH:], MLA_V, 0)
    w_mq, w_mo = wg["w_mem_q"].reshape(D_MODEL, D_MODEL), wg["w_mem_o"].reshape(D_MODEL, D_MODEL)
    w_mkv = _side_by_side(wg["w_mem_kv"])
    w_up_p = _pad_last(wg["w_up"], FF_PAD)
    w_dn = jnp.pad(wg["w_down"].reshape(4, FF_SHARD, D_MODEL),
                   ((0, 0), (0, FF_PAD - FF_SHARD), (0, 0))).reshape(D_FF_PAD, D_MODEL)
    w_fdw = _pad_last(wg["w_ffn_dw"], FF_PAD)
    w_fdw_g = w_fdw[:4].transpose(1, 0, 2).reshape(FFN_CONV_WIDTH, D_FF_PAD)
    w_fdw_v = w_fdw[4:].transpose(1, 0, 2).reshape(FFN_CONV_WIDTH, D_FF_PAD)
    x1 = _mm(u, w_out_u, add=xs, name="mm_out_u")
    x1 = _mm(attn, w_out_a, add=x1, name="mm_out_a")

    (hq,) = _rowwise(_f_rms, [x1], [g_memx], [(D_MODEL, BF16)], [], name="rms_memx")
    (hm,) = _rowwise(_f_rms, [mems], [g_memm], [(D_MODEL, BF16)], [], name="rms_memm", tm=mems.shape[0])
    qm0 = _mm(hq, w_mq, name="mm_memq")
    kvm0 = _mm(hm, w_mkv, name="mm_memkv", tm=mems.shape[0])
    (km,) = _rowwise(_f_mem_k, [(kvm0, D_MODEL, 0)], [g_mk], [(D_MODEL, BF16)], [],
                     name="mem_k", tm=mems.shape[0])
    vm = kvm0[:, D_MODEL:]
    (om,) = _rowwise(_f_mem_attn, [qm0], [km, vm, g_mq], [(D_MODEL, BF16)], [], name="mem_attn")
    x2 = _mm(om, w_mo, add=x1, name="mm_memo")

    (h3,) = _rowwise(_f_rms, [x2], [g_ffn], [(D_MODEL, BF16)], [], name="rms_ffn")
    up_g, up_v, act, dy, sq = _ffn_fwd(h3, w_up_p, w_fdw_g, w_fdw_v, b_f_g, b_f_v, w_dn, x2, target, "ffn_fwd")
    loss = lax.psum(0.5 * jnp.sum(sq) / D_MODEL, ("x", "y", "c"))

    gw, gs, gt = {}, {}, {}
    gw_dn = _mm_tn(act, dy, name="tn_down").reshape(4, FF_PAD, D_MODEL)
    gw["w_down"] = gw_dn[:, :FF_SHARD].reshape(N_DEV, FF_SHARD // 2, D_MODEL)
    dup_g, dup_v, dwf_g, dwf_v, db_g, db_v = _ffn_mid_bwd(
        up_g, up_v, dy, w_dn, w_fdw_g, w_fdw_v, b_f_g, b_f_v, "ffn_mid_bwd")
    db_f = jnp.concatenate([db_g.reshape(4, FF_PAD), db_v.reshape(4, FF_PAD)], axis=0)
    gs["b_ffn_dw"] = db_f[:, :FF_SHARD].reshape(1, 2 * D_FF)
    dwf = jnp.concatenate([dwf_g.reshape(FFN_CONV_WIDTH, 4, FF_PAD), dwf_v.reshape(FFN_CONV_WIDTH, 4, FF_PAD)], axis=1)
    gt["w_ffn_dw"] = dwf[:, :, :FF_SHARD].transpose(1, 0, 2)
    gw_up = jnp.concatenate([_mm_tn(h3, dup_g, shard_cols=FF_PAD, name="tn_up_g"),
                             _mm_tn(h3, dup_v, shard_cols=FF_PAD, name="tn_up_v")], axis=0)
    gw["w_up"] = gw_up[:, :, :FF_SHARD]
    w_up_flat = _side_by_side(w_up_p)
    dh3 = _mm(dup_g, w_up_flat, trans_b=True, b_kblock=0, name="mm_up_g_t")
    dx2, gs["ffn_norm_g"] = _mm(dup_v, w_up_flat, trans_b=True, b_kblock=1, add=dh3, tm=TM,
                                rms_bwd=(x2, dy, g_ffn), name="mm_up_v_t")

    gw["w_mem_o"] = _mm_tn(om, dx2, name="tn_memo").reshape(N_DEV, -1, D_MODEL)
    dom = _mm(dx2, w_mo, trans_b=True, out_dtype=BF16, name="mm_memo_t")
    n_mem = mems.shape[0]
    dqm0, dkm, dvm, gs["mem_q_norm_g"] = _rowwise(
        _b_mem_attn, [dom, qm0], [km, vm, g_mq], [(D_MODEL, BF16)],
        [(n_mem, D_MODEL), (n_mem, D_MODEL), (1, MEM_HEAD_DIM)], name="mem_attn_bwd")
    gw["w_mem_q"] = _mm_tn(hq, dqm0, name="tn_memq").reshape(N_DEV, -1, D_MODEL)
    dx1, gs["mem_norm_x_g"] = _mm(dqm0, w_mq, trans_b=True, rms_bwd=(x1, dx2, g_memx), name="mm_memq_t")
    dkk, gs["mem_k_norm_g"] = _rowwise(_b_mem_k, [(kvm0, D_MODEL, 0), dkm], [g_mk],
                                       [(D_MODEL, F32)], [(1, MEM_HEAD_DIM)], name="mem_k_bwd", tm=n_mem)
    dkvm0 = jnp.concatenate([dkk, dvm], axis=1)
    gw["w_mem_kv"] = _col_shards(_mm_tn(hm, dkvm0, name="tn_memkv", ts=n_mem))
    dhm = _mm(dkvm0, w_mkv, trans_b=True, name="mm_memkv_t", tm=n_mem)
    _, gs["mem_norm_m_g"] = _rowwise(_b_rms_nores, [mems, dhm], [g_memm], [(D_MODEL, F32)],
                                     [(1, D_MODEL)], name="rms_memm_bwd", tm=n_mem)

    gw_out_u = _mm_tn(u, dx1, name="tn_out_u")
    gw_out_a = _mm_tn(attn, dx1, name="tn_out_a")
    gw["w_out"] = jnp.concatenate([gw_out_u, _unpad_heads(gw_out_a, MLA_V, 0)], axis=0).reshape(N_DEV, -1, D_MODEL)
    du = _mm(dx1, w_out_u, trans_b=True, name="mm_out_u_t")
    dattn = _mm(dx1, w_out_a, trans_b=True, name="mm_out_a_t")
    dc1, gs["b_conv_dw"], gs["conv_ln_g"], gs["conv_ln_b"] = _rowwise(
        _b_ln_silu, [c1, du], [b_cdw, ln_g, ln_b], [(CONV_CH, F32)], [(1, CONV_CH)] * 3, name="ln_silu_bwd")
    rs_first = ["w_up", "w_down", "w_mem_o", "w_mem_q", "w_mem_kv", "w_out"]
    grads = [gw[n] for n in rs_first]
    (du0, g_cdw), gots = _conv_bwd(dc1, u0, w_cdw, "conv31_bwd", _plan_swap_sibling(grads))
    gt["w_conv_dw"] = _col_shards(g_cdw)
    sums_first = _rs_add(grads, gots, pos, "rs_add_first")
    delta_row = _attn_delta(dattn, attn, "attn_delta")
    (dqt, dkh, dv0), recvs_first = _flash_bwd(
        qh, kh, v0, dattn, lse_row, delta_row, "flash_bwd", _plan_swap_chips(sums_first))
    dqt_row = (dqt, pl.BlockSpec((MLA_HEADS, 1, LANES, TQ), lambda i: (0, i, 0, 0)))
    dq0, dkn0, dkr, dgq, dgk = _rowwise(
        _b_qk_prep, qk_rows + [dqt_row, dkh], [g_q, g_k],
        [(MLA_HEADS * LANES, BF16)] * 2 + [(LANES, F32)], [(1, LANES)] * 2, name="qk_prep_bwd", tm=TQ)
    gs["q_norm_g"], gs["k_norm_g"] = dgq[:, :MLA_QK], dgk[:, :MLA_QK]
    gw["w_uq"] = _col_shards(_mm_tn(cqn, dq0, name="tn_uq"))[:, :, :MLA_QK]
    g_uk = _col_shards(_mm_tn(ckvn, dkn0, name="tn_uk"))[:, :, :MLA_NOPE]
    g_uv = _col_shards(_mm_tn(ckvn, dv0, name="tn_uv"))[:, :, :MLA_V]
    gw["w_ukv"] = jnp.concatenate([g_uk, g_uv], axis=2)
    dcqn = _mm(dq0, w_uq_p, trans_b=True, name="mm_uq_t")
    dckvn = _mm(dkn0, w_uk_p, trans_b=True, name="mm_uk_t")
    dckvn = _mm(dv0, w_uv_p, trans_b=True, add=dckvn, name="mm_uv_t")
    dz, dba, dbg, gs["q_lat_norm_g"], gs["kv_lat_norm_g"] = _rowwise(
        _b_mix_pre, z_rows + [du0, dcqn, dckvn, dkr], [b_in_a, b_in_g, g_qlat, g_kvlat],
        [(IN_PAD, BF16)], [(1, CONV_CH)] * 2 + [(1, MLA_Q_RANK), (1, MLA_KV_RANK)], name="mix_pre_bwd")
    gs["b_conv_in"] = jnp.concatenate([dba, dbg], axis=1)
    gw_in = _mm_tn(h1, dz, name="tn_in")
    gw["w_in"] = _col_shards(jnp.concatenate([gw_in[:, :s3], gw_in[:, s3 + MLA_NOPE:s3 + MLA_QK]], axis=1))
    rs_last = [n for n, _ in BIG if n not in rs_first]
    grads = [gw[n] for n in rs_last]
    gots = _run_exchange(_plan_swap_sibling(grads), "rs_sibling_last")
    sums_last = _rs_add(grads, gots, pos, "rs_add_last")
    (dx, gs["mix_norm_g"]), recvs_last = _mm(dz, w_in_p, trans_b=True, rms_bwd=(xs, dx1, g_mix),
                                             name="mm_in_t", ex=_plan_swap_chips(sums_last))
    big = rs_first + rs_last
    flat = _adamw_big(list(sums_first) + list(sums_last), list(recvs_first) + list(recvs_last),
                      [a[n] for n in big], [a["m_" + n] for n in big], [a["v_" + n] for n in big],
                      pos, "adamw_big")
    res = [{n: flat[4 * i + k] for i, n in enumerate(big)} for k in range(4)]

    part = jnp.concatenate(
        [_pack_small(gs)] + [_tiny_rows(gt[n], rows).reshape(N_DEV * rows, LANES) for n, _, rows in TINY], axis=0)
    (parts,) = _all_gather([part], "ag_small_grads")
    small_in = [_pack_small({n: a[p + n] for n, _ in SMALL}) for p in ("", "m_", "v_")]
    tiny_in = [[_tiny_rows(a[p + n][0], rows) for p in ("", "m_", "v_")] for n, _, rows in TINY]
    flat = _adamw_small(parts, small_in, tiny_in, "adamw_small")
    for k in range(4):
        res[k].update(_unpack_small(flat[k]))
        for i, (n, shape, _) in enumerate(TINY):
            res[k][n] = flat[4 * (i + 1) + k].reshape(-1)[:math.prod(shape)].reshape((1,) + shape)

    return (loss, dx.reshape(1, seq, D_MODEL), *[res[k][n] for k in range(4) for n in WEIGHTS])
```

```python
import functools
import math

import jax
import jax.numpy as jnp
from jax import lax
from jax.experimental import pallas as pl
from jax.experimental.pallas import tpu as pltpu

F32 = jnp.float32
BF16 = jnp.bfloat16
EPS = 1e-6
LANES = 128
N_DEV = 8
D_MODEL = 1024
CONV_CH = 512
MLA_HEADS = 8
MLA_NOPE = 64
MLA_ROPE = 32
MLA_V = 64
MLA_QK = MLA_NOPE + MLA_ROPE
MLA_Q_RANK = 256
MLA_KV_RANK = 128
ROPE_THETA = 10000.0
IN_PAD = 2 * CONV_CH + MLA_Q_RANK + MLA_KV_RANK + LANES
MEM_HEAD_DIM = 256
D_FF = 2816
FFN_CONV_WIDTH = 3
CHUNK = 64
ATT_SCALE = 1.0 / math.sqrt(MLA_QK)
LN2 = math.log(2.0)
Q_SCALE = ATT_SCALE / LN2
MEM_SCALE = 1.0 / math.sqrt(MEM_HEAD_DIM)
ADAM_LR, ADAM_B1, ADAM_B2, ADAM_EPS, ADAM_WD, ADAM_STEP = 0.001, 0.9, 0.999, 1e-08, 0.01, 10

TM = 512
ROW_TILE = 1024
MM_ROWS = 1024
TN_ROWS = 2048
TQ = 512
HEADS_PER_STEP = 2
FWD_HEADS = 4
CONV_ROWS = 256
NEG = -1e30
VMEM_LIMIT = 56 * 1024 * 1024

MESH = pl.DeviceIdType.MESH
ANY = pl.BlockSpec(memory_space=pl.ANY)
NT_DIMS = (((1,), (1,)), ((), ()))

BIG = [
    ("w_in", (1024, 180)), ("w_uq", (256, 96)), ("w_ukv", (128, 128)), ("w_out", (128, 1024)),
    ("w_mem_q", (128, 1024)), ("w_mem_kv", (1024, 256)), ("w_mem_o", (128, 1024)),
    ("w_up", (1024, 704)), ("w_down", (352, 1024)),
]
TINY = [("w_conv_dw", (31, 64), 16), ("w_ffn_dw", (3, 704), 24)]
ROW_STEPS = 4
FF_SHARD = D_FF // 4
FF_PAD = 768
D_FF_PAD = 4 * FF_PAD
SMALL = [
    ("mix_norm_g", 1024), ("b_conv_in", 1024), ("b_conv_dw", 512), ("conv_ln_g", 512),
    ("conv_ln_b", 512), ("q_lat_norm_g", 256), ("kv_lat_norm_g", 128), ("q_norm_g", 96),
    ("k_norm_g", 96), ("mem_norm_x_g", 1024), ("mem_norm_m_g", 1024), ("mem_q_norm_g", 256),
    ("mem_k_norm_g", 256), ("ffn_norm_g", 1024), ("b_ffn_dw", 5632),
]
WEIGHTS = [
    "mix_norm_g", "w_in", "b_conv_in", "w_conv_dw", "b_conv_dw", "conv_ln_g", "conv_ln_b",
    "q_lat_norm_g", "w_uq", "kv_lat_norm_g", "w_ukv", "q_norm_g", "k_norm_g", "w_out",
    "mem_norm_x_g", "mem_norm_m_g", "w_mem_q", "w_mem_kv", "mem_q_norm_g", "mem_k_norm_g",
    "w_mem_o", "ffn_norm_g", "w_up", "w_ffn_dw", "b_ffn_dw", "w_down",
]


SMALL_PAD = [(-(-n // LANES)) * LANES for _, n in SMALL]
SMALL_ROWS = -(-sum(SMALL_PAD) // (8 * LANES)) * 8
TINY_BASE = [SMALL_ROWS + N_DEV * sum(r for _, _, r in TINY[:i]) for i in range(len(TINY))]
PART_ROWS = SMALL_ROWS + N_DEV * sum(r for _, _, r in TINY)


def _call(body, **kw):
    return pl.pallas_call(body, **kw)


def _params(*sem):
    return pltpu.CompilerParams(dimension_semantics=sem, vmem_limit_bytes=VMEM_LIMIT)


class _Exchange:
    def __init__(self, inputs, out_shape, scratch, start, finish):
        self.inputs, self.out_shape, self.scratch = list(inputs), list(out_shape), list(scratch)
        self.start, self.finish = start, finish


def _run_exchange(ex, name):
    n_in, n_out = len(ex.inputs), len(ex.out_shape)

    def body(*refs):
        parts = refs[:n_in], refs[n_in:n_in + n_out], refs[n_in + n_out:]
        ex.start(*parts)
        ex.finish(*parts)

    return _call(body, name=name, out_shape=ex.out_shape, in_specs=[ANY] * n_in,
                 out_specs=[ANY] * n_out, scratch_shapes=ex.scratch)(*ex.inputs)


def _plan_all_gather(xs):
    n = len(xs)

    def copies(x_refs, out_refs, sems):
        send_sems, recv_sems, local_sems = sems
        x, y, c = lax.axis_index("x"), lax.axis_index("y"), lax.axis_index("c")
        me, sibling = (x, y, c), (x, y, 1 - c)
        chips = [(1 - x, y), (x, 1 - y), (1 - x, 1 - y)]

        def slot(o, px, py, pc):
            return out_refs[o].at[4 * px + 2 * py + pc]

        def copy(o, k, block, to, src=None):
            return pltpu.make_async_remote_copy(
                src_ref=slot(o, *block) if src is None else src, dst_ref=slot(o, *block),
                send_sem=send_sems.at[o, k], recv_sem=recv_sems.at[o, k],
                device_id=to, device_id_type=MESH)

        mine = [pltpu.make_async_copy(x_refs[o], slot(o, *me), local_sems.at[o]) for o in range(n)]
        first = [copy(o, 0, me, sibling, src=x_refs[o]) for o in range(n)]
        first += [copy(o, 1 + j, me, (*chip, c), src=x_refs[o])
                  for j, chip in enumerate(chips) for o in range(n)]
        return me, sibling, chips, copy, mine, first

    def start(x_refs, out_refs, sems):
        _, _, _, _, mine, first = copies(x_refs, out_refs, sems)
        for cp in mine + first:
            cp.start()

    def finish(x_refs, out_refs, sems):
        me, sibling, chips, copy, mine, first = copies(x_refs, out_refs, sems)
        c = me[2]
        passed = []
        for j, chip in enumerate(chips):
            for o in range(n):
                copy(o, 1 + j, (*chip, c), me).wait_recv()
                passed.append(copy(o, 4 + j, (*chip, c), sibling))
                passed[-1].start()
        for o in range(n):
            copy(o, 0, sibling, me).wait_recv()
        for j, chip in enumerate(chips):
            for o in range(n):
                copy(o, 4 + j, (*chip, 1 - c), me).wait_recv()
        for cp in first + passed:
            cp.wait_send()
        for cp in mine:
            cp.wait()

    return _Exchange(
        xs, [jax.ShapeDtypeStruct((N_DEV,) + v.shape, v.dtype) for v in xs],
        [pltpu.SemaphoreType.DMA((n, 7)), pltpu.SemaphoreType.DMA((n, 7)), pltpu.SemaphoreType.DMA((n,))],
        start, finish)


def _all_gather(xs, name):
    return _run_exchange(_plan_all_gather(xs), name)


def _plan_swap_sibling(grads):
    n = len(grads)

    def copies(g_refs, got_refs, sems):
        send_sems, recv_sems = sems
        x, y, c = lax.axis_index("x"), lax.axis_index("y"), lax.axis_index("c")
        return [
            pltpu.make_async_remote_copy(
                src_ref=g_refs[o].at[2 * chip + 1 - c], dst_ref=got_refs[o].at[chip],
                send_sem=send_sems.at[o, chip], recv_sem=recv_sems.at[o, chip],
                device_id=(x, y, 1 - c), device_id_type=MESH)
            for o in range(n) for chip in range(4)]

    def start(g_refs, got_refs, sems):
        for cp in copies(g_refs, got_refs, sems):
            cp.start()

    def finish(g_refs, got_refs, sems):
        for cp in copies(g_refs, got_refs, sems):
            cp.wait()

    return _Exchange(
        grads, [jax.ShapeDtypeStruct((4,) + g.shape[1:], g.dtype) for g in grads],
        [pltpu.SemaphoreType.DMA((n, 4)), pltpu.SemaphoreType.DMA((n, 4))], start, finish)


def _plan_swap_chips(sums):
    n = len(sums)

    def copies(a_refs, r_refs, sems):
        send_sems, recv_sems = sems
        x, y, c = lax.axis_index("x"), lax.axis_index("y"), lax.axis_index("c")
        peers = [(x, 1 - y), (1 - x, y), (1 - x, 1 - y)]
        return [
            pltpu.make_async_remote_copy(
                src_ref=a_refs[o].at[2 * px + py], dst_ref=r_refs[o].at[k],
                send_sem=send_sems.at[o, k], recv_sem=recv_sems.at[o, k],
                device_id=(px, py, c), device_id_type=MESH)
            for k, (px, py) in enumerate(peers) for o in range(n)]

    def start(a_refs, r_refs, sems):
        for cp in copies(a_refs, r_refs, sems):
            cp.start()

    def finish(a_refs, r_refs, sems):
        for cp in copies(a_refs, r_refs, sems):
            cp.wait()

    return _Exchange(
        sums, [jax.ShapeDtypeStruct((3,) + a.shape[1:], a.dtype) for a in sums],
        [pltpu.SemaphoreType.DMA((n, 3)), pltpu.SemaphoreType.DMA((n, 3))], start, finish)


def _shard_block(shape):
    return (None, shape[-2] // ROW_STEPS, shape[-1])


def _rs_add(grads, gots, pos, name):
    n = len(grads)

    def body(pos_ref, *refs):
        for g_ref, t_ref, o_ref in zip(refs[:n], refs[n:2 * n], refs[2 * n:]):
            o_ref[...] = g_ref[...] + t_ref[...]

    in_specs = [pl.BlockSpec(_shard_block(g.shape), lambda a, t, pos: (2 * a + pos[1], t, 0)) for g in grads]
    in_specs += [pl.BlockSpec(_shard_block(g.shape), lambda a, t, pos: (a, t, 0)) for g in gots]
    return _call(
        body, name=name,
        grid_spec=pltpu.PrefetchScalarGridSpec(
            num_scalar_prefetch=1, grid=(4, ROW_STEPS), in_specs=in_specs,
            out_specs=[pl.BlockSpec(_shard_block(g.shape), lambda a, t, pos: (a, t, 0)) for g in gots]),
        out_shape=[jax.ShapeDtypeStruct(g.shape, g.dtype) for g in gots],
        compiler_params=_params("arbitrary", "arbitrary"),
    )(pos, *grads, *gots)


def _adamw_big(sums, recvs, ws, ms, vs, pos, name):
    n = len(sums)

    def body(pos_ref, *refs):
        ins, outs = refs[:7 * n], refs[7 * n:]
        for o in range(n):
            own, r1, r2, r3, w, m, v = [r[...] for r in ins[7 * o:7 * o + 7]]
            g = ((own + r1) + r2) + r3
            for ref, val in zip(outs[4 * o:4 * o + 4], (g,) + _adamw(w, g, m, v)):
                ref[...] = val

    in_specs, args, out_specs, out_shape = [], [], [], []
    for s_, r_, w_, m_, v_ in zip(sums, recvs, ws, ms, vs):
        blk = _shard_block(w_.shape)
        in_specs.append(pl.BlockSpec(blk, lambda t, pos: (pos[0], t, 0)))
        in_specs += [pl.BlockSpec(blk, lambda t, pos, k=k: (k, t, 0)) for k in range(3)]
        in_specs += [pl.BlockSpec(blk, lambda t, pos: (0, t, 0))] * 3
        args += [s_, r_, r_, r_, w_, m_, v_]
        out_specs += [pl.BlockSpec(blk, lambda t, pos: (0, t, 0))] * 4
        out_shape += [jax.ShapeDtypeStruct(w_.shape, F32)] * 4
    return _call(
        body, name=name,
        grid_spec=pltpu.PrefetchScalarGridSpec(
            num_scalar_prefetch=1, grid=(ROW_STEPS,), in_specs=in_specs, out_specs=out_specs),
        out_shape=out_shape, compiler_params=_params("arbitrary"),
    )(pos, *args)


def _tile(n, prefs):
    for t in prefs:
        if n % t == 0:
            return t
    return n


def _mm(a, b, *, name, trans_b=False, add=None, out_dtype=F32, tm=MM_ROWS, b_kblock=0, ex=None,
        rms_bwd=None):
    m, k = a.shape
    n = b.shape[0] if trans_b else b.shape[1]
    tn = _tile(n, (1536, 1408, 1024, 768, 512, 256, 128))
    tk = _tile(k, (3072, 2048, 1536, 1408, 1024, 768, 512, 256, 128))
    nk = k // tk
    has_add = add is not None
    n_post = 3 if rms_bwd else 0
    assert m % tm == 0 and (not rms_bwd or tn == n), (m, tm, n, tn)
    grid = (m // tm, n // tn, nk)

    def body(*refs):
        ins, outs, ex_refs = _split_refs(refs, 2 + has_add + n_post, 1 + bool(rms_bwd), ex, int(nk > 1))
        _hosted(ex, ex_refs, grid, True)
        a_ref, b_ref = ins[0], ins[1]
        add_ref = ins[2] if has_add else None
        o_ref = outs[0]
        av = a_ref[...].astype(BF16)
        bv = b_ref[...].astype(BF16)
        if trans_b:
            part = lax.dot_general(av, bv, NT_DIMS, preferred_element_type=F32)
        else:
            part = jnp.dot(av, bv, preferred_element_type=F32)

        def finish(acc):
            if has_add:
                acc = acc + add_ref[...].astype(F32)
            if rms_bwd:
                x_ref, dres_ref, g_ref = ins[2 + has_add:]
                acc, dg = _b_rms(x_ref[...], acc, dres_ref[...], g_ref[...])
                first_rows = pl.program_id(0) == 0

                @pl.when(first_rows)
                def _():
                    outs[1][...] = dg

                @pl.when(jnp.logical_not(first_rows))
                def _():
                    outs[1][...] += dg
            o_ref[...] = acc.astype(o_ref.dtype)

        if nk == 1:
            finish(part)
        else:
            acc_ref = outs[-1]
            kk = pl.program_id(2)

            @pl.when(kk == 0)
            def _():
                acc_ref[...] = part

            @pl.when(kk > 0)
            def _():
                acc_ref[...] += part

            @pl.when(kk == nk - 1)
            def _():
                finish(acc_ref[...])
        _hosted(ex, ex_refs, grid, False)

    in_specs = [pl.BlockSpec((tm, tk), lambda i, j, kk: (i, kk))]
    if trans_b:
        in_specs.append(pl.BlockSpec((tn, tk), lambda i, j, kk: (j, kk + b_kblock * nk)))
    else:
        in_specs.append(pl.BlockSpec((tk, tn), lambda i, j, kk: (kk, j)))
    args = [a, b]
    tile = pl.BlockSpec((tm, tn), lambda i, j, kk: (i, j))
    out_specs, out_shape = [tile], [jax.ShapeDtypeStruct((m, n), out_dtype)]
    if has_add:
        in_specs.append(tile)
        args.append(add)
    if rms_bwd:
        in_specs += [tile, tile, pl.BlockSpec((1, n), lambda i, j, kk: (0, 0))]
        args += list(rms_bwd)
        out_specs.append(pl.BlockSpec((1, n), lambda i, j, kk: (0, 0)))
        out_shape.append(jax.ShapeDtypeStruct((1, n), F32))
    outs, hosted = _host_call(
        body, ex, name, grid, in_specs, out_specs, out_shape, args,
        scratch=[pltpu.VMEM((tm, tn), F32)] if nk > 1 else [])
    res = tuple(outs) if rms_bwd else outs[0]
    return (res, hosted) if ex else res


def _mm_tn(a, b, *, name, ts=TN_ROWS, shard_cols=None):
    s, m = a.shape
    n = b.shape[1]
    assert s % ts == 0, (s, ts)
    tm = _tile(m, (1408, 1024, 768, 512, 256, 128))
    tn = shard_cols or _tile(n, (1536, 1408, 1024, 768, 512, 256, 128))
    if shard_cols:
        out_spec = pl.BlockSpec((None, tm, tn), lambda i, j, kk: (j, i, 0))
        out_shape = jax.ShapeDtypeStruct((n // tn, m, tn), F32)
    else:
        out_spec = pl.BlockSpec((tm, tn), lambda i, j, kk: (i, j))
        out_shape = jax.ShapeDtypeStruct((m, n), F32)

    def body(a_ref, b_ref, o_ref):
        kk = pl.program_id(2)
        part = jnp.dot(a_ref[...].astype(BF16).T, b_ref[...].astype(BF16),
                       preferred_element_type=F32)

        @pl.when(kk == 0)
        def _():
            o_ref[...] = part

        @pl.when(kk > 0)
        def _():
            o_ref[...] += part

    return _call(
        body, name=name, grid=(m // tm, n // tn, s // ts),
        in_specs=[pl.BlockSpec((ts, tm), lambda i, j, kk: (kk, i)),
                  pl.BlockSpec((ts, tn), lambda i, j, kk: (kk, j))],
        out_specs=out_spec, out_shape=out_shape,
        compiler_params=_params("parallel", "parallel", "arbitrary"),
    )(a, b)


def _rowwise(fn, rows, consts, row_outs, acc_outs, *, name, tm=ROW_TILE, ex=None):
    rows = [r if isinstance(r, tuple) else (r, r.shape[1], 0) for r in rows]
    s = rows[0][0].shape[0]
    nr, nc, no, na = len(rows), len(consts), len(row_outs), len(acc_outs)
    grid = (s // tm,)

    def body(*refs):
        ins, outs_, ex_refs = _split_refs(refs, nr + nc, no + na, ex)
        _hosted(ex, ex_refs, grid, True)
        r_in, c_in = ins[:nr], ins[nr:]
        o_refs, a_refs = outs_[:no], outs_[no:]
        outs = fn(*[r[...] for r in r_in], *[c[...] for c in c_in])
        for r, v in zip(o_refs, outs[:no]):
            r[...] = v.astype(r.dtype)
        if na:
            i = pl.program_id(0)

            @pl.when(i == 0)
            def _():
                for r, v in zip(a_refs, outs[no:]):
                    r[...] = v.astype(F32)

            @pl.when(i > 0)
            def _():
                for r, v in zip(a_refs, outs[no:]):
                    r[...] += v.astype(F32)
        _hosted(ex, ex_refs, grid, False)

    in_specs = [r[1] if isinstance(r[1], pl.BlockSpec) else
                pl.BlockSpec((tm, r[1]), lambda i, cb=r[2]: (i, cb)) for r in rows]
    in_specs += [pl.BlockSpec(c.shape, lambda i: (0, 0)) for c in consts]
    out_specs = [pl.BlockSpec((tm, w), lambda i: (i, 0)) for w, _ in row_outs]
    out_specs += [pl.BlockSpec(sh, lambda i: (0, 0)) for sh in acc_outs]
    out_shape = [jax.ShapeDtypeStruct((s, w), dt) for w, dt in row_outs]
    out_shape += [jax.ShapeDtypeStruct(sh, F32) for sh in acc_outs]
    res, hosted = _host_call(body, ex, name, grid, in_specs, out_specs, out_shape,
                             [r[0] for r in rows] + list(consts))
    return (res, hosted) if ex else res


def _rms(x, g, n=None):
    ms = jnp.sum(x * x, axis=-1, keepdims=True) / float(n or x.shape[-1])
    return x * lax.rsqrt(ms + EPS) * g


def _layer_norm(x, g, b):
    mu = jnp.sum(x, axis=-1, keepdims=True) / float(x.shape[-1])
    xc = x - mu
    var = jnp.sum(xc * xc, axis=-1, keepdims=True) / float(x.shape[-1])
    return xc * lax.rsqrt(var + EPS) * g + b


def _silu(x):
    return x * jax.nn.sigmoid(x)


@jax.custom_vjp
def _rope(y, cos, sin_a, sin_b):
    return y * cos + pltpu.roll(y, 112, 1) * sin_a + pltpu.roll(y, 16, 1) * sin_b


def _rope_fwd(y, cos, sin_a, sin_b):
    return _rope(y, cos, sin_a, sin_b), (cos, sin_a, sin_b)


def _rope_bwd(res, ct):
    cos, sin_a, sin_b = res
    dy = ct * cos + pltpu.roll(ct * sin_a, 16, 1) + pltpu.roll(ct * sin_b, 112, 1)
    return dy, jnp.zeros_like(cos), jnp.zeros_like(sin_a), jnp.zeros_like(sin_b)


_rope.defvjp(_rope_fwd, _rope_bwd)


def _qk_head(xh, g, cos, sin_a, sin_b):
    return _rope(_rms(xh, g, MLA_QK), cos, sin_a, sin_b)


def _heads(x, width):
    return [x[:, h * width:(h + 1) * width] for h in range(x.shape[1] // width)]


def _f_rms(x, g):
    return (_rms(x, g),)


def _f_rope_tab(pos, inv_freq):
    ang = pos.astype(F32) * inv_freq
    lane = lax.broadcasted_iota(jnp.int32, ang.shape, 1)
    sn = jnp.sin(ang)
    first = (lane >= MLA_NOPE) & (lane < MLA_NOPE + MLA_ROPE // 2)
    second = (lane >= MLA_NOPE + MLA_ROPE // 2) & (lane < MLA_QK)
    return jnp.cos(ang), jnp.where(first, -sn, 0.0), jnp.where(second, sn, 0.0)


def _mix_pre(za, zg, zcq, zckv, ba, bg, gq, gkv):
    u0 = (za + ba) * jax.nn.sigmoid(zg + bg)
    return u0, _rms(zcq, gq), _rms(zckv, gkv)


def _ln_silu(c1, bdw, lg, lb):
    return _silu(_layer_norm(c1 + bdw, lg, lb))


def _f_qk_prep(q0, kn, kr, cos, sa, sb, gq, gk):
    qs = [_qk_head(xh, gq, cos, sa, sb) * Q_SCALE for xh in _heads(q0, LANES)]
    ks = [_qk_head(xh + kr, gk, cos, sa, sb) for xh in _heads(kn, LANES)]
    return jnp.concatenate(qs, axis=1), jnp.concatenate(ks, axis=1)


def _act(cg, cv, bg, bv):
    return _silu(cg + bg) * (cv + bv)


def _f_mem_k(kk, g):
    return (jnp.concatenate([_rms(xh, g) for xh in _heads(kk, MEM_HEAD_DIM)], axis=1),)


def _mem_probs(qn, kmh):
    s = lax.dot_general(qn.astype(BF16), kmh, NT_DIMS, preferred_element_type=F32) * MEM_SCALE
    e = jnp.exp(s - jnp.max(s, axis=-1, keepdims=True))
    return e / jnp.sum(e, axis=-1, keepdims=True)


def _f_mem_attn(qm0, km, vm, g):
    outs = []
    for h, xh in enumerate(_heads(qm0, MEM_HEAD_DIM)):
        sl = slice(h * MEM_HEAD_DIM, (h + 1) * MEM_HEAD_DIM)
        p = _mem_probs(_rms(xh, g), km[:, sl])
        outs.append(jnp.dot(p.astype(BF16), vm[:, sl].astype(BF16), preferred_element_type=F32))
    return (jnp.concatenate(outs, axis=1),)


def _f_loss(y, t):
    e = y - t
    return e * (1.0 / D_MODEL), jnp.sum(e * e, axis=0, keepdims=True)


def _b_rms(x, dh, dres, g):
    _, vjp = jax.vjp(_rms, x, g)
    dx, dg = vjp(dh)
    return dx + dres, dg


def _b_rms_nores(x, dh, g):
    _, vjp = jax.vjp(_rms, x, g)
    dx, dg = vjp(dh)
    return dx, dg


def _b_mix_pre(za, zg, zcq, zckv, du0, dcqn, dckvn, dkr, ba, bg, gq, gkv):
    _, vjp = jax.vjp(_mix_pre, za, zg, zcq, zckv, ba, bg, gq, gkv)
    dza, dzg, dzcq, dzckv, dba, dbg, dgq, dgkv = vjp((du0, dcqn, dckvn))
    return jnp.concatenate([dza, dzg, dzcq, dzckv, dkr], axis=1), dba, dbg, dgq, dgkv


def _b_ln_silu(c1, du, bdw, lg, lb):
    _, vjp = jax.vjp(_ln_silu, c1, bdw, lg, lb)
    return vjp(du)


def _b_qk_prep(q0, kn, kr, cos, sa, sb, dq, dk, gq, gk):
    head = lambda xh, g: _qk_head(xh, g, cos, sa, sb)
    dk = dk * LN2
    dq0, dkn = [], []
    dkr = jnp.zeros_like(kr)
    dgq = jnp.zeros_like(gq)
    dgk = jnp.zeros_like(gk)
    for h, xh in enumerate(_heads(q0, LANES)):
        _, vjp = jax.vjp(head, xh, gq)
        dx, dg = vjp(dq[h, 0].T * ATT_SCALE)
        dq0.append(dx)
        dgq = dgq + dg
    for xh, ct in zip(_heads(kn, LANES), _heads(dk, LANES)):
        _, vjp = jax.vjp(head, xh + kr, gk)
        dx, dg = vjp(ct)
        dkn.append(dx)
        dkr = dkr + dx
        dgk = dgk + dg
    return jnp.concatenate(dq0, axis=1), jnp.concatenate(dkn, axis=1), dkr, dgq, dgk


def _b_mem_k(kk, dkm, g):
    dkk = []
    dg = jnp.zeros_like(g)
    for xh, ct in zip(_heads(kk, MEM_HEAD_DIM), _heads(dkm, MEM_HEAD_DIM)):
        _, vjp = jax.vjp(_rms, xh, g)
        dx, dgh = vjp(ct)
        dkk.append(dx)
        dg = dg + dgh
    return jnp.concatenate(dkk, axis=1), dg


def _b_mem_attn(dom, qm0, km, vm, g):
    dq0, dkm, dvm = [], [], []
    dg = jnp.zeros_like(g)
    for h, (xh, doh) in enumerate(zip(_heads(qm0, MEM_HEAD_DIM), _heads(dom, MEM_HEAD_DIM))):
        sl = slice(h * MEM_HEAD_DIM, (h + 1) * MEM_HEAD_DIM)
        kmh, vmh = km[:, sl], vm[:, sl].astype(BF16)
        qn, vjp = jax.vjp(_rms, xh, g)
        p = _mem_probs(qn, kmh)
        dob = doh.astype(BF16)
        dp = lax.dot_general(dob, vmh, NT_DIMS, preferred_element_type=F32)
        ds = (p * (dp - jnp.sum(dp * p, axis=-1, keepdims=True)) * MEM_SCALE).astype(BF16)
        dqn = jnp.dot(ds, kmh, preferred_element_type=F32)
        dkm.append(jnp.dot(ds.T, qn.astype(BF16), preferred_element_type=F32))
        dvm.append(jnp.dot(p.astype(BF16).T, dob, preferred_element_type=F32))
        dx, dgh = vjp(dqn)
        dq0.append(dx)
        dg = dg + dgh
    return (jnp.concatenate(dq0, axis=1), jnp.concatenate(dkm, axis=1),
            jnp.concatenate(dvm, axis=1), dg)


def _adamw(w, g, m, v):
    m = ADAM_B1 * m + (1.0 - ADAM_B1) * g
    v = ADAM_B2 * v + (1.0 - ADAM_B2) * jnp.square(g)
    m_hat = m / (1.0 - ADAM_B1 ** ADAM_STEP)
    v_hat = v / (1.0 - ADAM_B2 ** ADAM_STEP)
    delta = -ADAM_LR * (m_hat / (jnp.sqrt(v_hat) + ADAM_EPS) + ADAM_WD * w)
    return delta, m, v


def _adamw_small(parts, small, tiny, name):
    def body(*refs):
        p_ref, ins, outs = refs[0], refs[1:4 + 3 * len(TINY)], refs[4 + 3 * len(TINY):]
        me = 4 * lax.axis_index("x") + 2 * lax.axis_index("y") + lax.axis_index("c")
        groups = [(0, SMALL_ROWS)]
        groups += [(pl.multiple_of(base + me * rows, 8), rows) for base, (_, _, rows) in zip(TINY_BASE, TINY)]
        for k, (start, rows) in enumerate(groups):
            g = p_ref[0, pl.ds(start, rows), :]
            for d in range(1, N_DEV):
                g = g + p_ref[d, pl.ds(start, rows), :]
            w, m, v = [r[...] for r in ins[3 * k:3 * k + 3]]
            for ref, val in zip(outs[4 * k:4 * k + 4], (g,) + _adamw(w, g, m, v)):
                ref[...] = val
        total = p_ref[0, PART_ROWS:PART_ROWS + 8, :]
        for d in range(1, N_DEV):
            total = total + p_ref[d, PART_ROWS:PART_ROWS + 8, :]
        outs[-1][...] = total

    args = list(small) + [t for grp in tiny for t in grp]
    out_shape = []
    for grp in [small] + list(tiny):
        out_shape += [jax.ShapeDtypeStruct(grp[0].shape, F32)] * 4
    out_shape.append(jax.ShapeDtypeStruct((8, LANES), F32))
    return _call(body, name=name, out_shape=out_shape)(parts, *args)


def _conv_fwd(x, w, name):
    s, ch = x.shape
    kw = w.shape[0]
    halo = -(-(kw - 1) // 8) * 8
    r = CONV_ROWS
    n = s // r

    def chunk(window, wv):
        acc = jnp.zeros((r, LANES), F32)
        for k in range(kw):
            shift = kw - 1 - k
            sh = window if shift == 0 else pltpu.roll(window, shift, 0)
            acc = acc + sh[halo:halo + r] * wv[k:k + 1]
        return acc

    def body(x_ref, w_ref, y_ref):
        wv = w_ref[...]
        first = jnp.concatenate([jnp.zeros((halo, LANES), F32), x_ref[0:r]], axis=0)
        y_ref[0:r] = chunk(first, wv)

        def step(i, carry):
            base = pl.multiple_of(i * r, 8)
            y_ref[pl.ds(base, r)] = chunk(x_ref[pl.ds(base - halo, r + halo)], wv)
            return carry

        lax.fori_loop(1, n, step, 0)

    return _call(
        body, name=name, grid=(ch // LANES,),
        in_specs=[pl.BlockSpec((s, LANES), lambda c: (0, c)), pl.BlockSpec((kw, LANES), lambda c: (0, c))],
        out_specs=pl.BlockSpec((s, LANES), lambda c: (0, c)),
        out_shape=jax.ShapeDtypeStruct((s, ch), F32), compiler_params=_params("parallel"),
    )(x, w)


def _conv_bwd(dy, x, w, name, ex=None):
    s, ch = x.shape
    kw = w.shape[0]
    halo = -(-(kw - 1) // 8) * 8
    r = CONV_ROWS
    n = s // r

    def dx_chunk(window, wv):
        acc = jnp.zeros((r, LANES), F32)
        for k in range(kw):
            shift = kw - 1 - k
            sh = window if shift == 0 else pltpu.roll(window, r + halo - shift, 0)
            acc = acc + sh[0:r] * wv[k:k + 1]
        return acc

    def dw_chunk(xwin, dyc, acc_ref):
        for k in range(kw):
            shift = kw - 1 - k
            sh = xwin if shift == 0 else pltpu.roll(xwin, shift, 0)
            prod = sh[halo:halo + r] * dyc
            acc_ref[k] += jnp.sum(prod.reshape(r // 8, 8, LANES), axis=0)

    grid = (ch // LANES,)

    def body(*refs):
        (dy_ref, x_ref, w_ref), (dx_ref, dw_ref, acc_ref), ex_refs = _split_refs(refs, 3, 2, ex, 1)
        _hosted(ex, ex_refs, grid, True)
        wv = w_ref[...]
        acc_ref[...] = jnp.zeros_like(acc_ref)
        xfirst = jnp.concatenate([jnp.zeros((halo, LANES), F32), x_ref[0:r]], axis=0)
        dw_chunk(xfirst, dy_ref[0:r], acc_ref)
        last = jnp.concatenate([dy_ref[s - r:s], jnp.zeros((halo, LANES), F32)], axis=0)
        dx_ref[s - r:s] = dx_chunk(last, wv)

        def step(i, carry):
            base = pl.multiple_of(i * r, 8)
            dw_chunk(x_ref[pl.ds(base - halo, r + halo)], dy_ref[pl.ds(base, r)], acc_ref)
            prev = pl.multiple_of((i - 1) * r, 8)
            dx_ref[pl.ds(prev, r)] = dx_chunk(dy_ref[pl.ds(prev, r + halo)], wv)
            return carry

        lax.fori_loop(1, n, step, 0)
        dw_ref[...] = jnp.sum(acc_ref[...], axis=1)
        _hosted(ex, ex_refs, grid, False)

    spec = pl.BlockSpec((s, LANES), lambda c: (0, c))
    wspec = pl.BlockSpec((kw, LANES), lambda c: (0, c))
    return _host_call(
        body, ex, name, grid, [spec, spec, wspec], [spec, wspec],
        [jax.ShapeDtypeStruct((s, ch), F32), jax.ShapeDtypeStruct((kw, ch), F32)], (dy, x, w),
        scratch=[pltpu.VMEM((kw, 8, LANES), F32)])


HALO = 8


def _conv3(win, w, rows):
    return (pltpu.roll(win, 2, 0)[HALO:HALO + rows] * w[0:1] + pltpu.roll(win, 1, 0)[HALO:HALO + rows] * w[1:2]
            + win[HALO:HALO + rows] * w[2:3])


def _ffn_mid_bwd(up_g, up_v, dy, w_dn, w_g, w_v, b_g, b_v, name):
    s, width = up_g.shape
    tm, tc = TM, FF_PAD
    n_row = s // tm

    def body(dy_ref, ndy_ref, wd_ref, pg_ref, g_ref, ng_ref, pv_ref, v_ref, nv_ref, wg_ref, wv_ref,
             bg_ref, bv_ref, dug_ref, duv_ref, dwg_ref, dwv_ref, dbg_ref, dbv_ref):
        i = pl.program_id(1)
        first = (i > 0).astype(F32)
        last = (i < n_row - 1).astype(F32)
        ext = tm + HALO
        wg, wv, wd = wg_ref[...], wv_ref[...], wd_ref[...]
        ndy = jnp.concatenate([ndy_ref[...] * last, jnp.zeros((HALO, D_MODEL), F32)], axis=0)
        d_ext = jnp.concatenate([
            lax.dot_general(dy_ref[...].astype(BF16), wd, NT_DIMS, preferred_element_type=F32),
            lax.dot_general(ndy.astype(BF16), wd, NT_DIMS, preferred_element_type=F32)[:HALO]], axis=0)
        xg = jnp.concatenate([pg_ref[...] * first, g_ref[...], ng_ref[...]], axis=0)
        xv = jnp.concatenate([pv_ref[...] * first, v_ref[...], nv_ref[...]], axis=0)
        taps = [[pltpu.roll(x, 2, 0), pltpu.roll(x, 1, 0), x] for x in (xg, xv)]
        conv = [sum(t[HALO:HALO + ext] * w[k:k + 1] for k, t in enumerate(tp)) for tp, w in zip(taps, (wg, wv))]
        _, vjp = jax.vjp(lambda cg_, cv_: _act(cg_, cv_, bg_ref[...], bv_ref[...]), *conv)
        dcg, dcv = vjp(d_ext)
        results = []
        for tp, w, dc in ((taps[0], wg, dcg), (taps[1], wv, dcv)):
            dup = (dc[:tm] * w[2:3] + pltpu.roll(dc, ext - 1, 0)[:tm] * w[1:2]
                   + pltpu.roll(dc, ext - 2, 0)[:tm] * w[0:1])
            own = dc[:tm]
            dw = jnp.concatenate(
                [jnp.sum(own * t[HALO:HALO + tm], axis=0, keepdims=True) for t in tp], axis=0)
            results.append((dup, dw, jnp.sum(own, axis=0, keepdims=True)))
        (dug, dwg, dbg), (duv, dwv, dbv) = results
        dug_ref[...] = dug.astype(dug_ref.dtype)
        duv_ref[...] = duv.astype(duv_ref.dtype)

        @pl.when(i == 0)
        def _():
            dwg_ref[...], dwv_ref[...], dbg_ref[...], dbv_ref[...] = dwg, dwv, dbg, dbv

        @pl.when(i > 0)
        def _():
            dwg_ref[...] += dwg
            dwv_ref[...] += dwv
            dbg_ref[...] += dbg
            dbv_ref[...] += dbv

    per = tm // HALO
    tile = pl.BlockSpec((tm, tc), lambda c, i: (i, c))
    prev = pl.BlockSpec((HALO, tc), lambda c, i: (jnp.maximum(i * per - 1, 0), c))
    nxt = pl.BlockSpec((HALO, tc), lambda c, i: (jnp.minimum((i + 1) * per, s // HALO - 1), c))
    wspec = pl.BlockSpec((FFN_CONV_WIDTH, tc), lambda c, i: (0, c))
    bspec = pl.BlockSpec((1, tc), lambda c, i: (0, c))
    wide = jax.ShapeDtypeStruct((s, width), BF16)
    dy_tile = pl.BlockSpec((tm, D_MODEL), lambda c, i: (i, 0))
    dy_next = pl.BlockSpec((HALO, D_MODEL), lambda c, i: (jnp.minimum((i + 1) * per, s // HALO - 1), 0))
    return _call(
        body, name=name, grid=(width // tc, n_row),
        in_specs=[dy_tile, dy_next, pl.BlockSpec((tc, D_MODEL), lambda c, i: (c, 0)),
                  prev, tile, nxt, prev, tile, nxt, wspec, wspec, bspec, bspec],
        out_specs=[tile, tile, wspec, wspec, bspec, bspec],
        out_shape=[wide, wide] + [jax.ShapeDtypeStruct((FFN_CONV_WIDTH, width), F32)] * 2
        + [jax.ShapeDtypeStruct((1, width), F32)] * 2,
        compiler_params=_params("parallel", "arbitrary"),
    )(dy, dy, w_dn, up_g, up_g, up_g, up_v, up_v, up_v, w_g, w_v, b_g, b_v)


def _ffn_fwd(h3, w_up, w_g, w_v, b_g, b_v, w_dn, x2, target, name):
    s = h3.shape[0]
    tm, tc = TM, FF_PAD
    n_col = D_FF_PAD // tc

    def body(h_ref, wug_ref, wuv_ref, wg_ref, wv_ref, bg_ref, bv_ref, wd_ref, x_ref, t_ref,
             ug_ref, uv_ref, act_ref, dy_ref, sq_ref, halo_g, halo_v, y_acc):
        i, c = pl.program_id(0), pl.program_id(1)
        h = h_ref[...]
        up_g = jnp.dot(h, wug_ref[...], preferred_element_type=F32)
        up_v = jnp.dot(h, wuv_ref[...], preferred_element_type=F32)
        ug_ref[...] = up_g
        uv_ref[...] = up_v
        has_prev = i > 0
        prev_g = jnp.where(has_prev, halo_g[c], 0.0)
        prev_v = jnp.where(has_prev, halo_v[c], 0.0)
        halo_g[c] = up_g[tm - HALO:]
        halo_v[c] = up_v[tm - HALO:]
        cg = _conv3(jnp.concatenate([prev_g, up_g], axis=0), wg_ref[...], tm)
        cv = _conv3(jnp.concatenate([prev_v, up_v], axis=0), wv_ref[...], tm)
        act = _act(cg, cv, bg_ref[...], bv_ref[...]).astype(BF16)
        act_ref[...] = act
        part = jnp.dot(act, wd_ref[...], preferred_element_type=F32)

        @pl.when(c == 0)
        def _():
            y_acc[...] = part

        @pl.when(c > 0)
        def _():
            y_acc[...] += part

        @pl.when(c == n_col - 1)
        def _():
            dy, sq = _f_loss(x_ref[...] + y_acc[...], t_ref[...])
            dy_ref[...] = dy

            @pl.when(i == 0)
            def _():
                sq_ref[...] = sq

            @pl.when(i > 0)
            def _():
                sq_ref[...] += sq

    row = lambda w: pl.BlockSpec((tm, w), lambda i, c: (i, 0))
    tile = pl.BlockSpec((tm, tc), lambda i, c: (i, c))
    wspec = pl.BlockSpec((FFN_CONV_WIDTH, tc), lambda i, c: (0, c))
    bspec = pl.BlockSpec((1, tc), lambda i, c: (0, c))
    wide = jax.ShapeDtypeStruct((s, D_FF_PAD), F32)
    return _call(
        body, name=name, grid=(s // tm, n_col),
        in_specs=[row(D_MODEL),
                  pl.BlockSpec((None, D_MODEL, tc), lambda i, c: (c, 0, 0)),
                  pl.BlockSpec((None, D_MODEL, tc), lambda i, c: (n_col + c, 0, 0)),
                  wspec, wspec, bspec, bspec, pl.BlockSpec((tc, D_MODEL), lambda i, c: (c, 0)),
                  row(D_MODEL), row(D_MODEL)],
        out_specs=[tile, tile, tile, row(D_MODEL), pl.BlockSpec((1, D_MODEL), lambda i, c: (0, 0))],
        out_shape=[wide, wide, jax.ShapeDtypeStruct((s, D_FF_PAD), BF16),
                   jax.ShapeDtypeStruct((s, D_MODEL), F32), jax.ShapeDtypeStruct((1, D_MODEL), F32)],
        scratch_shapes=[pltpu.VMEM((n_col, HALO, tc), F32), pltpu.VMEM((n_col, HALO, tc), F32),
                        pltpu.VMEM((tm, D_MODEL), F32)],
        compiler_params=_params("arbitrary", "arbitrary"),
    )(h3, w_up, w_up, w_g, w_v, b_g, b_v, w_dn, x2, target)


def _chunk_mask(rows_are_queries):
    a = lax.broadcasted_iota(jnp.int32, (TQ, TQ), 0) // CHUNK
    b = lax.broadcasted_iota(jnp.int32, (TQ, TQ), 1) // CHUNK
    return (b <= a) if rows_are_queries else (a <= b)


def _head_lanes(hh):
    return slice(hh * LANES, (hh + 1) * LANES)


def _to_row(col):
    return jnp.broadcast_to(col, (TQ, LANES)).T[0:1, :]


def _flash_specs(s, heads=HEADS_PER_STEP):
    width = heads * LANES
    tile = pl.BlockSpec((TQ, width), lambda h, i: (i, h))
    whole = pl.BlockSpec((s, width), lambda h, i: (0, h))
    row_tile = pl.BlockSpec((heads, 1, 1, TQ), lambda h, i: (h, i, 0, 0))
    row_whole = pl.BlockSpec((heads, s // TQ, 1, TQ), lambda h, i: (h, 0, 0, 0))
    return tile, whole, row_tile, row_whole


def _split_refs(refs, n_in, n_out, ex, n_scratch=0):
    e_in, e_out = (len(ex.inputs), len(ex.out_shape)) if ex else (0, 0)
    a, b, c = n_in + e_in, n_in + e_in + n_out, n_in + e_in + n_out + e_out
    return refs[:n_in], refs[a:b] + refs[c:c + n_scratch], (refs[n_in:a], refs[b:c], refs[c + n_scratch:])


def _hosted(ex, ex_refs, grid, when_first):
    if ex is None:
        return
    ids = [pl.program_id(d) for d in range(len(grid))]
    cond = functools.reduce(
        lambda p, q_: p & q_, [i == (0 if when_first else g - 1) for i, g in zip(ids, grid)])

    @pl.when(cond)
    def _():
        (ex.start if when_first else ex.finish)(*ex_refs)


def _host_call(body, ex, name, grid, in_specs, out_specs, out_shape, args, scratch=()):
    e_in, e_out = (len(ex.inputs), len(ex.out_shape)) if ex else (0, 0)
    res = _call(
        body, name=name, grid=grid, in_specs=list(in_specs) + [ANY] * e_in,
        out_specs=list(out_specs) + [ANY] * e_out,
        out_shape=list(out_shape) + (ex.out_shape if ex else []),
        scratch_shapes=list(scratch) + (ex.scratch if ex else []),
        compiler_params=_params(*["arbitrary"] * len(grid)),
    )(*args, *(ex.inputs if ex else []))
    return res[:len(out_shape)], res[len(out_shape):]


def _flash_fwd(q, k, v, name, ex=None):
    s = q.shape[0]
    nq = s // TQ
    heads = FWD_HEADS
    grid = (MLA_HEADS // heads, nq)

    def body(*refs):
        (q_ref, k_ref, v_ref), (o_ref, lse_row_ref), ex_refs = _split_refs(refs, 3, 2, ex)
        _hosted(ex, ex_refs, grid, True)
        i = pl.program_id(1)
        qs = [q_ref[:, _head_lanes(hh)] for hh in range(heads)]

        def scores(j, hh):
            kj = k_ref[pl.ds(pl.multiple_of(j * TQ, TQ), TQ), _head_lanes(hh)]
            return lax.dot_general(kj, qs[hh], NT_DIMS, preferred_element_type=F32)

        def update(j, sc, m_prev, l_prev, acc, hh):
            vt = v_ref[pl.ds(pl.multiple_of(j * TQ, TQ), TQ), _head_lanes(hh)].T
            m_new = jnp.maximum(m_prev, jnp.max(sc, axis=0, keepdims=True))
            alpha = jnp.exp2(m_prev - m_new)
            p = jnp.exp2(sc - m_new)
            l_new = alpha * l_prev + jnp.sum(p, axis=0, keepdims=True)
            acc = acc * alpha + jnp.dot(vt, p.astype(BF16), preferred_element_type=F32)
            return m_new, l_new, acc

        def step(j, carry):
            out = []
            for hh in range(heads):
                sc, m_prev, l_prev, acc = carry[hh]
                out.append((scores(j + 1, hh),) + update(j, sc, m_prev, l_prev, acc, hh))
            return tuple(out)

        init = tuple((scores(0, hh), jnp.full((1, TQ), NEG, F32), jnp.zeros((1, TQ), F32),
                      jnp.zeros((LANES, TQ), F32)) for hh in range(heads))
        carry = lax.fori_loop(0, i, step, init)
        for hh, (sc, m_prev, l_prev, acc) in enumerate(carry):
            sc = jnp.where(_chunk_mask(False), sc, NEG)
            m_fin, l_fin, acc = update(i, sc, m_prev, l_prev, acc, hh)
            o_ref[:, _head_lanes(hh)] = (acc / l_fin).T
            lse_row_ref[hh, 0] = m_fin + jnp.log2(l_fin)
        _hosted(ex, ex_refs, grid, False)

    tile, whole, row_tile, _ = _flash_specs(s, heads)
    return _host_call(
        body, ex, name, grid, [tile, whole, whole], [tile, row_tile],
        [jax.ShapeDtypeStruct((s, MLA_HEADS * LANES), F32),
         jax.ShapeDtypeStruct((MLA_HEADS, nq, 1, TQ), F32)], (q, k, v))


def _attn_delta(do, o, name):
    s = do.shape[0]

    def body(do_ref, o_ref, d_ref):
        for h in range(MLA_HEADS):
            prod = do_ref[:, _head_lanes(h)] * o_ref[:, _head_lanes(h)]
            d_ref[h, 0] = _to_row(jnp.sum(prod, axis=-1, keepdims=True))

    tile = pl.BlockSpec((TQ, MLA_HEADS * LANES), lambda i: (i, 0))
    return _call(
        body, name=name, grid=(s // TQ,), in_specs=[tile, tile],
        out_specs=pl.BlockSpec((MLA_HEADS, 1, 1, TQ), lambda i: (0, i, 0, 0)),
        out_shape=jax.ShapeDtypeStruct((MLA_HEADS, s // TQ, 1, TQ), F32),
        compiler_params=_params("parallel"),
    )(do, o)


def _flash_bwd(q, k, v, do, lse_row, delta_row, name, ex=None):
    s = q.shape[0]
    nq = s // TQ
    grid = (MLA_HEADS // HEADS_PER_STEP, nq)

    def body(*refs):
        ins, (dqt_ref, dk_ref, dv_ref), ex_refs = _split_refs(refs, 6, 3, ex)
        q_ref, k_ref, v_ref, do_ref, lse_row_ref, delta_row_ref = ins
        _hosted(ex, ex_refs, grid, True)
        j = pl.program_id(1)

        @pl.when(j == 0)
        def _():
            dqt_ref[...] = jnp.zeros_like(dqt_ref)

        kjs = [k_ref[:, _head_lanes(hh)] for hh in range(HEADS_PER_STEP)]
        vjs = [v_ref[:, _head_lanes(hh)] for hh in range(HEADS_PER_STEP)]
        kts = [kj.T for kj in kjs]

        def step(i, carry, masked):
            base = pl.multiple_of(i * TQ, TQ)
            out = []
            for hh in range(HEADS_PER_STEP):
                dk, dv = carry[hh]
                qi = q_ref[pl.ds(base, TQ), _head_lanes(hh)]
                dob = do_ref[pl.ds(base, TQ), _head_lanes(hh)].astype(BF16)
                sc_t = lax.dot_general(kjs[hh], qi, NT_DIMS, preferred_element_type=F32)
                if masked:
                    sc_t = jnp.where(_chunk_mask(False), sc_t, NEG)
                p_t = jnp.exp2(sc_t - lse_row_ref[hh, i])
                dv = dv + jnp.dot(p_t.astype(BF16), dob, preferred_element_type=F32)
                dp_t = lax.dot_general(vjs[hh], dob, NT_DIMS, preferred_element_type=F32)
                ds_t = (p_t * (dp_t - delta_row_ref[hh, i])).astype(BF16)
                dk = dk + jnp.dot(ds_t, qi, preferred_element_type=F32)
                dqt_ref[hh, i] += jnp.dot(kts[hh], ds_t, preferred_element_type=F32)
                out.append((dk, dv))
            return tuple(out)

        zero = jnp.zeros((TQ, LANES), F32)
        carry = step(j, tuple((zero, zero) for _ in range(HEADS_PER_STEP)), True)
        carry = lax.fori_loop(j + 1, nq, functools.partial(step, masked=False), carry)
        for hh, (dk, dv) in enumerate(carry):
            dk_ref[:, _head_lanes(hh)] = dk
            dv_ref[:, _head_lanes(hh)] = dv.astype(dv_ref.dtype)
        _hosted(ex, ex_refs, grid, False)

    tile, whole, _, row_whole = _flash_specs(s)
    dqt_spec = pl.BlockSpec((HEADS_PER_STEP, nq, LANES, TQ), lambda h, j: (h, 0, 0, 0))
    wide = lambda dt: jax.ShapeDtypeStruct((s, MLA_HEADS * LANES), dt)
    return _host_call(
        body, ex, name, grid, [whole, tile, tile, whole, row_whole, row_whole], [dqt_spec, tile, tile],
        [jax.ShapeDtypeStruct((MLA_HEADS, nq, LANES, TQ), F32), wide(F32), wide(BF16)],
        (q, k, v, do, lse_row, delta_row))


def _side_by_side(g):
    return g.transpose(1, 0, 2).reshape(g.shape[1], N_DEV * g.shape[2])


def _col_shards(g):
    return g.reshape(g.shape[0], N_DEV, g.shape[1] // N_DEV).transpose(1, 0, 2)


def _pad_last(v, to):
    return jnp.pad(v, [(0, 0)] * (v.ndim - 1) + [(0, to - v.shape[-1])])


def _tiny_rows(v, rows):
    flat = v.reshape(v.shape[:-2] + (-1,))
    return _pad_last(flat, rows * LANES).reshape(v.shape[:-2] + (rows, LANES))


def _pack_small(vals):
    parts = []
    for (n, size), pad in zip(SMALL, SMALL_PAD):
        parts.append(jnp.pad(vals[n].reshape(-1), (0, pad - size)))
    flat = jnp.concatenate(parts)
    return jnp.pad(flat, (0, SMALL_ROWS * LANES - flat.shape[0])).reshape(SMALL_ROWS, LANES)


def _unpack_small(packed):
    flat = packed.reshape(-1)
    out, off = {}, 0
    for (n, size), pad in zip(SMALL, SMALL_PAD):
        out[n] = flat[off:off + size].reshape(1, size)
        off += pad
    return out


def _pad_heads(w, per_head, axis):
    shape = list(w.shape)
    shape[axis:axis + 1] = [MLA_HEADS, per_head]
    w = w.reshape(shape)
    pad = [(0, 0)] * len(shape)
    pad[axis + 1] = (0, LANES - per_head)
    w = jnp.pad(w, pad)
    shape[axis:axis + 2] = [MLA_HEADS * LANES]
    return w.reshape(shape)


def _unpad_heads(w, per_head, axis):
    shape = list(w.shape)
    shape[axis:axis + 1] = [MLA_HEADS, LANES]
    w = w.reshape(shape)
    w = lax.slice_in_dim(w, 0, per_head, axis=axis + 1)
    shape[axis:axis + 2] = [MLA_HEADS * per_head]
    return w.reshape(shape)


def _row(v, pad_to=None):
    v = v.reshape(1, -1)
    if pad_to is not None:
        v = jnp.pad(v, ((0, 0), (0, pad_to - v.shape[1])))
    return v


def kernel(x, mem, positions, mix_norm_g, w_in, b_conv_in, w_conv_dw, b_conv_dw, conv_ln_g, conv_ln_b, q_lat_norm_g, w_uq, kv_lat_norm_g, w_ukv, q_norm_g, k_norm_g, w_out, mem_norm_x_g, mem_norm_m_g, w_mem_q, w_mem_kv, mem_q_norm_g, mem_k_norm_g, w_mem_o, ffn_norm_g, w_up, w_ffn_dw, b_ffn_dw, w_down, loss_target, m_mix_norm_g, m_w_in, m_b_conv_in, m_w_conv_dw, m_b_conv_dw, m_conv_ln_g, m_conv_ln_b, m_q_lat_norm_g, m_w_uq, m_kv_lat_norm_g, m_w_ukv, m_q_norm_g, m_k_norm_g, m_w_out, m_mem_norm_x_g, m_mem_norm_m_g, m_w_mem_q, m_w_mem_kv, m_mem_q_norm_g, m_mem_k_norm_g, m_w_mem_o, m_ffn_norm_g, m_w_up, m_w_ffn_dw, m_b_ffn_dw, m_w_down, v_mix_norm_g, v_w_in, v_b_conv_in, v_w_conv_dw, v_b_conv_dw, v_conv_ln_g, v_conv_ln_b, v_q_lat_norm_g, v_w_uq, v_kv_lat_norm_g, v_w_ukv, v_q_norm_g, v_k_norm_g, v_w_out, v_mem_norm_x_g, v_mem_norm_m_g, v_w_mem_q, v_w_mem_kv, v_mem_q_norm_g, v_mem_k_norm_g, v_w_mem_o, v_ffn_norm_g, v_w_up, v_w_ffn_dw, v_b_ffn_dw, v_w_down):
    a = dict(locals())
    seq = x.shape[1]
    xs = x.reshape(seq, D_MODEL)
    mems = mem.reshape(-1, D_MODEL)
    target = loss_target.reshape(seq, D_MODEL)

    tiny = [n for n, _, _ in TINY]
    shard = lambda n: a[n][0] if n in tiny else a[n][0].astype(BF16)
    pos = jnp.stack([2 * lax.axis_index("x") + lax.axis_index("y"), lax.axis_index("c")]).astype(jnp.int32)
    ag_first = ["w_in", "w_uq", "w_ukv", "w_conv_dw"]
    ag_later = [n for n in [b for b, _ in BIG] + tiny if n not in ag_first]
    (h1,), first = _rowwise(_f_rms, [xs], [_row(mix_norm_g)], [(D_MODEL, BF16)], [], name="rms_mix",
                            ex=_plan_all_gather([shard(n) for n in ag_first]))
    wg = dict(zip(ag_first, first))
    wi = _side_by_side(wg["w_in"])
    s3 = 2 * CONV_CH + MLA_Q_RANK + MLA_KV_RANK
    w_in_p = jnp.concatenate([
        wi[:, :s3], jnp.zeros((D_MODEL, MLA_NOPE), BF16), wi[:, s3:],
        jnp.zeros((D_MODEL, LANES - MLA_QK), BF16)], axis=1)
    w_uq_p = _side_by_side(_pad_last(wg["w_uq"], LANES))
    w_uk_p = _side_by_side(_pad_last(wg["w_ukv"][:, :, :MLA_NOPE], LANES))
    w_uv_p = _side_by_side(_pad_last(wg["w_ukv"][:, :, MLA_NOPE:], LANES))
    w_cdw = _side_by_side(wg["w_conv_dw"])

    g_mix, g_qlat, g_kvlat = _row(mix_norm_g), _row(q_lat_norm_g), _row(kv_lat_norm_g)
    b_in = _row(b_conv_in)
    b_in_a, b_in_g = b_in[:, :CONV_CH], b_in[:, CONV_CH:]
    b_cdw, ln_g, ln_b = _row(b_conv_dw), _row(conv_ln_g), _row(conv_ln_b)
    g_q, g_k = _row(q_norm_g, LANES), _row(k_norm_g, LANES)
    g_memx, g_memm = _row(mem_norm_x_g), _row(mem_norm_m_g)
    g_mq, g_mk, g_ffn = _row(mem_q_norm_g), _row(mem_k_norm_g), _row(ffn_norm_g)
    b_f = _pad_last(b_ffn_dw.reshape(N_DEV, FF_SHARD), FF_PAD)
    b_f_g, b_f_v = b_f[:4].reshape(1, D_FF_PAD), b_f[4:].reshape(1, D_FF_PAD)

    freq = ROPE_THETA ** (-jnp.arange(0, MLA_ROPE, 2, dtype=F32) / MLA_ROPE)
    inv_freq = jnp.concatenate([jnp.zeros((MLA_NOPE,), F32), freq, freq,
                                jnp.zeros((LANES - MLA_QK,), F32)]).reshape(1, LANES)
    cos, sin_a, sin_b = _rowwise(_f_rope_tab, [positions.reshape(seq, 1)], [inv_freq],
                                 [(LANES, F32)] * 3, [], name="rope_tables")

    z = _mm(h1, w_in_p, name="mm_in")
    z_rows = [(z, CONV_CH, 0), (z, CONV_CH, 1), (z, MLA_Q_RANK, 4), (z, MLA_KV_RANK, 10)]
    z_kr = (z, LANES, 11)
    u0, cqn, ckvn = _rowwise(
        _mix_pre, z_rows, [b_in_a, b_in_g, g_qlat, g_kvlat],
        [(CONV_CH, F32), (MLA_Q_RANK, BF16), (MLA_KV_RANK, BF16)], [], name="mix_pre")
    c1 = _conv_fwd(u0, w_cdw, "conv31_fwd")
    (u,) = _rowwise(lambda c, b, g, bb: (_ln_silu(c, b, g, bb),), [c1], [b_cdw, ln_g, ln_b],
                    [(CONV_CH, BF16)], [], name="ln_silu")
    q0 = _mm(cqn, w_uq_p, name="mm_uq")
    kn0 = _mm(ckvn, w_uk_p, name="mm_uk")
    v0 = _mm(ckvn, w_uv_p, out_dtype=BF16, name="mm_uv")
    qk_rows = [q0, kn0, z_kr, cos, sin_a, sin_b]
    qh, kh = _rowwise(_f_qk_prep, qk_rows, [g_q, g_k],
                      [(MLA_HEADS * LANES, BF16)] * 2, [], name="qk_prep")
    (attn, lse_row), later = _flash_fwd(
        qh, kh, v0, "flash_fwd", _plan_all_gather([shard(n) for n in ag_later]))
    wg.update(zip(ag_later, later))
    w_out = wg["w_out"].reshape(D_MODEL, D_MODEL)
    w_out_u = w_out[:CONV_CH]
    w_out_a = _pad_heads(w_out[CONV_CH:], MLA_V, 0)
    w_mq, w_mo = wg["w_mem_q"].reshape(D_MODEL, D_MODEL), wg["w_mem_o"].reshape(D_MODEL, D_MODEL)
    w_mkv = _side_by_side(wg["w_mem_kv"])
    w_up_p = _pad_last(wg["w_up"], FF_PAD)
    w_dn = jnp.pad(wg["w_down"].reshape(4, FF_SHARD, D_MODEL),
                   ((0, 0), (0, FF_PAD - FF_SHARD), (0, 0))).reshape(D_FF_PAD, D_MODEL)
    w_fdw = _pad_last(wg["w_ffn_dw"], FF_PAD)
    w_fdw_g = w_fdw[:4].transpose(1, 0, 2).reshape(FFN_CONV_WIDTH, D_FF_PAD)
    w_fdw_v = w_fdw[4:].transpose(1, 0, 2).reshape(FFN_CONV_WIDTH, D_FF_PAD)
    x1 = _mm(u, w_out_u, add=xs, name="mm_out_u")
    x1 = _mm(attn, w_out_a, add=x1, name="mm_out_a")

    (hq,) = _rowwise(_f_rms, [x1], [g_memx], [(D_MODEL, BF16)], [], name="rms_memx")
    (hm,) = _rowwise(_f_rms, [mems], [g_memm], [(D_MODEL, BF16)], [], name="rms_memm", tm=mems.shape[0])
    qm0 = _mm(hq, w_mq, name="mm_memq")
    kvm0 = _mm(hm, w_mkv, name="mm_memkv", tm=mems.shape[0])
    (km,) = _rowwise(_f_mem_k, [(kvm0, D_MODEL, 0)], [g_mk], [(D_MODEL, BF16)], [],
                     name="mem_k", tm=mems.shape[0])
    vm = kvm0[:, D_MODEL:]
    (om,) = _rowwise(_f_mem_attn, [qm0], [km, vm, g_mq], [(D_MODEL, BF16)], [], name="mem_attn")
    x2 = _mm(om, w_mo, add=x1, name="mm_memo")

    (h3,) = _rowwise(_f_rms, [x2], [g_ffn], [(D_MODEL, BF16)], [], name="rms_ffn")
    up_g, up_v, act, dy, sq = _ffn_fwd(h3, w_up_p, w_fdw_g, w_fdw_v, b_f_g, b_f_v, w_dn, x2, target, "ffn_fwd")
    loss_share = jnp.full((8, LANES), 0.5 * jnp.sum(sq) / D_MODEL, F32)

    gw, gs, gt = {}, {}, {}
    gw_dn = _mm_tn(act, dy, name="tn_down").reshape(4, FF_PAD, D_MODEL)
    gw["w_down"] = gw_dn[:, :FF_SHARD].reshape(N_DEV, FF_SHARD // 2, D_MODEL)
    dup_g, dup_v, dwf_g, dwf_v, db_g, db_v = _ffn_mid_bwd(
        up_g, up_v, dy, w_dn, w_fdw_g, w_fdw_v, b_f_g, b_f_v, "ffn_mid_bwd")
    db_f = jnp.concatenate([db_g.reshape(4, FF_PAD), db_v.reshape(4, FF_PAD)], axis=0)
    gs["b_ffn_dw"] = db_f[:, :FF_SHARD].reshape(1, 2 * D_FF)
    dwf = jnp.concatenate([dwf_g.reshape(FFN_CONV_WIDTH, 4, FF_PAD), dwf_v.reshape(FFN_CONV_WIDTH, 4, FF_PAD)], axis=1)
    gt["w_ffn_dw"] = dwf[:, :, :FF_SHARD].transpose(1, 0, 2)
    gw_up = jnp.concatenate([_mm_tn(h3, dup_g, shard_cols=FF_PAD, name="tn_up_g"),
                             _mm_tn(h3, dup_v, shard_cols=FF_PAD, name="tn_up_v")], axis=0)
    gw["w_up"] = gw_up[:, :, :FF_SHARD]
    w_up_flat = _side_by_side(w_up_p)
    dh3 = _mm(dup_g, w_up_flat, trans_b=True, b_kblock=0, name="mm_up_g_t")
    dx2, gs["ffn_norm_g"] = _mm(dup_v, w_up_flat, trans_b=True, b_kblock=1, add=dh3, tm=TM,
                                rms_bwd=(x2, dy, g_ffn), name="mm_up_v_t")

    gw["w_mem_o"] = _mm_tn(om, dx2, name="tn_memo").reshape(N_DEV, -1, D_MODEL)
    dom = _mm(dx2, w_mo, trans_b=True, out_dtype=BF16, name="mm_memo_t")
    n_mem = mems.shape[0]
    dqm0, dkm, dvm, gs["mem_q_norm_g"] = _rowwise(
        _b_mem_attn, [dom, qm0], [km, vm, g_mq], [(D_MODEL, BF16)],
        [(n_mem, D_MODEL), (n_mem, D_MODEL), (1, MEM_HEAD_DIM)], name="mem_attn_bwd")
    gw["w_mem_q"] = _mm_tn(hq, dqm0, name="tn_memq").reshape(N_DEV, -1, D_MODEL)
    dx1, gs["mem_norm_x_g"] = _mm(dqm0, w_mq, trans_b=True, rms_bwd=(x1, dx2, g_memx), name="mm_memq_t")
    dkk, gs["mem_k_norm_g"] = _rowwise(_b_mem_k, [(kvm0, D_MODEL, 0), dkm], [g_mk],
                                       [(D_MODEL, F32)], [(1, MEM_HEAD_DIM)], name="mem_k_bwd", tm=n_mem)
    dkvm0 = jnp.concatenate([dkk, dvm], axis=1)
    gw["w_mem_kv"] = _col_shards(_mm_tn(hm, dkvm0, name="tn_memkv", ts=n_mem))
    dhm = _mm(dkvm0, w_mkv, trans_b=True, name="mm_memkv_t", tm=n_mem)
    _, gs["mem_norm_m_g"] = _rowwise(_b_rms_nores, [mems, dhm], [g_memm], [(D_MODEL, F32)],
                                     [(1, D_MODEL)], name="rms_memm_bwd", tm=n_mem)

    gw_out_u = _mm_tn(u, dx1, name="tn_out_u")
    gw_out_a = _mm_tn(attn, dx1, name="tn_out_a")
    gw["w_out"] = jnp.concatenate([gw_out_u, _unpad_heads(gw_out_a, MLA_V, 0)], axis=0).reshape(N_DEV, -1, D_MODEL)
    du = _mm(dx1, w_out_u, trans_b=True, name="mm_out_u_t")
    dattn = _mm(dx1, w_out_a, trans_b=True, name="mm_out_a_t")
    dc1, gs["b_conv_dw"], gs["conv_ln_g"], gs["conv_ln_b"] = _rowwise(
        _b_ln_silu, [c1, du], [b_cdw, ln_g, ln_b], [(CONV_CH, F32)], [(1, CONV_CH)] * 3, name="ln_silu_bwd")
    rs_first = ["w_up", "w_down", "w_mem_o", "w_mem_q", "w_mem_kv", "w_out"]
    grads = [gw[n] for n in rs_first]
    (du0, g_cdw), gots = _conv_bwd(dc1, u0, w_cdw, "conv31_bwd", _plan_swap_sibling(grads))
    gt["w_conv_dw"] = _col_shards(g_cdw)
    sums_first = _rs_add(grads, gots, pos, "rs_add_first")
    delta_row = _attn_delta(dattn, attn, "attn_delta")
    (dqt, dkh, dv0), recvs_first = _flash_bwd(
        qh, kh, v0, dattn, lse_row, delta_row, "flash_bwd", _plan_swap_chips(sums_first))
    dqt_row = (dqt, pl.BlockSpec((MLA_HEADS, 1, LANES, TQ), lambda i: (0, i, 0, 0)))
    dq0, dkn0, dkr, dgq, dgk = _rowwise(
        _b_qk_prep, qk_rows + [dqt_row, dkh], [g_q, g_k],
        [(MLA_HEADS * LANES, BF16)] * 2 + [(LANES, F32)], [(1, LANES)] * 2, name="qk_prep_bwd", tm=TQ)
    gs["q_norm_g"], gs["k_norm_g"] = dgq[:, :MLA_QK], dgk[:, :MLA_QK]
    gw["w_uq"] = _col_shards(_mm_tn(cqn, dq0, name="tn_uq"))[:, :, :MLA_QK]
    g_uk = _col_shards(_mm_tn(ckvn, dkn0, name="tn_uk"))[:, :, :MLA_NOPE]
    g_uv = _col_shards(_mm_tn(ckvn, dv0, name="tn_uv"))[:, :, :MLA_V]
    gw["w_ukv"] = jnp.concatenate([g_uk, g_uv], axis=2)
    dcqn = _mm(dq0, w_uq_p, trans_b=True, name="mm_uq_t")
    dckvn = _mm(dkn0, w_uk_p, trans_b=True, name="mm_uk_t")
    dckvn = _mm(dv0, w_uv_p, trans_b=True, add=dckvn, name="mm_uv_t")
    dz, dba, dbg, gs["q_lat_norm_g"], gs["kv_lat_norm_g"] = _rowwise(
        _b_mix_pre, z_rows + [du0, dcqn, dckvn, dkr], [b_in_a, b_in_g, g_qlat, g_kvlat],
        [(IN_PAD, BF16)], [(1, CONV_CH)] * 2 + [(1, MLA_Q_RANK), (1, MLA_KV_RANK)], name="mix_pre_bwd")
    gs["b_conv_in"] = jnp.concatenate([dba, dbg], axis=1)
    gw_in = _mm_tn(h1, dz, name="tn_in")
    gw["w_in"] = _col_shards(jnp.concatenate([gw_in[:, :s3], gw_in[:, s3 + MLA_NOPE:s3 + MLA_QK]], axis=1))
    rs_last = [n for n, _ in BIG if n not in rs_first]
    grads = [gw[n] for n in rs_last]
    gots = _run_exchange(_plan_swap_sibling(grads), "rs_sibling_last")
    sums_last = _rs_add(grads, gots, pos, "rs_add_last")
    (dx, gs["mix_norm_g"]), recvs_last = _mm(dz, w_in_p, trans_b=True, rms_bwd=(xs, dx1, g_mix),
                                             name="mm_in_t", ex=_plan_swap_chips(sums_last))
    big = rs_first + rs_last
    flat = _adamw_big(list(sums_first) + list(sums_last), list(recvs_first) + list(recvs_last),
                      [a[n] for n in big], [a["m_" + n] for n in big], [a["v_" + n] for n in big],
                      pos, "adamw_big")
    res = [{n: flat[4 * i + k] for i, n in enumerate(big)} for k in range(4)]

    part = jnp.concatenate(
        [_pack_small(gs)] + [_tiny_rows(gt[n], rows).reshape(N_DEV * rows, LANES) for n, _, rows in TINY]
        + [loss_share], axis=0)
    (parts,) = _all_gather([part], "ag_small_grads")
    small_in = [_pack_small({n: a[p + n] for n, _ in SMALL}) for p in ("", "m_", "v_")]
    tiny_in = [[_tiny_rows(a[p + n][0], rows) for p in ("", "m_", "v_")] for n, _, rows in TINY]
    flat = _adamw_small(parts, small_in, tiny_in, "adamw_small")
    for k in range(4):
        res[k].update(_unpack_small(flat[k]))
        for i, (n, shape, _) in enumerate(TINY):
            res[k][n] = flat[4 * (i + 1) + k].reshape(-1)[:math.prod(shape)].reshape((1,) + shape)

    return (flat[-1][0, 0], dx.reshape(1, seq, D_MODEL), *[res[k][n] for k in range(4) for n in WEIGHTS])
```

```python
import functools
import math

import jax
import jax.numpy as jnp
from jax import lax
from jax.experimental import pallas as pl
from jax.experimental.pallas import tpu as pltpu

F32 = jnp.float32
BF16 = jnp.bfloat16
EPS = 1e-6
LANES = 128
N_DEV = 8
D_MODEL = 1024
CONV_CH = 512
MLA_HEADS = 8
MLA_NOPE = 64
MLA_ROPE = 32
MLA_V = 64
MLA_QK = MLA_NOPE + MLA_ROPE
MLA_Q_RANK = 256
MLA_KV_RANK = 128
ROPE_THETA = 10000.0
IN_PAD = 2 * CONV_CH + MLA_Q_RANK + MLA_KV_RANK + LANES
MEM_HEAD_DIM = 256
D_FF = 2816
FFN_CONV_WIDTH = 3
CHUNK = 64
ATT_SCALE = 1.0 / math.sqrt(MLA_QK)
LN2 = math.log(2.0)
Q_SCALE = ATT_SCALE / LN2
MEM_SCALE = 1.0 / math.sqrt(MEM_HEAD_DIM)
ADAM_LR, ADAM_B1, ADAM_B2, ADAM_EPS, ADAM_WD, ADAM_STEP = 0.001, 0.9, 0.999, 1e-08, 0.01, 10

TM = 512
ROW_TILE = 1024
MM_ROWS = 1024
TN_ROWS = 2048
TQ = 512
HEADS_PER_STEP = 2
FWD_HEADS = 4
CONV_ROWS = 512
NEG = -1e30
VMEM_LIMIT = 56 * 1024 * 1024

MESH = pl.DeviceIdType.MESH
ANY = pl.BlockSpec(memory_space=pl.ANY)
NT_DIMS = (((1,), (1,)), ((), ()))

BIG = [
    ("w_in", (1024, 180)), ("w_uq", (256, 96)), ("w_ukv", (128, 128)), ("w_out", (128, 1024)),
    ("w_mem_q", (128, 1024)), ("w_mem_kv", (1024, 256)), ("w_mem_o", (128, 1024)),
    ("w_up", (1024, 704)), ("w_down", (352, 1024)),
]
TINY = [("w_conv_dw", (31, 64), 16), ("w_ffn_dw", (3, 704), 24)]
ROW_STEPS = 4
FF_SHARD = D_FF // 4
FF_PAD = 768
D_FF_PAD = 4 * FF_PAD
SMALL = [
    ("mix_norm_g", 1024), ("b_conv_in", 1024), ("b_conv_dw", 512), ("conv_ln_g", 512),
    ("conv_ln_b", 512), ("q_lat_norm_g", 256), ("kv_lat_norm_g", 128), ("q_norm_g", 96),
    ("k_norm_g", 96), ("mem_norm_x_g", 1024), ("mem_norm_m_g", 1024), ("mem_q_norm_g", 256),
    ("mem_k_norm_g", 256), ("ffn_norm_g", 1024), ("b_ffn_dw", 5632),
]
WEIGHTS = [
    "mix_norm_g", "w_in", "b_conv_in", "w_conv_dw", "b_conv_dw", "conv_ln_g", "conv_ln_b",
    "q_lat_norm_g", "w_uq", "kv_lat_norm_g", "w_ukv", "q_norm_g", "k_norm_g", "w_out",
    "mem_norm_x_g", "mem_norm_m_g", "w_mem_q", "w_mem_kv", "mem_q_norm_g", "mem_k_norm_g",
    "w_mem_o", "ffn_norm_g", "w_up", "w_ffn_dw", "b_ffn_dw", "w_down",
]


SMALL_PAD = [(-(-n // LANES)) * LANES for _, n in SMALL]
SMALL_ROWS = -(-sum(SMALL_PAD) // (8 * LANES)) * 8
TINY_BASE = [SMALL_ROWS + N_DEV * sum(r for _, _, r in TINY[:i]) for i in range(len(TINY))]
PART_ROWS = SMALL_ROWS + N_DEV * sum(r for _, _, r in TINY)


def _call(body, **kw):
    return pl.pallas_call(body, **kw)


def _params(*sem):
    return pltpu.CompilerParams(dimension_semantics=sem, vmem_limit_bytes=VMEM_LIMIT)


class _Exchange:
    def __init__(self, inputs, out_shape, scratch, start, finish):
        self.inputs, self.out_shape, self.scratch = list(inputs), list(out_shape), list(scratch)
        self.start, self.finish = start, finish


def _run_exchange(ex, name):
    n_in, n_out = len(ex.inputs), len(ex.out_shape)

    def body(*refs):
        parts = refs[:n_in], refs[n_in:n_in + n_out], refs[n_in + n_out:]
        ex.start(*parts)
        ex.finish(*parts)

    return _call(body, name=name, out_shape=ex.out_shape, in_specs=[ANY] * n_in,
                 out_specs=[ANY] * n_out, scratch_shapes=ex.scratch)(*ex.inputs)


def _plan_all_gather(xs):
    n = len(xs)

    def copies(x_refs, out_refs, sems):
        send_sems, recv_sems, local_sems = sems
        x, y, c = lax.axis_index("x"), lax.axis_index("y"), lax.axis_index("c")
        me, sibling = (x, y, c), (x, y, 1 - c)
        chips = [(1 - x, y), (x, 1 - y), (1 - x, 1 - y)]

        def slot(o, px, py, pc):
            return out_refs[o].at[4 * px + 2 * py + pc]

        def copy(o, k, block, to, src=None):
            return pltpu.make_async_remote_copy(
                src_ref=slot(o, *block) if src is None else src, dst_ref=slot(o, *block),
                send_sem=send_sems.at[o, k], recv_sem=recv_sems.at[o, k],
                device_id=to, device_id_type=MESH)

        mine = [pltpu.make_async_copy(x_refs[o], slot(o, *me), local_sems.at[o]) for o in range(n)]
        first = [copy(o, 0, me, sibling, src=x_refs[o]) for o in range(n)]
        first += [copy(o, 1 + j, me, (*chip, c), src=x_refs[o])
                  for j, chip in enumerate(chips) for o in range(n)]
        return me, sibling, chips, copy, mine, first

    def start(x_refs, out_refs, sems):
        _, _, _, _, mine, first = copies(x_refs, out_refs, sems)
        for cp in mine + first:
            cp.start()

    def finish(x_refs, out_refs, sems):
        me, sibling, chips, copy, mine, first = copies(x_refs, out_refs, sems)
        c = me[2]
        passed = []
        for j, chip in enumerate(chips):
            for o in range(n):
                copy(o, 1 + j, (*chip, c), me).wait_recv()
                passed.append(copy(o, 4 + j, (*chip, c), sibling))
                passed[-1].start()
        for o in range(n):
            copy(o, 0, sibling, me).wait_recv()
        for j, chip in enumerate(chips):
            for o in range(n):
                copy(o, 4 + j, (*chip, 1 - c), me).wait_recv()
        for cp in first + passed:
            cp.wait_send()
        for cp in mine:
            cp.wait()

    return _Exchange(
        xs, [jax.ShapeDtypeStruct((N_DEV,) + v.shape, v.dtype) for v in xs],
        [pltpu.SemaphoreType.DMA((n, 7)), pltpu.SemaphoreType.DMA((n, 7)), pltpu.SemaphoreType.DMA((n,))],
        start, finish)


def _all_gather(xs, name):
    return _run_exchange(_plan_all_gather(xs), name)


def _plan_swap_sibling(grads):
    n = len(grads)

    def copies(g_refs, got_refs, sems):
        send_sems, recv_sems = sems
        x, y, c = lax.axis_index("x"), lax.axis_index("y"), lax.axis_index("c")
        return [
            pltpu.make_async_remote_copy(
                src_ref=g_refs[o].at[2 * chip + 1 - c], dst_ref=got_refs[o].at[chip],
                send_sem=send_sems.at[o, chip], recv_sem=recv_sems.at[o, chip],
                device_id=(x, y, 1 - c), device_id_type=MESH)
            for o in range(n) for chip in range(4)]

    def start(g_refs, got_refs, sems):
        for cp in copies(g_refs, got_refs, sems):
            cp.start()

    def finish(g_refs, got_refs, sems):
        for cp in copies(g_refs, got_refs, sems):
            cp.wait()

    return _Exchange(
        grads, [jax.ShapeDtypeStruct((4,) + g.shape[1:], g.dtype) for g in grads],
        [pltpu.SemaphoreType.DMA((n, 4)), pltpu.SemaphoreType.DMA((n, 4))], start, finish)


def _plan_swap_chips(sums):
    n = len(sums)

    def copies(a_refs, r_refs, sems):
        send_sems, recv_sems = sems
        x, y, c = lax.axis_index("x"), lax.axis_index("y"), lax.axis_index("c")
        peers = [(x, 1 - y), (1 - x, y), (1 - x, 1 - y)]
        return [
            pltpu.make_async_remote_copy(
                src_ref=a_refs[o].at[2 * px + py], dst_ref=r_refs[o].at[k],
                send_sem=send_sems.at[o, k], recv_sem=recv_sems.at[o, k],
                device_id=(px, py, c), device_id_type=MESH)
            for k, (px, py) in enumerate(peers) for o in range(n)]

    def start(a_refs, r_refs, sems):
        for cp in copies(a_refs, r_refs, sems):
            cp.start()

    def finish(a_refs, r_refs, sems):
        for cp in copies(a_refs, r_refs, sems):
            cp.wait()

    return _Exchange(
        sums, [jax.ShapeDtypeStruct((3,) + a.shape[1:], a.dtype) for a in sums],
        [pltpu.SemaphoreType.DMA((n, 3)), pltpu.SemaphoreType.DMA((n, 3))], start, finish)


def _shard_block(shape):
    return (None, shape[-2] // ROW_STEPS, shape[-1])


def _rs_add(grads, gots, pos, name):
    n = len(grads)

    def body(pos_ref, *refs):
        for g_ref, t_ref, o_ref in zip(refs[:n], refs[n:2 * n], refs[2 * n:]):
            o_ref[...] = g_ref[...] + t_ref[...]

    in_specs = [pl.BlockSpec(_shard_block(g.shape), lambda a, t, pos: (2 * a + pos[1], t, 0)) for g in grads]
    in_specs += [pl.BlockSpec(_shard_block(g.shape), lambda a, t, pos: (a, t, 0)) for g in gots]
    return _call(
        body, name=name,
        grid_spec=pltpu.PrefetchScalarGridSpec(
            num_scalar_prefetch=1, grid=(4, ROW_STEPS), in_specs=in_specs,
            out_specs=[pl.BlockSpec(_shard_block(g.shape), lambda a, t, pos: (a, t, 0)) for g in gots]),
        out_shape=[jax.ShapeDtypeStruct(g.shape, g.dtype) for g in gots],
        compiler_params=_params("arbitrary", "arbitrary"),
    )(pos, *grads, *gots)


def _adamw_big(sums, recvs, ws, ms, vs, pos, name):
    n = len(sums)

    def body(pos_ref, *refs):
        ins, outs = refs[:7 * n], refs[7 * n:]
        for o in range(n):
            own, r1, r2, r3, w, m, v = [r[...] for r in ins[7 * o:7 * o + 7]]
            g = ((own + r1) + r2) + r3
            for ref, val in zip(outs[4 * o:4 * o + 4], (g,) + _adamw(w, g, m, v)):
                ref[...] = val

    in_specs, args, out_specs, out_shape = [], [], [], []
    for s_, r_, w_, m_, v_ in zip(sums, recvs, ws, ms, vs):
        blk = _shard_block(w_.shape)
        in_specs.append(pl.BlockSpec(blk, lambda t, pos: (pos[0], t, 0)))
        in_specs += [pl.BlockSpec(blk, lambda t, pos, k=k: (k, t, 0)) for k in range(3)]
        in_specs += [pl.BlockSpec(blk, lambda t, pos: (0, t, 0))] * 3
        args += [s_, r_, r_, r_, w_, m_, v_]
        out_specs += [pl.BlockSpec(blk, lambda t, pos: (0, t, 0))] * 4
        out_shape += [jax.ShapeDtypeStruct(w_.shape, F32)] * 4
    return _call(
        body, name=name,
        grid_spec=pltpu.PrefetchScalarGridSpec(
            num_scalar_prefetch=1, grid=(ROW_STEPS,), in_specs=in_specs, out_specs=out_specs),
        out_shape=out_shape, compiler_params=_params("arbitrary"),
    )(pos, *args)


def _tile(n, prefs):
    for t in prefs:
        if n % t == 0:
            return t
    return n


def _mm(a, b, *, name, trans_b=False, add=None, out_dtype=F32, tm=MM_ROWS, b_kblock=0, ex=None,
        rms_bwd=None):
    m, k = a.shape
    n = b.shape[0] if trans_b else b.shape[1]
    tn = _tile(n, (1536, 1408, 1024, 768, 512, 256, 128))
    tk = _tile(k, (3072, 2048, 1536, 1408, 1024, 768, 512, 256, 128))
    nk = k // tk
    has_add = add is not None
    n_post = 3 if rms_bwd else 0
    assert m % tm == 0 and (not rms_bwd or tn == n), (m, tm, n, tn)
    grid = (m // tm, n // tn, nk)

    def body(*refs):
        ins, outs, ex_refs = _split_refs(refs, 2 + has_add + n_post, 1 + bool(rms_bwd), ex, int(nk > 1))
        _hosted(ex, ex_refs, grid, True)
        a_ref, b_ref = ins[0], ins[1]
        add_ref = ins[2] if has_add else None
        o_ref = outs[0]
        av = a_ref[...].astype(BF16)
        bv = b_ref[...].astype(BF16)
        if trans_b:
            part = lax.dot_general(av, bv, NT_DIMS, preferred_element_type=F32)
        else:
            part = jnp.dot(av, bv, preferred_element_type=F32)

        def finish(acc):
            if has_add:
                acc = acc + add_ref[...].astype(F32)
            if rms_bwd:
                x_ref, dres_ref, g_ref = ins[2 + has_add:]
                acc, dg = _b_rms(x_ref[...], acc, dres_ref[...], g_ref[...])
                first_rows = pl.program_id(0) == 0

                @pl.when(first_rows)
                def _():
                    outs[1][...] = dg

                @pl.when(jnp.logical_not(first_rows))
                def _():
                    outs[1][...] += dg
            o_ref[...] = acc.astype(o_ref.dtype)

        if nk == 1:
            finish(part)
        else:
            acc_ref = outs[-1]
            kk = pl.program_id(2)

            @pl.when(kk == 0)
            def _():
                acc_ref[...] = part

            @pl.when(kk > 0)
            def _():
                acc_ref[...] += part

            @pl.when(kk == nk - 1)
            def _():
                finish(acc_ref[...])
        _hosted(ex, ex_refs, grid, False)

    in_specs = [pl.BlockSpec((tm, tk), lambda i, j, kk: (i, kk))]
    if trans_b:
        in_specs.append(pl.BlockSpec((tn, tk), lambda i, j, kk: (j, kk + b_kblock * nk)))
    else:
        in_specs.append(pl.BlockSpec((tk, tn), lambda i, j, kk: (kk, j)))
    args = [a, b]
    tile = pl.BlockSpec((tm, tn), lambda i, j, kk: (i, j))
    out_specs, out_shape = [tile], [jax.ShapeDtypeStruct((m, n), out_dtype)]
    if has_add:
        in_specs.append(tile)
        args.append(add)
    if rms_bwd:
        in_specs += [tile, tile, pl.BlockSpec((1, n), lambda i, j, kk: (0, 0))]
        args += list(rms_bwd)
        out_specs.append(pl.BlockSpec((1, n), lambda i, j, kk: (0, 0)))
        out_shape.append(jax.ShapeDtypeStruct((1, n), F32))
    outs, hosted = _host_call(
        body, ex, name, grid, in_specs, out_specs, out_shape, args,
        scratch=[pltpu.VMEM((tm, tn), F32)] if nk > 1 else [])
    res = tuple(outs) if rms_bwd else outs[0]
    return (res, hosted) if ex else res


def _mm_tn(a, b, *, name, ts=TN_ROWS, shard_cols=None):
    s, m = a.shape
    n = b.shape[1]
    assert s % ts == 0, (s, ts)
    tm = _tile(m, (1408, 1024, 768, 512, 256, 128))
    tn = shard_cols or _tile(n, (1536, 1408, 1024, 768, 512, 256, 128))
    if shard_cols:
        out_spec = pl.BlockSpec((None, tm, tn), lambda i, j, kk: (j, i, 0))
        out_shape = jax.ShapeDtypeStruct((n // tn, m, tn), F32)
    else:
        out_spec = pl.BlockSpec((tm, tn), lambda i, j, kk: (i, j))
        out_shape = jax.ShapeDtypeStruct((m, n), F32)

    def body(a_ref, b_ref, o_ref):
        kk = pl.program_id(2)
        part = jnp.dot(a_ref[...].astype(BF16).T, b_ref[...].astype(BF16),
                       preferred_element_type=F32)

        @pl.when(kk == 0)
        def _():
            o_ref[...] = part

        @pl.when(kk > 0)
        def _():
            o_ref[...] += part

    return _call(
        body, name=name, grid=(m // tm, n // tn, s // ts),
        in_specs=[pl.BlockSpec((ts, tm), lambda i, j, kk: (kk, i)),
                  pl.BlockSpec((ts, tn), lambda i, j, kk: (kk, j))],
        out_specs=out_spec, out_shape=out_shape,
        compiler_params=_params("parallel", "parallel", "arbitrary"),
    )(a, b)


def _rowwise(fn, rows, consts, row_outs, acc_outs, *, name, tm=ROW_TILE, ex=None):
    rows = [r if isinstance(r, tuple) else (r, r.shape[1], 0) for r in rows]
    s = rows[0][0].shape[0]
    nr, nc, no, na = len(rows), len(consts), len(row_outs), len(acc_outs)
    grid = (s // tm,)

    def body(*refs):
        ins, outs_, ex_refs = _split_refs(refs, nr + nc, no + na, ex)
        _hosted(ex, ex_refs, grid, True)
        r_in, c_in = ins[:nr], ins[nr:]
        o_refs, a_refs = outs_[:no], outs_[no:]
        outs = fn(*[r[...] for r in r_in], *[c[...] for c in c_in])
        for r, v in zip(o_refs, outs[:no]):
            r[...] = v.astype(r.dtype)
        if na:
            i = pl.program_id(0)

            @pl.when(i == 0)
            def _():
                for r, v in zip(a_refs, outs[no:]):
                    r[...] = v.astype(F32)

            @pl.when(i > 0)
            def _():
                for r, v in zip(a_refs, outs[no:]):
                    r[...] += v.astype(F32)
        _hosted(ex, ex_refs, grid, False)

    in_specs = [r[1] if isinstance(r[1], pl.BlockSpec) else
                pl.BlockSpec((tm, r[1]), lambda i, cb=r[2]: (i, cb)) for r in rows]
    in_specs += [pl.BlockSpec(c.shape, lambda i: (0, 0)) for c in consts]
    out_specs = [pl.BlockSpec((tm, w), lambda i: (i, 0)) for w, _ in row_outs]
    out_specs += [pl.BlockSpec(sh, lambda i: (0, 0)) for sh in acc_outs]
    out_shape = [jax.ShapeDtypeStruct((s, w), dt) for w, dt in row_outs]
    out_shape += [jax.ShapeDtypeStruct(sh, F32) for sh in acc_outs]
    res, hosted = _host_call(body, ex, name, grid, in_specs, out_specs, out_shape,
                             [r[0] for r in rows] + list(consts))
    return (res, hosted) if ex else res


def _rms(x, g, n=None):
    ms = jnp.sum(x * x, axis=-1, keepdims=True) / float(n or x.shape[-1])
    return x * lax.rsqrt(ms + EPS) * g


def _layer_norm(x, g, b):
    mu = jnp.sum(x, axis=-1, keepdims=True) / float(x.shape[-1])
    xc = x - mu
    var = jnp.sum(xc * xc, axis=-1, keepdims=True) / float(x.shape[-1])
    return xc * lax.rsqrt(var + EPS) * g + b


def _silu(x):
    return x * jax.nn.sigmoid(x)


@jax.custom_vjp
def _rope(y, cos, sin_a, sin_b):
    return y * cos + pltpu.roll(y, 112, 1) * sin_a + pltpu.roll(y, 16, 1) * sin_b


def _rope_fwd(y, cos, sin_a, sin_b):
    return _rope(y, cos, sin_a, sin_b), (cos, sin_a, sin_b)


def _rope_bwd(res, ct):
    cos, sin_a, sin_b = res
    dy = ct * cos + pltpu.roll(ct * sin_a, 16, 1) + pltpu.roll(ct * sin_b, 112, 1)
    return dy, jnp.zeros_like(cos), jnp.zeros_like(sin_a), jnp.zeros_like(sin_b)


_rope.defvjp(_rope_fwd, _rope_bwd)


def _qk_head(xh, g, cos, sin_a, sin_b):
    return _rope(_rms(xh, g, MLA_QK), cos, sin_a, sin_b)


def _heads(x, width):
    return [x[:, h * width:(h + 1) * width] for h in range(x.shape[1] // width)]


def _f_rms(x, g):
    return (_rms(x, g),)


def _f_rope_tab(pos, inv_freq):
    ang = pos.astype(F32) * inv_freq
    lane = lax.broadcasted_iota(jnp.int32, ang.shape, 1)
    sn = jnp.sin(ang)
    first = (lane >= MLA_NOPE) & (lane < MLA_NOPE + MLA_ROPE // 2)
    second = (lane >= MLA_NOPE + MLA_ROPE // 2) & (lane < MLA_QK)
    return jnp.cos(ang), jnp.where(first, -sn, 0.0), jnp.where(second, sn, 0.0)


def _mix_pre(za, zg, zcq, zckv, ba, bg, gq, gkv):
    u0 = (za + ba) * jax.nn.sigmoid(zg + bg)
    return u0, _rms(zcq, gq), _rms(zckv, gkv)


def _ln_silu(c1, bdw, lg, lb):
    return _silu(_layer_norm(c1 + bdw, lg, lb))


def _f_qk_prep(q0, kn, kr, cos, sa, sb, gq, gk):
    qs = [_qk_head(xh, gq, cos, sa, sb) * Q_SCALE for xh in _heads(q0, LANES)]
    ks = [_qk_head(xh + kr, gk, cos, sa, sb) for xh in _heads(kn, LANES)]
    return jnp.concatenate(qs, axis=1), jnp.concatenate(ks, axis=1)


def _act(cg, cv, bg, bv):
    return _silu(cg + bg) * (cv + bv)


def _f_mem_k(kk, g):
    return (jnp.concatenate([_rms(xh, g) for xh in _heads(kk, MEM_HEAD_DIM)], axis=1),)


def _mem_probs(qn, kmh):
    s = lax.dot_general(qn.astype(BF16), kmh, NT_DIMS, preferred_element_type=F32) * MEM_SCALE
    e = jnp.exp(s - jnp.max(s, axis=-1, keepdims=True))
    return e / jnp.sum(e, axis=-1, keepdims=True)


def _f_mem_attn(qm0, km, vm, g):
    outs = []
    for h, xh in enumerate(_heads(qm0, MEM_HEAD_DIM)):
        sl = slice(h * MEM_HEAD_DIM, (h + 1) * MEM_HEAD_DIM)
        p = _mem_probs(_rms(xh, g), km[:, sl])
        outs.append(jnp.dot(p.astype(BF16), vm[:, sl].astype(BF16), preferred_element_type=F32))
    return (jnp.concatenate(outs, axis=1),)


def _f_loss(y, t):
    e = y - t
    return e * (1.0 / D_MODEL), jnp.sum(e * e, axis=0, keepdims=True)


def _b_rms(x, dh, dres, g):
    _, vjp = jax.vjp(_rms, x, g)
    dx, dg = vjp(dh)
    return dx + dres, dg


def _b_rms_nores(x, dh, g):
    _, vjp = jax.vjp(_rms, x, g)
    dx, dg = vjp(dh)
    return dx, dg


def _b_mix_pre(za, zg, zcq, zckv, du0, dcqn, dckvn, dkr, ba, bg, gq, gkv):
    _, vjp = jax.vjp(_mix_pre, za, zg, zcq, zckv, ba, bg, gq, gkv)
    dza, dzg, dzcq, dzckv, dba, dbg, dgq, dgkv = vjp((du0, dcqn, dckvn))
    return jnp.concatenate([dza, dzg, dzcq, dzckv, dkr], axis=1), dba, dbg, dgq, dgkv


def _b_ln_silu(c1, du, bdw, lg, lb):
    _, vjp = jax.vjp(_ln_silu, c1, bdw, lg, lb)
    return vjp(du)


def _b_qk_prep(q0, kn, kr, cos, sa, sb, dq, dk, gq, gk):
    head = lambda xh, g: _qk_head(xh, g, cos, sa, sb)
    dk = dk * LN2
    dq0, dkn = [], []
    dkr = jnp.zeros_like(kr)
    dgq = jnp.zeros_like(gq)
    dgk = jnp.zeros_like(gk)
    for h, xh in enumerate(_heads(q0, LANES)):
        _, vjp = jax.vjp(head, xh, gq)
        dx, dg = vjp(dq[h, 0].T * ATT_SCALE)
        dq0.append(dx)
        dgq = dgq + dg
    for xh, ct in zip(_heads(kn, LANES), _heads(dk, LANES)):
        _, vjp = jax.vjp(head, xh + kr, gk)
        dx, dg = vjp(ct)
        dkn.append(dx)
        dkr = dkr + dx
        dgk = dgk + dg
    return jnp.concatenate(dq0, axis=1), jnp.concatenate(dkn, axis=1), dkr, dgq, dgk


def _b_mem_k(kk, dkm, g):
    dkk = []
    dg = jnp.zeros_like(g)
    for xh, ct in zip(_heads(kk, MEM_HEAD_DIM), _heads(dkm, MEM_HEAD_DIM)):
        _, vjp = jax.vjp(_rms, xh, g)
        dx, dgh = vjp(ct)
        dkk.append(dx)
        dg = dg + dgh
    return jnp.concatenate(dkk, axis=1), dg


def _b_mem_attn(dom, qm0, km, vm, g):
    dq0, dkm, dvm = [], [], []
    dg = jnp.zeros_like(g)
    for h, (xh, doh) in enumerate(zip(_heads(qm0, MEM_HEAD_DIM), _heads(dom, MEM_HEAD_DIM))):
        sl = slice(h * MEM_HEAD_DIM, (h + 1) * MEM_HEAD_DIM)
        kmh, vmh = km[:, sl], vm[:, sl].astype(BF16)
        qn, vjp = jax.vjp(_rms, xh, g)
        p = _mem_probs(qn, kmh)
        dob = doh.astype(BF16)
        dp = lax.dot_general(dob, vmh, NT_DIMS, preferred_element_type=F32)
        ds = (p * (dp - jnp.sum(dp * p, axis=-1, keepdims=True)) * MEM_SCALE).astype(BF16)
        dqn = jnp.dot(ds, kmh, preferred_element_type=F32)
        dkm.append(jnp.dot(ds.T, qn.astype(BF16), preferred_element_type=F32))
        dvm.append(jnp.dot(p.astype(BF16).T, dob, preferred_element_type=F32))
        dx, dgh = vjp(dqn)
        dq0.append(dx)
        dg = dg + dgh
    return (jnp.concatenate(dq0, axis=1), jnp.concatenate(dkm, axis=1),
            jnp.concatenate(dvm, axis=1), dg)


def _adamw(w, g, m, v):
    m = ADAM_B1 * m + (1.0 - ADAM_B1) * g
    v = ADAM_B2 * v + (1.0 - ADAM_B2) * jnp.square(g)
    m_hat = m / (1.0 - ADAM_B1 ** ADAM_STEP)
    v_hat = v / (1.0 - ADAM_B2 ** ADAM_STEP)
    delta = -ADAM_LR * (m_hat / (jnp.sqrt(v_hat) + ADAM_EPS) + ADAM_WD * w)
    return delta, m, v


def _adamw_small(parts, small, tiny, name):
    def body(*refs):
        p_ref, ins, outs = refs[0], refs[1:4 + 3 * len(TINY)], refs[4 + 3 * len(TINY):]
        me = 4 * lax.axis_index("x") + 2 * lax.axis_index("y") + lax.axis_index("c")
        groups = [(0, SMALL_ROWS)]
        groups += [(pl.multiple_of(base + me * rows, 8), rows) for base, (_, _, rows) in zip(TINY_BASE, TINY)]
        for k, (start, rows) in enumerate(groups):
            g = p_ref[0, pl.ds(start, rows), :]
            for d in range(1, N_DEV):
                g = g + p_ref[d, pl.ds(start, rows), :]
            w, m, v = [r[...] for r in ins[3 * k:3 * k + 3]]
            for ref, val in zip(outs[4 * k:4 * k + 4], (g,) + _adamw(w, g, m, v)):
                ref[...] = val
        total = p_ref[0, PART_ROWS:PART_ROWS + 8, :]
        for d in range(1, N_DEV):
            total = total + p_ref[d, PART_ROWS:PART_ROWS + 8, :]
        outs[-1][...] = total

    args = list(small) + [t for grp in tiny for t in grp]
    out_shape = []
    for grp in [small] + list(tiny):
        out_shape += [jax.ShapeDtypeStruct(grp[0].shape, F32)] * 4
    out_shape.append(jax.ShapeDtypeStruct((8, LANES), F32))
    return _call(body, name=name, out_shape=out_shape)(parts, *args)


def _conv_fwd(x, w, name):
    s, ch = x.shape
    kw = w.shape[0]
    halo = -(-(kw - 1) // 8) * 8
    r = CONV_ROWS
    n = s // r

    def chunk(window, wv):
        acc = jnp.zeros((r, LANES), F32)
        for k in range(kw):
            shift = kw - 1 - k
            sh = window if shift == 0 else pltpu.roll(window, shift, 0)
            acc = acc + sh[halo:halo + r] * wv[k:k + 1]
        return acc

    def body(x_ref, w_ref, y_ref):
        wv = w_ref[...]
        first = jnp.concatenate([jnp.zeros((halo, LANES), F32), x_ref[0:r]], axis=0)
        y_ref[0:r] = chunk(first, wv)

        def step(i, carry):
            base = pl.multiple_of(i * r, 8)
            y_ref[pl.ds(base, r)] = chunk(x_ref[pl.ds(base - halo, r + halo)], wv)
            return carry

        lax.fori_loop(1, n, step, 0)

    return _call(
        body, name=name, grid=(ch // LANES,),
        in_specs=[pl.BlockSpec((s, LANES), lambda c: (0, c)), pl.BlockSpec((kw, LANES), lambda c: (0, c))],
        out_specs=pl.BlockSpec((s, LANES), lambda c: (0, c)),
        out_shape=jax.ShapeDtypeStruct((s, ch), F32), compiler_params=_params("parallel"),
    )(x, w)


def _conv_bwd(dy, x, w, name, ex=None):
    s, ch = x.shape
    kw = w.shape[0]
    halo = -(-(kw - 1) // 8) * 8
    r = CONV_ROWS
    n = s // r

    def dx_chunk(window, wv):
        acc = jnp.zeros((r, LANES), F32)
        for k in range(kw):
            shift = kw - 1 - k
            sh = window if shift == 0 else pltpu.roll(window, r + halo - shift, 0)
            acc = acc + sh[0:r] * wv[k:k + 1]
        return acc

    def dw_chunk(xwin, dyc, acc_ref):
        for k in range(kw):
            shift = kw - 1 - k
            sh = xwin if shift == 0 else pltpu.roll(xwin, shift, 0)
            prod = sh[halo:halo + r] * dyc
            acc_ref[k] += jnp.sum(prod.reshape(r // 8, 8, LANES), axis=0)

    grid = (ch // LANES,)

    def body(*refs):
        (dy_ref, x_ref, w_ref), (dx_ref, dw_ref, acc_ref), ex_refs = _split_refs(refs, 3, 2, ex, 1)
        _hosted(ex, ex_refs, grid, True)
        wv = w_ref[...]
        acc_ref[...] = jnp.zeros_like(acc_ref)
        xfirst = jnp.concatenate([jnp.zeros((halo, LANES), F32), x_ref[0:r]], axis=0)
        dw_chunk(xfirst, dy_ref[0:r], acc_ref)
        last = jnp.concatenate([dy_ref[s - r:s], jnp.zeros((halo, LANES), F32)], axis=0)
        dx_ref[s - r:s] = dx_chunk(last, wv)

        def step(i, carry):
            base = pl.multiple_of(i * r, 8)
            dw_chunk(x_ref[pl.ds(base - halo, r + halo)], dy_ref[pl.ds(base, r)], acc_ref)
            prev = pl.multiple_of((i - 1) * r, 8)
            dx_ref[pl.ds(prev, r)] = dx_chunk(dy_ref[pl.ds(prev, r + halo)], wv)
            return carry

        lax.fori_loop(1, n, step, 0)
        dw_ref[...] = jnp.sum(acc_ref[...], axis=1)
        _hosted(ex, ex_refs, grid, False)

    spec = pl.BlockSpec((s, LANES), lambda c: (0, c))
    wspec = pl.BlockSpec((kw, LANES), lambda c: (0, c))
    return _host_call(
        body, ex, name, grid, [spec, spec, wspec], [spec, wspec],
        [jax.ShapeDtypeStruct((s, ch), F32), jax.ShapeDtypeStruct((kw, ch), F32)], (dy, x, w),
        scratch=[pltpu.VMEM((kw, 8, LANES), F32)])


HALO = 8


def _conv3(win, w, rows):
    return (pltpu.roll(win, 2, 0)[HALO:HALO + rows] * w[0:1] + pltpu.roll(win, 1, 0)[HALO:HALO + rows] * w[1:2]
            + win[HALO:HALO + rows] * w[2:3])


def _ffn_mid_bwd(up_g, up_v, dy, w_dn, w_g, w_v, b_g, b_v, name):
    s, width = up_g.shape
    tm, tc = TM, FF_PAD
    n_row = s // tm

    def body(dy_ref, ndy_ref, wd_ref, pg_ref, g_ref, ng_ref, pv_ref, v_ref, nv_ref, wg_ref, wv_ref,
             bg_ref, bv_ref, dug_ref, duv_ref, dwg_ref, dwv_ref, dbg_ref, dbv_ref):
        i = pl.program_id(1)
        first = (i > 0).astype(F32)
        last = (i < n_row - 1).astype(F32)
        ext = tm + HALO
        wg, wv, wd = wg_ref[...], wv_ref[...], wd_ref[...]
        ndy = jnp.concatenate([ndy_ref[...] * last, jnp.zeros((HALO, D_MODEL), F32)], axis=0)
        d_ext = jnp.concatenate([
            lax.dot_general(dy_ref[...].astype(BF16), wd, NT_DIMS, preferred_element_type=F32),
            lax.dot_general(ndy.astype(BF16), wd, NT_DIMS, preferred_element_type=F32)[:HALO]], axis=0)
        xg = jnp.concatenate([pg_ref[...] * first, g_ref[...], ng_ref[...]], axis=0)
        xv = jnp.concatenate([pv_ref[...] * first, v_ref[...], nv_ref[...]], axis=0)
        taps = [[pltpu.roll(x, 2, 0), pltpu.roll(x, 1, 0), x] for x in (xg, xv)]
        conv = [sum(t[HALO:HALO + ext] * w[k:k + 1] for k, t in enumerate(tp)) for tp, w in zip(taps, (wg, wv))]
        _, vjp = jax.vjp(lambda cg_, cv_: _act(cg_, cv_, bg_ref[...], bv_ref[...]), *conv)
        dcg, dcv = vjp(d_ext)
        results = []
        for tp, w, dc in ((taps[0], wg, dcg), (taps[1], wv, dcv)):
            dup = (dc[:tm] * w[2:3] + pltpu.roll(dc, ext - 1, 0)[:tm] * w[1:2]
                   + pltpu.roll(dc, ext - 2, 0)[:tm] * w[0:1])
            own = dc[:tm]
            dw = jnp.concatenate(
                [jnp.sum(own * t[HALO:HALO + tm], axis=0, keepdims=True) for t in tp], axis=0)
            results.append((dup, dw, jnp.sum(own, axis=0, keepdims=True)))
        (dug, dwg, dbg), (duv, dwv, dbv) = results
        dug_ref[...] = dug.astype(dug_ref.dtype)
        duv_ref[...] = duv.astype(duv_ref.dtype)

        @pl.when(i == 0)
        def _():
            dwg_ref[...], dwv_ref[...], dbg_ref[...], dbv_ref[...] = dwg, dwv, dbg, dbv

        @pl.when(i > 0)
        def _():
            dwg_ref[...] += dwg
            dwv_ref[...] += dwv
            dbg_ref[...] += dbg
            dbv_ref[...] += dbv

    per = tm // HALO
    tile = pl.BlockSpec((tm, tc), lambda c, i: (i, c))
    prev = pl.BlockSpec((HALO, tc), lambda c, i: (jnp.maximum(i * per - 1, 0), c))
    nxt = pl.BlockSpec((HALO, tc), lambda c, i: (jnp.minimum((i + 1) * per, s // HALO - 1), c))
    wspec = pl.BlockSpec((FFN_CONV_WIDTH, tc), lambda c, i: (0, c))
    bspec = pl.BlockSpec((1, tc), lambda c, i: (0, c))
    wide = jax.ShapeDtypeStruct((s, width), BF16)
    dy_tile = pl.BlockSpec((tm, D_MODEL), lambda c, i: (i, 0))
    dy_next = pl.BlockSpec((HALO, D_MODEL), lambda c, i: (jnp.minimum((i + 1) * per, s // HALO - 1), 0))
    return _call(
        body, name=name, grid=(width // tc, n_row),
        in_specs=[dy_tile, dy_next, pl.BlockSpec((tc, D_MODEL), lambda c, i: (c, 0)),
                  prev, tile, nxt, prev, tile, nxt, wspec, wspec, bspec, bspec],
        out_specs=[tile, tile, wspec, wspec, bspec, bspec],
        out_shape=[wide, wide] + [jax.ShapeDtypeStruct((FFN_CONV_WIDTH, width), F32)] * 2
        + [jax.ShapeDtypeStruct((1, width), F32)] * 2,
        compiler_params=_params("parallel", "arbitrary"),
    )(dy, dy, w_dn, up_g, up_g, up_g, up_v, up_v, up_v, w_g, w_v, b_g, b_v)


def _ffn_fwd(h3, w_up, w_g, w_v, b_g, b_v, w_dn, x2, target, name):
    s = h3.shape[0]
    tm, tc = TM, FF_PAD
    n_col = D_FF_PAD // tc

    def body(h_ref, wug_ref, wuv_ref, wg_ref, wv_ref, bg_ref, bv_ref, wd_ref, x_ref, t_ref,
             ug_ref, uv_ref, act_ref, dy_ref, sq_ref, halo_g, halo_v, y_acc):
        i, c = pl.program_id(0), pl.program_id(1)
        h = h_ref[...]
        up_g = jnp.dot(h, wug_ref[...], preferred_element_type=F32)
        up_v = jnp.dot(h, wuv_ref[...], preferred_element_type=F32)
        ug_ref[...] = up_g
        uv_ref[...] = up_v
        has_prev = i > 0
        prev_g = jnp.where(has_prev, halo_g[c], 0.0)
        prev_v = jnp.where(has_prev, halo_v[c], 0.0)
        halo_g[c] = up_g[tm - HALO:]
        halo_v[c] = up_v[tm - HALO:]
        cg = _conv3(jnp.concatenate([prev_g, up_g], axis=0), wg_ref[...], tm)
        cv = _conv3(jnp.concatenate([prev_v, up_v], axis=0), wv_ref[...], tm)
        act = _act(cg, cv, bg_ref[...], bv_ref[...]).astype(BF16)
        act_ref[...] = act
        part = jnp.dot(act, wd_ref[...], preferred_element_type=F32)

        @pl.when(c == 0)
        def _():
            y_acc[...] = part

        @pl.when(c > 0)
        def _():
            y_acc[...] += part

        @pl.when(c == n_col - 1)
        def _():
            dy, sq = _f_loss(x_ref[...] + y_acc[...], t_ref[...])
            dy_ref[...] = dy

            @pl.when(i == 0)
            def _():
                sq_ref[...] = sq

            @pl.when(i > 0)
            def _():
                sq_ref[...] += sq

    row = lambda w: pl.BlockSpec((tm, w), lambda i, c: (i, 0))
    tile = pl.BlockSpec((tm, tc), lambda i, c: (i, c))
    wspec = pl.BlockSpec((FFN_CONV_WIDTH, tc), lambda i, c: (0, c))
    bspec = pl.BlockSpec((1, tc), lambda i, c: (0, c))
    wide = jax.ShapeDtypeStruct((s, D_FF_PAD), F32)
    return _call(
        body, name=name, grid=(s // tm, n_col),
        in_specs=[row(D_MODEL),
                  pl.BlockSpec((None, D_MODEL, tc), lambda i, c: (c, 0, 0)),
                  pl.BlockSpec((None, D_MODEL, tc), lambda i, c: (n_col + c, 0, 0)),
                  wspec, wspec, bspec, bspec, pl.BlockSpec((tc, D_MODEL), lambda i, c: (c, 0)),
                  row(D_MODEL), row(D_MODEL)],
        out_specs=[tile, tile, tile, row(D_MODEL), pl.BlockSpec((1, D_MODEL), lambda i, c: (0, 0))],
        out_shape=[wide, wide, jax.ShapeDtypeStruct((s, D_FF_PAD), BF16),
                   jax.ShapeDtypeStruct((s, D_MODEL), F32), jax.ShapeDtypeStruct((1, D_MODEL), F32)],
        scratch_shapes=[pltpu.VMEM((n_col, HALO, tc), F32), pltpu.VMEM((n_col, HALO, tc), F32),
                        pltpu.VMEM((tm, D_MODEL), F32)],
        compiler_params=_params("arbitrary", "arbitrary"),
    )(h3, w_up, w_up, w_g, w_v, b_g, b_v, w_dn, x2, target)


def _chunk_mask(rows_are_queries):
    a = lax.broadcasted_iota(jnp.int32, (TQ, TQ), 0) // CHUNK
    b = lax.broadcasted_iota(jnp.int32, (TQ, TQ), 1) // CHUNK
    return (b <= a) if rows_are_queries else (a <= b)


def _head_lanes(hh):
    return slice(hh * LANES, (hh + 1) * LANES)


def _to_row(col):
    return jnp.broadcast_to(col, (TQ, LANES)).T[0:1, :]


def _flash_specs(s, heads=HEADS_PER_STEP):
    width = heads * LANES
    tile = pl.BlockSpec((TQ, width), lambda h, i: (i, h))
    whole = pl.BlockSpec((s, width), lambda h, i: (0, h))
    row_tile = pl.BlockSpec((heads, 1, 1, TQ), lambda h, i: (h, i, 0, 0))
    row_whole = pl.BlockSpec((heads, s // TQ, 1, TQ), lambda h, i: (h, 0, 0, 0))
    return tile, whole, row_tile, row_whole


def _split_refs(refs, n_in, n_out, ex, n_scratch=0):
    e_in, e_out = (len(ex.inputs), len(ex.out_shape)) if ex else (0, 0)
    a, b, c = n_in + e_in, n_in + e_in + n_out, n_in + e_in + n_out + e_out
    return refs[:n_in], refs[a:b] + refs[c:c + n_scratch], (refs[n_in:a], refs[b:c], refs[c + n_scratch:])


def _hosted(ex, ex_refs, grid, when_first):
    if ex is None:
        return
    ids = [pl.program_id(d) for d in range(len(grid))]
    cond = functools.reduce(
        lambda p, q_: p & q_, [i == (0 if when_first else g - 1) for i, g in zip(ids, grid)])

    @pl.when(cond)
    def _():
        (ex.start if when_first else ex.finish)(*ex_refs)


def _host_call(body, ex, name, grid, in_specs, out_specs, out_shape, args, scratch=()):
    e_in, e_out = (len(ex.inputs), len(ex.out_shape)) if ex else (0, 0)
    res = _call(
        body, name=name, grid=grid, in_specs=list(in_specs) + [ANY] * e_in,
        out_specs=list(out_specs) + [ANY] * e_out,
        out_shape=list(out_shape) + (ex.out_shape if ex else []),
        scratch_shapes=list(scratch) + (ex.scratch if ex else []),
        compiler_params=_params(*["arbitrary"] * len(grid)),
    )(*args, *(ex.inputs if ex else []))
    return res[:len(out_shape)], res[len(out_shape):]


def _flash_fwd(q, k, v, name, ex=None):
    s = q.shape[0]
    nq = s // TQ
    heads = FWD_HEADS
    grid = (MLA_HEADS // heads, nq)

    def body(*refs):
        (q_ref, k_ref, v_ref), (o_ref, lse_row_ref), ex_refs = _split_refs(refs, 3, 2, ex)
        _hosted(ex, ex_refs, grid, True)
        i = pl.program_id(1)
        qs = [q_ref[:, _head_lanes(hh)] for hh in range(heads)]

        def scores(j, hh):
            kj = k_ref[pl.ds(pl.multiple_of(j * TQ, TQ), TQ), _head_lanes(hh)]
            return lax.dot_general(kj, qs[hh], NT_DIMS, preferred_element_type=F32)

        def update(j, sc, m_prev, l_prev, acc, hh):
            vt = v_ref[pl.ds(pl.multiple_of(j * TQ, TQ), TQ), _head_lanes(hh)].T
            m_new = jnp.maximum(m_prev, jnp.max(sc, axis=0, keepdims=True))
            alpha = jnp.exp2(m_prev - m_new)
            p = jnp.exp2(sc - m_new)
            l_new = alpha * l_prev + jnp.sum(p, axis=0, keepdims=True)
            acc = acc * alpha + jnp.dot(vt, p.astype(BF16), preferred_element_type=F32)
            return m_new, l_new, acc

        def step(j, carry):
            out = []
            for hh in range(heads):
                sc, m_prev, l_prev, acc = carry[hh]
                out.append((scores(j + 1, hh),) + update(j, sc, m_prev, l_prev, acc, hh))
            return tuple(out)

        init = tuple((scores(0, hh), jnp.full((1, TQ), NEG, F32), jnp.zeros((1, TQ), F32),
                      jnp.zeros((LANES, TQ), F32)) for hh in range(heads))
        carry = lax.fori_loop(0, i, step, init)
        for hh, (sc, m_prev, l_prev, acc) in enumerate(carry):
            sc = jnp.where(_chunk_mask(False), sc, NEG)
            m_fin, l_fin, acc = update(i, sc, m_prev, l_prev, acc, hh)
            o_ref[:, _head_lanes(hh)] = (acc / l_fin).T
            lse_row_ref[hh, 0] = m_fin + jnp.log2(l_fin)
        _hosted(ex, ex_refs, grid, False)

    tile, whole, row_tile, _ = _flash_specs(s, heads)
    return _host_call(
        body, ex, name, grid, [tile, whole, whole], [tile, row_tile],
        [jax.ShapeDtypeStruct((s, MLA_HEADS * LANES), F32),
         jax.ShapeDtypeStruct((MLA_HEADS, nq, 1, TQ), F32)], (q, k, v))


def _attn_delta(do, o, name):
    s = do.shape[0]

    def body(do_ref, o_ref, d_ref):
        for h in range(MLA_HEADS):
            prod = do_ref[:, _head_lanes(h)] * o_ref[:, _head_lanes(h)]
            d_ref[h, 0] = _to_row(jnp.sum(prod, axis=-1, keepdims=True))

    tile = pl.BlockSpec((TQ, MLA_HEADS * LANES), lambda i: (i, 0))
    return _call(
        body, name=name, grid=(s // TQ,), in_specs=[tile, tile],
        out_specs=pl.BlockSpec((MLA_HEADS, 1, 1, TQ), lambda i: (0, i, 0, 0)),
        out_shape=jax.ShapeDtypeStruct((MLA_HEADS, s // TQ, 1, TQ), F32),
        compiler_params=_params("parallel"),
    )(do, o)


def _flash_bwd(q, k, v, do, lse_row, delta_row, name, ex=None):
    s = q.shape[0]
    nq = s // TQ
    grid = (MLA_HEADS // HEADS_PER_STEP, nq)

    def body(*refs):
        ins, (dqt_ref, dk_ref, dv_ref), ex_refs = _split_refs(refs, 6, 3, ex)
        q_ref, k_ref, v_ref, do_ref, lse_row_ref, delta_row_ref = ins
        _hosted(ex, ex_refs, grid, True)
        j = pl.program_id(1)

        @pl.when(j == 0)
        def _():
            dqt_ref[...] = jnp.zeros_like(dqt_ref)

        kjs = [k_ref[:, _head_lanes(hh)] for hh in range(HEADS_PER_STEP)]
        vjs = [v_ref[:, _head_lanes(hh)] for hh in range(HEADS_PER_STEP)]
        kts = [kj.T for kj in kjs]

        def step(i, carry, masked):
            base = pl.multiple_of(i * TQ, TQ)
            out = []
            for hh in range(HEADS_PER_STEP):
                dk, dv = carry[hh]
                qi = q_ref[pl.ds(base, TQ), _head_lanes(hh)]
                dob = do_ref[pl.ds(base, TQ), _head_lanes(hh)].astype(BF16)
                sc_t = lax.dot_general(kjs[hh], qi, NT_DIMS, preferred_element_type=F32)
                if masked:
                    sc_t = jnp.where(_chunk_mask(False), sc_t, NEG)
                p_t = jnp.exp2(sc_t - lse_row_ref[hh, i])
                dv = dv + jnp.dot(p_t.astype(BF16), dob, preferred_element_type=F32)
                dp_t = lax.dot_general(vjs[hh], dob, NT_DIMS, preferred_element_type=F32)
                ds_t = (p_t * (dp_t - delta_row_ref[hh, i])).astype(BF16)
                dk = dk + jnp.dot(ds_t, qi, preferred_element_type=F32)
                dqt_ref[hh, i] += jnp.dot(kts[hh], ds_t, preferred_element_type=F32)
                out.append((dk, dv))
            return tuple(out)

        zero = jnp.zeros((TQ, LANES), F32)
        carry = step(j, tuple((zero, zero) for _ in range(HEADS_PER_STEP)), True)
        carry = lax.fori_loop(j + 1, nq, functools.partial(step, masked=False), carry)
        for hh, (dk, dv) in enumerate(carry):
            dk_ref[:, _head_lanes(hh)] = dk
            dv_ref[:, _head_lanes(hh)] = dv.astype(dv_ref.dtype)
        _hosted(ex, ex_refs, grid, False)

    tile, whole, _, row_whole = _flash_specs(s)
    dqt_spec = pl.BlockSpec((HEADS_PER_STEP, nq, LANES, TQ), lambda h, j: (h, 0, 0, 0))
    wide = lambda dt: jax.ShapeDtypeStruct((s, MLA_HEADS * LANES), dt)
    return _host_call(
        body, ex, name, grid, [whole, tile, tile, whole, row_whole, row_whole], [dqt_spec, tile, tile],
        [jax.ShapeDtypeStruct((MLA_HEADS, nq, LANES, TQ), F32), wide(F32), wide(BF16)],
        (q, k, v, do, lse_row, delta_row))


def _side_by_side(g):
    return g.transpose(1, 0, 2).reshape(g.shape[1], N_DEV * g.shape[2])


def _col_shards(g):
    return g.reshape(g.shape[0], N_DEV, g.shape[1] // N_DEV).transpose(1, 0, 2)


def _pad_last(v, to):
    return jnp.pad(v, [(0, 0)] * (v.ndim - 1) + [(0, to - v.shape[-1])])


def _tiny_rows(v, rows):
    flat = v.reshape(v.shape[:-2] + (-1,))
    return _pad_last(flat, rows * LANES).reshape(v.shape[:-2] + (rows, LANES))


def _pack_small(vals):
    parts = []
    for (n, size), pad in zip(SMALL, SMALL_PAD):
        parts.append(jnp.pad(vals[n].reshape(-1), (0, pad - size)))
    flat = jnp.concatenate(parts)
    return jnp.pad(flat, (0, SMALL_ROWS * LANES - flat.shape[0])).reshape(SMALL_ROWS, LANES)


def _unpack_small(packed):
    flat = packed.reshape(-1)
    out, off = {}, 0
    for (n, size), pad in zip(SMALL, SMALL_PAD):
        out[n] = flat[off:off + size].reshape(1, size)
        off += pad
    return out


def _pad_heads(w, per_head, axis):
    shape = list(w.shape)
    shape[axis:axis + 1] = [MLA_HEADS, per_head]
    w = w.reshape(shape)
    pad = [(0, 0)] * len(shape)
    pad[axis + 1] = (0, LANES - per_head)
    w = jnp.pad(w, pad)
    shape[axis:axis + 2] = [MLA_HEADS * LANES]
    return w.reshape(shape)


def _unpad_heads(w, per_head, axis):
    shape = list(w.shape)
    shape[axis:axis + 1] = [MLA_HEADS, LANES]
    w = w.reshape(shape)
    w = lax.slice_in_dim(w, 0, per_head, axis=axis + 1)
    shape[axis:axis + 2] = [MLA_HEADS * per_head]
    return w.reshape(shape)


def _row(v, pad_to=None):
    v = v.reshape(1, -1)
    if pad_to is not None:
        v = jnp.pad(v, ((0, 0), (0, pad_to - v.shape[1])))
    return v


def kernel(x, mem, positions, mix_norm_g, w_in, b_conv_in, w_conv_dw, b_conv_dw, conv_ln_g, conv_ln_b, q_lat_norm_g, w_uq, kv_lat_norm_g, w_ukv, q_norm_g, k_norm_g, w_out, mem_norm_x_g, mem_norm_m_g, w_mem_q, w_mem_kv, mem_q_norm_g, mem_k_norm_g, w_mem_o, ffn_norm_g, w_up, w_ffn_dw, b_ffn_dw, w_down, loss_target, m_mix_norm_g, m_w_in, m_b_conv_in, m_w_conv_dw, m_b_conv_dw, m_conv_ln_g, m_conv_ln_b, m_q_lat_norm_g, m_w_uq, m_kv_lat_norm_g, m_w_ukv, m_q_norm_g, m_k_norm_g, m_w_out, m_mem_norm_x_g, m_mem_norm_m_g, m_w_mem_q, m_w_mem_kv, m_mem_q_norm_g, m_mem_k_norm_g, m_w_mem_o, m_ffn_norm_g, m_w_up, m_w_ffn_dw, m_b_ffn_dw, m_w_down, v_mix_norm_g, v_w_in, v_b_conv_in, v_w_conv_dw, v_b_conv_dw, v_conv_ln_g, v_conv_ln_b, v_q_lat_norm_g, v_w_uq, v_kv_lat_norm_g, v_w_ukv, v_q_norm_g, v_k_norm_g, v_w_out, v_mem_norm_x_g, v_mem_norm_m_g, v_w_mem_q, v_w_mem_kv, v_mem_q_norm_g, v_mem_k_norm_g, v_w_mem_o, v_ffn_norm_g, v_w_up, v_w_ffn_dw, v_b_ffn_dw, v_w_down):
    a = dict(locals())
    seq = x.shape[1]
    xs = x.reshape(seq, D_MODEL)
    mems = mem.reshape(-1, D_MODEL)
    target = loss_target.reshape(seq, D_MODEL)

    tiny = [n for n, _, _ in TINY]
    shard = lambda n: a[n][0] if n in tiny else a[n][0].astype(BF16)
    pos = jnp.stack([2 * lax.axis_index("x") + lax.axis_index("y"), lax.axis_index("c")]).astype(jnp.int32)
    ag_first = ["w_in", "w_uq", "w_ukv", "w_conv_dw"]
    ag_later = [n for n in [b for b, _ in BIG] + tiny if n not in ag_first]
    (h1,), first = _rowwise(_f_rms, [xs], [_row(mix_norm_g)], [(D_MODEL, BF16)], [], name="rms_mix",
                            ex=_plan_all_gather([shard(n) for n in ag_first]))
    wg = dict(zip(ag_first, first))
    wi = _side_by_side(wg["w_in"])
    s3 = 2 * CONV_CH + MLA_Q_RANK + MLA_KV_RANK
    w_in_p = jnp.concatenate([
        wi[:, :s3], jnp.zeros((D_MODEL, MLA_NOPE), BF16), wi[:, s3:],
        jnp.zeros((D_MODEL, LANES - MLA_QK), BF16)], axis=1)
    w_uq_p = _side_by_side(_pad_last(wg["w_uq"], LANES))
    w_uk_p = _side_by_side(_pad_last(wg["w_ukv"][:, :, :MLA_NOPE], LANES))
    w_uv_p = _side_by_side(_pad_last(wg["w_ukv"][:, :, MLA_NOPE:], LANES))
    w_cdw = _side_by_side(wg["w_conv_dw"])

    g_mix, g_qlat, g_kvlat = _row(mix_norm_g), _row(q_lat_norm_g), _row(kv_lat_norm_g)
    b_in = _row(b_conv_in)
    b_in_a, b_in_g = b_in[:, :CONV_CH], b_in[:, CONV_CH:]
    b_cdw, ln_g, ln_b = _row(b_conv_dw), _row(conv_ln_g), _row(conv_ln_b)
    g_q, g_k = _row(q_norm_g, LANES), _row(k_norm_g, LANES)
    g_memx, g_memm = _row(mem_norm_x_g), _row(mem_norm_m_g)
    g_mq, g_mk, g_ffn = _row(mem_q_norm_g), _row(mem_k_norm_g), _row(ffn_norm_g)
    b_f = _pad_last(b_ffn_dw.reshape(N_DEV, FF_SHARD), FF_PAD)
    b_f_g, b_f_v = b_f[:4].reshape(1, D_FF_PAD), b_f[4:].reshape(1, D_FF_PAD)

    freq = ROPE_THETA ** (-jnp.arange(0, MLA_ROPE, 2, dtype=F32) / MLA_ROPE)
    inv_freq = jnp.concatenate([jnp.zeros((MLA_NOPE,), F32), freq, freq,
                                jnp.zeros((LANES - MLA_QK,), F32)]).reshape(1, LANES)
    cos, sin_a, sin_b = _rowwise(_f_rope_tab, [positions.reshape(seq, 1)], [inv_freq],
                                 [(LANES, F32)] * 3, [], name="rope_tables")

    z = _mm(h1, w_in_p, name="mm_in")
    z_rows = [(z, CONV_CH, 0), (z, CONV_CH, 1), (z, MLA_Q_RANK, 4), (z, MLA_KV_RANK, 10)]
    z_kr = (z, LANES, 11)
    u0, cqn, ckvn = _rowwise(
        _mix_pre, z_rows, [b_in_a, b_in_g, g_qlat, g_kvlat],
        [(CONV_CH, F32), (MLA_Q_RANK, BF16), (MLA_KV_RANK, BF16)], [], name="mix_pre")
    c1 = _conv_fwd(u0, w_cdw, "conv31_fwd")
    (u,) = _rowwise(lambda c, b, g, bb: (_ln_silu(c, b, g, bb),), [c1], [b_cdw, ln_g, ln_b],
                    [(CONV_CH, BF16)], [], name="ln_silu")
    q0 = _mm(cqn, w_uq_p, name="mm_uq")
    kn0 = _mm(ckvn, w_uk_p, name="mm_uk")
    v0 = _mm(ckvn, w_uv_p, out_dtype=BF16, name="mm_uv")
    qk_rows = [q0, kn0, z_kr, cos, sin_a, sin_b]
    qh, kh = _rowwise(_f_qk_prep, qk_rows, [g_q, g_k],
                      [(MLA_HEADS * LANES, BF16)] * 2, [], name="qk_prep")
    (attn, lse_row), later = _flash_fwd(
        qh, kh, v0, "flash_fwd", _plan_all_gather([shard(n) for n in ag_later]))
    wg.update(zip(ag_later, later))
    w_out = wg["w_out"].reshape(D_MODEL, D_MODEL)
    w_out_u = w_out[:CONV_CH]
    w_out_a = _pad_heads(w_out[CONV_CH:], MLA_V, 0)
    w_mq, w_mo = wg["w_mem_q"].reshape(D_MODEL, D_MODEL), wg["w_mem_o"].reshape(D_MODEL, D_MODEL)
    w_mkv = _side_by_side(wg["w_mem_kv"])
    w_up_p = _pad_last(wg["w_up"], FF_PAD)
    w_dn = jnp.pad(wg["w_down"].reshape(4, FF_SHARD, D_MODEL),
                   ((0, 0), (0, FF_PAD - FF_SHARD), (0, 0))).reshape(D_FF_PAD, D_MODEL)
    w_fdw = _pad_last(wg["w_ffn_dw"], FF_PAD)
    w_fdw_g = w_fdw[:4].transpose(1, 0, 2).reshape(FFN_CONV_WIDTH, D_FF_PAD)
    w_fdw_v = w_fdw[4:].transpose(1, 0, 2).reshape(FFN_CONV_WIDTH, D_FF_PAD)
    x1 = _mm(u, w_out_u, add=xs, name="mm_out_u")
    x1 = _mm(attn, w_out_a, add=x1, name="mm_out_a")

    (hq,) = _rowwise(_f_rms, [x1], [g_memx], [(D_MODEL, BF16)], [], name="rms_memx")
    (hm,) = _rowwise(_f_rms, [mems], [g_memm], [(D_MODEL, BF16)], [], name="rms_memm", tm=mems.shape[0])
    qm0 = _mm(hq, w_mq, name="mm_memq")
    kvm0 = _mm(hm, w_mkv, name="mm_memkv", tm=mems.shape[0])
    (km,) = _rowwise(_f_mem_k, [(kvm0, D_MODEL, 0)], [g_mk], [(D_MODEL, BF16)], [],
                     name="mem_k", tm=mems.shape[0])
    vm = kvm0[:, D_MODEL:]
    (om,) = _rowwise(_f_mem_attn, [qm0], [km, vm, g_mq], [(D_MODEL, BF16)], [], name="mem_attn")
    x2 = _mm(om, w_mo, add=x1, name="mm_memo")

    (h3,) = _rowwise(_f_rms, [x2], [g_ffn], [(D_MODEL, BF16)], [], name="rms_ffn")
    up_g, up_v, act, dy, sq = _ffn_fwd(h3, w_up_p, w_fdw_g, w_fdw_v, b_f_g, b_f_v, w_dn, x2, target, "ffn_fwd")
    loss_share = jnp.full((8, LANES), 0.5 * jnp.sum(sq) / D_MODEL, F32)

    gw, gs, gt = {}, {}, {}
    gw_dn = _mm_tn(act, dy, name="tn_down").reshape(4, FF_PAD, D_MODEL)
    gw["w_down"] = gw_dn[:, :FF_SHARD].reshape(N_DEV, FF_SHARD // 2, D_MODEL)
    dup_g, dup_v, dwf_g, dwf_v, db_g, db_v = _ffn_mid_bwd(
        up_g, up_v, dy, w_dn, w_fdw_g, w_fdw_v, b_f_g, b_f_v, "ffn_mid_bwd")
    db_f = jnp.concatenate([db_g.reshape(4, FF_PAD), db_v.reshape(4, FF_PAD)], axis=0)
    gs["b_ffn_dw"] = db_f[:, :FF_SHARD].reshape(1, 2 * D_FF)
    dwf = jnp.concatenate([dwf_g.reshape(FFN_CONV_WIDTH, 4, FF_PAD), dwf_v.reshape(FFN_CONV_WIDTH, 4, FF_PAD)], axis=1)
    gt["w_ffn_dw"] = dwf[:, :, :FF_SHARD].transpose(1, 0, 2)
    gw_up = jnp.concatenate([_mm_tn(h3, dup_g, shard_cols=FF_PAD, name="tn_up_g"),
                             _mm_tn(h3, dup_v, shard_cols=FF_PAD, name="tn_up_v")], axis=0)
    gw["w_up"] = gw_up[:, :, :FF_SHARD]
    w_up_flat = _side_by_side(w_up_p)
    dh3 = _mm(dup_g, w_up_flat, trans_b=True, b_kblock=0, name="mm_up_g_t")
    dx2, gs["ffn_norm_g"] = _mm(dup_v, w_up_flat, trans_b=True, b_kblock=1, add=dh3, tm=TM,
                                rms_bwd=(x2, dy, g_ffn), name="mm_up_v_t")

    gw["w_mem_o"] = _mm_tn(om, dx2, name="tn_memo").reshape(N_DEV, -1, D_MODEL)
    dom = _mm(dx2, w_mo, trans_b=True, out_dtype=BF16, name="mm_memo_t")
    n_mem = mems.shape[0]
    dqm0, dkm, dvm, gs["mem_q_norm_g"] = _rowwise(
        _b_mem_attn, [dom, qm0], [km, vm, g_mq], [(D_MODEL, BF16)],
        [(n_mem, D_MODEL), (n_mem, D_MODEL), (1, MEM_HEAD_DIM)], name="mem_attn_bwd")
    gw["w_mem_q"] = _mm_tn(hq, dqm0, name="tn_memq").reshape(N_DEV, -1, D_MODEL)
    dx1, gs["mem_norm_x_g"] = _mm(dqm0, w_mq, trans_b=True, rms_bwd=(x1, dx2, g_memx), name="mm_memq_t")
    dkk, gs["mem_k_norm_g"] = _rowwise(_b_mem_k, [(kvm0, D_MODEL, 0), dkm], [g_mk],
                                       [(D_MODEL, F32)], [(1, MEM_HEAD_DIM)], name="mem_k_bwd", tm=n_mem)
    dkvm0 = jnp.concatenate([dkk, dvm], axis=1)
    gw["w_mem_kv"] = _col_shards(_mm_tn(hm, dkvm0, name="tn_memkv", ts=n_mem))
    dhm = _mm(dkvm0, w_mkv, trans_b=True, name="mm_memkv_t", tm=n_mem)
    _, gs["mem_norm_m_g"] = _rowwise(_b_rms_nores, [mems, dhm], [g_memm], [(D_MODEL, F32)],
                                     [(1, D_MODEL)], name="rms_memm_bwd", tm=n_mem)

    gw_out_u = _mm_tn(u, dx1, name="tn_out_u")
    gw_out_a = _mm_tn(attn, dx1, name="tn_out_a")
    gw["w_out"] = jnp.concatenate([gw_out_u, _unpad_heads(gw_out_a, MLA_V, 0)], axis=0).reshape(N_DEV, -1, D_MODEL)
    du = _mm(dx1, w_out_u, trans_b=True, name="mm_out_u_t")
    dattn = _mm(dx1, w_out_a, trans_b=True, name="mm_out_a_t")
    dc1, gs["b_conv_dw"], gs["conv_ln_g"], gs["conv_ln_b"] = _rowwise(
        _b_ln_silu, [c1, du], [b_cdw, ln_g, ln_b], [(CONV_CH, F32)], [(1, CONV_CH)] * 3, name="ln_silu_bwd")
    rs_first = ["w_up", "w_down", "w_mem_o", "w_mem_q", "w_mem_kv", "w_out"]
    grads = [gw[n] for n in rs_first]
    (du0, g_cdw), gots = _conv_bwd(dc1, u0, w_cdw, "conv31_bwd", _plan_swap_sibling(grads))
    gt["w_conv_dw"] = _col_shards(g_cdw)
    sums_first = _rs_add(grads, gots, pos, "rs_add_first")
    delta_row = _attn_delta(dattn, attn, "attn_delta")
    (dqt, dkh, dv0), recvs_first = _flash_bwd(
        qh, kh, v0, dattn, lse_row, delta_row, "flash_bwd", _plan_swap_chips(sums_first))
    dqt_row = (dqt, pl.BlockSpec((MLA_HEADS, 1, LANES, TQ), lambda i: (0, i, 0, 0)))
    dq0, dkn0, dkr, dgq, dgk = _rowwise(
        _b_qk_prep, qk_rows + [dqt_row, dkh], [g_q, g_k],
        [(MLA_HEADS * LANES, BF16)] * 2 + [(LANES, F32)], [(1, LANES)] * 2, name="qk_prep_bwd", tm=TQ)
    gs["q_norm_g"], gs["k_norm_g"] = dgq[:, :MLA_QK], dgk[:, :MLA_QK]
    gw["w_uq"] = _col_shards(_mm_tn(cqn, dq0, name="tn_uq"))[:, :, :MLA_QK]
    g_uk = _col_shards(_mm_tn(ckvn, dkn0, name="tn_uk"))[:, :, :MLA_NOPE]
    g_uv = _col_shards(_mm_tn(ckvn, dv0, name="tn_uv"))[:, :, :MLA_V]
    gw["w_ukv"] = jnp.concatenate([g_uk, g_uv], axis=2)
    dcqn = _mm(dq0, w_uq_p, trans_b=True, name="mm_uq_t")
    dckvn = _mm(dkn0, w_uk_p, trans_b=True, name="mm_uk_t")
    dckvn = _mm(dv0, w_uv_p, trans_b=True, add=dckvn, name="mm_uv_t")
    dz, dba, dbg, gs["q_lat_norm_g"], gs["kv_lat_norm_g"] = _rowwise(
        _b_mix_pre, z_rows + [du0, dcqn, dckvn, dkr], [b_in_a, b_in_g, g_qlat, g_kvlat],
        [(IN_PAD, BF16)], [(1, CONV_CH)] * 2 + [(1, MLA_Q_RANK), (1, MLA_KV_RANK)], name="mix_pre_bwd")
    gs["b_conv_in"] = jnp.concatenate([dba, dbg], axis=1)
    gw_in = _mm_tn(h1, dz, name="tn_in")
    gw["w_in"] = _col_shards(jnp.concatenate([gw_in[:, :s3], gw_in[:, s3 + MLA_NOPE:s3 + MLA_QK]], axis=1))
    rs_last = [n for n, _ in BIG if n not in rs_first]
    grads = [gw[n] for n in rs_last]
    gots = _run_exchange(_plan_swap_sibling(grads), "rs_sibling_last")
    sums_last = _rs_add(grads, gots, pos, "rs_add_last")
    (dx, gs["mix_norm_g"]), recvs_last = _mm(dz, w_in_p, trans_b=True, rms_bwd=(xs, dx1, g_mix),
                                             name="mm_in_t", ex=_plan_swap_chips(sums_last))
    big = rs_first + rs_last
    flat = _adamw_big(list(sums_first) + list(sums_last), list(recvs_first) + list(recvs_last),
                      [a[n] for n in big], [a["m_" + n] for n in big], [a["v_" + n] for n in big],
                      pos, "adamw_big")
    res = [{n: flat[4 * i + k] for i, n in enumerate(big)} for k in range(4)]

    part = jnp.concatenate(
        [_pack_small(gs)] + [_tiny_rows(gt[n], rows).reshape(N_DEV * rows, LANES) for n, _, rows in TINY]
        + [loss_share], axis=0)
    (parts,) = _all_gather([part], "ag_small_grads")
    small_in = [_pack_small({n: a[p + n] for n, _ in SMALL}) for p in ("", "m_", "v_")]
    tiny_in = [[_tiny_rows(a[p + n][0], rows) for p in ("", "m_", "v_")] for n, _, rows in TINY]
    flat = _adamw_small(parts, small_in, tiny_in, "adamw_small")
    for k in range(4):
        res[k].update(_unpack_small(flat[k]))
        for i, (n, shape, _) in enumerate(TINY):
            res[k][n] = flat[4 * (i + 1) + k].reshape(-1)[:math.prod(shape)].reshape((1,) + shape)

    return (flat[-1][0, 0], dx.reshape(1, seq, D_MODEL), *[res[k][n] for k in range(4) for n in WEIGHTS])
```
